```python
import math
import jax
import jax.numpy as jnp
from jax import lax
import numpy as np

D_MODEL = 1024
BATCH = 32
SEQ = 256
DEPTH = 2
DEC_BATCH = 2
DEC_SEQ = 4096
PAST_LEN = 256

GRID_W = 64
N_EVEN = (DEPTH + 1) // 2
N_ODD = DEPTH // 2
MIX_WIDTH = D_MODEL
HALF_MIX = MIX_WIDTH // 2
S5_CH = HALF_MIX
S5_GROUP_CH = 16
S5_GROUPS = S5_CH // S5_GROUP_CH
S5_STATE = 64
NA_HEAD_DIM = 64
NA_HEADS = HALF_MIX // NA_HEAD_DIM
NA_WIN_R = 8
NA_WIN_C = 16
DIFF_D = 64
DIFF_HEAD_DIM = 2 * DIFF_D
DIFF_HEADS = HALF_MIX // DIFF_HEAD_DIM
DIFF_W = DIFF_HEADS * DIFF_HEAD_DIM
ROPE_BASE = 10000.0
Q_BLOCK = 128
CONV_CH = HALF_MIX
CONV_WIDTH = 31
FFN_DIM = 2816
N_EXPERTS = 8
TOP_K = 2
EXPERT_DIM = 3584
IN_E = S5_CH + 3 * NA_HEADS * NA_HEAD_DIM
IN_O = 3 * DIFF_W + 2 * CONV_CH
EPS = 1e-6

kernel_name = 'hybrid_s5_natten_diffattn_conformer_dit_step'


def rms_norm(x, g):
    xf = x.astype(jnp.float32)
    y = xf * lax.rsqrt(jnp.mean(xf * xf, axis=-1, keepdims=True) + EPS)
    return (y * g.astype(jnp.float32)).astype(x.dtype)


def layer_norm(x, g, b):
    xf = x.astype(jnp.float32)
    mu = jnp.mean(xf, axis=-1, keepdims=True)
    var = jnp.mean(jnp.square(xf - mu), axis=-1, keepdims=True)
    y = (xf - mu) * lax.rsqrt(var + EPS) * g.astype(jnp.float32) + b.astype(jnp.float32)
    return y.astype(x.dtype)


def adaln(cond, w_mod, b_mod):
    mod = jnp.einsum('bd,de->be', jax.nn.silu(cond), w_mod) + b_mod
    return jnp.split(mod[:, None, :], 6, axis=-1)


def modulate(h, shift, scale):
    return h * (1 + scale) + shift


def swiglu(h, w_gate, w_up, w_down):
    a = jnp.einsum('bld,df->blf', h, w_gate)
    u = jnp.einsum('bld,df->blf', h, w_up)
    return jnp.einsum('blf,fd->bld', jax.nn.silu(a) * u, w_down)


def moe_swiglu(h, router_w, w_gate, w_up, w_down):
    logits = jnp.einsum('bld,de->ble', h, router_w).astype(jnp.float32)
    top_val, top_idx = lax.top_k(logits, TOP_K)
    gates = jax.nn.softmax(top_val, axis=-1)
    combine = jnp.sum(gates[..., None] * jax.nn.one_hot(top_idx, N_EXPERTS, dtype=jnp.float32), axis=-2).astype(h.dtype)
    out = jnp.zeros_like(h)
    for e in range(N_EXPERTS):
        out = out + combine[..., e:e + 1] * swiglu(h, w_gate[e], w_up[e], w_down[e])
    return out


def _ssm_combine(e1, e2):
    a1, b1 = e1
    a2, b2 = e2
    return a1 * a2, a2 * b1 + b2


def s5_mixer(u, lam_re, lam_im, log_dt, b_re, b_im, c_re, c_im, d_skip, w_glu, b_glu, h0):
    bsz, seq, _ = u.shape
    uf = u.astype(jnp.float32)
    uc = uf.reshape(bsz, seq, S5_GROUPS, S5_GROUP_CH).astype(jnp.complex64)
    y = uf * d_skip.astype(jnp.float32)
    finals = []
    for dirn in range(2):
        reverse = dirn == 1
        lam = lax.complex(lam_re[dirn].astype(jnp.float32), lam_im[dirn].astype(jnp.float32))
        dt = jnp.exp(log_dt[dirn].astype(jnp.float32))[:, None]
        lam_bar = jnp.exp(lam * dt)
        b_c = lax.complex(b_re[dirn].astype(jnp.float32), b_im[dirn].astype(jnp.float32))
        b_bar = ((lam_bar - 1.0) / lam)[..., None] * b_c
        bu = jnp.einsum('gpc,blgc->blgp', b_bar, uc)
        if h0 is not None:
            edge = seq - 1 if reverse else 0
            h_init = lax.complex(h0[:, dirn, 0].astype(jnp.float32), h0[:, dirn, 1].astype(jnp.float32))
            bu = bu.at[:, edge].add(lam_bar * h_init)
        a = jnp.broadcast_to(lam_bar, bu.shape)
        _, h = lax.associative_scan(_ssm_combine, (a, bu), reverse=reverse, axis=1)
        c_c = lax.complex(c_re[dirn].astype(jnp.float32), c_im[dirn].astype(jnp.float32))
        y = y + jnp.einsum('gcp,blgp->blgc', c_c, h).real.reshape(bsz, seq, S5_CH)
        h_last = h[:, 0] if reverse else h[:, seq - 1]
        finals.append(jnp.stack([h_last.real, h_last.imag], axis=1))
    z = jax.nn.gelu(y)
    out = z * jax.nn.sigmoid(jnp.einsum('blc,ce->ble', z, w_glu.astype(jnp.float32)) + b_glu.astype(jnp.float32))
    return out.astype(u.dtype), jnp.stack(finals, axis=1)


def attend_blocks(q, fn):
    bsz, heads, seq, dq = q.shape
    nblk = seq // Q_BLOCK
    qb = q.reshape(bsz, heads, nblk, Q_BLOCK, dq).transpose(2, 0, 1, 3, 4)
    o = lax.map(fn, qb)
    return o.transpose(1, 2, 0, 3, 4).reshape(bsz, heads, seq, -1)


def softmax_attend(qi, k, v):
    s = jnp.einsum('bhqd,bhkd->bhqk', qi, k).astype(jnp.float32) * (qi.shape[-1] ** -0.5)
    p = jax.nn.softmax(s, axis=-1).astype(v.dtype)
    return jnp.einsum('bhqk,bhkd->bhqd', p, v)


def diff_attend(qi, k, v, lam):
    scale = DIFF_D ** -0.5
    s1 = jnp.einsum('bhqd,bhkd->bhqk', qi[..., :DIFF_D], k[..., :DIFF_D]).astype(jnp.float32) * scale
    s2 = jnp.einsum('bhqd,bhkd->bhqk', qi[..., DIFF_D:], k[..., DIFF_D:]).astype(jnp.float32) * scale
    a = jax.nn.softmax(s1, axis=-1) - lam * jax.nn.softmax(s2, axis=-1)
    return jnp.einsum('bhqk,bhkd->bhqd', a.astype(v.dtype), v)


def diff_post(o, subln_g, lam_init):
    bsz, heads, seq, hd = o.shape
    o = rms_norm(o, subln_g) * (1.0 - lam_init)
    return o.transpose(0, 2, 1, 3).reshape(bsz, seq, heads * hd)


def axial_rope(x):
    seq = x.shape[2]
    t = jnp.arange(seq)
    row = (t // GRID_W).astype(jnp.float32)
    col = (t % GRID_W).astype(jnp.float32)
    n_freq = DIFF_D // 4
    inv = ROPE_BASE ** (-jnp.arange(n_freq, dtype=jnp.float32) / n_freq)
    ang = jnp.concatenate([row[:, None] * inv, col[:, None] * inv], axis=-1)
    cos = jnp.cos(ang)[:, None, :]
    sin = jnp.sin(ang)[:, None, :]
    xf = x.astype(jnp.float32)
    x1 = xf[..., 0::2]
    x2 = xf[..., 1::2]
    out = jnp.stack([x1 * cos - x2 * sin, x1 * sin + x2 * cos], axis=-1).reshape(x.shape)
    return out.astype(x.dtype)


def conformer_conv(a, g, w, b, ln_g, ln_b):
    x = a * jax.nn.sigmoid(g)
    y = lax.conv_general_dilated(x, w[:, None, :].astype(x.dtype), window_strides=(1,),
                                 padding=[(CONV_WIDTH // 2, CONV_WIDTH // 2)],
                                 dimension_numbers=('NWC', 'WIO', 'NWC'),
                                 feature_group_count=x.shape[-1]) + b
    return jax.nn.silu(layer_norm(y, ln_g, ln_b))


def na_latent(q, k, v, k_ctx, v_ctx, rpb):
    bsz, seq, heads, hd = q.shape
    rows = seq // GRID_W
    kr = min(NA_WIN_R, rows)
    kc = NA_WIN_C
    scale = hd ** -0.5
    qg = q.reshape(bsz, rows, GRID_W, heads, hd)
    kg = k.reshape(bsz, rows, GRID_W, heads, hd)
    vg = v.reshape(bsz, rows, GRID_W, heads, hd)
    row_start = jnp.clip(jnp.arange(rows) - kr // 2, 0, rows - kr)
    qcol = jnp.arange(GRID_W)
    col_idx = jnp.clip(qcol - kc // 2, 0, GRID_W - kc)[:, None] + jnp.arange(kc)[None, :]
    col_off = col_idx - qcol[:, None] + (NA_WIN_C - 1)

    def row_block(r):
        rs = row_start[r]
        kw = lax.dynamic_slice_in_dim(kg, rs, kr, axis=1)[:, :, col_idx]
        vw = lax.dynamic_slice_in_dim(vg, rs, kr, axis=1)[:, :, col_idx]
        q_r = lax.dynamic_index_in_dim(qg, r, axis=1, keepdims=False)
        row_off = rs + jnp.arange(kr) - r + (NA_WIN_R - 1)
        bias = rpb[:, row_off[:, None, None], col_off[None, :, :]].transpose(0, 2, 1, 3)
        s_loc = jnp.einsum('bqhd,brqchd->bhqrc', q_r, kw).astype(jnp.float32) * scale + bias.astype(jnp.float32)[None]
        s_ctx = jnp.einsum('bqhd,bhkd->bhqk', q_r, k_ctx).astype(jnp.float32) * scale
        logits = jnp.concatenate([s_loc.reshape(bsz, heads, GRID_W, kr * kc), s_ctx], axis=-1)
        p = jax.nn.softmax(logits, axis=-1).astype(v.dtype)
        p_loc = p[..., :kr * kc].reshape(bsz, heads, GRID_W, kr, kc)
        return (jnp.einsum('bhqrc,brqchd->bqhd', p_loc, vw)
                + jnp.einsum('bhqk,bhkd->bqhd', p[..., kr * kc:], v_ctx))

    o = lax.map(row_block, jnp.arange(rows))
    return o.transpose(1, 0, 2, 3, 4).reshape(bsz, seq, heads * hd)


def split_even(h, w_in):
    bsz, seq, _ = h.shape
    proj = jnp.einsum('bld,de->ble', h, w_in)
    u = proj[..., :S5_CH]
    qkv = proj[..., S5_CH:].reshape(bsz, seq, 3, NA_HEADS, NA_HEAD_DIM)
    return u, qkv[:, :, 0], qkv[:, :, 1], qkv[:, :, 2]


def even_mixer_ctx(h, w_in, w_out, s5p):
    bsz, seq, _ = h.shape
    u, q, k, v = split_even(h, w_in)
    q, k, v = q.transpose(0, 2, 1, 3), k.transpose(0, 2, 1, 3), v.transpose(0, 2, 1, 3)
    s5_out, s5_state = s5_mixer(u, *s5p, None)
    na_out = attend_blocks(q, lambda qi: softmax_attend(qi, k, v))
    na_out = na_out.transpose(0, 2, 1, 3).reshape(bsz, seq, NA_HEADS * NA_HEAD_DIM)
    out = jnp.einsum('ble,ed->bld', jnp.concatenate([s5_out, na_out], axis=-1), w_out)
    return out, s5_state, k, v


def even_mixer_lat(h, s5_state, k_ctx, v_ctx, w_in, w_out, s5p, rpb):
    u, q, k, v = split_even(h, w_in)
    s5_out, _ = s5_mixer(u, *s5p, s5_state)
    na_out = na_latent(q, k, v, k_ctx, v_ctx, rpb)
    return jnp.einsum('ble,ed->bld', jnp.concatenate([s5_out, na_out], axis=-1), w_out)


def split_odd(h, w_in):
    bsz, seq, _ = h.shape
    proj = jnp.einsum('bld,de->ble', h, w_in)
    qkv = proj[..., :3 * DIFF_W].reshape(bsz, seq, 3, DIFF_HEADS, DIFF_HEAD_DIM).transpose(2, 0, 3, 1, 4)
    a = proj[..., 3 * DIFF_W:3 * DIFF_W + CONV_CH]
    g = proj[..., 3 * DIFF_W + CONV_CH:]
    return qkv[0], qkv[1], qkv[2], a, g


def odd_mixer_ctx(h, w_in, w_out, lam, lam_init, subln_g, convp):
    q, k, v, a, g = split_odd(h, w_in)
    diff_out = diff_post(attend_blocks(q, lambda qi: diff_attend(qi, k, v, lam)), subln_g, lam_init)
    conv_out = conformer_conv(a, g, *convp)
    out = jnp.einsum('ble,ed->bld', jnp.concatenate([diff_out, conv_out], axis=-1), w_out)
    return out, k, v


def odd_mixer_lat(h, k_ctx, v_ctx, w_in, w_out, lam, lam_init, subln_g, convp):
    q, k, v, a, g = split_odd(h, w_in)
    bsz, heads, seq, hd = q.shape
    q = axial_rope(q.reshape(bsz, heads, seq, 2, DIFF_D)).reshape(bsz, heads, seq, hd)
    k = axial_rope(k.reshape(bsz, heads, seq, 2, DIFF_D)).reshape(bsz, heads, seq, hd)
    k_all = jnp.concatenate([k, k_ctx], axis=2)
    v_all = jnp.concatenate([v, v_ctx], axis=2)
    diff_out = diff_post(attend_blocks(q, lambda qi: diff_attend(qi, k_all, v_all, lam)), subln_g, lam_init)
    conv_out = conformer_conv(a, g, *convp)
    return jnp.einsum('ble,ed->bld', jnp.concatenate([diff_out, conv_out], axis=-1), w_out)


def setup_inputs(seed: int = 0) -> dict:
    key = jax.random.key(seed)
    keys = jax.random.split(key, 64)
    counter = [0]

    def nxt():
        k = keys[counter[0]]
        counter[0] += 1
        return k

    def nrm(shape, scale=1.0):
        return jax.random.normal(nxt(), shape, jnp.float32) * scale

    def gain(shape):
        return 1.0 + nrm(shape, 0.01)

    d = D_MODEL
    inp = {}
    inp['x_prompt'] = nrm((BATCH, SEQ, d))
    inp['x_sample'] = nrm((DEC_BATCH, DEC_SEQ, d))
    inp['state_s5'] = nrm((DEC_BATCH, N_EVEN, 2, 2, S5_GROUPS, S5_STATE), 0.1)
    inp['cache_na_k'] = nrm((DEC_BATCH, N_EVEN, NA_HEADS, PAST_LEN, NA_HEAD_DIM))
    inp['cache_na_v'] = nrm((DEC_BATCH, N_EVEN, NA_HEADS, PAST_LEN, NA_HEAD_DIM))
    inp['cache_diff_k'] = nrm((DEC_BATCH, N_ODD, DIFF_HEADS, PAST_LEN, DIFF_HEAD_DIM))
    inp['cache_diff_v'] = nrm((DEC_BATCH, N_ODD, DIFF_HEADS, PAST_LEN, DIFF_HEAD_DIM))
    inp['c'] = nrm((DEC_BATCH, d))
    inp['c_ctx'] = nrm((d,))
    inp['w_mod'] = nrm((DEPTH, d, 6 * d), 0.5 * d ** -0.5)
    inp['b_mod'] = nrm((DEPTH, 6 * d), 0.01)
    inp['norm_mix_g'] = gain((DEPTH, d))
    inp['norm_ffn_g'] = gain((DEPTH, d))
    inp['final_norm_g'] = gain((d,))
    inp['w_in_e'] = nrm((N_EVEN, d, IN_E), d ** -0.5)
    inp['w_out_e'] = nrm((N_EVEN, MIX_WIDTH, d), MIX_WIDTH ** -0.5)
    inp['s5_lam_re'] = -0.5 + nrm((N_EVEN, 2, S5_GROUPS, S5_STATE), 0.01)
    inp['s5_lam_im'] = math.pi * jnp.arange(S5_STATE, dtype=jnp.float32) + nrm((N_EVEN, 2, S5_GROUPS, S5_STATE), 0.01)
    inp['s5_log_dt'] = jax.random.uniform(nxt(), (N_EVEN, 2, S5_GROUPS), jnp.float32, math.log(0.001), math.log(0.1))
    inp['s5_b_re'] = nrm((N_EVEN, 2, S5_GROUPS, S5_STATE, S5_GROUP_CH), (2 * S5_GROUP_CH) ** -0.5)
    inp['s5_b_im'] = nrm((N_EVEN, 2, S5_GROUPS, S5_STATE, S5_GROUP_CH), (2 * S5_GROUP_CH) ** -0.5)
    inp['s5_c_re'] = nrm((N_EVEN, 2, S5_GROUPS, S5_GROUP_CH, S5_STATE), S5_STATE ** -0.5)
    inp['s5_c_im'] = nrm((N_EVEN, 2, S5_GROUPS, S5_GROUP_CH, S5_STATE), S5_STATE ** -0.5)
    inp['s5_d'] = nrm((N_EVEN, S5_CH))
    inp['s5_w_glu'] = nrm((N_EVEN, S5_CH, S5_CH), S5_CH ** -0.5)
    inp['s5_b_glu'] = nrm((N_EVEN, S5_CH), 0.01)
    inp['na_rpb'] = nrm((N_EVEN, NA_HEADS, 2 * NA_WIN_R - 1, 2 * NA_WIN_C - 1), 0.1)
    inp['ffn_w_gate'] = nrm((N_EVEN, d, FFN_DIM), d ** -0.5)
    inp['ffn_w_up'] = nrm((N_EVEN, d, FFN_DIM), d ** -0.5)
    inp['ffn_w_down'] = nrm((N_EVEN, FFN_DIM, d), FFN_DIM ** -0.5)
    inp['w_in_o'] = nrm((N_ODD, d, IN_O), d ** -0.5)
    inp['w_out_o'] = nrm((N_ODD, MIX_WIDTH, d), MIX_WIDTH ** -0.5)
    inp['diff_lam_q1'] = nrm((N_ODD, DIFF_D), 0.1)
    inp['diff_lam_k1'] = nrm((N_ODD, DIFF_D), 0.1)
    inp['diff_lam_q2'] = nrm((N_ODD, DIFF_D), 0.1)
    inp['diff_lam_k2'] = nrm((N_ODD, DIFF_D), 0.1)
    inp['diff_subln_g'] = gain((N_ODD, DIFF_HEAD_DIM))
    inp['conv_w'] = nrm((N_ODD, CONV_WIDTH, CONV_CH), CONV_WIDTH ** -0.5)
    inp['conv_b'] = nrm((N_ODD, CONV_CH), 0.01)
    inp['conv_ln_g'] = gain((N_ODD, CONV_CH))
    inp['conv_ln_b'] = nrm((N_ODD, CONV_CH), 0.01)
    inp['router_w'] = nrm((N_ODD, d, N_EXPERTS), d ** -0.5)
    inp['moe_w_gate'] = nrm((N_ODD, N_EXPERTS, d, EXPERT_DIM), d ** -0.5)
    inp['moe_w_up'] = nrm((N_ODD, N_EXPERTS, d, EXPERT_DIM), d ** -0.5)
    inp['moe_w_down'] = nrm((N_ODD, N_EXPERTS, EXPERT_DIM, d), EXPERT_DIM ** -0.5)
    return inp


def reference(x_prompt, x_sample, state_s5, cache_na_k, cache_na_v, cache_diff_k, cache_diff_v, c,
              c_ctx, w_mod, b_mod, norm_mix_g, norm_ffn_g, final_norm_g, w_in_e, w_out_e,
              s5_lam_re, s5_lam_im, s5_log_dt, s5_b_re, s5_b_im, s5_c_re, s5_c_im, s5_d, s5_w_glu, s5_b_glu,
              na_rpb, ffn_w_gate, ffn_w_up, ffn_w_down, w_in_o, w_out_o,
              diff_lam_q1, diff_lam_k1, diff_lam_q2, diff_lam_k2, diff_subln_g,
              conv_w, conv_b, conv_ln_g, conv_ln_b, router_w, moe_w_gate, moe_w_up, moe_w_down):
    xp = x_prompt
    xs = x_sample
    cond_p = jnp.broadcast_to(c_ctx, (xp.shape[0], c_ctx.shape[0]))
    s5_states, na_ks, na_vs, diff_ks, diff_vs = [], [], [], [], []
    for layer in range(DEPTH):
        i = layer // 2
        sh1p, sc1p, g1p, sh2p, sc2p, g2p = adaln(cond_p, w_mod[layer], b_mod[layer])
        sh1s, sc1s, g1s, sh2s, sc2s, g2s = adaln(c, w_mod[layer], b_mod[layer])
        hp = modulate(rms_norm(xp, norm_mix_g[layer]), sh1p, sc1p)
        hs = modulate(rms_norm(xs, norm_mix_g[layer]), sh1s, sc1s)
        if layer % 2 == 0:
            s5p = (s5_lam_re[i], s5_lam_im[i], s5_log_dt[i], s5_b_re[i], s5_b_im[i],
                   s5_c_re[i], s5_c_im[i], s5_d[i], s5_w_glu[i], s5_b_glu[i])
            mix_p, st_p, k_p, v_p = even_mixer_ctx(hp, w_in_e[i], w_out_e[i], s5p)
            mix_s = even_mixer_lat(hs, state_s5[:, i], cache_na_k[:, i], cache_na_v[:, i],
                                   w_in_e[i], w_out_e[i], s5p, na_rpb[i])
            s5_states.append(st_p)
            na_ks.append(k_p)
            na_vs.append(v_p)
        else:
            lam_init = 0.8 - 0.6 * math.exp(-0.3 * layer)
            lam = (jnp.exp(jnp.sum(diff_lam_q1[i].astype(jnp.float32) * diff_lam_k1[i].astype(jnp.float32)))
                   - jnp.exp(jnp.sum(diff_lam_q2[i].astype(jnp.float32) * diff_lam_k2[i].astype(jnp.float32)))
                   + lam_init)
            convp = (conv_w[i], conv_b[i], conv_ln_g[i], conv_ln_b[i])
            mix_p, k_p, v_p = odd_mixer_ctx(hp, w_in_o[i], w_out_o[i], lam, lam_init, diff_subln_g[i], convp)
            mix_s = odd_mixer_lat(hs, cache_diff_k[:, i], cache_diff_v[:, i], w_in_o[i], w_out_o[i],
                                  lam, lam_init, diff_subln_g[i], convp)
            diff_ks.append(k_p)
            diff_vs.append(v_p)
        xp = xp + g1p * mix_p
        xs = xs + g1s * mix_s
        hp = modulate(rms_norm(xp, norm_ffn_g[layer]), sh2p, sc2p)
        hs = modulate(rms_norm(xs, norm_ffn_g[layer]), sh2s, sc2s)
        if layer % 2 == 0:
            f_p = swiglu(hp, ffn_w_gate[i], ffn_w_up[i], ffn_w_down[i])
            f_s = swiglu(hs, ffn_w_gate[i], ffn_w_up[i], ffn_w_down[i])
        else:
            f_p = moe_swiglu(hp, router_w[i], moe_w_gate[i], moe_w_up[i], moe_w_down[i])
            f_s = moe_swiglu(hs, router_w[i], moe_w_gate[i], moe_w_up[i], moe_w_down[i])
        xp = xp + g2p * f_p
        xs = xs + g2s * f_s
    y_prompt = rms_norm(xp, final_norm_g)
    y_sample = rms_norm(xs, final_norm_g)
    new_state_s5 = jnp.stack(s5_states, axis=1)
    new_cache_na_k = jnp.stack(na_ks, axis=1)
    new_cache_na_v = jnp.stack(na_vs, axis=1)
    new_cache_diff_k = jnp.stack(diff_ks, axis=1)
    new_cache_diff_v = jnp.stack(diff_vs, axis=1)
    return (y_prompt, y_sample, new_state_s5, new_cache_na_k, new_cache_na_v, new_cache_diff_k, new_cache_diff_v)
```

```python
import functools
import math

import jax
import jax.numpy as jnp
from jax import lax
from jax.experimental import pallas as pl
from jax.experimental.pallas import tpu as pltpu

D_MODEL = 1024
DEPTH = 2
GRID_W = 64
HALF_MIX = 512
S5_GROUP_CH = 16
S5_GROUPS = 32
S5_STATE = 64
NA_HEAD_DIM = 64
NA_HEADS = 8
NA_WIN_R = 8
NA_WIN_C = 16
DIFF_D = 64
DIFF_HEAD_DIM = 128
DIFF_HEADS = 4
ROPE_BASE = 10000.0
CONV_WIDTH = 31
N_EXPERTS = 8
EPS = 1e-6

F32 = jnp.float32
BF16 = jnp.bfloat16
NEG_BIG = -1e30

VMEM_LIMIT_BYTES = 56 * 1024 * 1024
LANES = 128
SUBLANES = 8

S5_COL_GROUPS = 8
S5_COL_CH = S5_COL_GROUPS * S5_GROUP_CH
S5_COL_STATE = S5_COL_GROUPS * S5_STATE
S5_N_COL = S5_GROUPS // S5_COL_GROUPS
S5_TIME_BLOCK = 256


def _params(*sem):
    return pltpu.CompilerParams(dimension_semantics=sem, vmem_limit_bytes=VMEM_LIMIT_BYTES)


def _dot(a, b):
    return jnp.dot(a, b, preferred_element_type=F32)


def _dot_nt(a, b):
    return lax.dot_general(a, b, (((1,), (1,)), ((), ())), preferred_element_type=F32)


def _silu(x):
    return x * jax.nn.sigmoid(x)


def _norm_mod(x, g, shift, scale):
    y = x * lax.rsqrt(jnp.mean(x * x, axis=-1, keepdims=True) + EPS) * g
    return y * (1.0 + scale) + shift


def _mod_kernel(cond_ref, w_ref, b_ref, o_ref):
    s = _silu(cond_ref[...])
    o_ref[0] = jnp.dot(s, w_ref[0], precision=lax.Precision.HIGHEST,
                       preferred_element_type=F32) + b_ref[0]


def adaln_all(cond8, w_mod, b_mod):
    tn = 1536
    n = w_mod.shape[-1]
    return pl.pallas_call(
        _mod_kernel,
        grid=(DEPTH, n // tn),
        in_specs=[pl.BlockSpec((SUBLANES, D_MODEL), lambda l, j: (0, 0)),
                  pl.BlockSpec((1, D_MODEL, tn), lambda l, j: (l, 0, j)),
                  pl.BlockSpec((1, 1, tn), lambda l, j: (l, 0, j))],
        out_specs=pl.BlockSpec((1, SUBLANES, tn), lambda l, j: (l, 0, j)),
        out_shape=jax.ShapeDtypeStruct((DEPTH, SUBLANES, n), F32),
        compiler_params=_params("arbitrary", "arbitrary"),
        name="adaln_mod",
    )(cond8, w_mod, b_mod.reshape(DEPTH, 1, n))


def _mod_spec(chunk, tiles_per_row):
    return pl.BlockSpec((1, 1, D_MODEL), lambda i, *_: (i // tiles_per_row, 0, chunk))


def _in_proj_kernel(x_ref, g_ref, sh_ref, sc_ref, w_ref, o_ref, h_scr):
    @pl.when(pl.program_id(1) == 0)
    def _():
        h_scr[...] = _norm_mod(x_ref[...], g_ref[...], sh_ref[0], sc_ref[0]).astype(BF16)

    o_ref[...] = _dot(h_scr[...], w_ref[...].astype(BF16)).astype(o_ref.dtype)


def in_proj(x, g, mod, tiles_per_row, w, tm=1024, tn=512):
    t = x.shape[0]
    n = w.shape[1]
    return pl.pallas_call(
        _in_proj_kernel,
        grid=(t // tm, n // tn),
        in_specs=[pl.BlockSpec((tm, D_MODEL), lambda i, j: (i, 0)),
                  pl.BlockSpec((1, D_MODEL), lambda i, j: (0, 0)),
                  _mod_spec(0, tiles_per_row),
                  _mod_spec(1, tiles_per_row),
                  pl.BlockSpec((D_MODEL, tn), lambda i, j: (0, j))],
        out_specs=pl.BlockSpec((tm, tn), lambda i, j: (i, j)),
        out_shape=jax.ShapeDtypeStruct((t, n), F32),
        scratch_shapes=[pltpu.VMEM((tm, D_MODEL), BF16)],
        compiler_params=_params("arbitrary", "arbitrary"),
        name="in_proj",
    )(x, g.reshape(1, D_MODEL), mod, mod, w)


def _s5_scan_kernel(*refs, seq, chunks, has_init):
    if has_init:
        u_ref, bm_ref, cm_ref, lam_ref, h0_ref, y_ref, st_ref, bu_scr, ytm_scr = refs
    else:
        u_ref, bm_ref, cm_ref, lam_ref, y_ref, st_ref, bu_scr, ytm_scr = refs
        h0_ref = None
    tb = S5_TIME_BLOCK
    n_tb = seq // tb
    ns = S5_COL_STATE
    row = lax.broadcasted_iota(jnp.int32, (SUBLANES, ns), 0)
    piece = row % chunks

    for d in range(2):
        lam = lam_ref[d, 0]
        lr, li = lam[:, :ns], lam[:, ns:]
        bm = bm_ref[d, 0]
        cm = cm_ref[d, 0]
        blocks = list(range(n_tb)) if d == 0 else list(range(n_tb - 1, -1, -1))

        def load_bu(k):
            ub = u_ref[:, k * tb:(k + 1) * tb, :]
            utm = jnp.swapaxes(ub, 0, 1).reshape(tb * SUBLANES, S5_COL_CH).astype(BF16)
            bu_scr[...] = _dot(utm, bm).reshape(tb, SUBLANES, 2 * ns)

        def scan_block(h, store):
            def step(s, carry):
                hr, hi = carry
                t = (tb - 1 - s) if d == 1 else s
                b = bu_scr[t]
                nr = lr * hr - li * hi + b[:, :ns]
                ni = lr * hi + li * hr + b[:, ns:]
                if store:
                    bu_scr[t, :, :ns] = nr
                    bu_scr[t, :, ns:] = ni
                return nr, ni
            return lax.fori_loop(0, tb, step, h, unroll=2)

        zero = jnp.zeros((SUBLANES, ns), F32)
        if chunks > 1:
            h = (zero, zero)
            for k in blocks:
                load_bu(k)
                h = scan_block(h, False)
            fr, fi = h
            pr, pi = lr, li
            for _ in range(int(math.log2(seq))):
                pr, pi = pr * pr - pi * pi, 2.0 * pr * pi
            edge = 0 if d == 0 else chunks - 1
            shift = 1 if d == 0 else SUBLANES - 1
            if has_init:
                h0r = h0_ref[:, d, 0, :]
                h0i = h0_ref[:, d, 1, :]
                seq_of_row = row // chunks
                er, ei = zero, zero
                for b in range(SUBLANES // chunks):
                    er = jnp.where(seq_of_row == b, h0r[b:b + 1, :], er)
                    ei = jnp.where(seq_of_row == b, h0i[b:b + 1, :], ei)
            else:
                er, ei = zero, zero
            is_edge = piece == edge
            cr = jnp.where(is_edge, er, zero)
            ci = jnp.where(is_edge, ei, zero)
            for _ in range(chunks - 1):
                tr = fr + pr * cr - pi * ci
                ti = fi + pr * ci + pi * cr
                cr = jnp.where(is_edge, er, pltpu.roll(tr, shift, 0))
                ci = jnp.where(is_edge, ei, pltpu.roll(ti, shift, 0))
            h = (cr, ci)
        else:
            if has_init:
                h = (h0_ref[:, d, 0, :], h0_ref[:, d, 1, :])
            else:
                h = (zero, zero)

        for k in blocks:
            load_bu(k)
            h = scan_block(h, True)
            hb = bu_scr[...].reshape(tb * SUBLANES, 2 * ns).astype(BF16)
            yb = _dot(hb, cm).reshape(tb, SUBLANES, S5_COL_CH)
            if d == 0:
                ytm_scr[k * tb:(k + 1) * tb] = yb
            else:
                ytm_scr[k * tb:(k + 1) * tb] += yb
        st_ref[:, d, 0, :] = h[0]
        st_ref[:, d, 1, :] = h[1]

    y_ref[...] = jnp.swapaxes(ytm_scr[...], 0, 1)


def s5_scan(proj3, bmat, cmat, lam8, h0, chunks):
    rows, seq, _ = proj3.shape
    ns = S5_COL_STATE
    has_init = h0 is not None
    in_specs = [pl.BlockSpec((SUBLANES, seq, S5_COL_CH), lambda i, c: (i, 0, c)),
                pl.BlockSpec((2, 1, S5_COL_CH, 2 * ns), lambda i, c: (0, c, 0, 0)),
                pl.BlockSpec((2, 1, 2 * ns, S5_COL_CH), lambda i, c: (0, c, 0, 0)),
                pl.BlockSpec((2, 1, SUBLANES, 2 * ns), lambda i, c: (0, c, 0, 0))]
    args = [proj3, bmat, cmat, lam8]
    if has_init:
        nb = h0.shape[0]
        in_specs.append(pl.BlockSpec((nb, 2, 2, ns), lambda i, c: (0, 0, 0, c)))
        args.append(h0)
    y, st = pl.pallas_call(
        functools.partial(_s5_scan_kernel, seq=seq, chunks=chunks, has_init=has_init),
        grid=(rows // SUBLANES, S5_N_COL),
        in_specs=in_specs,
        out_specs=[pl.BlockSpec((SUBLANES, seq, S5_COL_CH), lambda i, c: (i, 0, c)),
                   pl.BlockSpec((SUBLANES, 2, 2, ns), lambda i, c: (i, 0, 0, c))],
        out_shape=[jax.ShapeDtypeStruct((rows, seq, HALF_MIX), F32),
                   jax.ShapeDtypeStruct((rows, 2, 2, S5_GROUPS * S5_STATE), F32)],
        scratch_shapes=[pltpu.VMEM((S5_TIME_BLOCK, SUBLANES, 2 * ns), F32),
                        pltpu.VMEM((seq, SUBLANES, S5_COL_CH), F32)],
        compiler_params=_params("arbitrary", "arbitrary"),
        name="s5_scan",
    )(*args)
    return y, st


def _s5_glu_kernel(y_ref, u_ref, d_ref, w_ref, b_ref, o_ref):
    y = u_ref[...] * d_ref[...] + y_ref[...]
    z = jax.nn.gelu(y)
    gate = _dot(z.astype(BF16), w_ref[...].astype(BF16)) + b_ref[...]
    o_ref[...] = (z * jax.nn.sigmoid(gate)).astype(o_ref.dtype)


def s5_glu(y, proj, d_skip, w_glu, b_glu, tm=1024):
    t = y.shape[0]
    return pl.pallas_call(
        _s5_glu_kernel,
        grid=(t // tm,),
        in_specs=[pl.BlockSpec((tm, HALF_MIX), lambda i: (i, 0)),
                  pl.BlockSpec((tm, HALF_MIX), lambda i: (i, 0)),
                  pl.BlockSpec((1, HALF_MIX), lambda i: (0, 0)),
                  pl.BlockSpec((HALF_MIX, HALF_MIX), lambda i: (0, 0)),
                  pl.BlockSpec((1, HALF_MIX), lambda i: (0, 0))],
        out_specs=pl.BlockSpec((tm, HALF_MIX), lambda i: (i, 0)),
        out_shape=jax.ShapeDtypeStruct((t, HALF_MIX), BF16),
        compiler_params=_params("arbitrary"),
        name="s5_glu",
    )(y, proj, d_skip.reshape(1, HALF_MIX), w_glu, b_glu.reshape(1, HALF_MIX))


def s5_params(lam_re, lam_im, log_dt, b_re, b_im, c_re, c_im):
    lam = lax.complex(lam_re.astype(F32), lam_im.astype(F32))
    dt = jnp.exp(log_dt.astype(F32))[..., None]
    lam_bar = jnp.exp(lam * dt)
    b_c = lax.complex(b_re.astype(F32), b_im.astype(F32))
    b_bar = ((lam_bar - 1.0) / lam)[..., None] * b_c
    eye = jnp.eye(S5_COL_GROUPS, dtype=F32)

    def block_diag_b(m):
        m = m.reshape(2, S5_N_COL, S5_COL_GROUPS, S5_STATE, S5_GROUP_CH)
        bd = jnp.einsum('dngpc,gh->dngchp', m, eye)
        return bd.reshape(2, S5_N_COL, S5_COL_CH, S5_COL_STATE)

    def block_diag_c(m):
        m = m.reshape(2, S5_N_COL, S5_COL_GROUPS, S5_GROUP_CH, S5_STATE)
        bd = jnp.einsum('dngcp,gh->dngphc', m, eye)
        return bd.reshape(2, S5_N_COL, S5_COL_STATE, S5_COL_CH)

    bmat = jnp.concatenate([block_diag_b(b_bar.real), block_diag_b(b_bar.imag)], axis=-1).astype(BF16)
    cmat = jnp.concatenate([block_diag_c(c_re.astype(F32)), block_diag_c(-c_im.astype(F32))],
                           axis=-2).astype(BF16)
    lam_cat = jnp.concatenate([lam_bar.real.reshape(2, S5_N_COL, S5_COL_STATE),
                               lam_bar.imag.reshape(2, S5_N_COL, S5_COL_STATE)], axis=-1)
    lam8 = jnp.broadcast_to(lam_cat[:, :, None, :], (2, S5_N_COL, SUBLANES, 2 * S5_COL_STATE))
    return bmat, cmat, lam8


def _na_ctx_kernel(q_ref, k_ref, v_ref, o_ref, ko_ref, vo_ref):
    scale = NA_HEAD_DIM ** -0.5
    q = q_ref[0]
    k = k_ref[0]
    v = v_ref[0]
    outs = []
    for h in range(NA_HEADS):
        sl = slice(h * NA_HEAD_DIM, (h + 1) * NA_HEAD_DIM)
        kh = k[:, sl]
        vh = v[:, sl]
        ko_ref[0, 0, h] = kh
        vo_ref[0, 0, h] = vh
        s = _dot_nt(q[:, sl].astype(BF16), kh.astype(BF16)) * scale
        p = jnp.exp(s - jnp.max(s, axis=-1, keepdims=True))
        p = p / jnp.sum(p, axis=-1, keepdims=True)
        outs.append(_dot(p.astype(BF16), vh.astype(BF16)))
    o_ref[0] = jnp.concatenate(outs, axis=-1).astype(o_ref.dtype)


def na_ctx(proj3):
    b, seq, _ = proj3.shape
    cache_shape = jax.ShapeDtypeStruct((b, 1, NA_HEADS, seq, NA_HEAD_DIM), F32)
    cache_spec = pl.BlockSpec((1, 1, NA_HEADS, seq, NA_HEAD_DIM), lambda i: (i, 0, 0, 0, 0))
    return pl.pallas_call(
        _na_ctx_kernel,
        grid=(b,),
        in_specs=[pl.BlockSpec((1, seq, HALF_MIX), lambda i: (i, 0, 1)),
                  pl.BlockSpec((1, seq, HALF_MIX), lambda i: (i, 0, 2)),
                  pl.BlockSpec((1, seq, HALF_MIX), lambda i: (i, 0, 3))],
        out_specs=[pl.BlockSpec((1, seq, HALF_MIX), lambda i: (i, 0, 0)), cache_spec, cache_spec],
        out_shape=[jax.ShapeDtypeStruct((b, seq, HALF_MIX), BF16), cache_shape, cache_shape],
        compiler_params=_params("arbitrary"),
        name="na_ctx",
    )(proj3, proj3, proj3)


def na_bias_table(rpb):
    j = jnp.arange(NA_WIN_R)
    rr = jnp.arange(NA_WIN_R)
    roff = rr[None, :] - j[:, None] + (NA_WIN_R - 1)
    qcol = jnp.arange(GRID_W)
    cc = jnp.arange(GRID_W)
    cs = jnp.clip(qcol - NA_WIN_C // 2, 0, GRID_W - NA_WIN_C)
    valid = (cc[None, :] >= cs[:, None]) & (cc[None, :] < cs[:, None] + NA_WIN_C)
    coff = jnp.clip(cc[None, :] - qcol[:, None] + (NA_WIN_C - 1), 0, 2 * NA_WIN_C - 2)
    tbl = rpb.astype(F32)[:, roff[:, :, None, None], coff[None, None, :, :]]
    tbl = jnp.where(valid[None, None, None], tbl, NEG_BIG)
    return tbl.transpose(1, 0, 3, 2, 4).reshape(NA_WIN_R, NA_HEADS, GRID_W, NA_WIN_R * GRID_W)


def _na_lat_kernel(q_ref, k_ref, v_ref, kc_ref, vc_ref, bias_ref, o_ref):
    scale = NA_HEAD_DIM ** -0.5
    r = pl.program_id(1)
    rows = k_ref.shape[1] // GRID_W
    rs = jnp.clip(r - NA_WIN_R // 2, 0, rows - NA_WIN_R)
    start = pl.multiple_of(rs * GRID_W, GRID_W)
    q = q_ref[0]
    kw = k_ref[0, pl.ds(start, NA_WIN_R * GRID_W), :]
    vw = v_ref[0, pl.ds(start, NA_WIN_R * GRID_W), :]
    outs = []
    for h in range(NA_HEADS):
        sl = slice(h * NA_HEAD_DIM, (h + 1) * NA_HEAD_DIM)
        qh = q[:, sl].astype(BF16)
        s_loc = _dot_nt(qh, kw[:, sl].astype(BF16)) * scale + bias_ref[0, h]
        s_ctx = _dot_nt(qh, kc_ref[0, 0, h].astype(BF16)) * scale
        m = jnp.maximum(jnp.max(s_loc, axis=-1, keepdims=True), jnp.max(s_ctx, axis=-1, keepdims=True))
        p_loc = jnp.exp(s_loc - m)
        p_ctx = jnp.exp(s_ctx - m)
        inv = 1.0 / (jnp.sum(p_loc, axis=-1, keepdims=True) + jnp.sum(p_ctx, axis=-1, keepdims=True))
        o = (_dot((p_loc * inv).astype(BF16), vw[:, sl].astype(BF16))
             + _dot((p_ctx * inv).astype(BF16), vc_ref[0, 0, h].astype(BF16)))
        outs.append(o)
    o_ref[0] = jnp.concatenate(outs, axis=-1).astype(o_ref.dtype)


def na_lat(proj3, k_ctx, v_ctx, bias):
    b, seq, _ = proj3.shape
    rows = seq // GRID_W
    lc = k_ctx.shape[3]
    ctx_spec = pl.BlockSpec((1, 1, NA_HEADS, lc, NA_HEAD_DIM), lambda i, r: (i, 0, 0, 0, 0))
    return pl.pallas_call(
        _na_lat_kernel,
        grid=(b, rows),
        in_specs=[pl.BlockSpec((1, GRID_W, HALF_MIX), lambda i, r: (i, r, 1)),
                  pl.BlockSpec((1, seq, HALF_MIX), lambda i, r: (i, 0, 2)),
                  pl.BlockSpec((1, seq, HALF_MIX), lambda i, r: (i, 0, 3)),
                  ctx_spec, ctx_spec,
                  pl.BlockSpec((1, NA_HEADS, GRID_W, NA_WIN_R * GRID_W),
                               lambda i, r: (r - jnp.clip(r - NA_WIN_R // 2, 0, rows - NA_WIN_R), 0, 0, 0))],
        out_specs=pl.BlockSpec((1, GRID_W, HALF_MIX), lambda i, r: (i, r, 0)),
        out_shape=jax.ShapeDtypeStruct((b, seq, HALF_MIX), BF16),
        compiler_params=_params("arbitrary", "arbitrary"),
        name="na_lat",
    )(proj3, proj3, proj3, k_ctx, v_ctx, bias)


def _softmax_pair_diff(s1, s2, lam):
    p1 = jnp.exp(s1 - jnp.max(s1, axis=-1, keepdims=True))
    p2 = jnp.exp(s2 - jnp.max(s2, axis=-1, keepdims=True))
    inv1 = 1.0 / jnp.sum(p1, axis=-1, keepdims=True)
    inv2 = lam / jnp.sum(p2, axis=-1, keepdims=True)
    return p1 * inv1 - p2 * inv2


def _sub_ln(o, g, lam_init):
    return o * lax.rsqrt(jnp.mean(o * o, axis=-1, keepdims=True) + EPS) * g * (1.0 - lam_init)


def _diff_ctx_kernel(lam_ref, q_ref, k_ref, v_ref, g_ref, o_ref, ko_ref, vo_ref, *, lam_init):
    scale = DIFF_D ** -0.5
    lam = lam_ref[0, 0]
    lane = lax.broadcasted_iota(jnp.int32, (q_ref.shape[1], DIFF_HEAD_DIM), 1)
    first = lane < DIFF_D
    for h in range(DIFF_HEADS):
        sl = slice(h * DIFF_HEAD_DIM, (h + 1) * DIFF_HEAD_DIM)
        qh = q_ref[0, :, sl]
        kh = k_ref[0, :, sl]
        vh = v_ref[0, :, sl]
        ko_ref[0, 0, h] = kh
        vo_ref[0, 0, h] = vh
        kb = kh.astype(BF16)
        s1 = _dot_nt(jnp.where(first, qh, 0.0).astype(BF16), kb) * scale
        s2 = _dot_nt(jnp.where(first, 0.0, qh).astype(BF16), kb) * scale
        a = _softmax_pair_diff(s1, s2, lam)
        o = _dot(a.astype(BF16), vh.astype(BF16))
        o_ref[0, :, sl] = _sub_ln(o, g_ref[...], lam_init).astype(o_ref.dtype)


def diff_ctx(proj3, lam, subln_g, lam_init):
    b, seq, _ = proj3.shape
    cache_shape = jax.ShapeDtypeStruct((b, 1, DIFF_HEADS, seq, DIFF_HEAD_DIM), F32)
    cache_spec = pl.BlockSpec((1, 1, DIFF_HEADS, seq, DIFF_HEAD_DIM), lambda i: (i, 0, 0, 0, 0))
    return pl.pallas_call(
        functools.partial(_diff_ctx_kernel, lam_init=lam_init),
        grid=(b,),
        in_specs=[pl.BlockSpec(memory_space=pltpu.SMEM),
                  pl.BlockSpec((1, seq, HALF_MIX), lambda i: (i, 0, 0)),
                  pl.BlockSpec((1, seq, HALF_MIX), lambda i: (i, 0, 1)),
                  pl.BlockSpec((1, seq, HALF_MIX), lambda i: (i, 0, 2)),
                  pl.BlockSpec((1, DIFF_HEAD_DIM), lambda i: (0, 0))],
        out_specs=[pl.BlockSpec((1, seq, HALF_MIX), lambda i: (i, 0, 0)), cache_spec, cache_spec],
        out_shape=[jax.ShapeDtypeStruct((b, seq, HALF_MIX), BF16), cache_shape, cache_shape],
        compiler_params=_params("arbitrary"),
        name="diff_ctx",
    )(lam.reshape(1, 1), proj3, proj3, proj3, subln_g.reshape(1, DIFF_HEAD_DIM))


def rope_tables(seq):
    t = jnp.arange(seq)
    row = (t // GRID_W).astype(F32)
    col = (t % GRID_W).astype(F32)
    n_freq = DIFF_D // 4
    inv = ROPE_BASE ** (-jnp.arange(n_freq, dtype=F32) / n_freq)
    ang = jnp.concatenate([row[:, None] * inv, col[:, None] * inv], axis=-1)
    cos = jnp.repeat(jnp.cos(ang), 2, axis=-1)
    sin = jnp.repeat(jnp.sin(ang), 2, axis=-1)
    sign = jnp.where(jnp.arange(DIFF_D) % 2 == 0, -1.0, 1.0).astype(F32)
    sin = sin * sign
    return jnp.tile(cos, (1, 2)), jnp.tile(sin, (1, 2))


def _rope(x, cos, sin_signed):
    lane = lax.broadcasted_iota(jnp.int32, x.shape, 1)
    nxt = pltpu.roll(x, x.shape[1] - 1, 1)
    prv = pltpu.roll(x, 1, 1)
    partner = jnp.where(lane % 2 == 0, nxt, prv)
    return x * cos + partner * sin_signed


def _diff_lat_kernel(lam_ref, q_ref, k_ref, v_ref, kc_ref, vc_ref, cq_ref, sq_ref, ck_ref, sk_ref,
                     g_ref, o_ref, k_all, v_all, *, lam_init):
    seq = k_ref.shape[1]

    @pl.when(pl.program_id(2) == 0)
    def _():
        k_all[0:seq, :] = _rope(k_ref[0], ck_ref[...], sk_ref[...]).astype(BF16)
        k_all[seq:, :] = kc_ref[0, 0, 0].astype(BF16)
        v_all[0:seq, :] = v_ref[0].astype(BF16)
        v_all[seq:, :] = vc_ref[0, 0, 0].astype(BF16)

    lam = lam_ref[0, 0]
    q = _rope(q_ref[0], cq_ref[...], sq_ref[...]) * (DIFF_D ** -0.5)
    lane = lax.broadcasted_iota(jnp.int32, q.shape, 1)
    first = lane < DIFF_D
    kb = k_all[...]
    s1 = _dot_nt(jnp.where(first, q, 0.0).astype(BF16), kb)
    s2 = _dot_nt(jnp.where(first, 0.0, q).astype(BF16), kb)
    a = _softmax_pair_diff(s1, s2, lam)
    o = _dot(a.astype(BF16), v_all[...])
    o_ref[0] = _sub_ln(o, g_ref[...], lam_init).astype(o_ref.dtype)


def diff_lat(proj3, k_ctx, v_ctx, cos, sin, lam, subln_g, lam_init, tq=256):
    b, seq, _ = proj3.shape
    lc = k_ctx.shape[3]
    hd = DIFF_HEAD_DIM
    ctx_spec = pl.BlockSpec((1, 1, 1, lc, hd), lambda i, h, q: (i, 0, h, 0, 0))
    tq_spec = pl.BlockSpec((tq, hd), lambda i, h, q: (q, 0))
    full_spec = pl.BlockSpec((seq, hd), lambda i, h, q: (0, 0))
    return pl.pallas_call(
        functools.partial(_diff_lat_kernel, lam_init=lam_init),
        grid=(b, DIFF_HEADS, seq // tq),
        in_specs=[pl.BlockSpec(memory_space=pltpu.SMEM),
                  pl.BlockSpec((1, tq, hd), lambda i, h, q: (i, q, h)),
                  pl.BlockSpec((1, seq, hd), lambda i, h, q: (i, 0, DIFF_HEADS + h)),
                  pl.BlockSpec((1, seq, hd), lambda i, h, q: (i, 0, 2 * DIFF_HEADS + h)),
                  ctx_spec, ctx_spec, tq_spec, tq_spec, full_spec, full_spec,
                  pl.BlockSpec((1, hd), lambda i, h, q: (0, 0))],
        out_specs=pl.BlockSpec((1, tq, hd), lambda i, h, q: (i, q, h)),
        out_shape=jax.ShapeDtypeStruct((b, seq, HALF_MIX), BF16),
        scratch_shapes=[pltpu.VMEM((seq + lc, hd), BF16), pltpu.VMEM((seq + lc, hd), BF16)],
        compiler_params=_params("arbitrary", "arbitrary", "arbitrary"),
        name="diff_lat",
    )(lam.reshape(1, 1), proj3, proj3, proj3, k_ctx, v_ctx, cos, sin, cos, sin,
      subln_g.reshape(1, hd))


CONV_PAD = 16
CONV_SUB = 64


def _conv_kernel(a_ref, g_ref, ap_ref, gp_ref, an_ref, gn_ref, w_ref, b_ref, lg_ref, lb_ref, o_ref, xp_scr):
    t = pl.program_id(1)
    tt = a_ref.shape[1]
    prev = ap_ref[0] * jax.nn.sigmoid(gp_ref[0])
    nxt = an_ref[0] * jax.nn.sigmoid(gn_ref[0])
    xp_scr[0:CONV_PAD, :] = jnp.where(t > 0, prev, 0.0)
    xp_scr[CONV_PAD + tt:, :] = jnp.where(t < pl.num_programs(1) - 1, nxt, 0.0)
    xp_scr[CONV_PAD:CONV_PAD + tt, :] = a_ref[0] * jax.nn.sigmoid(g_ref[0])
    first_tap = CONV_PAD - CONV_WIDTH // 2
    for i in range(tt // CONV_SUB):
        s = i * CONV_SUB
        acc = jnp.zeros((CONV_SUB, HALF_MIX), F32)
        for j in range(CONV_WIDTH):
            acc = acc + xp_scr[s + first_tap + j:s + first_tap + j + CONV_SUB, :] * w_ref[j:j + 1, :]
        y = acc + b_ref[...]
        mu = jnp.mean(y, axis=-1, keepdims=True)
        yc = y - mu
        var = jnp.mean(yc * yc, axis=-1, keepdims=True)
        yn = yc * lax.rsqrt(var + EPS) * lg_ref[...] + lb_ref[...]
        o_ref[0, s:s + CONV_SUB, :] = _silu(yn).astype(o_ref.dtype)


def conformer_conv(proj3, w, b, ln_g, ln_b, tt):
    bsz, seq, _ = proj3.shape
    n_t = seq // tt
    hb = tt // CONV_PAD
    last = seq // CONV_PAD - 1
    vec = pl.BlockSpec((1, HALF_MIX), lambda i, t: (0, 0))

    def main(col):
        return pl.BlockSpec((1, tt, HALF_MIX), lambda i, t: (i, t, col))

    def prev(col):
        return pl.BlockSpec((1, CONV_PAD, HALF_MIX), lambda i, t: (i, jnp.maximum(t * hb - 1, 0), col))

    def nxt(col):
        return pl.BlockSpec((1, CONV_PAD, HALF_MIX), lambda i, t: (i, jnp.minimum((t + 1) * hb, last), col))

    return pl.pallas_call(
        _conv_kernel,
        grid=(bsz, n_t),
        in_specs=[main(3), main(4), prev(3), prev(4), nxt(3), nxt(4),
                  pl.BlockSpec((CONV_WIDTH, HALF_MIX), lambda i, t: (0, 0)),
                  vec, vec, vec],
        out_specs=pl.BlockSpec((1, tt, HALF_MIX), lambda i, t: (i, t, 0)),
        out_shape=jax.ShapeDtypeStruct((bsz, seq, HALF_MIX), BF16),
        scratch_shapes=[pltpu.VMEM((tt + 2 * CONV_PAD, HALF_MIX), F32)],
        compiler_params=_params("arbitrary", "arbitrary"),
        name="conformer_conv",
    )(proj3, proj3, proj3, proj3, proj3, proj3, w, b.reshape(1, HALF_MIX), ln_g.reshape(1, HALF_MIX),
      ln_b.reshape(1, HALF_MIX))


def _mix_out_kernel(x_ref, m1_ref, m2_ref, w1_ref, w2_ref, gate_ref, o_ref):
    acc = _dot(m1_ref[...], w1_ref[...].astype(BF16)) + _dot(m2_ref[...], w2_ref[...].astype(BF16))
    o_ref[...] = x_ref[...] + gate_ref[0] * acc


def mix_out(x, m1, m2, w_out, mod, tiles_per_row, tm=512):
    t = x.shape[0]
    return pl.pallas_call(
        _mix_out_kernel,
        grid=(t // tm,),
        in_specs=[pl.BlockSpec((tm, D_MODEL), lambda i: (i, 0)),
                  pl.BlockSpec((tm, HALF_MIX), lambda i: (i, 0)),
                  pl.BlockSpec((tm, HALF_MIX), lambda i: (i, 0)),
                  pl.BlockSpec((HALF_MIX, D_MODEL), lambda i: (0, 0)),
                  pl.BlockSpec((HALF_MIX, D_MODEL), lambda i: (1, 0)),
                  _mod_spec(2, tiles_per_row)],
        out_specs=pl.BlockSpec((tm, D_MODEL), lambda i: (i, 0)),
        out_shape=jax.ShapeDtypeStruct((t, D_MODEL), F32),
        compiler_params=_params("arbitrary"),
        name="mix_out",
    )(x, m1, m2, w_out, w_out, mod)


def _ffn_kernel(x_ref, g_ref, sh_ref, sc_ref, gate_ref, wg_ref, wu_ref, wd_ref, o_ref, h_scr, acc_scr):
    f = pl.program_id(1)

    @pl.when(f == 0)
    def _():
        h_scr[...] = _norm_mod(x_ref[...], g_ref[...], sh_ref[0], sc_ref[0]).astype(BF16)
        acc_scr[...] = jnp.zeros_like(acc_scr)

    h = h_scr[...]
    a = _dot(h, wg_ref[...].astype(BF16))
    u = _dot(h, wu_ref[...].astype(BF16))
    acc_scr[...] += _dot((_silu(a) * u).astype(BF16), wd_ref[...].astype(BF16))

    @pl.when(f == pl.num_programs(1) - 1)
    def _():
        o_ref[...] = x_ref[...] + gate_ref[0] * acc_scr[...]


def ffn(x, g, mod, tiles_per_row, w_gate, w_up, w_down, tm=1024, tf=256):
    t = x.shape[0]
    fdim = w_gate.shape[1]
    return pl.pallas_call(
        _ffn_kernel,
        grid=(t // tm, fdim // tf),
        in_specs=[pl.BlockSpec((tm, D_MODEL), lambda i, f: (i, 0)),
                  pl.BlockSpec((1, D_MODEL), lambda i, f: (0, 0)),
                  _mod_spec(3, tiles_per_row), _mod_spec(4, tiles_per_row), _mod_spec(5, tiles_per_row),
                  pl.BlockSpec((D_MODEL, tf), lambda i, f: (0, f)),
                  pl.BlockSpec((D_MODEL, tf), lambda i, f: (0, f)),
                  pl.BlockSpec((tf, D_MODEL), lambda i, f: (f, 0))],
        out_specs=pl.BlockSpec((tm, D_MODEL), lambda i, f: (i, 0)),
        out_shape=jax.ShapeDtypeStruct((t, D_MODEL), F32),
        scratch_shapes=[pltpu.VMEM((tm, D_MODEL), BF16), pltpu.VMEM((tm, D_MODEL), F32)],
        compiler_params=_params("arbitrary", "arbitrary"),
        name="ffn",
    )(x, g.reshape(1, D_MODEL), mod, mod, mod, w_gate, w_up, w_down)


def _moe_kernel(x_ref, g_ref, sh_ref, sc_ref, gate_ref, rw_ref, fg_ref, wg_ref, wu_ref, wd_ref, o_ref,
                h_scr, comb_scr, acc_scr, tot_scr):
    e = pl.program_id(1)
    f = pl.program_id(2)
    last_f = pl.num_programs(2) - 1

    @pl.when((e == 0) & (f == 0))
    def _():
        h = _norm_mod(x_ref[...], g_ref[...], sh_ref[0], sc_ref[0])
        h_scr[...] = h.astype(BF16)
        logits = jnp.dot(h, rw_ref[...], precision=lax.Precision.HIGHEST, preferred_element_type=F32)
        lane = lax.broadcasted_iota(jnp.int32, logits.shape, 1)
        logits = jnp.where(lane < N_EXPERTS, logits, -jnp.inf)
        m1 = jnp.max(logits, axis=-1, keepdims=True)
        i1 = jnp.min(jnp.where(logits == m1, lane, LANES), axis=-1, keepdims=True)
        rest = jnp.where(lane == i1, -jnp.inf, logits)
        m2 = jnp.max(rest, axis=-1, keepdims=True)
        i2 = jnp.min(jnp.where(rest == m2, lane, LANES), axis=-1, keepdims=True)
        e2 = jnp.exp(m2 - m1)
        den = 1.0 + e2
        comb_scr[...] = jnp.where(lane == i1, 1.0 / den, 0.0) + jnp.where(lane == i2, e2 / den, 0.0)
        tot_scr[...] = jnp.zeros_like(tot_scr)

    @pl.when(f == 0)
    def _():
        acc_scr[...] = jnp.zeros_like(acc_scr)

    h = h_scr[...]
    a = _dot(h, wg_ref[0].astype(BF16))
    u = _dot(h, wu_ref[0].astype(BF16))
    acc_scr[...] += _dot((_silu(a) * u).astype(BF16), wd_ref[0].astype(BF16))

    @pl.when(f == last_f)
    def _():
        comb = comb_scr[...]
        lane = lax.broadcasted_iota(jnp.int32, comb.shape, 1)
        cw = jnp.sum(jnp.where(lane == e, comb, 0.0), axis=-1, keepdims=True)
        tot_scr[...] += cw * acc_scr[...]

    @pl.when((f == last_f) & (e == pl.num_programs(1) - 1))
    def _():
        y = x_ref[...] + gate_ref[0] * tot_scr[...]
        o_ref[...] = y * lax.rsqrt(jnp.mean(y * y, axis=-1, keepdims=True) + EPS) * fg_ref[...]


def moe_final(x, g, mod, tiles_per_row, router_w, final_g, w_gate, w_up, w_down, tm=1024, tf=512):
    t = x.shape[0]
    fdim = w_gate.shape[2]
    rw = jnp.pad(router_w, ((0, 0), (0, LANES - N_EXPERTS)))
    return pl.pallas_call(
        _moe_kernel,
        grid=(t // tm, N_EXPERTS, fdim // tf),
        in_specs=[pl.BlockSpec((tm, D_MODEL), lambda i, e, f: (i, 0)),
                  pl.BlockSpec((1, D_MODEL), lambda i, e, f: (0, 0)),
                  _mod_spec(3, tiles_per_row), _mod_spec(4, tiles_per_row), _mod_spec(5, tiles_per_row),
                  pl.BlockSpec((D_MODEL, LANES), lambda i, e, f: (0, 0)),
                  pl.BlockSpec((1, D_MODEL), lambda i, e, f: (0, 0)),
                  pl.BlockSpec((1, D_MODEL, tf), lambda i, e, f: (e, 0, f)),
                  pl.BlockSpec((1, D_MODEL, tf), lambda i, e, f: (e, 0, f)),
                  pl.BlockSpec((1, tf, D_MODEL), lambda i, e, f: (e, f, 0))],
        out_specs=pl.BlockSpec((tm, D_MODEL), lambda i, e, f: (i, 0)),
        out_shape=jax.ShapeDtypeStruct((t, D_MODEL), F32),
        scratch_shapes=[pltpu.VMEM((tm, D_MODEL), BF16), pltpu.VMEM((tm, LANES), F32),
                        pltpu.VMEM((tm, D_MODEL), F32), pltpu.VMEM((tm, D_MODEL), F32)],
        compiler_params=_params("arbitrary", "arbitrary", "arbitrary"),
        name="moe_final",
    )(x, g.reshape(1, D_MODEL), mod, mod, mod, rw, final_g.reshape(1, D_MODEL), w_gate, w_up, w_down)


def kernel(x_prompt, x_sample, state_s5, cache_na_k, cache_na_v, cache_diff_k, cache_diff_v, c, c_ctx, w_mod, b_mod, norm_mix_g, norm_ffn_g, final_norm_g, w_in_e, w_out_e, s5_lam_re, s5_lam_im, s5_log_dt, s5_b_re, s5_b_im, s5_c_re, s5_c_im, s5_d, s5_w_glu, s5_b_glu, na_rpb, ffn_w_gate, ffn_w_up, ffn_w_down, w_in_o, w_out_o, diff_lam_q1, diff_lam_k1, diff_lam_q2, diff_lam_k2, diff_subln_g, conv_w, conv_b, conv_ln_g, conv_ln_b, router_w, moe_w_gate, moe_w_up, moe_w_down):
    bp, lp, d = x_prompt.shape
    bs, ls, _ = x_sample.shape
    tm = 1024
    xp = x_prompt.reshape(bp * lp, d)
    xs = x_sample.reshape(bs * ls, d)
    rows_p = (bp * lp) // tm
    rows_s = ls // tm

    cond8 = jnp.concatenate([c, c_ctx[None, :], jnp.zeros((SUBLANES - bs - 1, d), F32)], axis=0)
    mod = adaln_all(cond8, w_mod, b_mod)
    mod_s = mod[:, 0:bs, None, :]
    mod_p = mod[:, bs:bs + 1, None, :]

    def tiles(rows, tile):
        return rows * tm // tile

    bmat, cmat, lam8 = s5_params(s5_lam_re[0], s5_lam_im[0], s5_log_dt[0], s5_b_re[0], s5_b_im[0],
                                 s5_c_re[0], s5_c_im[0])
    bias = na_bias_table(na_rpb[0])
    n_e = w_in_e.shape[-1]

    proj_p = in_proj(xp, norm_mix_g[0], mod_p[0], rows_p, w_in_e[0])
    proj_s = in_proj(xs, norm_mix_g[0], mod_s[0], rows_s, w_in_e[0])

    y_p, st_p = s5_scan(proj_p.reshape(bp, lp, n_e), bmat, cmat, lam8, None, 1)
    chunks = SUBLANES // bs
    h0 = state_s5[:, 0].reshape(bs, 2, 2, S5_GROUPS * S5_STATE)
    y_s, _ = s5_scan(proj_s.reshape(bs * chunks, ls // chunks, n_e), bmat, cmat, lam8, h0, chunks)
    s5o_p = s5_glu(y_p.reshape(bp * lp, HALF_MIX), proj_p, s5_d[0], s5_w_glu[0], s5_b_glu[0])
    s5o_s = s5_glu(y_s.reshape(bs * ls, HALF_MIX), proj_s, s5_d[0], s5_w_glu[0], s5_b_glu[0])

    nao_p, na_k, na_v = na_ctx(proj_p.reshape(bp, lp, n_e))
    nao_s = na_lat(proj_s.reshape(bs, ls, n_e), cache_na_k[:, 0:1], cache_na_v[:, 0:1], bias)

    xp = mix_out(xp, s5o_p, nao_p.reshape(bp * lp, HALF_MIX), w_out_e[0], mod_p[0], tiles(rows_p, 512))
    xs = mix_out(xs, s5o_s, nao_s.reshape(bs * ls, HALF_MIX), w_out_e[0], mod_s[0], tiles(rows_s, 512))
    xp = ffn(xp, norm_ffn_g[0], mod_p[0], rows_p, ffn_w_gate[0], ffn_w_up[0], ffn_w_down[0])
    xs = ffn(xs, norm_ffn_g[0], mod_s[0], rows_s, ffn_w_gate[0], ffn_w_up[0], ffn_w_down[0])

    lam_init = 0.8 - 0.6 * math.exp(-0.3 * 1)
    lam = (jnp.exp(jnp.sum(diff_lam_q1[0].astype(F32) * diff_lam_k1[0].astype(F32)))
           - jnp.exp(jnp.sum(diff_lam_q2[0].astype(F32) * diff_lam_k2[0].astype(F32)))
           + lam_init)
    cos, sin = rope_tables(ls)
    n_o = w_in_o.shape[-1]

    proj_p = in_proj(xp, norm_mix_g[1], mod_p[1], rows_p, w_in_o[0])
    proj_s = in_proj(xs, norm_mix_g[1], mod_s[1], rows_s, w_in_o[0])

    do_p, diff_k, diff_v = diff_ctx(proj_p.reshape(bp, lp, n_o), lam, diff_subln_g[0], lam_init)
    do_s = diff_lat(proj_s.reshape(bs, ls, n_o), cache_diff_k[:, 0:1], cache_diff_v[:, 0:1], cos, sin,
                    lam, diff_subln_g[0], lam_init)
    co_p = conformer_conv(proj_p.reshape(bp, lp, n_o), conv_w[0], conv_b[0], conv_ln_g[0], conv_ln_b[0], lp)
    co_s = conformer_conv(proj_s.reshape(bs, ls, n_o), conv_w[0], conv_b[0], conv_ln_g[0], conv_ln_b[0], 512)

    xp = mix_out(xp, do_p.reshape(bp * lp, HALF_MIX), co_p.reshape(bp * lp, HALF_MIX), w_out_o[0],
                 mod_p[1], tiles(rows_p, 512))
    xs = mix_out(xs, do_s.reshape(bs * ls, HALF_MIX), co_s.reshape(bs * ls, HALF_MIX), w_out_o[0],
                 mod_s[1], tiles(rows_s, 512))
    yp = moe_final(xp, norm_ffn_g[1], mod_p[1], rows_p, router_w[0], final_norm_g,
                   moe_w_gate[0], moe_w_up[0], moe_w_down[0])
    ys = moe_final(xs, norm_ffn_g[1], mod_s[1], rows_s, router_w[0], final_norm_g,
                   moe_w_gate[0], moe_w_up[0], moe_w_down[0])

    new_state = st_p.reshape(bp, 1, 2, 2, S5_GROUPS, S5_STATE)
    return (yp.reshape(bp, lp, d), ys.reshape(bs, ls, d), new_state, na_k, na_v, diff_k, diff_v)
```

```python
import functools
import math

import jax
import jax.numpy as jnp
import numpy as np
from jax import lax
from jax.experimental import pallas as pl
from jax.experimental.pallas import tpu as pltpu

D_MODEL = 1024
DEPTH = 2
GRID_W = 64
HALF_MIX = 512
S5_GROUP_CH = 16
S5_GROUPS = 32
S5_STATE = 64
NA_HEAD_DIM = 64
NA_HEADS = 8
NA_WIN_R = 8
NA_WIN_C = 16
DIFF_D = 64
DIFF_HEAD_DIM = 128
DIFF_HEADS = 4
ROPE_BASE = 10000.0
CONV_WIDTH = 31
N_EXPERTS = 8
EPS = 1e-6

F32 = jnp.float32
BF16 = jnp.bfloat16
NEG_BIG = -1e30

VMEM_LIMIT_BYTES = 56 * 1024 * 1024
LANES = 128
SUBLANES = 8

S5_COL_GROUPS = 8
S5_COL_CH = S5_COL_GROUPS * S5_GROUP_CH
S5_COL_STATE = S5_COL_GROUPS * S5_STATE
S5_N_COL = S5_GROUPS // S5_COL_GROUPS
S5_TIME_BLOCK = 256


def _params(*sem):
    return pltpu.CompilerParams(dimension_semantics=sem, vmem_limit_bytes=VMEM_LIMIT_BYTES)


def _dot(a, b):
    return jnp.dot(a, b, preferred_element_type=F32)


def _dot_nt(a, b):
    return lax.dot_general(a, b, (((1,), (1,)), ((), ())), preferred_element_type=F32)


def _silu(x):
    return x * jax.nn.sigmoid(x)


def _norm_mod(x, g, shift, scale):
    y = x * lax.rsqrt(jnp.mean(x * x, axis=-1, keepdims=True) + EPS) * g
    return y * (1.0 + scale) + shift


def _mod_kernel(cond_ref, w_ref, b_ref, o_ref):
    s = _silu(cond_ref[...])
    o_ref[0] = jnp.dot(s, w_ref[0], precision=lax.Precision.HIGHEST,
                       preferred_element_type=F32) + b_ref[0]


def adaln_all(cond8, w_mod, b_mod):
    tn = 1536
    n = w_mod.shape[-1]
    return pl.pallas_call(
        _mod_kernel,
        grid=(DEPTH, n // tn),
        in_specs=[pl.BlockSpec((SUBLANES, D_MODEL), lambda l, j: (0, 0)),
                  pl.BlockSpec((1, D_MODEL, tn), lambda l, j: (l, 0, j)),
                  pl.BlockSpec((1, 1, tn), lambda l, j: (l, 0, j))],
        out_specs=pl.BlockSpec((1, SUBLANES, tn), lambda l, j: (l, 0, j)),
        out_shape=jax.ShapeDtypeStruct((DEPTH, SUBLANES, n), F32),
        compiler_params=_params("arbitrary", "arbitrary"),
        name="adaln_mod",
    )(cond8, w_mod, b_mod.reshape(DEPTH, 1, n))


def _mod_spec(chunk, tiles_per_row):
    return pl.BlockSpec((1, 1, D_MODEL), lambda i, *_: (i // tiles_per_row, 0, chunk))


def _in_proj_kernel(x_ref, g_ref, sh_ref, sc_ref, w_ref, o_ref, h_scr):
    @pl.when(pl.program_id(1) == 0)
    def _():
        h_scr[...] = _norm_mod(x_ref[...], g_ref[...], sh_ref[0], sc_ref[0]).astype(BF16)

    o_ref[...] = _dot(h_scr[...], w_ref[...].astype(BF16)).astype(o_ref.dtype)


def in_proj(x, g, mod, tiles_per_row, w, tm=1024, tn=512):
    t = x.shape[0]
    n = w.shape[1]
    return pl.pallas_call(
        _in_proj_kernel,
        grid=(t // tm, n // tn),
        in_specs=[pl.BlockSpec((tm, D_MODEL), lambda i, j: (i, 0)),
                  pl.BlockSpec((1, D_MODEL), lambda i, j: (0, 0)),
                  _mod_spec(0, tiles_per_row),
                  _mod_spec(1, tiles_per_row),
                  pl.BlockSpec((D_MODEL, tn), lambda i, j: (0, j))],
        out_specs=pl.BlockSpec((tm, tn), lambda i, j: (i, j)),
        out_shape=jax.ShapeDtypeStruct((t, n), F32),
        scratch_shapes=[pltpu.VMEM((tm, D_MODEL), BF16)],
        compiler_params=_params("arbitrary", "arbitrary"),
        name="in_proj",
    )(x, g.reshape(1, D_MODEL), mod, mod, w)


def _s5_scan_kernel(*refs, seq, chunks, has_init):
    if has_init:
        u_ref, bm_ref, cm_ref, lam_ref, h0_ref, y_ref, st_ref, bu_scr, ytm_scr = refs
    else:
        u_ref, bm_ref, cm_ref, lam_ref, y_ref, st_ref, bu_scr, ytm_scr = refs
        h0_ref = None
    tb = S5_TIME_BLOCK
    n_tb = seq // tb
    ns = S5_COL_STATE
    row = lax.broadcasted_iota(jnp.int32, (SUBLANES, ns), 0)
    piece = row % chunks

    for d in range(2):
        lam = lam_ref[d, 0]
        lr, li = lam[:, :ns], lam[:, ns:]
        bm = bm_ref[d, 0]
        cm = cm_ref[d, 0]
        blocks = list(range(n_tb)) if d == 0 else list(range(n_tb - 1, -1, -1))

        def load_bu(k):
            ub = u_ref[:, k * tb:(k + 1) * tb, :]
            utm = jnp.swapaxes(ub, 0, 1).reshape(tb * SUBLANES, S5_COL_CH).astype(BF16)
            bu_scr[...] = _dot(utm, bm).reshape(tb, SUBLANES, 2 * ns)

        def scan_block(h, store):
            def step(s, carry):
                hr, hi = carry
                t = (tb - 1 - s) if d == 1 else s
                b = bu_scr[t]
                nr = lr * hr - li * hi + b[:, :ns]
                ni = lr * hi + li * hr + b[:, ns:]
                if store:
                    bu_scr[t, :, :ns] = nr
                    bu_scr[t, :, ns:] = ni
                return nr, ni
            return lax.fori_loop(0, tb, step, h, unroll=2)

        zero = jnp.zeros((SUBLANES, ns), F32)
        if chunks > 1:
            h = (zero, zero)
            for k in blocks:
                load_bu(k)
                h = scan_block(h, False)
            fr, fi = h
            pr, pi = lr, li
            for _ in range(int(math.log2(seq))):
                pr, pi = pr * pr - pi * pi, 2.0 * pr * pi
            edge = 0 if d == 0 else chunks - 1
            shift = 1 if d == 0 else SUBLANES - 1
            if has_init:
                h0r = h0_ref[:, d, 0, :]
                h0i = h0_ref[:, d, 1, :]
                seq_of_row = row // chunks
                er, ei = zero, zero
                for b in range(SUBLANES // chunks):
                    er = jnp.where(seq_of_row == b, h0r[b:b + 1, :], er)
                    ei = jnp.where(seq_of_row == b, h0i[b:b + 1, :], ei)
            else:
                er, ei = zero, zero
            is_edge = piece == edge
            cr = jnp.where(is_edge, er, zero)
            ci = jnp.where(is_edge, ei, zero)
            for _ in range(chunks - 1):
                tr = fr + pr * cr - pi * ci
                ti = fi + pr * ci + pi * cr
                cr = jnp.where(is_edge, er, pltpu.roll(tr, shift, 0))
                ci = jnp.where(is_edge, ei, pltpu.roll(ti, shift, 0))
            h = (cr, ci)
        else:
            if has_init:
                h = (h0_ref[:, d, 0, :], h0_ref[:, d, 1, :])
            else:
                h = (zero, zero)

        for k in blocks:
            load_bu(k)
            h = scan_block(h, True)
            hb = bu_scr[...].reshape(tb * SUBLANES, 2 * ns).astype(BF16)
            yb = _dot(hb, cm).reshape(tb, SUBLANES, S5_COL_CH)
            if d == 0:
                ytm_scr[k * tb:(k + 1) * tb] = yb
            else:
                ytm_scr[k * tb:(k + 1) * tb] += yb
        st_ref[:, d, 0, :] = h[0]
        st_ref[:, d, 1, :] = h[1]

    y_ref[...] = jnp.swapaxes(ytm_scr[...], 0, 1)


def s5_scan(proj3, bmat, cmat, lam8, h0, chunks):
    rows, seq, _ = proj3.shape
    ns = S5_COL_STATE
    has_init = h0 is not None
    in_specs = [pl.BlockSpec((SUBLANES, seq, S5_COL_CH), lambda i, c: (i, 0, c)),
                pl.BlockSpec((2, 1, S5_COL_CH, 2 * ns), lambda i, c: (0, c, 0, 0)),
                pl.BlockSpec((2, 1, 2 * ns, S5_COL_CH), lambda i, c: (0, c, 0, 0)),
                pl.BlockSpec((2, 1, SUBLANES, 2 * ns), lambda i, c: (0, c, 0, 0))]
    args = [proj3, bmat, cmat, lam8]
    if has_init:
        nb = h0.shape[0]
        in_specs.append(pl.BlockSpec((nb, 2, 2, ns), lambda i, c: (0, 0, 0, c)))
        args.append(h0)
    y, st = pl.pallas_call(
        functools.partial(_s5_scan_kernel, seq=seq, chunks=chunks, has_init=has_init),
        grid=(rows // SUBLANES, S5_N_COL),
        in_specs=in_specs,
        out_specs=[pl.BlockSpec((SUBLANES, seq, S5_COL_CH), lambda i, c: (i, 0, c)),
                   pl.BlockSpec((SUBLANES, 2, 2, ns), lambda i, c: (i, 0, 0, c))],
        out_shape=[jax.ShapeDtypeStruct((rows, seq, HALF_MIX), F32),
                   jax.ShapeDtypeStruct((rows, 2, 2, S5_GROUPS * S5_STATE), F32)],
        scratch_shapes=[pltpu.VMEM((S5_TIME_BLOCK, SUBLANES, 2 * ns), F32),
                        pltpu.VMEM((seq, SUBLANES, S5_COL_CH), F32)],
        compiler_params=_params("arbitrary", "arbitrary"),
        name="s5_scan",
    )(*args)
    return y, st


def _s5_glu_kernel(y_ref, u_ref, d_ref, w_ref, b_ref, o_ref):
    y = u_ref[...] * d_ref[...] + y_ref[...]
    z = jax.nn.gelu(y)
    gate = _dot(z.astype(BF16), w_ref[...].astype(BF16)) + b_ref[...]
    o_ref[...] = (z * jax.nn.sigmoid(gate)).astype(o_ref.dtype)


def s5_glu(y, proj, d_skip, w_glu, b_glu, tm=1024):
    t = y.shape[0]
    return pl.pallas_call(
        _s5_glu_kernel,
        grid=(t // tm,),
        in_specs=[pl.BlockSpec((tm, HALF_MIX), lambda i: (i, 0)),
                  pl.BlockSpec((tm, HALF_MIX), lambda i: (i, 0)),
                  pl.BlockSpec((1, HALF_MIX), lambda i: (0, 0)),
                  pl.BlockSpec((HALF_MIX, HALF_MIX), lambda i: (0, 0)),
                  pl.BlockSpec((1, HALF_MIX), lambda i: (0, 0))],
        out_specs=pl.BlockSpec((tm, HALF_MIX), lambda i: (i, 0)),
        out_shape=jax.ShapeDtypeStruct((t, HALF_MIX), BF16),
        compiler_params=_params("arbitrary"),
        name="s5_glu",
    )(y, proj, d_skip.reshape(1, HALF_MIX), w_glu, b_glu.reshape(1, HALF_MIX))


def s5_params(lam_re, lam_im, log_dt, b_re, b_im, c_re, c_im):
    lr = lam_re.astype(F32)
    li = lam_im.astype(F32)
    dt = jnp.exp(log_dt.astype(F32))[..., None]
    mag = jnp.exp(lr * dt)
    bar_re = mag * jnp.cos(li * dt)
    bar_im = mag * jnp.sin(li * dt)
    den = lr * lr + li * li
    q_re = ((bar_re - 1.0) * lr + bar_im * li) / den
    q_im = (bar_im * lr - (bar_re - 1.0) * li) / den
    br = b_re.astype(F32)
    bi = b_im.astype(F32)
    b_bar_re = q_re[..., None] * br - q_im[..., None] * bi
    b_bar_im = q_re[..., None] * bi + q_im[..., None] * br
    eye = jnp.eye(S5_COL_GROUPS, dtype=F32)

    def block_diag_b(m):
        m = m.reshape(2, S5_N_COL, S5_COL_GROUPS, S5_STATE, S5_GROUP_CH)
        bd = jnp.einsum('dngpc,gh->dngchp', m, eye)
        return bd.reshape(2, S5_N_COL, S5_COL_CH, S5_COL_STATE)

    def block_diag_c(m):
        m = m.reshape(2, S5_N_COL, S5_COL_GROUPS, S5_GROUP_CH, S5_STATE)
        bd = jnp.einsum('dngcp,gh->dngphc', m, eye)
        return bd.reshape(2, S5_N_COL, S5_COL_STATE, S5_COL_CH)

    bmat = jnp.concatenate([block_diag_b(b_bar_re), block_diag_b(b_bar_im)], axis=-1).astype(BF16)
    cmat = jnp.concatenate([block_diag_c(c_re.astype(F32)), block_diag_c(-c_im.astype(F32))],
                           axis=-2).astype(BF16)
    lam_cat = jnp.concatenate([bar_re.reshape(2, S5_N_COL, S5_COL_STATE),
                               bar_im.reshape(2, S5_N_COL, S5_COL_STATE)], axis=-1)
    lam8 = jnp.broadcast_to(lam_cat[:, :, None, :], (2, S5_N_COL, SUBLANES, 2 * S5_COL_STATE))
    return bmat, cmat, lam8


def _na_ctx_kernel(q_ref, k_ref, v_ref, o_ref, ko_ref, vo_ref):
    scale = NA_HEAD_DIM ** -0.5
    q = q_ref[0]
    k = k_ref[0]
    v = v_ref[0]
    outs = []
    for h in range(NA_HEADS):
        sl = slice(h * NA_HEAD_DIM, (h + 1) * NA_HEAD_DIM)
        kh = k[:, sl]
        vh = v[:, sl]
        ko_ref[0, 0, h] = kh
        vo_ref[0, 0, h] = vh
        s = _dot_nt(q[:, sl].astype(BF16), kh.astype(BF16)) * scale
        p = jnp.exp(s - jnp.max(s, axis=-1, keepdims=True))
        p = p / jnp.sum(p, axis=-1, keepdims=True)
        outs.append(_dot(p.astype(BF16), vh.astype(BF16)))
    o_ref[0] = jnp.concatenate(outs, axis=-1).astype(o_ref.dtype)


def na_ctx(proj3):
    b, seq, _ = proj3.shape
    cache_shape = jax.ShapeDtypeStruct((b, 1, NA_HEADS, seq, NA_HEAD_DIM), F32)
    cache_spec = pl.BlockSpec((1, 1, NA_HEADS, seq, NA_HEAD_DIM), lambda i: (i, 0, 0, 0, 0))
    return pl.pallas_call(
        _na_ctx_kernel,
        grid=(b,),
        in_specs=[pl.BlockSpec((1, seq, HALF_MIX), lambda i: (i, 0, 1)),
                  pl.BlockSpec((1, seq, HALF_MIX), lambda i: (i, 0, 2)),
                  pl.BlockSpec((1, seq, HALF_MIX), lambda i: (i, 0, 3))],
        out_specs=[pl.BlockSpec((1, seq, HALF_MIX), lambda i: (i, 0, 0)), cache_spec, cache_spec],
        out_shape=[jax.ShapeDtypeStruct((b, seq, HALF_MIX), BF16), cache_shape, cache_shape],
        compiler_params=_params("arbitrary"),
        name="na_ctx",
    )(proj3, proj3, proj3)


def na_bias_table(rpb):
    qcol = np.arange(GRID_W)
    cc = np.arange(GRID_W)
    cs = np.clip(qcol - NA_WIN_C // 2, 0, GRID_W - NA_WIN_C)
    valid = (cc[None, :] >= cs[:, None]) & (cc[None, :] < cs[:, None] + NA_WIN_C)
    coff = cc[None, :] - qcol[:, None] + (NA_WIN_C - 1)
    n_col = 2 * NA_WIN_C - 1
    sel = ((coff[None] == np.arange(n_col)[:, None, None]) & valid[None]).astype(np.float32)
    sel = sel.reshape(n_col, GRID_W * GRID_W)
    mask = np.where(valid, 0.0, NEG_BIG).astype(np.float32).reshape(1, GRID_W * GRID_W)
    n_row = 2 * NA_WIN_R - 1
    t1 = jnp.dot(rpb.astype(F32).reshape(NA_HEADS * n_row, n_col), jnp.asarray(sel),
                 precision=lax.Precision.HIGHEST) + jnp.asarray(mask)
    t1 = t1.reshape(NA_HEADS, n_row, GRID_W, GRID_W)
    per_j = [t1[:, NA_WIN_R - 1 - j:2 * NA_WIN_R - 1 - j].transpose(0, 2, 1, 3)
             .reshape(NA_HEADS, GRID_W, NA_WIN_R * GRID_W) for j in range(NA_WIN_R)]
    return jnp.stack(per_j, axis=0)


def _na_lat_kernel(q_ref, k_ref, v_ref, kc_ref, vc_ref, bias_ref, o_ref):
    scale = NA_HEAD_DIM ** -0.5
    r = pl.program_id(1)
    rows = k_ref.shape[1] // GRID_W
    rs = jnp.clip(r - NA_WIN_R // 2, 0, rows - NA_WIN_R)
    start = pl.multiple_of(rs * GRID_W, GRID_W)
    q = q_ref[0]
    kw = k_ref[0, pl.ds(start, NA_WIN_R * GRID_W), :]
    vw = v_ref[0, pl.ds(start, NA_WIN_R * GRID_W), :]
    outs = []
    for h in range(NA_HEADS):
        sl = slice(h * NA_HEAD_DIM, (h + 1) * NA_HEAD_DIM)
        qh = q[:, sl].astype(BF16)
        s_loc = _dot_nt(qh, kw[:, sl].astype(BF16)) * scale + bias_ref[0, h]
        s_ctx = _dot_nt(qh, kc_ref[0, 0, h].astype(BF16)) * scale
        m = jnp.maximum(jnp.max(s_loc, axis=-1, keepdims=True), jnp.max(s_ctx, axis=-1, keepdims=True))
        p_loc = jnp.exp(s_loc - m)
        p_ctx = jnp.exp(s_ctx - m)
        inv = 1.0 / (jnp.sum(p_loc, axis=-1, keepdims=True) + jnp.sum(p_ctx, axis=-1, keepdims=True))
        o = (_dot((p_loc * inv).astype(BF16), vw[:, sl].astype(BF16))
             + _dot((p_ctx * inv).astype(BF16), vc_ref[0, 0, h].astype(BF16)))
        outs.append(o)
    o_ref[0] = jnp.concatenate(outs, axis=-1).astype(o_ref.dtype)


def na_lat(proj3, k_ctx, v_ctx, bias):
    b, seq, _ = proj3.shape
    rows = seq // GRID_W
    lc = k_ctx.shape[3]
    ctx_spec = pl.BlockSpec((1, 1, NA_HEADS, lc, NA_HEAD_DIM), lambda i, r: (i, 0, 0, 0, 0))
    return pl.pallas_call(
        _na_lat_kernel,
        grid=(b, rows),
        in_specs=[pl.BlockSpec((1, GRID_W, HALF_MIX), lambda i, r: (i, r, 1)),
                  pl.BlockSpec((1, seq, HALF_MIX), lambda i, r: (i, 0, 2)),
                  pl.BlockSpec((1, seq, HALF_MIX), lambda i, r: (i, 0, 3)),
                  ctx_spec, ctx_spec,
                  pl.BlockSpec((1, NA_HEADS, GRID_W, NA_WIN_R * GRID_W),
                               lambda i, r: (r - jnp.clip(r - NA_WIN_R // 2, 0, rows - NA_WIN_R), 0, 0, 0))],
        out_specs=pl.BlockSpec((1, GRID_W, HALF_MIX), lambda i, r: (i, r, 0)),
        out_shape=jax.ShapeDtypeStruct((b, seq, HALF_MIX), BF16),
        compiler_params=_params("arbitrary", "arbitrary"),
        name="na_lat",
    )(proj3, proj3, proj3, k_ctx, v_ctx, bias)


def _softmax_pair_diff(s1, s2, lam):
    p1 = jnp.exp(s1 - jnp.max(s1, axis=-1, keepdims=True))
    p2 = jnp.exp(s2 - jnp.max(s2, axis=-1, keepdims=True))
    inv1 = 1.0 / jnp.sum(p1, axis=-1, keepdims=True)
    inv2 = lam / jnp.sum(p2, axis=-1, keepdims=True)
    return p1 * inv1 - p2 * inv2


def _sub_ln(o, g, lam_init):
    return o * lax.rsqrt(jnp.mean(o * o, axis=-1, keepdims=True) + EPS) * g * (1.0 - lam_init)


def _diff_ctx_kernel(lam_ref, q_ref, k_ref, v_ref, g_ref, o_ref, ko_ref, vo_ref, *, lam_init):
    scale = DIFF_D ** -0.5
    lam = lam_ref[0, 0]
    lane = lax.broadcasted_iota(jnp.int32, (q_ref.shape[1], DIFF_HEAD_DIM), 1)
    first = lane < DIFF_D
    for h in range(DIFF_HEADS):
        sl = slice(h * DIFF_HEAD_DIM, (h + 1) * DIFF_HEAD_DIM)
        qh = q_ref[0, :, sl]
        kh = k_ref[0, :, sl]
        vh = v_ref[0, :, sl]
        ko_ref[0, 0, h] = kh
        vo_ref[0, 0, h] = vh
        kb = kh.astype(BF16)
        s1 = _dot_nt(jnp.where(first, qh, 0.0).astype(BF16), kb) * scale
        s2 = _dot_nt(jnp.where(first, 0.0, qh).astype(BF16), kb) * scale
        a = _softmax_pair_diff(s1, s2, lam)
        o = _dot(a.astype(BF16), vh.astype(BF16))
        o_ref[0, :, sl] = _sub_ln(o, g_ref[...], lam_init).astype(o_ref.dtype)


def diff_ctx(proj3, lam, subln_g, lam_init):
    b, seq, _ = proj3.shape
    cache_shape = jax.ShapeDtypeStruct((b, 1, DIFF_HEADS, seq, DIFF_HEAD_DIM), F32)
    cache_spec = pl.BlockSpec((1, 1, DIFF_HEADS, seq, DIFF_HEAD_DIM), lambda i: (i, 0, 0, 0, 0))
    return pl.pallas_call(
        functools.partial(_diff_ctx_kernel, lam_init=lam_init),
        grid=(b,),
        in_specs=[pl.BlockSpec(memory_space=pltpu.SMEM),
                  pl.BlockSpec((1, seq, HALF_MIX), lambda i: (i, 0, 0)),
                  pl.BlockSpec((1, seq, HALF_MIX), lambda i: (i, 0, 1)),
                  pl.BlockSpec((1, seq, HALF_MIX), lambda i: (i, 0, 2)),
                  pl.BlockSpec((1, DIFF_HEAD_DIM), lambda i: (0, 0))],
        out_specs=[pl.BlockSpec((1, seq, HALF_MIX), lambda i: (i, 0, 0)), cache_spec, cache_spec],
        out_shape=[jax.ShapeDtypeStruct((b, seq, HALF_MIX), BF16), cache_shape, cache_shape],
        compiler_params=_params("arbitrary"),
        name="diff_ctx",
    )(lam.reshape(1, 1), proj3, proj3, proj3, subln_g.reshape(1, DIFF_HEAD_DIM))


def rope_tables(seq):
    t = jnp.arange(seq)
    row = (t // GRID_W).astype(F32)
    col = (t % GRID_W).astype(F32)
    n_freq = DIFF_D // 4
    inv = ROPE_BASE ** (-jnp.arange(n_freq, dtype=F32) / n_freq)
    ang = jnp.concatenate([row[:, None] * inv, col[:, None] * inv], axis=-1)
    cos = jnp.repeat(jnp.cos(ang), 2, axis=-1)
    sin = jnp.repeat(jnp.sin(ang), 2, axis=-1)
    sign = jnp.where(jnp.arange(DIFF_D) % 2 == 0, -1.0, 1.0).astype(F32)
    sin = sin * sign
    return jnp.tile(cos, (1, 2)), jnp.tile(sin, (1, 2))


def _rope(x, cos, sin_signed):
    lane = lax.broadcasted_iota(jnp.int32, x.shape, 1)
    nxt = pltpu.roll(x, x.shape[1] - 1, 1)
    prv = pltpu.roll(x, 1, 1)
    partner = jnp.where(lane % 2 == 0, nxt, prv)
    return x * cos + partner * sin_signed


def _diff_lat_kernel(lam_ref, q_ref, k_ref, v_ref, kc_ref, vc_ref, cq_ref, sq_ref, ck_ref, sk_ref,
                     g_ref, o_ref, k_all, v_all, *, lam_init):
    seq = k_ref.shape[1]

    @pl.when(pl.program_id(2) == 0)
    def _():
        k_all[0:seq, :] = _rope(k_ref[0], ck_ref[...], sk_ref[...]).astype(BF16)
        k_all[seq:, :] = kc_ref[0, 0, 0].astype(BF16)
        v_all[0:seq, :] = v_ref[0].astype(BF16)
        v_all[seq:, :] = vc_ref[0, 0, 0].astype(BF16)

    lam = lam_ref[0, 0]
    q = _rope(q_ref[0], cq_ref[...], sq_ref[...]) * (DIFF_D ** -0.5)
    lane = lax.broadcasted_iota(jnp.int32, q.shape, 1)
    first = lane < DIFF_D
    kb = k_all[...]
    s1 = _dot_nt(jnp.where(first, q, 0.0).astype(BF16), kb)
    s2 = _dot_nt(jnp.where(first, 0.0, q).astype(BF16), kb)
    a = _softmax_pair_diff(s1, s2, lam)
    o = _dot(a.astype(BF16), v_all[...])
    o_ref[0] = _sub_ln(o, g_ref[...], lam_init).astype(o_ref.dtype)


def diff_lat(proj3, k_ctx, v_ctx, cos, sin, lam, subln_g, lam_init, tq=256):
    b, seq, _ = proj3.shape
    lc = k_ctx.shape[3]
    hd = DIFF_HEAD_DIM
    ctx_spec = pl.BlockSpec((1, 1, 1, lc, hd), lambda i, h, q: (i, 0, h, 0, 0))
    tq_spec = pl.BlockSpec((tq, hd), lambda i, h, q: (q, 0))
    full_spec = pl.BlockSpec((seq, hd), lambda i, h, q: (0, 0))
    return pl.pallas_call(
        functools.partial(_diff_lat_kernel, lam_init=lam_init),
        grid=(b, DIFF_HEADS, seq // tq),
        in_specs=[pl.BlockSpec(memory_space=pltpu.SMEM),
                  pl.BlockSpec((1, tq, hd), lambda i, h, q: (i, q, h)),
                  pl.BlockSpec((1, seq, hd), lambda i, h, q: (i, 0, DIFF_HEADS + h)),
                  pl.BlockSpec((1, seq, hd), lambda i, h, q: (i, 0, 2 * DIFF_HEADS + h)),
                  ctx_spec, ctx_spec, tq_spec, tq_spec, full_spec, full_spec,
                  pl.BlockSpec((1, hd), lambda i, h, q: (0, 0))],
        out_specs=pl.BlockSpec((1, tq, hd), lambda i, h, q: (i, q, h)),
        out_shape=jax.ShapeDtypeStruct((b, seq, HALF_MIX), BF16),
        scratch_shapes=[pltpu.VMEM((seq + lc, hd), BF16), pltpu.VMEM((seq + lc, hd), BF16)],
        compiler_params=_params("arbitrary", "arbitrary", "arbitrary"),
        name="diff_lat",
    )(lam.reshape(1, 1), proj3, proj3, proj3, k_ctx, v_ctx, cos, sin, cos, sin,
      subln_g.reshape(1, hd))


CONV_PAD = 16
CONV_SUB = 64


def _conv_kernel(a_ref, g_ref, ap_ref, gp_ref, an_ref, gn_ref, w_ref, b_ref, lg_ref, lb_ref, o_ref, xp_scr,
                 xsh_scr):
    t = pl.program_id(1)
    tt = a_ref.shape[1]
    prev = ap_ref[0] * jax.nn.sigmoid(gp_ref[0])
    nxt = an_ref[0] * jax.nn.sigmoid(gn_ref[0])
    xp_scr[0:CONV_PAD, :] = jnp.where(t > 0, prev, 0.0)
    xp_scr[CONV_PAD + tt:, :] = jnp.where(t < pl.num_programs(1) - 1, nxt, 0.0)
    xp_scr[CONV_PAD:CONV_PAD + tt, :] = a_ref[0] * jax.nn.sigmoid(g_ref[0])
    first_tap = CONV_PAD - CONV_WIDTH // 2
    n_rows = xsh_scr.shape[1]
    for b in range(SUBLANES):
        xsh_scr[b] = xp_scr[b:b + n_rows, :]
    for i in range(tt // CONV_SUB):
        s = i * CONV_SUB
        acc = jnp.zeros((CONV_SUB, HALF_MIX), F32)
        for j in range(CONV_WIDTH):
            whole, phase = divmod(first_tap + j, SUBLANES)
            lo = s + whole * SUBLANES
            acc = acc + xsh_scr[phase, lo:lo + CONV_SUB, :] * w_ref[j:j + 1, :]
        y = acc + b_ref[...]
        mu = jnp.mean(y, axis=-1, keepdims=True)
        yc = y - mu
        var = jnp.mean(yc * yc, axis=-1, keepdims=True)
        yn = yc * lax.rsqrt(var + EPS) * lg_ref[...] + lb_ref[...]
        o_ref[0, s:s + CONV_SUB, :] = _silu(yn).astype(o_ref.dtype)


def conformer_conv(proj3, w, b, ln_g, ln_b, tt):
    bsz, seq, _ = proj3.shape
    n_t = seq // tt
    hb = tt // CONV_PAD
    last = seq // CONV_PAD - 1
    vec = pl.BlockSpec((1, HALF_MIX), lambda i, t: (0, 0))

    def main(col):
        return pl.BlockSpec((1, tt, HALF_MIX), lambda i, t: (i, t, col))

    def prev(col):
        return pl.BlockSpec((1, CONV_PAD, HALF_MIX), lambda i, t: (i, jnp.maximum(t * hb - 1, 0), col))

    def nxt(col):
        return pl.BlockSpec((1, CONV_PAD, HALF_MIX), lambda i, t: (i, jnp.minimum((t + 1) * hb, last), col))

    return pl.pallas_call(
        _conv_kernel,
        grid=(bsz, n_t),
        in_specs=[main(3), main(4), prev(3), prev(4), nxt(3), nxt(4),
                  pl.BlockSpec((CONV_WIDTH, HALF_MIX), lambda i, t: (0, 0)),
                  vec, vec, vec],
        out_specs=pl.BlockSpec((1, tt, HALF_MIX), lambda i, t: (i, t, 0)),
        out_shape=jax.ShapeDtypeStruct((bsz, seq, HALF_MIX), BF16),
        scratch_shapes=[pltpu.VMEM((tt + 2 * CONV_PAD, HALF_MIX), F32),
                        pltpu.VMEM((SUBLANES, tt + 2 * CONV_PAD - SUBLANES, HALF_MIX), F32)],
        compiler_params=_params("arbitrary", "arbitrary"),
        name="conformer_conv",
    )(proj3, proj3, proj3, proj3, proj3, proj3, w, b.reshape(1, HALF_MIX), ln_g.reshape(1, HALF_MIX),
      ln_b.reshape(1, HALF_MIX))


def _mix_out_kernel(x_ref, m1_ref, m2_ref, w1_ref, w2_ref, gate_ref, o_ref):
    acc = _dot(m1_ref[...], w1_ref[...].astype(BF16)) + _dot(m2_ref[...], w2_ref[...].astype(BF16))
    o_ref[...] = x_ref[...] + gate_ref[0] * acc


def mix_out(x, m1, m2, w_out, mod, tiles_per_row, tm=512):
    t = x.shape[0]
    return pl.pallas_call(
        _mix_out_kernel,
        grid=(t // tm,),
        in_specs=[pl.BlockSpec((tm, D_MODEL), lambda i: (i, 0)),
                  pl.BlockSpec((tm, HALF_MIX), lambda i: (i, 0)),
                  pl.BlockSpec((tm, HALF_MIX), lambda i: (i, 0)),
                  pl.BlockSpec((HALF_MIX, D_MODEL), lambda i: (0, 0)),
                  pl.BlockSpec((HALF_MIX, D_MODEL), lambda i: (1, 0)),
                  _mod_spec(2, tiles_per_row)],
        out_specs=pl.BlockSpec((tm, D_MODEL), lambda i: (i, 0)),
        out_shape=jax.ShapeDtypeStruct((t, D_MODEL), F32),
        compiler_params=_params("arbitrary"),
        name="mix_out",
    )(x, m1, m2, w_out, w_out, mod)


def _ffn_kernel(x_ref, g_ref, sh_ref, sc_ref, gate_ref, wg_ref, wu_ref, wd_ref, o_ref, h_scr, acc_scr):
    f = pl.program_id(1)

    @pl.when(f == 0)
    def _():
        h_scr[...] = _norm_mod(x_ref[...], g_ref[...], sh_ref[0], sc_ref[0]).astype(BF16)
        acc_scr[...] = jnp.zeros_like(acc_scr)

    h = h_scr[...]
    a = _dot(h, wg_ref[...].astype(BF16))
    u = _dot(h, wu_ref[...].astype(BF16))
    acc_scr[...] += _dot((_silu(a) * u).astype(BF16), wd_ref[...].astype(BF16))

    @pl.when(f == pl.num_programs(1) - 1)
    def _():
        o_ref[...] = x_ref[...] + gate_ref[0] * acc_scr[...]


def ffn(x, g, mod, tiles_per_row, w_gate, w_up, w_down, tm=1024, tf=256):
    t = x.shape[0]
    fdim = w_gate.shape[1]
    return pl.pallas_call(
        _ffn_kernel,
        grid=(t // tm, fdim // tf),
        in_specs=[pl.BlockSpec((tm, D_MODEL), lambda i, f: (i, 0)),
                  pl.BlockSpec((1, D_MODEL), lambda i, f: (0, 0)),
                  _mod_spec(3, tiles_per_row), _mod_spec(4, tiles_per_row), _mod_spec(5, tiles_per_row),
                  pl.BlockSpec((D_MODEL, tf), lambda i, f: (0, f)),
                  pl.BlockSpec((D_MODEL, tf), lambda i, f: (0, f)),
                  pl.BlockSpec((tf, D_MODEL), lambda i, f: (f, 0))],
        out_specs=pl.BlockSpec((tm, D_MODEL), lambda i, f: (i, 0)),
        out_shape=jax.ShapeDtypeStruct((t, D_MODEL), F32),
        scratch_shapes=[pltpu.VMEM((tm, D_MODEL), BF16), pltpu.VMEM((tm, D_MODEL), F32)],
        compiler_params=_params("arbitrary", "arbitrary"),
        name="ffn",
    )(x, g.reshape(1, D_MODEL), mod, mod, mod, w_gate, w_up, w_down)


MOE_ROW_TILE = 1024
MOE_ZERO_ROWS = 256
INFO_E0, INFO_E1, INFO_G0, INFO_G1, INFO_R0, INFO_R1 = range(6)


def _stream_mod_spec(chunk, prompt_tiles, tiles_per_req, ctx_row):
    def index(i, *_):
        return (jnp.where(i < prompt_tiles, ctx_row, (i - prompt_tiles) // tiles_per_req), 0, chunk)
    return pl.BlockSpec((1, 1, D_MODEL), index)


def _mix_out2_kernel(xp_ref, m1p_ref, m2p_ref, xs_ref, m1s_ref, m2s_ref, w1_ref, w2_ref, gate_ref, o_ref,
                     *, prompt_tiles):
    i = pl.program_id(0)
    w1 = w1_ref[...].astype(BF16)
    w2 = w2_ref[...].astype(BF16)

    @pl.when(i < prompt_tiles)
    def _():
        o_ref[...] = xp_ref[...] + gate_ref[0] * (_dot(m1p_ref[...], w1) + _dot(m2p_ref[...], w2))

    @pl.when(i >= prompt_tiles)
    def _():
        o_ref[...] = xs_ref[...] + gate_ref[0] * (_dot(m1s_ref[...], w1) + _dot(m2s_ref[...], w2))


def mix_out2(xp, m1p, m2p, xs, m1s, m2s, w_out, mod_all, ctx_row, req_tokens, tm=512):
    tp, ts = xp.shape[0], xs.shape[0]
    pt = tp // tm

    def p_spec(width):
        return pl.BlockSpec((tm, width), lambda i: (jnp.minimum(i, pt - 1), 0))

    def s_spec(width):
        return pl.BlockSpec((tm, width), lambda i: (jnp.maximum(i - pt, 0), 0))

    return pl.pallas_call(
        functools.partial(_mix_out2_kernel, prompt_tiles=pt),
        grid=((tp + ts) // tm,),
        in_specs=[p_spec(D_MODEL), p_spec(HALF_MIX), p_spec(HALF_MIX),
                  s_spec(D_MODEL), s_spec(HALF_MIX), s_spec(HALF_MIX),
                  pl.BlockSpec((HALF_MIX, D_MODEL), lambda i: (0, 0)),
                  pl.BlockSpec((HALF_MIX, D_MODEL), lambda i: (1, 0)),
                  _stream_mod_spec(2, pt, req_tokens // tm, ctx_row)],
        out_specs=pl.BlockSpec((tm, D_MODEL), lambda i: (i, 0)),
        out_shape=jax.ShapeDtypeStruct((tp + ts, D_MODEL), F32),
        compiler_params=_params("arbitrary"),
        name="mix_out2",
    )(xp, m1p, m2p, xs, m1s, m2s, w_out, w_out, mod_all)


def _route_kernel(x_ref, g_ref, sh_ref, sc_ref, rw_ref, info_ref, cnt_ref, tri_scr, run_scr):
    i = pl.program_id(0)
    tm = x_ref.shape[0]

    @pl.when(i == 0)
    def _():
        r = lax.broadcasted_iota(jnp.int32, (tm, tm), 0)
        c = lax.broadcasted_iota(jnp.int32, (tm, tm), 1)
        tri_scr[...] = jnp.where(c < r, 1.0, 0.0).astype(BF16)
        run_scr[...] = jnp.zeros_like(run_scr)

    h = _norm_mod(x_ref[...], g_ref[...], sh_ref[0], sc_ref[0])
    logits = jnp.dot(h, rw_ref[...], precision=lax.Precision.HIGHEST, preferred_element_type=F32)
    lane = lax.broadcasted_iota(jnp.int32, logits.shape, 1).astype(F32)
    logits = jnp.where(lane < N_EXPERTS, logits, -jnp.inf)
    m1 = jnp.max(logits, axis=-1, keepdims=True)
    i1 = jnp.min(jnp.where(logits == m1, lane, float(LANES)), axis=-1, keepdims=True)
    rest = jnp.where(lane == i1, -jnp.inf, logits)
    m2 = jnp.max(rest, axis=-1, keepdims=True)
    i2 = jnp.min(jnp.where(rest == m2, lane, float(LANES)), axis=-1, keepdims=True)
    e2 = jnp.exp(m2 - m1)
    den = 1.0 + e2
    hit = jnp.where(lane == i1, 1.0, 0.0) + jnp.where(lane == i2, 1.0, 0.0)
    before = _dot(tri_scr[...], hit.astype(BF16)) + run_scr[0:1, :]
    r1 = jnp.sum(jnp.where(lane == i1, before, 0.0), axis=-1, keepdims=True)
    r2 = jnp.sum(jnp.where(lane == i2, before, 0.0), axis=-1, keepdims=True)
    info = jnp.zeros_like(logits)
    for slot, val in ((INFO_E0, i1), (INFO_E1, i2), (INFO_G0, 1.0 / den), (INFO_G1, e2 / den),
                      (INFO_R0, r1), (INFO_R1, r2)):
        info = jnp.where(lane == float(slot), val, info)
    info_ref[...] = info
    run_scr[...] = run_scr[...] + jnp.sum(hit, axis=0, keepdims=True)
    cnt_ref[...] = run_scr[...]


def moe_route(x, g, mod_all, ctx_row, prompt_tokens, req_tokens, router_w, tm=1024):
    t = x.shape[0]
    rw = jnp.pad(router_w, ((0, 0), (0, LANES - N_EXPERTS)))
    pt = prompt_tokens // tm
    return pl.pallas_call(
        _route_kernel,
        grid=(t // tm,),
        in_specs=[pl.BlockSpec((tm, D_MODEL), lambda i: (i, 0)),
                  pl.BlockSpec((1, D_MODEL), lambda i: (0, 0)),
                  _stream_mod_spec(3, pt, req_tokens // tm, ctx_row),
                  _stream_mod_spec(4, pt, req_tokens // tm, ctx_row),
                  pl.BlockSpec((D_MODEL, LANES), lambda i: (0, 0))],
        out_specs=[pl.BlockSpec((tm, LANES), lambda i: (i, 0)),
                   pl.BlockSpec((SUBLANES, LANES), lambda i: (0, 0))],
        out_shape=[jax.ShapeDtypeStruct((t, LANES), F32), jax.ShapeDtypeStruct((SUBLANES, LANES), F32)],
        scratch_shapes=[pltpu.VMEM((tm, tm), BF16), pltpu.VMEM((SUBLANES, LANES), F32)],
        compiler_params=_params("arbitrary"),
        name="moe_route",
    )(x, g.reshape(1, D_MODEL), mod_all, mod_all, rw)


def _dispatch_kernel(d0_ref, d1_ref, fill_ref, x_ref, g_ref, sh_ref, sc_ref, xs_hbm, h_scr, zero_scr, sem, zsem):
    i = pl.program_id(0)
    n = pl.num_programs(0)
    tm = x_ref.shape[0]
    slot = i % 2

    def row_copy(r, dst, s):
        return pltpu.make_async_copy(h_scr.at[s, pl.ds(r, 1), :], xs_hbm.at[pl.ds(dst, 1), :], sem.at[s])

    def wait_rows(s):
        for _ in range(2):
            pltpu.make_async_copy(h_scr.at[s], xs_hbm.at[pl.ds(0, tm), :], sem.at[s]).wait()

    @pl.when(i >= 2)
    def _():
        wait_rows(slot)

    h_scr[slot] = _norm_mod(x_ref[...], g_ref[...], sh_ref[0], sc_ref[0])
    base = i * tm

    def body(r, c):
        row_copy(r, d0_ref[base + r], slot).start()
        row_copy(r, d1_ref[base + r], slot).start()
        return c

    lax.fori_loop(0, tm, body, 0)

    @pl.when(i == n - 1)
    def _():
        zero_scr[...] = jnp.zeros_like(zero_scr)

        def zero_row(r):
            return pltpu.make_async_copy(zero_scr.at[pl.ds(0, 1), :], xs_hbm.at[pl.ds(r, 1), :], zsem)

        def zero_block(b):
            start = pl.multiple_of(b * MOE_ZERO_ROWS, MOE_ZERO_ROWS)
            return pltpu.make_async_copy(zero_scr, xs_hbm.at[pl.ds(start, MOE_ZERO_ROWS), :], zsem)

        def start_all(copy):
            def body(r, c):
                copy(r).start()
                return c
            return body

        def wait_all(copy):
            def body(r, c):
                copy(r).wait()
                return c
            return body

        for e in range(N_EXPERTS):
            lax.fori_loop(fill_ref[e], fill_ref[N_EXPERTS + e], start_all(zero_row), 0)
            lax.fori_loop(fill_ref[e], fill_ref[N_EXPERTS + e], wait_all(zero_row), 0)
        blocks_per_tile = MOE_ROW_TILE // MOE_ZERO_ROWS
        first = fill_ref[2 * N_EXPERTS] * blocks_per_tile
        last = (xs_hbm.shape[0] // MOE_ROW_TILE) * blocks_per_tile
        lax.fori_loop(first, last, start_all(zero_block), 0)
        lax.fori_loop(first, last, wait_all(zero_block), 0)
        wait_rows(slot)

        @pl.when(n >= 2)
        def _():
            wait_rows(1 - slot)


def moe_dispatch(x, g, mod_all, ctx_row, prompt_tokens, req_tokens, dest0, dest1, fill, n_tiles, tm=512):
    t = x.shape[0]
    pt = prompt_tokens // tm
    grid_spec = pltpu.PrefetchScalarGridSpec(
        num_scalar_prefetch=3,
        grid=(t // tm,),
        in_specs=[pl.BlockSpec((tm, D_MODEL), lambda i, *_: (i, 0)),
                  pl.BlockSpec((1, D_MODEL), lambda i, *_: (0, 0)),
                  _stream_mod_spec(3, pt, req_tokens // tm, ctx_row),
                  _stream_mod_spec(4, pt, req_tokens // tm, ctx_row)],
        out_specs=pl.BlockSpec(memory_space=pl.ANY),
        scratch_shapes=[pltpu.VMEM((2, tm, D_MODEL), F32), pltpu.VMEM((MOE_ZERO_ROWS, D_MODEL), F32),
                        pltpu.SemaphoreType.DMA((2,)), pltpu.SemaphoreType.DMA(())],
    )
    return pl.pallas_call(
        _dispatch_kernel,
        grid_spec=grid_spec,
        out_shape=jax.ShapeDtypeStruct((n_tiles * MOE_ROW_TILE, D_MODEL), F32),
        compiler_params=pltpu.CompilerParams(dimension_semantics=("arbitrary",),
                                             vmem_limit_bytes=VMEM_LIMIT_BYTES,
                                             disable_bounds_checks=True),
        name="moe_dispatch",
    )(dest0, dest1, fill, x, g.reshape(1, D_MODEL), mod_all, mod_all)


def _experts_kernel(te_ref, tv_ref, x_ref, wg_ref, wu_ref, wd_ref, o_ref, h_scr, acc_scr):
    i = pl.program_id(0)
    f = pl.program_id(1)
    last_f = pl.num_programs(1) - 1

    @pl.when((tv_ref[i] == 0) & (f == last_f))
    def _():
        o_ref[...] = jnp.zeros_like(o_ref)

    @pl.when(tv_ref[i] > 0)
    def _():
        @pl.when(f == 0)
        def _():
            h_scr[...] = x_ref[...].astype(BF16)
            acc_scr[...] = jnp.zeros_like(acc_scr)

        h = h_scr[...]
        a = _dot(h, wg_ref[0].astype(BF16))
        u = _dot(h, wu_ref[0].astype(BF16))
        acc_scr[...] += _dot((_silu(a) * u).astype(BF16), wd_ref[0].astype(BF16))

        @pl.when(f == last_f)
        def _():
            o_ref[...] = acc_scr[...]


def moe_experts(xs_sorted, tile_expert, tile_valid, w_gate, w_up, w_down, tf=512):
    rows = xs_sorted.shape[0]
    fdim = w_gate.shape[2]
    n_f = fdim // tf
    tr = MOE_ROW_TILE

    def f_eff(i, f, tv):
        return jnp.where(tv[i] > 0, f, n_f - 1)

    grid_spec = pltpu.PrefetchScalarGridSpec(
        num_scalar_prefetch=2,
        grid=(rows // tr, n_f),
        in_specs=[pl.BlockSpec((tr, D_MODEL), lambda i, f, te, tv: (i, 0)),
                  pl.BlockSpec((1, D_MODEL, tf), lambda i, f, te, tv: (te[i], 0, f_eff(i, f, tv))),
                  pl.BlockSpec((1, D_MODEL, tf), lambda i, f, te, tv: (te[i], 0, f_eff(i, f, tv))),
                  pl.BlockSpec((1, tf, D_MODEL), lambda i, f, te, tv: (te[i], f_eff(i, f, tv), 0))],
        out_specs=pl.BlockSpec((tr, D_MODEL), lambda i, f, te, tv: (i, 0)),
        scratch_shapes=[pltpu.VMEM((tr, D_MODEL), BF16), pltpu.VMEM((tr, D_MODEL), F32)],
    )
    return pl.pallas_call(
        _experts_kernel,
        grid_spec=grid_spec,
        out_shape=jax.ShapeDtypeStruct((rows, D_MODEL), F32),
        compiler_params=_params("arbitrary", "arbitrary"),
        name="moe_experts",
    )(tile_expert, tile_valid, xs_sorted, w_gate, w_up, w_down)


def _combine_kernel(d0_ref, d1_ref, x_ref, info_ref, gate_ref, fg_ref, ys_hbm, op_ref, os_ref, rbuf, sem,
                    *, prompt_tiles):
    i = pl.program_id(0)
    n = pl.num_programs(0)
    tm = x_ref.shape[0]
    slot = i % 2

    def issue(tile, s):
        base = tile * tm

        def body(r, c):
            pltpu.make_async_copy(ys_hbm.at[pl.ds(d0_ref[base + r], 1), :],
                                  rbuf.at[s, 0, pl.ds(r, 1), :], sem.at[s]).start()
            pltpu.make_async_copy(ys_hbm.at[pl.ds(d1_ref[base + r], 1), :],
                                  rbuf.at[s, 1, pl.ds(r, 1), :], sem.at[s]).start()
            return c

        lax.fori_loop(0, tm, body, 0)

    @pl.when(i == 0)
    def _():
        issue(0, 0)

    @pl.when(i + 1 < n)
    def _():
        issue(i + 1, 1 - slot)

    for k in range(2):
        pltpu.make_async_copy(ys_hbm.at[pl.ds(0, tm), :], rbuf.at[slot, k], sem.at[slot]).wait()

    info = info_ref[...]
    moe = info[:, INFO_G0:INFO_G0 + 1] * rbuf[slot, 0] + info[:, INFO_G1:INFO_G1 + 1] * rbuf[slot, 1]
    y = x_ref[...] + gate_ref[0] * moe
    out = y * lax.rsqrt(jnp.mean(y * y, axis=-1, keepdims=True) + EPS) * fg_ref[...]

    @pl.when(i < prompt_tiles)
    def _():
        op_ref[...] = out

    @pl.when(i >= prompt_tiles)
    def _():
        os_ref[...] = out


def moe_combine(x, info, ys_sorted, dest0, dest1, mod_all, ctx_row, prompt_tokens, req_tokens, final_g, tm=256):
    t = x.shape[0]
    pt = prompt_tokens // tm
    grid_spec = pltpu.PrefetchScalarGridSpec(
        num_scalar_prefetch=2,
        grid=(t // tm,),
        in_specs=[pl.BlockSpec((tm, D_MODEL), lambda i, *_: (i, 0)),
                  pl.BlockSpec((tm, LANES), lambda i, *_: (i, 0)),
                  _stream_mod_spec(5, pt, req_tokens // tm, ctx_row),
                  pl.BlockSpec((1, D_MODEL), lambda i, *_: (0, 0)),
                  pl.BlockSpec(memory_space=pl.ANY)],
        out_specs=[pl.BlockSpec((tm, D_MODEL), lambda i, *_: (jnp.minimum(i, pt - 1), 0)),
                   pl.BlockSpec((tm, D_MODEL), lambda i, *_: (jnp.maximum(i - pt, 0), 0))],
        scratch_shapes=[pltpu.VMEM((2, 2, tm, D_MODEL), F32), pltpu.SemaphoreType.DMA((2,))],
    )
    return pl.pallas_call(
        functools.partial(_combine_kernel, prompt_tiles=pt),
        grid_spec=grid_spec,
        out_shape=[jax.ShapeDtypeStruct((prompt_tokens, D_MODEL), F32),
                   jax.ShapeDtypeStruct((t - prompt_tokens, D_MODEL), F32)],
        compiler_params=pltpu.CompilerParams(dimension_semantics=("arbitrary",),
                                             vmem_limit_bytes=VMEM_LIMIT_BYTES,
                                             disable_bounds_checks=True),
        name="moe_combine",
    )(dest0, dest1, x, info, mod_all, final_g.reshape(1, D_MODEL), ys_sorted)


def moe_layout(counts, info, n_tiles):
    nt_e = (counts + MOE_ROW_TILE - 1) // MOE_ROW_TILE
    ends = jnp.cumsum(nt_e)
    total = ends[-1]
    offset = (ends - nt_e) * MOE_ROW_TILE
    experts = jnp.arange(N_EXPERTS, dtype=jnp.int32)

    def dest(e_lane, r_lane):
        e = info[:, e_lane].astype(jnp.int32)
        off = jnp.sum(jnp.where(e[:, None] == experts[None, :], offset[None, :], 0), axis=1)
        return (off + info[:, r_lane].astype(jnp.int32)).astype(jnp.int32)

    fill = jnp.concatenate([offset + counts, ends * MOE_ROW_TILE, total[None]]).astype(jnp.int32)
    ids = jnp.arange(n_tiles, dtype=jnp.int32)
    ids_c = jnp.minimum(ids, total - 1)
    te = jnp.sum((ids_c[:, None] >= ends[None, :]).astype(jnp.int32), axis=1)
    return (dest(INFO_E0, INFO_R0), dest(INFO_E1, INFO_R1), fill, te.astype(jnp.int32),
            (ids < total).astype(jnp.int32))


def moe_final(x, g, mod_all, ctx_row, prompt_tokens, req_tokens, router_w, final_g, w_gate, w_up, w_down):
    t = x.shape[0]
    info, cnt = moe_route(x, g, mod_all, ctx_row, prompt_tokens, req_tokens, router_w)
    counts = cnt[0, :N_EXPERTS].astype(jnp.int32)
    n_tiles = (2 * t) // MOE_ROW_TILE + N_EXPERTS
    dest0, dest1, fill, te, tv = moe_layout(counts, info, n_tiles)
    xs_sorted = moe_dispatch(x, g, mod_all, ctx_row, prompt_tokens, req_tokens, dest0, dest1, fill, n_tiles)
    ys_sorted = moe_experts(xs_sorted, te, tv, w_gate, w_up, w_down)
    return moe_combine(x, info, ys_sorted, dest0, dest1, mod_all, ctx_row, prompt_tokens, req_tokens, final_g)


def kernel(x_prompt, x_sample, state_s5, cache_na_k, cache_na_v, cache_diff_k, cache_diff_v, c, c_ctx, w_mod, b_mod, norm_mix_g, norm_ffn_g, final_norm_g, w_in_e, w_out_e, s5_lam_re, s5_lam_im, s5_log_dt, s5_b_re, s5_b_im, s5_c_re, s5_c_im, s5_d, s5_w_glu, s5_b_glu, na_rpb, ffn_w_gate, ffn_w_up, ffn_w_down, w_in_o, w_out_o, diff_lam_q1, diff_lam_k1, diff_lam_q2, diff_lam_k2, diff_subln_g, conv_w, conv_b, conv_ln_g, conv_ln_b, router_w, moe_w_gate, moe_w_up, moe_w_down):
    bp, lp, d = x_prompt.shape
    bs, ls, _ = x_sample.shape
    tm = 1024
    xp = x_prompt.reshape(bp * lp, d)
    xs = x_sample.reshape(bs * ls, d)
    rows_p = (bp * lp) // tm
    rows_s = ls // tm

    cond8 = jnp.concatenate([c, c_ctx[None, :], jnp.zeros((SUBLANES - bs - 1, d), F32)], axis=0)
    mod = adaln_all(cond8, w_mod, b_mod)
    mod_s = mod[:, 0:bs, None, :]
    mod_p = mod[:, bs:bs + 1, None, :]

    def tiles(rows, tile):
        return rows * tm // tile

    bmat, cmat, lam8 = s5_params(s5_lam_re[0], s5_lam_im[0], s5_log_dt[0], s5_b_re[0], s5_b_im[0],
                                 s5_c_re[0], s5_c_im[0])
    bias = na_bias_table(na_rpb[0])
    n_e = w_in_e.shape[-1]

    proj_p = in_proj(xp, norm_mix_g[0], mod_p[0], rows_p, w_in_e[0])
    proj_s = in_proj(xs, norm_mix_g[0], mod_s[0], rows_s, w_in_e[0])

    y_p, st_p = s5_scan(proj_p.reshape(bp, lp, n_e), bmat, cmat, lam8, None, 1)
    chunks = SUBLANES // bs
    h0 = state_s5[:, 0].reshape(bs, 2, 2, S5_GROUPS * S5_STATE)
    y_s, _ = s5_scan(proj_s.reshape(bs * chunks, ls // chunks, n_e), bmat, cmat, lam8, h0, chunks)
    s5o_p = s5_glu(y_p.reshape(bp * lp, HALF_MIX), proj_p, s5_d[0], s5_w_glu[0], s5_b_glu[0])
    s5o_s = s5_glu(y_s.reshape(bs * ls, HALF_MIX), proj_s, s5_d[0], s5_w_glu[0], s5_b_glu[0])

    nao_p, na_k, na_v = na_ctx(proj_p.reshape(bp, lp, n_e))
    nao_s = na_lat(proj_s.reshape(bs, ls, n_e), cache_na_k[:, 0:1], cache_na_v[:, 0:1], bias)

    xp = mix_out(xp, s5o_p, nao_p.reshape(bp * lp, HALF_MIX), w_out_e[0], mod_p[0], tiles(rows_p, 512))
    xs = mix_out(xs, s5o_s, nao_s.reshape(bs * ls, HALF_MIX), w_out_e[0], mod_s[0], tiles(rows_s, 512))
    xp = ffn(xp, norm_ffn_g[0], mod_p[0], rows_p, ffn_w_gate[0], ffn_w_up[0], ffn_w_down[0])
    xs = ffn(xs, norm_ffn_g[0], mod_s[0], rows_s, ffn_w_gate[0], ffn_w_up[0], ffn_w_down[0])

    lam_init = 0.8 - 0.6 * math.exp(-0.3 * 1)
    lam = (jnp.exp(jnp.sum(diff_lam_q1[0].astype(F32) * diff_lam_k1[0].astype(F32)))
           - jnp.exp(jnp.sum(diff_lam_q2[0].astype(F32) * diff_lam_k2[0].astype(F32)))
           + lam_init)
    cos, sin = rope_tables(ls)
    n_o = w_in_o.shape[-1]

    proj_p = in_proj(xp, norm_mix_g[1], mod_p[1], rows_p, w_in_o[0])
    proj_s = in_proj(xs, norm_mix_g[1], mod_s[1], rows_s, w_in_o[0])

    do_p, diff_k, diff_v = diff_ctx(proj_p.reshape(bp, lp, n_o), lam, diff_subln_g[0], lam_init)
    do_s = diff_lat(proj_s.reshape(bs, ls, n_o), cache_diff_k[:, 0:1], cache_diff_v[:, 0:1], cos, sin,
                    lam, diff_subln_g[0], lam_init)
    co_p = conformer_conv(proj_p.reshape(bp, lp, n_o), conv_w[0], conv_b[0], conv_ln_g[0], conv_ln_b[0], lp)
    co_s = conformer_conv(proj_s.reshape(bs, ls, n_o), conv_w[0], conv_b[0], conv_ln_g[0], conv_ln_b[0], 512)

    mod_all = mod[1][:, None, :]
    x_all = mix_out2(xp, do_p.reshape(bp * lp, HALF_MIX), co_p.reshape(bp * lp, HALF_MIX),
                     xs, do_s.reshape(bs * ls, HALF_MIX), co_s.reshape(bs * ls, HALF_MIX),
                     w_out_o[0], mod_all, bs, ls)
    yp, ys = moe_final(x_all, norm_ffn_g[1], mod_all, bs, bp * lp, ls, router_w[0], final_norm_g,
                       moe_w_gate[0], moe_w_up[0], moe_w_down[0])

    new_state = st_p.reshape(bp, 1, 2, 2, S5_GROUPS, S5_STATE)
    return (yp.reshape(bp, lp, d), ys.reshape(bs, ls, d), new_state, na_k, na_v, diff_k, diff_v)
```

```python
import functools
import math

import jax
import jax.numpy as jnp
import numpy as np
from jax import lax
from jax.experimental import pallas as pl
from jax.experimental.pallas import tpu as pltpu

D_MODEL = 1024
DEPTH = 2
GRID_W = 64
HALF_MIX = 512
S5_GROUP_CH = 16
S5_GROUPS = 32
S5_STATE = 64
NA_HEAD_DIM = 64
NA_HEADS = 8
NA_WIN_R = 8
NA_WIN_C = 16
DIFF_D = 64
DIFF_HEAD_DIM = 128
DIFF_HEADS = 4
ROPE_BASE = 10000.0
CONV_WIDTH = 31
N_EXPERTS = 8
EPS = 1e-6

F32 = jnp.float32
BF16 = jnp.bfloat16
NEG_BIG = -1e30

VMEM_LIMIT_BYTES = 56 * 1024 * 1024
LANES = 128
SUBLANES = 8

S5_COL_GROUPS = 8
S5_COL_CH = S5_COL_GROUPS * S5_GROUP_CH
S5_COL_STATE = S5_COL_GROUPS * S5_STATE
S5_N_COL = S5_GROUPS // S5_COL_GROUPS
S5_TIME_BLOCK = 256


def _params(*sem):
    return pltpu.CompilerParams(dimension_semantics=sem, vmem_limit_bytes=VMEM_LIMIT_BYTES)


def _dot(a, b):
    return jnp.dot(a, b, preferred_element_type=F32)


def _dot_nt(a, b):
    return lax.dot_general(a, b, (((1,), (1,)), ((), ())), preferred_element_type=F32)


def _silu(x):
    return x * jax.nn.sigmoid(x)


def _norm_mod(x, g, shift, scale):
    y = x * lax.rsqrt(jnp.mean(x * x, axis=-1, keepdims=True) + EPS) * g
    return y * (1.0 + scale) + shift


def _mod_kernel(cond_ref, w_ref, b_ref, o_ref):
    s = _silu(cond_ref[...])
    o_ref[0] = jnp.dot(s, w_ref[0], precision=lax.Precision.HIGHEST,
                       preferred_element_type=F32) + b_ref[0]


def adaln_all(cond8, w_mod, b_mod):
    tn = 1536
    n = w_mod.shape[-1]
    return pl.pallas_call(
        _mod_kernel,
        grid=(DEPTH, n // tn),
        in_specs=[pl.BlockSpec((SUBLANES, D_MODEL), lambda l, j: (0, 0)),
                  pl.BlockSpec((1, D_MODEL, tn), lambda l, j: (l, 0, j)),
                  pl.BlockSpec((1, 1, tn), lambda l, j: (l, 0, j))],
        out_specs=pl.BlockSpec((1, SUBLANES, tn), lambda l, j: (l, 0, j)),
        out_shape=jax.ShapeDtypeStruct((DEPTH, SUBLANES, n), F32),
        compiler_params=_params("arbitrary", "arbitrary"),
        name="adaln_mod",
    )(cond8, w_mod, b_mod.reshape(DEPTH, 1, n))


def _mod_spec(chunk, tiles_per_row):
    return pl.BlockSpec((1, 1, D_MODEL), lambda i, *_: (i // tiles_per_row, 0, chunk))


def _in_proj_kernel(x_ref, g_ref, sh_ref, sc_ref, w_ref, o_ref, h_scr):
    @pl.when(pl.program_id(1) == 0)
    def _():
        h_scr[...] = _norm_mod(x_ref[...], g_ref[...], sh_ref[0], sc_ref[0]).astype(BF16)

    o_ref[...] = _dot(h_scr[...], w_ref[...].astype(BF16)).astype(o_ref.dtype)


def in_proj(x, g, mod, tiles_per_row, w, tm=1024, tn=512):
    t = x.shape[0]
    n = w.shape[1]
    return pl.pallas_call(
        _in_proj_kernel,
        grid=(t // tm, n // tn),
        in_specs=[pl.BlockSpec((tm, D_MODEL), lambda i, j: (i, 0)),
                  pl.BlockSpec((1, D_MODEL), lambda i, j: (0, 0)),
                  _mod_spec(0, tiles_per_row),
                  _mod_spec(1, tiles_per_row),
                  pl.BlockSpec((D_MODEL, tn), lambda i, j: (0, j))],
        out_specs=pl.BlockSpec((tm, tn), lambda i, j: (i, j)),
        out_shape=jax.ShapeDtypeStruct((t, n), F32),
        scratch_shapes=[pltpu.VMEM((tm, D_MODEL), BF16)],
        compiler_params=_params("arbitrary", "arbitrary"),
        name="in_proj",
    )(x, g.reshape(1, D_MODEL), mod, mod, w)


def _in_proj_split_kernel(x_ref, g_ref, sh_ref, sc_ref, w_ref, of_ref, ob_ref, h_scr, *, f32_tiles):
    j = pl.program_id(1)

    @pl.when(j == 0)
    def _():
        h_scr[...] = _norm_mod(x_ref[...], g_ref[...], sh_ref[0], sc_ref[0]).astype(BF16)

    y = _dot(h_scr[...], w_ref[...].astype(BF16))

    @pl.when(j < f32_tiles)
    def _():
        of_ref[...] = y

    @pl.when(j >= f32_tiles)
    def _():
        ob_ref[...] = y.astype(BF16)


def in_proj_split(x, g, mod, tiles_per_row, w, n_f32, tm=1024, tn=512):
    t = x.shape[0]
    n = w.shape[1]
    ft = n_f32 // tn
    return pl.pallas_call(
        functools.partial(_in_proj_split_kernel, f32_tiles=ft),
        grid=(t // tm, n // tn),
        in_specs=[pl.BlockSpec((tm, D_MODEL), lambda i, j: (i, 0)),
                  pl.BlockSpec((1, D_MODEL), lambda i, j: (0, 0)),
                  _mod_spec(0, tiles_per_row),
                  _mod_spec(1, tiles_per_row),
                  pl.BlockSpec((D_MODEL, tn), lambda i, j: (0, j))],
        out_specs=[pl.BlockSpec((tm, tn), lambda i, j: (i, jnp.minimum(j, ft - 1))),
                   pl.BlockSpec((tm, tn), lambda i, j: (i, jnp.maximum(j - ft, 0)))],
        out_shape=[jax.ShapeDtypeStruct((t, n_f32), F32), jax.ShapeDtypeStruct((t, n - n_f32), BF16)],
        scratch_shapes=[pltpu.VMEM((tm, D_MODEL), BF16)],
        compiler_params=_params("arbitrary", "arbitrary"),
        name="in_proj_split",
    )(x, g.reshape(1, D_MODEL), mod, mod, w)


def _s5_scan_kernel(*refs, seq, chunks, has_init):
    if has_init:
        u_ref, bm_ref, cm_ref, lam_ref, h0_ref, y_ref, st_ref, bu_scr, ytm_scr = refs
    else:
        u_ref, bm_ref, cm_ref, lam_ref, y_ref, st_ref, bu_scr, ytm_scr = refs
        h0_ref = None
    tb = S5_TIME_BLOCK
    n_tb = seq // tb
    ns = S5_COL_STATE
    row = lax.broadcasted_iota(jnp.int32, (SUBLANES, ns), 0)
    piece = row % chunks

    for d in range(2):
        lam = lam_ref[d, 0]
        lr, li = lam[:, :ns], lam[:, ns:]
        bm = bm_ref[d, 0]
        cm = cm_ref[d, 0]
        blocks = list(range(n_tb)) if d == 0 else list(range(n_tb - 1, -1, -1))

        def load_bu(k):
            ub = u_ref[:, k * tb:(k + 1) * tb, :]
            utm = jnp.swapaxes(ub, 0, 1).reshape(tb * SUBLANES, S5_COL_CH).astype(BF16)
            bu_scr[...] = _dot(utm, bm).reshape(tb, SUBLANES, 2 * ns)

        def scan_block(h, store):
            def step(s, carry):
                hr, hi = carry
                t = (tb - 1 - s) if d == 1 else s
                b = bu_scr[t]
                nr = lr * hr - li * hi + b[:, :ns]
                ni = lr * hi + li * hr + b[:, ns:]
                if store:
                    bu_scr[t, :, :ns] = nr
                    bu_scr[t, :, ns:] = ni
                return nr, ni
            return lax.fori_loop(0, tb, step, h, unroll=2)

        zero = jnp.zeros((SUBLANES, ns), F32)
        if chunks > 1:
            h = (zero, zero)
            for k in blocks:
                load_bu(k)
                h = scan_block(h, False)
            fr, fi = h
            pr, pi = lr, li
            for _ in range(int(math.log2(seq))):
                pr, pi = pr * pr - pi * pi, 2.0 * pr * pi
            edge = 0 if d == 0 else chunks - 1
            shift = 1 if d == 0 else SUBLANES - 1
            if has_init:
                h0r = h0_ref[:, d, 0, :]
                h0i = h0_ref[:, d, 1, :]
                seq_of_row = row // chunks
                er, ei = zero, zero
                for b in range(SUBLANES // chunks):
                    er = jnp.where(seq_of_row == b, h0r[b:b + 1, :], er)
                    ei = jnp.where(seq_of_row == b, h0i[b:b + 1, :], ei)
            else:
                er, ei = zero, zero
            is_edge = piece == edge
            cr = jnp.where(is_edge, er, zero)
            ci = jnp.where(is_edge, ei, zero)
            for _ in range(chunks - 1):
                tr = fr + pr * cr - pi * ci
                ti = fi + pr * ci + pi * cr
                cr = jnp.where(is_edge, er, pltpu.roll(tr, shift, 0))
                ci = jnp.where(is_edge, ei, pltpu.roll(ti, shift, 0))
            h = (cr, ci)
        else:
            if has_init:
                h = (h0_ref[:, d, 0, :], h0_ref[:, d, 1, :])
            else:
                h = (zero, zero)

        for k in blocks:
            load_bu(k)
            h = scan_block(h, True)
            hb = bu_scr[...].reshape(tb * SUBLANES, 2 * ns).astype(BF16)
            yb = _dot(hb, cm).reshape(tb, SUBLANES, S5_COL_CH)
            if d == 0:
                ytm_scr[k * tb:(k + 1) * tb] = yb
            else:
                ytm_scr[k * tb:(k + 1) * tb] += yb
        st_ref[:, d, 0, :] = h[0]
        st_ref[:, d, 1, :] = h[1]

    y_ref[...] = jnp.swapaxes(ytm_scr[...], 0, 1)


def s5_scan(proj3, bmat, cmat, lam8, h0, chunks):
    rows, seq, _ = proj3.shape
    ns = S5_COL_STATE
    has_init = h0 is not None
    in_specs = [pl.BlockSpec((SUBLANES, seq, S5_COL_CH), lambda i, c: (i, 0, c)),
                pl.BlockSpec((2, 1, S5_COL_CH, 2 * ns), lambda i, c: (0, c, 0, 0)),
                pl.BlockSpec((2, 1, 2 * ns, S5_COL_CH), lambda i, c: (0, c, 0, 0)),
                pl.BlockSpec((2, 1, SUBLANES, 2 * ns), lambda i, c: (0, c, 0, 0))]
    args = [proj3, bmat, cmat, lam8]
    if has_init:
        nb = h0.shape[0]
        in_specs.append(pl.BlockSpec((nb, 2, 2, ns), lambda i, c: (0, 0, 0, c)))
        args.append(h0)
    y, st = pl.pallas_call(
        functools.partial(_s5_scan_kernel, seq=seq, chunks=chunks, has_init=has_init),
        grid=(rows // SUBLANES, S5_N_COL),
        in_specs=in_specs,
        out_specs=[pl.BlockSpec((SUBLANES, seq, S5_COL_CH), lambda i, c: (i, 0, c)),
                   pl.BlockSpec((SUBLANES, 2, 2, ns), lambda i, c: (i, 0, 0, c))],
        out_shape=[jax.ShapeDtypeStruct((rows, seq, HALF_MIX), F32),
                   jax.ShapeDtypeStruct((rows, 2, 2, S5_GROUPS * S5_STATE), F32)],
        scratch_shapes=[pltpu.VMEM((S5_TIME_BLOCK, SUBLANES, 2 * ns), F32),
                        pltpu.VMEM((seq, SUBLANES, S5_COL_CH), F32)],
        compiler_params=_params("arbitrary", "arbitrary"),
        name="s5_scan",
    )(*args)
    return y, st


def _s5_glu_kernel(y_ref, u_ref, d_ref, w_ref, b_ref, o_ref):
    y = u_ref[...] * d_ref[...] + y_ref[...]
    z = jax.nn.gelu(y)
    gate = _dot(z.astype(BF16), w_ref[...].astype(BF16)) + b_ref[...]
    o_ref[...] = (z * jax.nn.sigmoid(gate)).astype(o_ref.dtype)


def s5_glu(y, proj, d_skip, w_glu, b_glu, tm=1024):
    t = y.shape[0]
    return pl.pallas_call(
        _s5_glu_kernel,
        grid=(t // tm,),
        in_specs=[pl.BlockSpec((tm, HALF_MIX), lambda i: (i, 0)),
                  pl.BlockSpec((tm, HALF_MIX), lambda i: (i, 0)),
                  pl.BlockSpec((1, HALF_MIX), lambda i: (0, 0)),
                  pl.BlockSpec((HALF_MIX, HALF_MIX), lambda i: (0, 0)),
                  pl.BlockSpec((1, HALF_MIX), lambda i: (0, 0))],
        out_specs=pl.BlockSpec((tm, HALF_MIX), lambda i: (i, 0)),
        out_shape=jax.ShapeDtypeStruct((t, HALF_MIX), BF16),
        compiler_params=_params("arbitrary"),
        name="s5_glu",
    )(y, proj, d_skip.reshape(1, HALF_MIX), w_glu, b_glu.reshape(1, HALF_MIX))


def s5_params(lam_re, lam_im, log_dt, b_re, b_im, c_re, c_im):
    lr = lam_re.astype(F32)
    li = lam_im.astype(F32)
    dt = jnp.exp(log_dt.astype(F32))[..., None]
    mag = jnp.exp(lr * dt)
    bar_re = mag * jnp.cos(li * dt)
    bar_im = mag * jnp.sin(li * dt)
    den = lr * lr + li * li
    q_re = ((bar_re - 1.0) * lr + bar_im * li) / den
    q_im = (bar_im * lr - (bar_re - 1.0) * li) / den
    br = b_re.astype(F32)
    bi = b_im.astype(F32)
    b_bar_re = q_re[..., None] * br - q_im[..., None] * bi
    b_bar_im = q_re[..., None] * bi + q_im[..., None] * br
    eye = jnp.eye(S5_COL_GROUPS, dtype=F32)

    def block_diag_b(m):
        m = m.reshape(2, S5_N_COL, S5_COL_GROUPS, S5_STATE, S5_GROUP_CH)
        bd = jnp.einsum('dngpc,gh->dngchp', m, eye)
        return bd.reshape(2, S5_N_COL, S5_COL_CH, S5_COL_STATE)

    def block_diag_c(m):
        m = m.reshape(2, S5_N_COL, S5_COL_GROUPS, S5_GROUP_CH, S5_STATE)
        bd = jnp.einsum('dngcp,gh->dngphc', m, eye)
        return bd.reshape(2, S5_N_COL, S5_COL_STATE, S5_COL_CH)

    bmat = jnp.concatenate([block_diag_b(b_bar_re), block_diag_b(b_bar_im)], axis=-1).astype(BF16)
    cmat = jnp.concatenate([block_diag_c(c_re.astype(F32)), block_diag_c(-c_im.astype(F32))],
                           axis=-2).astype(BF16)
    lam_cat = jnp.concatenate([bar_re.reshape(2, S5_N_COL, S5_COL_STATE),
                               bar_im.reshape(2, S5_N_COL, S5_COL_STATE)], axis=-1)
    lam8 = jnp.broadcast_to(lam_cat[:, :, None, :], (2, S5_N_COL, SUBLANES, 2 * S5_COL_STATE))
    return bmat, cmat, lam8


def _na_ctx_kernel(q_ref, k_ref, v_ref, o_ref, ko_ref, vo_ref):
    scale = NA_HEAD_DIM ** -0.5
    q = q_ref[0]
    k = k_ref[0]
    v = v_ref[0]
    outs = []
    for h in range(NA_HEADS):
        sl = slice(h * NA_HEAD_DIM, (h + 1) * NA_HEAD_DIM)
        kh = k[:, sl]
        vh = v[:, sl]
        ko_ref[0, 0, h] = kh
        vo_ref[0, 0, h] = vh
        s = _dot_nt(q[:, sl].astype(BF16), kh.astype(BF16)) * scale
        p = jnp.exp(s - jnp.max(s, axis=-1, keepdims=True))
        p = p / jnp.sum(p, axis=-1, keepdims=True)
        outs.append(_dot(p.astype(BF16), vh.astype(BF16)))
    o_ref[0] = jnp.concatenate(outs, axis=-1).astype(o_ref.dtype)


def na_ctx(proj3):
    b, seq, _ = proj3.shape
    cache_shape = jax.ShapeDtypeStruct((b, 1, NA_HEADS, seq, NA_HEAD_DIM), F32)
    cache_spec = pl.BlockSpec((1, 1, NA_HEADS, seq, NA_HEAD_DIM), lambda i: (i, 0, 0, 0, 0))
    return pl.pallas_call(
        _na_ctx_kernel,
        grid=(b,),
        in_specs=[pl.BlockSpec((1, seq, HALF_MIX), lambda i: (i, 0, 1)),
                  pl.BlockSpec((1, seq, HALF_MIX), lambda i: (i, 0, 2)),
                  pl.BlockSpec((1, seq, HALF_MIX), lambda i: (i, 0, 3))],
        out_specs=[pl.BlockSpec((1, seq, HALF_MIX), lambda i: (i, 0, 0)), cache_spec, cache_spec],
        out_shape=[jax.ShapeDtypeStruct((b, seq, HALF_MIX), BF16), cache_shape, cache_shape],
        compiler_params=_params("arbitrary"),
        name="na_ctx",
    )(proj3, proj3, proj3)


NA_Q_ROWS = 4
NA_KEY_ROWS = 12


def na_bias_blocks(rpb):
    qcol = np.arange(GRID_W)
    cc = np.arange(GRID_W)
    cs = np.clip(qcol - NA_WIN_C // 2, 0, GRID_W - NA_WIN_C)
    valid = (cc[None, :] >= cs[:, None]) & (cc[None, :] < cs[:, None] + NA_WIN_C)
    coff = cc[None, :] - qcol[:, None] + (NA_WIN_C - 1)
    n_col = 2 * NA_WIN_C - 1
    sel = ((coff[None] == np.arange(n_col)[:, None, None]) & valid[None]).astype(np.float32)
    sel = sel.reshape(n_col, GRID_W * GRID_W)
    mask = np.where(valid, 0.0, NEG_BIG).astype(np.float32).reshape(1, GRID_W * GRID_W)
    n_row = 2 * NA_WIN_R - 1
    t1 = jnp.dot(rpb.astype(F32).reshape(NA_HEADS * n_row, n_col), jnp.asarray(sel),
                 precision=lax.Precision.HIGHEST) + jnp.asarray(mask)
    t1 = t1.reshape(NA_HEADS, n_row, GRID_W, GRID_W)
    neg = jnp.full((NA_HEADS, GRID_W, GRID_W), NEG_BIG, F32)
    variants = ((lambda ri: 0, NA_WIN_R - 1), (lambda ri: ri, NA_WIN_R // 2 - 1),
                (lambda ri: NA_KEY_ROWS - NA_WIN_R, -1))
    out = []
    for lo_of, shift in variants:
        per_ri = []
        for ri in range(NA_Q_ROWS):
            lo = lo_of(ri)
            blocks = [t1[:, wr - ri + shift] if lo <= wr < lo + NA_WIN_R else neg for wr in range(NA_KEY_ROWS)]
            per_ri.append(jnp.concatenate(blocks, axis=-1))
        out.append(jnp.stack(per_ri, axis=1).reshape(NA_HEADS, NA_Q_ROWS * GRID_W, NA_KEY_ROWS * GRID_W))
    return jnp.stack(out, axis=0)


def _na_lat_kernel(q_ref, k_ref, v_ref, kc_ref, vc_ref, bias_ref, o_ref):
    qb = pl.program_id(1)
    rows = k_ref.shape[1] // GRID_W
    nk = NA_KEY_ROWS * GRID_W
    first_row = jnp.clip(qb * NA_Q_ROWS - NA_WIN_R // 2, 0, rows - NA_KEY_ROWS)
    start = pl.multiple_of(first_row * GRID_W, GRID_W)
    tq = q_ref.shape[1]
    lane = lax.broadcasted_iota(jnp.int32, (tq, LANES), 1)
    low = lane < NA_HEAD_DIM
    outs = []
    for pr in range(NA_HEADS // 2):
        cols = slice(pr * LANES, (pr + 1) * LANES)
        qp = q_ref[0, :, cols].astype(F32) * (NA_HEAD_DIM ** -0.5)
        kw = k_ref[0, pl.ds(start, nk), cols]
        vw = v_ref[0, pl.ds(start, nk), cols]
        kc = kc_ref[0, :, cols]
        vc = vc_ref[0, :, cols]
        o_pair = None
        for half in range(2):
            qm = jnp.where(low if half == 0 else jnp.logical_not(low), qp, 0.0).astype(BF16)
            s_loc = _dot_nt(qm, kw) + bias_ref[0, 2 * pr + half]
            s_ctx = _dot_nt(qm, kc)
            m = jnp.maximum(jnp.max(s_loc, axis=-1, keepdims=True), jnp.max(s_ctx, axis=-1, keepdims=True))
            p_loc = jnp.exp(s_loc - m)
            p_ctx = jnp.exp(s_ctx - m)
            inv = 1.0 / (jnp.sum(p_loc, axis=-1, keepdims=True) + jnp.sum(p_ctx, axis=-1, keepdims=True))
            o = (_dot(p_loc.astype(BF16), vw) + _dot(p_ctx.astype(BF16), vc)) * inv
            o_pair = o if half == 0 else jnp.where(low, o_pair, o)
        outs.append(o_pair)
    o_ref[0] = jnp.concatenate(outs, axis=-1).astype(o_ref.dtype)


def na_lat(qkv3, k_ctx, v_ctx, bias):
    b, seq, _ = qkv3.shape
    tq = NA_Q_ROWS * GRID_W
    n_q = seq // tq
    lc = k_ctx.shape[1]
    ctx_spec = pl.BlockSpec((1, lc, HALF_MIX), lambda i, r: (i, 0, 0))
    return pl.pallas_call(
        _na_lat_kernel,
        grid=(b, n_q),
        in_specs=[pl.BlockSpec((1, tq, HALF_MIX), lambda i, r: (i, r, 0)),
                  pl.BlockSpec((1, seq, HALF_MIX), lambda i, r: (i, 0, 1)),
                  pl.BlockSpec((1, seq, HALF_MIX), lambda i, r: (i, 0, 2)),
                  ctx_spec, ctx_spec,
                  pl.BlockSpec((1, NA_HEADS, tq, NA_KEY_ROWS * GRID_W),
                               lambda i, r: (jnp.where(r == 0, 0, jnp.where(r == n_q - 1, 2, 1)), 0, 0, 0))],
        out_specs=pl.BlockSpec((1, tq, HALF_MIX), lambda i, r: (i, r, 0)),
        out_shape=jax.ShapeDtypeStruct((b, seq, HALF_MIX), BF16),
        compiler_params=_params("arbitrary", "arbitrary"),
        name="na_lat",
    )(qkv3, qkv3, qkv3, k_ctx, v_ctx, bias)


def _softmax_pair_diff(s1, s2, lam):
    p1 = jnp.exp(s1 - jnp.max(s1, axis=-1, keepdims=True))
    p2 = jnp.exp(s2 - jnp.max(s2, axis=-1, keepdims=True))
    inv1 = 1.0 / jnp.sum(p1, axis=-1, keepdims=True)
    inv2 = lam / jnp.sum(p2, axis=-1, keepdims=True)
    return p1 * inv1 - p2 * inv2


def _sub_ln(o, g, lam_init):
    return o * lax.rsqrt(jnp.mean(o * o, axis=-1, keepdims=True) + EPS) * g * (1.0 - lam_init)


def _diff_ctx_kernel(lam_ref, q_ref, k_ref, v_ref, g_ref, o_ref, ko_ref, vo_ref, *, lam_init):
    scale = DIFF_D ** -0.5
    lam = lam_ref[0, 0]
    lane = lax.broadcasted_iota(jnp.int32, (q_ref.shape[1], DIFF_HEAD_DIM), 1)
    first = lane < DIFF_D
    for h in range(DIFF_HEADS):
        sl = slice(h * DIFF_HEAD_DIM, (h + 1) * DIFF_HEAD_DIM)
        qh = q_ref[0, :, sl]
        kh = k_ref[0, :, sl]
        vh = v_ref[0, :, sl]
        ko_ref[0, 0, h] = kh
        vo_ref[0, 0, h] = vh
        kb = kh.astype(BF16)
        s1 = _dot_nt(jnp.where(first, qh, 0.0).astype(BF16), kb) * scale
        s2 = _dot_nt(jnp.where(first, 0.0, qh).astype(BF16), kb) * scale
        a = _softmax_pair_diff(s1, s2, lam)
        o = _dot(a.astype(BF16), vh.astype(BF16))
        o_ref[0, :, sl] = _sub_ln(o, g_ref[...], lam_init).astype(o_ref.dtype)


def diff_ctx(proj3, lam, subln_g, lam_init):
    b, seq, _ = proj3.shape
    cache_shape = jax.ShapeDtypeStruct((b, 1, DIFF_HEADS, seq, DIFF_HEAD_DIM), F32)
    cache_spec = pl.BlockSpec((1, 1, DIFF_HEADS, seq, DIFF_HEAD_DIM), lambda i: (i, 0, 0, 0, 0))
    return pl.pallas_call(
        functools.partial(_diff_ctx_kernel, lam_init=lam_init),
        grid=(b,),
        in_specs=[pl.BlockSpec(memory_space=pltpu.SMEM),
                  pl.BlockSpec((1, seq, HALF_MIX), lambda i: (i, 0, 0)),
                  pl.BlockSpec((1, seq, HALF_MIX), lambda i: (i, 0, 1)),
                  pl.BlockSpec((1, seq, HALF_MIX), lambda i: (i, 0, 2)),
                  pl.BlockSpec((1, DIFF_HEAD_DIM), lambda i: (0, 0))],
        out_specs=[pl.BlockSpec((1, seq, HALF_MIX), lambda i: (i, 0, 0)), cache_spec, cache_spec],
        out_shape=[jax.ShapeDtypeStruct((b, seq, HALF_MIX), BF16), cache_shape, cache_shape],
        compiler_params=_params("arbitrary"),
        name="diff_ctx",
    )(lam.reshape(1, 1), proj3, proj3, proj3, subln_g.reshape(1, DIFF_HEAD_DIM))


def rope_tables(seq):
    t = jnp.arange(seq)
    row = (t // GRID_W).astype(F32)
    col = (t % GRID_W).astype(F32)
    n_freq = DIFF_D // 4
    inv = ROPE_BASE ** (-jnp.arange(n_freq, dtype=F32) / n_freq)
    ang = jnp.concatenate([row[:, None] * inv, col[:, None] * inv], axis=-1)
    cos = jnp.repeat(jnp.cos(ang), 2, axis=-1)
    sin = jnp.repeat(jnp.sin(ang), 2, axis=-1)
    sign = jnp.where(jnp.arange(DIFF_D) % 2 == 0, -1.0, 1.0).astype(F32)
    sin = sin * sign
    return jnp.tile(cos, (1, 2)), jnp.tile(sin, (1, 2))


def _rope(x, cos, sin_signed):
    lane = lax.broadcasted_iota(jnp.int32, x.shape, 1)
    nxt = pltpu.roll(x, x.shape[1] - 1, 1)
    prv = pltpu.roll(x, 1, 1)
    partner = jnp.where(lane % 2 == 0, nxt, prv)
    return x * cos + partner * sin_signed


def _diff_lat_kernel(lam_ref, q_ref, k_ref, v_ref, kc_ref, vc_ref, cq_ref, sq_ref, ck_ref, sk_ref,
                     g_ref, o_ref, k_all, v_all, *, lam_init):
    seq = k_ref.shape[1]

    @pl.when(pl.program_id(2) == 0)
    def _():
        k_all[0:seq, :] = _rope(k_ref[0], ck_ref[...], sk_ref[...]).astype(BF16)
        k_all[seq:, :] = kc_ref[0, 0, 0].astype(BF16)
        v_all[0:seq, :] = v_ref[0].astype(BF16)
        v_all[seq:, :] = vc_ref[0, 0, 0].astype(BF16)

    lam = lam_ref[0, 0]
    q = _rope(q_ref[0], cq_ref[...], sq_ref[...]) * (DIFF_D ** -0.5)
    lane = lax.broadcasted_iota(jnp.int32, q.shape, 1)
    first = lane < DIFF_D
    kb = k_all[...]
    s1 = _dot_nt(jnp.where(first, q, 0.0).astype(BF16), kb)
    s2 = _dot_nt(jnp.where(first, 0.0, q).astype(BF16), kb)
    a = _softmax_pair_diff(s1, s2, lam)
    o = _dot(a.astype(BF16), v_all[...])
    o_ref[0] = _sub_ln(o, g_ref[...], lam_init).astype(o_ref.dtype)


def diff_lat(proj3, k_ctx, v_ctx, cos, sin, lam, subln_g, lam_init, tq=256):
    b, seq, _ = proj3.shape
    lc = k_ctx.shape[3]
    hd = DIFF_HEAD_DIM
    ctx_spec = pl.BlockSpec((1, 1, 1, lc, hd), lambda i, h, q: (i, 0, h, 0, 0))
    tq_spec = pl.BlockSpec((tq, hd), lambda i, h, q: (q, 0))
    full_spec = pl.BlockSpec((seq, hd), lambda i, h, q: (0, 0))
    return pl.pallas_call(
        functools.partial(_diff_lat_kernel, lam_init=lam_init),
        grid=(b, DIFF_HEADS, seq // tq),
        in_specs=[pl.BlockSpec(memory_space=pltpu.SMEM),
                  pl.BlockSpec((1, tq, hd), lambda i, h, q: (i, q, h)),
                  pl.BlockSpec((1, seq, hd), lambda i, h, q: (i, 0, DIFF_HEADS + h)),
                  pl.BlockSpec((1, seq, hd), lambda i, h, q: (i, 0, 2 * DIFF_HEADS + h)),
                  ctx_spec, ctx_spec, tq_spec, tq_spec, full_spec, full_spec,
                  pl.BlockSpec((1, hd), lambda i, h, q: (0, 0))],
        out_specs=pl.BlockSpec((1, tq, hd), lambda i, h, q: (i, q, h)),
        out_shape=jax.ShapeDtypeStruct((b, seq, HALF_MIX), BF16),
        scratch_shapes=[pltpu.VMEM((seq + lc, hd), BF16), pltpu.VMEM((seq + lc, hd), BF16)],
        compiler_params=_params("arbitrary", "arbitrary", "arbitrary"),
        name="diff_lat",
    )(lam.reshape(1, 1), proj3, proj3, proj3, k_ctx, v_ctx, cos, sin, cos, sin,
      subln_g.reshape(1, hd))


CONV_PAD = 16
CONV_SUB = 64


def _conv_kernel(a_ref, g_ref, ap_ref, gp_ref, an_ref, gn_ref, w_ref, b_ref, lg_ref, lb_ref, o_ref, xp_scr,
                 xsh_scr):
    t = pl.program_id(1)
    tt = a_ref.shape[1]
    prev = ap_ref[0] * jax.nn.sigmoid(gp_ref[0])
    nxt = an_ref[0] * jax.nn.sigmoid(gn_ref[0])
    xp_scr[0:CONV_PAD, :] = jnp.where(t > 0, prev, 0.0)
    xp_scr[CONV_PAD + tt:, :] = jnp.where(t < pl.num_programs(1) - 1, nxt, 0.0)
    xp_scr[CONV_PAD:CONV_PAD + tt, :] = a_ref[0] * jax.nn.sigmoid(g_ref[0])
    first_tap = CONV_PAD - CONV_WIDTH // 2
    n_rows = xsh_scr.shape[1]
    for b in range(SUBLANES):
        xsh_scr[b] = xp_scr[b:b + n_rows, :]
    for i in range(tt // CONV_SUB):
        s = i * CONV_SUB
        acc = jnp.zeros((CONV_SUB, HALF_MIX), F32)
        for j in range(CONV_WIDTH):
            whole, phase = divmod(first_tap + j, SUBLANES)
            lo = s + whole * SUBLANES
            acc = acc + xsh_scr[phase, lo:lo + CONV_SUB, :] * w_ref[j:j + 1, :]
        y = acc + b_ref[...]
        mu = jnp.mean(y, axis=-1, keepdims=True)
        yc = y - mu
        var = jnp.mean(yc * yc, axis=-1, keepdims=True)
        yn = yc * lax.rsqrt(var + EPS) * lg_ref[...] + lb_ref[...]
        o_ref[0, s:s + CONV_SUB, :] = _silu(yn).astype(o_ref.dtype)


def conformer_conv(proj3, w, b, ln_g, ln_b, tt):
    bsz, seq, _ = proj3.shape
    n_t = seq // tt
    hb = tt // CONV_PAD
    last = seq // CONV_PAD - 1
    vec = pl.BlockSpec((1, HALF_MIX), lambda i, t: (0, 0))

    def main(col):
        return pl.BlockSpec((1, tt, HALF_MIX), lambda i, t: (i, t, col))

    def prev(col):
        return pl.BlockSpec((1, CONV_PAD, HALF_MIX), lambda i, t: (i, jnp.maximum(t * hb - 1, 0), col))

    def nxt(col):
        return pl.BlockSpec((1, CONV_PAD, HALF_MIX), lambda i, t: (i, jnp.minimum((t + 1) * hb, last), col))

    return pl.pallas_call(
        _conv_kernel,
        grid=(bsz, n_t),
        in_specs=[main(3), main(4), prev(3), prev(4), nxt(3), nxt(4),
                  pl.BlockSpec((CONV_WIDTH, HALF_MIX), lambda i, t: (0, 0)),
                  vec, vec, vec],
        out_specs=pl.BlockSpec((1, tt, HALF_MIX), lambda i, t: (i, t, 0)),
        out_shape=jax.ShapeDtypeStruct((bsz, seq, HALF_MIX), BF16),
        scratch_shapes=[pltpu.VMEM((tt + 2 * CONV_PAD, HALF_MIX), F32),
                        pltpu.VMEM((SUBLANES, tt + 2 * CONV_PAD - SUBLANES, HALF_MIX), F32)],
        compiler_params=_params("arbitrary", "arbitrary"),
        name="conformer_conv",
    )(proj3, proj3, proj3, proj3, proj3, proj3, w, b.reshape(1, HALF_MIX), ln_g.reshape(1, HALF_MIX),
      ln_b.reshape(1, HALF_MIX))


def _mix_out_kernel(x_ref, m1_ref, m2_ref, w1_ref, w2_ref, gate_ref, o_ref):
    acc = _dot(m1_ref[...], w1_ref[...].astype(BF16)) + _dot(m2_ref[...], w2_ref[...].astype(BF16))
    o_ref[...] = x_ref[...] + gate_ref[0] * acc


def mix_out(x, m1, m2, w_out, mod, tiles_per_row, tm=512):
    t = x.shape[0]
    return pl.pallas_call(
        _mix_out_kernel,
        grid=(t // tm,),
        in_specs=[pl.BlockSpec((tm, D_MODEL), lambda i: (i, 0)),
                  pl.BlockSpec((tm, HALF_MIX), lambda i: (i, 0)),
                  pl.BlockSpec((tm, HALF_MIX), lambda i: (i, 0)),
                  pl.BlockSpec((HALF_MIX, D_MODEL), lambda i: (0, 0)),
                  pl.BlockSpec((HALF_MIX, D_MODEL), lambda i: (1, 0)),
                  _mod_spec(2, tiles_per_row)],
        out_specs=pl.BlockSpec((tm, D_MODEL), lambda i: (i, 0)),
        out_shape=jax.ShapeDtypeStruct((t, D_MODEL), F32),
        compiler_params=_params("arbitrary"),
        name="mix_out",
    )(x, m1, m2, w_out, w_out, mod)


def _ffn_kernel(x_ref, g_ref, sh_ref, sc_ref, gate_ref, wg_ref, wu_ref, wd_ref, o_ref, h_scr, acc_scr):
    f = pl.program_id(1)

    @pl.when(f == 0)
    def _():
        h_scr[...] = _norm_mod(x_ref[...], g_ref[...], sh_ref[0], sc_ref[0]).astype(BF16)
        acc_scr[...] = jnp.zeros_like(acc_scr)

    h = h_scr[...]
    a = _dot(h, wg_ref[...].astype(BF16))
    u = _dot(h, wu_ref[...].astype(BF16))
    acc_scr[...] += _dot((_silu(a) * u).astype(BF16), wd_ref[...].astype(BF16))

    @pl.when(f == pl.num_programs(1) - 1)
    def _():
        o_ref[...] = x_ref[...] + gate_ref[0] * acc_scr[...]


def ffn(x, g, mod, tiles_per_row, w_gate, w_up, w_down, tm=1024, tf=256):
    t = x.shape[0]
    fdim = w_gate.shape[1]
    return pl.pallas_call(
        _ffn_kernel,
        grid=(t // tm, fdim // tf),
        in_specs=[pl.BlockSpec((tm, D_MODEL), lambda i, f: (i, 0)),
                  pl.BlockSpec((1, D_MODEL), lambda i, f: (0, 0)),
                  _mod_spec(3, tiles_per_row), _mod_spec(4, tiles_per_row), _mod_spec(5, tiles_per_row),
                  pl.BlockSpec((D_MODEL, tf), lambda i, f: (0, f)),
                  pl.BlockSpec((D_MODEL, tf), lambda i, f: (0, f)),
                  pl.BlockSpec((tf, D_MODEL), lambda i, f: (f, 0))],
        out_specs=pl.BlockSpec((tm, D_MODEL), lambda i, f: (i, 0)),
        out_shape=jax.ShapeDtypeStruct((t, D_MODEL), F32),
        scratch_shapes=[pltpu.VMEM((tm, D_MODEL), BF16), pltpu.VMEM((tm, D_MODEL), F32)],
        compiler_params=_params("arbitrary", "arbitrary"),
        name="ffn",
    )(x, g.reshape(1, D_MODEL), mod, mod, mod, w_gate, w_up, w_down)


MOE_ROW_TILE = 1024
MOE_ZERO_ROWS = 256
INFO_E0, INFO_E1, INFO_G0, INFO_G1, INFO_R0, INFO_R1 = range(6)


def _stream_mod_spec(chunk, prompt_tiles, tiles_per_req, ctx_row):
    def index(i, *_):
        return (jnp.where(i < prompt_tiles, ctx_row, (i - prompt_tiles) // tiles_per_req), 0, chunk)
    return pl.BlockSpec((1, 1, D_MODEL), index)


def _mix_out2_kernel(xp_ref, m1p_ref, m2p_ref, xs_ref, m1s_ref, m2s_ref, w1_ref, w2_ref, gate_ref, o_ref,
                     *, prompt_tiles):
    i = pl.program_id(0)
    w1 = w1_ref[...].astype(BF16)
    w2 = w2_ref[...].astype(BF16)

    @pl.when(i < prompt_tiles)
    def _():
        o_ref[...] = xp_ref[...] + gate_ref[0] * (_dot(m1p_ref[...], w1) + _dot(m2p_ref[...], w2))

    @pl.when(i >= prompt_tiles)
    def _():
        o_ref[...] = xs_ref[...] + gate_ref[0] * (_dot(m1s_ref[...], w1) + _dot(m2s_ref[...], w2))


def mix_out2(xp, m1p, m2p, xs, m1s, m2s, w_out, mod_all, ctx_row, req_tokens, tm=512):
    tp, ts = xp.shape[0], xs.shape[0]
    pt = tp // tm

    def p_spec(width):
        return pl.BlockSpec((tm, width), lambda i: (jnp.minimum(i, pt - 1), 0))

    def s_spec(width):
        return pl.BlockSpec((tm, width), lambda i: (jnp.maximum(i - pt, 0), 0))

    return pl.pallas_call(
        functools.partial(_mix_out2_kernel, prompt_tiles=pt),
        grid=((tp + ts) // tm,),
        in_specs=[p_spec(D_MODEL), p_spec(HALF_MIX), p_spec(HALF_MIX),
                  s_spec(D_MODEL), s_spec(HALF_MIX), s_spec(HALF_MIX),
                  pl.BlockSpec((HALF_MIX, D_MODEL), lambda i: (0, 0)),
                  pl.BlockSpec((HALF_MIX, D_MODEL), lambda i: (1, 0)),
                  _stream_mod_spec(2, pt, req_tokens // tm, ctx_row)],
        out_specs=pl.BlockSpec((tm, D_MODEL), lambda i: (i, 0)),
        out_shape=jax.ShapeDtypeStruct((tp + ts, D_MODEL), F32),
        compiler_params=_params("arbitrary"),
        name="mix_out2",
    )(xp, m1p, m2p, xs, m1s, m2s, w_out, w_out, mod_all)


def _route_kernel(x_ref, g_ref, sh_ref, sc_ref, rw_ref, info_ref, cnt_ref, tri_scr, run_scr):
    i = pl.program_id(0)
    tm = x_ref.shape[0]

    @pl.when(i == 0)
    def _():
        r = lax.broadcasted_iota(jnp.int32, (tm, tm), 0)
        c = lax.broadcasted_iota(jnp.int32, (tm, tm), 1)
        tri_scr[...] = jnp.where(c < r, 1.0, 0.0).astype(BF16)
        run_scr[...] = jnp.zeros_like(run_scr)

    h = _norm_mod(x_ref[...], g_ref[...], sh_ref[0], sc_ref[0])
    logits = jnp.dot(h, rw_ref[...], precision=lax.Precision.HIGHEST, preferred_element_type=F32)
    lane = lax.broadcasted_iota(jnp.int32, logits.shape, 1).astype(F32)
    logits = jnp.where(lane < N_EXPERTS, logits, -jnp.inf)
    m1 = jnp.max(logits, axis=-1, keepdims=True)
    i1 = jnp.min(jnp.where(logits == m1, lane, float(LANES)), axis=-1, keepdims=True)
    rest = jnp.where(lane == i1, -jnp.inf, logits)
    m2 = jnp.max(rest, axis=-1, keepdims=True)
    i2 = jnp.min(jnp.where(rest == m2, lane, float(LANES)), axis=-1, keepdims=True)
    e2 = jnp.exp(m2 - m1)
    den = 1.0 + e2
    hit = jnp.where(lane == i1, 1.0, 0.0) + jnp.where(lane == i2, 1.0, 0.0)
    before = _dot(tri_scr[...], hit.astype(BF16)) + run_scr[0:1, :]
    r1 = jnp.sum(jnp.where(lane == i1, before, 0.0), axis=-1, keepdims=True)
    r2 = jnp.sum(jnp.where(lane == i2, before, 0.0), axis=-1, keepdims=True)
    info = jnp.zeros_like(logits)
    for slot, val in ((INFO_E0, i1), (INFO_E1, i2), (INFO_G0, 1.0 / den), (INFO_G1, e2 / den),
                      (INFO_R0, r1), (INFO_R1, r2)):
        info = jnp.where(lane == float(slot), val, info)
    info_ref[...] = info
    run_scr[...] = run_scr[...] + jnp.sum(hit, axis=0, keepdims=True)
    cnt_ref[...] = run_scr[...]


def moe_route(x, g, mod_all, ctx_row, prompt_tokens, req_tokens, router_w, tm=1024):
    t = x.shape[0]
    rw = jnp.pad(router_w, ((0, 0), (0, LANES - N_EXPERTS)))
    pt = prompt_tokens // tm
    return pl.pallas_call(
        _route_kernel,
        grid=(t // tm,),
        in_specs=[pl.BlockSpec((tm, D_MODEL), lambda i: (i, 0)),
                  pl.BlockSpec((1, D_MODEL), lambda i: (0, 0)),
                  _stream_mod_spec(3, pt, req_tokens // tm, ctx_row),
                  _stream_mod_spec(4, pt, req_tokens // tm, ctx_row),
                  pl.BlockSpec((D_MODEL, LANES), lambda i: (0, 0))],
        out_specs=[pl.BlockSpec((tm, LANES), lambda i: (i, 0)),
                   pl.BlockSpec((SUBLANES, LANES), lambda i: (0, 0))],
        out_shape=[jax.ShapeDtypeStruct((t, LANES), F32), jax.ShapeDtypeStruct((SUBLANES, LANES), F32)],
        scratch_shapes=[pltpu.VMEM((tm, tm), BF16), pltpu.VMEM((SUBLANES, LANES), F32)],
        compiler_params=_params("arbitrary"),
        name="moe_route",
    )(x, g.reshape(1, D_MODEL), mod_all, mod_all, rw)


def _dispatch_kernel(d0_ref, d1_ref, fill_ref, x_ref, g_ref, sh_ref, sc_ref, xs_hbm, h_scr, zero_scr, sem, zsem):
    i = pl.program_id(0)
    n = pl.num_programs(0)
    tm = x_ref.shape[0]
    slot = i % 2

    def row_copy(r, dst, s):
        return pltpu.make_async_copy(h_scr.at[s, pl.ds(r, 1), :], xs_hbm.at[pl.ds(dst, 1), :], sem.at[s])

    def wait_rows(s):
        for _ in range(2):
            pltpu.make_async_copy(h_scr.at[s], xs_hbm.at[pl.ds(0, tm), :], sem.at[s]).wait()

    @pl.when(i >= 2)
    def _():
        wait_rows(slot)

    h_scr[slot] = _norm_mod(x_ref[...], g_ref[...], sh_ref[0], sc_ref[0])
    base = i * tm

    def body(r8, c):
        rb = pl.multiple_of(r8 * SUBLANES, SUBLANES)
        for k in range(SUBLANES):
            row_copy(rb + k, d0_ref[base + rb + k], slot).start()
            row_copy(rb + k, d1_ref[base + rb + k], slot).start()
        return c

    lax.fori_loop(0, tm // SUBLANES, body, 0)

    @pl.when(i == n - 1)
    def _():
        zero_scr[...] = jnp.zeros_like(zero_scr)

        def zero_row(r):
            return pltpu.make_async_copy(zero_scr.at[pl.ds(0, 1), :], xs_hbm.at[pl.ds(r, 1), :], zsem)

        def zero_block(b):
            start = pl.multiple_of(b * MOE_ZERO_ROWS, MOE_ZERO_ROWS)
            return pltpu.make_async_copy(zero_scr, xs_hbm.at[pl.ds(start, MOE_ZERO_ROWS), :], zsem)

        def start_all(copy):
            def body(r, c):
                copy(r).start()
                return c
            return body

        def wait_all(copy):
            def body(r, c):
                copy(r).wait()
                return c
            return body

        for e in range(N_EXPERTS):
            lax.fori_loop(fill_ref[e], fill_ref[N_EXPERTS + e], start_all(zero_row), 0)
            lax.fori_loop(fill_ref[e], fill_ref[N_EXPERTS + e], wait_all(zero_row), 0)
        blocks_per_tile = MOE_ROW_TILE // MOE_ZERO_ROWS
        first = fill_ref[2 * N_EXPERTS] * blocks_per_tile
        last = (xs_hbm.shape[0] // MOE_ROW_TILE) * blocks_per_tile
        lax.fori_loop(first, last, start_all(zero_block), 0)
        lax.fori_loop(first, last, wait_all(zero_block), 0)
        wait_rows(slot)

        @pl.when(n >= 2)
        def _():
            wait_rows(1 - slot)


def moe_dispatch(x, g, mod_all, ctx_row, prompt_tokens, req_tokens, dest0, dest1, fill, n_tiles, tm=512):
    t = x.shape[0]
    pt = prompt_tokens // tm
    grid_spec = pltpu.PrefetchScalarGridSpec(
        num_scalar_prefetch=3,
        grid=(t // tm,),
        in_specs=[pl.BlockSpec((tm, D_MODEL), lambda i, *_: (i, 0)),
                  pl.BlockSpec((1, D_MODEL), lambda i, *_: (0, 0)),
                  _stream_mod_spec(3, pt, req_tokens // tm, ctx_row),
                  _stream_mod_spec(4, pt, req_tokens // tm, ctx_row)],
        out_specs=pl.BlockSpec(memory_space=pl.ANY),
        scratch_shapes=[pltpu.VMEM((2, tm, D_MODEL), F32), pltpu.VMEM((MOE_ZERO_ROWS, D_MODEL), F32),
                        pltpu.SemaphoreType.DMA((2,)), pltpu.SemaphoreType.DMA(())],
    )
    return pl.pallas_call(
        _dispatch_kernel,
        grid_spec=grid_spec,
        out_shape=jax.ShapeDtypeStruct((n_tiles * MOE_ROW_TILE, D_MODEL), F32),
        compiler_params=pltpu.CompilerParams(dimension_semantics=("arbitrary",),
                                             vmem_limit_bytes=VMEM_LIMIT_BYTES,
                                             disable_bounds_checks=True),
        name="moe_dispatch",
    )(dest0, dest1, fill, x, g.reshape(1, D_MODEL), mod_all, mod_all)


def _experts_kernel(te_ref, tv_ref, x_ref, wg_ref, wu_ref, wd_ref, o_ref, h_scr, acc_scr):
    i = pl.program_id(0)
    f = pl.program_id(1)
    last_f = pl.num_programs(1) - 1

    @pl.when((tv_ref[i] == 0) & (f == last_f))
    def _():
        o_ref[...] = jnp.zeros_like(o_ref)

    @pl.when(tv_ref[i] > 0)
    def _():
        @pl.when(f == 0)
        def _():
            h_scr[...] = x_ref[...].astype(BF16)
            acc_scr[...] = jnp.zeros_like(acc_scr)

        h = h_scr[...]
        a = _dot(h, wg_ref[0].astype(BF16))
        u = _dot(h, wu_ref[0].astype(BF16))
        acc_scr[...] += _dot((_silu(a) * u).astype(BF16), wd_ref[0].astype(BF16))

        @pl.when(f == last_f)
        def _():
            o_ref[...] = acc_scr[...]


def moe_experts(xs_sorted, tile_expert, tile_valid, w_gate, w_up, w_down, tf=512):
    rows = xs_sorted.shape[0]
    fdim = w_gate.shape[2]
    n_f = fdim // tf
    tr = MOE_ROW_TILE

    def f_eff(i, f, tv):
        return jnp.where(tv[i] > 0, f, n_f - 1)

    grid_spec = pltpu.PrefetchScalarGridSpec(
        num_scalar_prefetch=2,
        grid=(rows // tr, n_f),
        in_specs=[pl.BlockSpec((tr, D_MODEL), lambda i, f, te, tv: (i, 0)),
                  pl.BlockSpec((1, D_MODEL, tf), lambda i, f, te, tv: (te[i], 0, f_eff(i, f, tv))),
                  pl.BlockSpec((1, D_MODEL, tf), lambda i, f, te, tv: (te[i], 0, f_eff(i, f, tv))),
                  pl.BlockSpec((1, tf, D_MODEL), lambda i, f, te, tv: (te[i], f_eff(i, f, tv), 0))],
        out_specs=pl.BlockSpec((tr, D_MODEL), lambda i, f, te, tv: (i, 0)),
        scratch_shapes=[pltpu.VMEM((tr, D_MODEL), BF16), pltpu.VMEM((tr, D_MODEL), F32)],
    )
    return pl.pallas_call(
        _experts_kernel,
        grid_spec=grid_spec,
        out_shape=jax.ShapeDtypeStruct((rows, D_MODEL), F32),
        compiler_params=_params("arbitrary", "arbitrary"),
        name="moe_experts",
    )(tile_expert, tile_valid, xs_sorted, w_gate, w_up, w_down)


def _combine_kernel(d0_ref, d1_ref, x_ref, info_ref, gate_ref, fg_ref, ys_hbm, op_ref, os_ref, rbuf, sem,
                    *, prompt_tiles):
    i = pl.program_id(0)
    n = pl.num_programs(0)
    tm = x_ref.shape[0]
    slot = i % 2

    def issue(tile, s):
        base = tile * tm

        def body(r8, c):
            rb = pl.multiple_of(r8 * SUBLANES, SUBLANES)
            for k in range(SUBLANES):
                pltpu.make_async_copy(ys_hbm.at[pl.ds(d0_ref[base + rb + k], 1), :],
                                      rbuf.at[s, 0, pl.ds(rb + k, 1), :], sem.at[s]).start()
                pltpu.make_async_copy(ys_hbm.at[pl.ds(d1_ref[base + rb + k], 1), :],
                                      rbuf.at[s, 1, pl.ds(rb + k, 1), :], sem.at[s]).start()
            return c

        lax.fori_loop(0, tm // SUBLANES, body, 0)

    @pl.when(i == 0)
    def _():
        issue(0, 0)

    @pl.when(i + 1 < n)
    def _():
        issue(i + 1, 1 - slot)

    for k in range(2):
        pltpu.make_async_copy(ys_hbm.at[pl.ds(0, tm), :], rbuf.at[slot, k], sem.at[slot]).wait()

    info = info_ref[...]
    moe = info[:, INFO_G0:INFO_G0 + 1] * rbuf[slot, 0] + info[:, INFO_G1:INFO_G1 + 1] * rbuf[slot, 1]
    y = x_ref[...] + gate_ref[0] * moe
    out = y * lax.rsqrt(jnp.mean(y * y, axis=-1, keepdims=True) + EPS) * fg_ref[...]

    @pl.when(i < prompt_tiles)
    def _():
        op_ref[...] = out

    @pl.when(i >= prompt_tiles)
    def _():
        os_ref[...] = out


def moe_combine(x, info, ys_sorted, dest0, dest1, mod_all, ctx_row, prompt_tokens, req_tokens, final_g, tm=256):
    t = x.shape[0]
    pt = prompt_tokens // tm
    grid_spec = pltpu.PrefetchScalarGridSpec(
        num_scalar_prefetch=2,
        grid=(t // tm,),
        in_specs=[pl.BlockSpec((tm, D_MODEL), lambda i, *_: (i, 0)),
                  pl.BlockSpec((tm, LANES), lambda i, *_: (i, 0)),
                  _stream_mod_spec(5, pt, req_tokens // tm, ctx_row),
                  pl.BlockSpec((1, D_MODEL), lambda i, *_: (0, 0)),
                  pl.BlockSpec(memory_space=pl.ANY)],
        out_specs=[pl.BlockSpec((tm, D_MODEL), lambda i, *_: (jnp.minimum(i, pt - 1), 0)),
                   pl.BlockSpec((tm, D_MODEL), lambda i, *_: (jnp.maximum(i - pt, 0), 0))],
        scratch_shapes=[pltpu.VMEM((2, 2, tm, D_MODEL), F32), pltpu.SemaphoreType.DMA((2,))],
    )
    return pl.pallas_call(
        functools.partial(_combine_kernel, prompt_tiles=pt),
        grid_spec=grid_spec,
        out_shape=[jax.ShapeDtypeStruct((prompt_tokens, D_MODEL), F32),
                   jax.ShapeDtypeStruct((t - prompt_tokens, D_MODEL), F32)],
        compiler_params=pltpu.CompilerParams(dimension_semantics=("arbitrary",),
                                             vmem_limit_bytes=VMEM_LIMIT_BYTES,
                                             disable_bounds_checks=True),
        name="moe_combine",
    )(dest0, dest1, x, info, mod_all, final_g.reshape(1, D_MODEL), ys_sorted)


def moe_layout(counts, info, n_tiles):
    nt_e = (counts + MOE_ROW_TILE - 1) // MOE_ROW_TILE
    ends = jnp.cumsum(nt_e)
    total = ends[-1]
    offset = (ends - nt_e) * MOE_ROW_TILE
    experts = jnp.arange(N_EXPERTS, dtype=jnp.int32)

    def dest(e_lane, r_lane):
        e = info[:, e_lane].astype(jnp.int32)
        off = jnp.sum(jnp.where(e[:, None] == experts[None, :], offset[None, :], 0), axis=1)
        return (off + info[:, r_lane].astype(jnp.int32)).astype(jnp.int32)

    fill = jnp.concatenate([offset + counts, ends * MOE_ROW_TILE, total[None]]).astype(jnp.int32)
    ids = jnp.arange(n_tiles, dtype=jnp.int32)
    ids_c = jnp.minimum(ids, total - 1)
    te = jnp.sum((ids_c[:, None] >= ends[None, :]).astype(jnp.int32), axis=1)
    return (dest(INFO_E0, INFO_R0), dest(INFO_E1, INFO_R1), fill, te.astype(jnp.int32),
            (ids < total).astype(jnp.int32))


def moe_final(x, g, mod_all, ctx_row, prompt_tokens, req_tokens, router_w, final_g, w_gate, w_up, w_down):
    t = x.shape[0]
    info, cnt = moe_route(x, g, mod_all, ctx_row, prompt_tokens, req_tokens, router_w)
    counts = cnt[0, :N_EXPERTS].astype(jnp.int32)
    n_tiles = (2 * t) // MOE_ROW_TILE + N_EXPERTS
    dest0, dest1, fill, te, tv = moe_layout(counts, info, n_tiles)
    xs_sorted = moe_dispatch(x, g, mod_all, ctx_row, prompt_tokens, req_tokens, dest0, dest1, fill, n_tiles)
    ys_sorted = moe_experts(xs_sorted, te, tv, w_gate, w_up, w_down)
    return moe_combine(x, info, ys_sorted, dest0, dest1, mod_all, ctx_row, prompt_tokens, req_tokens, final_g)


def kernel(x_prompt, x_sample, state_s5, cache_na_k, cache_na_v, cache_diff_k, cache_diff_v, c, c_ctx, w_mod, b_mod, norm_mix_g, norm_ffn_g, final_norm_g, w_in_e, w_out_e, s5_lam_re, s5_lam_im, s5_log_dt, s5_b_re, s5_b_im, s5_c_re, s5_c_im, s5_d, s5_w_glu, s5_b_glu, na_rpb, ffn_w_gate, ffn_w_up, ffn_w_down, w_in_o, w_out_o, diff_lam_q1, diff_lam_k1, diff_lam_q2, diff_lam_k2, diff_subln_g, conv_w, conv_b, conv_ln_g, conv_ln_b, router_w, moe_w_gate, moe_w_up, moe_w_down):
    bp, lp, d = x_prompt.shape
    bs, ls, _ = x_sample.shape
    tm = 1024
    xp = x_prompt.reshape(bp * lp, d)
    xs = x_sample.reshape(bs * ls, d)
    rows_p = (bp * lp) // tm
    rows_s = ls // tm

    cond8 = jnp.concatenate([c, c_ctx[None, :], jnp.zeros((SUBLANES - bs - 1, d), F32)], axis=0)
    mod = adaln_all(cond8, w_mod, b_mod)
    mod_s = mod[:, 0:bs, None, :]
    mod_p = mod[:, bs:bs + 1, None, :]

    def tiles(rows, tile):
        return rows * tm // tile

    bmat, cmat, lam8 = s5_params(s5_lam_re[0], s5_lam_im[0], s5_log_dt[0], s5_b_re[0], s5_b_im[0],
                                 s5_c_re[0], s5_c_im[0])
    bias = na_bias_blocks(na_rpb[0])
    n_e = w_in_e.shape[-1]

    proj_p = in_proj(xp, norm_mix_g[0], mod_p[0], rows_p, w_in_e[0])
    u_s, qkv_s = in_proj_split(xs, norm_mix_g[0], mod_s[0], rows_s, w_in_e[0], HALF_MIX)

    y_p, st_p = s5_scan(proj_p.reshape(bp, lp, n_e), bmat, cmat, lam8, None, 1)
    chunks = SUBLANES // bs
    h0 = state_s5[:, 0].reshape(bs, 2, 2, S5_GROUPS * S5_STATE)
    y_s, _ = s5_scan(u_s.reshape(bs * chunks, ls // chunks, HALF_MIX), bmat, cmat, lam8, h0, chunks)
    s5o_p = s5_glu(y_p.reshape(bp * lp, HALF_MIX), proj_p, s5_d[0], s5_w_glu[0], s5_b_glu[0])
    s5o_s = s5_glu(y_s.reshape(bs * ls, HALF_MIX), u_s, s5_d[0], s5_w_glu[0], s5_b_glu[0])

    nao_p, na_k, na_v = na_ctx(proj_p.reshape(bp, lp, n_e))

    def heads_to_lanes(cache):
        return cache.transpose(0, 2, 1, 3).reshape(bs, cache.shape[2], HALF_MIX).astype(BF16)

    nao_s = na_lat(qkv_s.reshape(bs, ls, n_e - HALF_MIX), heads_to_lanes(cache_na_k[:, 0]),
                   heads_to_lanes(cache_na_v[:, 0]), bias)

    xp = mix_out(xp, s5o_p, nao_p.reshape(bp * lp, HALF_MIX), w_out_e[0], mod_p[0], tiles(rows_p, 512))
    xs = mix_out(xs, s5o_s, nao_s.reshape(bs * ls, HALF_MIX), w_out_e[0], mod_s[0], tiles(rows_s, 512))
    xp = ffn(xp, norm_ffn_g[0], mod_p[0], rows_p, ffn_w_gate[0], ffn_w_up[0], ffn_w_down[0])
    xs = ffn(xs, norm_ffn_g[0], mod_s[0], rows_s, ffn_w_gate[0], ffn_w_up[0], ffn_w_down[0])

    lam_init = 0.8 - 0.6 * math.exp(-0.3 * 1)
    lam = (jnp.exp(jnp.sum(diff_lam_q1[0].astype(F32) * diff_lam_k1[0].astype(F32)))
           - jnp.exp(jnp.sum(diff_lam_q2[0].astype(F32) * diff_lam_k2[0].astype(F32)))
           + lam_init)
    cos, sin = rope_tables(ls)
    n_o = w_in_o.shape[-1]

    proj_p = in_proj(xp, norm_mix_g[1], mod_p[1], rows_p, w_in_o[0])
    proj_s = in_proj(xs, norm_mix_g[1], mod_s[1], rows_s, w_in_o[0])

    do_p, diff_k, diff_v = diff_ctx(proj_p.reshape(bp, lp, n_o), lam, diff_subln_g[0], lam_init)
    do_s = diff_lat(proj_s.reshape(bs, ls, n_o), cache_diff_k[:, 0:1], cache_diff_v[:, 0:1], cos, sin,
                    lam, diff_subln_g[0], lam_init)
    co_p = conformer_conv(proj_p.reshape(bp, lp, n_o), conv_w[0], conv_b[0], conv_ln_g[0], conv_ln_b[0], lp)
    co_s = conformer_conv(proj_s.reshape(bs, ls, n_o), conv_w[0], conv_b[0], conv_ln_g[0], conv_ln_b[0], 512)

    mod_all = mod[1][:, None, :]
    x_all = mix_out2(xp, do_p.reshape(bp * lp, HALF_MIX), co_p.reshape(bp * lp, HALF_MIX),
                     xs, do_s.reshape(bs * ls, HALF_MIX), co_s.reshape(bs * ls, HALF_MIX),
                     w_out_o[0], mod_all, bs, ls)
    yp, ys = moe_final(x_all, norm_ffn_g[1], mod_all, bs, bp * lp, ls, router_w[0], final_norm_g,
                       moe_w_gate[0], moe_w_up[0], moe_w_down[0])

    new_state = st_p.reshape(bp, 1, 2, 2, S5_GROUPS, S5_STATE)
    return (yp.reshape(bp, lp, d), ys.reshape(bs, ls, d), new_state, na_k, na_v, diff_k, diff_v)
```

```python
import functools
import math

import jax
import jax.numpy as jnp
import numpy as np
from jax import lax
from jax.experimental import pallas as pl
from jax.experimental.pallas import tpu as pltpu

D_MODEL = 1024
DEPTH = 2
GRID_W = 64
HALF_MIX = 512
S5_GROUP_CH = 16
S5_GROUPS = 32
S5_STATE = 64
NA_HEAD_DIM = 64
NA_HEADS = 8
NA_WIN_R = 8
NA_WIN_C = 16
DIFF_D = 64
DIFF_HEAD_DIM = 128
DIFF_HEADS = 4
ROPE_BASE = 10000.0
CONV_WIDTH = 31
N_EXPERTS = 8
EPS = 1e-6

F32 = jnp.float32
BF16 = jnp.bfloat16
NEG_BIG = -1e30

VMEM_LIMIT_BYTES = 56 * 1024 * 1024
LANES = 128
SUBLANES = 8

S5_COL_GROUPS = 8
S5_COL_CH = S5_COL_GROUPS * S5_GROUP_CH
S5_COL_STATE = S5_COL_GROUPS * S5_STATE
S5_N_COL = S5_GROUPS // S5_COL_GROUPS
S5_TIME_BLOCK = 256


def _params(*sem):
    return pltpu.CompilerParams(dimension_semantics=sem, vmem_limit_bytes=VMEM_LIMIT_BYTES)


def _dot(a, b):
    return jnp.dot(a, b, preferred_element_type=F32)


def _dot_nt(a, b):
    return lax.dot_general(a, b, (((1,), (1,)), ((), ())), preferred_element_type=F32)


def _silu(x):
    return x * jax.nn.sigmoid(x)


def _norm_mod(x, g, shift, scale):
    y = x * lax.rsqrt(jnp.mean(x * x, axis=-1, keepdims=True) + EPS) * g
    return y * (1.0 + scale) + shift


def _mod_kernel(cond_ref, w_ref, b_ref, o_ref):
    s = _silu(cond_ref[...])
    o_ref[0] = jnp.dot(s, w_ref[0], precision=lax.Precision.HIGHEST,
                       preferred_element_type=F32) + b_ref[0]


def adaln_all(cond8, w_mod, b_mod):
    tn = 1536
    n = w_mod.shape[-1]
    return pl.pallas_call(
        _mod_kernel,
        grid=(DEPTH, n // tn),
        in_specs=[pl.BlockSpec((SUBLANES, D_MODEL), lambda l, j: (0, 0)),
                  pl.BlockSpec((1, D_MODEL, tn), lambda l, j: (l, 0, j)),
                  pl.BlockSpec((1, 1, tn), lambda l, j: (l, 0, j))],
        out_specs=pl.BlockSpec((1, SUBLANES, tn), lambda l, j: (l, 0, j)),
        out_shape=jax.ShapeDtypeStruct((DEPTH, SUBLANES, n), F32),
        compiler_params=_params("arbitrary", "arbitrary"),
        name="adaln_mod",
    )(cond8, w_mod, b_mod.reshape(DEPTH, 1, n))


def _mod_spec(chunk, tiles_per_row):
    return pl.BlockSpec((1, 1, D_MODEL), lambda i, *_: (i // tiles_per_row, 0, chunk))


def _in_proj_kernel(x_ref, g_ref, sh_ref, sc_ref, w_ref, *refs, n_f32):
    *outs, wb_scr = refs

    @pl.when(pl.program_id(0) == 0)
    def _():
        wb_scr[...] = w_ref[...].astype(BF16)

    h = _norm_mod(x_ref[...], g_ref[...], sh_ref[0], sc_ref[0]).astype(BF16)
    y = _dot(h, wb_scr[...])
    if len(outs) == 1:
        outs[0][...] = y
    else:
        outs[0][...] = y[:, :n_f32]
        outs[1][...] = y[:, n_f32:].astype(BF16)


def in_proj(x, g, mod, tiles_per_row, w, n_f32=None, tm=512):
    t = x.shape[0]
    n = w.shape[1]
    if n_f32 is None:
        out_specs = pl.BlockSpec((tm, n), lambda i: (i, 0))
        out_shape = jax.ShapeDtypeStruct((t, n), F32)
    else:
        out_specs = [pl.BlockSpec((tm, n_f32), lambda i: (i, 0)), pl.BlockSpec((tm, n - n_f32), lambda i: (i, 0))]
        out_shape = [jax.ShapeDtypeStruct((t, n_f32), F32), jax.ShapeDtypeStruct((t, n - n_f32), BF16)]
    return pl.pallas_call(
        functools.partial(_in_proj_kernel, n_f32=n_f32),
        grid=(t // tm,),
        in_specs=[pl.BlockSpec((tm, D_MODEL), lambda i: (i, 0)),
                  pl.BlockSpec((1, D_MODEL), lambda i: (0, 0)),
                  _mod_spec(0, tiles_per_row),
                  _mod_spec(1, tiles_per_row),
                  pl.BlockSpec((D_MODEL, n), lambda i: (0, 0), pipeline_mode=pl.Buffered(1))],
        out_specs=out_specs,
        out_shape=out_shape,
        scratch_shapes=[pltpu.VMEM((D_MODEL, n), BF16)],
        compiler_params=_params("arbitrary"),
        name="in_proj",
    )(x, g.reshape(1, D_MODEL), mod, mod, w)


def _s5_scan_kernel(*refs, seq, chunks, has_init):
    if has_init:
        u_ref, bm_ref, cm_ref, lam_ref, h0_ref, y_ref, st_ref, bu_scr, ytm_scr = refs
    else:
        u_ref, bm_ref, cm_ref, lam_ref, y_ref, st_ref, bu_scr, ytm_scr = refs
        h0_ref = None
    tb = S5_TIME_BLOCK
    n_tb = seq // tb
    ns = S5_COL_STATE
    row = lax.broadcasted_iota(jnp.int32, (SUBLANES, ns), 0)
    piece = row % chunks

    for d in range(2):
        lam = lam_ref[d, 0]
        lr, li = lam[:, :ns], lam[:, ns:]
        bm = bm_ref[d, 0]
        cm = cm_ref[d, 0]
        blocks = list(range(n_tb)) if d == 0 else list(range(n_tb - 1, -1, -1))

        def load_bu(k):
            ub = u_ref[:, k * tb:(k + 1) * tb, :]
            utm = jnp.swapaxes(ub, 0, 1).reshape(tb * SUBLANES, S5_COL_CH).astype(BF16)
            bu_scr[...] = _dot(utm, bm).reshape(tb, SUBLANES, 2 * ns)

        def scan_block(h, store):
            def step(s, carry):
                hr, hi = carry
                t = (tb - 1 - s) if d == 1 else s
                b = bu_scr[t]
                nr = lr * hr - li * hi + b[:, :ns]
                ni = lr * hi + li * hr + b[:, ns:]
                if store:
                    bu_scr[t, :, :ns] = nr
                    bu_scr[t, :, ns:] = ni
                return nr, ni
            return lax.fori_loop(0, tb, step, h, unroll=2)

        zero = jnp.zeros((SUBLANES, ns), F32)
        if chunks > 1:
            h = (zero, zero)
            for k in blocks:
                load_bu(k)
                h = scan_block(h, False)
            fr, fi = h
            pr, pi = lr, li
            for _ in range(int(math.log2(seq))):
                pr, pi = pr * pr - pi * pi, 2.0 * pr * pi
            edge = 0 if d == 0 else chunks - 1
            shift = 1 if d == 0 else SUBLANES - 1
            if has_init:
                h0r = h0_ref[:, d, 0, :]
                h0i = h0_ref[:, d, 1, :]
                seq_of_row = row // chunks
                er, ei = zero, zero
                for b in range(SUBLANES // chunks):
                    er = jnp.where(seq_of_row == b, h0r[b:b + 1, :], er)
                    ei = jnp.where(seq_of_row == b, h0i[b:b + 1, :], ei)
            else:
                er, ei = zero, zero
            is_edge = piece == edge
            cr = jnp.where(is_edge, er, zero)
            ci = jnp.where(is_edge, ei, zero)
            for _ in range(chunks - 1):
                tr = fr + pr * cr - pi * ci
                ti = fi + pr * ci + pi * cr
                cr = jnp.where(is_edge, er, pltpu.roll(tr, shift, 0))
                ci = jnp.where(is_edge, ei, pltpu.roll(ti, shift, 0))
            h = (cr, ci)
        else:
            if has_init:
                h = (h0_ref[:, d, 0, :], h0_ref[:, d, 1, :])
            else:
                h = (zero, zero)

        for k in blocks:
            load_bu(k)
            h = scan_block(h, True)
            hb = bu_scr[...].reshape(tb * SUBLANES, 2 * ns).astype(BF16)
            yb = _dot(hb, cm).reshape(tb, SUBLANES, S5_COL_CH)
            if d == 0:
                ytm_scr[k * tb:(k + 1) * tb] = yb
            else:
                ytm_scr[k * tb:(k + 1) * tb] += yb
        st_ref[:, d, 0, :] = h[0]
        st_ref[:, d, 1, :] = h[1]

    y_ref[...] = jnp.swapaxes(ytm_scr[...], 0, 1)


def s5_scan(proj3, bmat, cmat, lam8, h0, chunks):
    rows, seq, _ = proj3.shape
    ns = S5_COL_STATE
    has_init = h0 is not None
    in_specs = [pl.BlockSpec((SUBLANES, seq, S5_COL_CH), lambda i, c: (i, 0, c)),
                pl.BlockSpec((2, 1, S5_COL_CH, 2 * ns), lambda i, c: (0, c, 0, 0)),
                pl.BlockSpec((2, 1, 2 * ns, S5_COL_CH), lambda i, c: (0, c, 0, 0)),
                pl.BlockSpec((2, 1, SUBLANES, 2 * ns), lambda i, c: (0, c, 0, 0))]
    args = [proj3, bmat, cmat, lam8]
    if has_init:
        nb = h0.shape[0]
        in_specs.append(pl.BlockSpec((nb, 2, 2, ns), lambda i, c: (0, 0, 0, c)))
        args.append(h0)
    y, st = pl.pallas_call(
        functools.partial(_s5_scan_kernel, seq=seq, chunks=chunks, has_init=has_init),
        grid=(rows // SUBLANES, S5_N_COL),
        in_specs=in_specs,
        out_specs=[pl.BlockSpec((SUBLANES, seq, S5_COL_CH), lambda i, c: (i, 0, c)),
                   pl.BlockSpec((SUBLANES, 2, 2, ns), lambda i, c: (i, 0, 0, c))],
        out_shape=[jax.ShapeDtypeStruct((rows, seq, HALF_MIX), F32),
                   jax.ShapeDtypeStruct((rows, 2, 2, S5_GROUPS * S5_STATE), F32)],
        scratch_shapes=[pltpu.VMEM((S5_TIME_BLOCK, SUBLANES, 2 * ns), F32),
                        pltpu.VMEM((seq, SUBLANES, S5_COL_CH), F32)],
        compiler_params=_params("arbitrary", "arbitrary"),
        name="s5_scan",
    )(*args)
    return y, st


def _s5_glu_kernel(y_ref, u_ref, d_ref, w_ref, b_ref, o_ref):
    y = u_ref[...] * d_ref[...] + y_ref[...]
    z = jax.nn.gelu(y)
    gate = _dot(z.astype(BF16), w_ref[...].astype(BF16)) + b_ref[...]
    o_ref[...] = (z * jax.nn.sigmoid(gate)).astype(o_ref.dtype)


def s5_glu(y, proj, d_skip, w_glu, b_glu, tm=1024):
    t = y.shape[0]
    return pl.pallas_call(
        _s5_glu_kernel,
        grid=(t // tm,),
        in_specs=[pl.BlockSpec((tm, HALF_MIX), lambda i: (i, 0)),
                  pl.BlockSpec((tm, HALF_MIX), lambda i: (i, 0)),
                  pl.BlockSpec((1, HALF_MIX), lambda i: (0, 0)),
                  pl.BlockSpec((HALF_MIX, HALF_MIX), lambda i: (0, 0)),
                  pl.BlockSpec((1, HALF_MIX), lambda i: (0, 0))],
        out_specs=pl.BlockSpec((tm, HALF_MIX), lambda i: (i, 0)),
        out_shape=jax.ShapeDtypeStruct((t, HALF_MIX), BF16),
        compiler_params=_params("arbitrary"),
        name="s5_glu",
    )(y, proj, d_skip.reshape(1, HALF_MIX), w_glu, b_glu.reshape(1, HALF_MIX))


def s5_params(lam_re, lam_im, log_dt, b_re, b_im, c_re, c_im):
    lr = lam_re.astype(F32)
    li = lam_im.astype(F32)
    dt = jnp.exp(log_dt.astype(F32))[..., None]
    mag = jnp.exp(lr * dt)
    bar_re = mag * jnp.cos(li * dt)
    bar_im = mag * jnp.sin(li * dt)
    den = lr * lr + li * li
    q_re = ((bar_re - 1.0) * lr + bar_im * li) / den
    q_im = (bar_im * lr - (bar_re - 1.0) * li) / den
    br = b_re.astype(F32)
    bi = b_im.astype(F32)
    b_bar_re = q_re[..., None] * br - q_im[..., None] * bi
    b_bar_im = q_re[..., None] * bi + q_im[..., None] * br
    eye = jnp.eye(S5_COL_GROUPS, dtype=F32)

    def block_diag_b(m):
        m = m.reshape(2, S5_N_COL, S5_COL_GROUPS, S5_STATE, S5_GROUP_CH)
        bd = jnp.einsum('dngpc,gh->dngchp', m, eye)
        return bd.reshape(2, S5_N_COL, S5_COL_CH, S5_COL_STATE)

    def block_diag_c(m):
        m = m.reshape(2, S5_N_COL, S5_COL_GROUPS, S5_GROUP_CH, S5_STATE)
        bd = jnp.einsum('dngcp,gh->dngphc', m, eye)
        return bd.reshape(2, S5_N_COL, S5_COL_STATE, S5_COL_CH)

    bmat = jnp.concatenate([block_diag_b(b_bar_re), block_diag_b(b_bar_im)], axis=-1).astype(BF16)
    cmat = jnp.concatenate([block_diag_c(c_re.astype(F32)), block_diag_c(-c_im.astype(F32))],
                           axis=-2).astype(BF16)
    lam_cat = jnp.concatenate([bar_re.reshape(2, S5_N_COL, S5_COL_STATE),
                               bar_im.reshape(2, S5_N_COL, S5_COL_STATE)], axis=-1)
    lam8 = jnp.broadcast_to(lam_cat[:, :, None, :], (2, S5_N_COL, SUBLANES, 2 * S5_COL_STATE))
    return bmat, cmat, lam8


def _na_ctx_kernel(q_ref, k_ref, v_ref, o_ref, ko_ref, vo_ref):
    scale = NA_HEAD_DIM ** -0.5
    q = q_ref[0]
    k = k_ref[0]
    v = v_ref[0]
    outs = []
    for h in range(NA_HEADS):
        sl = slice(h * NA_HEAD_DIM, (h + 1) * NA_HEAD_DIM)
        kh = k[:, sl]
        vh = v[:, sl]
        ko_ref[0, 0, h] = kh
        vo_ref[0, 0, h] = vh
        s = _dot_nt(q[:, sl].astype(BF16), kh.astype(BF16)) * scale
        p = jnp.exp(s - jnp.max(s, axis=-1, keepdims=True))
        p = p / jnp.sum(p, axis=-1, keepdims=True)
        outs.append(_dot(p.astype(BF16), vh.astype(BF16)))
    o_ref[0] = jnp.concatenate(outs, axis=-1).astype(o_ref.dtype)


def na_ctx(proj3):
    b, seq, _ = proj3.shape
    cache_shape = jax.ShapeDtypeStruct((b, 1, NA_HEADS, seq, NA_HEAD_DIM), F32)
    cache_spec = pl.BlockSpec((1, 1, NA_HEADS, seq, NA_HEAD_DIM), lambda i: (i, 0, 0, 0, 0))
    return pl.pallas_call(
        _na_ctx_kernel,
        grid=(b,),
        in_specs=[pl.BlockSpec((1, seq, HALF_MIX), lambda i: (i, 0, 1)),
                  pl.BlockSpec((1, seq, HALF_MIX), lambda i: (i, 0, 2)),
                  pl.BlockSpec((1, seq, HALF_MIX), lambda i: (i, 0, 3))],
        out_specs=[pl.BlockSpec((1, seq, HALF_MIX), lambda i: (i, 0, 0)), cache_spec, cache_spec],
        out_shape=[jax.ShapeDtypeStruct((b, seq, HALF_MIX), BF16), cache_shape, cache_shape],
        compiler_params=_params("arbitrary"),
        name="na_ctx",
    )(proj3, proj3, proj3)


NA_Q_ROWS = 4
NA_KEY_ROWS = 12


def na_bias_blocks(rpb):
    qcol = np.arange(GRID_W)
    cc = np.arange(GRID_W)
    cs = np.clip(qcol - NA_WIN_C // 2, 0, GRID_W - NA_WIN_C)
    valid = (cc[None, :] >= cs[:, None]) & (cc[None, :] < cs[:, None] + NA_WIN_C)
    coff = cc[None, :] - qcol[:, None] + (NA_WIN_C - 1)
    n_col = 2 * NA_WIN_C - 1
    sel = ((coff[None] == np.arange(n_col)[:, None, None]) & valid[None]).astype(np.float32)
    sel = sel.reshape(n_col, GRID_W * GRID_W)
    mask = np.where(valid, 0.0, NEG_BIG).astype(np.float32).reshape(1, GRID_W * GRID_W)
    n_row = 2 * NA_WIN_R - 1
    t1 = jnp.dot(rpb.astype(F32).reshape(NA_HEADS * n_row, n_col), jnp.asarray(sel),
                 precision=lax.Precision.HIGHEST) + jnp.asarray(mask)
    t1 = t1.reshape(NA_HEADS, n_row, GRID_W, GRID_W)
    neg = jnp.full((NA_HEADS, GRID_W, GRID_W), NEG_BIG, F32)
    variants = ((lambda ri: 0, NA_WIN_R - 1), (lambda ri: ri, NA_WIN_R // 2 - 1),
                (lambda ri: NA_KEY_ROWS - NA_WIN_R, -1))
    out = []
    for lo_of, shift in variants:
        per_ri = []
        for ri in range(NA_Q_ROWS):
            lo = lo_of(ri)
            blocks = [t1[:, wr - ri + shift] if lo <= wr < lo + NA_WIN_R else neg for wr in range(NA_KEY_ROWS)]
            per_ri.append(jnp.concatenate(blocks, axis=-1))
        out.append(jnp.stack(per_ri, axis=1).reshape(NA_HEADS, NA_Q_ROWS * GRID_W, NA_KEY_ROWS * GRID_W))
    return jnp.stack(out, axis=0)


def _na_lat_kernel(q_ref, k_ref, v_ref, kc_ref, vc_ref, bias_ref, o_ref):
    qb = pl.program_id(1)
    rows = k_ref.shape[1] // GRID_W
    nk = NA_KEY_ROWS * GRID_W
    first_row = jnp.clip(qb * NA_Q_ROWS - NA_WIN_R // 2, 0, rows - NA_KEY_ROWS)
    start = pl.multiple_of(first_row * GRID_W, GRID_W)
    tq = q_ref.shape[1]
    lane = lax.broadcasted_iota(jnp.int32, (tq, LANES), 1)
    low = lane < NA_HEAD_DIM
    outs = []
    for pr in range(NA_HEADS // 2):
        cols = slice(pr * LANES, (pr + 1) * LANES)
        qp = q_ref[0, :, cols].astype(F32) * (NA_HEAD_DIM ** -0.5)
        kw = k_ref[0, pl.ds(start, nk), cols]
        vw = v_ref[0, pl.ds(start, nk), cols]
        kc = kc_ref[0, :, cols]
        vc = vc_ref[0, :, cols]
        o_pair = None
        for half in range(2):
            qm = jnp.where(low if half == 0 else jnp.logical_not(low), qp, 0.0).astype(BF16)
            s_loc = _dot_nt(qm, kw) + bias_ref[0, 2 * pr + half]
            s_ctx = _dot_nt(qm, kc)
            m = jnp.maximum(jnp.max(s_loc, axis=-1, keepdims=True), jnp.max(s_ctx, axis=-1, keepdims=True))
            p_loc = jnp.exp(s_loc - m)
            p_ctx = jnp.exp(s_ctx - m)
            inv = 1.0 / (jnp.sum(p_loc, axis=-1, keepdims=True) + jnp.sum(p_ctx, axis=-1, keepdims=True))
            o = (_dot(p_loc.astype(BF16), vw) + _dot(p_ctx.astype(BF16), vc)) * inv
            o_pair = o if half == 0 else jnp.where(low, o_pair, o)
        outs.append(o_pair)
    o_ref[0] = jnp.concatenate(outs, axis=-1).astype(o_ref.dtype)


def na_lat(qkv3, k_ctx, v_ctx, bias):
    b, seq, _ = qkv3.shape
    tq = NA_Q_ROWS * GRID_W
    n_q = seq // tq
    lc = k_ctx.shape[1]
    ctx_spec = pl.BlockSpec((1, lc, HALF_MIX), lambda i, r: (i, 0, 0))
    return pl.pallas_call(
        _na_lat_kernel,
        grid=(b, n_q),
        in_specs=[pl.BlockSpec((1, tq, HALF_MIX), lambda i, r: (i, r, 0)),
                  pl.BlockSpec((1, seq, HALF_MIX), lambda i, r: (i, 0, 1)),
                  pl.BlockSpec((1, seq, HALF_MIX), lambda i, r: (i, 0, 2)),
                  ctx_spec, ctx_spec,
                  pl.BlockSpec((1, NA_HEADS, tq, NA_KEY_ROWS * GRID_W),
                               lambda i, r: (jnp.where(r == 0, 0, jnp.where(r == n_q - 1, 2, 1)), 0, 0, 0))],
        out_specs=pl.BlockSpec((1, tq, HALF_MIX), lambda i, r: (i, r, 0)),
        out_shape=jax.ShapeDtypeStruct((b, seq, HALF_MIX), BF16),
        compiler_params=_params("arbitrary", "arbitrary"),
        name="na_lat",
    )(qkv3, qkv3, qkv3, k_ctx, v_ctx, bias)


def _softmax_pair_diff(s1, s2, lam):
    p1 = jnp.exp(s1 - jnp.max(s1, axis=-1, keepdims=True))
    p2 = jnp.exp(s2 - jnp.max(s2, axis=-1, keepdims=True))
    inv1 = 1.0 / jnp.sum(p1, axis=-1, keepdims=True)
    inv2 = lam / jnp.sum(p2, axis=-1, keepdims=True)
    return p1 * inv1 - p2 * inv2


def _sub_ln(o, g, lam_init):
    return o * lax.rsqrt(jnp.mean(o * o, axis=-1, keepdims=True) + EPS) * g * (1.0 - lam_init)


def _diff_ctx_kernel(lam_ref, q_ref, k_ref, v_ref, g_ref, o_ref, ko_ref, vo_ref, *, lam_init):
    scale = DIFF_D ** -0.5
    lam = lam_ref[0, 0]
    lane = lax.broadcasted_iota(jnp.int32, (q_ref.shape[1], DIFF_HEAD_DIM), 1)
    first = lane < DIFF_D
    for h in range(DIFF_HEADS):
        sl = slice(h * DIFF_HEAD_DIM, (h + 1) * DIFF_HEAD_DIM)
        qh = q_ref[0, :, sl]
        kh = k_ref[0, :, sl]
        vh = v_ref[0, :, sl]
        ko_ref[0, 0, h] = kh
        vo_ref[0, 0, h] = vh
        kb = kh.astype(BF16)
        s1 = _dot_nt(jnp.where(first, qh, 0.0).astype(BF16), kb) * scale
        s2 = _dot_nt(jnp.where(first, 0.0, qh).astype(BF16), kb) * scale
        a = _softmax_pair_diff(s1, s2, lam)
        o = _dot(a.astype(BF16), vh.astype(BF16))
        o_ref[0, :, sl] = _sub_ln(o, g_ref[...], lam_init).astype(o_ref.dtype)


def diff_ctx(proj3, lam, subln_g, lam_init):
    b, seq, _ = proj3.shape
    cache_shape = jax.ShapeDtypeStruct((b, 1, DIFF_HEADS, seq, DIFF_HEAD_DIM), F32)
    cache_spec = pl.BlockSpec((1, 1, DIFF_HEADS, seq, DIFF_HEAD_DIM), lambda i: (i, 0, 0, 0, 0))
    return pl.pallas_call(
        functools.partial(_diff_ctx_kernel, lam_init=lam_init),
        grid=(b,),
        in_specs=[pl.BlockSpec(memory_space=pltpu.SMEM),
                  pl.BlockSpec((1, seq, HALF_MIX), lambda i: (i, 0, 0)),
                  pl.BlockSpec((1, seq, HALF_MIX), lambda i: (i, 0, 1)),
                  pl.BlockSpec((1, seq, HALF_MIX), lambda i: (i, 0, 2)),
                  pl.BlockSpec((1, DIFF_HEAD_DIM), lambda i: (0, 0))],
        out_specs=[pl.BlockSpec((1, seq, HALF_MIX), lambda i: (i, 0, 0)), cache_spec, cache_spec],
        out_shape=[jax.ShapeDtypeStruct((b, seq, HALF_MIX), BF16), cache_shape, cache_shape],
        compiler_params=_params("arbitrary"),
        name="diff_ctx",
    )(lam.reshape(1, 1), proj3, proj3, proj3, subln_g.reshape(1, DIFF_HEAD_DIM))


def rope_tables(seq):
    t = jnp.arange(seq)
    row = (t // GRID_W).astype(F32)
    col = (t % GRID_W).astype(F32)
    n_freq = DIFF_D // 4
    inv = ROPE_BASE ** (-jnp.arange(n_freq, dtype=F32) / n_freq)
    ang = jnp.concatenate([row[:, None] * inv, col[:, None] * inv], axis=-1)
    cos = jnp.repeat(jnp.cos(ang), 2, axis=-1)
    sin = jnp.repeat(jnp.sin(ang), 2, axis=-1)
    sign = jnp.where(jnp.arange(DIFF_D) % 2 == 0, -1.0, 1.0).astype(F32)
    sin = sin * sign
    return jnp.tile(cos, (1, 2)), jnp.tile(sin, (1, 2))


def _rope(x, cos, sin_signed):
    lane = lax.broadcasted_iota(jnp.int32, x.shape, 1)
    nxt = pltpu.roll(x, x.shape[1] - 1, 1)
    prv = pltpu.roll(x, 1, 1)
    partner = jnp.where(lane % 2 == 0, nxt, prv)
    return x * cos + partner * sin_signed


def _diff_lat_kernel(lam_ref, q_ref, k_ref, v_ref, kc_ref, vc_ref, cq_ref, sq_ref, ck_ref, sk_ref,
                     g_ref, o_ref, k_all, v_all, *, lam_init):
    seq = k_ref.shape[1]

    @pl.when(pl.program_id(2) == 0)
    def _():
        k_all[0:seq, :] = _rope(k_ref[0], ck_ref[...], sk_ref[...]).astype(BF16)
        k_all[seq:, :] = kc_ref[0, 0, 0].astype(BF16)
        v_all[0:seq, :] = v_ref[0].astype(BF16)
        v_all[seq:, :] = vc_ref[0, 0, 0].astype(BF16)

    lam = lam_ref[0, 0]
    q = _rope(q_ref[0], cq_ref[...], sq_ref[...]) * (DIFF_D ** -0.5)
    lane = lax.broadcasted_iota(jnp.int32, q.shape, 1)
    first = lane < DIFF_D
    kb = k_all[...]
    s1 = _dot_nt(jnp.where(first, q, 0.0).astype(BF16), kb)
    s2 = _dot_nt(jnp.where(first, 0.0, q).astype(BF16), kb)
    a = _softmax_pair_diff(s1, s2, lam)
    o = _dot(a.astype(BF16), v_all[...])
    o_ref[0] = _sub_ln(o, g_ref[...], lam_init).astype(o_ref.dtype)


def diff_lat(proj3, k_ctx, v_ctx, cos, sin, lam, subln_g, lam_init, tq=256):
    b, seq, _ = proj3.shape
    lc = k_ctx.shape[3]
    hd = DIFF_HEAD_DIM
    ctx_spec = pl.BlockSpec((1, 1, 1, lc, hd), lambda i, h, q: (i, 0, h, 0, 0))
    tq_spec = pl.BlockSpec((tq, hd), lambda i, h, q: (q, 0))
    full_spec = pl.BlockSpec((seq, hd), lambda i, h, q: (0, 0))
    return pl.pallas_call(
        functools.partial(_diff_lat_kernel, lam_init=lam_init),
        grid=(b, DIFF_HEADS, seq // tq),
        in_specs=[pl.BlockSpec(memory_space=pltpu.SMEM),
                  pl.BlockSpec((1, tq, hd), lambda i, h, q: (i, q, h)),
                  pl.BlockSpec((1, seq, hd), lambda i, h, q: (i, 0, DIFF_HEADS + h)),
                  pl.BlockSpec((1, seq, hd), lambda i, h, q: (i, 0, 2 * DIFF_HEADS + h)),
                  ctx_spec, ctx_spec, tq_spec, tq_spec, full_spec, full_spec,
                  pl.BlockSpec((1, hd), lambda i, h, q: (0, 0))],
        out_specs=pl.BlockSpec((1, tq, hd), lambda i, h, q: (i, q, h)),
        out_shape=jax.ShapeDtypeStruct((b, seq, HALF_MIX), BF16),
        scratch_shapes=[pltpu.VMEM((seq + lc, hd), BF16), pltpu.VMEM((seq + lc, hd), BF16)],
        compiler_params=_params("arbitrary", "arbitrary", "arbitrary"),
        name="diff_lat",
    )(lam.reshape(1, 1), proj3, proj3, proj3, k_ctx, v_ctx, cos, sin, cos, sin,
      subln_g.reshape(1, hd))


CONV_PAD = 16
CONV_SUB = 64


def _conv_kernel(a_ref, g_ref, ap_ref, gp_ref, an_ref, gn_ref, w_ref, b_ref, lg_ref, lb_ref, o_ref, xp_scr,
                 xsh_scr):
    t = pl.program_id(1)
    tt = a_ref.shape[1]
    prev = ap_ref[0] * jax.nn.sigmoid(gp_ref[0])
    nxt = an_ref[0] * jax.nn.sigmoid(gn_ref[0])
    xp_scr[0:CONV_PAD, :] = jnp.where(t > 0, prev, 0.0)
    xp_scr[CONV_PAD + tt:, :] = jnp.where(t < pl.num_programs(1) - 1, nxt, 0.0)
    xp_scr[CONV_PAD:CONV_PAD + tt, :] = a_ref[0] * jax.nn.sigmoid(g_ref[0])
    first_tap = CONV_PAD - CONV_WIDTH // 2
    n_rows = xsh_scr.shape[1]
    for b in range(SUBLANES):
        xsh_scr[b] = xp_scr[b:b + n_rows, :]
    for i in range(tt // CONV_SUB):
        s = i * CONV_SUB
        acc = jnp.zeros((CONV_SUB, HALF_MIX), F32)
        for j in range(CONV_WIDTH):
            whole, phase = divmod(first_tap + j, SUBLANES)
            lo = s + whole * SUBLANES
            acc = acc + xsh_scr[phase, lo:lo + CONV_SUB, :] * w_ref[j:j + 1, :]
        y = acc + b_ref[...]
        mu = jnp.mean(y, axis=-1, keepdims=True)
        yc = y - mu
        var = jnp.mean(yc * yc, axis=-1, keepdims=True)
        yn = yc * lax.rsqrt(var + EPS) * lg_ref[...] + lb_ref[...]
        o_ref[0, s:s + CONV_SUB, :] = _silu(yn).astype(o_ref.dtype)


def conformer_conv(proj3, w, b, ln_g, ln_b, tt):
    bsz, seq, _ = proj3.shape
    n_t = seq // tt
    hb = tt // CONV_PAD
    last = seq // CONV_PAD - 1
    vec = pl.BlockSpec((1, HALF_MIX), lambda i, t: (0, 0))

    def main(col):
        return pl.BlockSpec((1, tt, HALF_MIX), lambda i, t: (i, t, col))

    def prev(col):
        return pl.BlockSpec((1, CONV_PAD, HALF_MIX), lambda i, t: (i, jnp.maximum(t * hb - 1, 0), col))

    def nxt(col):
        return pl.BlockSpec((1, CONV_PAD, HALF_MIX), lambda i, t: (i, jnp.minimum((t + 1) * hb, last), col))

    return pl.pallas_call(
        _conv_kernel,
        grid=(bsz, n_t),
        in_specs=[main(3), main(4), prev(3), prev(4), nxt(3), nxt(4),
                  pl.BlockSpec((CONV_WIDTH, HALF_MIX), lambda i, t: (0, 0)),
                  vec, vec, vec],
        out_specs=pl.BlockSpec((1, tt, HALF_MIX), lambda i, t: (i, t, 0)),
        out_shape=jax.ShapeDtypeStruct((bsz, seq, HALF_MIX), BF16),
        scratch_shapes=[pltpu.VMEM((tt + 2 * CONV_PAD, HALF_MIX), F32),
                        pltpu.VMEM((SUBLANES, tt + 2 * CONV_PAD - SUBLANES, HALF_MIX), F32)],
        compiler_params=_params("arbitrary", "arbitrary"),
        name="conformer_conv",
    )(proj3, proj3, proj3, proj3, proj3, proj3, w, b.reshape(1, HALF_MIX), ln_g.reshape(1, HALF_MIX),
      ln_b.reshape(1, HALF_MIX))


def _mix_ffn_kernel(x_ref, m1_ref, m2_ref, w1_ref, w2_ref, g1_ref, g_ref, sh_ref, sc_ref, g2_ref,
                    wg_ref, wu_ref, wd_ref, o_ref, x1_scr, h_scr, acc_scr):
    f = pl.program_id(1)

    @pl.when(f == 0)
    def _():
        mix = _dot(m1_ref[...], w1_ref[...]) + _dot(m2_ref[...], w2_ref[...])
        x1 = x_ref[...] + g1_ref[0] * mix
        x1_scr[...] = x1
        h_scr[...] = _norm_mod(x1, g_ref[...], sh_ref[0], sc_ref[0]).astype(BF16)
        acc_scr[...] = jnp.zeros_like(acc_scr)

    h = h_scr[...]
    a = _dot(h, wg_ref[...])
    u = _dot(h, wu_ref[...])
    acc_scr[...] += _dot((_silu(a) * u).astype(BF16), wd_ref[...])

    @pl.when(f == pl.num_programs(1) - 1)
    def _():
        o_ref[...] = x1_scr[...] + g2_ref[0] * acc_scr[...]


def mix_ffn(x, m1, m2, w_out, g, mod, tiles_per_row, w_gate, w_up, w_down, tm=512, tf=1408):
    t = x.shape[0]
    fdim = w_gate.shape[1]
    half = pl.BlockSpec((tm, HALF_MIX), lambda i, f: (i, 0))
    return pl.pallas_call(
        _mix_ffn_kernel,
        grid=(t // tm, fdim // tf),
        in_specs=[pl.BlockSpec((tm, D_MODEL), lambda i, f: (i, 0)), half, half,
                  pl.BlockSpec((HALF_MIX, D_MODEL), lambda i, f: (0, 0)),
                  pl.BlockSpec((HALF_MIX, D_MODEL), lambda i, f: (1, 0)),
                  _mod_spec(2, tiles_per_row),
                  pl.BlockSpec((1, D_MODEL), lambda i, f: (0, 0)),
                  _mod_spec(3, tiles_per_row), _mod_spec(4, tiles_per_row), _mod_spec(5, tiles_per_row),
                  pl.BlockSpec((D_MODEL, tf), lambda i, f: (0, f)),
                  pl.BlockSpec((D_MODEL, tf), lambda i, f: (0, f)),
                  pl.BlockSpec((tf, D_MODEL), lambda i, f: (f, 0))],
        out_specs=pl.BlockSpec((tm, D_MODEL), lambda i, f: (i, 0)),
        out_shape=jax.ShapeDtypeStruct((t, D_MODEL), F32),
        scratch_shapes=[pltpu.VMEM((tm, D_MODEL), F32), pltpu.VMEM((tm, D_MODEL), BF16),
                        pltpu.VMEM((tm, D_MODEL), F32)],
        compiler_params=_params("arbitrary", "arbitrary"),
        name="mix_ffn",
    )(x, m1, m2, w_out, w_out, mod, g.reshape(1, D_MODEL), mod, mod, mod, w_gate, w_up, w_down)


MOE_ROW_TILE = 1024
MOE_ZERO_ROWS = 256
INFO_E0, INFO_E1, INFO_G0, INFO_G1, INFO_R0, INFO_R1 = range(6)


def _stream_mod_spec(chunk, prompt_tiles, tiles_per_req, ctx_row):
    def index(i, *_):
        return (jnp.where(i < prompt_tiles, ctx_row, (i - prompt_tiles) // tiles_per_req), 0, chunk)
    return pl.BlockSpec((1, 1, D_MODEL), index)


def _mix_out2_kernel(xp_ref, m1p_ref, m2p_ref, xs_ref, m1s_ref, m2s_ref, w1_ref, w2_ref, gate_ref, o_ref,
                     *, prompt_tiles):
    i = pl.program_id(0)
    w1 = w1_ref[...]
    w2 = w2_ref[...]

    @pl.when(i < prompt_tiles)
    def _():
        o_ref[...] = xp_ref[...] + gate_ref[0] * (_dot(m1p_ref[...], w1) + _dot(m2p_ref[...], w2))

    @pl.when(i >= prompt_tiles)
    def _():
        o_ref[...] = xs_ref[...] + gate_ref[0] * (_dot(m1s_ref[...], w1) + _dot(m2s_ref[...], w2))


def mix_out2(xp, m1p, m2p, xs, m1s, m2s, w_out, mod_all, ctx_row, req_tokens, tm=512):
    tp, ts = xp.shape[0], xs.shape[0]
    pt = tp // tm

    def p_spec(width):
        return pl.BlockSpec((tm, width), lambda i: (jnp.minimum(i, pt - 1), 0))

    def s_spec(width):
        return pl.BlockSpec((tm, width), lambda i: (jnp.maximum(i - pt, 0), 0))

    return pl.pallas_call(
        functools.partial(_mix_out2_kernel, prompt_tiles=pt),
        grid=((tp + ts) // tm,),
        in_specs=[p_spec(D_MODEL), p_spec(HALF_MIX), p_spec(HALF_MIX),
                  s_spec(D_MODEL), s_spec(HALF_MIX), s_spec(HALF_MIX),
                  pl.BlockSpec((HALF_MIX, D_MODEL), lambda i: (0, 0)),
                  pl.BlockSpec((HALF_MIX, D_MODEL), lambda i: (1, 0)),
                  _stream_mod_spec(2, pt, req_tokens // tm, ctx_row)],
        out_specs=pl.BlockSpec((tm, D_MODEL), lambda i: (i, 0)),
        out_shape=jax.ShapeDtypeStruct((tp + ts, D_MODEL), F32),
        compiler_params=_params("arbitrary"),
        name="mix_out2",
    )(xp, m1p, m2p, xs, m1s, m2s, w_out, w_out, mod_all)


def _route_kernel(x_ref, g_ref, sh_ref, sc_ref, rw_ref, info_ref, cnt_ref, tri_scr, run_scr):
    i = pl.program_id(0)
    tm = x_ref.shape[0]

    @pl.when(i == 0)
    def _():
        r = lax.broadcasted_iota(jnp.int32, (tm, tm), 0)
        c = lax.broadcasted_iota(jnp.int32, (tm, tm), 1)
        tri_scr[...] = jnp.where(c < r, 1.0, 0.0).astype(BF16)
        run_scr[...] = jnp.zeros_like(run_scr)

    h = _norm_mod(x_ref[...], g_ref[...], sh_ref[0], sc_ref[0])
    logits = jnp.dot(h, rw_ref[...], precision=lax.Precision.HIGHEST, preferred_element_type=F32)
    lane = lax.broadcasted_iota(jnp.int32, logits.shape, 1).astype(F32)
    logits = jnp.where(lane < N_EXPERTS, logits, -jnp.inf)
    m1 = jnp.max(logits, axis=-1, keepdims=True)
    i1 = jnp.min(jnp.where(logits == m1, lane, float(LANES)), axis=-1, keepdims=True)
    rest = jnp.where(lane == i1, -jnp.inf, logits)
    m2 = jnp.max(rest, axis=-1, keepdims=True)
    i2 = jnp.min(jnp.where(rest == m2, lane, float(LANES)), axis=-1, keepdims=True)
    e2 = jnp.exp(m2 - m1)
    den = 1.0 + e2
    hit = jnp.where(lane == i1, 1.0, 0.0) + jnp.where(lane == i2, 1.0, 0.0)
    before = _dot(tri_scr[...], hit.astype(BF16)) + run_scr[0:1, :]
    r1 = jnp.sum(jnp.where(lane == i1, before, 0.0), axis=-1, keepdims=True)
    r2 = jnp.sum(jnp.where(lane == i2, before, 0.0), axis=-1, keepdims=True)
    info = jnp.zeros_like(logits)
    for slot, val in ((INFO_E0, i1), (INFO_E1, i2), (INFO_G0, 1.0 / den), (INFO_G1, e2 / den),
                      (INFO_R0, r1), (INFO_R1, r2)):
        info = jnp.where(lane == float(slot), val, info)
    info_ref[...] = info
    run_scr[...] = run_scr[...] + jnp.sum(hit, axis=0, keepdims=True)
    cnt_ref[...] = run_scr[...]


def moe_route(x, g, mod_all, ctx_row, prompt_tokens, req_tokens, router_w, tm=1024):
    t = x.shape[0]
    rw = jnp.pad(router_w, ((0, 0), (0, LANES - N_EXPERTS)))
    pt = prompt_tokens // tm
    return pl.pallas_call(
        _route_kernel,
        grid=(t // tm,),
        in_specs=[pl.BlockSpec((tm, D_MODEL), lambda i: (i, 0)),
                  pl.BlockSpec((1, D_MODEL), lambda i: (0, 0)),
                  _stream_mod_spec(3, pt, req_tokens // tm, ctx_row),
                  _stream_mod_spec(4, pt, req_tokens // tm, ctx_row),
                  pl.BlockSpec((D_MODEL, LANES), lambda i: (0, 0))],
        out_specs=[pl.BlockSpec((tm, LANES), lambda i: (i, 0)),
                   pl.BlockSpec((SUBLANES, LANES), lambda i: (0, 0))],
        out_shape=[jax.ShapeDtypeStruct((t, LANES), F32), jax.ShapeDtypeStruct((SUBLANES, LANES), F32)],
        scratch_shapes=[pltpu.VMEM((tm, tm), BF16), pltpu.VMEM((SUBLANES, LANES), F32)],
        compiler_params=_params("arbitrary"),
        name="moe_route",
    )(x, g.reshape(1, D_MODEL), mod_all, mod_all, rw)


def _dispatch_kernel(d0_ref, d1_ref, fill_ref, x_ref, g_ref, sh_ref, sc_ref, xs_hbm, h_scr, zero_scr, sem, zsem):
    i = pl.program_id(0)
    n = pl.num_programs(0)
    tm = x_ref.shape[0]
    slot = i % 2

    def row_copy(r, dst, s):
        return pltpu.make_async_copy(h_scr.at[s, pl.ds(r, 1), :], xs_hbm.at[pl.ds(dst, 1), :], sem.at[s])

    def wait_rows(s):
        for _ in range(2):
            pltpu.make_async_copy(h_scr.at[s], xs_hbm.at[pl.ds(0, tm), :], sem.at[s]).wait()

    @pl.when(i >= 2)
    def _():
        wait_rows(slot)

    h_scr[slot] = _norm_mod(x_ref[...], g_ref[...], sh_ref[0], sc_ref[0])
    base = i * tm

    def body(r8, c):
        rb = pl.multiple_of(r8 * SUBLANES, SUBLANES)
        for k in range(SUBLANES):
            row_copy(rb + k, d0_ref[base + rb + k], slot).start()
            row_copy(rb + k, d1_ref[base + rb + k], slot).start()
        return c

    lax.fori_loop(0, tm // SUBLANES, body, 0)

    @pl.when(i == n - 1)
    def _():
        zero_scr[...] = jnp.zeros_like(zero_scr)

        def zero_row(r):
            return pltpu.make_async_copy(zero_scr.at[pl.ds(0, 1), :], xs_hbm.at[pl.ds(r, 1), :], zsem)

        def zero_block(b):
            start = pl.multiple_of(b * MOE_ZERO_ROWS, MOE_ZERO_ROWS)
            return pltpu.make_async_copy(zero_scr, xs_hbm.at[pl.ds(start, MOE_ZERO_ROWS), :], zsem)

        def start_all(copy):
            def body(r, c):
                copy(r).start()
                return c
            return body

        def wait_all(copy):
            def body(r, c):
                copy(r).wait()
                return c
            return body

        for e in range(N_EXPERTS):
            lax.fori_loop(fill_ref[e], fill_ref[N_EXPERTS + e], start_all(zero_row), 0)
            lax.fori_loop(fill_ref[e], fill_ref[N_EXPERTS + e], wait_all(zero_row), 0)
        blocks_per_tile = MOE_ROW_TILE // MOE_ZERO_ROWS
        first = fill_ref[2 * N_EXPERTS] * blocks_per_tile
        last = (xs_hbm.shape[0] // MOE_ROW_TILE) * blocks_per_tile
        lax.fori_loop(first, last, start_all(zero_block), 0)
        lax.fori_loop(first, last, wait_all(zero_block), 0)
        wait_rows(slot)

        @pl.when(n >= 2)
        def _():
            wait_rows(1 - slot)


def moe_dispatch(x, g, mod_all, ctx_row, prompt_tokens, req_tokens, dest0, dest1, fill, n_tiles, tm=512):
    t = x.shape[0]
    pt = prompt_tokens // tm
    grid_spec = pltpu.PrefetchScalarGridSpec(
        num_scalar_prefetch=3,
        grid=(t // tm,),
        in_specs=[pl.BlockSpec((tm, D_MODEL), lambda i, *_: (i, 0)),
                  pl.BlockSpec((1, D_MODEL), lambda i, *_: (0, 0)),
                  _stream_mod_spec(3, pt, req_tokens // tm, ctx_row),
                  _stream_mod_spec(4, pt, req_tokens // tm, ctx_row)],
        out_specs=pl.BlockSpec(memory_space=pl.ANY),
        scratch_shapes=[pltpu.VMEM((2, tm, D_MODEL), F32), pltpu.VMEM((MOE_ZERO_ROWS, D_MODEL), F32),
                        pltpu.SemaphoreType.DMA((2,)), pltpu.SemaphoreType.DMA(())],
    )
    return pl.pallas_call(
        _dispatch_kernel,
        grid_spec=grid_spec,
        out_shape=jax.ShapeDtypeStruct((n_tiles * MOE_ROW_TILE, D_MODEL), F32),
        compiler_params=pltpu.CompilerParams(dimension_semantics=("arbitrary",),
                                             vmem_limit_bytes=VMEM_LIMIT_BYTES,
                                             disable_bounds_checks=True),
        name="moe_dispatch",
    )(dest0, dest1, fill, x, g.reshape(1, D_MODEL), mod_all, mod_all)


def _experts_kernel(te_ref, tv_ref, x_ref, wg_ref, wu_ref, wd_ref, o_ref, h_scr, acc_scr):
    i = pl.program_id(0)
    f = pl.program_id(1)
    last_f = pl.num_programs(1) - 1

    @pl.when((tv_ref[i] == 0) & (f == last_f))
    def _():
        o_ref[...] = jnp.zeros_like(o_ref)

    @pl.when(tv_ref[i] > 0)
    def _():
        @pl.when(f == 0)
        def _():
            h_scr[...] = x_ref[...].astype(BF16)
            acc_scr[...] = jnp.zeros_like(acc_scr)

        h = h_scr[...]
        a = _dot(h, wg_ref[0].astype(BF16))
        u = _dot(h, wu_ref[0].astype(BF16))
        acc_scr[...] += _dot((_silu(a) * u).astype(BF16), wd_ref[0].astype(BF16))

        @pl.when(f == last_f)
        def _():
            o_ref[...] = acc_scr[...]


def moe_experts(xs_sorted, tile_expert, tile_valid, w_gate, w_up, w_down, tf=512):
    rows = xs_sorted.shape[0]
    fdim = w_gate.shape[2]
    n_f = fdim // tf
    tr = MOE_ROW_TILE

    def f_eff(i, f, tv):
        return jnp.where(tv[i] > 0, f, n_f - 1)

    grid_spec = pltpu.PrefetchScalarGridSpec(
        num_scalar_prefetch=2,
        grid=(rows // tr, n_f),
        in_specs=[pl.BlockSpec((tr, D_MODEL), lambda i, f, te, tv: (i, 0)),
                  pl.BlockSpec((1, D_MODEL, tf), lambda i, f, te, tv: (te[i], 0, f_eff(i, f, tv))),
                  pl.BlockSpec((1, D_MODEL, tf), lambda i, f, te, tv: (te[i], 0, f_eff(i, f, tv))),
                  pl.BlockSpec((1, tf, D_MODEL), lambda i, f, te, tv: (te[i], f_eff(i, f, tv), 0))],
        out_specs=pl.BlockSpec((tr, D_MODEL), lambda i, f, te, tv: (i, 0)),
        scratch_shapes=[pltpu.VMEM((tr, D_MODEL), BF16), pltpu.VMEM((tr, D_MODEL), F32)],
    )
    return pl.pallas_call(
        _experts_kernel,
        grid_spec=grid_spec,
        out_shape=jax.ShapeDtypeStruct((rows, D_MODEL), F32),
        compiler_params=_params("arbitrary", "arbitrary"),
        name="moe_experts",
    )(tile_expert, tile_valid, xs_sorted, w_gate, w_up, w_down)


def _combine_kernel(d0_ref, d1_ref, x_ref, info_ref, gate_ref, fg_ref, ys_hbm, op_ref, os_ref, rbuf, sem,
                    *, prompt_tiles):
    i = pl.program_id(0)
    n = pl.num_programs(0)
    tm = x_ref.shape[0]
    slot = i % 2

    def issue(tile, s):
        base = tile * tm

        def body(r8, c):
            rb = pl.multiple_of(r8 * SUBLANES, SUBLANES)
            for k in range(SUBLANES):
                pltpu.make_async_copy(ys_hbm.at[pl.ds(d0_ref[base + rb + k], 1), :],
                                      rbuf.at[s, 0, pl.ds(rb + k, 1), :], sem.at[s]).start()
                pltpu.make_async_copy(ys_hbm.at[pl.ds(d1_ref[base + rb + k], 1), :],
                                      rbuf.at[s, 1, pl.ds(rb + k, 1), :], sem.at[s]).start()
            return c

        lax.fori_loop(0, tm // SUBLANES, body, 0)

    @pl.when(i == 0)
    def _():
        issue(0, 0)

    @pl.when(i + 1 < n)
    def _():
        issue(i + 1, 1 - slot)

    for k in range(2):
        pltpu.make_async_copy(ys_hbm.at[pl.ds(0, tm), :], rbuf.at[slot, k], sem.at[slot]).wait()

    info = info_ref[...]
    moe = info[:, INFO_G0:INFO_G0 + 1] * rbuf[slot, 0] + info[:, INFO_G1:INFO_G1 + 1] * rbuf[slot, 1]
    y = x_ref[...] + gate_ref[0] * moe
    out = y * lax.rsqrt(jnp.mean(y * y, axis=-1, keepdims=True) + EPS) * fg_ref[...]

    @pl.when(i < prompt_tiles)
    def _():
        op_ref[...] = out

    @pl.when(i >= prompt_tiles)
    def _():
        os_ref[...] = out


def moe_combine(x, info, ys_sorted, dest0, dest1, mod_all, ctx_row, prompt_tokens, req_tokens, final_g, tm=256):
    t = x.shape[0]
    pt = prompt_tokens // tm
    grid_spec = pltpu.PrefetchScalarGridSpec(
        num_scalar_prefetch=2,
        grid=(t // tm,),
        in_specs=[pl.BlockSpec((tm, D_MODEL), lambda i, *_: (i, 0)),
                  pl.BlockSpec((tm, LANES), lambda i, *_: (i, 0)),
                  _stream_mod_spec(5, pt, req_tokens // tm, ctx_row),
                  pl.BlockSpec((1, D_MODEL), lambda i, *_: (0, 0)),
                  pl.BlockSpec(memory_space=pl.ANY)],
        out_specs=[pl.BlockSpec((tm, D_MODEL), lambda i, *_: (jnp.minimum(i, pt - 1), 0)),
                   pl.BlockSpec((tm, D_MODEL), lambda i, *_: (jnp.maximum(i - pt, 0), 0))],
        scratch_shapes=[pltpu.VMEM((2, 2, tm, D_MODEL), F32), pltpu.SemaphoreType.DMA((2,))],
    )
    return pl.pallas_call(
        functools.partial(_combine_kernel, prompt_tiles=pt),
        grid_spec=grid_spec,
        out_shape=[jax.ShapeDtypeStruct((prompt_tokens, D_MODEL), F32),
                   jax.ShapeDtypeStruct((t - prompt_tokens, D_MODEL), F32)],
        compiler_params=pltpu.CompilerParams(dimension_semantics=("arbitrary",),
                                             vmem_limit_bytes=VMEM_LIMIT_BYTES,
                                             disable_bounds_checks=True),
        name="moe_combine",
    )(dest0, dest1, x, info, mod_all, final_g.reshape(1, D_MODEL), ys_sorted)


def moe_layout(counts, info, n_tiles):
    nt_e = (counts + MOE_ROW_TILE - 1) // MOE_ROW_TILE
    ends = jnp.cumsum(nt_e)
    total = ends[-1]
    offset = (ends - nt_e) * MOE_ROW_TILE
    experts = jnp.arange(N_EXPERTS, dtype=jnp.int32)

    def dest(e_lane, r_lane):
        e = info[:, e_lane].astype(jnp.int32)
        off = jnp.sum(jnp.where(e[:, None] == experts[None, :], offset[None, :], 0), axis=1)
        return (off + info[:, r_lane].astype(jnp.int32)).astype(jnp.int32)

    fill = jnp.concatenate([offset + counts, ends * MOE_ROW_TILE, total[None]]).astype(jnp.int32)
    ids = jnp.arange(n_tiles, dtype=jnp.int32)
    ids_c = jnp.minimum(ids, total - 1)
    te = jnp.sum((ids_c[:, None] >= ends[None, :]).astype(jnp.int32), axis=1)
    return (dest(INFO_E0, INFO_R0), dest(INFO_E1, INFO_R1), fill, te.astype(jnp.int32),
            (ids < total).astype(jnp.int32))


def moe_final(x, g, mod_all, ctx_row, prompt_tokens, req_tokens, router_w, final_g, w_gate, w_up, w_down):
    t = x.shape[0]
    info, cnt = moe_route(x, g, mod_all, ctx_row, prompt_tokens, req_tokens, router_w)
    counts = cnt[0, :N_EXPERTS].astype(jnp.int32)
    n_tiles = (2 * t) // MOE_ROW_TILE + N_EXPERTS
    dest0, dest1, fill, te, tv = moe_layout(counts, info, n_tiles)
    xs_sorted = moe_dispatch(x, g, mod_all, ctx_row, prompt_tokens, req_tokens, dest0, dest1, fill, n_tiles)
    ys_sorted = moe_experts(xs_sorted, te, tv, w_gate, w_up, w_down)
    return moe_combine(x, info, ys_sorted, dest0, dest1, mod_all, ctx_row, prompt_tokens, req_tokens, final_g)


def kernel(x_prompt, x_sample, state_s5, cache_na_k, cache_na_v, cache_diff_k, cache_diff_v, c, c_ctx, w_mod, b_mod, norm_mix_g, norm_ffn_g, final_norm_g, w_in_e, w_out_e, s5_lam_re, s5_lam_im, s5_log_dt, s5_b_re, s5_b_im, s5_c_re, s5_c_im, s5_d, s5_w_glu, s5_b_glu, na_rpb, ffn_w_gate, ffn_w_up, ffn_w_down, w_in_o, w_out_o, diff_lam_q1, diff_lam_k1, diff_lam_q2, diff_lam_k2, diff_subln_g, conv_w, conv_b, conv_ln_g, conv_ln_b, router_w, moe_w_gate, moe_w_up, moe_w_down):
    bp, lp, d = x_prompt.shape
    bs, ls, _ = x_sample.shape
    tm = 1024
    xp = x_prompt.reshape(bp * lp, d)
    xs = x_sample.reshape(bs * ls, d)
    rows_p = (bp * lp) // tm
    rows_s = ls // tm

    cond8 = jnp.concatenate([c, c_ctx[None, :], jnp.zeros((SUBLANES - bs - 1, d), F32)], axis=0)
    mod = adaln_all(cond8, w_mod, b_mod)
    mod_s = mod[:, 0:bs, None, :]
    mod_p = mod[:, bs:bs + 1, None, :]

    def tiles(rows, tile):
        return rows * tm // tile

    bmat, cmat, lam8 = s5_params(s5_lam_re[0], s5_lam_im[0], s5_log_dt[0], s5_b_re[0], s5_b_im[0],
                                 s5_c_re[0], s5_c_im[0])
    bias = na_bias_blocks(na_rpb[0])
    n_e = w_in_e.shape[-1]

    tiles_p = tiles(rows_p, 512)
    tiles_s = tiles(rows_s, 512)
    proj_p = in_proj(xp, norm_mix_g[0], mod_p[0], tiles_p, w_in_e[0])
    u_s, qkv_s = in_proj(xs, norm_mix_g[0], mod_s[0], tiles_s, w_in_e[0], n_f32=HALF_MIX)

    y_p, st_p = s5_scan(proj_p.reshape(bp, lp, n_e), bmat, cmat, lam8, None, 1)
    chunks = SUBLANES // bs
    h0 = state_s5[:, 0].reshape(bs, 2, 2, S5_GROUPS * S5_STATE)
    y_s, _ = s5_scan(u_s.reshape(bs * chunks, ls // chunks, HALF_MIX), bmat, cmat, lam8, h0, chunks)
    s5o_p = s5_glu(y_p.reshape(bp * lp, HALF_MIX), proj_p, s5_d[0], s5_w_glu[0], s5_b_glu[0])
    s5o_s = s5_glu(y_s.reshape(bs * ls, HALF_MIX), u_s, s5_d[0], s5_w_glu[0], s5_b_glu[0])

    nao_p, na_k, na_v = na_ctx(proj_p.reshape(bp, lp, n_e))

    def heads_to_lanes(cache):
        return cache.transpose(0, 2, 1, 3).reshape(bs, cache.shape[2], HALF_MIX).astype(BF16)

    nao_s = na_lat(qkv_s.reshape(bs, ls, n_e - HALF_MIX), heads_to_lanes(cache_na_k[:, 0]),
                   heads_to_lanes(cache_na_v[:, 0]), bias)

    ffn_w = (w_out_e[0].astype(BF16), norm_ffn_g[0])
    ffn_w3 = (ffn_w_gate[0].astype(BF16), ffn_w_up[0].astype(BF16), ffn_w_down[0].astype(BF16))
    xp = mix_ffn(xp, s5o_p, nao_p.reshape(bp * lp, HALF_MIX), *ffn_w, mod_p[0], tiles_p, *ffn_w3)
    xs = mix_ffn(xs, s5o_s, nao_s.reshape(bs * ls, HALF_MIX), *ffn_w, mod_s[0], tiles_s, *ffn_w3)

    lam_init = 0.8 - 0.6 * math.exp(-0.3 * 1)
    lam = (jnp.exp(jnp.sum(diff_lam_q1[0].astype(F32) * diff_lam_k1[0].astype(F32)))
           - jnp.exp(jnp.sum(diff_lam_q2[0].astype(F32) * diff_lam_k2[0].astype(F32)))
           + lam_init)
    cos, sin = rope_tables(ls)
    n_o = w_in_o.shape[-1]

    proj_p = in_proj(xp, norm_mix_g[1], mod_p[1], tiles_p, w_in_o[0])
    proj_s = in_proj(xs, norm_mix_g[1], mod_s[1], tiles_s, w_in_o[0])

    do_p, diff_k, diff_v = diff_ctx(proj_p.reshape(bp, lp, n_o), lam, diff_subln_g[0], lam_init)
    do_s = diff_lat(proj_s.reshape(bs, ls, n_o), cache_diff_k[:, 0:1], cache_diff_v[:, 0:1], cos, sin,
                    lam, diff_subln_g[0], lam_init)
    co_p = conformer_conv(proj_p.reshape(bp, lp, n_o), conv_w[0], conv_b[0], conv_ln_g[0], conv_ln_b[0], lp)
    co_s = conformer_conv(proj_s.reshape(bs, ls, n_o), conv_w[0], conv_b[0], conv_ln_g[0], conv_ln_b[0], 512)

    mod_all = mod[1][:, None, :]
    x_all = mix_out2(xp, do_p.reshape(bp * lp, HALF_MIX), co_p.reshape(bp * lp, HALF_MIX),
                     xs, do_s.reshape(bs * ls, HALF_MIX), co_s.reshape(bs * ls, HALF_MIX),
                     w_out_o[0].astype(BF16), mod_all, bs, ls)
    yp, ys = moe_final(x_all, norm_ffn_g[1], mod_all, bs, bp * lp, ls, router_w[0], final_norm_g,
                       moe_w_gate[0], moe_w_up[0], moe_w_down[0])

    new_state = st_p.reshape(bp, 1, 2, 2, S5_GROUPS, S5_STATE)
    return (yp.reshape(bp, lp, d), ys.reshape(bs, ls, d), new_state, na_k, na_v, diff_k, diff_v)
```

```python
import functools
import math

import jax
import jax.numpy as jnp
import numpy as np
from jax import lax
from jax.experimental import pallas as pl
from jax.experimental.pallas import tpu as pltpu

D_MODEL = 1024
DEPTH = 2
GRID_W = 64
HALF_MIX = 512
S5_GROUP_CH = 16
S5_GROUPS = 32
S5_STATE = 64
NA_HEAD_DIM = 64
NA_HEADS = 8
NA_WIN_R = 8
NA_WIN_C = 16
DIFF_D = 64
DIFF_HEAD_DIM = 128
DIFF_HEADS = 4
ROPE_BASE = 10000.0
CONV_WIDTH = 31
N_EXPERTS = 8
EPS = 1e-6

F32 = jnp.float32
BF16 = jnp.bfloat16
NEG_BIG = -1e30

VMEM_LIMIT_BYTES = 56 * 1024 * 1024
LANES = 128
SUBLANES = 8

S5_COL_GROUPS = 8
S5_COL_CH = S5_COL_GROUPS * S5_GROUP_CH
S5_COL_STATE = S5_COL_GROUPS * S5_STATE
S5_N_COL = S5_GROUPS // S5_COL_GROUPS
S5_TIME_BLOCK = 256


def _params(*sem):
    return pltpu.CompilerParams(dimension_semantics=sem, vmem_limit_bytes=VMEM_LIMIT_BYTES)


def _dot(a, b):
    return jnp.dot(a, b, preferred_element_type=F32)


def _dot_nt(a, b):
    return lax.dot_general(a, b, (((1,), (1,)), ((), ())), preferred_element_type=F32)


def _silu(x):
    return x * jax.nn.sigmoid(x)


def _norm_mod(x, g, shift, scale):
    y = x * lax.rsqrt(jnp.mean(x * x, axis=-1, keepdims=True) + EPS) * g
    return y * (1.0 + scale) + shift


def _mod_kernel(cond_ref, w_ref, b_ref, o_ref):
    s = _silu(cond_ref[...])
    o_ref[0] = jnp.dot(s, w_ref[0], precision=lax.Precision.HIGHEST,
                       preferred_element_type=F32) + b_ref[0]


def adaln_all(cond8, w_mod, b_mod):
    tn = 1536
    n = w_mod.shape[-1]
    return pl.pallas_call(
        _mod_kernel,
        grid=(DEPTH, n // tn),
        in_specs=[pl.BlockSpec((SUBLANES, D_MODEL), lambda l, j: (0, 0)),
                  pl.BlockSpec((1, D_MODEL, tn), lambda l, j: (l, 0, j)),
                  pl.BlockSpec((1, 1, tn), lambda l, j: (l, 0, j))],
        out_specs=pl.BlockSpec((1, SUBLANES, tn), lambda l, j: (l, 0, j)),
        out_shape=jax.ShapeDtypeStruct((DEPTH, SUBLANES, n), F32),
        compiler_params=_params("arbitrary", "arbitrary"),
        name="adaln_mod",
    )(cond8, w_mod, b_mod.reshape(DEPTH, 1, n))


def _mod_spec(chunk, tiles_per_row):
    return pl.BlockSpec((1, 1, D_MODEL), lambda i, *_: (i // tiles_per_row, 0, chunk))


def _in_proj_kernel(x_ref, g_ref, sh_ref, sc_ref, w_ref, *refs, n_f32):
    *outs, wb_scr = refs

    @pl.when(pl.program_id(0) == 0)
    def _():
        wb_scr[...] = w_ref[...].astype(BF16)

    h = _norm_mod(x_ref[...], g_ref[...], sh_ref[0], sc_ref[0]).astype(BF16)
    y = _dot(h, wb_scr[...])
    if len(outs) == 1:
        outs[0][...] = y
    else:
        outs[0][...] = y[:, :n_f32]
        outs[1][...] = y[:, n_f32:].astype(BF16)


def in_proj(x, g, mod, tiles_per_row, w, n_f32=None, tm=512):
    t = x.shape[0]
    n = w.shape[1]
    if n_f32 is None:
        out_specs = pl.BlockSpec((tm, n), lambda i: (i, 0))
        out_shape = jax.ShapeDtypeStruct((t, n), F32)
    else:
        out_specs = [pl.BlockSpec((tm, n_f32), lambda i: (i, 0)), pl.BlockSpec((tm, n - n_f32), lambda i: (i, 0))]
        out_shape = [jax.ShapeDtypeStruct((t, n_f32), F32), jax.ShapeDtypeStruct((t, n - n_f32), BF16)]
    return pl.pallas_call(
        functools.partial(_in_proj_kernel, n_f32=n_f32),
        grid=(t // tm,),
        in_specs=[pl.BlockSpec((tm, D_MODEL), lambda i: (i, 0)),
                  pl.BlockSpec((1, D_MODEL), lambda i: (0, 0)),
                  _mod_spec(0, tiles_per_row),
                  _mod_spec(1, tiles_per_row),
                  pl.BlockSpec((D_MODEL, n), lambda i: (0, 0), pipeline_mode=pl.Buffered(1))],
        out_specs=out_specs,
        out_shape=out_shape,
        scratch_shapes=[pltpu.VMEM((D_MODEL, n), BF16)],
        compiler_params=_params("arbitrary"),
        name="in_proj",
    )(x, g.reshape(1, D_MODEL), mod, mod, w)


def _s5_scan_kernel(*refs, seq, chunks, has_init):
    if has_init:
        u_ref, bm_ref, cm_ref, lam_ref, h0_ref, y_ref, st_ref, bu_scr, hb_scr, ytm_scr = refs
    else:
        u_ref, bm_ref, cm_ref, lam_ref, y_ref, st_ref, bu_scr, hb_scr, ytm_scr = refs
        h0_ref = None
    tb = S5_TIME_BLOCK
    n_tb = seq // tb
    ns = S5_COL_STATE
    row = lax.broadcasted_iota(jnp.int32, (SUBLANES, ns), 0)
    piece = row % chunks

    for d in range(2):
        lam = lam_ref[d, 0]
        lr, li = lam[:, :ns], lam[:, ns:]
        bm = bm_ref[d, 0]
        cm = cm_ref[d, 0]
        blocks = list(range(n_tb)) if d == 0 else list(range(n_tb - 1, -1, -1))

        def load_bu(k):
            ub = u_ref[:, k * tb:(k + 1) * tb, :]
            utm = jnp.swapaxes(ub, 0, 1).reshape(tb * SUBLANES, S5_COL_CH).astype(BF16)
            bu_scr[...] = _dot(utm, bm).reshape(tb, SUBLANES, 2 * ns)

        def scan_block(h, store):
            def advance(t, hr, hi):
                b = bu_scr[t]
                return lr * hr - li * hi + b[:, :ns], lr * hi + li * hr + b[:, ns:]

            def step(s, carry):
                t_a = (tb - 1 - 2 * s) if d == 1 else 2 * s
                t_b = t_a - 1 if d == 1 else t_a + 1
                ar, ai = advance(t_a, *carry)
                br, bi = advance(t_b, ar, ai)
                if store:
                    first = jnp.concatenate([br, bi] if d == 1 else [ar, ai], axis=1)
                    second = jnp.concatenate([ar, ai] if d == 1 else [br, bi], axis=1)
                    t_lo = t_b if d == 1 else t_a
                    rows = pl.ds(pl.multiple_of(t_lo * SUBLANES, 2 * SUBLANES), 2 * SUBLANES)
                    hb_scr[rows, :] = jnp.concatenate([first, second], axis=0).astype(BF16)
                return br, bi
            return lax.fori_loop(0, tb // 2, step, h)

        zero = jnp.zeros((SUBLANES, ns), F32)
        if chunks > 1:
            h = (zero, zero)
            for k in blocks:
                load_bu(k)
                h = scan_block(h, False)
            fr, fi = h
            pr, pi = lr, li
            for _ in range(int(math.log2(seq))):
                pr, pi = pr * pr - pi * pi, 2.0 * pr * pi
            edge = 0 if d == 0 else chunks - 1
            shift = 1 if d == 0 else SUBLANES - 1
            if has_init:
                h0r = h0_ref[:, d, 0, :]
                h0i = h0_ref[:, d, 1, :]
                seq_of_row = row // chunks
                er, ei = zero, zero
                for b in range(SUBLANES // chunks):
                    er = jnp.where(seq_of_row == b, h0r[b:b + 1, :], er)
                    ei = jnp.where(seq_of_row == b, h0i[b:b + 1, :], ei)
            else:
                er, ei = zero, zero
            is_edge = piece == edge
            cr = jnp.where(is_edge, er, zero)
            ci = jnp.where(is_edge, ei, zero)
            for _ in range(chunks - 1):
                tr = fr + pr * cr - pi * ci
                ti = fi + pr * ci + pi * cr
                cr = jnp.where(is_edge, er, pltpu.roll(tr, shift, 0))
                ci = jnp.where(is_edge, ei, pltpu.roll(ti, shift, 0))
            h = (cr, ci)
        else:
            if has_init:
                h = (h0_ref[:, d, 0, :], h0_ref[:, d, 1, :])
            else:
                h = (zero, zero)

        for k in blocks:
            load_bu(k)
            h = scan_block(h, True)
            yb = _dot(hb_scr[...], cm).reshape(tb, SUBLANES, S5_COL_CH)
            if d == 0:
                ytm_scr[k * tb:(k + 1) * tb] = yb
            else:
                ytm_scr[k * tb:(k + 1) * tb] += yb
        st_ref[:, d, 0, :] = h[0]
        st_ref[:, d, 1, :] = h[1]

    y_ref[...] = jnp.swapaxes(ytm_scr[...], 0, 1)


def s5_scan(proj3, bmat, cmat, lam8, h0, chunks):
    rows, seq, _ = proj3.shape
    ns = S5_COL_STATE
    has_init = h0 is not None
    in_specs = [pl.BlockSpec((SUBLANES, seq, S5_COL_CH), lambda i, c: (i, 0, c)),
                pl.BlockSpec((2, 1, S5_COL_CH, 2 * ns), lambda i, c: (0, c, 0, 0)),
                pl.BlockSpec((2, 1, 2 * ns, S5_COL_CH), lambda i, c: (0, c, 0, 0)),
                pl.BlockSpec((2, 1, SUBLANES, 2 * ns), lambda i, c: (0, c, 0, 0))]
    args = [proj3, bmat, cmat, lam8]
    if has_init:
        nb = h0.shape[0]
        in_specs.append(pl.BlockSpec((nb, 2, 2, ns), lambda i, c: (0, 0, 0, c)))
        args.append(h0)
    y, st = pl.pallas_call(
        functools.partial(_s5_scan_kernel, seq=seq, chunks=chunks, has_init=has_init),
        grid=(rows // SUBLANES, S5_N_COL),
        in_specs=in_specs,
        out_specs=[pl.BlockSpec((SUBLANES, seq, S5_COL_CH), lambda i, c: (i, 0, c)),
                   pl.BlockSpec((SUBLANES, 2, 2, ns), lambda i, c: (i, 0, 0, c))],
        out_shape=[jax.ShapeDtypeStruct((rows, seq, HALF_MIX), F32),
                   jax.ShapeDtypeStruct((rows, 2, 2, S5_GROUPS * S5_STATE), F32)],
        scratch_shapes=[pltpu.VMEM((S5_TIME_BLOCK, SUBLANES, 2 * ns), F32),
                        pltpu.VMEM((S5_TIME_BLOCK * SUBLANES, 2 * ns), BF16),
                        pltpu.VMEM((seq, SUBLANES, S5_COL_CH), F32)],
        compiler_params=_params("arbitrary", "arbitrary"),
        name="s5_scan",
    )(*args)
    return y, st


def s5_params(lam_re, lam_im, log_dt, b_re, b_im, c_re, c_im):
    lr = lam_re.astype(F32)
    li = lam_im.astype(F32)
    dt = jnp.exp(log_dt.astype(F32))[..., None]
    mag = jnp.exp(lr * dt)
    bar_re = mag * jnp.cos(li * dt)
    bar_im = mag * jnp.sin(li * dt)
    den = lr * lr + li * li
    q_re = ((bar_re - 1.0) * lr + bar_im * li) / den
    q_im = (bar_im * lr - (bar_re - 1.0) * li) / den
    br = b_re.astype(F32)
    bi = b_im.astype(F32)
    b_bar_re = q_re[..., None] * br - q_im[..., None] * bi
    b_bar_im = q_re[..., None] * bi + q_im[..., None] * br
    eye = jnp.eye(S5_COL_GROUPS, dtype=F32)

    def block_diag_b(m):
        m = m.reshape(2, S5_N_COL, S5_COL_GROUPS, S5_STATE, S5_GROUP_CH)
        bd = jnp.einsum('dngpc,gh->dngchp', m, eye)
        return bd.reshape(2, S5_N_COL, S5_COL_CH, S5_COL_STATE)

    def block_diag_c(m):
        m = m.reshape(2, S5_N_COL, S5_COL_GROUPS, S5_GROUP_CH, S5_STATE)
        bd = jnp.einsum('dngcp,gh->dngphc', m, eye)
        return bd.reshape(2, S5_N_COL, S5_COL_STATE, S5_COL_CH)

    bmat = jnp.concatenate([block_diag_b(b_bar_re), block_diag_b(b_bar_im)], axis=-1).astype(BF16)
    cmat = jnp.concatenate([block_diag_c(c_re.astype(F32)), block_diag_c(-c_im.astype(F32))],
                           axis=-2).astype(BF16)
    lam_cat = jnp.concatenate([bar_re.reshape(2, S5_N_COL, S5_COL_STATE),
                               bar_im.reshape(2, S5_N_COL, S5_COL_STATE)], axis=-1)
    lam8 = jnp.broadcast_to(lam_cat[:, :, None, :], (2, S5_N_COL, SUBLANES, 2 * S5_COL_STATE))
    return bmat, cmat, lam8


def _na_ctx_kernel(q_ref, k_ref, v_ref, o_ref, ko_ref, vo_ref):
    scale = NA_HEAD_DIM ** -0.5
    q = q_ref[0]
    k = k_ref[0]
    v = v_ref[0]
    outs = []
    for h in range(NA_HEADS):
        sl = slice(h * NA_HEAD_DIM, (h + 1) * NA_HEAD_DIM)
        kh = k[:, sl]
        vh = v[:, sl]
        ko_ref[0, 0, h] = kh
        vo_ref[0, 0, h] = vh
        s = _dot_nt(q[:, sl].astype(BF16), kh.astype(BF16)) * scale
        p = jnp.exp(s - jnp.max(s, axis=-1, keepdims=True))
        p = p / jnp.sum(p, axis=-1, keepdims=True)
        outs.append(_dot(p.astype(BF16), vh.astype(BF16)))
    o_ref[0] = jnp.concatenate(outs, axis=-1).astype(o_ref.dtype)


def na_ctx(proj3):
    b, seq, _ = proj3.shape
    cache_shape = jax.ShapeDtypeStruct((b, 1, NA_HEADS, seq, NA_HEAD_DIM), F32)
    cache_spec = pl.BlockSpec((1, 1, NA_HEADS, seq, NA_HEAD_DIM), lambda i: (i, 0, 0, 0, 0))
    return pl.pallas_call(
        _na_ctx_kernel,
        grid=(b,),
        in_specs=[pl.BlockSpec((1, seq, HALF_MIX), lambda i: (i, 0, 1)),
                  pl.BlockSpec((1, seq, HALF_MIX), lambda i: (i, 0, 2)),
                  pl.BlockSpec((1, seq, HALF_MIX), lambda i: (i, 0, 3))],
        out_specs=[pl.BlockSpec((1, seq, HALF_MIX), lambda i: (i, 0, 0)), cache_spec, cache_spec],
        out_shape=[jax.ShapeDtypeStruct((b, seq, HALF_MIX), BF16), cache_shape, cache_shape],
        compiler_params=_params("arbitrary"),
        name="na_ctx",
    )(proj3, proj3, proj3)


NA_Q_ROWS = 4
NA_KEY_ROWS = 12


def na_bias_blocks(rpb):
    qcol = np.arange(GRID_W)
    cc = np.arange(GRID_W)
    cs = np.clip(qcol - NA_WIN_C // 2, 0, GRID_W - NA_WIN_C)
    valid = (cc[None, :] >= cs[:, None]) & (cc[None, :] < cs[:, None] + NA_WIN_C)
    coff = cc[None, :] - qcol[:, None] + (NA_WIN_C - 1)
    n_col = 2 * NA_WIN_C - 1
    sel = ((coff[None] == np.arange(n_col)[:, None, None]) & valid[None]).astype(np.float32)
    sel = sel.reshape(n_col, GRID_W * GRID_W)
    mask = np.where(valid, 0.0, NEG_BIG).astype(np.float32).reshape(1, GRID_W * GRID_W)
    n_row = 2 * NA_WIN_R - 1
    t1 = jnp.dot(rpb.astype(F32).reshape(NA_HEADS * n_row, n_col), jnp.asarray(sel),
                 precision=lax.Precision.HIGHEST) + jnp.asarray(mask)
    t1 = t1.reshape(NA_HEADS, n_row, GRID_W, GRID_W)
    neg = jnp.full((NA_HEADS, GRID_W, GRID_W), NEG_BIG, F32)
    variants = ((lambda ri: 0, NA_WIN_R - 1), (lambda ri: ri, NA_WIN_R // 2 - 1),
                (lambda ri: NA_KEY_ROWS - NA_WIN_R, -1))
    out = []
    for lo_of, shift in variants:
        per_ri = []
        for ri in range(NA_Q_ROWS):
            lo = lo_of(ri)
            blocks = [t1[:, wr - ri + shift] if lo <= wr < lo + NA_WIN_R else neg for wr in range(NA_KEY_ROWS)]
            per_ri.append(jnp.concatenate(blocks, axis=-1))
        out.append(jnp.stack(per_ri, axis=1).reshape(NA_HEADS, NA_Q_ROWS * GRID_W, NA_KEY_ROWS * GRID_W))
    return jnp.stack(out, axis=0)


def _na_lat_kernel(q_ref, k_ref, v_ref, kc_ref, vc_ref, bias_ref, o_ref):
    qb = pl.program_id(1)
    rows = k_ref.shape[1] // GRID_W
    nk = NA_KEY_ROWS * GRID_W
    first_row = jnp.clip(qb * NA_Q_ROWS - NA_WIN_R // 2, 0, rows - NA_KEY_ROWS)
    start = pl.multiple_of(first_row * GRID_W, GRID_W)
    tq = q_ref.shape[1]
    lane = lax.broadcasted_iota(jnp.int32, (tq, LANES), 1)
    low = lane < NA_HEAD_DIM
    outs = []
    for pr in range(NA_HEADS // 2):
        cols = slice(pr * LANES, (pr + 1) * LANES)
        qp = q_ref[0, :, cols].astype(F32) * (NA_HEAD_DIM ** -0.5)
        kw = k_ref[0, pl.ds(start, nk), cols]
        vw = v_ref[0, pl.ds(start, nk), cols]
        kc = kc_ref[0, :, cols]
        vc = vc_ref[0, :, cols]
        o_pair = None
        for half in range(2):
            qm = jnp.where(low if half == 0 else jnp.logical_not(low), qp, 0.0).astype(BF16)
            s_loc = _dot_nt(qm, kw) + bias_ref[0, 2 * pr + half]
            s_ctx = _dot_nt(qm, kc)
            m = jnp.maximum(jnp.max(s_loc, axis=-1, keepdims=True), jnp.max(s_ctx, axis=-1, keepdims=True))
            p_loc = jnp.exp(s_loc - m)
            p_ctx = jnp.exp(s_ctx - m)
            inv = 1.0 / (jnp.sum(p_loc, axis=-1, keepdims=True) + jnp.sum(p_ctx, axis=-1, keepdims=True))
            o = (_dot(p_loc.astype(BF16), vw) + _dot(p_ctx.astype(BF16), vc)) * inv
            o_pair = o if half == 0 else jnp.where(low, o_pair, o)
        outs.append(o_pair)
    o_ref[0] = jnp.concatenate(outs, axis=-1).astype(o_ref.dtype)


def na_lat(qkv3, k_ctx, v_ctx, bias):
    b, seq, _ = qkv3.shape
    tq = NA_Q_ROWS * GRID_W
    n_q = seq // tq
    lc = k_ctx.shape[1]
    ctx_spec = pl.BlockSpec((1, lc, HALF_MIX), lambda i, r: (i, 0, 0))
    return pl.pallas_call(
        _na_lat_kernel,
        grid=(b, n_q),
        in_specs=[pl.BlockSpec((1, tq, HALF_MIX), lambda i, r: (i, r, 0)),
                  pl.BlockSpec((1, seq, HALF_MIX), lambda i, r: (i, 0, 1)),
                  pl.BlockSpec((1, seq, HALF_MIX), lambda i, r: (i, 0, 2)),
                  ctx_spec, ctx_spec,
                  pl.BlockSpec((1, NA_HEADS, tq, NA_KEY_ROWS * GRID_W),
                               lambda i, r: (jnp.where(r == 0, 0, jnp.where(r == n_q - 1, 2, 1)), 0, 0, 0))],
        out_specs=pl.BlockSpec((1, tq, HALF_MIX), lambda i, r: (i, r, 0)),
        out_shape=jax.ShapeDtypeStruct((b, seq, HALF_MIX), BF16),
        compiler_params=_params("arbitrary", "arbitrary"),
        name="na_lat",
    )(qkv3, qkv3, qkv3, k_ctx, v_ctx, bias)


def _softmax_pair_diff(s1, s2, lam):
    p1 = jnp.exp(s1 - jnp.max(s1, axis=-1, keepdims=True))
    p2 = jnp.exp(s2 - jnp.max(s2, axis=-1, keepdims=True))
    inv1 = 1.0 / jnp.sum(p1, axis=-1, keepdims=True)
    inv2 = lam / jnp.sum(p2, axis=-1, keepdims=True)
    return p1 * inv1 - p2 * inv2


def _sub_ln(o, g, lam_init):
    return o * lax.rsqrt(jnp.mean(o * o, axis=-1, keepdims=True) + EPS) * g * (1.0 - lam_init)


def _diff_ctx_kernel(lam_ref, q_ref, k_ref, v_ref, g_ref, o_ref, ko_ref, vo_ref, *, lam_init):
    scale = DIFF_D ** -0.5
    lam = lam_ref[0, 0]
    lane = lax.broadcasted_iota(jnp.int32, (q_ref.shape[1], DIFF_HEAD_DIM), 1)
    first = lane < DIFF_D
    for h in range(DIFF_HEADS):
        sl = slice(h * DIFF_HEAD_DIM, (h + 1) * DIFF_HEAD_DIM)
        qh = q_ref[0, :, sl]
        kh = k_ref[0, :, sl]
        vh = v_ref[0, :, sl]
        ko_ref[0, 0, h] = kh
        vo_ref[0, 0, h] = vh
        kb = kh.astype(BF16)
        s1 = _dot_nt(jnp.where(first, qh, 0.0).astype(BF16), kb) * scale
        s2 = _dot_nt(jnp.where(first, 0.0, qh).astype(BF16), kb) * scale
        a = _softmax_pair_diff(s1, s2, lam)
        o = _dot(a.astype(BF16), vh.astype(BF16))
        o_ref[0, :, sl] = _sub_ln(o, g_ref[...], lam_init).astype(o_ref.dtype)


def diff_ctx(proj3, lam, subln_g, lam_init):
    b, seq, _ = proj3.shape
    cache_shape = jax.ShapeDtypeStruct((b, 1, DIFF_HEADS, seq, DIFF_HEAD_DIM), F32)
    cache_spec = pl.BlockSpec((1, 1, DIFF_HEADS, seq, DIFF_HEAD_DIM), lambda i: (i, 0, 0, 0, 0))
    return pl.pallas_call(
        functools.partial(_diff_ctx_kernel, lam_init=lam_init),
        grid=(b,),
        in_specs=[pl.BlockSpec(memory_space=pltpu.SMEM),
                  pl.BlockSpec((1, seq, HALF_MIX), lambda i: (i, 0, 0)),
                  pl.BlockSpec((1, seq, HALF_MIX), lambda i: (i, 0, 1)),
                  pl.BlockSpec((1, seq, HALF_MIX), lambda i: (i, 0, 2)),
                  pl.BlockSpec((1, DIFF_HEAD_DIM), lambda i: (0, 0))],
        out_specs=[pl.BlockSpec((1, seq, HALF_MIX), lambda i: (i, 0, 0)), cache_spec, cache_spec],
        out_shape=[jax.ShapeDtypeStruct((b, seq, HALF_MIX), BF16), cache_shape, cache_shape],
        compiler_params=_params("arbitrary"),
        name="diff_ctx",
    )(lam.reshape(1, 1), proj3, proj3, proj3, subln_g.reshape(1, DIFF_HEAD_DIM))


def rope_tables(seq):
    t = np.arange(seq)
    row = (t // GRID_W).astype(np.float32)
    col = (t % GRID_W).astype(np.float32)
    n_freq = DIFF_D // 4
    inv = np.float32(ROPE_BASE) ** (-np.arange(n_freq, dtype=np.float32) / np.float32(n_freq))
    ang = np.concatenate([row[:, None] * inv, col[:, None] * inv], axis=-1)
    cos = np.repeat(np.cos(ang), 2, axis=-1)
    sin = np.repeat(np.sin(ang), 2, axis=-1)
    sign = np.where(np.arange(DIFF_D) % 2 == 0, -1.0, 1.0).astype(np.float32)
    sin = sin * sign
    return (jnp.asarray(np.tile(cos, (1, 2)).astype(np.float32)),
            jnp.asarray(np.tile(sin, (1, 2)).astype(np.float32)))


def _rope(x, cos, sin_signed):
    lane = lax.broadcasted_iota(jnp.int32, x.shape, 1)
    nxt = pltpu.roll(x, x.shape[1] - 1, 1)
    prv = pltpu.roll(x, 1, 1)
    partner = jnp.where(lane % 2 == 0, nxt, prv)
    return x * cos + partner * sin_signed


def _diff_lat_kernel(lam_ref, q_ref, k_ref, v_ref, kc_ref, vc_ref, cq_ref, sq_ref, ck_ref, sk_ref,
                     g_ref, o_ref, k_all, v_all, *, lam_init):
    seq = k_ref.shape[1]

    @pl.when(pl.program_id(2) == 0)
    def _():
        k_all[0:seq, :] = _rope(k_ref[0], ck_ref[...], sk_ref[...]).astype(BF16)
        k_all[seq:, :] = kc_ref[0, 0, 0].astype(BF16)
        v_all[0:seq, :] = v_ref[0].astype(BF16)
        v_all[seq:, :] = vc_ref[0, 0, 0].astype(BF16)

    lam = lam_ref[0, 0]
    q = _rope(q_ref[0], cq_ref[...], sq_ref[...]) * (DIFF_D ** -0.5)
    lane = lax.broadcasted_iota(jnp.int32, q.shape, 1)
    first = lane < DIFF_D
    kb = k_all[...]
    s1 = _dot_nt(jnp.where(first, q, 0.0).astype(BF16), kb)
    s2 = _dot_nt(jnp.where(first, 0.0, q).astype(BF16), kb)
    a = _softmax_pair_diff(s1, s2, lam)
    o = _dot(a.astype(BF16), v_all[...])
    o_ref[0] = _sub_ln(o, g_ref[...], lam_init).astype(o_ref.dtype)


def diff_lat(proj3, k_ctx, v_ctx, cos, sin, lam, subln_g, lam_init, tq=256):
    b, seq, _ = proj3.shape
    lc = k_ctx.shape[3]
    hd = DIFF_HEAD_DIM
    ctx_spec = pl.BlockSpec((1, 1, 1, lc, hd), lambda i, h, q: (i, 0, h, 0, 0))
    tq_spec = pl.BlockSpec((tq, hd), lambda i, h, q: (q, 0))
    full_spec = pl.BlockSpec((seq, hd), lambda i, h, q: (0, 0))
    return pl.pallas_call(
        functools.partial(_diff_lat_kernel, lam_init=lam_init),
        grid=(b, DIFF_HEADS, seq // tq),
        in_specs=[pl.BlockSpec(memory_space=pltpu.SMEM),
                  pl.BlockSpec((1, tq, hd), lambda i, h, q: (i, q, h)),
                  pl.BlockSpec((1, seq, hd), lambda i, h, q: (i, 0, DIFF_HEADS + h)),
                  pl.BlockSpec((1, seq, hd), lambda i, h, q: (i, 0, 2 * DIFF_HEADS + h)),
                  ctx_spec, ctx_spec, tq_spec, tq_spec, full_spec, full_spec,
                  pl.BlockSpec((1, hd), lambda i, h, q: (0, 0))],
        out_specs=pl.BlockSpec((1, tq, hd), lambda i, h, q: (i, q, h)),
        out_shape=jax.ShapeDtypeStruct((b, seq, HALF_MIX), BF16),
        scratch_shapes=[pltpu.VMEM((seq + lc, hd), BF16), pltpu.VMEM((seq + lc, hd), BF16)],
        compiler_params=_params("arbitrary", "arbitrary", "arbitrary"),
        name="diff_lat",
    )(lam.reshape(1, 1), proj3, proj3, proj3, k_ctx, v_ctx, cos, sin, cos, sin,
      subln_g.reshape(1, hd))


CONV_PAD = 16
CONV_SUB = 64


def _conv_kernel(a_ref, g_ref, ap_ref, gp_ref, an_ref, gn_ref, w_ref, b_ref, lg_ref, lb_ref, o_ref, xp_scr,
                 xsh_scr):
    t = pl.program_id(1)
    tt = a_ref.shape[1]
    prev = ap_ref[0] * jax.nn.sigmoid(gp_ref[0])
    nxt = an_ref[0] * jax.nn.sigmoid(gn_ref[0])
    xp_scr[0:CONV_PAD, :] = jnp.where(t > 0, prev, 0.0)
    xp_scr[CONV_PAD + tt:, :] = jnp.where(t < pl.num_programs(1) - 1, nxt, 0.0)
    xp_scr[CONV_PAD:CONV_PAD + tt, :] = a_ref[0] * jax.nn.sigmoid(g_ref[0])
    first_tap = CONV_PAD - CONV_WIDTH // 2
    n_rows = xsh_scr.shape[1]
    for b in range(SUBLANES):
        xsh_scr[b] = xp_scr[b:b + n_rows, :]
    for i in range(tt // CONV_SUB):
        s = i * CONV_SUB
        acc = jnp.zeros((CONV_SUB, HALF_MIX), F32)
        for j in range(CONV_WIDTH):
            whole, phase = divmod(first_tap + j, SUBLANES)
            lo = s + whole * SUBLANES
            acc = acc + xsh_scr[phase, lo:lo + CONV_SUB, :] * w_ref[j:j + 1, :]
        y = acc + b_ref[...]
        mu = jnp.mean(y, axis=-1, keepdims=True)
        yc = y - mu
        var = jnp.mean(yc * yc, axis=-1, keepdims=True)
        yn = yc * lax.rsqrt(var + EPS) * lg_ref[...] + lb_ref[...]
        o_ref[0, s:s + CONV_SUB, :] = _silu(yn).astype(o_ref.dtype)


def conformer_conv(proj3, w, b, ln_g, ln_b, tt):
    bsz, seq, _ = proj3.shape
    n_t = seq // tt
    hb = tt // CONV_PAD
    last = seq // CONV_PAD - 1
    vec = pl.BlockSpec((1, HALF_MIX), lambda i, t: (0, 0))

    def main(col):
        return pl.BlockSpec((1, tt, HALF_MIX), lambda i, t: (i, t, col))

    def prev(col):
        return pl.BlockSpec((1, CONV_PAD, HALF_MIX), lambda i, t: (i, jnp.maximum(t * hb - 1, 0), col))

    def nxt(col):
        return pl.BlockSpec((1, CONV_PAD, HALF_MIX), lambda i, t: (i, jnp.minimum((t + 1) * hb, last), col))

    return pl.pallas_call(
        _conv_kernel,
        grid=(bsz, n_t),
        in_specs=[main(3), main(4), prev(3), prev(4), nxt(3), nxt(4),
                  pl.BlockSpec((CONV_WIDTH, HALF_MIX), lambda i, t: (0, 0)),
                  vec, vec, vec],
        out_specs=pl.BlockSpec((1, tt, HALF_MIX), lambda i, t: (i, t, 0)),
        out_shape=jax.ShapeDtypeStruct((bsz, seq, HALF_MIX), BF16),
        scratch_shapes=[pltpu.VMEM((tt + 2 * CONV_PAD, HALF_MIX), F32),
                        pltpu.VMEM((SUBLANES, tt + 2 * CONV_PAD - SUBLANES, HALF_MIX), F32)],
        compiler_params=_params("arbitrary", "arbitrary"),
        name="conformer_conv",
    )(proj3, proj3, proj3, proj3, proj3, proj3, w, b.reshape(1, HALF_MIX), ln_g.reshape(1, HALF_MIX),
      ln_b.reshape(1, HALF_MIX))


FFN_CHUNK = 1024


def _mix_ffn_kernel(x_ref, y_ref, u_ref, d_ref, wglu_ref, bglu_ref, m2_ref, w1_ref, w2_ref, g1_ref,
                    g_ref, sh_ref, sc_ref, g2_ref, wg_ref, wu_ref, wd_ref, o_ref):
    z = jax.nn.gelu(u_ref[...] * d_ref[...] + y_ref[...])
    s5_out = z * jax.nn.sigmoid(_dot(z.astype(BF16), wglu_ref[...]) + bglu_ref[...])
    mix = _dot(s5_out.astype(BF16), w1_ref[...]) + _dot(m2_ref[...], w2_ref[...])
    x1 = x_ref[...] + g1_ref[0] * mix
    h = _norm_mod(x1, g_ref[...], sh_ref[0], sc_ref[0]).astype(BF16)
    fdim = wg_ref.shape[1]
    acc = None
    for lo in range(0, fdim, FFN_CHUNK):
        hi = min(lo + FFN_CHUNK, fdim)
        a = _dot(h, wg_ref[:, lo:hi])
        up = _dot(h, wu_ref[:, lo:hi])
        part = _dot((_silu(a) * up).astype(BF16), wd_ref[lo:hi, :])
        acc = part if acc is None else acc + part
    o_ref[...] = x1 + g2_ref[0] * acc


def mix_ffn(x, y, u, d_skip, w_glu, b_glu, m2, w_out, g, mod, tiles_per_row, w_gate, w_up, w_down, tm=512):
    t = x.shape[0]
    fdim = w_gate.shape[1]
    half = pl.BlockSpec((tm, HALF_MIX), lambda i: (i, 0))
    vec = pl.BlockSpec((1, HALF_MIX), lambda i: (0, 0))

    def resident(shape, index):
        return pl.BlockSpec(shape, index, pipeline_mode=pl.Buffered(1))

    return pl.pallas_call(
        _mix_ffn_kernel,
        grid=(t // tm,),
        in_specs=[pl.BlockSpec((tm, D_MODEL), lambda i: (i, 0)), half, half, vec,
                  resident((HALF_MIX, HALF_MIX), lambda i: (0, 0)), vec, half,
                  resident((HALF_MIX, D_MODEL), lambda i: (0, 0)),
                  resident((HALF_MIX, D_MODEL), lambda i: (1, 0)),
                  _mod_spec(2, tiles_per_row),
                  pl.BlockSpec((1, D_MODEL), lambda i: (0, 0)),
                  _mod_spec(3, tiles_per_row), _mod_spec(4, tiles_per_row), _mod_spec(5, tiles_per_row),
                  resident((D_MODEL, fdim), lambda i: (0, 0)),
                  resident((D_MODEL, fdim), lambda i: (0, 0)),
                  resident((fdim, D_MODEL), lambda i: (0, 0))],
        out_specs=pl.BlockSpec((tm, D_MODEL), lambda i: (i, 0)),
        out_shape=jax.ShapeDtypeStruct((t, D_MODEL), F32),
        compiler_params=_params("arbitrary"),
        name="mix_ffn",
    )(x, y, u, d_skip.reshape(1, HALF_MIX), w_glu, b_glu.reshape(1, HALF_MIX), m2, w_out, w_out, mod,
      g.reshape(1, D_MODEL), mod, mod, mod, w_gate, w_up, w_down)


MOE_ROW_TILE = 1024
MOE_ZERO_ROWS = 256
INFO_E0, INFO_E1, INFO_G0, INFO_G1, INFO_R0, INFO_R1 = range(6)


def _stream_mod_spec(chunk, prompt_tiles, tiles_per_req, ctx_row):
    def index(i, *_):
        return (jnp.where(i < prompt_tiles, ctx_row, (i - prompt_tiles) // tiles_per_req), 0, chunk)
    return pl.BlockSpec((1, 1, D_MODEL), index)


def _mix_out2_kernel(xp_ref, m1p_ref, m2p_ref, xs_ref, m1s_ref, m2s_ref, w1_ref, w2_ref, gate_ref, o_ref,
                     *, prompt_tiles):
    i = pl.program_id(0)
    w1 = w1_ref[...]
    w2 = w2_ref[...]

    @pl.when(i < prompt_tiles)
    def _():
        o_ref[...] = xp_ref[...] + gate_ref[0] * (_dot(m1p_ref[...], w1) + _dot(m2p_ref[...], w2))

    @pl.when(i >= prompt_tiles)
    def _():
        o_ref[...] = xs_ref[...] + gate_ref[0] * (_dot(m1s_ref[...], w1) + _dot(m2s_ref[...], w2))


def mix_out2(xp, m1p, m2p, xs, m1s, m2s, w_out, mod_all, ctx_row, req_tokens, tm=512):
    tp, ts = xp.shape[0], xs.shape[0]
    pt = tp // tm

    def p_spec(width):
        return pl.BlockSpec((tm, width), lambda i: (jnp.minimum(i, pt - 1), 0))

    def s_spec(width):
        return pl.BlockSpec((tm, width), lambda i: (jnp.maximum(i - pt, 0), 0))

    return pl.pallas_call(
        functools.partial(_mix_out2_kernel, prompt_tiles=pt),
        grid=((tp + ts) // tm,),
        in_specs=[p_spec(D_MODEL), p_spec(HALF_MIX), p_spec(HALF_MIX),
                  s_spec(D_MODEL), s_spec(HALF_MIX), s_spec(HALF_MIX),
                  pl.BlockSpec((HALF_MIX, D_MODEL), lambda i: (0, 0)),
                  pl.BlockSpec((HALF_MIX, D_MODEL), lambda i: (1, 0)),
                  _stream_mod_spec(2, pt, req_tokens // tm, ctx_row)],
        out_specs=pl.BlockSpec((tm, D_MODEL), lambda i: (i, 0)),
        out_shape=jax.ShapeDtypeStruct((tp + ts, D_MODEL), F32),
        compiler_params=_params("arbitrary"),
        name="mix_out2",
    )(xp, m1p, m2p, xs, m1s, m2s, w_out, w_out, mod_all)


def _route_kernel(x_ref, g_ref, sh_ref, sc_ref, rwh_ref, rwl_ref, info_ref, fields_ref, cnt_ref, tri_scr,
                  run_scr):
    i = pl.program_id(0)
    tm = x_ref.shape[0]

    @pl.when(i == 0)
    def _():
        r = lax.broadcasted_iota(jnp.int32, (tm, tm), 0)
        c = lax.broadcasted_iota(jnp.int32, (tm, tm), 1)
        tri_scr[...] = jnp.where(c < r, 1.0, 0.0).astype(BF16)
        run_scr[...] = jnp.zeros_like(run_scr)

    h = _norm_mod(x_ref[...], g_ref[...], sh_ref[0], sc_ref[0])
    h_hi = h.astype(BF16)
    h_lo = (h - h_hi.astype(F32)).astype(BF16)
    logits = _dot(h_hi, rwh_ref[...]) + (_dot(h_hi, rwl_ref[...]) + _dot(h_lo, rwh_ref[...]))
    lane = lax.broadcasted_iota(jnp.int32, logits.shape, 1).astype(F32)
    logits = jnp.where(lane < N_EXPERTS, logits, -jnp.inf)
    m1 = jnp.max(logits, axis=-1, keepdims=True)
    i1 = jnp.min(jnp.where(logits == m1, lane, float(LANES)), axis=-1, keepdims=True)
    rest = jnp.where(lane == i1, -jnp.inf, logits)
    m2 = jnp.max(rest, axis=-1, keepdims=True)
    i2 = jnp.min(jnp.where(rest == m2, lane, float(LANES)), axis=-1, keepdims=True)
    e2 = jnp.exp(m2 - m1)
    den = 1.0 + e2
    hit = jnp.where(lane == i1, 1.0, 0.0) + jnp.where(lane == i2, 1.0, 0.0)
    before = _dot(tri_scr[...], hit.astype(BF16)) + run_scr[0:1, :]
    r1 = jnp.sum(jnp.where(lane == i1, before, 0.0), axis=-1, keepdims=True)
    r2 = jnp.sum(jnp.where(lane == i2, before, 0.0), axis=-1, keepdims=True)
    info = jnp.zeros_like(logits)
    for slot, val in ((INFO_E0, i1), (INFO_E1, i2), (INFO_G0, 1.0 / den), (INFO_G1, e2 / den),
                      (INFO_R0, r1), (INFO_R1, r2)):
        info = jnp.where(lane == float(slot), val, info)
    info_ref[...] = info
    fields_ref[...] = jnp.transpose(info)[0:SUBLANES, :]
    run_scr[...] = run_scr[...] + jnp.sum(hit, axis=0, keepdims=True)
    cnt_ref[...] = run_scr[...]


def moe_route(x, g, mod_all, ctx_row, prompt_tokens, req_tokens, router_w, tm=1024):
    t = x.shape[0]
    rw = jnp.pad(router_w.astype(F32), ((0, 0), (0, LANES - N_EXPERTS)))
    rw_hi = rw.astype(BF16)
    rw_lo = (rw - rw_hi.astype(F32)).astype(BF16)
    pt = prompt_tokens // tm
    return pl.pallas_call(
        _route_kernel,
        grid=(t // tm,),
        in_specs=[pl.BlockSpec((tm, D_MODEL), lambda i: (i, 0)),
                  pl.BlockSpec((1, D_MODEL), lambda i: (0, 0)),
                  _stream_mod_spec(3, pt, req_tokens // tm, ctx_row),
                  _stream_mod_spec(4, pt, req_tokens // tm, ctx_row),
                  pl.BlockSpec((D_MODEL, LANES), lambda i: (0, 0)),
                  pl.BlockSpec((D_MODEL, LANES), lambda i: (0, 0))],
        out_specs=[pl.BlockSpec((tm, LANES), lambda i: (i, 0)),
                   pl.BlockSpec((SUBLANES, tm), lambda i: (0, i)),
                   pl.BlockSpec((SUBLANES, LANES), lambda i: (0, 0))],
        out_shape=[jax.ShapeDtypeStruct((t, LANES), F32), jax.ShapeDtypeStruct((SUBLANES, t), F32),
                   jax.ShapeDtypeStruct((SUBLANES, LANES), F32)],
        scratch_shapes=[pltpu.VMEM((tm, tm), BF16), pltpu.VMEM((SUBLANES, LANES), F32)],
        compiler_params=_params("arbitrary"),
        name="moe_route",
    )(x, g.reshape(1, D_MODEL), mod_all, mod_all, rw_hi, rw_lo)


def _dispatch_kernel(d0_ref, d1_ref, fill_ref, x_ref, g_ref, sh_ref, sc_ref, xs_hbm, h_scr, zero_scr, sem, zsem):
    i = pl.program_id(0)
    n = pl.num_programs(0)
    tm = x_ref.shape[0]
    slot = i % 2

    def row_copy(r, dst, s):
        return pltpu.make_async_copy(h_scr.at[s, pl.ds(r, 1), :], xs_hbm.at[pl.ds(dst, 1), :], sem.at[s])

    def wait_rows(s):
        for _ in range(2):
            pltpu.make_async_copy(h_scr.at[s], xs_hbm.at[pl.ds(0, tm), :], sem.at[s]).wait()

    @pl.when(i >= 2)
    def _():
        wait_rows(slot)

    h_scr[slot] = _norm_mod(x_ref[...], g_ref[...], sh_ref[0], sc_ref[0])
    base = i * tm

    def body(r8, c):
        rb = pl.multiple_of(r8 * SUBLANES, SUBLANES)
        for k in range(SUBLANES):
            row_copy(rb + k, d0_ref[base + rb + k], slot).start()
            row_copy(rb + k, d1_ref[base + rb + k], slot).start()
        return c

    lax.fori_loop(0, tm // SUBLANES, body, 0)

    @pl.when(i == n - 1)
    def _():
        zero_scr[...] = jnp.zeros_like(zero_scr)

        def zero_row(r):
            return pltpu.make_async_copy(zero_scr.at[pl.ds(0, 1), :], xs_hbm.at[pl.ds(r, 1), :], zsem)

        def zero_block(b):
            start = pl.multiple_of(b * MOE_ZERO_ROWS, MOE_ZERO_ROWS)
            return pltpu.make_async_copy(zero_scr, xs_hbm.at[pl.ds(start, MOE_ZERO_ROWS), :], zsem)

        def start_all(copy):
            def body(r, c):
                copy(r).start()
                return c
            return body

        def wait_all(copy):
            def body(r, c):
                copy(r).wait()
                return c
            return body

        def zero_group(b):
            start = pl.multiple_of(b * SUBLANES, SUBLANES)
            return pltpu.make_async_copy(zero_scr.at[pl.ds(0, SUBLANES), :],
                                         xs_hbm.at[pl.ds(start, SUBLANES), :], zsem)

        for e in range(N_EXPERTS):
            lo = fill_ref[e]
            hi = fill_ref[N_EXPERTS + e]
            lo_group = (lo + SUBLANES - 1) // SUBLANES
            lax.fori_loop(lo, lo_group * SUBLANES, start_all(zero_row), 0)
            lax.fori_loop(lo_group, hi // SUBLANES, start_all(zero_group), 0)
            lax.fori_loop(lo, lo_group * SUBLANES, wait_all(zero_row), 0)
            lax.fori_loop(lo_group, hi // SUBLANES, wait_all(zero_group), 0)
        blocks_per_tile = MOE_ROW_TILE // MOE_ZERO_ROWS
        first = fill_ref[2 * N_EXPERTS] * blocks_per_tile
        last = (xs_hbm.shape[0] // MOE_ROW_TILE) * blocks_per_tile
        lax.fori_loop(first, last, start_all(zero_block), 0)
        lax.fori_loop(first, last, wait_all(zero_block), 0)
        wait_rows(slot)

        @pl.when(n >= 2)
        def _():
            wait_rows(1 - slot)


def moe_dispatch(x, g, mod_all, ctx_row, prompt_tokens, req_tokens, dest0, dest1, fill, n_tiles, tm=512):
    t = x.shape[0]
    pt = prompt_tokens // tm
    grid_spec = pltpu.PrefetchScalarGridSpec(
        num_scalar_prefetch=3,
        grid=(t // tm,),
        in_specs=[pl.BlockSpec((tm, D_MODEL), lambda i, *_: (i, 0)),
                  pl.BlockSpec((1, D_MODEL), lambda i, *_: (0, 0)),
                  _stream_mod_spec(3, pt, req_tokens // tm, ctx_row),
                  _stream_mod_spec(4, pt, req_tokens // tm, ctx_row)],
        out_specs=pl.BlockSpec(memory_space=pl.ANY),
        scratch_shapes=[pltpu.VMEM((2, tm, D_MODEL), F32), pltpu.VMEM((MOE_ZERO_ROWS, D_MODEL), F32),
                        pltpu.SemaphoreType.DMA((2,)), pltpu.SemaphoreType.DMA(())],
    )
    return pl.pallas_call(
        _dispatch_kernel,
        grid_spec=grid_spec,
        out_shape=jax.ShapeDtypeStruct((n_tiles * MOE_ROW_TILE, D_MODEL), F32),
        compiler_params=pltpu.CompilerParams(dimension_semantics=("arbitrary",),
                                             vmem_limit_bytes=VMEM_LIMIT_BYTES,
                                             disable_bounds_checks=True),
        name="moe_dispatch",
    )(dest0, dest1, fill, x, g.reshape(1, D_MODEL), mod_all, mod_all)


def _experts_kernel(te_ref, tv_ref, x_ref, wg_ref, wu_ref, wd_ref, o_ref, h_scr, acc_scr):
    i = pl.program_id(0)
    f = pl.program_id(1)
    last_f = pl.num_programs(1) - 1

    @pl.when((tv_ref[i] == 0) & (f == last_f))
    def _():
        o_ref[...] = jnp.zeros_like(o_ref)

    @pl.when(tv_ref[i] > 0)
    def _():
        @pl.when(f == 0)
        def _():
            h_scr[...] = x_ref[...].astype(BF16)
            acc_scr[...] = jnp.zeros_like(acc_scr)

        h = h_scr[...]
        a = _dot(h, wg_ref[0].astype(BF16))
        u = _dot(h, wu_ref[0].astype(BF16))
        acc_scr[...] += _dot((_silu(a) * u).astype(BF16), wd_ref[0].astype(BF16))

        @pl.when(f == last_f)
        def _():
            o_ref[...] = acc_scr[...]


def moe_experts(xs_sorted, tile_expert, tile_valid, w_gate, w_up, w_down, tf=512):
    rows = xs_sorted.shape[0]
    fdim = w_gate.shape[2]
    n_f = fdim // tf
    tr = MOE_ROW_TILE

    def f_eff(i, f, tv):
        return jnp.where(tv[i] > 0, f, n_f - 1)

    grid_spec = pltpu.PrefetchScalarGridSpec(
        num_scalar_prefetch=2,
        grid=(rows // tr, n_f),
        in_specs=[pl.BlockSpec((tr, D_MODEL), lambda i, f, te, tv: (i, 0)),
                  pl.BlockSpec((1, D_MODEL, tf), lambda i, f, te, tv: (te[i], 0, f_eff(i, f, tv))),
                  pl.BlockSpec((1, D_MODEL, tf), lambda i, f, te, tv: (te[i], 0, f_eff(i, f, tv))),
                  pl.BlockSpec((1, tf, D_MODEL), lambda i, f, te, tv: (te[i], f_eff(i, f, tv), 0))],
        out_specs=pl.BlockSpec((tr, D_MODEL), lambda i, f, te, tv: (i, 0)),
        scratch_shapes=[pltpu.VMEM((tr, D_MODEL), BF16), pltpu.VMEM((tr, D_MODEL), F32)],
    )
    return pl.pallas_call(
        _experts_kernel,
        grid_spec=grid_spec,
        out_shape=jax.ShapeDtypeStruct((rows, D_MODEL), F32),
        compiler_params=_params("arbitrary", "arbitrary"),
        name="moe_experts",
    )(tile_expert, tile_valid, xs_sorted, w_gate, w_up, w_down)


def _combine_kernel(d0_ref, d1_ref, x_ref, info_ref, gate_ref, fg_ref, ys_hbm, op_ref, os_ref, rbuf, sem,
                    *, prompt_tiles):
    i = pl.program_id(0)
    n = pl.num_programs(0)
    tm = x_ref.shape[0]
    slot = i % 2

    def issue(tile, s):
        base = tile * tm

        def body(r8, c):
            rb = pl.multiple_of(r8 * SUBLANES, SUBLANES)
            for k in range(SUBLANES):
                pltpu.make_async_copy(ys_hbm.at[pl.ds(d0_ref[base + rb + k], 1), :],
                                      rbuf.at[s, 0, pl.ds(rb + k, 1), :], sem.at[s]).start()
                pltpu.make_async_copy(ys_hbm.at[pl.ds(d1_ref[base + rb + k], 1), :],
                                      rbuf.at[s, 1, pl.ds(rb + k, 1), :], sem.at[s]).start()
            return c

        lax.fori_loop(0, tm // SUBLANES, body, 0)

    @pl.when(i == 0)
    def _():
        issue(0, 0)

    @pl.when(i + 1 < n)
    def _():
        issue(i + 1, 1 - slot)

    for k in range(2):
        pltpu.make_async_copy(ys_hbm.at[pl.ds(0, tm), :], rbuf.at[slot, k], sem.at[slot]).wait()

    info = info_ref[...]
    moe = info[:, INFO_G0:INFO_G0 + 1] * rbuf[slot, 0] + info[:, INFO_G1:INFO_G1 + 1] * rbuf[slot, 1]
    y = x_ref[...] + gate_ref[0] * moe
    out = y * lax.rsqrt(jnp.mean(y * y, axis=-1, keepdims=True) + EPS) * fg_ref[...]

    @pl.when(i < prompt_tiles)
    def _():
        op_ref[...] = out

    @pl.when(i >= prompt_tiles)
    def _():
        os_ref[...] = out


def moe_combine(x, info, ys_sorted, dest0, dest1, mod_all, ctx_row, prompt_tokens, req_tokens, final_g, tm=256):
    t = x.shape[0]
    pt = prompt_tokens // tm
    grid_spec = pltpu.PrefetchScalarGridSpec(
        num_scalar_prefetch=2,
        grid=(t // tm,),
        in_specs=[pl.BlockSpec((tm, D_MODEL), lambda i, *_: (i, 0)),
                  pl.BlockSpec((tm, LANES), lambda i, *_: (i, 0)),
                  _stream_mod_spec(5, pt, req_tokens // tm, ctx_row),
                  pl.BlockSpec((1, D_MODEL), lambda i, *_: (0, 0)),
                  pl.BlockSpec(memory_space=pl.ANY)],
        out_specs=[pl.BlockSpec((tm, D_MODEL), lambda i, *_: (jnp.minimum(i, pt - 1), 0)),
                   pl.BlockSpec((tm, D_MODEL), lambda i, *_: (jnp.maximum(i - pt, 0), 0))],
        scratch_shapes=[pltpu.VMEM((2, 2, tm, D_MODEL), F32), pltpu.SemaphoreType.DMA((2,))],
    )
    return pl.pallas_call(
        functools.partial(_combine_kernel, prompt_tiles=pt),
        grid_spec=grid_spec,
        out_shape=[jax.ShapeDtypeStruct((prompt_tokens, D_MODEL), F32),
                   jax.ShapeDtypeStruct((t - prompt_tokens, D_MODEL), F32)],
        compiler_params=pltpu.CompilerParams(dimension_semantics=("arbitrary",),
                                             vmem_limit_bytes=VMEM_LIMIT_BYTES,
                                             disable_bounds_checks=True),
        name="moe_combine",
    )(dest0, dest1, x, info, mod_all, final_g.reshape(1, D_MODEL), ys_sorted)


def moe_layout(counts, fields, n_tiles):
    nt_e = (counts + MOE_ROW_TILE - 1) // MOE_ROW_TILE
    ends = jnp.cumsum(nt_e)
    total = ends[-1]
    offset = (ends - nt_e) * MOE_ROW_TILE
    experts = jnp.arange(N_EXPERTS, dtype=jnp.int32)

    def dest(e_lane, r_lane):
        e = fields[e_lane].astype(jnp.int32)
        off = jnp.sum(jnp.where(e[:, None] == experts[None, :], offset[None, :], 0), axis=1)
        return (off + fields[r_lane].astype(jnp.int32)).astype(jnp.int32)

    fill = jnp.concatenate([offset + counts, ends * MOE_ROW_TILE, total[None]]).astype(jnp.int32)
    ids = jnp.arange(n_tiles, dtype=jnp.int32)
    ids_c = jnp.minimum(ids, total - 1)
    te = jnp.sum((ids_c[:, None] >= ends[None, :]).astype(jnp.int32), axis=1)
    return (dest(INFO_E0, INFO_R0), dest(INFO_E1, INFO_R1), fill, te.astype(jnp.int32),
            (ids < total).astype(jnp.int32))


def moe_final(x, g, mod_all, ctx_row, prompt_tokens, req_tokens, router_w, final_g, w_gate, w_up, w_down):
    t = x.shape[0]
    info, fields, cnt = moe_route(x, g, mod_all, ctx_row, prompt_tokens, req_tokens, router_w)
    counts = cnt[0, :N_EXPERTS].astype(jnp.int32)
    n_tiles = (2 * t) // MOE_ROW_TILE + N_EXPERTS
    dest0, dest1, fill, te, tv = moe_layout(counts, fields, n_tiles)
    xs_sorted = moe_dispatch(x, g, mod_all, ctx_row, prompt_tokens, req_tokens, dest0, dest1, fill, n_tiles)
    ys_sorted = moe_experts(xs_sorted, te, tv, w_gate, w_up, w_down)
    return moe_combine(x, info, ys_sorted, dest0, dest1, mod_all, ctx_row, prompt_tokens, req_tokens, final_g)


def kernel(x_prompt, x_sample, state_s5, cache_na_k, cache_na_v, cache_diff_k, cache_diff_v, c, c_ctx, w_mod, b_mod, norm_mix_g, norm_ffn_g, final_norm_g, w_in_e, w_out_e, s5_lam_re, s5_lam_im, s5_log_dt, s5_b_re, s5_b_im, s5_c_re, s5_c_im, s5_d, s5_w_glu, s5_b_glu, na_rpb, ffn_w_gate, ffn_w_up, ffn_w_down, w_in_o, w_out_o, diff_lam_q1, diff_lam_k1, diff_lam_q2, diff_lam_k2, diff_subln_g, conv_w, conv_b, conv_ln_g, conv_ln_b, router_w, moe_w_gate, moe_w_up, moe_w_down):
    bp, lp, d = x_prompt.shape
    bs, ls, _ = x_sample.shape
    tm = 1024
    xp = x_prompt.reshape(bp * lp, d)
    xs = x_sample.reshape(bs * ls, d)
    rows_p = (bp * lp) // tm
    rows_s = ls // tm

    cond8 = jnp.concatenate([c, c_ctx[None, :], jnp.zeros((SUBLANES - bs - 1, d), F32)], axis=0)
    mod = adaln_all(cond8, w_mod, b_mod)
    mod_s = mod[:, 0:bs, None, :]
    mod_p = mod[:, bs:bs + 1, None, :]

    def tiles(rows, tile):
        return rows * tm // tile

    bmat, cmat, lam8 = s5_params(s5_lam_re[0], s5_lam_im[0], s5_log_dt[0], s5_b_re[0], s5_b_im[0],
                                 s5_c_re[0], s5_c_im[0])
    bias = na_bias_blocks(na_rpb[0])
    n_e = w_in_e.shape[-1]

    tiles_p = tiles(rows_p, 512)
    tiles_s = tiles(rows_s, 512)
    proj_p = in_proj(xp, norm_mix_g[0], mod_p[0], tiles_p, w_in_e[0])
    u_s, qkv_s = in_proj(xs, norm_mix_g[0], mod_s[0], tiles_s, w_in_e[0], n_f32=HALF_MIX)

    y_p, st_p = s5_scan(proj_p.reshape(bp, lp, n_e), bmat, cmat, lam8, None, 1)
    chunks = SUBLANES // bs
    h0 = state_s5[:, 0].reshape(bs, 2, 2, S5_GROUPS * S5_STATE)
    y_s, _ = s5_scan(u_s.reshape(bs * chunks, ls // chunks, HALF_MIX), bmat, cmat, lam8, h0, chunks)
    nao_p, na_k, na_v = na_ctx(proj_p.reshape(bp, lp, n_e))

    def heads_to_lanes(cache):
        return cache.transpose(0, 2, 1, 3).reshape(bs, cache.shape[2], HALF_MIX).astype(BF16)

    nao_s = na_lat(qkv_s.reshape(bs, ls, n_e - HALF_MIX), heads_to_lanes(cache_na_k[:, 0]),
                   heads_to_lanes(cache_na_v[:, 0]), bias)

    glu_w = (s5_d[0], s5_w_glu[0].astype(BF16), s5_b_glu[0])
    ffn_w = (w_out_e[0].astype(BF16), norm_ffn_g[0])
    ffn_w3 = (ffn_w_gate[0].astype(BF16), ffn_w_up[0].astype(BF16), ffn_w_down[0].astype(BF16))
    xp = mix_ffn(xp, y_p.reshape(bp * lp, HALF_MIX), proj_p, *glu_w, nao_p.reshape(bp * lp, HALF_MIX),
                 *ffn_w, mod_p[0], tiles_p, *ffn_w3)
    xs = mix_ffn(xs, y_s.reshape(bs * ls, HALF_MIX), u_s, *glu_w, nao_s.reshape(bs * ls, HALF_MIX),
                 *ffn_w, mod_s[0], tiles_s, *ffn_w3)

    lam_init = 0.8 - 0.6 * math.exp(-0.3 * 1)
    lam = (jnp.exp(jnp.sum(diff_lam_q1[0].astype(F32) * diff_lam_k1[0].astype(F32)))
           - jnp.exp(jnp.sum(diff_lam_q2[0].astype(F32) * diff_lam_k2[0].astype(F32)))
           + lam_init)
    cos, sin = rope_tables(ls)
    n_o = w_in_o.shape[-1]

    proj_p = in_proj(xp, norm_mix_g[1], mod_p[1], tiles_p, w_in_o[0])
    proj_s = in_proj(xs, norm_mix_g[1], mod_s[1], tiles_s, w_in_o[0])

    do_p, diff_k, diff_v = diff_ctx(proj_p.reshape(bp, lp, n_o), lam, diff_subln_g[0], lam_init)
    do_s = diff_lat(proj_s.reshape(bs, ls, n_o), cache_diff_k[:, 0:1], cache_diff_v[:, 0:1], cos, sin,
                    lam, diff_subln_g[0], lam_init)
    co_p = conformer_conv(proj_p.reshape(bp, lp, n_o), conv_w[0], conv_b[0], conv_ln_g[0], conv_ln_b[0], lp)
    co_s = conformer_conv(proj_s.reshape(bs, ls, n_o), conv_w[0], conv_b[0], conv_ln_g[0], conv_ln_b[0], 512)

    mod_all = mod[1][:, None, :]
    x_all = mix_out2(xp, do_p.reshape(bp * lp, HALF_MIX), co_p.reshape(bp * lp, HALF_MIX),
                     xs, do_s.reshape(bs * ls, HALF_MIX), co_s.reshape(bs * ls, HALF_MIX),
                     w_out_o[0].astype(BF16), mod_all, bs, ls)
    yp, ys = moe_final(x_all, norm_ffn_g[1], mod_all, bs, bp * lp, ls, router_w[0], final_norm_g,
                       moe_w_gate[0], moe_w_up[0], moe_w_down[0])

    new_state = st_p.reshape(bp, 1, 2, 2, S5_GROUPS, S5_STATE)
    return (yp.reshape(bp, lp, d), ys.reshape(bs, ls, d), new_state, na_k, na_v, diff_k, diff_v)
```

```python
import functools
import math

import jax
import jax.numpy as jnp
import numpy as np
from jax import lax
from jax.experimental import pallas as pl
from jax.experimental.pallas import tpu as pltpu

D_MODEL = 1024
DEPTH = 2
GRID_W = 64
HALF_MIX = 512
S5_GROUP_CH = 16
S5_GROUPS = 32
S5_STATE = 64
NA_HEAD_DIM = 64
NA_HEADS = 8
NA_WIN_R = 8
NA_WIN_C = 16
DIFF_D = 64
DIFF_HEAD_DIM = 128
DIFF_HEADS = 4
ROPE_BASE = 10000.0
CONV_WIDTH = 31
N_EXPERTS = 8
EPS = 1e-6

F32 = jnp.float32
BF16 = jnp.bfloat16
NEG_BIG = -1e30

VMEM_LIMIT_BYTES = 56 * 1024 * 1024
LANES = 128
SUBLANES = 8

S5_COL_GROUPS = 8
S5_COL_CH = S5_COL_GROUPS * S5_GROUP_CH
S5_COL_STATE = S5_COL_GROUPS * S5_STATE
S5_N_COL = S5_GROUPS // S5_COL_GROUPS
S5_TIME_BLOCK = 256


def _params(*sem):
    return pltpu.CompilerParams(dimension_semantics=sem, vmem_limit_bytes=VMEM_LIMIT_BYTES)


def _dot(a, b):
    return jnp.dot(a, b, preferred_element_type=F32)


def _dot_nt(a, b):
    return lax.dot_general(a, b, (((1,), (1,)), ((), ())), preferred_element_type=F32)


def _silu(x):
    return x * jax.nn.sigmoid(x)


def _norm_mod(x, g, shift, scale):
    y = x * lax.rsqrt(jnp.mean(x * x, axis=-1, keepdims=True) + EPS) * g
    return y * (1.0 + scale) + shift


def _mod_kernel(cond_ref, w_ref, b_ref, o_ref):
    s = _silu(cond_ref[...])
    o_ref[0] = jnp.dot(s, w_ref[0], precision=lax.Precision.HIGHEST,
                       preferred_element_type=F32) + b_ref[0]


def adaln_all(cond8, w_mod, b_mod):
    tn = 1536
    n = w_mod.shape[-1]
    return pl.pallas_call(
        _mod_kernel,
        grid=(DEPTH, n // tn),
        in_specs=[pl.BlockSpec((SUBLANES, D_MODEL), lambda l, j: (0, 0)),
                  pl.BlockSpec((1, D_MODEL, tn), lambda l, j: (l, 0, j)),
                  pl.BlockSpec((1, 1, tn), lambda l, j: (l, 0, j))],
        out_specs=pl.BlockSpec((1, SUBLANES, tn), lambda l, j: (l, 0, j)),
        out_shape=jax.ShapeDtypeStruct((DEPTH, SUBLANES, n), F32),
        compiler_params=_params("arbitrary", "arbitrary"),
        name="adaln_mod",
    )(cond8, w_mod, b_mod.reshape(DEPTH, 1, n))


def _mod_spec(chunk, tiles_per_row):
    return pl.BlockSpec((1, 1, D_MODEL), lambda i, *_: (i // tiles_per_row, 0, chunk))


def _in_proj_kernel(x_ref, g_ref, sh_ref, sc_ref, w_ref, *refs, n_f32):
    *outs, wb_scr = refs

    @pl.when(pl.program_id(0) == 0)
    def _():
        wb_scr[...] = w_ref[...].astype(BF16)

    h = _norm_mod(x_ref[...], g_ref[...], sh_ref[0], sc_ref[0]).astype(BF16)
    y = _dot(h, wb_scr[...])
    if len(outs) == 1:
        outs[0][...] = y
    else:
        outs[0][...] = y[:, :n_f32]
        outs[1][...] = y[:, n_f32:].astype(BF16)


def in_proj(x, g, mod, tiles_per_row, w, n_f32=None, tm=512):
    t = x.shape[0]
    n = w.shape[1]
    if n_f32 is None:
        out_specs = pl.BlockSpec((tm, n), lambda i: (i, 0))
        out_shape = jax.ShapeDtypeStruct((t, n), F32)
    else:
        out_specs = [pl.BlockSpec((tm, n_f32), lambda i: (i, 0)), pl.BlockSpec((tm, n - n_f32), lambda i: (i, 0))]
        out_shape = [jax.ShapeDtypeStruct((t, n_f32), F32), jax.ShapeDtypeStruct((t, n - n_f32), BF16)]
    return pl.pallas_call(
        functools.partial(_in_proj_kernel, n_f32=n_f32),
        grid=(t // tm,),
        in_specs=[pl.BlockSpec((tm, D_MODEL), lambda i: (i, 0)),
                  pl.BlockSpec((1, D_MODEL), lambda i: (0, 0)),
                  _mod_spec(0, tiles_per_row),
                  _mod_spec(1, tiles_per_row),
                  pl.BlockSpec((D_MODEL, n), lambda i: (0, 0), pipeline_mode=pl.Buffered(1))],
        out_specs=out_specs,
        out_shape=out_shape,
        scratch_shapes=[pltpu.VMEM((D_MODEL, n), BF16)],
        compiler_params=_params("arbitrary"),
        name="in_proj",
    )(x, g.reshape(1, D_MODEL), mod, mod, w)


def _s5_scan_kernel(*refs, seq, chunks, has_init):
    if has_init:
        u_ref, bm_ref, cm_ref, lam_ref, h0_ref, y_ref, st_ref, bu_scr, hb_scr, ytm_scr = refs
    else:
        u_ref, bm_ref, cm_ref, lam_ref, y_ref, st_ref, bu_scr, hb_scr, ytm_scr = refs
        h0_ref = None
    tb = S5_TIME_BLOCK
    n_tb = seq // tb
    ns = S5_COL_STATE
    row = lax.broadcasted_iota(jnp.int32, (SUBLANES, ns), 0)
    piece = row % chunks

    for d in range(2):
        lam = lam_ref[d, 0]
        lr, li = lam[:, :ns], lam[:, ns:]
        bm = bm_ref[d, 0]
        cm = cm_ref[d, 0]
        blocks = list(range(n_tb)) if d == 0 else list(range(n_tb - 1, -1, -1))

        def load_bu(k):
            ub = u_ref[:, k * tb:(k + 1) * tb, :]
            utm = jnp.swapaxes(ub, 0, 1).reshape(tb * SUBLANES, S5_COL_CH).astype(BF16)
            bu_scr[...] = _dot(utm, bm).reshape(tb, SUBLANES, 2 * ns)

        def scan_block(h, store):
            def advance(t, hr, hi):
                b = bu_scr[t]
                return lr * hr - li * hi + b[:, :ns], lr * hi + li * hr + b[:, ns:]

            def step(s, carry):
                t_a = (tb - 1 - 2 * s) if d == 1 else 2 * s
                t_b = t_a - 1 if d == 1 else t_a + 1
                ar, ai = advance(t_a, *carry)
                br, bi = advance(t_b, ar, ai)
                if store:
                    first = jnp.concatenate([br, bi] if d == 1 else [ar, ai], axis=1)
                    second = jnp.concatenate([ar, ai] if d == 1 else [br, bi], axis=1)
                    t_lo = t_b if d == 1 else t_a
                    rows = pl.ds(pl.multiple_of(t_lo * SUBLANES, 2 * SUBLANES), 2 * SUBLANES)
                    hb_scr[rows, :] = jnp.concatenate([first, second], axis=0).astype(BF16)
                return br, bi
            return lax.fori_loop(0, tb // 2, step, h)

        zero = jnp.zeros((SUBLANES, ns), F32)
        if chunks > 1:
            h = (zero, zero)
            for k in blocks:
                load_bu(k)
                h = scan_block(h, False)
            fr, fi = h
            pr, pi = lr, li
            for _ in range(int(math.log2(seq))):
                pr, pi = pr * pr - pi * pi, 2.0 * pr * pi
            edge = 0 if d == 0 else chunks - 1
            shift = 1 if d == 0 else SUBLANES - 1
            if has_init:
                h0r = h0_ref[:, d, 0, :]
                h0i = h0_ref[:, d, 1, :]
                seq_of_row = row // chunks
                er, ei = zero, zero
                for b in range(SUBLANES // chunks):
                    er = jnp.where(seq_of_row == b, h0r[b:b + 1, :], er)
                    ei = jnp.where(seq_of_row == b, h0i[b:b + 1, :], ei)
            else:
                er, ei = zero, zero
            is_edge = piece == edge
            cr = jnp.where(is_edge, er, zero)
            ci = jnp.where(is_edge, ei, zero)
            for _ in range(chunks - 1):
                tr = fr + pr * cr - pi * ci
                ti = fi + pr * ci + pi * cr
                cr = jnp.where(is_edge, er, pltpu.roll(tr, shift, 0))
                ci = jnp.where(is_edge, ei, pltpu.roll(ti, shift, 0))
            h = (cr, ci)
        else:
            if has_init:
                h = (h0_ref[:, d, 0, :], h0_ref[:, d, 1, :])
            else:
                h = (zero, zero)

        for k in blocks:
            load_bu(k)
            h = scan_block(h, True)
            yb = _dot(hb_scr[...], cm).reshape(tb, SUBLANES, S5_COL_CH)
            if d == 0:
                ytm_scr[k * tb:(k + 1) * tb] = yb
            else:
                ytm_scr[k * tb:(k + 1) * tb] += yb
        st_ref[:, d, 0, :] = h[0]
        st_ref[:, d, 1, :] = h[1]

    y_ref[...] = jnp.swapaxes(ytm_scr[...], 0, 1)


def s5_scan(proj3, bmat, cmat, lam8, h0, chunks):
    rows, seq, _ = proj3.shape
    ns = S5_COL_STATE
    has_init = h0 is not None
    in_specs = [pl.BlockSpec((SUBLANES, seq, S5_COL_CH), lambda i, c: (i, 0, c)),
                pl.BlockSpec((2, 1, S5_COL_CH, 2 * ns), lambda i, c: (0, c, 0, 0)),
                pl.BlockSpec((2, 1, 2 * ns, S5_COL_CH), lambda i, c: (0, c, 0, 0)),
                pl.BlockSpec((2, 1, SUBLANES, 2 * ns), lambda i, c: (0, c, 0, 0))]
    args = [proj3, bmat, cmat, lam8]
    if has_init:
        nb = h0.shape[0]
        in_specs.append(pl.BlockSpec((nb, 2, 2, ns), lambda i, c: (0, 0, 0, c)))
        args.append(h0)
    y, st = pl.pallas_call(
        functools.partial(_s5_scan_kernel, seq=seq, chunks=chunks, has_init=has_init),
        grid=(rows // SUBLANES, S5_N_COL),
        in_specs=in_specs,
        out_specs=[pl.BlockSpec((SUBLANES, seq, S5_COL_CH), lambda i, c: (i, 0, c)),
                   pl.BlockSpec((SUBLANES, 2, 2, ns), lambda i, c: (i, 0, 0, c))],
        out_shape=[jax.ShapeDtypeStruct((rows, seq, HALF_MIX), F32),
                   jax.ShapeDtypeStruct((rows, 2, 2, S5_GROUPS * S5_STATE), F32)],
        scratch_shapes=[pltpu.VMEM((S5_TIME_BLOCK, SUBLANES, 2 * ns), F32),
                        pltpu.VMEM((S5_TIME_BLOCK * SUBLANES, 2 * ns), BF16),
                        pltpu.VMEM((seq, SUBLANES, S5_COL_CH), F32)],
        compiler_params=_params("arbitrary", "arbitrary"),
        name="s5_scan",
    )(*args)
    return y, st


def s5_params(lam_re, lam_im, log_dt, b_re, b_im, c_re, c_im):
    lr = lam_re.astype(F32)
    li = lam_im.astype(F32)
    dt = jnp.exp(log_dt.astype(F32))[..., None]
    mag = jnp.exp(lr * dt)
    bar_re = mag * jnp.cos(li * dt)
    bar_im = mag * jnp.sin(li * dt)
    den = lr * lr + li * li
    q_re = ((bar_re - 1.0) * lr + bar_im * li) / den
    q_im = (bar_im * lr - (bar_re - 1.0) * li) / den
    br = b_re.astype(F32)
    bi = b_im.astype(F32)
    b_bar_re = q_re[..., None] * br - q_im[..., None] * bi
    b_bar_im = q_re[..., None] * bi + q_im[..., None] * br
    eye = jnp.eye(S5_COL_GROUPS, dtype=F32)

    def block_diag_b(m):
        m = m.reshape(2, S5_N_COL, S5_COL_GROUPS, S5_STATE, S5_GROUP_CH)
        bd = jnp.einsum('dngpc,gh->dngchp', m, eye)
        return bd.reshape(2, S5_N_COL, S5_COL_CH, S5_COL_STATE)

    def block_diag_c(m):
        m = m.reshape(2, S5_N_COL, S5_COL_GROUPS, S5_GROUP_CH, S5_STATE)
        bd = jnp.einsum('dngcp,gh->dngphc', m, eye)
        return bd.reshape(2, S5_N_COL, S5_COL_STATE, S5_COL_CH)

    bmat = jnp.concatenate([block_diag_b(b_bar_re), block_diag_b(b_bar_im)], axis=-1).astype(BF16)
    cmat = jnp.concatenate([block_diag_c(c_re.astype(F32)), block_diag_c(-c_im.astype(F32))],
                           axis=-2).astype(BF16)
    lam_cat = jnp.concatenate([bar_re.reshape(2, S5_N_COL, S5_COL_STATE),
                               bar_im.reshape(2, S5_N_COL, S5_COL_STATE)], axis=-1)
    lam8 = jnp.broadcast_to(lam_cat[:, :, None, :], (2, S5_N_COL, SUBLANES, 2 * S5_COL_STATE))
    return bmat, cmat, lam8


def _na_ctx_kernel(q_ref, k_ref, v_ref, o_ref, ko_ref, vo_ref):
    seq = q_ref.shape[1]
    lane = lax.broadcasted_iota(jnp.int32, (seq, LANES), 1)
    low = lane < NA_HEAD_DIM
    outs = []
    for pr in range(NA_HEADS // 2):
        cols = slice(pr * LANES, (pr + 1) * LANES)
        qp = q_ref[0, :, cols] * (NA_HEAD_DIM ** -0.5)
        kp = k_ref[0, :, cols]
        vp = v_ref[0, :, cols]
        for half in range(2):
            sl = slice(half * NA_HEAD_DIM, (half + 1) * NA_HEAD_DIM)
            ko_ref[0, 0, 2 * pr + half] = kp[:, sl]
            vo_ref[0, 0, 2 * pr + half] = vp[:, sl]
        kb = kp.astype(BF16)
        vb = vp.astype(BF16)
        o_pair = None
        for half in range(2):
            qm = jnp.where(low if half == 0 else jnp.logical_not(low), qp, 0.0).astype(BF16)
            s = _dot_nt(qm, kb)
            p = jnp.exp(s - jnp.max(s, axis=-1, keepdims=True))
            o = _dot(p.astype(BF16), vb) * (1.0 / jnp.sum(p, axis=-1, keepdims=True))
            o_pair = o if half == 0 else jnp.where(low, o_pair, o)
        outs.append(o_pair)
    o_ref[0] = jnp.concatenate(outs, axis=-1).astype(o_ref.dtype)


def na_ctx(proj3):
    b, seq, _ = proj3.shape
    cache_shape = jax.ShapeDtypeStruct((b, 1, NA_HEADS, seq, NA_HEAD_DIM), F32)
    cache_spec = pl.BlockSpec((1, 1, NA_HEADS, seq, NA_HEAD_DIM), lambda i: (i, 0, 0, 0, 0))
    return pl.pallas_call(
        _na_ctx_kernel,
        grid=(b,),
        in_specs=[pl.BlockSpec((1, seq, HALF_MIX), lambda i: (i, 0, 1)),
                  pl.BlockSpec((1, seq, HALF_MIX), lambda i: (i, 0, 2)),
                  pl.BlockSpec((1, seq, HALF_MIX), lambda i: (i, 0, 3))],
        out_specs=[pl.BlockSpec((1, seq, HALF_MIX), lambda i: (i, 0, 0)), cache_spec, cache_spec],
        out_shape=[jax.ShapeDtypeStruct((b, seq, HALF_MIX), BF16), cache_shape, cache_shape],
        compiler_params=_params("arbitrary"),
        name="na_ctx",
    )(proj3, proj3, proj3)


NA_Q_ROWS = 4
NA_KEY_ROWS = 12


def na_bias_blocks(rpb):
    qcol = np.arange(GRID_W)
    cc = np.arange(GRID_W)
    cs = np.clip(qcol - NA_WIN_C // 2, 0, GRID_W - NA_WIN_C)
    valid = (cc[None, :] >= cs[:, None]) & (cc[None, :] < cs[:, None] + NA_WIN_C)
    coff = cc[None, :] - qcol[:, None] + (NA_WIN_C - 1)
    n_col = 2 * NA_WIN_C - 1
    sel = ((coff[None] == np.arange(n_col)[:, None, None]) & valid[None]).astype(np.float32)
    sel = sel.reshape(n_col, GRID_W * GRID_W)
    mask = np.where(valid, 0.0, NEG_BIG).astype(np.float32).reshape(1, GRID_W * GRID_W)
    n_row = 2 * NA_WIN_R - 1
    t1 = jnp.dot(rpb.astype(F32).reshape(NA_HEADS * n_row, n_col), jnp.asarray(sel),
                 precision=lax.Precision.HIGHEST) + jnp.asarray(mask)
    t1 = t1.reshape(NA_HEADS, n_row, GRID_W, GRID_W)
    neg = jnp.full((NA_HEADS, GRID_W, GRID_W), NEG_BIG, F32)
    variants = ((lambda ri: 0, NA_WIN_R - 1), (lambda ri: ri, NA_WIN_R // 2 - 1),
                (lambda ri: NA_KEY_ROWS - NA_WIN_R, -1))
    out = []
    for lo_of, shift in variants:
        per_ri = []
        for ri in range(NA_Q_ROWS):
            lo = lo_of(ri)
            blocks = [t1[:, wr - ri + shift] if lo <= wr < lo + NA_WIN_R else neg for wr in range(NA_KEY_ROWS)]
            per_ri.append(jnp.concatenate(blocks, axis=-1))
        out.append(jnp.stack(per_ri, axis=1).reshape(NA_HEADS, NA_Q_ROWS * GRID_W, NA_KEY_ROWS * GRID_W))
    return jnp.stack(out, axis=0)


def _na_lat_kernel(q_ref, k_ref, v_ref, kc_ref, vc_ref, bias_ref, o_ref):
    qb = pl.program_id(1)
    rows = k_ref.shape[1] // GRID_W
    nk = NA_KEY_ROWS * GRID_W
    first_row = jnp.clip(qb * NA_Q_ROWS - NA_WIN_R // 2, 0, rows - NA_KEY_ROWS)
    start = pl.multiple_of(first_row * GRID_W, GRID_W)
    tq = q_ref.shape[1]
    lane = lax.broadcasted_iota(jnp.int32, (tq, LANES), 1)
    low = lane < NA_HEAD_DIM
    outs = []
    for pr in range(NA_HEADS // 2):
        cols = slice(pr * LANES, (pr + 1) * LANES)
        qp = q_ref[0, :, cols].astype(F32) * (NA_HEAD_DIM ** -0.5)
        kw = k_ref[0, pl.ds(start, nk), cols]
        vw = v_ref[0, pl.ds(start, nk), cols]
        kc = kc_ref[0, :, cols]
        vc = vc_ref[0, :, cols]
        o_pair = None
        for half in range(2):
            qm = jnp.where(low if half == 0 else jnp.logical_not(low), qp, 0.0).astype(BF16)
            s_loc = _dot_nt(qm, kw) + bias_ref[0, 2 * pr + half]
            s_ctx = _dot_nt(qm, kc)
            m = jnp.maximum(jnp.max(s_loc, axis=-1, keepdims=True), jnp.max(s_ctx, axis=-1, keepdims=True))
            p_loc = jnp.exp(s_loc - m)
            p_ctx = jnp.exp(s_ctx - m)
            inv = 1.0 / (jnp.sum(p_loc, axis=-1, keepdims=True) + jnp.sum(p_ctx, axis=-1, keepdims=True))
            o = (_dot(p_loc.astype(BF16), vw) + _dot(p_ctx.astype(BF16), vc)) * inv
            o_pair = o if half == 0 else jnp.where(low, o_pair, o)
        outs.append(o_pair)
    o_ref[0] = jnp.concatenate(outs, axis=-1).astype(o_ref.dtype)


def na_lat(qkv3, k_ctx, v_ctx, bias):
    b, seq, _ = qkv3.shape
    tq = NA_Q_ROWS * GRID_W
    n_q = seq // tq
    lc = k_ctx.shape[1]
    ctx_spec = pl.BlockSpec((1, lc, HALF_MIX), lambda i, r: (i, 0, 0))
    return pl.pallas_call(
        _na_lat_kernel,
        grid=(b, n_q),
        in_specs=[pl.BlockSpec((1, tq, HALF_MIX), lambda i, r: (i, r, 0)),
                  pl.BlockSpec((1, seq, HALF_MIX), lambda i, r: (i, 0, 1)),
                  pl.BlockSpec((1, seq, HALF_MIX), lambda i, r: (i, 0, 2)),
                  ctx_spec, ctx_spec,
                  pl.BlockSpec((1, NA_HEADS, tq, NA_KEY_ROWS * GRID_W),
                               lambda i, r: (jnp.where(r == 0, 0, jnp.where(r == n_q - 1, 2, 1)), 0, 0, 0))],
        out_specs=pl.BlockSpec((1, tq, HALF_MIX), lambda i, r: (i, r, 0)),
        out_shape=jax.ShapeDtypeStruct((b, seq, HALF_MIX), BF16),
        compiler_params=_params("arbitrary", "arbitrary"),
        name="na_lat",
    )(qkv3, qkv3, qkv3, k_ctx, v_ctx, bias)


def _softmax_pair_diff(s1, s2, lam):
    p1 = jnp.exp(s1 - jnp.max(s1, axis=-1, keepdims=True))
    p2 = jnp.exp(s2 - jnp.max(s2, axis=-1, keepdims=True))
    inv1 = 1.0 / jnp.sum(p1, axis=-1, keepdims=True)
    inv2 = lam / jnp.sum(p2, axis=-1, keepdims=True)
    return p1 * inv1 - p2 * inv2


def _sub_ln(o, g, lam_init):
    return o * lax.rsqrt(jnp.mean(o * o, axis=-1, keepdims=True) + EPS) * g * (1.0 - lam_init)


def _diff_ctx_kernel(lam_ref, q_ref, k_ref, v_ref, g_ref, o_ref, ko_ref, vo_ref, *, lam_init):
    scale = DIFF_D ** -0.5
    lam = lam_ref[0, 0]
    lane = lax.broadcasted_iota(jnp.int32, (q_ref.shape[1], DIFF_HEAD_DIM), 1)
    first = lane < DIFF_D
    for h in range(DIFF_HEADS):
        sl = slice(h * DIFF_HEAD_DIM, (h + 1) * DIFF_HEAD_DIM)
        qh = q_ref[0, :, sl]
        kh = k_ref[0, :, sl]
        vh = v_ref[0, :, sl]
        ko_ref[0, 0, h] = kh
        vo_ref[0, 0, h] = vh
        kb = kh.astype(BF16)
        s1 = _dot_nt(jnp.where(first, qh, 0.0).astype(BF16), kb) * scale
        s2 = _dot_nt(jnp.where(first, 0.0, qh).astype(BF16), kb) * scale
        a = _softmax_pair_diff(s1, s2, lam)
        o = _dot(a.astype(BF16), vh.astype(BF16))
        o_ref[0, :, sl] = _sub_ln(o, g_ref[...], lam_init).astype(o_ref.dtype)


def diff_ctx(proj3, lam, subln_g, lam_init):
    b, seq, _ = proj3.shape
    cache_shape = jax.ShapeDtypeStruct((b, 1, DIFF_HEADS, seq, DIFF_HEAD_DIM), F32)
    cache_spec = pl.BlockSpec((1, 1, DIFF_HEADS, seq, DIFF_HEAD_DIM), lambda i: (i, 0, 0, 0, 0))
    return pl.pallas_call(
        functools.partial(_diff_ctx_kernel, lam_init=lam_init),
        grid=(b,),
        in_specs=[pl.BlockSpec(memory_space=pltpu.SMEM),
                  pl.BlockSpec((1, seq, HALF_MIX), lambda i: (i, 0, 0)),
                  pl.BlockSpec((1, seq, HALF_MIX), lambda i: (i, 0, 1)),
                  pl.BlockSpec((1, seq, HALF_MIX), lambda i: (i, 0, 2)),
                  pl.BlockSpec((1, DIFF_HEAD_DIM), lambda i: (0, 0))],
        out_specs=[pl.BlockSpec((1, seq, HALF_MIX), lambda i: (i, 0, 0)), cache_spec, cache_spec],
        out_shape=[jax.ShapeDtypeStruct((b, seq, HALF_MIX), BF16), cache_shape, cache_shape],
        compiler_params=_params("arbitrary"),
        name="diff_ctx",
    )(lam.reshape(1, 1), proj3, proj3, proj3, subln_g.reshape(1, DIFF_HEAD_DIM))


def rope_tables(seq):
    t = np.arange(seq)
    row = (t // GRID_W).astype(np.float32)
    col = (t % GRID_W).astype(np.float32)
    n_freq = DIFF_D // 4
    inv = np.float32(ROPE_BASE) ** (-np.arange(n_freq, dtype=np.float32) / np.float32(n_freq))
    ang = np.concatenate([row[:, None] * inv, col[:, None] * inv], axis=-1)
    cos = np.repeat(np.cos(ang), 2, axis=-1)
    sin = np.repeat(np.sin(ang), 2, axis=-1)
    sign = np.where(np.arange(DIFF_D) % 2 == 0, -1.0, 1.0).astype(np.float32)
    sin = sin * sign
    return (jnp.asarray(np.tile(cos, (1, 2)).astype(np.float32)),
            jnp.asarray(np.tile(sin, (1, 2)).astype(np.float32)))


def _rope(x, cos, sin_signed):
    lane = lax.broadcasted_iota(jnp.int32, x.shape, 1)
    nxt = pltpu.roll(x, x.shape[1] - 1, 1)
    prv = pltpu.roll(x, 1, 1)
    partner = jnp.where(lane % 2 == 0, nxt, prv)
    return x * cos + partner * sin_signed


def _diff_lat_kernel(lam_ref, q_ref, k_ref, v_ref, kc_ref, vc_ref, cq_ref, sq_ref, ck_ref, sk_ref,
                     g_ref, o_ref, k_all, v_all, *, lam_init):
    seq = k_ref.shape[1]

    @pl.when(pl.program_id(2) == 0)
    def _():
        k_all[0:seq, :] = _rope(k_ref[0], ck_ref[...], sk_ref[...]).astype(BF16)
        k_all[seq:, :] = kc_ref[0, 0, 0].astype(BF16)
        v_all[0:seq, :] = v_ref[0].astype(BF16)
        v_all[seq:, :] = vc_ref[0, 0, 0].astype(BF16)

    lam = lam_ref[0, 0]
    q = _rope(q_ref[0], cq_ref[...], sq_ref[...]) * (DIFF_D ** -0.5)
    lane = lax.broadcasted_iota(jnp.int32, q.shape, 1)
    first = lane < DIFF_D
    kb = k_all[...]
    s1 = _dot_nt(jnp.where(first, q, 0.0).astype(BF16), kb)
    s2 = _dot_nt(jnp.where(first, 0.0, q).astype(BF16), kb)
    a = _softmax_pair_diff(s1, s2, lam)
    o = _dot(a.astype(BF16), v_all[...])
    o_ref[0] = _sub_ln(o, g_ref[...], lam_init).astype(o_ref.dtype)


def diff_lat(proj3, k_ctx, v_ctx, cos, sin, lam, subln_g, lam_init, tq=256):
    b, seq, _ = proj3.shape
    lc = k_ctx.shape[3]
    hd = DIFF_HEAD_DIM
    ctx_spec = pl.BlockSpec((1, 1, 1, lc, hd), lambda i, h, q: (i, 0, h, 0, 0))
    tq_spec = pl.BlockSpec((tq, hd), lambda i, h, q: (q, 0))
    full_spec = pl.BlockSpec((seq, hd), lambda i, h, q: (0, 0))
    return pl.pallas_call(
        functools.partial(_diff_lat_kernel, lam_init=lam_init),
        grid=(b, DIFF_HEADS, seq // tq),
        in_specs=[pl.BlockSpec(memory_space=pltpu.SMEM),
                  pl.BlockSpec((1, tq, hd), lambda i, h, q: (i, q, h)),
                  pl.BlockSpec((1, seq, hd), lambda i, h, q: (i, 0, DIFF_HEADS + h)),
                  pl.BlockSpec((1, seq, hd), lambda i, h, q: (i, 0, 2 * DIFF_HEADS + h)),
                  ctx_spec, ctx_spec, tq_spec, tq_spec, full_spec, full_spec,
                  pl.BlockSpec((1, hd), lambda i, h, q: (0, 0))],
        out_specs=pl.BlockSpec((1, tq, hd), lambda i, h, q: (i, q, h)),
        out_shape=jax.ShapeDtypeStruct((b, seq, HALF_MIX), BF16),
        scratch_shapes=[pltpu.VMEM((seq + lc, hd), BF16), pltpu.VMEM((seq + lc, hd), BF16)],
        compiler_params=_params("arbitrary", "arbitrary", "arbitrary"),
        name="diff_lat",
    )(lam.reshape(1, 1), proj3, proj3, proj3, k_ctx, v_ctx, cos, sin, cos, sin,
      subln_g.reshape(1, hd))


CONV_PAD = 16
CONV_SUB = 64


def _conv_kernel(a_ref, g_ref, ap_ref, gp_ref, an_ref, gn_ref, w_ref, b_ref, lg_ref, lb_ref, o_ref, xp_scr,
                 xsh_scr):
    t = pl.program_id(1)
    tt = a_ref.shape[1]
    prev = ap_ref[0] * jax.nn.sigmoid(gp_ref[0])
    nxt = an_ref[0] * jax.nn.sigmoid(gn_ref[0])
    xp_scr[0:CONV_PAD, :] = jnp.where(t > 0, prev, 0.0)
    xp_scr[CONV_PAD + tt:, :] = jnp.where(t < pl.num_programs(1) - 1, nxt, 0.0)
    xp_scr[CONV_PAD:CONV_PAD + tt, :] = a_ref[0] * jax.nn.sigmoid(g_ref[0])
    first_tap = CONV_PAD - CONV_WIDTH // 2
    n_rows = xsh_scr.shape[1]
    for b in range(SUBLANES):
        xsh_scr[b] = xp_scr[b:b + n_rows, :]
    for i in range(tt // CONV_SUB):
        s = i * CONV_SUB
        acc = jnp.zeros((CONV_SUB, HALF_MIX), F32)
        for j in range(CONV_WIDTH):
            whole, phase = divmod(first_tap + j, SUBLANES)
            lo = s + whole * SUBLANES
            acc = acc + xsh_scr[phase, lo:lo + CONV_SUB, :] * w_ref[j:j + 1, :]
        y = acc + b_ref[...]
        mu = jnp.mean(y, axis=-1, keepdims=True)
        yc = y - mu
        var = jnp.mean(yc * yc, axis=-1, keepdims=True)
        yn = yc * lax.rsqrt(var + EPS) * lg_ref[...] + lb_ref[...]
        o_ref[0, s:s + CONV_SUB, :] = _silu(yn).astype(o_ref.dtype)


def conformer_conv(proj3, w, b, ln_g, ln_b, tt):
    bsz, seq, _ = proj3.shape
    n_t = seq // tt
    hb = tt // CONV_PAD
    last = seq // CONV_PAD - 1
    vec = pl.BlockSpec((1, HALF_MIX), lambda i, t: (0, 0))

    def main(col):
        return pl.BlockSpec((1, tt, HALF_MIX), lambda i, t: (i, t, col))

    def prev(col):
        return pl.BlockSpec((1, CONV_PAD, HALF_MIX), lambda i, t: (i, jnp.maximum(t * hb - 1, 0), col))

    def nxt(col):
        return pl.BlockSpec((1, CONV_PAD, HALF_MIX), lambda i, t: (i, jnp.minimum((t + 1) * hb, last), col))

    return pl.pallas_call(
        _conv_kernel,
        grid=(bsz, n_t),
        in_specs=[main(3), main(4), prev(3), prev(4), nxt(3), nxt(4),
                  pl.BlockSpec((CONV_WIDTH, HALF_MIX), lambda i, t: (0, 0)),
                  vec, vec, vec],
        out_specs=pl.BlockSpec((1, tt, HALF_MIX), lambda i, t: (i, t, 0)),
        out_shape=jax.ShapeDtypeStruct((bsz, seq, HALF_MIX), BF16),
        scratch_shapes=[pltpu.VMEM((tt + 2 * CONV_PAD, HALF_MIX), F32),
                        pltpu.VMEM((SUBLANES, tt + 2 * CONV_PAD - SUBLANES, HALF_MIX), F32)],
        compiler_params=_params("arbitrary", "arbitrary"),
        name="conformer_conv",
    )(proj3, proj3, proj3, proj3, proj3, proj3, w, b.reshape(1, HALF_MIX), ln_g.reshape(1, HALF_MIX),
      ln_b.reshape(1, HALF_MIX))


FFN_CHUNK = 1024


def _mix_ffn_kernel(x_ref, y_ref, u_ref, d_ref, wglu_ref, bglu_ref, m2_ref, w1_ref, w2_ref, g1_ref,
                    g_ref, sh_ref, sc_ref, g2_ref, wg_ref, wu_ref, wd_ref, o_ref):
    z = jax.nn.gelu(u_ref[...] * d_ref[...] + y_ref[...])
    s5_out = z * jax.nn.sigmoid(_dot(z.astype(BF16), wglu_ref[...]) + bglu_ref[...])
    mix = _dot(s5_out.astype(BF16), w1_ref[...]) + _dot(m2_ref[...], w2_ref[...])
    x1 = x_ref[...] + g1_ref[0] * mix
    h = _norm_mod(x1, g_ref[...], sh_ref[0], sc_ref[0]).astype(BF16)
    fdim = wg_ref.shape[1]
    acc = None
    for lo in range(0, fdim, FFN_CHUNK):
        hi = min(lo + FFN_CHUNK, fdim)
        a = _dot(h, wg_ref[:, lo:hi])
        up = _dot(h, wu_ref[:, lo:hi])
        part = _dot((_silu(a) * up).astype(BF16), wd_ref[lo:hi, :])
        acc = part if acc is None else acc + part
    o_ref[...] = x1 + g2_ref[0] * acc


def mix_ffn(x, y, u, d_skip, w_glu, b_glu, m2, w_out, g, mod, tiles_per_row, w_gate, w_up, w_down, tm=512):
    t = x.shape[0]
    fdim = w_gate.shape[1]
    half = pl.BlockSpec((tm, HALF_MIX), lambda i: (i, 0))
    vec = pl.BlockSpec((1, HALF_MIX), lambda i: (0, 0))

    def resident(shape, index):
        return pl.BlockSpec(shape, index, pipeline_mode=pl.Buffered(1))

    return pl.pallas_call(
        _mix_ffn_kernel,
        grid=(t // tm,),
        in_specs=[pl.BlockSpec((tm, D_MODEL), lambda i: (i, 0)), half, half, vec,
                  resident((HALF_MIX, HALF_MIX), lambda i: (0, 0)), vec, half,
                  resident((HALF_MIX, D_MODEL), lambda i: (0, 0)),
                  resident((HALF_MIX, D_MODEL), lambda i: (1, 0)),
                  _mod_spec(2, tiles_per_row),
                  pl.BlockSpec((1, D_MODEL), lambda i: (0, 0)),
                  _mod_spec(3, tiles_per_row), _mod_spec(4, tiles_per_row), _mod_spec(5, tiles_per_row),
                  resident((D_MODEL, fdim), lambda i: (0, 0)),
                  resident((D_MODEL, fdim), lambda i: (0, 0)),
                  resident((fdim, D_MODEL), lambda i: (0, 0))],
        out_specs=pl.BlockSpec((tm, D_MODEL), lambda i: (i, 0)),
        out_shape=jax.ShapeDtypeStruct((t, D_MODEL), F32),
        compiler_params=_params("arbitrary"),
        name="mix_ffn",
    )(x, y, u, d_skip.reshape(1, HALF_MIX), w_glu, b_glu.reshape(1, HALF_MIX), m2, w_out, w_out, mod,
      g.reshape(1, D_MODEL), mod, mod, mod, w_gate, w_up, w_down)


MOE_ROW_TILE = 1024
MOE_ZERO_ROWS = 256
INFO_E0, INFO_E1, INFO_G0, INFO_G1, INFO_R0, INFO_R1 = range(6)


def _stream_mod_spec(chunk, prompt_tiles, tiles_per_req, ctx_row):
    def index(i, *_):
        return (jnp.where(i < prompt_tiles, ctx_row, (i - prompt_tiles) // tiles_per_req), 0, chunk)
    return pl.BlockSpec((1, 1, D_MODEL), index)


def _mix_route_kernel(xp_ref, m1p_ref, m2p_ref, xs_ref, m1s_ref, m2s_ref, w1_ref, w2_ref, gate_ref,
                      g_ref, sh_ref, sc_ref, rwh_ref, rwl_ref, o_ref, info_ref, fields_ref, cnt_ref,
                      tri_scr, run_scr, *, prompt_tiles):
    i = pl.program_id(0)
    tm = o_ref.shape[0]
    w1 = w1_ref[...]
    w2 = w2_ref[...]

    @pl.when(i < prompt_tiles)
    def _():
        o_ref[...] = xp_ref[...] + gate_ref[0] * (_dot(m1p_ref[...], w1) + _dot(m2p_ref[...], w2))

    @pl.when(i >= prompt_tiles)
    def _():
        o_ref[...] = xs_ref[...] + gate_ref[0] * (_dot(m1s_ref[...], w1) + _dot(m2s_ref[...], w2))

    @pl.when(i == 0)
    def _():
        r = lax.broadcasted_iota(jnp.int32, (tm, tm), 0)
        c = lax.broadcasted_iota(jnp.int32, (tm, tm), 1)
        tri_scr[...] = jnp.where(c < r, 1.0, 0.0).astype(BF16)
        run_scr[...] = jnp.zeros_like(run_scr)

    h = _norm_mod(o_ref[...], g_ref[...], sh_ref[0], sc_ref[0])
    h_hi = h.astype(BF16)
    h_lo = (h - h_hi.astype(F32)).astype(BF16)
    logits = _dot(h_hi, rwh_ref[...]) + (_dot(h_hi, rwl_ref[...]) + _dot(h_lo, rwh_ref[...]))
    lane = lax.broadcasted_iota(jnp.int32, logits.shape, 1).astype(F32)
    logits = jnp.where(lane < N_EXPERTS, logits, -jnp.inf)
    m1 = jnp.max(logits, axis=-1, keepdims=True)
    i1 = jnp.min(jnp.where(logits == m1, lane, float(LANES)), axis=-1, keepdims=True)
    rest = jnp.where(lane == i1, -jnp.inf, logits)
    m2 = jnp.max(rest, axis=-1, keepdims=True)
    i2 = jnp.min(jnp.where(rest == m2, lane, float(LANES)), axis=-1, keepdims=True)
    e2 = jnp.exp(m2 - m1)
    den = 1.0 + e2
    hit = jnp.where(lane == i1, 1.0, 0.0) + jnp.where(lane == i2, 1.0, 0.0)
    before = _dot(tri_scr[...], hit.astype(BF16)) + run_scr[0:1, :]
    r1 = jnp.sum(jnp.where(lane == i1, before, 0.0), axis=-1, keepdims=True)
    r2 = jnp.sum(jnp.where(lane == i2, before, 0.0), axis=-1, keepdims=True)
    info = jnp.zeros_like(logits)
    for slot, val in ((INFO_E0, i1), (INFO_E1, i2), (INFO_G0, 1.0 / den), (INFO_G1, e2 / den),
                      (INFO_R0, r1), (INFO_R1, r2)):
        info = jnp.where(lane == float(slot), val, info)
    info_ref[...] = info
    fields_ref[...] = jnp.transpose(info)[0:SUBLANES, :]
    run_scr[...] = run_scr[...] + jnp.sum(hit, axis=0, keepdims=True)
    cnt_ref[...] = run_scr[...]


def mix_route(xp, m1p, m2p, xs, m1s, m2s, w_out, g, mod_all, ctx_row, req_tokens, router_w, tm=512):
    tp, ts = xp.shape[0], xs.shape[0]
    t = tp + ts
    pt = tp // tm
    rw = jnp.pad(router_w.astype(F32), ((0, 0), (0, LANES - N_EXPERTS)))
    rw_hi = rw.astype(BF16)
    rw_lo = (rw - rw_hi.astype(F32)).astype(BF16)

    def p_spec(width):
        return pl.BlockSpec((tm, width), lambda i: (jnp.minimum(i, pt - 1), 0))

    def s_spec(width):
        return pl.BlockSpec((tm, width), lambda i: (jnp.maximum(i - pt, 0), 0))

    def mod_spec(chunk):
        return _stream_mod_spec(chunk, pt, req_tokens // tm, ctx_row)

    rw_spec = pl.BlockSpec((D_MODEL, LANES), lambda i: (0, 0))
    return pl.pallas_call(
        functools.partial(_mix_route_kernel, prompt_tiles=pt),
        grid=(t // tm,),
        in_specs=[p_spec(D_MODEL), p_spec(HALF_MIX), p_spec(HALF_MIX),
                  s_spec(D_MODEL), s_spec(HALF_MIX), s_spec(HALF_MIX),
                  pl.BlockSpec((HALF_MIX, D_MODEL), lambda i: (0, 0)),
                  pl.BlockSpec((HALF_MIX, D_MODEL), lambda i: (1, 0)),
                  mod_spec(2), pl.BlockSpec((1, D_MODEL), lambda i: (0, 0)), mod_spec(3), mod_spec(4),
                  rw_spec, rw_spec],
        out_specs=[pl.BlockSpec((tm, D_MODEL), lambda i: (i, 0)),
                   pl.BlockSpec((tm, LANES), lambda i: (i, 0)),
                   pl.BlockSpec((SUBLANES, tm), lambda i: (0, i)),
                   pl.BlockSpec((SUBLANES, LANES), lambda i: (0, 0))],
        out_shape=[jax.ShapeDtypeStruct((t, D_MODEL), F32), jax.ShapeDtypeStruct((t, LANES), F32),
                   jax.ShapeDtypeStruct((SUBLANES, t), F32), jax.ShapeDtypeStruct((SUBLANES, LANES), F32)],
        scratch_shapes=[pltpu.VMEM((tm, tm), BF16), pltpu.VMEM((SUBLANES, LANES), F32)],
        compiler_params=_params("arbitrary"),
        name="mix_route",
    )(xp, m1p, m2p, xs, m1s, m2s, w_out, w_out, mod_all, g.reshape(1, D_MODEL), mod_all, mod_all,
      rw_hi, rw_lo)


def _dispatch_kernel(d0_ref, d1_ref, fill_ref, x_ref, g_ref, sh_ref, sc_ref, xs_hbm, h_scr, zero_scr, sem, zsem):
    i = pl.program_id(0)
    n = pl.num_programs(0)
    tm = x_ref.shape[0]
    slot = i % 2

    def row_copy(r, dst, s):
        return pltpu.make_async_copy(h_scr.at[s, pl.ds(r, 1), :], xs_hbm.at[pl.ds(dst, 1), :], sem.at[s])

    def wait_rows(s):
        for _ in range(2):
            pltpu.make_async_copy(h_scr.at[s], xs_hbm.at[pl.ds(0, tm), :], sem.at[s]).wait()

    @pl.when(i >= 2)
    def _():
        wait_rows(slot)

    h_scr[slot] = _norm_mod(x_ref[...], g_ref[...], sh_ref[0], sc_ref[0])
    base = i * tm

    def body(r8, c):
        rb = pl.multiple_of(r8 * SUBLANES, SUBLANES)
        for k in range(SUBLANES):
            row_copy(rb + k, d0_ref[base + rb + k], slot).start()
            row_copy(rb + k, d1_ref[base + rb + k], slot).start()
        return c

    lax.fori_loop(0, tm // SUBLANES, body, 0)

    @pl.when(i == n - 1)
    def _():
        zero_scr[...] = jnp.zeros_like(zero_scr)

        def zero_row(r):
            return pltpu.make_async_copy(zero_scr.at[pl.ds(0, 1), :], xs_hbm.at[pl.ds(r, 1), :], zsem)

        def zero_block(b):
            start = pl.multiple_of(b * MOE_ZERO_ROWS, MOE_ZERO_ROWS)
            return pltpu.make_async_copy(zero_scr, xs_hbm.at[pl.ds(start, MOE_ZERO_ROWS), :], zsem)

        def start_all(copy):
            def body(r, c):
                copy(r).start()
                return c
            return body

        def wait_all(copy):
            def body(r, c):
                copy(r).wait()
                return c
            return body

        def zero_group(b):
            start = pl.multiple_of(b * SUBLANES, SUBLANES)
            return pltpu.make_async_copy(zero_scr.at[pl.ds(0, SUBLANES), :],
                                         xs_hbm.at[pl.ds(start, SUBLANES), :], zsem)

        for e in range(N_EXPERTS):
            lo = fill_ref[e]
            hi = fill_ref[N_EXPERTS + e]
            lo_group = (lo + SUBLANES - 1) // SUBLANES
            lax.fori_loop(lo, lo_group * SUBLANES, start_all(zero_row), 0)
            lax.fori_loop(lo_group, hi // SUBLANES, start_all(zero_group), 0)
            lax.fori_loop(lo, lo_group * SUBLANES, wait_all(zero_row), 0)
            lax.fori_loop(lo_group, hi // SUBLANES, wait_all(zero_group), 0)
        blocks_per_tile = MOE_ROW_TILE // MOE_ZERO_ROWS
        first = fill_ref[2 * N_EXPERTS] * blocks_per_tile
        last = (xs_hbm.shape[0] // MOE_ROW_TILE) * blocks_per_tile
        lax.fori_loop(first, last, start_all(zero_block), 0)
        lax.fori_loop(first, last, wait_all(zero_block), 0)
        wait_rows(slot)

        @pl.when(n >= 2)
        def _():
            wait_rows(1 - slot)


def moe_dispatch(x, g, mod_all, ctx_row, prompt_tokens, req_tokens, dest0, dest1, fill, n_tiles, tm=512):
    t = x.shape[0]
    pt = prompt_tokens // tm
    grid_spec = pltpu.PrefetchScalarGridSpec(
        num_scalar_prefetch=3,
        grid=(t // tm,),
        in_specs=[pl.BlockSpec((tm, D_MODEL), lambda i, *_: (i, 0)),
                  pl.BlockSpec((1, D_MODEL), lambda i, *_: (0, 0)),
                  _stream_mod_spec(3, pt, req_tokens // tm, ctx_row),
                  _stream_mod_spec(4, pt, req_tokens // tm, ctx_row)],
        out_specs=pl.BlockSpec(memory_space=pl.ANY),
        scratch_shapes=[pltpu.VMEM((2, tm, D_MODEL), F32), pltpu.VMEM((MOE_ZERO_ROWS, D_MODEL), F32),
                        pltpu.SemaphoreType.DMA((2,)), pltpu.SemaphoreType.DMA(())],
    )
    return pl.pallas_call(
        _dispatch_kernel,
        grid_spec=grid_spec,
        out_shape=jax.ShapeDtypeStruct((n_tiles * MOE_ROW_TILE, D_MODEL), F32),
        compiler_params=pltpu.CompilerParams(dimension_semantics=("arbitrary",),
                                             vmem_limit_bytes=VMEM_LIMIT_BYTES,
                                             disable_bounds_checks=True),
        name="moe_dispatch",
    )(dest0, dest1, fill, x, g.reshape(1, D_MODEL), mod_all, mod_all)


def _experts_kernel(te_ref, tv_ref, x_ref, wg_ref, wu_ref, wd_ref, o_ref, h_scr, acc_scr):
    i = pl.program_id(0)
    f = pl.program_id(1)
    last_f = pl.num_programs(1) - 1

    @pl.when((tv_ref[i] == 0) & (f == last_f))
    def _():
        o_ref[...] = jnp.zeros_like(o_ref)

    @pl.when(tv_ref[i] > 0)
    def _():
        @pl.when(f == 0)
        def _():
            h_scr[...] = x_ref[...].astype(BF16)
            acc_scr[...] = jnp.zeros_like(acc_scr)

        h = h_scr[...]
        a = _dot(h, wg_ref[0].astype(BF16))
        u = _dot(h, wu_ref[0].astype(BF16))
        acc_scr[...] += _dot((_silu(a) * u).astype(BF16), wd_ref[0].astype(BF16))

        @pl.when(f == last_f)
        def _():
            o_ref[...] = acc_scr[...]


def moe_experts(xs_sorted, tile_expert, tile_valid, w_gate, w_up, w_down, tf=512):
    rows = xs_sorted.shape[0]
    fdim = w_gate.shape[2]
    n_f = fdim // tf
    tr = MOE_ROW_TILE

    def f_eff(i, f, tv):
        return jnp.where(tv[i] > 0, f, n_f - 1)

    grid_spec = pltpu.PrefetchScalarGridSpec(
        num_scalar_prefetch=2,
        grid=(rows // tr, n_f),
        in_specs=[pl.BlockSpec((tr, D_MODEL), lambda i, f, te, tv: (i, 0)),
                  pl.BlockSpec((1, D_MODEL, tf), lambda i, f, te, tv: (te[i], 0, f_eff(i, f, tv))),
                  pl.BlockSpec((1, D_MODEL, tf), lambda i, f, te, tv: (te[i], 0, f_eff(i, f, tv))),
                  pl.BlockSpec((1, tf, D_MODEL), lambda i, f, te, tv: (te[i], f_eff(i, f, tv), 0))],
        out_specs=pl.BlockSpec((tr, D_MODEL), lambda i, f, te, tv: (i, 0)),
        scratch_shapes=[pltpu.VMEM((tr, D_MODEL), BF16), pltpu.VMEM((tr, D_MODEL), F32)],
    )
    return pl.pallas_call(
        _experts_kernel,
        grid_spec=grid_spec,
        out_shape=jax.ShapeDtypeStruct((rows, D_MODEL), F32),
        compiler_params=_params("arbitrary", "arbitrary"),
        name="moe_experts",
    )(tile_expert, tile_valid, xs_sorted, w_gate, w_up, w_down)


def _combine_kernel(d0_ref, d1_ref, x_ref, info_ref, gate_ref, fg_ref, ys_hbm, op_ref, os_ref, rbuf, sem,
                    *, prompt_tiles):
    i = pl.program_id(0)
    n = pl.num_programs(0)
    tm = x_ref.shape[0]
    slot = i % 2

    def issue(tile, s):
        base = tile * tm

        def body(r8, c):
            rb = pl.multiple_of(r8 * SUBLANES, SUBLANES)
            for k in range(SUBLANES):
                pltpu.make_async_copy(ys_hbm.at[pl.ds(d0_ref[base + rb + k], 1), :],
                                      rbuf.at[s, 0, pl.ds(rb + k, 1), :], sem.at[s]).start()
                pltpu.make_async_copy(ys_hbm.at[pl.ds(d1_ref[base + rb + k], 1), :],
                                      rbuf.at[s, 1, pl.ds(rb + k, 1), :], sem.at[s]).start()
            return c

        lax.fori_loop(0, tm // SUBLANES, body, 0)

    @pl.when(i == 0)
    def _():
        issue(0, 0)

    @pl.when(i + 1 < n)
    def _():
        issue(i + 1, 1 - slot)

    for k in range(2):
        pltpu.make_async_copy(ys_hbm.at[pl.ds(0, tm), :], rbuf.at[slot, k], sem.at[slot]).wait()

    info = info_ref[...]
    moe = info[:, INFO_G0:INFO_G0 + 1] * rbuf[slot, 0] + info[:, INFO_G1:INFO_G1 + 1] * rbuf[slot, 1]
    y = x_ref[...] + gate_ref[0] * moe
    out = y * lax.rsqrt(jnp.mean(y * y, axis=-1, keepdims=True) + EPS) * fg_ref[...]

    @pl.when(i < prompt_tiles)
    def _():
        op_ref[...] = out

    @pl.when(i >= prompt_tiles)
    def _():
        os_ref[...] = out


def moe_combine(x, info, ys_sorted, dest0, dest1, mod_all, ctx_row, prompt_tokens, req_tokens, final_g, tm=512):
    t = x.shape[0]
    pt = prompt_tokens // tm
    grid_spec = pltpu.PrefetchScalarGridSpec(
        num_scalar_prefetch=2,
        grid=(t // tm,),
        in_specs=[pl.BlockSpec((tm, D_MODEL), lambda i, *_: (i, 0)),
                  pl.BlockSpec((tm, LANES), lambda i, *_: (i, 0)),
                  _stream_mod_spec(5, pt, req_tokens // tm, ctx_row),
                  pl.BlockSpec((1, D_MODEL), lambda i, *_: (0, 0)),
                  pl.BlockSpec(memory_space=pl.ANY)],
        out_specs=[pl.BlockSpec((tm, D_MODEL), lambda i, *_: (jnp.minimum(i, pt - 1), 0)),
                   pl.BlockSpec((tm, D_MODEL), lambda i, *_: (jnp.maximum(i - pt, 0), 0))],
        scratch_shapes=[pltpu.VMEM((2, 2, tm, D_MODEL), F32), pltpu.SemaphoreType.DMA((2,))],
    )
    return pl.pallas_call(
        functools.partial(_combine_kernel, prompt_tiles=pt),
        grid_spec=grid_spec,
        out_shape=[jax.ShapeDtypeStruct((prompt_tokens, D_MODEL), F32),
                   jax.ShapeDtypeStruct((t - prompt_tokens, D_MODEL), F32)],
        compiler_params=pltpu.CompilerParams(dimension_semantics=("arbitrary",),
                                             vmem_limit_bytes=VMEM_LIMIT_BYTES,
                                             disable_bounds_checks=True),
        name="moe_combine",
    )(dest0, dest1, x, info, mod_all, final_g.reshape(1, D_MODEL), ys_sorted)


def moe_layout(counts, fields, n_tiles):
    nt_e = (counts + MOE_ROW_TILE - 1) // MOE_ROW_TILE
    ends = jnp.cumsum(nt_e)
    total = ends[-1]
    offset = (ends - nt_e) * MOE_ROW_TILE
    experts = jnp.arange(N_EXPERTS, dtype=jnp.int32)

    def dest(e_lane, r_lane):
        e = fields[e_lane].astype(jnp.int32)
        off = jnp.sum(jnp.where(e[:, None] == experts[None, :], offset[None, :], 0), axis=1)
        return (off + fields[r_lane].astype(jnp.int32)).astype(jnp.int32)

    fill = jnp.concatenate([offset + counts, ends * MOE_ROW_TILE, total[None]]).astype(jnp.int32)
    ids = jnp.arange(n_tiles, dtype=jnp.int32)
    ids_c = jnp.minimum(ids, total - 1)
    te = jnp.sum((ids_c[:, None] >= ends[None, :]).astype(jnp.int32), axis=1)
    return (dest(INFO_E0, INFO_R0), dest(INFO_E1, INFO_R1), fill, te.astype(jnp.int32),
            (ids < total).astype(jnp.int32))


def moe_final(x, info, fields, cnt, g, mod_all, ctx_row, prompt_tokens, req_tokens, final_g, w_gate, w_up,
              w_down):
    t = x.shape[0]
    counts = cnt[0, :N_EXPERTS].astype(jnp.int32)
    n_tiles = (2 * t) // MOE_ROW_TILE + N_EXPERTS
    dest0, dest1, fill, te, tv = moe_layout(counts, fields, n_tiles)
    xs_sorted = moe_dispatch(x, g, mod_all, ctx_row, prompt_tokens, req_tokens, dest0, dest1, fill, n_tiles)
    ys_sorted = moe_experts(xs_sorted, te, tv, w_gate, w_up, w_down)
    return moe_combine(x, info, ys_sorted, dest0, dest1, mod_all, ctx_row, prompt_tokens, req_tokens, final_g)


def kernel(x_prompt, x_sample, state_s5, cache_na_k, cache_na_v, cache_diff_k, cache_diff_v, c, c_ctx, w_mod, b_mod, norm_mix_g, norm_ffn_g, final_norm_g, w_in_e, w_out_e, s5_lam_re, s5_lam_im, s5_log_dt, s5_b_re, s5_b_im, s5_c_re, s5_c_im, s5_d, s5_w_glu, s5_b_glu, na_rpb, ffn_w_gate, ffn_w_up, ffn_w_down, w_in_o, w_out_o, diff_lam_q1, diff_lam_k1, diff_lam_q2, diff_lam_k2, diff_subln_g, conv_w, conv_b, conv_ln_g, conv_ln_b, router_w, moe_w_gate, moe_w_up, moe_w_down):
    bp, lp, d = x_prompt.shape
    bs, ls, _ = x_sample.shape
    tm = 1024
    xp = x_prompt.reshape(bp * lp, d)
    xs = x_sample.reshape(bs * ls, d)
    rows_p = (bp * lp) // tm
    rows_s = ls // tm

    cond8 = jnp.concatenate([c, c_ctx[None, :], jnp.zeros((SUBLANES - bs - 1, d), F32)], axis=0)
    mod = adaln_all(cond8, w_mod, b_mod)
    mod_s = mod[:, 0:bs, None, :]
    mod_p = mod[:, bs:bs + 1, None, :]

    def tiles(rows, tile):
        return rows * tm // tile

    bmat, cmat, lam8 = s5_params(s5_lam_re[0], s5_lam_im[0], s5_log_dt[0], s5_b_re[0], s5_b_im[0],
                                 s5_c_re[0], s5_c_im[0])
    bias = na_bias_blocks(na_rpb[0])
    n_e = w_in_e.shape[-1]

    tiles_p = tiles(rows_p, 512)
    tiles_s = tiles(rows_s, 512)
    proj_p = in_proj(xp, norm_mix_g[0], mod_p[0], tiles_p, w_in_e[0])
    u_s, qkv_s = in_proj(xs, norm_mix_g[0], mod_s[0], tiles_s, w_in_e[0], n_f32=HALF_MIX)

    y_p, st_p = s5_scan(proj_p.reshape(bp, lp, n_e), bmat, cmat, lam8, None, 1)
    chunks = SUBLANES // bs
    h0 = state_s5[:, 0].reshape(bs, 2, 2, S5_GROUPS * S5_STATE)
    y_s, _ = s5_scan(u_s.reshape(bs * chunks, ls // chunks, HALF_MIX), bmat, cmat, lam8, h0, chunks)
    nao_p, na_k, na_v = na_ctx(proj_p.reshape(bp, lp, n_e))

    def heads_to_lanes(cache):
        return cache.transpose(0, 2, 1, 3).reshape(bs, cache.shape[2], HALF_MIX).astype(BF16)

    nao_s = na_lat(qkv_s.reshape(bs, ls, n_e - HALF_MIX), heads_to_lanes(cache_na_k[:, 0]),
                   heads_to_lanes(cache_na_v[:, 0]), bias)

    glu_w = (s5_d[0], s5_w_glu[0].astype(BF16), s5_b_glu[0])
    ffn_w = (w_out_e[0].astype(BF16), norm_ffn_g[0])
    ffn_w3 = (ffn_w_gate[0].astype(BF16), ffn_w_up[0].astype(BF16), ffn_w_down[0].astype(BF16))
    xp = mix_ffn(xp, y_p.reshape(bp * lp, HALF_MIX), proj_p, *glu_w, nao_p.reshape(bp * lp, HALF_MIX),
                 *ffn_w, mod_p[0], tiles_p, *ffn_w3)
    xs = mix_ffn(xs, y_s.reshape(bs * ls, HALF_MIX), u_s, *glu_w, nao_s.reshape(bs * ls, HALF_MIX),
                 *ffn_w, mod_s[0], tiles_s, *ffn_w3)

    lam_init = 0.8 - 0.6 * math.exp(-0.3 * 1)
    lam = (jnp.exp(jnp.sum(diff_lam_q1[0].astype(F32) * diff_lam_k1[0].astype(F32)))
           - jnp.exp(jnp.sum(diff_lam_q2[0].astype(F32) * diff_lam_k2[0].astype(F32)))
           + lam_init)
    cos, sin = rope_tables(ls)
    n_o = w_in_o.shape[-1]

    proj_p = in_proj(xp, norm_mix_g[1], mod_p[1], tiles_p, w_in_o[0])
    proj_s = in_proj(xs, norm_mix_g[1], mod_s[1], tiles_s, w_in_o[0])

    do_p, diff_k, diff_v = diff_ctx(proj_p.reshape(bp, lp, n_o), lam, diff_subln_g[0], lam_init)
    do_s = diff_lat(proj_s.reshape(bs, ls, n_o), cache_diff_k[:, 0:1], cache_diff_v[:, 0:1], cos, sin,
                    lam, diff_subln_g[0], lam_init)
    co_p = conformer_conv(proj_p.reshape(bp, lp, n_o), conv_w[0], conv_b[0], conv_ln_g[0], conv_ln_b[0], lp)
    co_s = conformer_conv(proj_s.reshape(bs, ls, n_o), conv_w[0], conv_b[0], conv_ln_g[0], conv_ln_b[0], 512)

    mod_all = mod[1][:, None, :]
    x_all, info, fields, cnt = mix_route(
        xp, do_p.reshape(bp * lp, HALF_MIX), co_p.reshape(bp * lp, HALF_MIX),
        xs, do_s.reshape(bs * ls, HALF_MIX), co_s.reshape(bs * ls, HALF_MIX),
        w_out_o[0].astype(BF16), norm_ffn_g[1], mod_all, bs, ls, router_w[0])
    yp, ys = moe_final(x_all, info, fields, cnt, norm_ffn_g[1], mod_all, bs, bp * lp, ls, final_norm_g,
                       moe_w_gate[0], moe_w_up[0], moe_w_down[0])

    new_state = st_p.reshape(bp, 1, 2, 2, S5_GROUPS, S5_STATE)
    return (yp.reshape(bp, lp, d), ys.reshape(bs, ls, d), new_state, na_k, na_v, diff_k, diff_v)
```

```python
import functools
import math

import jax
import jax.numpy as jnp
import numpy as np
from jax import lax
from jax.experimental import pallas as pl
from jax.experimental.pallas import tpu as pltpu

D_MODEL = 1024
DEPTH = 2
GRID_W = 64
HALF_MIX = 512
S5_GROUP_CH = 16
S5_GROUPS = 32
S5_STATE = 64
NA_HEAD_DIM = 64
NA_HEADS = 8
NA_WIN_R = 8
NA_WIN_C = 16
DIFF_D = 64
DIFF_HEAD_DIM = 128
DIFF_HEADS = 4
ROPE_BASE = 10000.0
CONV_WIDTH = 31
N_EXPERTS = 8
EPS = 1e-6

F32 = jnp.float32
BF16 = jnp.bfloat16
NEG_BIG = -1e30

VMEM_LIMIT_BYTES = 56 * 1024 * 1024
LANES = 128
SUBLANES = 8

S5_COL_GROUPS = 8
S5_COL_CH = S5_COL_GROUPS * S5_GROUP_CH
S5_COL_STATE = S5_COL_GROUPS * S5_STATE
S5_N_COL = S5_GROUPS // S5_COL_GROUPS
S5_TIME_BLOCK = 256


def _params(*sem):
    return pltpu.CompilerParams(dimension_semantics=sem, vmem_limit_bytes=VMEM_LIMIT_BYTES)


def _dot(a, b):
    return jnp.dot(a, b, preferred_element_type=F32)


def _dot_nt(a, b):
    return lax.dot_general(a, b, (((1,), (1,)), ((), ())), preferred_element_type=F32)


def _silu(x):
    return x * jax.nn.sigmoid(x)


def _norm_mod(x, g, shift, scale):
    y = x * lax.rsqrt(jnp.mean(x * x, axis=-1, keepdims=True) + EPS) * g
    return y * (1.0 + scale) + shift


def _mod_kernel(cond_ref, w_ref, b_ref, o_ref):
    s = _silu(cond_ref[...])
    o_ref[0, :, 0, :] = jnp.dot(s, w_ref[0], precision=lax.Precision.HIGHEST,
                                preferred_element_type=F32) + b_ref[0]


def adaln_all(cond8, w_mod, b_mod):
    tn = 1536
    n = w_mod.shape[-1]
    return pl.pallas_call(
        _mod_kernel,
        grid=(DEPTH, n // tn),
        in_specs=[pl.BlockSpec((SUBLANES, D_MODEL), lambda l, j: (0, 0)),
                  pl.BlockSpec((1, D_MODEL, tn), lambda l, j: (l, 0, j)),
                  pl.BlockSpec((1, 1, tn), lambda l, j: (l, 0, j))],
        out_specs=pl.BlockSpec((1, SUBLANES, 1, tn), lambda l, j: (l, 0, 0, j)),
        out_shape=jax.ShapeDtypeStruct((DEPTH, SUBLANES, 1, n), F32),
        compiler_params=_params("arbitrary", "arbitrary"),
        name="adaln_mod",
    )(cond8, w_mod, b_mod.reshape(DEPTH, 1, n))


def _mod_spec(chunk, tiles_per_row):
    return pl.BlockSpec((1, 1, D_MODEL), lambda i, *_: (i // tiles_per_row, 0, chunk))


def _in_proj_kernel(x_ref, g_ref, sh_ref, sc_ref, w_ref, *refs, n_f32):
    *outs, wb_scr = refs

    @pl.when(pl.program_id(0) == 0)
    def _():
        wb_scr[...] = w_ref[...].astype(BF16)

    h = _norm_mod(x_ref[...], g_ref[...], sh_ref[0], sc_ref[0]).astype(BF16)
    y = _dot(h, wb_scr[...])
    if len(outs) == 1:
        outs[0][...] = y
    else:
        outs[0][...] = y[:, :n_f32]
        outs[1][...] = y[:, n_f32:].astype(BF16)


def in_proj(x, g, mod, tiles_per_row, w, n_f32=None, tm=512):
    t = x.shape[0]
    n = w.shape[1]
    if n_f32 is None:
        out_specs = pl.BlockSpec((tm, n), lambda i: (i, 0))
        out_shape = jax.ShapeDtypeStruct((t, n), F32)
    else:
        out_specs = [pl.BlockSpec((tm, n_f32), lambda i: (i, 0)), pl.BlockSpec((tm, n - n_f32), lambda i: (i, 0))]
        out_shape = [jax.ShapeDtypeStruct((t, n_f32), F32), jax.ShapeDtypeStruct((t, n - n_f32), BF16)]
    return pl.pallas_call(
        functools.partial(_in_proj_kernel, n_f32=n_f32),
        grid=(t // tm,),
        in_specs=[pl.BlockSpec((tm, D_MODEL), lambda i: (i, 0)),
                  pl.BlockSpec((1, D_MODEL), lambda i: (0, 0)),
                  _mod_spec(0, tiles_per_row),
                  _mod_spec(1, tiles_per_row),
                  pl.BlockSpec((D_MODEL, n), lambda i: (0, 0), pipeline_mode=pl.Buffered(1))],
        out_specs=out_specs,
        out_shape=out_shape,
        scratch_shapes=[pltpu.VMEM((D_MODEL, n), BF16)],
        compiler_params=_params("arbitrary"),
        name="in_proj",
    )(x, g.reshape(1, D_MODEL), mod, mod, w)


def _s5_scan_kernel(*refs, seq, chunks, has_init):
    if has_init:
        u_ref, bm_ref, cm_ref, lam_ref, h0_ref, y_ref, st_ref, bu_scr, hb_scr, ytm_scr = refs
    else:
        u_ref, bm_ref, cm_ref, lam_ref, y_ref, st_ref, bu_scr, hb_scr, ytm_scr = refs
        h0_ref = None
    tb = S5_TIME_BLOCK
    n_tb = seq // tb
    ns = S5_COL_STATE
    row = lax.broadcasted_iota(jnp.int32, (SUBLANES, ns), 0)
    piece = row % chunks

    for d in range(2):
        lam = lam_ref[d, 0]
        lr, li = lam[:, :ns], lam[:, ns:]
        bm = bm_ref[d, 0]
        cm = cm_ref[d, 0]
        blocks = list(range(n_tb)) if d == 0 else list(range(n_tb - 1, -1, -1))

        def load_bu(k):
            ub = u_ref[:, k * tb:(k + 1) * tb, :]
            utm = jnp.swapaxes(ub, 0, 1).reshape(tb * SUBLANES, S5_COL_CH).astype(BF16)
            bu_scr[...] = _dot(utm, bm).reshape(tb, SUBLANES, 2 * ns)

        def scan_block(h, store):
            def advance(t, hr, hi):
                b = bu_scr[t]
                return lr * hr - li * hi + b[:, :ns], lr * hi + li * hr + b[:, ns:]

            def step(s, carry):
                t_a = (tb - 1 - 2 * s) if d == 1 else 2 * s
                t_b = t_a - 1 if d == 1 else t_a + 1
                ar, ai = advance(t_a, *carry)
                br, bi = advance(t_b, ar, ai)
                if store:
                    first = jnp.concatenate([br, bi] if d == 1 else [ar, ai], axis=1)
                    second = jnp.concatenate([ar, ai] if d == 1 else [br, bi], axis=1)
                    t_lo = t_b if d == 1 else t_a
                    rows = pl.ds(pl.multiple_of(t_lo * SUBLANES, 2 * SUBLANES), 2 * SUBLANES)
                    hb_scr[rows, :] = jnp.concatenate([first, second], axis=0).astype(BF16)
                return br, bi
            return lax.fori_loop(0, tb // 2, step, h)

        zero = jnp.zeros((SUBLANES, ns), F32)
        if chunks > 1:
            h = (zero, zero)
            for k in blocks:
                load_bu(k)
                h = scan_block(h, False)
            fr, fi = h
            pr, pi = lr, li
            for _ in range(int(math.log2(seq))):
                pr, pi = pr * pr - pi * pi, 2.0 * pr * pi
            edge = 0 if d == 0 else chunks - 1
            shift = 1 if d == 0 else SUBLANES - 1
            if has_init:
                h0r = h0_ref[:, d, 0, :]
                h0i = h0_ref[:, d, 1, :]
                seq_of_row = row // chunks
                er, ei = zero, zero
                for b in range(SUBLANES // chunks):
                    er = jnp.where(seq_of_row == b, h0r[b:b + 1, :], er)
                    ei = jnp.where(seq_of_row == b, h0i[b:b + 1, :], ei)
            else:
                er, ei = zero, zero
            is_edge = piece == edge
            cr = jnp.where(is_edge, er, zero)
            ci = jnp.where(is_edge, ei, zero)
            for _ in range(chunks - 1):
                tr = fr + pr * cr - pi * ci
                ti = fi + pr * ci + pi * cr
                cr = jnp.where(is_edge, er, pltpu.roll(tr, shift, 0))
                ci = jnp.where(is_edge, ei, pltpu.roll(ti, shift, 0))
            h = (cr, ci)
        else:
            if has_init:
                h = (h0_ref[:, d, 0, :], h0_ref[:, d, 1, :])
            else:
                h = (zero, zero)

        for k in blocks:
            load_bu(k)
            h = scan_block(h, True)
            yb = _dot(hb_scr[...], cm).reshape(tb, SUBLANES, S5_COL_CH)
            if d == 0:
                ytm_scr[k * tb:(k + 1) * tb] = yb
            else:
                ytm_scr[k * tb:(k + 1) * tb] += yb
        st_ref[:, d, 0, :] = h[0]
        st_ref[:, d, 1, :] = h[1]

    y_ref[...] = jnp.swapaxes(ytm_scr[...], 0, 1)


def s5_scan(proj3, bmat, cmat, lam8, h0, chunks):
    rows, seq, _ = proj3.shape
    ns = S5_COL_STATE
    has_init = h0 is not None
    in_specs = [pl.BlockSpec((SUBLANES, seq, S5_COL_CH), lambda i, c: (i, 0, c)),
                pl.BlockSpec((2, 1, S5_COL_CH, 2 * ns), lambda i, c: (0, c, 0, 0)),
                pl.BlockSpec((2, 1, 2 * ns, S5_COL_CH), lambda i, c: (0, c, 0, 0)),
                pl.BlockSpec((2, 1, SUBLANES, 2 * ns), lambda i, c: (0, c, 0, 0))]
    args = [proj3, bmat, cmat, lam8]
    if has_init:
        nb = h0.shape[0]
        in_specs.append(pl.BlockSpec((nb, 2, 2, ns), lambda i, c: (0, 0, 0, c)))
        args.append(h0)
    y, st = pl.pallas_call(
        functools.partial(_s5_scan_kernel, seq=seq, chunks=chunks, has_init=has_init),
        grid=(rows // SUBLANES, S5_N_COL),
        in_specs=in_specs,
        out_specs=[pl.BlockSpec((SUBLANES, seq, S5_COL_CH), lambda i, c: (i, 0, c)),
                   pl.BlockSpec((SUBLANES, 2, 2, ns), lambda i, c: (i, 0, 0, c))],
        out_shape=[jax.ShapeDtypeStruct((rows, seq, HALF_MIX), F32),
                   jax.ShapeDtypeStruct((rows, 2, 2, S5_GROUPS * S5_STATE), F32)],
        scratch_shapes=[pltpu.VMEM((S5_TIME_BLOCK, SUBLANES, 2 * ns), F32),
                        pltpu.VMEM((S5_TIME_BLOCK * SUBLANES, 2 * ns), BF16),
                        pltpu.VMEM((seq, SUBLANES, S5_COL_CH), F32)],
        compiler_params=_params("arbitrary", "arbitrary"),
        name="s5_scan",
    )(*args)
    return y, st


def s5_params(lam_re, lam_im, log_dt, b_re, b_im, c_re, c_im):
    lr = lam_re.astype(F32)
    li = lam_im.astype(F32)
    dt = jnp.exp(log_dt.astype(F32))[..., None]
    mag = jnp.exp(lr * dt)
    bar_re = mag * jnp.cos(li * dt)
    bar_im = mag * jnp.sin(li * dt)
    den = lr * lr + li * li
    q_re = ((bar_re - 1.0) * lr + bar_im * li) / den
    q_im = (bar_im * lr - (bar_re - 1.0) * li) / den
    br = b_re.astype(F32)
    bi = b_im.astype(F32)
    b_bar_re = q_re[..., None] * br - q_im[..., None] * bi
    b_bar_im = q_re[..., None] * bi + q_im[..., None] * br
    eye = jnp.eye(S5_COL_GROUPS, dtype=F32)

    def block_diag_b(m):
        m = m.reshape(2, S5_N_COL, S5_COL_GROUPS, S5_STATE, S5_GROUP_CH)
        bd = jnp.einsum('dngpc,gh->dngchp', m, eye)
        return bd.reshape(2, S5_N_COL, S5_COL_CH, S5_COL_STATE)

    def block_diag_c(m):
        m = m.reshape(2, S5_N_COL, S5_COL_GROUPS, S5_GROUP_CH, S5_STATE)
        bd = jnp.einsum('dngcp,gh->dngphc', m, eye)
        return bd.reshape(2, S5_N_COL, S5_COL_STATE, S5_COL_CH)

    bmat = jnp.concatenate([block_diag_b(b_bar_re), block_diag_b(b_bar_im)], axis=-1).astype(BF16)
    cmat = jnp.concatenate([block_diag_c(c_re.astype(F32)), block_diag_c(-c_im.astype(F32))],
                           axis=-2).astype(BF16)
    lam_cat = jnp.concatenate([bar_re.reshape(2, S5_N_COL, S5_COL_STATE),
                               bar_im.reshape(2, S5_N_COL, S5_COL_STATE)], axis=-1)
    lam8 = jnp.broadcast_to(lam_cat[:, :, None, :], (2, S5_N_COL, SUBLANES, 2 * S5_COL_STATE))
    return bmat, cmat, lam8


def _na_ctx_kernel(q_ref, k_ref, v_ref, o_ref, ko_ref, vo_ref):
    seq = q_ref.shape[1]
    lane = lax.broadcasted_iota(jnp.int32, (seq, LANES), 1)
    low = lane < NA_HEAD_DIM
    outs = []
    for pr in range(NA_HEADS // 2):
        cols = slice(pr * LANES, (pr + 1) * LANES)
        qp = q_ref[0, :, cols] * (NA_HEAD_DIM ** -0.5)
        kp = k_ref[0, :, cols]
        vp = v_ref[0, :, cols]
        for half in range(2):
            sl = slice(half * NA_HEAD_DIM, (half + 1) * NA_HEAD_DIM)
            ko_ref[0, 0, 2 * pr + half] = kp[:, sl]
            vo_ref[0, 0, 2 * pr + half] = vp[:, sl]
        kb = kp.astype(BF16)
        vb = vp.astype(BF16)
        o_pair = None
        for half in range(2):
            qm = jnp.where(low if half == 0 else jnp.logical_not(low), qp, 0.0).astype(BF16)
            s = _dot_nt(qm, kb)
            p = jnp.exp(s - jnp.max(s, axis=-1, keepdims=True))
            o = _dot(p.astype(BF16), vb) * (1.0 / jnp.sum(p, axis=-1, keepdims=True))
            o_pair = o if half == 0 else jnp.where(low, o_pair, o)
        outs.append(o_pair)
    o_ref[0] = jnp.concatenate(outs, axis=-1).astype(o_ref.dtype)


def na_ctx(proj3):
    b, seq, _ = proj3.shape
    cache_shape = jax.ShapeDtypeStruct((b, 1, NA_HEADS, seq, NA_HEAD_DIM), F32)
    cache_spec = pl.BlockSpec((1, 1, NA_HEADS, seq, NA_HEAD_DIM), lambda i: (i, 0, 0, 0, 0))
    return pl.pallas_call(
        _na_ctx_kernel,
        grid=(b,),
        in_specs=[pl.BlockSpec((1, seq, HALF_MIX), lambda i: (i, 0, 1)),
                  pl.BlockSpec((1, seq, HALF_MIX), lambda i: (i, 0, 2)),
                  pl.BlockSpec((1, seq, HALF_MIX), lambda i: (i, 0, 3))],
        out_specs=[pl.BlockSpec((1, seq, HALF_MIX), lambda i: (i, 0, 0)), cache_spec, cache_spec],
        out_shape=[jax.ShapeDtypeStruct((b, seq, HALF_MIX), BF16), cache_shape, cache_shape],
        compiler_params=_params("arbitrary"),
        name="na_ctx",
    )(proj3, proj3, proj3)


NA_Q_ROWS = 4
NA_KEY_ROWS = 12


def na_bias_blocks(rpb):
    qcol = np.arange(GRID_W)
    cc = np.arange(GRID_W)
    cs = np.clip(qcol - NA_WIN_C // 2, 0, GRID_W - NA_WIN_C)
    valid = (cc[None, :] >= cs[:, None]) & (cc[None, :] < cs[:, None] + NA_WIN_C)
    coff = cc[None, :] - qcol[:, None] + (NA_WIN_C - 1)
    n_col = 2 * NA_WIN_C - 1
    sel = ((coff[None] == np.arange(n_col)[:, None, None]) & valid[None]).astype(np.float32)
    sel = sel.reshape(n_col, GRID_W * GRID_W)
    mask = np.where(valid, 0.0, NEG_BIG).astype(np.float32).reshape(1, GRID_W * GRID_W)
    n_row = 2 * NA_WIN_R - 1
    t1 = jnp.dot(rpb.astype(F32).reshape(NA_HEADS * n_row, n_col), jnp.asarray(sel),
                 precision=lax.Precision.HIGHEST) + jnp.asarray(mask)
    t1 = t1.reshape(NA_HEADS, n_row, GRID_W, GRID_W)
    neg = jnp.full((NA_HEADS, GRID_W, GRID_W), NEG_BIG, F32)
    variants = ((lambda ri: 0, NA_WIN_R - 1), (lambda ri: ri, NA_WIN_R // 2 - 1),
                (lambda ri: NA_KEY_ROWS - NA_WIN_R, -1))
    out = []
    for lo_of, shift in variants:
        per_ri = []
        for ri in range(NA_Q_ROWS):
            lo = lo_of(ri)
            blocks = [t1[:, wr - ri + shift] if lo <= wr < lo + NA_WIN_R else neg for wr in range(NA_KEY_ROWS)]
            per_ri.append(jnp.concatenate(blocks, axis=-1))
        out.append(jnp.stack(per_ri, axis=1).reshape(NA_HEADS, NA_Q_ROWS * GRID_W, NA_KEY_ROWS * GRID_W))
    return jnp.stack(out, axis=0)


def _na_lat_kernel(q_ref, k_ref, v_ref, kc_ref, vc_ref, bias_ref, o_ref):
    qb = pl.program_id(1)
    rows = k_ref.shape[1] // GRID_W
    nk = NA_KEY_ROWS * GRID_W
    first_row = jnp.clip(qb * NA_Q_ROWS - NA_WIN_R // 2, 0, rows - NA_KEY_ROWS)
    start = pl.multiple_of(first_row * GRID_W, GRID_W)
    tq = q_ref.shape[1]
    lane = lax.broadcasted_iota(jnp.int32, (tq, LANES), 1)
    low = lane < NA_HEAD_DIM
    outs = []
    for pr in range(NA_HEADS // 2):
        cols = slice(pr * LANES, (pr + 1) * LANES)
        qp = q_ref[0, :, cols].astype(F32) * (NA_HEAD_DIM ** -0.5)
        kw = k_ref[0, pl.ds(start, nk), cols]
        vw = v_ref[0, pl.ds(start, nk), cols]
        kc = kc_ref[0, :, cols]
        vc = vc_ref[0, :, cols]
        o_pair = None
        for half in range(2):
            qm = jnp.where(low if half == 0 else jnp.logical_not(low), qp, 0.0).astype(BF16)
            s_loc = _dot_nt(qm, kw) + bias_ref[0, 2 * pr + half]
            s_ctx = _dot_nt(qm, kc)
            m = jnp.maximum(jnp.max(s_loc, axis=-1, keepdims=True), jnp.max(s_ctx, axis=-1, keepdims=True))
            p_loc = jnp.exp(s_loc - m)
            p_ctx = jnp.exp(s_ctx - m)
            inv = 1.0 / (jnp.sum(p_loc, axis=-1, keepdims=True) + jnp.sum(p_ctx, axis=-1, keepdims=True))
            o = (_dot(p_loc.astype(BF16), vw) + _dot(p_ctx.astype(BF16), vc)) * inv
            o_pair = o if half == 0 else jnp.where(low, o_pair, o)
        outs.append(o_pair)
    o_ref[0] = jnp.concatenate(outs, axis=-1).astype(o_ref.dtype)


def na_lat(qkv3, k_ctx, v_ctx, bias):
    b, seq, _ = qkv3.shape
    tq = NA_Q_ROWS * GRID_W
    n_q = seq // tq
    lc = k_ctx.shape[1]
    ctx_spec = pl.BlockSpec((1, lc, HALF_MIX), lambda i, r: (i, 0, 0))
    return pl.pallas_call(
        _na_lat_kernel,
        grid=(b, n_q),
        in_specs=[pl.BlockSpec((1, tq, HALF_MIX), lambda i, r: (i, r, 0)),
                  pl.BlockSpec((1, seq, HALF_MIX), lambda i, r: (i, 0, 1)),
                  pl.BlockSpec((1, seq, HALF_MIX), lambda i, r: (i, 0, 2)),
                  ctx_spec, ctx_spec,
                  pl.BlockSpec((1, NA_HEADS, tq, NA_KEY_ROWS * GRID_W),
                               lambda i, r: (jnp.where(r == 0, 0, jnp.where(r == n_q - 1, 2, 1)), 0, 0, 0))],
        out_specs=pl.BlockSpec((1, tq, HALF_MIX), lambda i, r: (i, r, 0)),
        out_shape=jax.ShapeDtypeStruct((b, seq, HALF_MIX), BF16),
        compiler_params=_params("arbitrary", "arbitrary"),
        name="na_lat",
    )(qkv3, qkv3, qkv3, k_ctx, v_ctx, bias)


def _softmax_pair_diff(s1, s2, lam):
    p1 = jnp.exp(s1 - jnp.max(s1, axis=-1, keepdims=True))
    p2 = jnp.exp(s2 - jnp.max(s2, axis=-1, keepdims=True))
    inv1 = 1.0 / jnp.sum(p1, axis=-1, keepdims=True)
    inv2 = lam / jnp.sum(p2, axis=-1, keepdims=True)
    return p1 * inv1 - p2 * inv2


def _sub_ln(o, g, lam_init):
    return o * lax.rsqrt(jnp.mean(o * o, axis=-1, keepdims=True) + EPS) * g * (1.0 - lam_init)


def _diff_ctx_kernel(lam_ref, q_ref, k_ref, v_ref, g_ref, o_ref, ko_ref, vo_ref, *, lam_init):
    scale = DIFF_D ** -0.5
    lam = lam_ref[0, 0]
    lane = lax.broadcasted_iota(jnp.int32, (q_ref.shape[1], DIFF_HEAD_DIM), 1)
    first = lane < DIFF_D
    for h in range(DIFF_HEADS):
        sl = slice(h * DIFF_HEAD_DIM, (h + 1) * DIFF_HEAD_DIM)
        qh = q_ref[0, :, sl]
        kh = k_ref[0, :, sl]
        vh = v_ref[0, :, sl]
        ko_ref[0, 0, h] = kh
        vo_ref[0, 0, h] = vh
        kb = kh.astype(BF16)
        s1 = _dot_nt(jnp.where(first, qh, 0.0).astype(BF16), kb) * scale
        s2 = _dot_nt(jnp.where(first, 0.0, qh).astype(BF16), kb) * scale
        a = _softmax_pair_diff(s1, s2, lam)
        o = _dot(a.astype(BF16), vh.astype(BF16))
        o_ref[0, :, sl] = _sub_ln(o, g_ref[...], lam_init).astype(o_ref.dtype)


def diff_ctx(proj3, lam, subln_g, lam_init):
    b, seq, _ = proj3.shape
    cache_shape = jax.ShapeDtypeStruct((b, 1, DIFF_HEADS, seq, DIFF_HEAD_DIM), F32)
    cache_spec = pl.BlockSpec((1, 1, DIFF_HEADS, seq, DIFF_HEAD_DIM), lambda i: (i, 0, 0, 0, 0))
    return pl.pallas_call(
        functools.partial(_diff_ctx_kernel, lam_init=lam_init),
        grid=(b,),
        in_specs=[pl.BlockSpec(memory_space=pltpu.SMEM),
                  pl.BlockSpec((1, seq, HALF_MIX), lambda i: (i, 0, 0)),
                  pl.BlockSpec((1, seq, HALF_MIX), lambda i: (i, 0, 1)),
                  pl.BlockSpec((1, seq, HALF_MIX), lambda i: (i, 0, 2)),
                  pl.BlockSpec((1, DIFF_HEAD_DIM), lambda i: (0, 0))],
        out_specs=[pl.BlockSpec((1, seq, HALF_MIX), lambda i: (i, 0, 0)), cache_spec, cache_spec],
        out_shape=[jax.ShapeDtypeStruct((b, seq, HALF_MIX), BF16), cache_shape, cache_shape],
        compiler_params=_params("arbitrary"),
        name="diff_ctx",
    )(lam.reshape(1, 1), proj3, proj3, proj3, subln_g.reshape(1, DIFF_HEAD_DIM))


def rope_tables(seq):
    t = np.arange(seq)
    row = (t // GRID_W).astype(np.float32)
    col = (t % GRID_W).astype(np.float32)
    n_freq = DIFF_D // 4
    inv = np.float32(ROPE_BASE) ** (-np.arange(n_freq, dtype=np.float32) / np.float32(n_freq))
    ang = np.concatenate([row[:, None] * inv, col[:, None] * inv], axis=-1)
    cos = np.repeat(np.cos(ang), 2, axis=-1)
    sin = np.repeat(np.sin(ang), 2, axis=-1)
    sign = np.where(np.arange(DIFF_D) % 2 == 0, -1.0, 1.0).astype(np.float32)
    sin = sin * sign
    return (jnp.asarray(np.tile(cos, (1, 2)).astype(np.float32)),
            jnp.asarray(np.tile(sin, (1, 2)).astype(np.float32)))


def _rope(x, cos, sin_signed):
    lane = lax.broadcasted_iota(jnp.int32, x.shape, 1)
    nxt = pltpu.roll(x, x.shape[1] - 1, 1)
    prv = pltpu.roll(x, 1, 1)
    partner = jnp.where(lane % 2 == 0, nxt, prv)
    return x * cos + partner * sin_signed


def _diff_lat_kernel(lam_ref, q_ref, k_ref, v_ref, kc_ref, vc_ref, cq_ref, sq_ref, ck_ref, sk_ref,
                     g_ref, o_ref, k_all, v_all, *, lam_init):
    seq = k_ref.shape[1]

    @pl.when(pl.program_id(2) == 0)
    def _():
        k_all[0:seq, :] = _rope(k_ref[0], ck_ref[...], sk_ref[...]).astype(BF16)
        k_all[seq:, :] = kc_ref[0, 0, 0].astype(BF16)
        v_all[0:seq, :] = v_ref[0].astype(BF16)
        v_all[seq:, :] = vc_ref[0, 0, 0].astype(BF16)

    lam = lam_ref[0, 0]
    q = _rope(q_ref[0], cq_ref[...], sq_ref[...]) * (DIFF_D ** -0.5)
    lane = lax.broadcasted_iota(jnp.int32, q.shape, 1)
    first = lane < DIFF_D
    kb = k_all[...]
    s1 = _dot_nt(jnp.where(first, q, 0.0).astype(BF16), kb)
    s2 = _dot_nt(jnp.where(first, 0.0, q).astype(BF16), kb)
    a = _softmax_pair_diff(s1, s2, lam)
    o = _dot(a.astype(BF16), v_all[...])
    o_ref[0] = _sub_ln(o, g_ref[...], lam_init).astype(o_ref.dtype)


def diff_lat(proj3, k_ctx, v_ctx, cos, sin, lam, subln_g, lam_init, tq=256):
    b, seq, _ = proj3.shape
    lc = k_ctx.shape[3]
    hd = DIFF_HEAD_DIM
    ctx_spec = pl.BlockSpec((1, 1, 1, lc, hd), lambda i, h, q: (i, 0, h, 0, 0))
    tq_spec = pl.BlockSpec((tq, hd), lambda i, h, q: (q, 0))
    full_spec = pl.BlockSpec((seq, hd), lambda i, h, q: (0, 0))
    return pl.pallas_call(
        functools.partial(_diff_lat_kernel, lam_init=lam_init),
        grid=(b, DIFF_HEADS, seq // tq),
        in_specs=[pl.BlockSpec(memory_space=pltpu.SMEM),
                  pl.BlockSpec((1, tq, hd), lambda i, h, q: (i, q, h)),
                  pl.BlockSpec((1, seq, hd), lambda i, h, q: (i, 0, DIFF_HEADS + h)),
                  pl.BlockSpec((1, seq, hd), lambda i, h, q: (i, 0, 2 * DIFF_HEADS + h)),
                  ctx_spec, ctx_spec, tq_spec, tq_spec, full_spec, full_spec,
                  pl.BlockSpec((1, hd), lambda i, h, q: (0, 0))],
        out_specs=pl.BlockSpec((1, tq, hd), lambda i, h, q: (i, q, h)),
        out_shape=jax.ShapeDtypeStruct((b, seq, HALF_MIX), BF16),
        scratch_shapes=[pltpu.VMEM((seq + lc, hd), BF16), pltpu.VMEM((seq + lc, hd), BF16)],
        compiler_params=_params("arbitrary", "arbitrary", "arbitrary"),
        name="diff_lat",
    )(lam.reshape(1, 1), proj3, proj3, proj3, k_ctx, v_ctx, cos, sin, cos, sin,
      subln_g.reshape(1, hd))


CONV_PAD = 16
CONV_SUB = 64


def _conv_kernel(a_ref, g_ref, ap_ref, gp_ref, an_ref, gn_ref, w_ref, b_ref, lg_ref, lb_ref, o_ref, xp_scr,
                 xsh_scr):
    t = pl.program_id(1)
    tt = a_ref.shape[1]
    prev = ap_ref[0] * jax.nn.sigmoid(gp_ref[0])
    nxt = an_ref[0] * jax.nn.sigmoid(gn_ref[0])
    xp_scr[0:CONV_PAD, :] = jnp.where(t > 0, prev, 0.0)
    xp_scr[CONV_PAD + tt:, :] = jnp.where(t < pl.num_programs(1) - 1, nxt, 0.0)
    xp_scr[CONV_PAD:CONV_PAD + tt, :] = a_ref[0] * jax.nn.sigmoid(g_ref[0])
    first_tap = CONV_PAD - CONV_WIDTH // 2
    n_rows = xsh_scr.shape[1]
    for b in range(SUBLANES):
        xsh_scr[b] = xp_scr[b:b + n_rows, :]
    for i in range(tt // CONV_SUB):
        s = i * CONV_SUB
        acc = jnp.zeros((CONV_SUB, HALF_MIX), F32)
        for j in range(CONV_WIDTH):
            whole, phase = divmod(first_tap + j, SUBLANES)
            lo = s + whole * SUBLANES
            acc = acc + xsh_scr[phase, lo:lo + CONV_SUB, :] * w_ref[j:j + 1, :]
        y = acc + b_ref[...]
        mu = jnp.mean(y, axis=-1, keepdims=True)
        yc = y - mu
        var = jnp.mean(yc * yc, axis=-1, keepdims=True)
        yn = yc * lax.rsqrt(var + EPS) * lg_ref[...] + lb_ref[...]
        o_ref[0, s:s + CONV_SUB, :] = _silu(yn).astype(o_ref.dtype)


def conformer_conv(proj3, w, b, ln_g, ln_b, tt):
    bsz, seq, _ = proj3.shape
    n_t = seq // tt
    hb = tt // CONV_PAD
    last = seq // CONV_PAD - 1
    vec = pl.BlockSpec((1, HALF_MIX), lambda i, t: (0, 0))

    def main(col):
        return pl.BlockSpec((1, tt, HALF_MIX), lambda i, t: (i, t, col))

    def prev(col):
        return pl.BlockSpec((1, CONV_PAD, HALF_MIX), lambda i, t: (i, jnp.maximum(t * hb - 1, 0), col))

    def nxt(col):
        return pl.BlockSpec((1, CONV_PAD, HALF_MIX), lambda i, t: (i, jnp.minimum((t + 1) * hb, last), col))

    return pl.pallas_call(
        _conv_kernel,
        grid=(bsz, n_t),
        in_specs=[main(3), main(4), prev(3), prev(4), nxt(3), nxt(4),
                  pl.BlockSpec((CONV_WIDTH, HALF_MIX), lambda i, t: (0, 0)),
                  vec, vec, vec],
        out_specs=pl.BlockSpec((1, tt, HALF_MIX), lambda i, t: (i, t, 0)),
        out_shape=jax.ShapeDtypeStruct((bsz, seq, HALF_MIX), BF16),
        scratch_shapes=[pltpu.VMEM((tt + 2 * CONV_PAD, HALF_MIX), F32),
                        pltpu.VMEM((SUBLANES, tt + 2 * CONV_PAD - SUBLANES, HALF_MIX), F32)],
        compiler_params=_params("arbitrary", "arbitrary"),
        name="conformer_conv",
    )(proj3, proj3, proj3, proj3, proj3, proj3, w, b.reshape(1, HALF_MIX), ln_g.reshape(1, HALF_MIX),
      ln_b.reshape(1, HALF_MIX))


FFN_CHUNK = 1024


def _mix_ffn_kernel(x_ref, y_ref, u_ref, d_ref, wglu_ref, bglu_ref, m2_ref, w1_ref, w2_ref, g1_ref,
                    g_ref, sh_ref, sc_ref, g2_ref, wg_ref, wu_ref, wd_ref, o_ref):
    z = jax.nn.gelu(u_ref[...] * d_ref[...] + y_ref[...])
    s5_out = z * jax.nn.sigmoid(_dot(z.astype(BF16), wglu_ref[...]) + bglu_ref[...])
    mix = _dot(s5_out.astype(BF16), w1_ref[...]) + _dot(m2_ref[...], w2_ref[...])
    x1 = x_ref[...] + g1_ref[0] * mix
    h = _norm_mod(x1, g_ref[...], sh_ref[0], sc_ref[0]).astype(BF16)
    fdim = wg_ref.shape[1]
    acc = None
    for lo in range(0, fdim, FFN_CHUNK):
        hi = min(lo + FFN_CHUNK, fdim)
        a = _dot(h, wg_ref[:, lo:hi])
        up = _dot(h, wu_ref[:, lo:hi])
        part = _dot((_silu(a) * up).astype(BF16), wd_ref[lo:hi, :])
        acc = part if acc is None else acc + part
    o_ref[...] = x1 + g2_ref[0] * acc


def mix_ffn(x, y, u, d_skip, w_glu, b_glu, m2, w_out, g, mod, tiles_per_row, w_gate, w_up, w_down, tm=512):
    t = x.shape[0]
    fdim = w_gate.shape[1]
    half = pl.BlockSpec((tm, HALF_MIX), lambda i: (i, 0))
    vec = pl.BlockSpec((1, HALF_MIX), lambda i: (0, 0))

    def resident(shape, index):
        return pl.BlockSpec(shape, index, pipeline_mode=pl.Buffered(1))

    return pl.pallas_call(
        _mix_ffn_kernel,
        grid=(t // tm,),
        in_specs=[pl.BlockSpec((tm, D_MODEL), lambda i: (i, 0)), half, half, vec,
                  resident((HALF_MIX, HALF_MIX), lambda i: (0, 0)), vec, half,
                  resident((HALF_MIX, D_MODEL), lambda i: (0, 0)),
                  resident((HALF_MIX, D_MODEL), lambda i: (1, 0)),
                  _mod_spec(2, tiles_per_row),
                  pl.BlockSpec((1, D_MODEL), lambda i: (0, 0)),
                  _mod_spec(3, tiles_per_row), _mod_spec(4, tiles_per_row), _mod_spec(5, tiles_per_row),
                  resident((D_MODEL, fdim), lambda i: (0, 0)),
                  resident((D_MODEL, fdim), lambda i: (0, 0)),
                  resident((fdim, D_MODEL), lambda i: (0, 0))],
        out_specs=pl.BlockSpec((tm, D_MODEL), lambda i: (i, 0)),
        out_shape=jax.ShapeDtypeStruct((t, D_MODEL), F32),
        compiler_params=_params("arbitrary"),
        name="mix_ffn",
    )(x, y, u, d_skip.reshape(1, HALF_MIX), w_glu, b_glu.reshape(1, HALF_MIX), m2, w_out, w_out, mod,
      g.reshape(1, D_MODEL), mod, mod, mod, w_gate, w_up, w_down)


MOE_ROW_TILE = 1024
MOE_ZERO_ROWS = 256
INFO_E0, INFO_E1, INFO_G0, INFO_G1, INFO_R0, INFO_R1 = range(6)


def _stream_mod_spec(chunk, prompt_tiles, tiles_per_req, ctx_row):
    def index(i, *_):
        return (jnp.where(i < prompt_tiles, ctx_row, (i - prompt_tiles) // tiles_per_req), 0, chunk)
    return pl.BlockSpec((1, 1, D_MODEL), index)


def _mix_route_kernel(xp_ref, m1p_ref, m2p_ref, xs_ref, m1s_ref, m2s_ref, w1_ref, w2_ref, gate_ref,
                      g_ref, sh_ref, sc_ref, rwh_ref, rwl_ref, o_ref, info_ref, fields_ref, cnt_ref,
                      tri_scr, run_scr, *, prompt_tiles):
    i = pl.program_id(0)
    tm = o_ref.shape[0]
    w1 = w1_ref[...]
    w2 = w2_ref[...]

    @pl.when(i < prompt_tiles)
    def _():
        o_ref[...] = xp_ref[...] + gate_ref[0] * (_dot(m1p_ref[...], w1) + _dot(m2p_ref[...], w2))

    @pl.when(i >= prompt_tiles)
    def _():
        o_ref[...] = xs_ref[...] + gate_ref[0] * (_dot(m1s_ref[...], w1) + _dot(m2s_ref[...], w2))

    @pl.when(i == 0)
    def _():
        r = lax.broadcasted_iota(jnp.int32, (tm, tm), 0)
        c = lax.broadcasted_iota(jnp.int32, (tm, tm), 1)
        tri_scr[...] = jnp.where(c < r, 1.0, 0.0).astype(BF16)
        run_scr[...] = jnp.zeros_like(run_scr)

    h = _norm_mod(o_ref[...], g_ref[...], sh_ref[0], sc_ref[0])
    h_hi = h.astype(BF16)
    h_lo = (h - h_hi.astype(F32)).astype(BF16)
    logits = _dot(h_hi, rwh_ref[...]) + (_dot(h_hi, rwl_ref[...]) + _dot(h_lo, rwh_ref[...]))
    lane = lax.broadcasted_iota(jnp.int32, logits.shape, 1).astype(F32)
    logits = jnp.where(lane < N_EXPERTS, logits, -jnp.inf)
    m1 = jnp.max(logits, axis=-1, keepdims=True)
    i1 = jnp.min(jnp.where(logits == m1, lane, float(LANES)), axis=-1, keepdims=True)
    rest = jnp.where(lane == i1, -jnp.inf, logits)
    m2 = jnp.max(rest, axis=-1, keepdims=True)
    i2 = jnp.min(jnp.where(rest == m2, lane, float(LANES)), axis=-1, keepdims=True)
    e2 = jnp.exp(m2 - m1)
    den = 1.0 + e2
    hit = jnp.where(lane == i1, 1.0, 0.0) + jnp.where(lane == i2, 1.0, 0.0)
    before = _dot(tri_scr[...], hit.astype(BF16)) + run_scr[0:1, :]
    r1 = jnp.sum(jnp.where(lane == i1, before, 0.0), axis=-1, keepdims=True)
    r2 = jnp.sum(jnp.where(lane == i2, before, 0.0), axis=-1, keepdims=True)
    info = jnp.zeros_like(logits)
    for slot, val in ((INFO_E0, i1), (INFO_E1, i2), (INFO_G0, 1.0 / den), (INFO_G1, e2 / den),
                      (INFO_R0, r1), (INFO_R1, r2)):
        info = jnp.where(lane == float(slot), val, info)
    info_ref[...] = info
    fields_ref[...] = jnp.transpose(info)[0:SUBLANES, :]
    run_scr[...] = run_scr[...] + jnp.sum(hit, axis=0, keepdims=True)
    cnt_ref[...] = run_scr[...]


def mix_route(xp, m1p, m2p, xs, m1s, m2s, w_out, g, mod_all, ctx_row, req_tokens, router_w, tm=512):
    tp, ts = xp.shape[0], xs.shape[0]
    t = tp + ts
    pt = tp // tm
    rw = jnp.pad(router_w.astype(F32), ((0, 0), (0, LANES - N_EXPERTS)))
    rw_hi = rw.astype(BF16)
    rw_lo = (rw - rw_hi.astype(F32)).astype(BF16)

    def p_spec(width):
        return pl.BlockSpec((tm, width), lambda i: (jnp.minimum(i, pt - 1), 0))

    def s_spec(width):
        return pl.BlockSpec((tm, width), lambda i: (jnp.maximum(i - pt, 0), 0))

    def mod_spec(chunk):
        return _stream_mod_spec(chunk, pt, req_tokens // tm, ctx_row)

    rw_spec = pl.BlockSpec((D_MODEL, LANES), lambda i: (0, 0))
    return pl.pallas_call(
        functools.partial(_mix_route_kernel, prompt_tiles=pt),
        grid=(t // tm,),
        in_specs=[p_spec(D_MODEL), p_spec(HALF_MIX), p_spec(HALF_MIX),
                  s_spec(D_MODEL), s_spec(HALF_MIX), s_spec(HALF_MIX),
                  pl.BlockSpec((HALF_MIX, D_MODEL), lambda i: (0, 0)),
                  pl.BlockSpec((HALF_MIX, D_MODEL), lambda i: (1, 0)),
                  mod_spec(2), pl.BlockSpec((1, D_MODEL), lambda i: (0, 0)), mod_spec(3), mod_spec(4),
                  rw_spec, rw_spec],
        out_specs=[pl.BlockSpec((tm, D_MODEL), lambda i: (i, 0)),
                   pl.BlockSpec((tm, LANES), lambda i: (i, 0)),
                   pl.BlockSpec((SUBLANES, tm), lambda i: (0, i)),
                   pl.BlockSpec((SUBLANES, LANES), lambda i: (0, 0))],
        out_shape=[jax.ShapeDtypeStruct((t, D_MODEL), F32), jax.ShapeDtypeStruct((t, LANES), F32),
                   jax.ShapeDtypeStruct((SUBLANES, t), F32), jax.ShapeDtypeStruct((SUBLANES, LANES), F32)],
        scratch_shapes=[pltpu.VMEM((tm, tm), BF16), pltpu.VMEM((SUBLANES, LANES), F32)],
        compiler_params=_params("arbitrary"),
        name="mix_route",
    )(xp, m1p, m2p, xs, m1s, m2s, w_out, w_out, mod_all, g.reshape(1, D_MODEL), mod_all, mod_all,
      rw_hi, rw_lo)


def _dispatch_kernel(d0_ref, d1_ref, fill_ref, x_ref, g_ref, sh_ref, sc_ref, xs_hbm, h_scr, zero_scr, sem, zsem):
    i = pl.program_id(0)
    n = pl.num_programs(0)
    tm = x_ref.shape[0]
    slot = i % 2

    def row_copy(r, dst, s):
        return pltpu.make_async_copy(h_scr.at[s, pl.ds(r, 1), :], xs_hbm.at[pl.ds(dst, 1), :], sem.at[s])

    def wait_rows(s):
        for _ in range(2):
            pltpu.make_async_copy(h_scr.at[s], xs_hbm.at[pl.ds(0, tm), :], sem.at[s]).wait()

    @pl.when(i >= 2)
    def _():
        wait_rows(slot)

    h_scr[slot] = _norm_mod(x_ref[...], g_ref[...], sh_ref[0], sc_ref[0])
    base = i * tm

    def body(r8, c):
        rb = pl.multiple_of(r8 * SUBLANES, SUBLANES)
        for k in range(SUBLANES):
            row_copy(rb + k, d0_ref[base + rb + k], slot).start()
            row_copy(rb + k, d1_ref[base + rb + k], slot).start()
        return c

    lax.fori_loop(0, tm // SUBLANES, body, 0)

    @pl.when(i == n - 1)
    def _():
        zero_scr[...] = jnp.zeros_like(zero_scr)

        def zero_row(r):
            return pltpu.make_async_copy(zero_scr.at[pl.ds(0, 1), :], xs_hbm.at[pl.ds(r, 1), :], zsem)

        def zero_block(b):
            start = pl.multiple_of(b * MOE_ZERO_ROWS, MOE_ZERO_ROWS)
            return pltpu.make_async_copy(zero_scr, xs_hbm.at[pl.ds(start, MOE_ZERO_ROWS), :], zsem)

        def start_all(copy):
            def body(r, c):
                copy(r).start()
                return c
            return body

        def wait_all(copy):
            def body(r, c):
                copy(r).wait()
                return c
            return body

        def zero_group(b):
            start = pl.multiple_of(b * SUBLANES, SUBLANES)
            return pltpu.make_async_copy(zero_scr.at[pl.ds(0, SUBLANES), :],
                                         xs_hbm.at[pl.ds(start, SUBLANES), :], zsem)

        for e in range(N_EXPERTS):
            lo = fill_ref[e]
            hi = fill_ref[N_EXPERTS + e]
            lo_group = (lo + SUBLANES - 1) // SUBLANES
            lax.fori_loop(lo, lo_group * SUBLANES, start_all(zero_row), 0)
            lax.fori_loop(lo_group, hi // SUBLANES, start_all(zero_group), 0)
            lax.fori_loop(lo, lo_group * SUBLANES, wait_all(zero_row), 0)
            lax.fori_loop(lo_group, hi // SUBLANES, wait_all(zero_group), 0)
        blocks_per_tile = MOE_ROW_TILE // MOE_ZERO_ROWS
        first = fill_ref[2 * N_EXPERTS] * blocks_per_tile
        last = (xs_hbm.shape[0] // MOE_ROW_TILE) * blocks_per_tile
        lax.fori_loop(first, last, start_all(zero_block), 0)
        lax.fori_loop(first, last, wait_all(zero_block), 0)
        wait_rows(slot)

        @pl.when(n >= 2)
        def _():
            wait_rows(1 - slot)


def moe_dispatch(x, g, mod_all, ctx_row, prompt_tokens, req_tokens, dest0, dest1, fill, n_tiles, tm=512):
    t = x.shape[0]
    pt = prompt_tokens // tm
    grid_spec = pltpu.PrefetchScalarGridSpec(
        num_scalar_prefetch=3,
        grid=(t // tm,),
        in_specs=[pl.BlockSpec((tm, D_MODEL), lambda i, *_: (i, 0)),
                  pl.BlockSpec((1, D_MODEL), lambda i, *_: (0, 0)),
                  _stream_mod_spec(3, pt, req_tokens // tm, ctx_row),
                  _stream_mod_spec(4, pt, req_tokens // tm, ctx_row)],
        out_specs=pl.BlockSpec(memory_space=pl.ANY),
        scratch_shapes=[pltpu.VMEM((2, tm, D_MODEL), F32), pltpu.VMEM((MOE_ZERO_ROWS, D_MODEL), F32),
                        pltpu.SemaphoreType.DMA((2,)), pltpu.SemaphoreType.DMA(())],
    )
    return pl.pallas_call(
        _dispatch_kernel,
        grid_spec=grid_spec,
        out_shape=jax.ShapeDtypeStruct((n_tiles * MOE_ROW_TILE, D_MODEL), F32),
        compiler_params=pltpu.CompilerParams(dimension_semantics=("arbitrary",),
                                             vmem_limit_bytes=VMEM_LIMIT_BYTES,
                                             disable_bounds_checks=True),
        name="moe_dispatch",
    )(dest0, dest1, fill, x, g.reshape(1, D_MODEL), mod_all, mod_all)


def _experts_kernel(te_ref, tv_ref, x_ref, wg_ref, wu_ref, wd_ref, o_ref, h_scr, acc_scr):
    i = pl.program_id(0)
    f = pl.program_id(1)
    last_f = pl.num_programs(1) - 1

    n_valid = tv_ref[i]
    half = h_scr.shape[0] // 2

    def swiglu_rows(n_rows):
        rows = slice(0, n_rows)

        @pl.when(f == 0)
        def _():
            h_scr[rows, :] = x_ref[rows, :].astype(BF16)
            acc_scr[rows, :] = jnp.zeros((n_rows, D_MODEL), F32)

        h = h_scr[rows, :]
        a = _dot(h, wg_ref[0].astype(BF16))
        u = _dot(h, wu_ref[0].astype(BF16))
        acc_scr[rows, :] += _dot((_silu(a) * u).astype(BF16), wd_ref[0].astype(BF16))

        @pl.when(f == last_f)
        def _():
            o_ref[rows, :] = acc_scr[rows, :]

    @pl.when(n_valid > half)
    def _():
        swiglu_rows(2 * half)

    @pl.when((n_valid > 0) & (n_valid <= half))
    def _():
        swiglu_rows(half)

        @pl.when(f == last_f)
        def _():
            o_ref[half:, :] = jnp.zeros((half, D_MODEL), F32)

    @pl.when((n_valid == 0) & (f == last_f))
    def _():
        o_ref[...] = jnp.zeros_like(o_ref)


def moe_experts(xs_sorted, tile_expert, tile_valid, w_gate, w_up, w_down, tf=512):
    rows = xs_sorted.shape[0]
    fdim = w_gate.shape[2]
    n_f = fdim // tf
    tr = MOE_ROW_TILE

    def f_eff(i, f, tv):
        return jnp.where(tv[i] > 0, f, n_f - 1)

    grid_spec = pltpu.PrefetchScalarGridSpec(
        num_scalar_prefetch=2,
        grid=(rows // tr, n_f),
        in_specs=[pl.BlockSpec((tr, D_MODEL), lambda i, f, te, tv: (i, 0)),
                  pl.BlockSpec((1, D_MODEL, tf), lambda i, f, te, tv: (te[i], 0, f_eff(i, f, tv))),
                  pl.BlockSpec((1, D_MODEL, tf), lambda i, f, te, tv: (te[i], 0, f_eff(i, f, tv))),
                  pl.BlockSpec((1, tf, D_MODEL), lambda i, f, te, tv: (te[i], f_eff(i, f, tv), 0))],
        out_specs=pl.BlockSpec((tr, D_MODEL), lambda i, f, te, tv: (i, 0)),
        scratch_shapes=[pltpu.VMEM((tr, D_MODEL), BF16), pltpu.VMEM((tr, D_MODEL), F32)],
    )
    return pl.pallas_call(
        _experts_kernel,
        grid_spec=grid_spec,
        out_shape=jax.ShapeDtypeStruct((rows, D_MODEL), F32),
        compiler_params=_params("arbitrary", "arbitrary"),
        name="moe_experts",
    )(tile_expert, tile_valid, xs_sorted, w_gate, w_up, w_down)


def _combine_kernel(d0_ref, d1_ref, x_ref, info_ref, gate_ref, fg_ref, ys_hbm, op_ref, os_ref, rbuf, sem,
                    *, prompt_tiles):
    i = pl.program_id(0)
    n = pl.num_programs(0)
    tm = x_ref.shape[0]
    slot = i % 2

    def issue(tile, s):
        base = tile * tm

        def body(r8, c):
            rb = pl.multiple_of(r8 * SUBLANES, SUBLANES)
            for k in range(SUBLANES):
                pltpu.make_async_copy(ys_hbm.at[pl.ds(d0_ref[base + rb + k], 1), :],
                                      rbuf.at[s, 0, pl.ds(rb + k, 1), :], sem.at[s]).start()
                pltpu.make_async_copy(ys_hbm.at[pl.ds(d1_ref[base + rb + k], 1), :],
                                      rbuf.at[s, 1, pl.ds(rb + k, 1), :], sem.at[s]).start()
            return c

        lax.fori_loop(0, tm // SUBLANES, body, 0)

    @pl.when(i == 0)
    def _():
        issue(0, 0)

    @pl.when(i + 1 < n)
    def _():
        issue(i + 1, 1 - slot)

    for k in range(2):
        pltpu.make_async_copy(ys_hbm.at[pl.ds(0, tm), :], rbuf.at[slot, k], sem.at[slot]).wait()

    info = info_ref[...]
    moe = info[:, INFO_G0:INFO_G0 + 1] * rbuf[slot, 0] + info[:, INFO_G1:INFO_G1 + 1] * rbuf[slot, 1]
    y = x_ref[...] + gate_ref[0] * moe
    out = y * lax.rsqrt(jnp.mean(y * y, axis=-1, keepdims=True) + EPS) * fg_ref[...]

    @pl.when(i < prompt_tiles)
    def _():
        op_ref[...] = out

    @pl.when(i >= prompt_tiles)
    def _():
        os_ref[...] = out


def moe_combine(x, info, ys_sorted, dest0, dest1, mod_all, ctx_row, prompt_tokens, req_tokens, final_g, tm=512):
    t = x.shape[0]
    pt = prompt_tokens // tm
    grid_spec = pltpu.PrefetchScalarGridSpec(
        num_scalar_prefetch=2,
        grid=(t // tm,),
        in_specs=[pl.BlockSpec((tm, D_MODEL), lambda i, *_: (i, 0)),
                  pl.BlockSpec((tm, LANES), lambda i, *_: (i, 0)),
                  _stream_mod_spec(5, pt, req_tokens // tm, ctx_row),
                  pl.BlockSpec((1, D_MODEL), lambda i, *_: (0, 0)),
                  pl.BlockSpec(memory_space=pl.ANY)],
        out_specs=[pl.BlockSpec((tm, D_MODEL), lambda i, *_: (jnp.minimum(i, pt - 1), 0)),
                   pl.BlockSpec((tm, D_MODEL), lambda i, *_: (jnp.maximum(i - pt, 0), 0))],
        scratch_shapes=[pltpu.VMEM((2, 2, tm, D_MODEL), F32), pltpu.SemaphoreType.DMA((2,))],
    )
    return pl.pallas_call(
        functools.partial(_combine_kernel, prompt_tiles=pt),
        grid_spec=grid_spec,
        out_shape=[jax.ShapeDtypeStruct((prompt_tokens, D_MODEL), F32),
                   jax.ShapeDtypeStruct((t - prompt_tokens, D_MODEL), F32)],
        compiler_params=pltpu.CompilerParams(dimension_semantics=("arbitrary",),
                                             vmem_limit_bytes=VMEM_LIMIT_BYTES,
                                             disable_bounds_checks=True),
        name="moe_combine",
    )(dest0, dest1, x, info, mod_all, final_g.reshape(1, D_MODEL), ys_sorted)


def moe_layout(counts, fields, n_tiles):
    nt_e = (counts + MOE_ROW_TILE - 1) // MOE_ROW_TILE
    ends = jnp.cumsum(nt_e)
    total = ends[-1]
    offset = (ends - nt_e) * MOE_ROW_TILE
    experts = jnp.arange(N_EXPERTS, dtype=jnp.int32)

    def dest(e_lane, r_lane):
        e = fields[e_lane].astype(jnp.int32)
        off = jnp.sum(jnp.where(e[:, None] == experts[None, :], offset[None, :], 0), axis=1)
        return (off + fields[r_lane].astype(jnp.int32)).astype(jnp.int32)

    fill = jnp.concatenate([offset + counts, ends * MOE_ROW_TILE, total[None]]).astype(jnp.int32)
    ids = jnp.arange(n_tiles, dtype=jnp.int32)
    ids_c = jnp.minimum(ids, total - 1)
    te = jnp.sum((ids_c[:, None] >= ends[None, :]).astype(jnp.int32), axis=1)
    mine = te[:, None] == experts[None, :]
    first_tile = jnp.sum(jnp.where(mine, (ends - nt_e)[None, :], 0), axis=1)
    count = jnp.sum(jnp.where(mine, counts[None, :], 0), axis=1)
    rows_left = jnp.clip(count - (ids - first_tile) * MOE_ROW_TILE, 0, MOE_ROW_TILE)
    tile_rows = jnp.where(ids < total, rows_left, 0)
    return (dest(INFO_E0, INFO_R0), dest(INFO_E1, INFO_R1), fill, te.astype(jnp.int32),
            tile_rows.astype(jnp.int32))


def moe_final(x, info, fields, cnt, g, mod_all, ctx_row, prompt_tokens, req_tokens, final_g, w_gate, w_up,
              w_down):
    t = x.shape[0]
    counts = cnt[0, :N_EXPERTS].astype(jnp.int32)
    n_tiles = (2 * t) // MOE_ROW_TILE + N_EXPERTS
    dest0, dest1, fill, te, tv = moe_layout(counts, fields, n_tiles)
    xs_sorted = moe_dispatch(x, g, mod_all, ctx_row, prompt_tokens, req_tokens, dest0, dest1, fill, n_tiles)
    ys_sorted = moe_experts(xs_sorted, te, tv, w_gate, w_up, w_down)
    return moe_combine(x, info, ys_sorted, dest0, dest1, mod_all, ctx_row, prompt_tokens, req_tokens, final_g)


def kernel(x_prompt, x_sample, state_s5, cache_na_k, cache_na_v, cache_diff_k, cache_diff_v, c, c_ctx, w_mod, b_mod, norm_mix_g, norm_ffn_g, final_norm_g, w_in_e, w_out_e, s5_lam_re, s5_lam_im, s5_log_dt, s5_b_re, s5_b_im, s5_c_re, s5_c_im, s5_d, s5_w_glu, s5_b_glu, na_rpb, ffn_w_gate, ffn_w_up, ffn_w_down, w_in_o, w_out_o, diff_lam_q1, diff_lam_k1, diff_lam_q2, diff_lam_k2, diff_subln_g, conv_w, conv_b, conv_ln_g, conv_ln_b, router_w, moe_w_gate, moe_w_up, moe_w_down):
    bp, lp, d = x_prompt.shape
    bs, ls, _ = x_sample.shape
    tm = 1024
    xp = x_prompt.reshape(bp * lp, d)
    xs = x_sample.reshape(bs * ls, d)
    rows_p = (bp * lp) // tm
    rows_s = ls // tm

    cond8 = jnp.concatenate([c, c_ctx[None, :], jnp.zeros((SUBLANES - bs - 1, d), F32)], axis=0)
    mod = adaln_all(cond8, w_mod, b_mod)
    mod_s = mod[:, 0:bs]
    mod_p = mod[:, bs:bs + 1]

    def tiles(rows, tile):
        return rows * tm // tile

    bmat, cmat, lam8 = s5_params(s5_lam_re[0], s5_lam_im[0], s5_log_dt[0], s5_b_re[0], s5_b_im[0],
                                 s5_c_re[0], s5_c_im[0])
    bias = na_bias_blocks(na_rpb[0])
    n_e = w_in_e.shape[-1]

    tiles_p = tiles(rows_p, 512)
    tiles_s = tiles(rows_s, 512)
    proj_p = in_proj(xp, norm_mix_g[0], mod_p[0], tiles_p, w_in_e[0])
    u_s, qkv_s = in_proj(xs, norm_mix_g[0], mod_s[0], tiles_s, w_in_e[0], n_f32=HALF_MIX)

    y_p, st_p = s5_scan(proj_p.reshape(bp, lp, n_e), bmat, cmat, lam8, None, 1)
    chunks = SUBLANES // bs
    h0 = state_s5[:, 0].reshape(bs, 2, 2, S5_GROUPS * S5_STATE)
    y_s, _ = s5_scan(u_s.reshape(bs * chunks, ls // chunks, HALF_MIX), bmat, cmat, lam8, h0, chunks)
    nao_p, na_k, na_v = na_ctx(proj_p.reshape(bp, lp, n_e))

    def heads_to_lanes(cache):
        return cache.transpose(0, 2, 1, 3).reshape(bs, cache.shape[2], HALF_MIX).astype(BF16)

    nao_s = na_lat(qkv_s.reshape(bs, ls, n_e - HALF_MIX), heads_to_lanes(cache_na_k[:, 0]),
                   heads_to_lanes(cache_na_v[:, 0]), bias)

    glu_w = (s5_d[0], s5_w_glu[0].astype(BF16), s5_b_glu[0])
    ffn_w = (w_out_e[0].astype(BF16), norm_ffn_g[0])
    ffn_w3 = (ffn_w_gate[0].astype(BF16), ffn_w_up[0].astype(BF16), ffn_w_down[0].astype(BF16))
    xp = mix_ffn(xp, y_p.reshape(bp * lp, HALF_MIX), proj_p, *glu_w, nao_p.reshape(bp * lp, HALF_MIX),
                 *ffn_w, mod_p[0], tiles_p, *ffn_w3)
    xs = mix_ffn(xs, y_s.reshape(bs * ls, HALF_MIX), u_s, *glu_w, nao_s.reshape(bs * ls, HALF_MIX),
                 *ffn_w, mod_s[0], tiles_s, *ffn_w3)

    lam_init = 0.8 - 0.6 * math.exp(-0.3 * 1)
    lam = (jnp.exp(jnp.sum(diff_lam_q1[0].astype(F32) * diff_lam_k1[0].astype(F32)))
           - jnp.exp(jnp.sum(diff_lam_q2[0].astype(F32) * diff_lam_k2[0].astype(F32)))
           + lam_init)
    cos, sin = rope_tables(ls)
    n_o = w_in_o.shape[-1]

    proj_p = in_proj(xp, norm_mix_g[1], mod_p[1], tiles_p, w_in_o[0])
    proj_s = in_proj(xs, norm_mix_g[1], mod_s[1], tiles_s, w_in_o[0])

    do_p, diff_k, diff_v = diff_ctx(proj_p.reshape(bp, lp, n_o), lam, diff_subln_g[0], lam_init)
    do_s = diff_lat(proj_s.reshape(bs, ls, n_o), cache_diff_k[:, 0:1], cache_diff_v[:, 0:1], cos, sin,
                    lam, diff_subln_g[0], lam_init)
    co_p = conformer_conv(proj_p.reshape(bp, lp, n_o), conv_w[0], conv_b[0], conv_ln_g[0], conv_ln_b[0], lp)
    co_s = conformer_conv(proj_s.reshape(bs, ls, n_o), conv_w[0], conv_b[0], conv_ln_g[0], conv_ln_b[0], 512)

    mod_all = mod[1]
    x_all, info, fields, cnt = mix_route(
        xp, do_p.reshape(bp * lp, HALF_MIX), co_p.reshape(bp * lp, HALF_MIX),
        xs, do_s.reshape(bs * ls, HALF_MIX), co_s.reshape(bs * ls, HALF_MIX),
        w_out_o[0].astype(BF16), norm_ffn_g[1], mod_all, bs, ls, router_w[0])
    yp, ys = moe_final(x_all, info, fields, cnt, norm_ffn_g[1], mod_all, bs, bp * lp, ls, final_norm_g,
                       moe_w_gate[0], moe_w_up[0], moe_w_down[0])

    new_state = st_p.reshape(bp, 1, 2, 2, S5_GROUPS, S5_STATE)
    return (yp.reshape(bp, lp, d), ys.reshape(bs, ls, d), new_state, na_k, na_v, diff_k, diff_v)
```

```python
import functools
import math

import jax
import jax.numpy as jnp
import numpy as np
from jax import lax
from jax.experimental import pallas as pl
from jax.experimental.pallas import tpu as pltpu

D_MODEL = 1024
DEPTH = 2
GRID_W = 64
HALF_MIX = 512
S5_GROUP_CH = 16
S5_GROUPS = 32
S5_STATE = 64
NA_HEAD_DIM = 64
NA_HEADS = 8
NA_WIN_R = 8
NA_WIN_C = 16
DIFF_D = 64
DIFF_HEAD_DIM = 128
DIFF_HEADS = 4
ROPE_BASE = 10000.0
CONV_WIDTH = 31
N_EXPERTS = 8
EPS = 1e-6

F32 = jnp.float32
BF16 = jnp.bfloat16
NEG_BIG = -1e30

VMEM_LIMIT_BYTES = 56 * 1024 * 1024
LANES = 128
SUBLANES = 8

S5_COL_GROUPS = 8
S5_COL_CH = S5_COL_GROUPS * S5_GROUP_CH
S5_COL_STATE = S5_COL_GROUPS * S5_STATE
S5_N_COL = S5_GROUPS // S5_COL_GROUPS
S5_TIME_BLOCK = 256


def _params(*sem):
    return pltpu.CompilerParams(dimension_semantics=sem, vmem_limit_bytes=VMEM_LIMIT_BYTES)


def _dot(a, b):
    return jnp.dot(a, b, preferred_element_type=F32)


def _dot_nt(a, b):
    return lax.dot_general(a, b, (((1,), (1,)), ((), ())), preferred_element_type=F32)


def _silu(x):
    return x * jax.nn.sigmoid(x)


def _norm_mod(x, g, shift, scale):
    y = x * lax.rsqrt(jnp.mean(x * x, axis=-1, keepdims=True) + EPS) * g
    return y * (1.0 + scale) + shift


def _mod_kernel(cond_ref, w_ref, b_ref, o_ref):
    s = _silu(cond_ref[...])
    o_ref[0, :, 0, :] = jnp.dot(s, w_ref[0], precision=lax.Precision.HIGHEST,
                                preferred_element_type=F32) + b_ref[0]


def adaln_all(cond8, w_mod, b_mod):
    tn = 1536
    n = w_mod.shape[-1]
    return pl.pallas_call(
        _mod_kernel,
        grid=(DEPTH, n // tn),
        in_specs=[pl.BlockSpec((SUBLANES, D_MODEL), lambda l, j: (0, 0)),
                  pl.BlockSpec((1, D_MODEL, tn), lambda l, j: (l, 0, j)),
                  pl.BlockSpec((1, 1, tn), lambda l, j: (l, 0, j))],
        out_specs=pl.BlockSpec((1, SUBLANES, 1, tn), lambda l, j: (l, 0, 0, j)),
        out_shape=jax.ShapeDtypeStruct((DEPTH, SUBLANES, 1, n), F32),
        compiler_params=_params("arbitrary", "arbitrary"),
        name="adaln_mod",
    )(cond8, w_mod, b_mod.reshape(DEPTH, 1, n))


def _mod_spec(chunk, tiles_per_row):
    return pl.BlockSpec((1, 1, D_MODEL), lambda i, *_: (i // tiles_per_row, 0, chunk))


def _in_proj_kernel(x_ref, g_ref, sh_ref, sc_ref, w_ref, *refs, n_f32):
    *outs, wb_scr = refs

    @pl.when(pl.program_id(0) == 0)
    def _():
        wb_scr[...] = w_ref[...].astype(BF16)

    h = _norm_mod(x_ref[...], g_ref[...], sh_ref[0], sc_ref[0]).astype(BF16)
    y = _dot(h, wb_scr[...])
    if len(outs) == 1:
        outs[0][...] = y
    else:
        outs[0][...] = y[:, :n_f32]
        outs[1][...] = y[:, n_f32:].astype(BF16)


def in_proj(x, g, mod, tiles_per_row, w, n_f32=None, tm=512):
    t = x.shape[0]
    n = w.shape[1]
    if n_f32 is None:
        out_specs = pl.BlockSpec((tm, n), lambda i: (i, 0))
        out_shape = jax.ShapeDtypeStruct((t, n), F32)
    else:
        out_specs = [pl.BlockSpec((tm, n_f32), lambda i: (i, 0)), pl.BlockSpec((tm, n - n_f32), lambda i: (i, 0))]
        out_shape = [jax.ShapeDtypeStruct((t, n_f32), F32), jax.ShapeDtypeStruct((t, n - n_f32), BF16)]
    return pl.pallas_call(
        functools.partial(_in_proj_kernel, n_f32=n_f32),
        grid=(t // tm,),
        in_specs=[pl.BlockSpec((tm, D_MODEL), lambda i: (i, 0)),
                  pl.BlockSpec((1, D_MODEL), lambda i: (0, 0)),
                  _mod_spec(0, tiles_per_row),
                  _mod_spec(1, tiles_per_row),
                  pl.BlockSpec((D_MODEL, n), lambda i: (0, 0), pipeline_mode=pl.Buffered(1))],
        out_specs=out_specs,
        out_shape=out_shape,
        scratch_shapes=[pltpu.VMEM((D_MODEL, n), BF16)],
        compiler_params=_params("arbitrary"),
        name="in_proj",
    )(x, g.reshape(1, D_MODEL), mod, mod, w)


def _s5_scan_kernel(*refs, seq, chunks, has_init):
    if has_init:
        u_ref, bm_ref, cm_ref, lam_ref, h0_ref, y_ref, st_ref, bu_scr, hb_scr, ytm_scr = refs
    else:
        u_ref, bm_ref, cm_ref, lam_ref, y_ref, st_ref, bu_scr, hb_scr, ytm_scr = refs
        h0_ref = None
    tb = S5_TIME_BLOCK
    n_tb = seq // tb
    ns = S5_COL_STATE
    row = lax.broadcasted_iota(jnp.int32, (SUBLANES, ns), 0)
    piece = row % chunks

    for d in range(2):
        lam = lam_ref[d, 0]
        lr, li = lam[:, :ns], lam[:, ns:]
        bm = bm_ref[d, 0]
        cm = cm_ref[d, 0]
        blocks = list(range(n_tb)) if d == 0 else list(range(n_tb - 1, -1, -1))

        def load_bu(k):
            ub = u_ref[:, k * tb:(k + 1) * tb, :]
            utm = jnp.swapaxes(ub, 0, 1).reshape(tb * SUBLANES, S5_COL_CH).astype(BF16)
            bu_scr[...] = _dot(utm, bm).reshape(tb, SUBLANES, 2 * ns)

        def scan_block(h, store):
            def advance(t, hr, hi):
                b = bu_scr[t]
                return lr * hr - li * hi + b[:, :ns], lr * hi + li * hr + b[:, ns:]

            def step(s, carry):
                t_a = (tb - 1 - 2 * s) if d == 1 else 2 * s
                t_b = t_a - 1 if d == 1 else t_a + 1
                ar, ai = advance(t_a, *carry)
                br, bi = advance(t_b, ar, ai)
                if store:
                    first = jnp.concatenate([br, bi] if d == 1 else [ar, ai], axis=1)
                    second = jnp.concatenate([ar, ai] if d == 1 else [br, bi], axis=1)
                    t_lo = t_b if d == 1 else t_a
                    rows = pl.ds(pl.multiple_of(t_lo * SUBLANES, 2 * SUBLANES), 2 * SUBLANES)
                    hb_scr[rows, :] = jnp.concatenate([first, second], axis=0).astype(BF16)
                return br, bi
            return lax.fori_loop(0, tb // 2, step, h)

        zero = jnp.zeros((SUBLANES, ns), F32)
        if chunks > 1:
            h = (zero, zero)
            for k in blocks:
                load_bu(k)
                h = scan_block(h, False)
            fr, fi = h
            pr, pi = lr, li
            for _ in range(int(math.log2(seq))):
                pr, pi = pr * pr - pi * pi, 2.0 * pr * pi
            edge = 0 if d == 0 else chunks - 1
            shift = 1 if d == 0 else SUBLANES - 1
            if has_init:
                h0r = h0_ref[:, d, 0, :]
                h0i = h0_ref[:, d, 1, :]
                seq_of_row = row // chunks
                er, ei = zero, zero
                for b in range(SUBLANES // chunks):
                    er = jnp.where(seq_of_row == b, h0r[b:b + 1, :], er)
                    ei = jnp.where(seq_of_row == b, h0i[b:b + 1, :], ei)
            else:
                er, ei = zero, zero
            is_edge = piece == edge
            cr = jnp.where(is_edge, er, zero)
            ci = jnp.where(is_edge, ei, zero)
            for _ in range(chunks - 1):
                tr = fr + pr * cr - pi * ci
                ti = fi + pr * ci + pi * cr
                cr = jnp.where(is_edge, er, pltpu.roll(tr, shift, 0))
                ci = jnp.where(is_edge, ei, pltpu.roll(ti, shift, 0))
            h = (cr, ci)
        else:
            if has_init:
                h = (h0_ref[:, d, 0, :], h0_ref[:, d, 1, :])
            else:
                h = (zero, zero)

        for k in blocks:
            load_bu(k)
            h = scan_block(h, True)
            yb = _dot(hb_scr[...], cm).reshape(tb, SUBLANES, S5_COL_CH)
            if d == 0:
                ytm_scr[k * tb:(k + 1) * tb] = yb
            else:
                ytm_scr[k * tb:(k + 1) * tb] += yb
        st_ref[:, d, 0, :] = h[0]
        st_ref[:, d, 1, :] = h[1]

    y_ref[...] = jnp.swapaxes(ytm_scr[...], 0, 1)


def s5_scan(proj3, bmat, cmat, lam8, h0, chunks):
    rows, seq, _ = proj3.shape
    ns = S5_COL_STATE
    has_init = h0 is not None
    in_specs = [pl.BlockSpec((SUBLANES, seq, S5_COL_CH), lambda i, c: (i, 0, c)),
                pl.BlockSpec((2, 1, S5_COL_CH, 2 * ns), lambda i, c: (0, c, 0, 0)),
                pl.BlockSpec((2, 1, 2 * ns, S5_COL_CH), lambda i, c: (0, c, 0, 0)),
                pl.BlockSpec((2, 1, SUBLANES, 2 * ns), lambda i, c: (0, c, 0, 0))]
    args = [proj3, bmat, cmat, lam8]
    if has_init:
        nb = h0.shape[0]
        in_specs.append(pl.BlockSpec((nb, 2, 2, ns), lambda i, c: (0, 0, 0, c)))
        args.append(h0)
    y, st = pl.pallas_call(
        functools.partial(_s5_scan_kernel, seq=seq, chunks=chunks, has_init=has_init),
        grid=(rows // SUBLANES, S5_N_COL),
        in_specs=in_specs,
        out_specs=[pl.BlockSpec((SUBLANES, seq, S5_COL_CH), lambda i, c: (i, 0, c)),
                   pl.BlockSpec((SUBLANES, 2, 2, ns), lambda i, c: (i, 0, 0, c))],
        out_shape=[jax.ShapeDtypeStruct((rows, seq, HALF_MIX), F32),
                   jax.ShapeDtypeStruct((rows, 2, 2, S5_GROUPS * S5_STATE), F32)],
        scratch_shapes=[pltpu.VMEM((S5_TIME_BLOCK, SUBLANES, 2 * ns), F32),
                        pltpu.VMEM((S5_TIME_BLOCK * SUBLANES, 2 * ns), BF16),
                        pltpu.VMEM((seq, SUBLANES, S5_COL_CH), F32)],
        compiler_params=_params("arbitrary", "arbitrary"),
        name="s5_scan",
    )(*args)
    return y, st


def s5_params(lam_re, lam_im, log_dt, b_re, b_im, c_re, c_im):
    lr = lam_re.astype(F32)
    li = lam_im.astype(F32)
    dt = jnp.exp(log_dt.astype(F32))[..., None]
    mag = jnp.exp(lr * dt)
    bar_re = mag * jnp.cos(li * dt)
    bar_im = mag * jnp.sin(li * dt)
    den = lr * lr + li * li
    q_re = ((bar_re - 1.0) * lr + bar_im * li) / den
    q_im = (bar_im * lr - (bar_re - 1.0) * li) / den
    br = b_re.astype(F32)
    bi = b_im.astype(F32)
    b_bar_re = q_re[..., None] * br - q_im[..., None] * bi
    b_bar_im = q_re[..., None] * bi + q_im[..., None] * br
    eye = jnp.eye(S5_COL_GROUPS, dtype=F32)

    def block_diag_b(m):
        m = m.reshape(2, S5_N_COL, S5_COL_GROUPS, S5_STATE, S5_GROUP_CH)
        bd = jnp.einsum('dngpc,gh->dngchp', m, eye)
        return bd.reshape(2, S5_N_COL, S5_COL_CH, S5_COL_STATE)

    def block_diag_c(m):
        m = m.reshape(2, S5_N_COL, S5_COL_GROUPS, S5_GROUP_CH, S5_STATE)
        bd = jnp.einsum('dngcp,gh->dngphc', m, eye)
        return bd.reshape(2, S5_N_COL, S5_COL_STATE, S5_COL_CH)

    bmat = jnp.concatenate([block_diag_b(b_bar_re), block_diag_b(b_bar_im)], axis=-1).astype(BF16)
    cmat = jnp.concatenate([block_diag_c(c_re.astype(F32)), block_diag_c(-c_im.astype(F32))],
                           axis=-2).astype(BF16)
    lam_cat = jnp.concatenate([bar_re.reshape(2, S5_N_COL, S5_COL_STATE),
                               bar_im.reshape(2, S5_N_COL, S5_COL_STATE)], axis=-1)
    lam8 = jnp.broadcast_to(lam_cat[:, :, None, :], (2, S5_N_COL, SUBLANES, 2 * S5_COL_STATE))
    return bmat, cmat, lam8


def _na_ctx_kernel(q_ref, k_ref, v_ref, o_ref, ko_ref, vo_ref):
    seq = q_ref.shape[1]
    lane = lax.broadcasted_iota(jnp.int32, (seq, LANES), 1)
    low = lane < NA_HEAD_DIM
    outs = []
    for pr in range(NA_HEADS // 2):
        cols = slice(pr * LANES, (pr + 1) * LANES)
        qp = q_ref[0, :, cols] * (NA_HEAD_DIM ** -0.5)
        kp = k_ref[0, :, cols]
        vp = v_ref[0, :, cols]
        for half in range(2):
            sl = slice(half * NA_HEAD_DIM, (half + 1) * NA_HEAD_DIM)
            ko_ref[0, 0, 2 * pr + half] = kp[:, sl]
            vo_ref[0, 0, 2 * pr + half] = vp[:, sl]
        kb = kp.astype(BF16)
        vb = vp.astype(BF16)
        o_pair = None
        for half in range(2):
            qm = jnp.where(low if half == 0 else jnp.logical_not(low), qp, 0.0).astype(BF16)
            s = _dot_nt(qm, kb)
            p = jnp.exp(s - jnp.max(s, axis=-1, keepdims=True))
            o = _dot(p.astype(BF16), vb) * (1.0 / jnp.sum(p, axis=-1, keepdims=True))
            o_pair = o if half == 0 else jnp.where(low, o_pair, o)
        outs.append(o_pair)
    o_ref[0] = jnp.concatenate(outs, axis=-1).astype(o_ref.dtype)


def na_ctx(proj3):
    b, seq, _ = proj3.shape
    cache_shape = jax.ShapeDtypeStruct((b, 1, NA_HEADS, seq, NA_HEAD_DIM), F32)
    cache_spec = pl.BlockSpec((1, 1, NA_HEADS, seq, NA_HEAD_DIM), lambda i: (i, 0, 0, 0, 0))
    return pl.pallas_call(
        _na_ctx_kernel,
        grid=(b,),
        in_specs=[pl.BlockSpec((1, seq, HALF_MIX), lambda i: (i, 0, 1)),
                  pl.BlockSpec((1, seq, HALF_MIX), lambda i: (i, 0, 2)),
                  pl.BlockSpec((1, seq, HALF_MIX), lambda i: (i, 0, 3))],
        out_specs=[pl.BlockSpec((1, seq, HALF_MIX), lambda i: (i, 0, 0)), cache_spec, cache_spec],
        out_shape=[jax.ShapeDtypeStruct((b, seq, HALF_MIX), BF16), cache_shape, cache_shape],
        compiler_params=_params("arbitrary"),
        name="na_ctx",
    )(proj3, proj3, proj3)


NA_Q_ROWS = 4
NA_KEY_ROWS = 12


def na_bias_blocks(rpb):
    qcol = np.arange(GRID_W)
    cc = np.arange(GRID_W)
    cs = np.clip(qcol - NA_WIN_C // 2, 0, GRID_W - NA_WIN_C)
    valid = (cc[None, :] >= cs[:, None]) & (cc[None, :] < cs[:, None] + NA_WIN_C)
    coff = cc[None, :] - qcol[:, None] + (NA_WIN_C - 1)
    n_col = 2 * NA_WIN_C - 1
    sel = ((coff[None] == np.arange(n_col)[:, None, None]) & valid[None]).astype(np.float32)
    sel = sel.reshape(n_col, GRID_W * GRID_W)
    mask = np.where(valid, 0.0, NEG_BIG).astype(np.float32).reshape(1, GRID_W * GRID_W)
    n_row = 2 * NA_WIN_R - 1
    t1 = jnp.dot(rpb.astype(F32).reshape(NA_HEADS * n_row, n_col), jnp.asarray(sel),
                 precision=lax.Precision.HIGHEST) + jnp.asarray(mask)
    t1 = t1.reshape(NA_HEADS, n_row, GRID_W, GRID_W)
    return pl.pallas_call(
        _na_bias_kernel,
        grid=(NA_HEADS,),
        in_specs=[pl.BlockSpec((1, n_row, GRID_W, GRID_W), lambda h: (h, 0, 0, 0))],
        out_specs=pl.BlockSpec((3, 1, NA_Q_ROWS * GRID_W, NA_KEY_ROWS * GRID_W), lambda h: (0, h, 0, 0)),
        out_shape=jax.ShapeDtypeStruct((3, NA_HEADS, NA_Q_ROWS * GRID_W, NA_KEY_ROWS * GRID_W), F32),
        compiler_params=_params("arbitrary"),
        name="na_bias",
    )(t1)


def _na_bias_kernel(t1_ref, o_ref):
    variants = ((lambda ri: 0, NA_WIN_R - 1), (lambda ri: ri, NA_WIN_R // 2 - 1),
                (lambda ri: NA_KEY_ROWS - NA_WIN_R, -1))
    outside = jnp.full((GRID_W, GRID_W), NEG_BIG, F32)
    for v, (lo_of, shift) in enumerate(variants):
        for ri in range(NA_Q_ROWS):
            for wr in range(NA_KEY_ROWS):
                inside = lo_of(ri) <= wr < lo_of(ri) + NA_WIN_R
                o_ref[v, 0, ri * GRID_W:(ri + 1) * GRID_W, wr * GRID_W:(wr + 1) * GRID_W] = (
                    t1_ref[0, wr - ri + shift] if inside else outside)


def _na_lat_kernel(q_ref, k_ref, v_ref, kc_ref, vc_ref, bias_ref, o_ref):
    qb = pl.program_id(1)
    rows = k_ref.shape[1] // GRID_W
    nk = NA_KEY_ROWS * GRID_W
    first_row = jnp.clip(qb * NA_Q_ROWS - NA_WIN_R // 2, 0, rows - NA_KEY_ROWS)
    start = pl.multiple_of(first_row * GRID_W, GRID_W)
    tq = q_ref.shape[1]
    lane = lax.broadcasted_iota(jnp.int32, (tq, LANES), 1)
    low = lane < NA_HEAD_DIM
    outs = []
    for pr in range(NA_HEADS // 2):
        cols = slice(pr * LANES, (pr + 1) * LANES)
        qp = q_ref[0, :, cols].astype(F32) * (NA_HEAD_DIM ** -0.5)
        kw = k_ref[0, pl.ds(start, nk), cols]
        vw = v_ref[0, pl.ds(start, nk), cols]
        kc = kc_ref[0, :, cols]
        vc = vc_ref[0, :, cols]
        o_pair = None
        for half in range(2):
            qm = jnp.where(low if half == 0 else jnp.logical_not(low), qp, 0.0).astype(BF16)
            s_loc = _dot_nt(qm, kw) + bias_ref[0, 2 * pr + half]
            s_ctx = _dot_nt(qm, kc)
            m = jnp.maximum(jnp.max(s_loc, axis=-1, keepdims=True), jnp.max(s_ctx, axis=-1, keepdims=True))
            p_loc = jnp.exp(s_loc - m)
            p_ctx = jnp.exp(s_ctx - m)
            inv = 1.0 / (jnp.sum(p_loc, axis=-1, keepdims=True) + jnp.sum(p_ctx, axis=-1, keepdims=True))
            o = (_dot(p_loc.astype(BF16), vw) + _dot(p_ctx.astype(BF16), vc)) * inv
            o_pair = o if half == 0 else jnp.where(low, o_pair, o)
        outs.append(o_pair)
    o_ref[0] = jnp.concatenate(outs, axis=-1).astype(o_ref.dtype)


def na_lat(qkv3, k_ctx, v_ctx, bias):
    b, seq, _ = qkv3.shape
    tq = NA_Q_ROWS * GRID_W
    n_q = seq // tq
    lc = k_ctx.shape[1]
    ctx_spec = pl.BlockSpec((1, lc, HALF_MIX), lambda i, r: (i, 0, 0))
    return pl.pallas_call(
        _na_lat_kernel,
        grid=(b, n_q),
        in_specs=[pl.BlockSpec((1, tq, HALF_MIX), lambda i, r: (i, r, 0)),
                  pl.BlockSpec((1, seq, HALF_MIX), lambda i, r: (i, 0, 1)),
                  pl.BlockSpec((1, seq, HALF_MIX), lambda i, r: (i, 0, 2)),
                  ctx_spec, ctx_spec,
                  pl.BlockSpec((1, NA_HEADS, tq, NA_KEY_ROWS * GRID_W),
                               lambda i, r: (jnp.where(r == 0, 0, jnp.where(r == n_q - 1, 2, 1)), 0, 0, 0))],
        out_specs=pl.BlockSpec((1, tq, HALF_MIX), lambda i, r: (i, r, 0)),
        out_shape=jax.ShapeDtypeStruct((b, seq, HALF_MIX), BF16),
        compiler_params=_params("arbitrary", "arbitrary"),
        name="na_lat",
    )(qkv3, qkv3, qkv3, k_ctx, v_ctx, bias)


def _softmax_pair_diff(s1, s2, lam):
    p1 = jnp.exp(s1 - jnp.max(s1, axis=-1, keepdims=True))
    p2 = jnp.exp(s2 - jnp.max(s2, axis=-1, keepdims=True))
    inv1 = 1.0 / jnp.sum(p1, axis=-1, keepdims=True)
    inv2 = lam / jnp.sum(p2, axis=-1, keepdims=True)
    return p1 * inv1 - p2 * inv2


def _sub_ln(o, g, lam_init):
    return o * lax.rsqrt(jnp.mean(o * o, axis=-1, keepdims=True) + EPS) * g * (1.0 - lam_init)


def _diff_ctx_kernel(lam_ref, q_ref, k_ref, v_ref, g_ref, o_ref, ko_ref, vo_ref, *, lam_init):
    scale = DIFF_D ** -0.5
    lam = lam_ref[0, 0]
    lane = lax.broadcasted_iota(jnp.int32, (q_ref.shape[1], DIFF_HEAD_DIM), 1)
    first = lane < DIFF_D
    for h in range(DIFF_HEADS):
        sl = slice(h * DIFF_HEAD_DIM, (h + 1) * DIFF_HEAD_DIM)
        qh = q_ref[0, :, sl]
        kh = k_ref[0, :, sl]
        vh = v_ref[0, :, sl]
        ko_ref[0, 0, h] = kh
        vo_ref[0, 0, h] = vh
        kb = kh.astype(BF16)
        s1 = _dot_nt(jnp.where(first, qh, 0.0).astype(BF16), kb) * scale
        s2 = _dot_nt(jnp.where(first, 0.0, qh).astype(BF16), kb) * scale
        a = _softmax_pair_diff(s1, s2, lam)
        o = _dot(a.astype(BF16), vh.astype(BF16))
        o_ref[0, :, sl] = _sub_ln(o, g_ref[...], lam_init).astype(o_ref.dtype)


def diff_ctx(proj3, lam, subln_g, lam_init):
    b, seq, _ = proj3.shape
    cache_shape = jax.ShapeDtypeStruct((b, 1, DIFF_HEADS, seq, DIFF_HEAD_DIM), F32)
    cache_spec = pl.BlockSpec((1, 1, DIFF_HEADS, seq, DIFF_HEAD_DIM), lambda i: (i, 0, 0, 0, 0))
    return pl.pallas_call(
        functools.partial(_diff_ctx_kernel, lam_init=lam_init),
        grid=(b,),
        in_specs=[pl.BlockSpec(memory_space=pltpu.SMEM),
                  pl.BlockSpec((1, seq, HALF_MIX), lambda i: (i, 0, 0)),
                  pl.BlockSpec((1, seq, HALF_MIX), lambda i: (i, 0, 1)),
                  pl.BlockSpec((1, seq, HALF_MIX), lambda i: (i, 0, 2)),
                  pl.BlockSpec((1, DIFF_HEAD_DIM), lambda i: (0, 0))],
        out_specs=[pl.BlockSpec((1, seq, HALF_MIX), lambda i: (i, 0, 0)), cache_spec, cache_spec],
        out_shape=[jax.ShapeDtypeStruct((b, seq, HALF_MIX), BF16), cache_shape, cache_shape],
        compiler_params=_params("arbitrary"),
        name="diff_ctx",
    )(lam.reshape(1, 1), proj3, proj3, proj3, subln_g.reshape(1, DIFF_HEAD_DIM))


def rope_tables(seq):
    t = np.arange(seq)
    row = (t // GRID_W).astype(np.float32)
    col = (t % GRID_W).astype(np.float32)
    n_freq = DIFF_D // 4
    inv = np.float32(ROPE_BASE) ** (-np.arange(n_freq, dtype=np.float32) / np.float32(n_freq))
    ang = np.concatenate([row[:, None] * inv, col[:, None] * inv], axis=-1)
    cos = np.repeat(np.cos(ang), 2, axis=-1)
    sin = np.repeat(np.sin(ang), 2, axis=-1)
    sign = np.where(np.arange(DIFF_D) % 2 == 0, -1.0, 1.0).astype(np.float32)
    sin = sin * sign
    return (jnp.asarray(np.tile(cos, (1, 2)).astype(np.float32)),
            jnp.asarray(np.tile(sin, (1, 2)).astype(np.float32)))


def _rope(x, cos, sin_signed):
    lane = lax.broadcasted_iota(jnp.int32, x.shape, 1)
    nxt = pltpu.roll(x, x.shape[1] - 1, 1)
    prv = pltpu.roll(x, 1, 1)
    partner = jnp.where(lane % 2 == 0, nxt, prv)
    return x * cos + partner * sin_signed


def _diff_lat_kernel(lam_ref, q_ref, k_ref, v_ref, kc_ref, vc_ref, cq_ref, sq_ref, ck_ref, sk_ref,
                     g_ref, o_ref, k_all, v_all, *, lam_init):
    seq = k_ref.shape[1]

    @pl.when(pl.program_id(2) == 0)
    def _():
        k_all[0:seq, :] = _rope(k_ref[0], ck_ref[...], sk_ref[...]).astype(BF16)
        k_all[seq:, :] = kc_ref[0, 0, 0].astype(BF16)
        v_all[0:seq, :] = v_ref[0].astype(BF16)
        v_all[seq:, :] = vc_ref[0, 0, 0].astype(BF16)

    lam = lam_ref[0, 0]
    q = _rope(q_ref[0], cq_ref[...], sq_ref[...]) * (DIFF_D ** -0.5)
    lane = lax.broadcasted_iota(jnp.int32, q.shape, 1)
    first = lane < DIFF_D
    kb = k_all[...]
    s1 = _dot_nt(jnp.where(first, q, 0.0).astype(BF16), kb)
    s2 = _dot_nt(jnp.where(first, 0.0, q).astype(BF16), kb)
    a = _softmax_pair_diff(s1, s2, lam)
    o = _dot(a.astype(BF16), v_all[...])
    o_ref[0] = _sub_ln(o, g_ref[...], lam_init).astype(o_ref.dtype)


def diff_lat(proj3, k_ctx, v_ctx, cos, sin, lam, subln_g, lam_init, tq=512):
    b, seq, _ = proj3.shape
    lc = k_ctx.shape[3]
    hd = DIFF_HEAD_DIM
    ctx_spec = pl.BlockSpec((1, 1, 1, lc, hd), lambda i, h, q: (i, 0, h, 0, 0))
    tq_spec = pl.BlockSpec((tq, hd), lambda i, h, q: (q, 0))
    full_spec = pl.BlockSpec((seq, hd), lambda i, h, q: (0, 0))
    return pl.pallas_call(
        functools.partial(_diff_lat_kernel, lam_init=lam_init),
        grid=(b, DIFF_HEADS, seq // tq),
        in_specs=[pl.BlockSpec(memory_space=pltpu.SMEM),
                  pl.BlockSpec((1, tq, hd), lambda i, h, q: (i, q, h)),
                  pl.BlockSpec((1, seq, hd), lambda i, h, q: (i, 0, DIFF_HEADS + h)),
                  pl.BlockSpec((1, seq, hd), lambda i, h, q: (i, 0, 2 * DIFF_HEADS + h)),
                  ctx_spec, ctx_spec, tq_spec, tq_spec, full_spec, full_spec,
                  pl.BlockSpec((1, hd), lambda i, h, q: (0, 0))],
        out_specs=pl.BlockSpec((1, tq, hd), lambda i, h, q: (i, q, h)),
        out_shape=jax.ShapeDtypeStruct((b, seq, HALF_MIX), BF16),
        scratch_shapes=[pltpu.VMEM((seq + lc, hd), BF16), pltpu.VMEM((seq + lc, hd), BF16)],
        compiler_params=_params("arbitrary", "arbitrary", "arbitrary"),
        name="diff_lat",
    )(lam.reshape(1, 1), proj3, proj3, proj3, k_ctx, v_ctx, cos, sin, cos, sin,
      subln_g.reshape(1, hd))


CONV_PAD = 16
CONV_SUB = 64


def _conv_kernel(a_ref, g_ref, ap_ref, gp_ref, an_ref, gn_ref, w_ref, b_ref, lg_ref, lb_ref, o_ref, xp_scr,
                 xsh_scr):
    t = pl.program_id(1)
    tt = a_ref.shape[1]
    prev = ap_ref[0] * jax.nn.sigmoid(gp_ref[0])
    nxt = an_ref[0] * jax.nn.sigmoid(gn_ref[0])
    xp_scr[0:CONV_PAD, :] = jnp.where(t > 0, prev, 0.0)
    xp_scr[CONV_PAD + tt:, :] = jnp.where(t < pl.num_programs(1) - 1, nxt, 0.0)
    xp_scr[CONV_PAD:CONV_PAD + tt, :] = a_ref[0] * jax.nn.sigmoid(g_ref[0])
    first_tap = CONV_PAD - CONV_WIDTH // 2
    n_rows = xsh_scr.shape[1]
    for b in range(SUBLANES):
        xsh_scr[b] = xp_scr[b:b + n_rows, :]
    for i in range(tt // CONV_SUB):
        s = i * CONV_SUB
        acc = jnp.zeros((CONV_SUB, HALF_MIX), F32)
        for j in range(CONV_WIDTH):
            whole, phase = divmod(first_tap + j, SUBLANES)
            lo = s + whole * SUBLANES
            acc = acc + xsh_scr[phase, lo:lo + CONV_SUB, :] * w_ref[j:j + 1, :]
        y = acc + b_ref[...]
        mu = jnp.mean(y, axis=-1, keepdims=True)
        yc = y - mu
        var = jnp.mean(yc * yc, axis=-1, keepdims=True)
        yn = yc * lax.rsqrt(var + EPS) * lg_ref[...] + lb_ref[...]
        o_ref[0, s:s + CONV_SUB, :] = _silu(yn).astype(o_ref.dtype)


def conformer_conv(proj3, w, b, ln_g, ln_b, tt):
    bsz, seq, _ = proj3.shape
    n_t = seq // tt
    hb = tt // CONV_PAD
    last = seq // CONV_PAD - 1
    vec = pl.BlockSpec((1, HALF_MIX), lambda i, t: (0, 0))

    def main(col):
        return pl.BlockSpec((1, tt, HALF_MIX), lambda i, t: (i, t, col))

    def prev(col):
        return pl.BlockSpec((1, CONV_PAD, HALF_MIX), lambda i, t: (i, jnp.maximum(t * hb - 1, 0), col))

    def nxt(col):
        return pl.BlockSpec((1, CONV_PAD, HALF_MIX), lambda i, t: (i, jnp.minimum((t + 1) * hb, last), col))

    return pl.pallas_call(
        _conv_kernel,
        grid=(bsz, n_t),
        in_specs=[main(3), main(4), prev(3), prev(4), nxt(3), nxt(4),
                  pl.BlockSpec((CONV_WIDTH, HALF_MIX), lambda i, t: (0, 0)),
                  vec, vec, vec],
        out_specs=pl.BlockSpec((1, tt, HALF_MIX), lambda i, t: (i, t, 0)),
        out_shape=jax.ShapeDtypeStruct((bsz, seq, HALF_MIX), BF16),
        scratch_shapes=[pltpu.VMEM((tt + 2 * CONV_PAD, HALF_MIX), F32),
                        pltpu.VMEM((SUBLANES, tt + 2 * CONV_PAD - SUBLANES, HALF_MIX), F32)],
        compiler_params=_params("arbitrary", "arbitrary"),
        name="conformer_conv",
    )(proj3, proj3, proj3, proj3, proj3, proj3, w, b.reshape(1, HALF_MIX), ln_g.reshape(1, HALF_MIX),
      ln_b.reshape(1, HALF_MIX))


FFN_CHUNK = 1024


def _mix_ffn_kernel(x_ref, y_ref, u_ref, d_ref, wglu_ref, bglu_ref, m2_ref, w1_ref, w2_ref, g1_ref,
                    g_ref, sh_ref, sc_ref, g2_ref, wg_ref, wu_ref, wd_ref, o_ref):
    z = jax.nn.gelu(u_ref[...] * d_ref[...] + y_ref[...])
    s5_out = z * jax.nn.sigmoid(_dot(z.astype(BF16), wglu_ref[...]) + bglu_ref[...])
    mix = _dot(s5_out.astype(BF16), w1_ref[...]) + _dot(m2_ref[...], w2_ref[...])
    x1 = x_ref[...] + g1_ref[0] * mix
    h = _norm_mod(x1, g_ref[...], sh_ref[0], sc_ref[0]).astype(BF16)
    fdim = wg_ref.shape[1]
    acc = None
    for lo in range(0, fdim, FFN_CHUNK):
        hi = min(lo + FFN_CHUNK, fdim)
        a = _dot(h, wg_ref[:, lo:hi])
        up = _dot(h, wu_ref[:, lo:hi])
        part = _dot((_silu(a) * up).astype(BF16), wd_ref[lo:hi, :])
        acc = part if acc is None else acc + part
    o_ref[...] = x1 + g2_ref[0] * acc


def mix_ffn(x, y, u, d_skip, w_glu, b_glu, m2, w_out, g, mod, tiles_per_row, w_gate, w_up, w_down, tm=512):
    t = x.shape[0]
    fdim = w_gate.shape[1]
    half = pl.BlockSpec((tm, HALF_MIX), lambda i: (i, 0))
    vec = pl.BlockSpec((1, HALF_MIX), lambda i: (0, 0))

    def resident(shape, index):
        return pl.BlockSpec(shape, index, pipeline_mode=pl.Buffered(1))

    return pl.pallas_call(
        _mix_ffn_kernel,
        grid=(t // tm,),
        in_specs=[pl.BlockSpec((tm, D_MODEL), lambda i: (i, 0)), half, half, vec,
                  resident((HALF_MIX, HALF_MIX), lambda i: (0, 0)), vec, half,
                  resident((HALF_MIX, D_MODEL), lambda i: (0, 0)),
                  resident((HALF_MIX, D_MODEL), lambda i: (1, 0)),
                  _mod_spec(2, tiles_per_row),
                  pl.BlockSpec((1, D_MODEL), lambda i: (0, 0)),
                  _mod_spec(3, tiles_per_row), _mod_spec(4, tiles_per_row), _mod_spec(5, tiles_per_row),
                  resident((D_MODEL, fdim), lambda i: (0, 0)),
                  resident((D_MODEL, fdim), lambda i: (0, 0)),
                  resident((fdim, D_MODEL), lambda i: (0, 0))],
        out_specs=pl.BlockSpec((tm, D_MODEL), lambda i: (i, 0)),
        out_shape=jax.ShapeDtypeStruct((t, D_MODEL), F32),
        compiler_params=_params("arbitrary"),
        name="mix_ffn",
    )(x, y, u, d_skip.reshape(1, HALF_MIX), w_glu, b_glu.reshape(1, HALF_MIX), m2, w_out, w_out, mod,
      g.reshape(1, D_MODEL), mod, mod, mod, w_gate, w_up, w_down)


MOE_ROW_TILE = 1024
MOE_ZERO_ROWS = 256
INFO_E0, INFO_E1, INFO_G0, INFO_G1, INFO_R0, INFO_R1 = range(6)


def _stream_mod_spec(chunk, prompt_tiles, tiles_per_req, ctx_row):
    def index(i, *_):
        return (jnp.where(i < prompt_tiles, ctx_row, (i - prompt_tiles) // tiles_per_req), 0, chunk)
    return pl.BlockSpec((1, 1, D_MODEL), index)


def _mix_route_kernel(xp_ref, m1p_ref, m2p_ref, xs_ref, m1s_ref, m2s_ref, w1_ref, w2_ref, gate_ref,
                      g_ref, sh_ref, sc_ref, rwh_ref, rwl_ref, o_ref, info_ref, fields_ref, cnt_ref,
                      tri_scr, run_scr, *, prompt_tiles):
    i = pl.program_id(0)
    tm = o_ref.shape[0]
    w1 = w1_ref[...]
    w2 = w2_ref[...]

    @pl.when(i < prompt_tiles)
    def _():
        o_ref[...] = xp_ref[...] + gate_ref[0] * (_dot(m1p_ref[...], w1) + _dot(m2p_ref[...], w2))

    @pl.when(i >= prompt_tiles)
    def _():
        o_ref[...] = xs_ref[...] + gate_ref[0] * (_dot(m1s_ref[...], w1) + _dot(m2s_ref[...], w2))

    @pl.when(i == 0)
    def _():
        r = lax.broadcasted_iota(jnp.int32, (tm, tm), 0)
        c = lax.broadcasted_iota(jnp.int32, (tm, tm), 1)
        tri_scr[...] = jnp.where(c < r, 1.0, 0.0).astype(BF16)
        run_scr[...] = jnp.zeros_like(run_scr)

    h = _norm_mod(o_ref[...], g_ref[...], sh_ref[0], sc_ref[0])
    h_hi = h.astype(BF16)
    h_lo = (h - h_hi.astype(F32)).astype(BF16)
    logits = _dot(h_hi, rwh_ref[...]) + (_dot(h_hi, rwl_ref[...]) + _dot(h_lo, rwh_ref[...]))
    lane = lax.broadcasted_iota(jnp.int32, logits.shape, 1).astype(F32)
    logits = jnp.where(lane < N_EXPERTS, logits, -jnp.inf)
    m1 = jnp.max(logits, axis=-1, keepdims=True)
    i1 = jnp.min(jnp.where(logits == m1, lane, float(LANES)), axis=-1, keepdims=True)
    rest = jnp.where(lane == i1, -jnp.inf, logits)
    m2 = jnp.max(rest, axis=-1, keepdims=True)
    i2 = jnp.min(jnp.where(rest == m2, lane, float(LANES)), axis=-1, keepdims=True)
    e2 = jnp.exp(m2 - m1)
    den = 1.0 + e2
    hit = jnp.where(lane == i1, 1.0, 0.0) + jnp.where(lane == i2, 1.0, 0.0)
    before = _dot(tri_scr[...], hit.astype(BF16)) + run_scr[0:1, :]
    r1 = jnp.sum(jnp.where(lane == i1, before, 0.0), axis=-1, keepdims=True)
    r2 = jnp.sum(jnp.where(lane == i2, before, 0.0), axis=-1, keepdims=True)
    info = jnp.zeros_like(logits)
    for slot, val in ((INFO_E0, i1), (INFO_E1, i2), (INFO_G0, 1.0 / den), (INFO_G1, e2 / den),
                      (INFO_R0, r1), (INFO_R1, r2)):
        info = jnp.where(lane == float(slot), val, info)
    info_ref[...] = info
    fields_ref[...] = jnp.transpose(info)[0:SUBLANES, :]
    run_scr[...] = run_scr[...] + jnp.sum(hit, axis=0, keepdims=True)
    cnt_ref[...] = run_scr[...]


def mix_route(xp, m1p, m2p, xs, m1s, m2s, w_out, g, mod_all, ctx_row, req_tokens, router_w, tm=512):
    tp, ts = xp.shape[0], xs.shape[0]
    t = tp + ts
    pt = tp // tm
    rw = jnp.pad(router_w.astype(F32), ((0, 0), (0, LANES - N_EXPERTS)))
    rw_hi = rw.astype(BF16)
    rw_lo = (rw - rw_hi.astype(F32)).astype(BF16)

    def p_spec(width):
        return pl.BlockSpec((tm, width), lambda i: (jnp.minimum(i, pt - 1), 0))

    def s_spec(width):
        return pl.BlockSpec((tm, width), lambda i: (jnp.maximum(i - pt, 0), 0))

    def mod_spec(chunk):
        return _stream_mod_spec(chunk, pt, req_tokens // tm, ctx_row)

    rw_spec = pl.BlockSpec((D_MODEL, LANES), lambda i: (0, 0))
    return pl.pallas_call(
        functools.partial(_mix_route_kernel, prompt_tiles=pt),
        grid=(t // tm,),
        in_specs=[p_spec(D_MODEL), p_spec(HALF_MIX), p_spec(HALF_MIX),
                  s_spec(D_MODEL), s_spec(HALF_MIX), s_spec(HALF_MIX),
                  pl.BlockSpec((HALF_MIX, D_MODEL), lambda i: (0, 0)),
                  pl.BlockSpec((HALF_MIX, D_MODEL), lambda i: (1, 0)),
                  mod_spec(2), pl.BlockSpec((1, D_MODEL), lambda i: (0, 0)), mod_spec(3), mod_spec(4),
                  rw_spec, rw_spec],
        out_specs=[pl.BlockSpec((tm, D_MODEL), lambda i: (i, 0)),
                   pl.BlockSpec((tm, LANES), lambda i: (i, 0)),
                   pl.BlockSpec((SUBLANES, tm), lambda i: (0, i)),
                   pl.BlockSpec((SUBLANES, LANES), lambda i: (0, 0))],
        out_shape=[jax.ShapeDtypeStruct((t, D_MODEL), F32), jax.ShapeDtypeStruct((t, LANES), F32),
                   jax.ShapeDtypeStruct((SUBLANES, t), F32), jax.ShapeDtypeStruct((SUBLANES, LANES), F32)],
        scratch_shapes=[pltpu.VMEM((tm, tm), BF16), pltpu.VMEM((SUBLANES, LANES), F32)],
        compiler_params=_params("arbitrary"),
        name="mix_route",
    )(xp, m1p, m2p, xs, m1s, m2s, w_out, w_out, mod_all, g.reshape(1, D_MODEL), mod_all, mod_all,
      rw_hi, rw_lo)


def _dispatch_kernel(d0_ref, d1_ref, fill_ref, x_ref, g_ref, sh_ref, sc_ref, xs_hbm, h_scr, zero_scr, sem, zsem):
    i = pl.program_id(0)
    n = pl.num_programs(0)
    tm = x_ref.shape[0]
    slot = i % 2

    def row_copy(r, dst, s):
        return pltpu.make_async_copy(h_scr.at[s, pl.ds(r, 1), :], xs_hbm.at[pl.ds(dst, 1), :], sem.at[s])

    def wait_rows(s):
        for _ in range(2):
            pltpu.make_async_copy(h_scr.at[s], xs_hbm.at[pl.ds(0, tm), :], sem.at[s]).wait()

    @pl.when(i >= 2)
    def _():
        wait_rows(slot)

    h_scr[slot] = _norm_mod(x_ref[...], g_ref[...], sh_ref[0], sc_ref[0])
    base = i * tm

    def body(r8, c):
        rb = pl.multiple_of(r8 * SUBLANES, SUBLANES)
        for k in range(SUBLANES):
            row_copy(rb + k, d0_ref[base + rb + k], slot).start()
            row_copy(rb + k, d1_ref[base + rb + k], slot).start()
        return c

    lax.fori_loop(0, tm // SUBLANES, body, 0)

    @pl.when(i == n - 1)
    def _():
        zero_scr[...] = jnp.zeros_like(zero_scr)

        def zero_row(r):
            return pltpu.make_async_copy(zero_scr.at[pl.ds(0, 1), :], xs_hbm.at[pl.ds(r, 1), :], zsem)

        def zero_block(b):
            start = pl.multiple_of(b * MOE_ZERO_ROWS, MOE_ZERO_ROWS)
            return pltpu.make_async_copy(zero_scr, xs_hbm.at[pl.ds(start, MOE_ZERO_ROWS), :], zsem)

        def start_all(copy):
            def body(r, c):
                copy(r).start()
                return c
            return body

        def wait_all(copy):
            def body(r, c):
                copy(r).wait()
                return c
            return body

        def zero_group(b):
            start = pl.multiple_of(b * SUBLANES, SUBLANES)
            return pltpu.make_async_copy(zero_scr.at[pl.ds(0, SUBLANES), :],
                                         xs_hbm.at[pl.ds(start, SUBLANES), :], zsem)

        for e in range(N_EXPERTS):
            lo = fill_ref[e]
            hi = fill_ref[N_EXPERTS + e]
            lo_group = (lo + SUBLANES - 1) // SUBLANES
            lax.fori_loop(lo, lo_group * SUBLANES, start_all(zero_row), 0)
            lax.fori_loop(lo_group, hi // SUBLANES, start_all(zero_group), 0)
            lax.fori_loop(lo, lo_group * SUBLANES, wait_all(zero_row), 0)
            lax.fori_loop(lo_group, hi // SUBLANES, wait_all(zero_group), 0)
        blocks_per_tile = MOE_ROW_TILE // MOE_ZERO_ROWS
        first = fill_ref[2 * N_EXPERTS] * blocks_per_tile
        last = (xs_hbm.shape[0] // MOE_ROW_TILE) * blocks_per_tile
        lax.fori_loop(first, last, start_all(zero_block), 0)
        lax.fori_loop(first, last, wait_all(zero_block), 0)
        wait_rows(slot)

        @pl.when(n >= 2)
        def _():
            wait_rows(1 - slot)


def moe_dispatch(x, g, mod_all, ctx_row, prompt_tokens, req_tokens, dest0, dest1, fill, n_tiles, tm=512):
    t = x.shape[0]
    pt = prompt_tokens // tm
    grid_spec = pltpu.PrefetchScalarGridSpec(
        num_scalar_prefetch=3,
        grid=(t // tm,),
        in_specs=[pl.BlockSpec((tm, D_MODEL), lambda i, *_: (i, 0)),
                  pl.BlockSpec((1, D_MODEL), lambda i, *_: (0, 0)),
                  _stream_mod_spec(3, pt, req_tokens // tm, ctx_row),
                  _stream_mod_spec(4, pt, req_tokens // tm, ctx_row)],
        out_specs=pl.BlockSpec(memory_space=pl.ANY),
        scratch_shapes=[pltpu.VMEM((2, tm, D_MODEL), F32), pltpu.VMEM((MOE_ZERO_ROWS, D_MODEL), F32),
                        pltpu.SemaphoreType.DMA((2,)), pltpu.SemaphoreType.DMA(())],
    )
    return pl.pallas_call(
        _dispatch_kernel,
        grid_spec=grid_spec,
        out_shape=jax.ShapeDtypeStruct((n_tiles * MOE_ROW_TILE, D_MODEL), F32),
        compiler_params=pltpu.CompilerParams(dimension_semantics=("arbitrary",),
                                             vmem_limit_bytes=VMEM_LIMIT_BYTES,
                                             disable_bounds_checks=True),
        name="moe_dispatch",
    )(dest0, dest1, fill, x, g.reshape(1, D_MODEL), mod_all, mod_all)


def _experts_kernel(te_ref, tv_ref, x_ref, wg_ref, wu_ref, wd_ref, o_ref, h_scr, acc_scr):
    i = pl.program_id(0)
    f = pl.program_id(1)
    last_f = pl.num_programs(1) - 1

    n_valid = tv_ref[i]
    half = h_scr.shape[0] // 2

    def swiglu_rows(n_rows):
        rows = slice(0, n_rows)

        @pl.when(f == 0)
        def _():
            h_scr[rows, :] = x_ref[rows, :].astype(BF16)
            acc_scr[rows, :] = jnp.zeros((n_rows, D_MODEL), F32)

        h = h_scr[rows, :]
        a = _dot(h, wg_ref[0].astype(BF16))
        u = _dot(h, wu_ref[0].astype(BF16))
        acc_scr[rows, :] += _dot((_silu(a) * u).astype(BF16), wd_ref[0].astype(BF16))

        @pl.when(f == last_f)
        def _():
            o_ref[rows, :] = acc_scr[rows, :]

    @pl.when(n_valid > half)
    def _():
        swiglu_rows(2 * half)

    @pl.when((n_valid > 0) & (n_valid <= half))
    def _():
        swiglu_rows(half)

        @pl.when(f == last_f)
        def _():
            o_ref[half:, :] = jnp.zeros((half, D_MODEL), F32)

    @pl.when((n_valid == 0) & (f == last_f))
    def _():
        o_ref[...] = jnp.zeros_like(o_ref)


def moe_experts(xs_sorted, tile_expert, tile_valid, w_gate, w_up, w_down, tf=512):
    rows = xs_sorted.shape[0]
    fdim = w_gate.shape[2]
    n_f = fdim // tf
    tr = MOE_ROW_TILE

    def f_eff(i, f, tv):
        return jnp.where(tv[i] > 0, f, n_f - 1)

    grid_spec = pltpu.PrefetchScalarGridSpec(
        num_scalar_prefetch=2,
        grid=(rows // tr, n_f),
        in_specs=[pl.BlockSpec((tr, D_MODEL), lambda i, f, te, tv: (i, 0)),
                  pl.BlockSpec((1, D_MODEL, tf), lambda i, f, te, tv: (te[i], 0, f_eff(i, f, tv))),
                  pl.BlockSpec((1, D_MODEL, tf), lambda i, f, te, tv: (te[i], 0, f_eff(i, f, tv))),
                  pl.BlockSpec((1, tf, D_MODEL), lambda i, f, te, tv: (te[i], f_eff(i, f, tv), 0))],
        out_specs=pl.BlockSpec((tr, D_MODEL), lambda i, f, te, tv: (i, 0)),
        scratch_shapes=[pltpu.VMEM((tr, D_MODEL), BF16), pltpu.VMEM((tr, D_MODEL), F32)],
    )
    return pl.pallas_call(
        _experts_kernel,
        grid_spec=grid_spec,
        out_shape=jax.ShapeDtypeStruct((rows, D_MODEL), F32),
        compiler_params=_params("arbitrary", "arbitrary"),
        name="moe_experts",
    )(tile_expert, tile_valid, xs_sorted, w_gate, w_up, w_down)


def _combine_kernel(d0_ref, d1_ref, x_ref, info_ref, gate_ref, fg_ref, ys_hbm, op_ref, os_ref, rbuf, sem,
                    *, prompt_tiles):
    i = pl.program_id(0)
    n = pl.num_programs(0)
    tm = x_ref.shape[0]
    slot = i % 2

    def issue(tile, s):
        base = tile * tm

        def body(r8, c):
            rb = pl.multiple_of(r8 * SUBLANES, SUBLANES)
            for k in range(SUBLANES):
                pltpu.make_async_copy(ys_hbm.at[pl.ds(d0_ref[base + rb + k], 1), :],
                                      rbuf.at[s, 0, pl.ds(rb + k, 1), :], sem.at[s]).start()
                pltpu.make_async_copy(ys_hbm.at[pl.ds(d1_ref[base + rb + k], 1), :],
                                      rbuf.at[s, 1, pl.ds(rb + k, 1), :], sem.at[s]).start()
            return c

        lax.fori_loop(0, tm // SUBLANES, body, 0)

    @pl.when(i == 0)
    def _():
        issue(0, 0)

    @pl.when(i + 1 < n)
    def _():
        issue(i + 1, 1 - slot)

    for k in range(2):
        pltpu.make_async_copy(ys_hbm.at[pl.ds(0, tm), :], rbuf.at[slot, k], sem.at[slot]).wait()

    info = info_ref[...]
    moe = info[:, INFO_G0:INFO_G0 + 1] * rbuf[slot, 0] + info[:, INFO_G1:INFO_G1 + 1] * rbuf[slot, 1]
    y = x_ref[...] + gate_ref[0] * moe
    out = y * lax.rsqrt(jnp.mean(y * y, axis=-1, keepdims=True) + EPS) * fg_ref[...]

    @pl.when(i < prompt_tiles)
    def _():
        op_ref[...] = out

    @pl.when(i >= prompt_tiles)
    def _():
        os_ref[...] = out


def moe_combine(x, info, ys_sorted, dest0, dest1, mod_all, ctx_row, prompt_tokens, req_tokens, final_g, tm=512):
    t = x.shape[0]
    pt = prompt_tokens // tm
    grid_spec = pltpu.PrefetchScalarGridSpec(
        num_scalar_prefetch=2,
        grid=(t // tm,),
        in_specs=[pl.BlockSpec((tm, D_MODEL), lambda i, *_: (i, 0)),
                  pl.BlockSpec((tm, LANES), lambda i, *_: (i, 0)),
                  _stream_mod_spec(5, pt, req_tokens // tm, ctx_row),
                  pl.BlockSpec((1, D_MODEL), lambda i, *_: (0, 0)),
                  pl.BlockSpec(memory_space=pl.ANY)],
        out_specs=[pl.BlockSpec((tm, D_MODEL), lambda i, *_: (jnp.minimum(i, pt - 1), 0)),
                   pl.BlockSpec((tm, D_MODEL), lambda i, *_: (jnp.maximum(i - pt, 0), 0))],
        scratch_shapes=[pltpu.VMEM((2, 2, tm, D_MODEL), F32), pltpu.SemaphoreType.DMA((2,))],
    )
    return pl.pallas_call(
        functools.partial(_combine_kernel, prompt_tiles=pt),
        grid_spec=grid_spec,
        out_shape=[jax.ShapeDtypeStruct((prompt_tokens, D_MODEL), F32),
                   jax.ShapeDtypeStruct((t - prompt_tokens, D_MODEL), F32)],
        compiler_params=pltpu.CompilerParams(dimension_semantics=("arbitrary",),
                                             vmem_limit_bytes=VMEM_LIMIT_BYTES,
                                             disable_bounds_checks=True),
        name="moe_combine",
    )(dest0, dest1, x, info, mod_all, final_g.reshape(1, D_MODEL), ys_sorted)


def moe_layout(counts, fields, n_tiles):
    nt_e = (counts + MOE_ROW_TILE - 1) // MOE_ROW_TILE
    ends = jnp.cumsum(nt_e)
    total = ends[-1]
    offset = (ends - nt_e) * MOE_ROW_TILE
    experts = jnp.arange(N_EXPERTS, dtype=jnp.int32)

    def dest(e_lane, r_lane):
        e = fields[e_lane].astype(jnp.int32)
        off = jnp.sum(jnp.where(e[:, None] == experts[None, :], offset[None, :], 0), axis=1)
        return (off + fields[r_lane].astype(jnp.int32)).astype(jnp.int32)

    fill = jnp.concatenate([offset + counts, ends * MOE_ROW_TILE, total[None]]).astype(jnp.int32)
    ids = jnp.arange(n_tiles, dtype=jnp.int32)
    ids_c = jnp.minimum(ids, total - 1)
    te = jnp.sum((ids_c[:, None] >= ends[None, :]).astype(jnp.int32), axis=1)
    mine = te[:, None] == experts[None, :]
    first_tile = jnp.sum(jnp.where(mine, (ends - nt_e)[None, :], 0), axis=1)
    count = jnp.sum(jnp.where(mine, counts[None, :], 0), axis=1)
    rows_left = jnp.clip(count - (ids - first_tile) * MOE_ROW_TILE, 0, MOE_ROW_TILE)
    tile_rows = jnp.where(ids < total, rows_left, 0)
    return (dest(INFO_E0, INFO_R0), dest(INFO_E1, INFO_R1), fill, te.astype(jnp.int32),
            tile_rows.astype(jnp.int32))


def moe_final(x, info, fields, cnt, g, mod_all, ctx_row, prompt_tokens, req_tokens, final_g, w_gate, w_up,
              w_down):
    t = x.shape[0]
    counts = cnt[0, :N_EXPERTS].astype(jnp.int32)
    n_tiles = (2 * t) // MOE_ROW_TILE + N_EXPERTS
    dest0, dest1, fill, te, tv = moe_layout(counts, fields, n_tiles)
    xs_sorted = moe_dispatch(x, g, mod_all, ctx_row, prompt_tokens, req_tokens, dest0, dest1, fill, n_tiles)
    ys_sorted = moe_experts(xs_sorted, te, tv, w_gate, w_up, w_down)
    return moe_combine(x, info, ys_sorted, dest0, dest1, mod_all, ctx_row, prompt_tokens, req_tokens, final_g)


def kernel(x_prompt, x_sample, state_s5, cache_na_k, cache_na_v, cache_diff_k, cache_diff_v, c, c_ctx, w_mod, b_mod, norm_mix_g, norm_ffn_g, final_norm_g, w_in_e, w_out_e, s5_lam_re, s5_lam_im, s5_log_dt, s5_b_re, s5_b_im, s5_c_re, s5_c_im, s5_d, s5_w_glu, s5_b_glu, na_rpb, ffn_w_gate, ffn_w_up, ffn_w_down, w_in_o, w_out_o, diff_lam_q1, diff_lam_k1, diff_lam_q2, diff_lam_k2, diff_subln_g, conv_w, conv_b, conv_ln_g, conv_ln_b, router_w, moe_w_gate, moe_w_up, moe_w_down):
    bp, lp, d = x_prompt.shape
    bs, ls, _ = x_sample.shape
    tm = 1024
    xp = x_prompt.reshape(bp * lp, d)
    xs = x_sample.reshape(bs * ls, d)
    rows_p = (bp * lp) // tm
    rows_s = ls // tm

    cond8 = jnp.concatenate([c, c_ctx[None, :], jnp.zeros((SUBLANES - bs - 1, d), F32)], axis=0)
    mod = adaln_all(cond8, w_mod, b_mod)
    mod_s = mod[:, 0:bs]
    mod_p = mod[:, bs:bs + 1]

    def tiles(rows, tile):
        return rows * tm // tile

    bmat, cmat, lam8 = s5_params(s5_lam_re[0], s5_lam_im[0], s5_log_dt[0], s5_b_re[0], s5_b_im[0],
                                 s5_c_re[0], s5_c_im[0])
    bias = na_bias_blocks(na_rpb[0])
    n_e = w_in_e.shape[-1]

    tiles_p = tiles(rows_p, 512)
    tiles_s = tiles(rows_s, 512)
    proj_p = in_proj(xp, norm_mix_g[0], mod_p[0], tiles_p, w_in_e[0])
    u_s, qkv_s = in_proj(xs, norm_mix_g[0], mod_s[0], tiles_s, w_in_e[0], n_f32=HALF_MIX)

    y_p, st_p = s5_scan(proj_p.reshape(bp, lp, n_e), bmat, cmat, lam8, None, 1)
    chunks = SUBLANES // bs
    h0 = state_s5[:, 0].reshape(bs, 2, 2, S5_GROUPS * S5_STATE)
    y_s, _ = s5_scan(u_s.reshape(bs * chunks, ls // chunks, HALF_MIX), bmat, cmat, lam8, h0, chunks)
    nao_p, na_k, na_v = na_ctx(proj_p.reshape(bp, lp, n_e))

    def heads_to_lanes(cache):
        return cache.transpose(0, 2, 1, 3).reshape(bs, cache.shape[2], HALF_MIX).astype(BF16)

    nao_s = na_lat(qkv_s.reshape(bs, ls, n_e - HALF_MIX), heads_to_lanes(cache_na_k[:, 0]),
                   heads_to_lanes(cache_na_v[:, 0]), bias)

    glu_w = (s5_d[0], s5_w_glu[0].astype(BF16), s5_b_glu[0])
    ffn_w = (w_out_e[0].astype(BF16), norm_ffn_g[0])
    ffn_w3 = (ffn_w_gate[0].astype(BF16), ffn_w_up[0].astype(BF16), ffn_w_down[0].astype(BF16))
    xp = mix_ffn(xp, y_p.reshape(bp * lp, HALF_MIX), proj_p, *glu_w, nao_p.reshape(bp * lp, HALF_MIX),
                 *ffn_w, mod_p[0], tiles_p, *ffn_w3)
    xs = mix_ffn(xs, y_s.reshape(bs * ls, HALF_MIX), u_s, *glu_w, nao_s.reshape(bs * ls, HALF_MIX),
                 *ffn_w, mod_s[0], tiles_s, *ffn_w3)

    lam_init = 0.8 - 0.6 * math.exp(-0.3 * 1)
    lam = (jnp.exp(jnp.sum(diff_lam_q1[0].astype(F32) * diff_lam_k1[0].astype(F32)))
           - jnp.exp(jnp.sum(diff_lam_q2[0].astype(F32) * diff_lam_k2[0].astype(F32)))
           + lam_init)
    cos, sin = rope_tables(ls)
    n_o = w_in_o.shape[-1]

    proj_p = in_proj(xp, norm_mix_g[1], mod_p[1], tiles_p, w_in_o[0])
    proj_s = in_proj(xs, norm_mix_g[1], mod_s[1], tiles_s, w_in_o[0])

    do_p, diff_k, diff_v = diff_ctx(proj_p.reshape(bp, lp, n_o), lam, diff_subln_g[0], lam_init)
    do_s = diff_lat(proj_s.reshape(bs, ls, n_o), cache_diff_k[:, 0:1], cache_diff_v[:, 0:1], cos, sin,
                    lam, diff_subln_g[0], lam_init)
    co_p = conformer_conv(proj_p.reshape(bp, lp, n_o), conv_w[0], conv_b[0], conv_ln_g[0], conv_ln_b[0], lp)
    co_s = conformer_conv(proj_s.reshape(bs, ls, n_o), conv_w[0], conv_b[0], conv_ln_g[0], conv_ln_b[0], 512)

    mod_all = mod[1]
    x_all, info, fields, cnt = mix_route(
        xp, do_p.reshape(bp * lp, HALF_MIX), co_p.reshape(bp * lp, HALF_MIX),
        xs, do_s.reshape(bs * ls, HALF_MIX), co_s.reshape(bs * ls, HALF_MIX),
        w_out_o[0].astype(BF16), norm_ffn_g[1], mod_all, bs, ls, router_w[0])
    yp, ys = moe_final(x_all, info, fields, cnt, norm_ffn_g[1], mod_all, bs, bp * lp, ls, final_norm_g,
                       moe_w_gate[0], moe_w_up[0], moe_w_down[0])

    new_state = st_p.reshape(bp, 1, 2, 2, S5_GROUPS, S5_STATE)
    return (yp.reshape(bp, lp, d), ys.reshape(bs, ls, d), new_state, na_k, na_v, diff_k, diff_v)
```

```python
import functools
import math

import jax
import jax.numpy as jnp
import numpy as np
from jax import lax
from jax.experimental import pallas as pl
from jax.experimental.pallas import tpu as pltpu

D_MODEL = 1024
DEPTH = 2
GRID_W = 64
HALF_MIX = 512
S5_GROUP_CH = 16
S5_GROUPS = 32
S5_STATE = 64
NA_HEAD_DIM = 64
NA_HEADS = 8
NA_WIN_R = 8
NA_WIN_C = 16
DIFF_D = 64
DIFF_HEAD_DIM = 128
DIFF_HEADS = 4
ROPE_BASE = 10000.0
CONV_WIDTH = 31
N_EXPERTS = 8
EPS = 1e-6

F32 = jnp.float32
BF16 = jnp.bfloat16
NEG_BIG = -1e30

VMEM_LIMIT_BYTES = 56 * 1024 * 1024
LANES = 128
SUBLANES = 8

S5_COL_GROUPS = 8
S5_COL_CH = S5_COL_GROUPS * S5_GROUP_CH
S5_COL_STATE = S5_COL_GROUPS * S5_STATE
S5_N_COL = S5_GROUPS // S5_COL_GROUPS
S5_TIME_BLOCK = 256


def _params(*sem):
    return pltpu.CompilerParams(dimension_semantics=sem, vmem_limit_bytes=VMEM_LIMIT_BYTES)


def _dot(a, b):
    return jnp.dot(a, b, preferred_element_type=F32)


def _dot_nt(a, b):
    return lax.dot_general(a, b, (((1,), (1,)), ((), ())), preferred_element_type=F32)


def _silu(x):
    return x * jax.nn.sigmoid(x)


def _norm_mod(x, g, shift, scale):
    y = x * lax.rsqrt(jnp.mean(x * x, axis=-1, keepdims=True) + EPS) * g
    return y * (1.0 + scale) + shift


def _mod_kernel(cond_ref, w_ref, b_ref, o_ref):
    s = _silu(cond_ref[...])
    o_ref[0, :, 0, :] = jnp.dot(s, w_ref[0], precision=lax.Precision.HIGHEST,
                                preferred_element_type=F32) + b_ref[0]


def adaln_all(cond8, w_mod, b_mod):
    tn = 1536
    n = w_mod.shape[-1]
    return pl.pallas_call(
        _mod_kernel,
        grid=(DEPTH, n // tn),
        in_specs=[pl.BlockSpec((SUBLANES, D_MODEL), lambda l, j: (0, 0)),
                  pl.BlockSpec((1, D_MODEL, tn), lambda l, j: (l, 0, j)),
                  pl.BlockSpec((1, 1, tn), lambda l, j: (l, 0, j))],
        out_specs=pl.BlockSpec((1, SUBLANES, 1, tn), lambda l, j: (l, 0, 0, j)),
        out_shape=jax.ShapeDtypeStruct((DEPTH, SUBLANES, 1, n), F32),
        compiler_params=_params("arbitrary", "arbitrary"),
        name="adaln_mod",
    )(cond8, w_mod, b_mod.reshape(DEPTH, 1, n))


def _mod_spec(chunk, tiles_per_row):
    return pl.BlockSpec((1, 1, D_MODEL), lambda i, *_: (i // tiles_per_row, 0, chunk))


def _in_proj_kernel(x_ref, g_ref, sh_ref, sc_ref, w_ref, *refs, n_f32):
    *outs, wb_scr = refs

    @pl.when(pl.program_id(0) == 0)
    def _():
        wb_scr[...] = w_ref[...].astype(BF16)

    h = _norm_mod(x_ref[...], g_ref[...], sh_ref[0], sc_ref[0]).astype(BF16)
    y = _dot(h, wb_scr[...])
    if len(outs) == 1:
        outs[0][...] = y
    else:
        outs[0][...] = y[:, :n_f32]
        outs[1][...] = y[:, n_f32:].astype(BF16)


def in_proj(x, g, mod, tiles_per_row, w, n_f32=None, tm=512):
    t = x.shape[0]
    n = w.shape[1]
    if n_f32 is None:
        out_specs = pl.BlockSpec((tm, n), lambda i: (i, 0))
        out_shape = jax.ShapeDtypeStruct((t, n), F32)
    else:
        out_specs = [pl.BlockSpec((tm, n_f32), lambda i: (i, 0)), pl.BlockSpec((tm, n - n_f32), lambda i: (i, 0))]
        out_shape = [jax.ShapeDtypeStruct((t, n_f32), F32), jax.ShapeDtypeStruct((t, n - n_f32), BF16)]
    return pl.pallas_call(
        functools.partial(_in_proj_kernel, n_f32=n_f32),
        grid=(t // tm,),
        in_specs=[pl.BlockSpec((tm, D_MODEL), lambda i: (i, 0)),
                  pl.BlockSpec((1, D_MODEL), lambda i: (0, 0)),
                  _mod_spec(0, tiles_per_row),
                  _mod_spec(1, tiles_per_row),
                  pl.BlockSpec((D_MODEL, n), lambda i: (0, 0), pipeline_mode=pl.Buffered(1))],
        out_specs=out_specs,
        out_shape=out_shape,
        scratch_shapes=[pltpu.VMEM((D_MODEL, n), BF16)],
        compiler_params=_params("arbitrary"),
        name="in_proj",
    )(x, g.reshape(1, D_MODEL), mod, mod, w)


def _s5_scan_kernel(*refs, seq, chunks, has_init):
    if has_init:
        u_ref, bm_ref, cm_ref, lam_ref, h0_ref, y_ref, st_ref, bu_scr, hb_scr, ytm_scr = refs
    else:
        u_ref, bm_ref, cm_ref, lam_ref, y_ref, st_ref, bu_scr, hb_scr, ytm_scr = refs
        h0_ref = None
    tb = S5_TIME_BLOCK
    n_tb = seq // tb
    ns = S5_COL_STATE
    row = lax.broadcasted_iota(jnp.int32, (SUBLANES, ns), 0)
    piece = row % chunks

    for d in range(2):
        lam = lam_ref[d, 0]
        lr, li = lam[:, :ns], lam[:, ns:]
        bm = bm_ref[d, 0]
        cm = cm_ref[d, 0]
        blocks = list(range(n_tb)) if d == 0 else list(range(n_tb - 1, -1, -1))

        def load_bu(k):
            ub = u_ref[:, k * tb:(k + 1) * tb, :]
            utm = jnp.swapaxes(ub, 0, 1).reshape(tb * SUBLANES, S5_COL_CH).astype(BF16)
            bu_scr[...] = _dot(utm, bm).reshape(tb, SUBLANES, 2 * ns)

        def scan_block(h, store):
            def advance(t, hr, hi):
                b = bu_scr[t]
                return lr * hr - li * hi + b[:, :ns], lr * hi + li * hr + b[:, ns:]

            def step(s, carry):
                t_a = (tb - 1 - 2 * s) if d == 1 else 2 * s
                t_b = t_a - 1 if d == 1 else t_a + 1
                ar, ai = advance(t_a, *carry)
                br, bi = advance(t_b, ar, ai)
                if store:
                    first = jnp.concatenate([br, bi] if d == 1 else [ar, ai], axis=1)
                    second = jnp.concatenate([ar, ai] if d == 1 else [br, bi], axis=1)
                    t_lo = t_b if d == 1 else t_a
                    rows = pl.ds(pl.multiple_of(t_lo * SUBLANES, 2 * SUBLANES), 2 * SUBLANES)
                    hb_scr[rows, :] = jnp.concatenate([first, second], axis=0).astype(BF16)
                return br, bi
            return lax.fori_loop(0, tb // 2, step, h)

        zero = jnp.zeros((SUBLANES, ns), F32)
        if chunks > 1:
            h = (zero, zero)
            for k in blocks:
                load_bu(k)
                h = scan_block(h, False)
            fr, fi = h
            pr, pi = lr, li
            for _ in range(int(math.log2(seq))):
                pr, pi = pr * pr - pi * pi, 2.0 * pr * pi
            edge = 0 if d == 0 else chunks - 1
            shift = 1 if d == 0 else SUBLANES - 1
            if has_init:
                h0r = h0_ref[:, d, 0, :]
                h0i = h0_ref[:, d, 1, :]
                seq_of_row = row // chunks
                er, ei = zero, zero
                for b in range(SUBLANES // chunks):
                    er = jnp.where(seq_of_row == b, h0r[b:b + 1, :], er)
                    ei = jnp.where(seq_of_row == b, h0i[b:b + 1, :], ei)
            else:
                er, ei = zero, zero
            is_edge = piece == edge
            cr = jnp.where(is_edge, er, zero)
            ci = jnp.where(is_edge, ei, zero)
            for _ in range(chunks - 1):
                tr = fr + pr * cr - pi * ci
                ti = fi + pr * ci + pi * cr
                cr = jnp.where(is_edge, er, pltpu.roll(tr, shift, 0))
                ci = jnp.where(is_edge, ei, pltpu.roll(ti, shift, 0))
            h = (cr, ci)
        else:
            if has_init:
                h = (h0_ref[:, d, 0, :], h0_ref[:, d, 1, :])
            else:
                h = (zero, zero)

        for k in blocks:
            load_bu(k)
            h = scan_block(h, True)
            yb = _dot(hb_scr[...], cm).reshape(tb, SUBLANES, S5_COL_CH)
            if d == 0:
                ytm_scr[k * tb:(k + 1) * tb] = yb
            else:
                ytm_scr[k * tb:(k + 1) * tb] += yb
        st_ref[:, d, 0, :] = h[0]
        st_ref[:, d, 1, :] = h[1]

    y_ref[...] = jnp.swapaxes(ytm_scr[...], 0, 1)


def s5_scan(proj3, bmat, cmat, lam8, h0, chunks):
    rows, seq, _ = proj3.shape
    ns = S5_COL_STATE
    has_init = h0 is not None
    in_specs = [pl.BlockSpec((SUBLANES, seq, S5_COL_CH), lambda i, c: (i, 0, c)),
                pl.BlockSpec((2, 1, S5_COL_CH, 2 * ns), lambda i, c: (0, c, 0, 0)),
                pl.BlockSpec((2, 1, 2 * ns, S5_COL_CH), lambda i, c: (0, c, 0, 0)),
                pl.BlockSpec((2, 1, SUBLANES, 2 * ns), lambda i, c: (0, c, 0, 0))]
    args = [proj3, bmat, cmat, lam8]
    if has_init:
        nb = h0.shape[0]
        in_specs.append(pl.BlockSpec((nb, 2, 2, ns), lambda i, c: (0, 0, 0, c)))
        args.append(h0)
    y, st = pl.pallas_call(
        functools.partial(_s5_scan_kernel, seq=seq, chunks=chunks, has_init=has_init),
        grid=(rows // SUBLANES, S5_N_COL),
        in_specs=in_specs,
        out_specs=[pl.BlockSpec((SUBLANES, seq, S5_COL_CH), lambda i, c: (i, 0, c)),
                   pl.BlockSpec((SUBLANES, 2, 2, ns), lambda i, c: (i, 0, 0, c))],
        out_shape=[jax.ShapeDtypeStruct((rows, seq, HALF_MIX), F32),
                   jax.ShapeDtypeStruct((rows, 2, 2, S5_GROUPS * S5_STATE), F32)],
        scratch_shapes=[pltpu.VMEM((S5_TIME_BLOCK, SUBLANES, 2 * ns), F32),
                        pltpu.VMEM((S5_TIME_BLOCK * SUBLANES, 2 * ns), BF16),
                        pltpu.VMEM((seq, SUBLANES, S5_COL_CH), F32)],
        compiler_params=_params("arbitrary", "arbitrary"),
        name="s5_scan",
    )(*args)
    return y, st


def s5_params(lam_re, lam_im, log_dt, b_re, b_im, c_re, c_im):
    lr = lam_re.astype(F32)
    li = lam_im.astype(F32)
    dt = jnp.exp(log_dt.astype(F32))[..., None]
    mag = jnp.exp(lr * dt)
    bar_re = mag * jnp.cos(li * dt)
    bar_im = mag * jnp.sin(li * dt)
    den = lr * lr + li * li
    q_re = ((bar_re - 1.0) * lr + bar_im * li) / den
    q_im = (bar_im * lr - (bar_re - 1.0) * li) / den
    br = b_re.astype(F32)
    bi = b_im.astype(F32)
    b_bar_re = q_re[..., None] * br - q_im[..., None] * bi
    b_bar_im = q_re[..., None] * bi + q_im[..., None] * br
    eye = jnp.eye(S5_COL_GROUPS, dtype=F32)

    def block_diag_b(m):
        m = m.reshape(2, S5_N_COL, S5_COL_GROUPS, S5_STATE, S5_GROUP_CH)
        bd = jnp.einsum('dngpc,gh->dngchp', m, eye)
        return bd.reshape(2, S5_N_COL, S5_COL_CH, S5_COL_STATE)

    def block_diag_c(m):
        m = m.reshape(2, S5_N_COL, S5_COL_GROUPS, S5_GROUP_CH, S5_STATE)
        bd = jnp.einsum('dngcp,gh->dngphc', m, eye)
        return bd.reshape(2, S5_N_COL, S5_COL_STATE, S5_COL_CH)

    bmat = jnp.concatenate([block_diag_b(b_bar_re), block_diag_b(b_bar_im)], axis=-1).astype(BF16)
    cmat = jnp.concatenate([block_diag_c(c_re.astype(F32)), block_diag_c(-c_im.astype(F32))],
                           axis=-2).astype(BF16)
    lam_cat = jnp.concatenate([bar_re.reshape(2, S5_N_COL, S5_COL_STATE),
                               bar_im.reshape(2, S5_N_COL, S5_COL_STATE)], axis=-1)
    lam8 = jnp.broadcast_to(lam_cat[:, :, None, :], (2, S5_N_COL, SUBLANES, 2 * S5_COL_STATE))
    return bmat, cmat, lam8


def _na_ctx_kernel(q_ref, k_ref, v_ref, o_ref, ko_ref, vo_ref):
    seq = q_ref.shape[1]
    lane = lax.broadcasted_iota(jnp.int32, (seq, LANES), 1)
    low = lane < NA_HEAD_DIM
    outs = []
    for pr in range(NA_HEADS // 2):
        cols = slice(pr * LANES, (pr + 1) * LANES)
        qp = q_ref[0, :, cols] * (NA_HEAD_DIM ** -0.5)
        kp = k_ref[0, :, cols]
        vp = v_ref[0, :, cols]
        for half in range(2):
            sl = slice(half * NA_HEAD_DIM, (half + 1) * NA_HEAD_DIM)
            ko_ref[0, 0, 2 * pr + half] = kp[:, sl]
            vo_ref[0, 0, 2 * pr + half] = vp[:, sl]
        kb = kp.astype(BF16)
        vb = vp.astype(BF16)
        o_pair = None
        for half in range(2):
            qm = jnp.where(low if half == 0 else jnp.logical_not(low), qp, 0.0).astype(BF16)
            s = _dot_nt(qm, kb)
            p = jnp.exp(s - jnp.max(s, axis=-1, keepdims=True))
            o = _dot(p.astype(BF16), vb) * (1.0 / jnp.sum(p, axis=-1, keepdims=True))
            o_pair = o if half == 0 else jnp.where(low, o_pair, o)
        outs.append(o_pair)
    o_ref[0] = jnp.concatenate(outs, axis=-1).astype(o_ref.dtype)


def na_ctx(proj3):
    b, seq, _ = proj3.shape
    cache_shape = jax.ShapeDtypeStruct((b, 1, NA_HEADS, seq, NA_HEAD_DIM), F32)
    cache_spec = pl.BlockSpec((1, 1, NA_HEADS, seq, NA_HEAD_DIM), lambda i: (i, 0, 0, 0, 0))
    return pl.pallas_call(
        _na_ctx_kernel,
        grid=(b,),
        in_specs=[pl.BlockSpec((1, seq, HALF_MIX), lambda i: (i, 0, 1)),
                  pl.BlockSpec((1, seq, HALF_MIX), lambda i: (i, 0, 2)),
                  pl.BlockSpec((1, seq, HALF_MIX), lambda i: (i, 0, 3))],
        out_specs=[pl.BlockSpec((1, seq, HALF_MIX), lambda i: (i, 0, 0)), cache_spec, cache_spec],
        out_shape=[jax.ShapeDtypeStruct((b, seq, HALF_MIX), BF16), cache_shape, cache_shape],
        compiler_params=_params("arbitrary"),
        name="na_ctx",
    )(proj3, proj3, proj3)


NA_Q_ROWS = 4
NA_KEY_ROWS = 12


def na_bias_blocks(rpb):
    qcol = np.arange(GRID_W)
    cc = np.arange(GRID_W)
    cs = np.clip(qcol - NA_WIN_C // 2, 0, GRID_W - NA_WIN_C)
    valid = (cc[None, :] >= cs[:, None]) & (cc[None, :] < cs[:, None] + NA_WIN_C)
    coff = cc[None, :] - qcol[:, None] + (NA_WIN_C - 1)
    n_col = 2 * NA_WIN_C - 1
    sel = ((coff[None] == np.arange(n_col)[:, None, None]) & valid[None]).astype(np.float32)
    sel = sel.reshape(n_col, GRID_W * GRID_W)
    mask = np.where(valid, 0.0, NEG_BIG).astype(np.float32).reshape(1, GRID_W * GRID_W)
    n_row = 2 * NA_WIN_R - 1
    t1 = jnp.dot(rpb.astype(F32).reshape(NA_HEADS * n_row, n_col), jnp.asarray(sel),
                 precision=lax.Precision.HIGHEST) + jnp.asarray(mask)
    t1 = t1.reshape(NA_HEADS, n_row, GRID_W, GRID_W)
    return pl.pallas_call(
        _na_bias_kernel,
        grid=(NA_HEADS,),
        in_specs=[pl.BlockSpec((1, n_row, GRID_W, GRID_W), lambda h: (h, 0, 0, 0))],
        out_specs=pl.BlockSpec((3, 1, NA_Q_ROWS * GRID_W, NA_KEY_ROWS * GRID_W), lambda h: (0, h, 0, 0)),
        out_shape=jax.ShapeDtypeStruct((3, NA_HEADS, NA_Q_ROWS * GRID_W, NA_KEY_ROWS * GRID_W), F32),
        compiler_params=_params("arbitrary"),
        name="na_bias",
    )(t1)


def _na_bias_kernel(t1_ref, o_ref):
    variants = ((lambda ri: 0, NA_WIN_R - 1), (lambda ri: ri, NA_WIN_R // 2 - 1),
                (lambda ri: NA_KEY_ROWS - NA_WIN_R, -1))
    outside = jnp.full((GRID_W, GRID_W), NEG_BIG, F32)
    for v, (lo_of, shift) in enumerate(variants):
        for ri in range(NA_Q_ROWS):
            for wr in range(NA_KEY_ROWS):
                inside = lo_of(ri) <= wr < lo_of(ri) + NA_WIN_R
                o_ref[v, 0, ri * GRID_W:(ri + 1) * GRID_W, wr * GRID_W:(wr + 1) * GRID_W] = (
                    t1_ref[0, wr - ri + shift] if inside else outside)


def _na_lat_kernel(q_ref, k_ref, v_ref, kc_ref, vc_ref, bias_ref, o_ref):
    qb = pl.program_id(1)
    rows = k_ref.shape[1] // GRID_W
    nk = NA_KEY_ROWS * GRID_W
    first_row = jnp.clip(qb * NA_Q_ROWS - NA_WIN_R // 2, 0, rows - NA_KEY_ROWS)
    start = pl.multiple_of(first_row * GRID_W, GRID_W)
    tq = q_ref.shape[1]
    lane = lax.broadcasted_iota(jnp.int32, (tq, LANES), 1)
    low = lane < NA_HEAD_DIM
    outs = []
    for pr in range(NA_HEADS // 2):
        cols = slice(pr * LANES, (pr + 1) * LANES)
        qp = q_ref[0, :, cols].astype(F32) * (NA_HEAD_DIM ** -0.5)
        kw = k_ref[0, pl.ds(start, nk), cols]
        vw = v_ref[0, pl.ds(start, nk), cols]
        kc = kc_ref[0, :, cols]
        vc = vc_ref[0, :, cols]
        o_pair = None
        for half in range(2):
            qm = jnp.where(low if half == 0 else jnp.logical_not(low), qp, 0.0).astype(BF16)
            s_loc = _dot_nt(qm, kw) + bias_ref[0, 2 * pr + half]
            s_ctx = _dot_nt(qm, kc)
            m = jnp.maximum(jnp.max(s_loc, axis=-1, keepdims=True), jnp.max(s_ctx, axis=-1, keepdims=True))
            p_loc = jnp.exp(s_loc - m)
            p_ctx = jnp.exp(s_ctx - m)
            inv = 1.0 / (jnp.sum(p_loc, axis=-1, keepdims=True) + jnp.sum(p_ctx, axis=-1, keepdims=True))
            o = (_dot(p_loc.astype(BF16), vw) + _dot(p_ctx.astype(BF16), vc)) * inv
            o_pair = o if half == 0 else jnp.where(low, o_pair, o)
        outs.append(o_pair)
    o_ref[0] = jnp.concatenate(outs, axis=-1).astype(o_ref.dtype)


def na_lat(qkv3, k_ctx, v_ctx, bias):
    b, seq, _ = qkv3.shape
    tq = NA_Q_ROWS * GRID_W
    n_q = seq // tq
    lc = k_ctx.shape[1]
    ctx_spec = pl.BlockSpec((1, lc, HALF_MIX), lambda i, r: (i, 0, 0))
    return pl.pallas_call(
        _na_lat_kernel,
        grid=(b, n_q),
        in_specs=[pl.BlockSpec((1, tq, HALF_MIX), lambda i, r: (i, r, 0)),
                  pl.BlockSpec((1, seq, HALF_MIX), lambda i, r: (i, 0, 1)),
                  pl.BlockSpec((1, seq, HALF_MIX), lambda i, r: (i, 0, 2)),
                  ctx_spec, ctx_spec,
                  pl.BlockSpec((1, NA_HEADS, tq, NA_KEY_ROWS * GRID_W),
                               lambda i, r: (jnp.where(r == 0, 0, jnp.where(r == n_q - 1, 2, 1)), 0, 0, 0))],
        out_specs=pl.BlockSpec((1, tq, HALF_MIX), lambda i, r: (i, r, 0)),
        out_shape=jax.ShapeDtypeStruct((b, seq, HALF_MIX), BF16),
        compiler_params=_params("arbitrary", "arbitrary"),
        name="na_lat",
    )(qkv3, qkv3, qkv3, k_ctx, v_ctx, bias)


def _softmax_pair_diff(s1, s2, lam):
    p1 = jnp.exp(s1 - jnp.max(s1, axis=-1, keepdims=True))
    p2 = jnp.exp(s2 - jnp.max(s2, axis=-1, keepdims=True))
    inv1 = 1.0 / jnp.sum(p1, axis=-1, keepdims=True)
    inv2 = lam / jnp.sum(p2, axis=-1, keepdims=True)
    return p1 * inv1 - p2 * inv2


def _sub_ln(o, g, lam_init):
    return o * lax.rsqrt(jnp.mean(o * o, axis=-1, keepdims=True) + EPS) * g * (1.0 - lam_init)


def _diff_ctx_kernel(lam_ref, q_ref, k_ref, v_ref, g_ref, o_ref, ko_ref, vo_ref, *, lam_init):
    scale = DIFF_D ** -0.5
    lam = lam_ref[0, 0]
    lane = lax.broadcasted_iota(jnp.int32, (q_ref.shape[1], DIFF_HEAD_DIM), 1)
    first = lane < DIFF_D
    for h in range(DIFF_HEADS):
        sl = slice(h * DIFF_HEAD_DIM, (h + 1) * DIFF_HEAD_DIM)
        qh = q_ref[0, :, sl]
        kh = k_ref[0, :, sl]
        vh = v_ref[0, :, sl]
        ko_ref[0, 0, h] = kh
        vo_ref[0, 0, h] = vh
        kb = kh.astype(BF16)
        s1 = _dot_nt(jnp.where(first, qh, 0.0).astype(BF16), kb) * scale
        s2 = _dot_nt(jnp.where(first, 0.0, qh).astype(BF16), kb) * scale
        a = _softmax_pair_diff(s1, s2, lam)
        o = _dot(a.astype(BF16), vh.astype(BF16))
        o_ref[0, :, sl] = _sub_ln(o, g_ref[...], lam_init).astype(o_ref.dtype)


def diff_ctx(proj3, lam, subln_g, lam_init):
    b, seq, _ = proj3.shape
    cache_shape = jax.ShapeDtypeStruct((b, 1, DIFF_HEADS, seq, DIFF_HEAD_DIM), F32)
    cache_spec = pl.BlockSpec((1, 1, DIFF_HEADS, seq, DIFF_HEAD_DIM), lambda i: (i, 0, 0, 0, 0))
    return pl.pallas_call(
        functools.partial(_diff_ctx_kernel, lam_init=lam_init),
        grid=(b,),
        in_specs=[pl.BlockSpec(memory_space=pltpu.SMEM),
                  pl.BlockSpec((1, seq, HALF_MIX), lambda i: (i, 0, 0)),
                  pl.BlockSpec((1, seq, HALF_MIX), lambda i: (i, 0, 1)),
                  pl.BlockSpec((1, seq, HALF_MIX), lambda i: (i, 0, 2)),
                  pl.BlockSpec((1, DIFF_HEAD_DIM), lambda i: (0, 0))],
        out_specs=[pl.BlockSpec((1, seq, HALF_MIX), lambda i: (i, 0, 0)), cache_spec, cache_spec],
        out_shape=[jax.ShapeDtypeStruct((b, seq, HALF_MIX), BF16), cache_shape, cache_shape],
        compiler_params=_params("arbitrary"),
        name="diff_ctx",
    )(lam.reshape(1, 1), proj3, proj3, proj3, subln_g.reshape(1, DIFF_HEAD_DIM))


def rope_tables(seq):
    t = np.arange(seq)
    row = (t // GRID_W).astype(np.float32)
    col = (t % GRID_W).astype(np.float32)
    n_freq = DIFF_D // 4
    inv = np.float32(ROPE_BASE) ** (-np.arange(n_freq, dtype=np.float32) / np.float32(n_freq))
    ang = np.concatenate([row[:, None] * inv, col[:, None] * inv], axis=-1)
    cos = np.repeat(np.cos(ang), 2, axis=-1)
    sin = np.repeat(np.sin(ang), 2, axis=-1)
    sign = np.where(np.arange(DIFF_D) % 2 == 0, -1.0, 1.0).astype(np.float32)
    sin = sin * sign
    return (jnp.asarray(np.tile(cos, (1, 2)).astype(np.float32)),
            jnp.asarray(np.tile(sin, (1, 2)).astype(np.float32)))


def _rope(x, cos, sin_signed):
    lane = lax.broadcasted_iota(jnp.int32, x.shape, 1)
    nxt = pltpu.roll(x, x.shape[1] - 1, 1)
    prv = pltpu.roll(x, 1, 1)
    partner = jnp.where(lane % 2 == 0, nxt, prv)
    return x * cos + partner * sin_signed


def _diff_lat_kernel(lam_ref, q_ref, k_ref, v_ref, kc_ref, vc_ref, cq_ref, sq_ref, ck_ref, sk_ref,
                     g_ref, o_ref, k_all, v_all, *, lam_init):
    seq = k_ref.shape[1]

    @pl.when(pl.program_id(2) == 0)
    def _():
        k_all[0:seq, :] = _rope(k_ref[0], ck_ref[...], sk_ref[...]).astype(BF16)
        k_all[seq:, :] = kc_ref[0, 0, 0].astype(BF16)
        v_all[0:seq, :] = v_ref[0].astype(BF16)
        v_all[seq:, :] = vc_ref[0, 0, 0].astype(BF16)

    lam = lam_ref[0, 0]
    q = _rope(q_ref[0], cq_ref[...], sq_ref[...]) * (DIFF_D ** -0.5)
    lane = lax.broadcasted_iota(jnp.int32, q.shape, 1)
    first = lane < DIFF_D
    kb = k_all[...]
    s1 = _dot_nt(jnp.where(first, q, 0.0).astype(BF16), kb)
    s2 = _dot_nt(jnp.where(first, 0.0, q).astype(BF16), kb)
    a = _softmax_pair_diff(s1, s2, lam)
    o = _dot(a.astype(BF16), v_all[...])
    o_ref[0] = _sub_ln(o, g_ref[...], lam_init).astype(o_ref.dtype)


def diff_lat(proj3, k_ctx, v_ctx, cos, sin, lam, subln_g, lam_init, tq=256):
    b, seq, _ = proj3.shape
    lc = k_ctx.shape[3]
    hd = DIFF_HEAD_DIM
    ctx_spec = pl.BlockSpec((1, 1, 1, lc, hd), lambda i, h, q: (i, 0, h, 0, 0))
    tq_spec = pl.BlockSpec((tq, hd), lambda i, h, q: (q, 0))
    full_spec = pl.BlockSpec((seq, hd), lambda i, h, q: (0, 0))
    return pl.pallas_call(
        functools.partial(_diff_lat_kernel, lam_init=lam_init),
        grid=(b, DIFF_HEADS, seq // tq),
        in_specs=[pl.BlockSpec(memory_space=pltpu.SMEM),
                  pl.BlockSpec((1, tq, hd), lambda i, h, q: (i, q, h)),
                  pl.BlockSpec((1, seq, hd), lambda i, h, q: (i, 0, DIFF_HEADS + h)),
                  pl.BlockSpec((1, seq, hd), lambda i, h, q: (i, 0, 2 * DIFF_HEADS + h)),
                  ctx_spec, ctx_spec, tq_spec, tq_spec, full_spec, full_spec,
                  pl.BlockSpec((1, hd), lambda i, h, q: (0, 0))],
        out_specs=pl.BlockSpec((1, tq, hd), lambda i, h, q: (i, q, h)),
        out_shape=jax.ShapeDtypeStruct((b, seq, HALF_MIX), BF16),
        scratch_shapes=[pltpu.VMEM((seq + lc, hd), BF16), pltpu.VMEM((seq + lc, hd), BF16)],
        compiler_params=_params("arbitrary", "arbitrary", "arbitrary"),
        name="diff_lat",
    )(lam.reshape(1, 1), proj3, proj3, proj3, k_ctx, v_ctx, cos, sin, cos, sin,
      subln_g.reshape(1, hd))


CONV_PAD = 16
CONV_SUB = 64


def _conv_kernel(a_ref, g_ref, ap_ref, gp_ref, an_ref, gn_ref, w_ref, b_ref, lg_ref, lb_ref, o_ref, xp_scr,
                 xsh_scr):
    t = pl.program_id(1)
    tt = a_ref.shape[1]
    prev = ap_ref[0] * jax.nn.sigmoid(gp_ref[0])
    nxt = an_ref[0] * jax.nn.sigmoid(gn_ref[0])
    xp_scr[0:CONV_PAD, :] = jnp.where(t > 0, prev, 0.0)
    xp_scr[CONV_PAD + tt:, :] = jnp.where(t < pl.num_programs(1) - 1, nxt, 0.0)
    xp_scr[CONV_PAD:CONV_PAD + tt, :] = a_ref[0] * jax.nn.sigmoid(g_ref[0])
    first_tap = CONV_PAD - CONV_WIDTH // 2
    n_rows = xsh_scr.shape[1]
    for b in range(SUBLANES):
        xsh_scr[b] = xp_scr[b:b + n_rows, :]
    for i in range(tt // CONV_SUB):
        s = i * CONV_SUB
        acc = jnp.zeros((CONV_SUB, HALF_MIX), F32)
        for j in range(CONV_WIDTH):
            whole, phase = divmod(first_tap + j, SUBLANES)
            lo = s + whole * SUBLANES
            acc = acc + xsh_scr[phase, lo:lo + CONV_SUB, :] * w_ref[j:j + 1, :]
        y = acc + b_ref[...]
        mu = jnp.mean(y, axis=-1, keepdims=True)
        yc = y - mu
        var = jnp.mean(yc * yc, axis=-1, keepdims=True)
        yn = yc * lax.rsqrt(var + EPS) * lg_ref[...] + lb_ref[...]
        o_ref[0, s:s + CONV_SUB, :] = _silu(yn).astype(o_ref.dtype)


def conformer_conv(proj3, w, b, ln_g, ln_b, tt):
    bsz, seq, _ = proj3.shape
    n_t = seq // tt
    hb = tt // CONV_PAD
    last = seq // CONV_PAD - 1
    vec = pl.BlockSpec((1, HALF_MIX), lambda i, t: (0, 0))

    def main(col):
        return pl.BlockSpec((1, tt, HALF_MIX), lambda i, t: (i, t, col))

    def prev(col):
        return pl.BlockSpec((1, CONV_PAD, HALF_MIX), lambda i, t: (i, jnp.maximum(t * hb - 1, 0), col))

    def nxt(col):
        return pl.BlockSpec((1, CONV_PAD, HALF_MIX), lambda i, t: (i, jnp.minimum((t + 1) * hb, last), col))

    return pl.pallas_call(
        _conv_kernel,
        grid=(bsz, n_t),
        in_specs=[main(3), main(4), prev(3), prev(4), nxt(3), nxt(4),
                  pl.BlockSpec((CONV_WIDTH, HALF_MIX), lambda i, t: (0, 0)),
                  vec, vec, vec],
        out_specs=pl.BlockSpec((1, tt, HALF_MIX), lambda i, t: (i, t, 0)),
        out_shape=jax.ShapeDtypeStruct((bsz, seq, HALF_MIX), BF16),
        scratch_shapes=[pltpu.VMEM((tt + 2 * CONV_PAD, HALF_MIX), F32),
                        pltpu.VMEM((SUBLANES, tt + 2 * CONV_PAD - SUBLANES, HALF_MIX), F32)],
        compiler_params=_params("arbitrary", "arbitrary"),
        name="conformer_conv",
    )(proj3, proj3, proj3, proj3, proj3, proj3, w, b.reshape(1, HALF_MIX), ln_g.reshape(1, HALF_MIX),
      ln_b.reshape(1, HALF_MIX))


FFN_CHUNK = 1024


def _mix_ffn_kernel(x_ref, y_ref, u_ref, d_ref, wglu_ref, bglu_ref, m2_ref, w1_ref, w2_ref, g1_ref,
                    g_ref, sh_ref, sc_ref, g2_ref, wg_ref, wu_ref, wd_ref, o_ref):
    z = jax.nn.gelu(u_ref[...] * d_ref[...] + y_ref[...])
    s5_out = z * jax.nn.sigmoid(_dot(z.astype(BF16), wglu_ref[...]) + bglu_ref[...])
    mix = _dot(s5_out.astype(BF16), w1_ref[...]) + _dot(m2_ref[...], w2_ref[...])
    x1 = x_ref[...] + g1_ref[0] * mix
    h = _norm_mod(x1, g_ref[...], sh_ref[0], sc_ref[0]).astype(BF16)
    fdim = wg_ref.shape[1]
    acc = None
    for lo in range(0, fdim, FFN_CHUNK):
        hi = min(lo + FFN_CHUNK, fdim)
        a = _dot(h, wg_ref[:, lo:hi])
        up = _dot(h, wu_ref[:, lo:hi])
        part = _dot((_silu(a) * up).astype(BF16), wd_ref[lo:hi, :])
        acc = part if acc is None else acc + part
    o_ref[...] = x1 + g2_ref[0] * acc


def mix_ffn(x, y, u, d_skip, w_glu, b_glu, m2, w_out, g, mod, tiles_per_row, w_gate, w_up, w_down, tm=512):
    t = x.shape[0]
    fdim = w_gate.shape[1]
    half = pl.BlockSpec((tm, HALF_MIX), lambda i: (i, 0))
    vec = pl.BlockSpec((1, HALF_MIX), lambda i: (0, 0))

    def resident(shape, index):
        return pl.BlockSpec(shape, index, pipeline_mode=pl.Buffered(1))

    return pl.pallas_call(
        _mix_ffn_kernel,
        grid=(t // tm,),
        in_specs=[pl.BlockSpec((tm, D_MODEL), lambda i: (i, 0)), half, half, vec,
                  resident((HALF_MIX, HALF_MIX), lambda i: (0, 0)), vec, half,
                  resident((HALF_MIX, D_MODEL), lambda i: (0, 0)),
                  resident((HALF_MIX, D_MODEL), lambda i: (1, 0)),
                  _mod_spec(2, tiles_per_row),
                  pl.BlockSpec((1, D_MODEL), lambda i: (0, 0)),
                  _mod_spec(3, tiles_per_row), _mod_spec(4, tiles_per_row), _mod_spec(5, tiles_per_row),
                  resident((D_MODEL, fdim), lambda i: (0, 0)),
                  resident((D_MODEL, fdim), lambda i: (0, 0)),
                  resident((fdim, D_MODEL), lambda i: (0, 0))],
        out_specs=pl.BlockSpec((tm, D_MODEL), lambda i: (i, 0)),
        out_shape=jax.ShapeDtypeStruct((t, D_MODEL), F32),
        compiler_params=_params("arbitrary"),
        name="mix_ffn",
    )(x, y, u, d_skip.reshape(1, HALF_MIX), w_glu, b_glu.reshape(1, HALF_MIX), m2, w_out, w_out, mod,
      g.reshape(1, D_MODEL), mod, mod, mod, w_gate, w_up, w_down)


MOE_ROW_TILE = 1024
MOE_ZERO_ROWS = 256
MOE_ROW_PARTS = 4
INFO_E0, INFO_E1, INFO_G0, INFO_G1, INFO_R0, INFO_R1 = range(6)


def _stream_mod_spec(chunk, prompt_tiles, tiles_per_req, ctx_row):
    def index(i, *_):
        return (jnp.where(i < prompt_tiles, ctx_row, (i - prompt_tiles) // tiles_per_req), 0, chunk)
    return pl.BlockSpec((1, 1, D_MODEL), index)


def _mix_route_kernel(xp_ref, m1p_ref, m2p_ref, xs_ref, m1s_ref, m2s_ref, w1_ref, w2_ref, gate_ref,
                      g_ref, sh_ref, sc_ref, rwh_ref, rwl_ref, o_ref, info_ref, fields_ref, cnt_ref,
                      tri_scr, run_scr, *, prompt_tiles):
    i = pl.program_id(0)
    tm = o_ref.shape[0]
    w1 = w1_ref[...]
    w2 = w2_ref[...]

    @pl.when(i < prompt_tiles)
    def _():
        o_ref[...] = xp_ref[...] + gate_ref[0] * (_dot(m1p_ref[...], w1) + _dot(m2p_ref[...], w2))

    @pl.when(i >= prompt_tiles)
    def _():
        o_ref[...] = xs_ref[...] + gate_ref[0] * (_dot(m1s_ref[...], w1) + _dot(m2s_ref[...], w2))

    @pl.when(i == 0)
    def _():
        r = lax.broadcasted_iota(jnp.int32, (tm, tm), 0)
        c = lax.broadcasted_iota(jnp.int32, (tm, tm), 1)
        tri_scr[...] = jnp.where(c < r, 1.0, 0.0).astype(BF16)
        run_scr[...] = jnp.zeros_like(run_scr)

    h = _norm_mod(o_ref[...], g_ref[...], sh_ref[0], sc_ref[0])
    h_hi = h.astype(BF16)
    h_lo = (h - h_hi.astype(F32)).astype(BF16)
    logits = _dot(h_hi, rwh_ref[...]) + (_dot(h_hi, rwl_ref[...]) + _dot(h_lo, rwh_ref[...]))
    lane = lax.broadcasted_iota(jnp.int32, logits.shape, 1).astype(F32)
    logits = jnp.where(lane < N_EXPERTS, logits, -jnp.inf)
    m1 = jnp.max(logits, axis=-1, keepdims=True)
    i1 = jnp.min(jnp.where(logits == m1, lane, float(LANES)), axis=-1, keepdims=True)
    rest = jnp.where(lane == i1, -jnp.inf, logits)
    m2 = jnp.max(rest, axis=-1, keepdims=True)
    i2 = jnp.min(jnp.where(rest == m2, lane, float(LANES)), axis=-1, keepdims=True)
    e2 = jnp.exp(m2 - m1)
    den = 1.0 + e2
    hit = jnp.where(lane == i1, 1.0, 0.0) + jnp.where(lane == i2, 1.0, 0.0)
    before = _dot(tri_scr[...], hit.astype(BF16)) + run_scr[0:1, :]
    r1 = jnp.sum(jnp.where(lane == i1, before, 0.0), axis=-1, keepdims=True)
    r2 = jnp.sum(jnp.where(lane == i2, before, 0.0), axis=-1, keepdims=True)
    info = jnp.zeros_like(logits)
    for slot, val in ((INFO_E0, i1), (INFO_E1, i2), (INFO_G0, 1.0 / den), (INFO_G1, e2 / den),
                      (INFO_R0, r1), (INFO_R1, r2)):
        info = jnp.where(lane == float(slot), val, info)
    info_ref[...] = info
    fields_ref[...] = jnp.transpose(info)[0:SUBLANES, :]
    run_scr[...] = run_scr[...] + jnp.sum(hit, axis=0, keepdims=True)
    cnt_ref[...] = run_scr[...]


def mix_route(xp, m1p, m2p, xs, m1s, m2s, w_out, g, mod_all, ctx_row, req_tokens, router_w, tm=512):
    tp, ts = xp.shape[0], xs.shape[0]
    t = tp + ts
    pt = tp // tm
    rw = jnp.pad(router_w.astype(F32), ((0, 0), (0, LANES - N_EXPERTS)))
    rw_hi = rw.astype(BF16)
    rw_lo = (rw - rw_hi.astype(F32)).astype(BF16)

    def p_spec(width):
        return pl.BlockSpec((tm, width), lambda i: (jnp.minimum(i, pt - 1), 0))

    def s_spec(width):
        return pl.BlockSpec((tm, width), lambda i: (jnp.maximum(i - pt, 0), 0))

    def mod_spec(chunk):
        return _stream_mod_spec(chunk, pt, req_tokens // tm, ctx_row)

    rw_spec = pl.BlockSpec((D_MODEL, LANES), lambda i: (0, 0))
    return pl.pallas_call(
        functools.partial(_mix_route_kernel, prompt_tiles=pt),
        grid=(t // tm,),
        in_specs=[p_spec(D_MODEL), p_spec(HALF_MIX), p_spec(HALF_MIX),
                  s_spec(D_MODEL), s_spec(HALF_MIX), s_spec(HALF_MIX),
                  pl.BlockSpec((HALF_MIX, D_MODEL), lambda i: (0, 0)),
                  pl.BlockSpec((HALF_MIX, D_MODEL), lambda i: (1, 0)),
                  mod_spec(2), pl.BlockSpec((1, D_MODEL), lambda i: (0, 0)), mod_spec(3), mod_spec(4),
                  rw_spec, rw_spec],
        out_specs=[pl.BlockSpec((tm, D_MODEL), lambda i: (i, 0)),
                   pl.BlockSpec((tm, LANES), lambda i: (i, 0)),
                   pl.BlockSpec((SUBLANES, tm), lambda i: (0, i)),
                   pl.BlockSpec((SUBLANES, LANES), lambda i: (0, 0))],
        out_shape=[jax.ShapeDtypeStruct((t, D_MODEL), F32), jax.ShapeDtypeStruct((t, LANES), F32),
                   jax.ShapeDtypeStruct((SUBLANES, t), F32), jax.ShapeDtypeStruct((SUBLANES, LANES), F32)],
        scratch_shapes=[pltpu.VMEM((tm, tm), BF16), pltpu.VMEM((SUBLANES, LANES), F32)],
        compiler_params=_params("arbitrary"),
        name="mix_route",
    )(xp, m1p, m2p, xs, m1s, m2s, w_out, w_out, mod_all, g.reshape(1, D_MODEL), mod_all, mod_all,
      rw_hi, rw_lo)


def _dispatch_kernel(d0_ref, d1_ref, fill_ref, x_ref, g_ref, sh_ref, sc_ref, xs_hbm, h_scr, zero_scr, sem, zsem):
    i = pl.program_id(0)
    n = pl.num_programs(0)
    tm = x_ref.shape[0]
    slot = i % 2

    def row_copy(r, dst, s):
        return pltpu.make_async_copy(h_scr.at[s, pl.ds(r, 1), :], xs_hbm.at[pl.ds(dst, 1), :], sem.at[s])

    def wait_rows(s):
        for _ in range(2):
            pltpu.make_async_copy(h_scr.at[s], xs_hbm.at[pl.ds(0, tm), :], sem.at[s]).wait()

    @pl.when(i >= 2)
    def _():
        wait_rows(slot)

    h_scr[slot] = _norm_mod(x_ref[...], g_ref[...], sh_ref[0], sc_ref[0])
    base = i * tm

    def body(r8, c):
        rb = pl.multiple_of(r8 * SUBLANES, SUBLANES)
        for k in range(SUBLANES):
            row_copy(rb + k, d0_ref[base + rb + k], slot).start()
            row_copy(rb + k, d1_ref[base + rb + k], slot).start()
        return c

    lax.fori_loop(0, tm // SUBLANES, body, 0)

    @pl.when(i == n - 1)
    def _():
        zero_scr[...] = jnp.zeros_like(zero_scr)

        def zero_row(r):
            return pltpu.make_async_copy(zero_scr.at[pl.ds(0, 1), :], xs_hbm.at[pl.ds(r, 1), :], zsem)

        def zero_block(b):
            start = pl.multiple_of(b * MOE_ZERO_ROWS, MOE_ZERO_ROWS)
            return pltpu.make_async_copy(zero_scr, xs_hbm.at[pl.ds(start, MOE_ZERO_ROWS), :], zsem)

        def start_all(copy):
            def body(r, c):
                copy(r).start()
                return c
            return body

        def wait_all(copy):
            def body(r, c):
                copy(r).wait()
                return c
            return body

        def zero_group(b):
            start = pl.multiple_of(b * SUBLANES, SUBLANES)
            return pltpu.make_async_copy(zero_scr.at[pl.ds(0, SUBLANES), :],
                                         xs_hbm.at[pl.ds(start, SUBLANES), :], zsem)

        for e in range(N_EXPERTS):
            lo = fill_ref[e]
            hi = fill_ref[N_EXPERTS + e]
            lo_group = (lo + SUBLANES - 1) // SUBLANES
            lax.fori_loop(lo, lo_group * SUBLANES, start_all(zero_row), 0)
            lax.fori_loop(lo_group, hi // SUBLANES, start_all(zero_group), 0)
            lax.fori_loop(lo, lo_group * SUBLANES, wait_all(zero_row), 0)
            lax.fori_loop(lo_group, hi // SUBLANES, wait_all(zero_group), 0)
        blocks_per_tile = MOE_ROW_TILE // MOE_ZERO_ROWS
        first = fill_ref[2 * N_EXPERTS] * blocks_per_tile
        last = (xs_hbm.shape[0] // MOE_ROW_TILE) * blocks_per_tile
        lax.fori_loop(first, last, start_all(zero_block), 0)
        lax.fori_loop(first, last, wait_all(zero_block), 0)
        wait_rows(slot)

        @pl.when(n >= 2)
        def _():
            wait_rows(1 - slot)


def moe_dispatch(x, g, mod_all, ctx_row, prompt_tokens, req_tokens, dest0, dest1, fill, n_tiles, tm=512):
    t = x.shape[0]
    pt = prompt_tokens // tm
    grid_spec = pltpu.PrefetchScalarGridSpec(
        num_scalar_prefetch=3,
        grid=(t // tm,),
        in_specs=[pl.BlockSpec((tm, D_MODEL), lambda i, *_: (i, 0)),
                  pl.BlockSpec((1, D_MODEL), lambda i, *_: (0, 0)),
                  _stream_mod_spec(3, pt, req_tokens // tm, ctx_row),
                  _stream_mod_spec(4, pt, req_tokens // tm, ctx_row)],
        out_specs=pl.BlockSpec(memory_space=pl.ANY),
        scratch_shapes=[pltpu.VMEM((2, tm, D_MODEL), F32), pltpu.VMEM((MOE_ZERO_ROWS, D_MODEL), F32),
                        pltpu.SemaphoreType.DMA((2,)), pltpu.SemaphoreType.DMA(())],
    )
    return pl.pallas_call(
        _dispatch_kernel,
        grid_spec=grid_spec,
        out_shape=jax.ShapeDtypeStruct((n_tiles * MOE_ROW_TILE, D_MODEL), F32),
        compiler_params=pltpu.CompilerParams(dimension_semantics=("arbitrary",),
                                             vmem_limit_bytes=VMEM_LIMIT_BYTES,
                                             disable_bounds_checks=True),
        name="moe_dispatch",
    )(dest0, dest1, fill, x, g.reshape(1, D_MODEL), mod_all, mod_all)


def _experts_kernel(te_ref, tv_ref, x_ref, wg_ref, wu_ref, wd_ref, o_ref, h_scr, acc_scr):
    i = pl.program_id(0)
    f = pl.program_id(1)
    last_f = pl.num_programs(1) - 1

    n_valid = tv_ref[i]
    tile_rows = h_scr.shape[0]
    step = tile_rows // MOE_ROW_PARTS

    def swiglu_rows(n_rows):
        rows = slice(0, n_rows)

        @pl.when(f == 0)
        def _():
            h_scr[rows, :] = x_ref[rows, :].astype(BF16)
            acc_scr[rows, :] = jnp.zeros((n_rows, D_MODEL), F32)

        h = h_scr[rows, :]
        a = _dot(h, wg_ref[0].astype(BF16))
        u = _dot(h, wu_ref[0].astype(BF16))
        acc_scr[rows, :] += _dot((_silu(a) * u).astype(BF16), wd_ref[0].astype(BF16))

        @pl.when(f == last_f)
        def _():
            o_ref[rows, :] = acc_scr[rows, :]

    for part in range(1, MOE_ROW_PARTS + 1):
        n_rows = part * step

        @pl.when((n_valid > n_rows - step) & (n_valid <= n_rows))
        def _(n_rows=n_rows):
            swiglu_rows(n_rows)
            if n_rows < tile_rows:
                @pl.when(f == last_f)
                def _():
                    o_ref[n_rows:, :] = jnp.zeros((tile_rows - n_rows, D_MODEL), F32)

    @pl.when((n_valid == 0) & (f == last_f))
    def _():
        o_ref[...] = jnp.zeros_like(o_ref)


def moe_experts(xs_sorted, tile_expert, tile_valid, w_gate, w_up, w_down, tf=512):
    rows = xs_sorted.shape[0]
    fdim = w_gate.shape[2]
    n_f = fdim // tf
    tr = MOE_ROW_TILE

    def f_eff(i, f, tv):
        return jnp.where(tv[i] > 0, f, n_f - 1)

    grid_spec = pltpu.PrefetchScalarGridSpec(
        num_scalar_prefetch=2,
        grid=(rows // tr, n_f),
        in_specs=[pl.BlockSpec((tr, D_MODEL), lambda i, f, te, tv: (i, 0)),
                  pl.BlockSpec((1, D_MODEL, tf), lambda i, f, te, tv: (te[i], 0, f_eff(i, f, tv))),
                  pl.BlockSpec((1, D_MODEL, tf), lambda i, f, te, tv: (te[i], 0, f_eff(i, f, tv))),
                  pl.BlockSpec((1, tf, D_MODEL), lambda i, f, te, tv: (te[i], f_eff(i, f, tv), 0))],
        out_specs=pl.BlockSpec((tr, D_MODEL), lambda i, f, te, tv: (i, 0)),
        scratch_shapes=[pltpu.VMEM((tr, D_MODEL), BF16), pltpu.VMEM((tr, D_MODEL), F32)],
    )
    return pl.pallas_call(
        _experts_kernel,
        grid_spec=grid_spec,
        out_shape=jax.ShapeDtypeStruct((rows, D_MODEL), F32),
        compiler_params=_params("arbitrary", "arbitrary"),
        name="moe_experts",
    )(tile_expert, tile_valid, xs_sorted, w_gate, w_up, w_down)


def _combine_kernel(d0_ref, d1_ref, x_ref, info_ref, gate_ref, fg_ref, ys_hbm, op_ref, os_ref, rbuf, sem,
                    *, prompt_tiles):
    i = pl.program_id(0)
    n = pl.num_programs(0)
    tm = x_ref.shape[0]
    slot = i % 2

    def issue(tile, s):
        base = tile * tm

        def body(r8, c):
            rb = pl.multiple_of(r8 * SUBLANES, SUBLANES)
            for k in range(SUBLANES):
                pltpu.make_async_copy(ys_hbm.at[pl.ds(d0_ref[base + rb + k], 1), :],
                                      rbuf.at[s, 0, pl.ds(rb + k, 1), :], sem.at[s]).start()
                pltpu.make_async_copy(ys_hbm.at[pl.ds(d1_ref[base + rb + k], 1), :],
                                      rbuf.at[s, 1, pl.ds(rb + k, 1), :], sem.at[s]).start()
            return c

        lax.fori_loop(0, tm // SUBLANES, body, 0)

    @pl.when(i == 0)
    def _():
        issue(0, 0)

    @pl.when(i + 1 < n)
    def _():
        issue(i + 1, 1 - slot)

    for k in range(2):
        pltpu.make_async_copy(ys_hbm.at[pl.ds(0, tm), :], rbuf.at[slot, k], sem.at[slot]).wait()

    info = info_ref[...]
    moe = info[:, INFO_G0:INFO_G0 + 1] * rbuf[slot, 0] + info[:, INFO_G1:INFO_G1 + 1] * rbuf[slot, 1]
    y = x_ref[...] + gate_ref[0] * moe
    out = y * lax.rsqrt(jnp.mean(y * y, axis=-1, keepdims=True) + EPS) * fg_ref[...]

    @pl.when(i < prompt_tiles)
    def _():
        op_ref[...] = out

    @pl.when(i >= prompt_tiles)
    def _():
        os_ref[...] = out


def moe_combine(x, info, ys_sorted, dest0, dest1, mod_all, ctx_row, prompt_tokens, req_tokens, final_g, tm=512):
    t = x.shape[0]
    pt = prompt_tokens // tm
    grid_spec = pltpu.PrefetchScalarGridSpec(
        num_scalar_prefetch=2,
        grid=(t // tm,),
        in_specs=[pl.BlockSpec((tm, D_MODEL), lambda i, *_: (i, 0)),
                  pl.BlockSpec((tm, LANES), lambda i, *_: (i, 0)),
                  _stream_mod_spec(5, pt, req_tokens // tm, ctx_row),
                  pl.BlockSpec((1, D_MODEL), lambda i, *_: (0, 0)),
                  pl.BlockSpec(memory_space=pl.ANY)],
        out_specs=[pl.BlockSpec((tm, D_MODEL), lambda i, *_: (jnp.minimum(i, pt - 1), 0)),
                   pl.BlockSpec((tm, D_MODEL), lambda i, *_: (jnp.maximum(i - pt, 0), 0))],
        scratch_shapes=[pltpu.VMEM((2, 2, tm, D_MODEL), F32), pltpu.SemaphoreType.DMA((2,))],
    )
    return pl.pallas_call(
        functools.partial(_combine_kernel, prompt_tiles=pt),
        grid_spec=grid_spec,
        out_shape=[jax.ShapeDtypeStruct((prompt_tokens, D_MODEL), F32),
                   jax.ShapeDtypeStruct((t - prompt_tokens, D_MODEL), F32)],
        compiler_params=pltpu.CompilerParams(dimension_semantics=("arbitrary",),
                                             vmem_limit_bytes=VMEM_LIMIT_BYTES,
                                             disable_bounds_checks=True),
        name="moe_combine",
    )(dest0, dest1, x, info, mod_all, final_g.reshape(1, D_MODEL), ys_sorted)


def moe_layout(counts, fields, n_tiles):
    nt_e = (counts + MOE_ROW_TILE - 1) // MOE_ROW_TILE
    ends = jnp.cumsum(nt_e)
    total = ends[-1]
    offset = (ends - nt_e) * MOE_ROW_TILE
    experts = jnp.arange(N_EXPERTS, dtype=jnp.int32)

    def dest(e_lane, r_lane):
        e = fields[e_lane].astype(jnp.int32)
        off = jnp.sum(jnp.where(e[:, None] == experts[None, :], offset[None, :], 0), axis=1)
        return (off + fields[r_lane].astype(jnp.int32)).astype(jnp.int32)

    fill = jnp.concatenate([offset + counts, ends * MOE_ROW_TILE, total[None]]).astype(jnp.int32)
    ids = jnp.arange(n_tiles, dtype=jnp.int32)
    ids_c = jnp.minimum(ids, total - 1)
    te = jnp.sum((ids_c[:, None] >= ends[None, :]).astype(jnp.int32), axis=1)
    mine = te[:, None] == experts[None, :]
    first_tile = jnp.sum(jnp.where(mine, (ends - nt_e)[None, :], 0), axis=1)
    count = jnp.sum(jnp.where(mine, counts[None, :], 0), axis=1)
    rows_left = jnp.clip(count - (ids - first_tile) * MOE_ROW_TILE, 0, MOE_ROW_TILE)
    tile_rows = jnp.where(ids < total, rows_left, 0)
    return (dest(INFO_E0, INFO_R0), dest(INFO_E1, INFO_R1), fill, te.astype(jnp.int32),
            tile_rows.astype(jnp.int32))


def moe_final(x, info, fields, cnt, g, mod_all, ctx_row, prompt_tokens, req_tokens, final_g, w_gate, w_up,
              w_down):
    t = x.shape[0]
    counts = cnt[0, :N_EXPERTS].astype(jnp.int32)
    n_tiles = (2 * t) // MOE_ROW_TILE + N_EXPERTS
    dest0, dest1, fill, te, tv = moe_layout(counts, fields, n_tiles)
    xs_sorted = moe_dispatch(x, g, mod_all, ctx_row, prompt_tokens, req_tokens, dest0, dest1, fill, n_tiles)
    ys_sorted = moe_experts(xs_sorted, te, tv, w_gate, w_up, w_down)
    return moe_combine(x, info, ys_sorted, dest0, dest1, mod_all, ctx_row, prompt_tokens, req_tokens, final_g)


def kernel(x_prompt, x_sample, state_s5, cache_na_k, cache_na_v, cache_diff_k, cache_diff_v, c, c_ctx, w_mod, b_mod, norm_mix_g, norm_ffn_g, final_norm_g, w_in_e, w_out_e, s5_lam_re, s5_lam_im, s5_log_dt, s5_b_re, s5_b_im, s5_c_re, s5_c_im, s5_d, s5_w_glu, s5_b_glu, na_rpb, ffn_w_gate, ffn_w_up, ffn_w_down, w_in_o, w_out_o, diff_lam_q1, diff_lam_k1, diff_lam_q2, diff_lam_k2, diff_subln_g, conv_w, conv_b, conv_ln_g, conv_ln_b, router_w, moe_w_gate, moe_w_up, moe_w_down):
    bp, lp, d = x_prompt.shape
    bs, ls, _ = x_sample.shape
    tm = 1024
    xp = x_prompt.reshape(bp * lp, d)
    xs = x_sample.reshape(bs * ls, d)
    rows_p = (bp * lp) // tm
    rows_s = ls // tm

    cond8 = jnp.concatenate([c, c_ctx[None, :], jnp.zeros((SUBLANES - bs - 1, d), F32)], axis=0)
    mod = adaln_all(cond8, w_mod, b_mod)
    mod_s = mod[:, 0:bs]
    mod_p = mod[:, bs:bs + 1]

    def tiles(rows, tile):
        return rows * tm // tile

    bmat, cmat, lam8 = s5_params(s5_lam_re[0], s5_lam_im[0], s5_log_dt[0], s5_b_re[0], s5_b_im[0],
                                 s5_c_re[0], s5_c_im[0])
    bias = na_bias_blocks(na_rpb[0])
    n_e = w_in_e.shape[-1]

    tiles_p = tiles(rows_p, 512)
    tiles_s = tiles(rows_s, 512)
    proj_p = in_proj(xp, norm_mix_g[0], mod_p[0], tiles_p, w_in_e[0])
    u_s, qkv_s = in_proj(xs, norm_mix_g[0], mod_s[0], tiles_s, w_in_e[0], n_f32=HALF_MIX)

    y_p, st_p = s5_scan(proj_p.reshape(bp, lp, n_e), bmat, cmat, lam8, None, 1)
    chunks = SUBLANES // bs
    h0 = state_s5[:, 0].reshape(bs, 2, 2, S5_GROUPS * S5_STATE)
    y_s, _ = s5_scan(u_s.reshape(bs * chunks, ls // chunks, HALF_MIX), bmat, cmat, lam8, h0, chunks)
    nao_p, na_k, na_v = na_ctx(proj_p.reshape(bp, lp, n_e))

    def heads_to_lanes(cache):
        return cache.transpose(0, 2, 1, 3).reshape(bs, cache.shape[2], HALF_MIX).astype(BF16)

    nao_s = na_lat(qkv_s.reshape(bs, ls, n_e - HALF_MIX), heads_to_lanes(cache_na_k[:, 0]),
                   heads_to_lanes(cache_na_v[:, 0]), bias)

    glu_w = (s5_d[0], s5_w_glu[0].astype(BF16), s5_b_glu[0])
    ffn_w = (w_out_e[0].astype(BF16), norm_ffn_g[0])
    ffn_w3 = (ffn_w_gate[0].astype(BF16), ffn_w_up[0].astype(BF16), ffn_w_down[0].astype(BF16))
    xp = mix_ffn(xp, y_p.reshape(bp * lp, HALF_MIX), proj_p, *glu_w, nao_p.reshape(bp * lp, HALF_MIX),
                 *ffn_w, mod_p[0], tiles_p, *ffn_w3)
    xs = mix_ffn(xs, y_s.reshape(bs * ls, HALF_MIX), u_s, *glu_w, nao_s.reshape(bs * ls, HALF_MIX),
                 *ffn_w, mod_s[0], tiles_s, *ffn_w3)

    lam_init = 0.8 - 0.6 * math.exp(-0.3 * 1)
    lam = (jnp.exp(jnp.sum(diff_lam_q1[0].astype(F32) * diff_lam_k1[0].astype(F32)))
           - jnp.exp(jnp.sum(diff_lam_q2[0].astype(F32) * diff_lam_k2[0].astype(F32)))
           + lam_init)
    cos, sin = rope_tables(ls)
    n_o = w_in_o.shape[-1]

    proj_p = in_proj(xp, norm_mix_g[1], mod_p[1], tiles_p, w_in_o[0])
    proj_s = in_proj(xs, norm_mix_g[1], mod_s[1], tiles_s, w_in_o[0])

    do_p, diff_k, diff_v = diff_ctx(proj_p.reshape(bp, lp, n_o), lam, diff_subln_g[0], lam_init)
    do_s = diff_lat(proj_s.reshape(bs, ls, n_o), cache_diff_k[:, 0:1], cache_diff_v[:, 0:1], cos, sin,
                    lam, diff_subln_g[0], lam_init)
    co_p = conformer_conv(proj_p.reshape(bp, lp, n_o), conv_w[0], conv_b[0], conv_ln_g[0], conv_ln_b[0], lp)
    co_s = conformer_conv(proj_s.reshape(bs, ls, n_o), conv_w[0], conv_b[0], conv_ln_g[0], conv_ln_b[0], 512)

    mod_all = mod[1]
    x_all, info, fields, cnt = mix_route(
        xp, do_p.reshape(bp * lp, HALF_MIX), co_p.reshape(bp * lp, HALF_MIX),
        xs, do_s.reshape(bs * ls, HALF_MIX), co_s.reshape(bs * ls, HALF_MIX),
        w_out_o[0].astype(BF16), norm_ffn_g[1], mod_all, bs, ls, router_w[0])
    yp, ys = moe_final(x_all, info, fields, cnt, norm_ffn_g[1], mod_all, bs, bp * lp, ls, final_norm_g,
                       moe_w_gate[0], moe_w_up[0], moe_w_down[0])

    new_state = st_p.reshape(bp, 1, 2, 2, S5_GROUPS, S5_STATE)
    return (yp.reshape(bp, lp, d), ys.reshape(bs, ls, d), new_state, na_k, na_v, diff_k, diff_v)
```

```python
import functools
import math

import jax
import jax.numpy as jnp
import numpy as np
from jax import lax
from jax.experimental import pallas as pl
from jax.experimental.pallas import tpu as pltpu

D_MODEL = 1024
DEPTH = 2
GRID_W = 64
HALF_MIX = 512
S5_GROUP_CH = 16
S5_GROUPS = 32
S5_STATE = 64
NA_HEAD_DIM = 64
NA_HEADS = 8
NA_WIN_R = 8
NA_WIN_C = 16
DIFF_D = 64
DIFF_HEAD_DIM = 128
DIFF_HEADS = 4
ROPE_BASE = 10000.0
CONV_WIDTH = 31
N_EXPERTS = 8
EPS = 1e-6

F32 = jnp.float32
BF16 = jnp.bfloat16
NEG_BIG = -1e30

VMEM_LIMIT_BYTES = 56 * 1024 * 1024
LANES = 128
SUBLANES = 8

S5_COL_GROUPS = 8
S5_COL_CH = S5_COL_GROUPS * S5_GROUP_CH
S5_COL_STATE = S5_COL_GROUPS * S5_STATE
S5_N_COL = S5_GROUPS // S5_COL_GROUPS
S5_TIME_BLOCK = 256


def _params(*sem):
    return pltpu.CompilerParams(dimension_semantics=sem, vmem_limit_bytes=VMEM_LIMIT_BYTES)


def _dot(a, b):
    return jnp.dot(a, b, preferred_element_type=F32)


def _dot_nt(a, b):
    return lax.dot_general(a, b, (((1,), (1,)), ((), ())), preferred_element_type=F32)


def _silu(x):
    return x * jax.nn.sigmoid(x)


def _norm_mod(x, g, shift, scale):
    y = x * lax.rsqrt(jnp.mean(x * x, axis=-1, keepdims=True) + EPS) * g
    return y * (1.0 + scale) + shift


def _mod_kernel(cond_ref, w_ref, b_ref, o_ref):
    s = _silu(cond_ref[...])
    o_ref[0, :, 0, :] = jnp.dot(s, w_ref[0], precision=lax.Precision.HIGHEST,
                                preferred_element_type=F32) + b_ref[0]


def adaln_all(cond8, w_mod, b_mod):
    tn = 1536
    n = w_mod.shape[-1]
    return pl.pallas_call(
        _mod_kernel,
        grid=(DEPTH, n // tn),
        in_specs=[pl.BlockSpec((SUBLANES, D_MODEL), lambda l, j: (0, 0)),
                  pl.BlockSpec((1, D_MODEL, tn), lambda l, j: (l, 0, j)),
                  pl.BlockSpec((1, 1, tn), lambda l, j: (l, 0, j))],
        out_specs=pl.BlockSpec((1, SUBLANES, 1, tn), lambda l, j: (l, 0, 0, j)),
        out_shape=jax.ShapeDtypeStruct((DEPTH, SUBLANES, 1, n), F32),
        compiler_params=_params("arbitrary", "arbitrary"),
        name="adaln_mod",
    )(cond8, w_mod, b_mod.reshape(DEPTH, 1, n))


def _mod_spec(chunk, tiles_per_row):
    return pl.BlockSpec((1, 1, D_MODEL), lambda i, *_: (i // tiles_per_row, 0, chunk))


def _in_proj_kernel(x_ref, g_ref, sh_ref, sc_ref, w_ref, *refs, n_f32):
    *outs, wb_scr = refs

    @pl.when(pl.program_id(0) == 0)
    def _():
        wb_scr[...] = w_ref[...].astype(BF16)

    h = _norm_mod(x_ref[...], g_ref[...], sh_ref[0], sc_ref[0]).astype(BF16)
    y = _dot(h, wb_scr[...])
    if len(outs) == 1:
        outs[0][...] = y
    else:
        outs[0][...] = y[:, :n_f32]
        outs[1][...] = y[:, n_f32:].astype(BF16)


def in_proj(x, g, mod, tiles_per_row, w, n_f32=None, tm=512):
    t = x.shape[0]
    n = w.shape[1]
    if n_f32 is None:
        out_specs = pl.BlockSpec((tm, n), lambda i: (i, 0))
        out_shape = jax.ShapeDtypeStruct((t, n), F32)
    else:
        out_specs = [pl.BlockSpec((tm, n_f32), lambda i: (i, 0)), pl.BlockSpec((tm, n - n_f32), lambda i: (i, 0))]
        out_shape = [jax.ShapeDtypeStruct((t, n_f32), F32), jax.ShapeDtypeStruct((t, n - n_f32), BF16)]
    return pl.pallas_call(
        functools.partial(_in_proj_kernel, n_f32=n_f32),
        grid=(t // tm,),
        in_specs=[pl.BlockSpec((tm, D_MODEL), lambda i: (i, 0)),
                  pl.BlockSpec((1, D_MODEL), lambda i: (0, 0)),
                  _mod_spec(0, tiles_per_row),
                  _mod_spec(1, tiles_per_row),
                  pl.BlockSpec((D_MODEL, n), lambda i: (0, 0), pipeline_mode=pl.Buffered(1))],
        out_specs=out_specs,
        out_shape=out_shape,
        scratch_shapes=[pltpu.VMEM((D_MODEL, n), BF16)],
        compiler_params=_params("arbitrary"),
        name="in_proj",
    )(x, g.reshape(1, D_MODEL), mod, mod, w)


def _s5_scan_kernel(*refs, seq, chunks, has_init):
    if has_init:
        u_ref, bm_ref, cm_ref, lam_ref, h0_ref, y_ref, st_ref, bu_scr, hb_scr, ytm_scr = refs
    else:
        u_ref, bm_ref, cm_ref, lam_ref, y_ref, st_ref, bu_scr, hb_scr, ytm_scr = refs
        h0_ref = None
    tb = S5_TIME_BLOCK
    n_tb = seq // tb
    ns = S5_COL_STATE
    row = lax.broadcasted_iota(jnp.int32, (SUBLANES, ns), 0)
    piece = row % chunks

    for d in range(2):
        lam = lam_ref[d, 0]
        lr, li = lam[:, :ns], lam[:, ns:]
        bm = bm_ref[d, 0]
        cm = cm_ref[d, 0]
        blocks = list(range(n_tb)) if d == 0 else list(range(n_tb - 1, -1, -1))

        def load_bu(k):
            ub = u_ref[:, k * tb:(k + 1) * tb, :]
            utm = jnp.swapaxes(ub, 0, 1).reshape(tb * SUBLANES, S5_COL_CH).astype(BF16)
            bu_scr[...] = _dot(utm, bm).reshape(tb, SUBLANES, 2 * ns)

        def scan_block(h, store):
            def advance(t, hr, hi):
                b = bu_scr[t]
                return lr * hr - li * hi + b[:, :ns], lr * hi + li * hr + b[:, ns:]

            def step(s, carry):
                t_a = (tb - 1 - 2 * s) if d == 1 else 2 * s
                t_b = t_a - 1 if d == 1 else t_a + 1
                ar, ai = advance(t_a, *carry)
                br, bi = advance(t_b, ar, ai)
                if store:
                    first = jnp.concatenate([br, bi] if d == 1 else [ar, ai], axis=1)
                    second = jnp.concatenate([ar, ai] if d == 1 else [br, bi], axis=1)
                    t_lo = t_b if d == 1 else t_a
                    rows = pl.ds(pl.multiple_of(t_lo * SUBLANES, 2 * SUBLANES), 2 * SUBLANES)
                    hb_scr[rows, :] = jnp.concatenate([first, second], axis=0).astype(BF16)
                return br, bi
            return lax.fori_loop(0, tb // 2, step, h)

        zero = jnp.zeros((SUBLANES, ns), F32)
        if chunks > 1:
            h = (zero, zero)
            for k in blocks:
                load_bu(k)
                h = scan_block(h, False)
            fr, fi = h
            pr, pi = lr, li
            for _ in range(int(math.log2(seq))):
                pr, pi = pr * pr - pi * pi, 2.0 * pr * pi
            edge = 0 if d == 0 else chunks - 1
            shift = 1 if d == 0 else SUBLANES - 1
            if has_init:
                h0r = h0_ref[:, d, 0, :]
                h0i = h0_ref[:, d, 1, :]
                seq_of_row = row // chunks
                er, ei = zero, zero
                for b in range(SUBLANES // chunks):
                    er = jnp.where(seq_of_row == b, h0r[b:b + 1, :], er)
                    ei = jnp.where(seq_of_row == b, h0i[b:b + 1, :], ei)
            else:
                er, ei = zero, zero
            is_edge = piece == edge
            cr = jnp.where(is_edge, er, zero)
            ci = jnp.where(is_edge, ei, zero)
            for _ in range(chunks - 1):
                tr = fr + pr * cr - pi * ci
                ti = fi + pr * ci + pi * cr
                cr = jnp.where(is_edge, er, pltpu.roll(tr, shift, 0))
                ci = jnp.where(is_edge, ei, pltpu.roll(ti, shift, 0))
            h = (cr, ci)
        else:
            if has_init:
                h = (h0_ref[:, d, 0, :], h0_ref[:, d, 1, :])
            else:
                h = (zero, zero)

        for k in blocks:
            load_bu(k)
            h = scan_block(h, True)
            yb = _dot(hb_scr[...], cm).reshape(tb, SUBLANES, S5_COL_CH)
            if d == 0:
                ytm_scr[k * tb:(k + 1) * tb] = yb
            else:
                ytm_scr[k * tb:(k + 1) * tb] += yb
        st_ref[:, d, 0, :] = h[0]
        st_ref[:, d, 1, :] = h[1]

    y_ref[...] = jnp.swapaxes(ytm_scr[...], 0, 1)


def s5_scan(proj3, bmat, cmat, lam8, h0, chunks):
    rows, seq, _ = proj3.shape
    ns = S5_COL_STATE
    has_init = h0 is not None
    in_specs = [pl.BlockSpec((SUBLANES, seq, S5_COL_CH), lambda i, c: (i, 0, c)),
                pl.BlockSpec((2, 1, S5_COL_CH, 2 * ns), lambda i, c: (0, c, 0, 0)),
                pl.BlockSpec((2, 1, 2 * ns, S5_COL_CH), lambda i, c: (0, c, 0, 0)),
                pl.BlockSpec((2, 1, SUBLANES, 2 * ns), lambda i, c: (0, c, 0, 0))]
    args = [proj3, bmat, cmat, lam8]
    if has_init:
        nb = h0.shape[0]
        in_specs.append(pl.BlockSpec((nb, 2, 2, ns), lambda i, c: (0, 0, 0, c)))
        args.append(h0)
    y, st = pl.pallas_call(
        functools.partial(_s5_scan_kernel, seq=seq, chunks=chunks, has_init=has_init),
        grid=(rows // SUBLANES, S5_N_COL),
        in_specs=in_specs,
        out_specs=[pl.BlockSpec((SUBLANES, seq, S5_COL_CH), lambda i, c: (i, 0, c)),
                   pl.BlockSpec((SUBLANES, 2, 2, ns), lambda i, c: (i, 0, 0, c))],
        out_shape=[jax.ShapeDtypeStruct((rows, seq, HALF_MIX), F32),
                   jax.ShapeDtypeStruct((rows, 2, 2, S5_GROUPS * S5_STATE), F32)],
        scratch_shapes=[pltpu.VMEM((S5_TIME_BLOCK, SUBLANES, 2 * ns), F32),
                        pltpu.VMEM((S5_TIME_BLOCK * SUBLANES, 2 * ns), BF16),
                        pltpu.VMEM((seq, SUBLANES, S5_COL_CH), F32)],
        compiler_params=_params("arbitrary", "arbitrary"),
        name="s5_scan",
    )(*args)
    return y, st


def s5_params(lam_re, lam_im, log_dt, b_re, b_im, c_re, c_im):
    lr = lam_re.astype(F32)
    li = lam_im.astype(F32)
    dt = jnp.exp(log_dt.astype(F32))[..., None]
    mag = jnp.exp(lr * dt)
    bar_re = mag * jnp.cos(li * dt)
    bar_im = mag * jnp.sin(li * dt)
    den = lr * lr + li * li
    q_re = ((bar_re - 1.0) * lr + bar_im * li) / den
    q_im = (bar_im * lr - (bar_re - 1.0) * li) / den
    br = b_re.astype(F32)
    bi = b_im.astype(F32)
    b_bar_re = q_re[..., None] * br - q_im[..., None] * bi
    b_bar_im = q_re[..., None] * bi + q_im[..., None] * br
    eye = jnp.eye(S5_COL_GROUPS, dtype=F32)

    def block_diag_b(m):
        m = m.reshape(2, S5_N_COL, S5_COL_GROUPS, S5_STATE, S5_GROUP_CH)
        bd = jnp.einsum('dngpc,gh->dngchp', m, eye)
        return bd.reshape(2, S5_N_COL, S5_COL_CH, S5_COL_STATE)

    def block_diag_c(m):
        m = m.reshape(2, S5_N_COL, S5_COL_GROUPS, S5_GROUP_CH, S5_STATE)
        bd = jnp.einsum('dngcp,gh->dngphc', m, eye)
        return bd.reshape(2, S5_N_COL, S5_COL_STATE, S5_COL_CH)

    bmat = jnp.concatenate([block_diag_b(b_bar_re), block_diag_b(b_bar_im)], axis=-1).astype(BF16)
    cmat = jnp.concatenate([block_diag_c(c_re.astype(F32)), block_diag_c(-c_im.astype(F32))],
                           axis=-2).astype(BF16)
    lam_cat = jnp.concatenate([bar_re.reshape(2, S5_N_COL, S5_COL_STATE),
                               bar_im.reshape(2, S5_N_COL, S5_COL_STATE)], axis=-1)
    lam8 = jnp.broadcast_to(lam_cat[:, :, None, :], (2, S5_N_COL, SUBLANES, 2 * S5_COL_STATE))
    return bmat, cmat, lam8


def _na_ctx_kernel(q_ref, k_ref, v_ref, o_ref, ko_ref, vo_ref):
    seq = q_ref.shape[1]
    lane = lax.broadcasted_iota(jnp.int32, (seq, LANES), 1)
    low = lane < NA_HEAD_DIM
    outs = []
    for pr in range(NA_HEADS // 2):
        cols = slice(pr * LANES, (pr + 1) * LANES)
        qp = q_ref[0, :, cols] * (NA_HEAD_DIM ** -0.5)
        kp = k_ref[0, :, cols]
        vp = v_ref[0, :, cols]
        for half in range(2):
            sl = slice(half * NA_HEAD_DIM, (half + 1) * NA_HEAD_DIM)
            ko_ref[0, 0, 2 * pr + half] = kp[:, sl]
            vo_ref[0, 0, 2 * pr + half] = vp[:, sl]
        kb = kp.astype(BF16)
        vb = vp.astype(BF16)
        o_pair = None
        for half in range(2):
            qm = jnp.where(low if half == 0 else jnp.logical_not(low), qp, 0.0).astype(BF16)
            s = _dot_nt(qm, kb)
            p = jnp.exp(s - jnp.max(s, axis=-1, keepdims=True))
            o = _dot(p.astype(BF16), vb) * (1.0 / jnp.sum(p, axis=-1, keepdims=True))
            o_pair = o if half == 0 else jnp.where(low, o_pair, o)
        outs.append(o_pair)
    o_ref[0] = jnp.concatenate(outs, axis=-1).astype(o_ref.dtype)


def na_ctx(proj3):
    b, seq, _ = proj3.shape
    cache_shape = jax.ShapeDtypeStruct((b, 1, NA_HEADS, seq, NA_HEAD_DIM), F32)
    cache_spec = pl.BlockSpec((1, 1, NA_HEADS, seq, NA_HEAD_DIM), lambda i: (i, 0, 0, 0, 0))
    return pl.pallas_call(
        _na_ctx_kernel,
        grid=(b,),
        in_specs=[pl.BlockSpec((1, seq, HALF_MIX), lambda i: (i, 0, 1)),
                  pl.BlockSpec((1, seq, HALF_MIX), lambda i: (i, 0, 2)),
                  pl.BlockSpec((1, seq, HALF_MIX), lambda i: (i, 0, 3))],
        out_specs=[pl.BlockSpec((1, seq, HALF_MIX), lambda i: (i, 0, 0)), cache_spec, cache_spec],
        out_shape=[jax.ShapeDtypeStruct((b, seq, HALF_MIX), BF16), cache_shape, cache_shape],
        compiler_params=_params("arbitrary"),
        name="na_ctx",
    )(proj3, proj3, proj3)


NA_Q_ROWS = 4
NA_KEY_ROWS = 12


def na_bias_blocks(rpb):
    qcol = np.arange(GRID_W)
    cc = np.arange(GRID_W)
    cs = np.clip(qcol - NA_WIN_C // 2, 0, GRID_W - NA_WIN_C)
    valid = (cc[None, :] >= cs[:, None]) & (cc[None, :] < cs[:, None] + NA_WIN_C)
    coff = cc[None, :] - qcol[:, None] + (NA_WIN_C - 1)
    n_col = 2 * NA_WIN_C - 1
    sel = ((coff[None] == np.arange(n_col)[:, None, None]) & valid[None]).astype(np.float32)
    sel = sel.reshape(n_col, GRID_W * GRID_W)
    mask = np.where(valid, 0.0, NEG_BIG).astype(np.float32).reshape(1, GRID_W * GRID_W)
    n_row = 2 * NA_WIN_R - 1
    t1 = jnp.dot(rpb.astype(F32).reshape(NA_HEADS * n_row, n_col), jnp.asarray(sel),
                 precision=lax.Precision.HIGHEST) + jnp.asarray(mask)
    t1 = t1.reshape(NA_HEADS, n_row, GRID_W, GRID_W)
    return pl.pallas_call(
        _na_bias_kernel,
        grid=(NA_HEADS,),
        in_specs=[pl.BlockSpec((1, n_row, GRID_W, GRID_W), lambda h: (h, 0, 0, 0))],
        out_specs=pl.BlockSpec((3, 1, NA_Q_ROWS * GRID_W, NA_KEY_ROWS * GRID_W), lambda h: (0, h, 0, 0)),
        out_shape=jax.ShapeDtypeStruct((3, NA_HEADS, NA_Q_ROWS * GRID_W, NA_KEY_ROWS * GRID_W), F32),
        compiler_params=_params("arbitrary"),
        name="na_bias",
    )(t1)


def _na_bias_kernel(t1_ref, o_ref):
    variants = ((lambda ri: 0, NA_WIN_R - 1), (lambda ri: ri, NA_WIN_R // 2 - 1),
                (lambda ri: NA_KEY_ROWS - NA_WIN_R, -1))
    outside = jnp.full((GRID_W, GRID_W), NEG_BIG, F32)
    for v, (lo_of, shift) in enumerate(variants):
        for ri in range(NA_Q_ROWS):
            for wr in range(NA_KEY_ROWS):
                inside = lo_of(ri) <= wr < lo_of(ri) + NA_WIN_R
                o_ref[v, 0, ri * GRID_W:(ri + 1) * GRID_W, wr * GRID_W:(wr + 1) * GRID_W] = (
                    t1_ref[0, wr - ri + shift] if inside else outside)


def _na_lat_kernel(q_ref, k_ref, v_ref, kc_ref, vc_ref, bias_ref, o_ref):
    qb = pl.program_id(1)
    rows = k_ref.shape[1] // GRID_W
    nk = NA_KEY_ROWS * GRID_W
    first_row = jnp.clip(qb * NA_Q_ROWS - NA_WIN_R // 2, 0, rows - NA_KEY_ROWS)
    start = pl.multiple_of(first_row * GRID_W, GRID_W)
    tq = q_ref.shape[1]
    lane = lax.broadcasted_iota(jnp.int32, (tq, LANES), 1)
    low = lane < NA_HEAD_DIM
    outs = []
    for pr in range(NA_HEADS // 2):
        cols = slice(pr * LANES, (pr + 1) * LANES)
        qp = q_ref[0, :, cols].astype(F32) * (NA_HEAD_DIM ** -0.5)
        kw = k_ref[0, pl.ds(start, nk), cols]
        vw = v_ref[0, pl.ds(start, nk), cols]
        kc = kc_ref[0, :, cols]
        vc = vc_ref[0, :, cols]
        o_pair = None
        for half in range(2):
            qm = jnp.where(low if half == 0 else jnp.logical_not(low), qp, 0.0).astype(BF16)
            s_loc = _dot_nt(qm, kw) + bias_ref[0, 2 * pr + half]
            s_ctx = _dot_nt(qm, kc)
            m = jnp.maximum(jnp.max(s_loc, axis=-1, keepdims=True), jnp.max(s_ctx, axis=-1, keepdims=True))
            p_loc = jnp.exp(s_loc - m)
            p_ctx = jnp.exp(s_ctx - m)
            inv = 1.0 / (jnp.sum(p_loc, axis=-1, keepdims=True) + jnp.sum(p_ctx, axis=-1, keepdims=True))
            o = (_dot(p_loc.astype(BF16), vw) + _dot(p_ctx.astype(BF16), vc)) * inv
            o_pair = o if half == 0 else jnp.where(low, o_pair, o)
        outs.append(o_pair)
    o_ref[0] = jnp.concatenate(outs, axis=-1).astype(o_ref.dtype)


def na_lat(qkv3, k_ctx, v_ctx, bias):
    b, seq, _ = qkv3.shape
    tq = NA_Q_ROWS * GRID_W
    n_q = seq // tq
    lc = k_ctx.shape[1]
    ctx_spec = pl.BlockSpec((1, lc, HALF_MIX), lambda i, r: (i, 0, 0))
    return pl.pallas_call(
        _na_lat_kernel,
        grid=(b, n_q),
        in_specs=[pl.BlockSpec((1, tq, HALF_MIX), lambda i, r: (i, r, 0)),
                  pl.BlockSpec((1, seq, HALF_MIX), lambda i, r: (i, 0, 1)),
                  pl.BlockSpec((1, seq, HALF_MIX), lambda i, r: (i, 0, 2)),
                  ctx_spec, ctx_spec,
                  pl.BlockSpec((1, NA_HEADS, tq, NA_KEY_ROWS * GRID_W),
                               lambda i, r: (jnp.where(r == 0, 0, jnp.where(r == n_q - 1, 2, 1)), 0, 0, 0))],
        out_specs=pl.BlockSpec((1, tq, HALF_MIX), lambda i, r: (i, r, 0)),
        out_shape=jax.ShapeDtypeStruct((b, seq, HALF_MIX), BF16),
        compiler_params=_params("arbitrary", "arbitrary"),
        name="na_lat",
    )(qkv3, qkv3, qkv3, k_ctx, v_ctx, bias)


def _softmax_pair_diff(s1, s2, lam):
    p1 = jnp.exp(s1 - jnp.max(s1, axis=-1, keepdims=True))
    p2 = jnp.exp(s2 - jnp.max(s2, axis=-1, keepdims=True))
    inv1 = 1.0 / jnp.sum(p1, axis=-1, keepdims=True)
    inv2 = lam / jnp.sum(p2, axis=-1, keepdims=True)
    return p1 * inv1 - p2 * inv2


def _sub_ln(o, g, lam_init):
    return o * lax.rsqrt(jnp.mean(o * o, axis=-1, keepdims=True) + EPS) * g * (1.0 - lam_init)


def _diff_ctx_kernel(lam_ref, q_ref, k_ref, v_ref, g_ref, o_ref, ko_ref, vo_ref, *, lam_init):
    scale = DIFF_D ** -0.5
    lam = lam_ref[0, 0]
    lane = lax.broadcasted_iota(jnp.int32, (q_ref.shape[1], DIFF_HEAD_DIM), 1)
    first = lane < DIFF_D
    for h in range(DIFF_HEADS):
        sl = slice(h * DIFF_HEAD_DIM, (h + 1) * DIFF_HEAD_DIM)
        qh = q_ref[0, :, sl]
        kh = k_ref[0, :, sl]
        vh = v_ref[0, :, sl]
        ko_ref[0, 0, h] = kh
        vo_ref[0, 0, h] = vh
        kb = kh.astype(BF16)
        s1 = _dot_nt(jnp.where(first, qh, 0.0).astype(BF16), kb) * scale
        s2 = _dot_nt(jnp.where(first, 0.0, qh).astype(BF16), kb) * scale
        a = _softmax_pair_diff(s1, s2, lam)
        o = _dot(a.astype(BF16), vh.astype(BF16))
        o_ref[0, :, sl] = _sub_ln(o, g_ref[...], lam_init).astype(o_ref.dtype)


def diff_ctx(proj3, lam, subln_g, lam_init):
    b, seq, _ = proj3.shape
    cache_shape = jax.ShapeDtypeStruct((b, 1, DIFF_HEADS, seq, DIFF_HEAD_DIM), F32)
    cache_spec = pl.BlockSpec((1, 1, DIFF_HEADS, seq, DIFF_HEAD_DIM), lambda i: (i, 0, 0, 0, 0))
    return pl.pallas_call(
        functools.partial(_diff_ctx_kernel, lam_init=lam_init),
        grid=(b,),
        in_specs=[pl.BlockSpec(memory_space=pltpu.SMEM),
                  pl.BlockSpec((1, seq, HALF_MIX), lambda i: (i, 0, 0)),
                  pl.BlockSpec((1, seq, HALF_MIX), lambda i: (i, 0, 1)),
                  pl.BlockSpec((1, seq, HALF_MIX), lambda i: (i, 0, 2)),
                  pl.BlockSpec((1, DIFF_HEAD_DIM), lambda i: (0, 0))],
        out_specs=[pl.BlockSpec((1, seq, HALF_MIX), lambda i: (i, 0, 0)), cache_spec, cache_spec],
        out_shape=[jax.ShapeDtypeStruct((b, seq, HALF_MIX), BF16), cache_shape, cache_shape],
        compiler_params=_params("arbitrary"),
        name="diff_ctx",
    )(lam.reshape(1, 1), proj3, proj3, proj3, subln_g.reshape(1, DIFF_HEAD_DIM))


def rope_tables(seq):
    t = np.arange(seq)
    row = (t // GRID_W).astype(np.float32)
    col = (t % GRID_W).astype(np.float32)
    n_freq = DIFF_D // 4
    inv = np.float32(ROPE_BASE) ** (-np.arange(n_freq, dtype=np.float32) / np.float32(n_freq))
    ang = np.concatenate([row[:, None] * inv, col[:, None] * inv], axis=-1)
    cos = np.repeat(np.cos(ang), 2, axis=-1)
    sin = np.repeat(np.sin(ang), 2, axis=-1)
    sign = np.where(np.arange(DIFF_D) % 2 == 0, -1.0, 1.0).astype(np.float32)
    sin = sin * sign
    return (jnp.asarray(np.tile(cos, (1, 2)).astype(np.float32)),
            jnp.asarray(np.tile(sin, (1, 2)).astype(np.float32)))


def _rope(x, cos, sin_signed):
    lane = lax.broadcasted_iota(jnp.int32, x.shape, 1)
    nxt = pltpu.roll(x, x.shape[1] - 1, 1)
    prv = pltpu.roll(x, 1, 1)
    partner = jnp.where(lane % 2 == 0, nxt, prv)
    return x * cos + partner * sin_signed


def _diff_lat_kernel(lam_ref, q_ref, k_ref, v_ref, kc_ref, vc_ref, cq_ref, sq_ref, ck_ref, sk_ref,
                     g_ref, o_ref, k_all, v_all, *, lam_init):
    seq = k_ref.shape[1]

    @pl.when(pl.program_id(2) == 0)
    def _():
        k_all[0:seq, :] = _rope(k_ref[0], ck_ref[...], sk_ref[...]).astype(BF16)
        k_all[seq:, :] = kc_ref[0, 0, 0].astype(BF16)
        v_all[0:seq, :] = v_ref[0].astype(BF16)
        v_all[seq:, :] = vc_ref[0, 0, 0].astype(BF16)

    lam = lam_ref[0, 0]
    q = _rope(q_ref[0], cq_ref[...], sq_ref[...]) * (DIFF_D ** -0.5)
    lane = lax.broadcasted_iota(jnp.int32, q.shape, 1)
    first = lane < DIFF_D
    kb = k_all[...]
    s1 = _dot_nt(jnp.where(first, q, 0.0).astype(BF16), kb)
    s2 = _dot_nt(jnp.where(first, 0.0, q).astype(BF16), kb)
    a = _softmax_pair_diff(s1, s2, lam)
    o = _dot(a.astype(BF16), v_all[...])
    o_ref[0] = _sub_ln(o, g_ref[...], lam_init).astype(o_ref.dtype)


def diff_lat(proj3, k_ctx, v_ctx, cos, sin, lam, subln_g, lam_init, tq=256):
    b, seq, _ = proj3.shape
    lc = k_ctx.shape[3]
    hd = DIFF_HEAD_DIM
    ctx_spec = pl.BlockSpec((1, 1, 1, lc, hd), lambda i, h, q: (i, 0, h, 0, 0))
    tq_spec = pl.BlockSpec((tq, hd), lambda i, h, q: (q, 0))
    full_spec = pl.BlockSpec((seq, hd), lambda i, h, q: (0, 0))
    return pl.pallas_call(
        functools.partial(_diff_lat_kernel, lam_init=lam_init),
        grid=(b, DIFF_HEADS, seq // tq),
        in_specs=[pl.BlockSpec(memory_space=pltpu.SMEM),
                  pl.BlockSpec((1, tq, hd), lambda i, h, q: (i, q, h)),
                  pl.BlockSpec((1, seq, hd), lambda i, h, q: (i, 0, DIFF_HEADS + h)),
                  pl.BlockSpec((1, seq, hd), lambda i, h, q: (i, 0, 2 * DIFF_HEADS + h)),
                  ctx_spec, ctx_spec, tq_spec, tq_spec, full_spec, full_spec,
                  pl.BlockSpec((1, hd), lambda i, h, q: (0, 0))],
        out_specs=pl.BlockSpec((1, tq, hd), lambda i, h, q: (i, q, h)),
        out_shape=jax.ShapeDtypeStruct((b, seq, HALF_MIX), BF16),
        scratch_shapes=[pltpu.VMEM((seq + lc, hd), BF16), pltpu.VMEM((seq + lc, hd), BF16)],
        compiler_params=_params("arbitrary", "arbitrary", "arbitrary"),
        name="diff_lat",
    )(lam.reshape(1, 1), proj3, proj3, proj3, k_ctx, v_ctx, cos, sin, cos, sin,
      subln_g.reshape(1, hd))


CONV_PAD = 16
CONV_SUB = 64


def _conv_kernel(a_ref, g_ref, ap_ref, gp_ref, an_ref, gn_ref, w_ref, b_ref, lg_ref, lb_ref, o_ref, xp_scr,
                 xsh_scr):
    t = pl.program_id(1)
    tt = a_ref.shape[1]
    prev = ap_ref[0] * jax.nn.sigmoid(gp_ref[0])
    nxt = an_ref[0] * jax.nn.sigmoid(gn_ref[0])
    xp_scr[0:CONV_PAD, :] = jnp.where(t > 0, prev, 0.0)
    xp_scr[CONV_PAD + tt:, :] = jnp.where(t < pl.num_programs(1) - 1, nxt, 0.0)
    xp_scr[CONV_PAD:CONV_PAD + tt, :] = a_ref[0] * jax.nn.sigmoid(g_ref[0])
    first_tap = CONV_PAD - CONV_WIDTH // 2
    n_rows = xsh_scr.shape[1]
    for b in range(SUBLANES):
        xsh_scr[b] = xp_scr[b:b + n_rows, :]
    for i in range(tt // CONV_SUB):
        s = i * CONV_SUB
        acc = jnp.zeros((CONV_SUB, HALF_MIX), F32)
        for j in range(CONV_WIDTH):
            whole, phase = divmod(first_tap + j, SUBLANES)
            lo = s + whole * SUBLANES
            acc = acc + xsh_scr[phase, lo:lo + CONV_SUB, :] * w_ref[j:j + 1, :]
        y = acc + b_ref[...]
        mu = jnp.mean(y, axis=-1, keepdims=True)
        yc = y - mu
        var = jnp.mean(yc * yc, axis=-1, keepdims=True)
        yn = yc * lax.rsqrt(var + EPS) * lg_ref[...] + lb_ref[...]
        o_ref[0, s:s + CONV_SUB, :] = _silu(yn).astype(o_ref.dtype)


def conformer_conv(proj3, w, b, ln_g, ln_b, tt):
    bsz, seq, _ = proj3.shape
    n_t = seq // tt
    hb = tt // CONV_PAD
    last = seq // CONV_PAD - 1
    vec = pl.BlockSpec((1, HALF_MIX), lambda i, t: (0, 0))

    def main(col):
        return pl.BlockSpec((1, tt, HALF_MIX), lambda i, t: (i, t, col))

    def prev(col):
        return pl.BlockSpec((1, CONV_PAD, HALF_MIX), lambda i, t: (i, jnp.maximum(t * hb - 1, 0), col))

    def nxt(col):
        return pl.BlockSpec((1, CONV_PAD, HALF_MIX), lambda i, t: (i, jnp.minimum((t + 1) * hb, last), col))

    return pl.pallas_call(
        _conv_kernel,
        grid=(bsz, n_t),
        in_specs=[main(3), main(4), prev(3), prev(4), nxt(3), nxt(4),
                  pl.BlockSpec((CONV_WIDTH, HALF_MIX), lambda i, t: (0, 0)),
                  vec, vec, vec],
        out_specs=pl.BlockSpec((1, tt, HALF_MIX), lambda i, t: (i, t, 0)),
        out_shape=jax.ShapeDtypeStruct((bsz, seq, HALF_MIX), BF16),
        scratch_shapes=[pltpu.VMEM((tt + 2 * CONV_PAD, HALF_MIX), F32),
                        pltpu.VMEM((SUBLANES, tt + 2 * CONV_PAD - SUBLANES, HALF_MIX), F32)],
        compiler_params=_params("arbitrary", "arbitrary"),
        name="conformer_conv",
    )(proj3, proj3, proj3, proj3, proj3, proj3, w, b.reshape(1, HALF_MIX), ln_g.reshape(1, HALF_MIX),
      ln_b.reshape(1, HALF_MIX))


FFN_CHUNK = 1024


def _mix_ffn_kernel(x_ref, y_ref, u_ref, d_ref, wglu_ref, bglu_ref, m2_ref, w1_ref, w2_ref, g1_ref,
                    g_ref, sh_ref, sc_ref, g2_ref, wg_ref, wu_ref, wd_ref, o_ref):
    z = jax.nn.gelu(u_ref[...] * d_ref[...] + y_ref[...])
    s5_out = z * jax.nn.sigmoid(_dot(z.astype(BF16), wglu_ref[...]) + bglu_ref[...])
    mix = _dot(s5_out.astype(BF16), w1_ref[...]) + _dot(m2_ref[...], w2_ref[...])
    x1 = x_ref[...] + g1_ref[0] * mix
    h = _norm_mod(x1, g_ref[...], sh_ref[0], sc_ref[0]).astype(BF16)
    fdim = wg_ref.shape[1]
    acc = None
    for lo in range(0, fdim, FFN_CHUNK):
        hi = min(lo + FFN_CHUNK, fdim)
        a = _dot(h, wg_ref[:, lo:hi])
        up = _dot(h, wu_ref[:, lo:hi])
        part = _dot((_silu(a) * up).astype(BF16), wd_ref[lo:hi, :])
        acc = part if acc is None else acc + part
    o_ref[...] = x1 + g2_ref[0] * acc


def mix_ffn(x, y, u, d_skip, w_glu, b_glu, m2, w_out, g, mod, tiles_per_row, w_gate, w_up, w_down, tm=512):
    t = x.shape[0]
    fdim = w_gate.shape[1]
    half = pl.BlockSpec((tm, HALF_MIX), lambda i: (i, 0))
    vec = pl.BlockSpec((1, HALF_MIX), lambda i: (0, 0))

    def resident(shape, index):
        return pl.BlockSpec(shape, index, pipeline_mode=pl.Buffered(1))

    return pl.pallas_call(
        _mix_ffn_kernel,
        grid=(t // tm,),
        in_specs=[pl.BlockSpec((tm, D_MODEL), lambda i: (i, 0)), half, half, vec,
                  resident((HALF_MIX, HALF_MIX), lambda i: (0, 0)), vec, half,
                  resident((HALF_MIX, D_MODEL), lambda i: (0, 0)),
                  resident((HALF_MIX, D_MODEL), lambda i: (1, 0)),
                  _mod_spec(2, tiles_per_row),
                  pl.BlockSpec((1, D_MODEL), lambda i: (0, 0)),
                  _mod_spec(3, tiles_per_row), _mod_spec(4, tiles_per_row), _mod_spec(5, tiles_per_row),
                  resident((D_MODEL, fdim), lambda i: (0, 0)),
                  resident((D_MODEL, fdim), lambda i: (0, 0)),
                  resident((fdim, D_MODEL), lambda i: (0, 0))],
        out_specs=pl.BlockSpec((tm, D_MODEL), lambda i: (i, 0)),
        out_shape=jax.ShapeDtypeStruct((t, D_MODEL), F32),
        compiler_params=_params("arbitrary"),
        name="mix_ffn",
    )(x, y, u, d_skip.reshape(1, HALF_MIX), w_glu, b_glu.reshape(1, HALF_MIX), m2, w_out, w_out, mod,
      g.reshape(1, D_MODEL), mod, mod, mod, w_gate, w_up, w_down)


MOE_ROW_TILE = 1024
MOE_ZERO_ROWS = 256
MOE_ROW_PARTS = 8
INFO_E0, INFO_E1, INFO_G0, INFO_G1, INFO_R0, INFO_R1 = range(6)


def _stream_mod_spec(chunk, prompt_tiles, tiles_per_req, ctx_row):
    def index(i, *_):
        return (jnp.where(i < prompt_tiles, ctx_row, (i - prompt_tiles) // tiles_per_req), 0, chunk)
    return pl.BlockSpec((1, 1, D_MODEL), index)


def _mix_route_kernel(xp_ref, m1p_ref, m2p_ref, xs_ref, m1s_ref, m2s_ref, w1_ref, w2_ref, gate_ref,
                      g_ref, sh_ref, sc_ref, rwh_ref, rwl_ref, o_ref, info_ref, fields_ref, cnt_ref,
                      tri_scr, run_scr, *, prompt_tiles):
    i = pl.program_id(0)
    tm = o_ref.shape[0]
    w1 = w1_ref[...]
    w2 = w2_ref[...]

    @pl.when(i < prompt_tiles)
    def _():
        o_ref[...] = xp_ref[...] + gate_ref[0] * (_dot(m1p_ref[...], w1) + _dot(m2p_ref[...], w2))

    @pl.when(i >= prompt_tiles)
    def _():
        o_ref[...] = xs_ref[...] + gate_ref[0] * (_dot(m1s_ref[...], w1) + _dot(m2s_ref[...], w2))

    @pl.when(i == 0)
    def _():
        r = lax.broadcasted_iota(jnp.int32, (tm, tm), 0)
        c = lax.broadcasted_iota(jnp.int32, (tm, tm), 1)
        tri_scr[...] = jnp.where(c < r, 1.0, 0.0).astype(BF16)
        run_scr[...] = jnp.zeros_like(run_scr)

    h = _norm_mod(o_ref[...], g_ref[...], sh_ref[0], sc_ref[0])
    h_hi = h.astype(BF16)
    h_lo = (h - h_hi.astype(F32)).astype(BF16)
    logits = _dot(h_hi, rwh_ref[...]) + (_dot(h_hi, rwl_ref[...]) + _dot(h_lo, rwh_ref[...]))
    lane = lax.broadcasted_iota(jnp.int32, logits.shape, 1).astype(F32)
    logits = jnp.where(lane < N_EXPERTS, logits, -jnp.inf)
    m1 = jnp.max(logits, axis=-1, keepdims=True)
    i1 = jnp.min(jnp.where(logits == m1, lane, float(LANES)), axis=-1, keepdims=True)
    rest = jnp.where(lane == i1, -jnp.inf, logits)
    m2 = jnp.max(rest, axis=-1, keepdims=True)
    i2 = jnp.min(jnp.where(rest == m2, lane, float(LANES)), axis=-1, keepdims=True)
    e2 = jnp.exp(m2 - m1)
    den = 1.0 + e2
    hit = jnp.where(lane == i1, 1.0, 0.0) + jnp.where(lane == i2, 1.0, 0.0)
    before = _dot(tri_scr[...], hit.astype(BF16)) + run_scr[0:1, :]
    r1 = jnp.sum(jnp.where(lane == i1, before, 0.0), axis=-1, keepdims=True)
    r2 = jnp.sum(jnp.where(lane == i2, before, 0.0), axis=-1, keepdims=True)
    info = jnp.zeros_like(logits)
    for slot, val in ((INFO_E0, i1), (INFO_E1, i2), (INFO_G0, 1.0 / den), (INFO_G1, e2 / den),
                      (INFO_R0, r1), (INFO_R1, r2)):
        info = jnp.where(lane == float(slot), val, info)
    info_ref[...] = info
    fields_ref[...] = jnp.transpose(info)[0:SUBLANES, :]
    run_scr[...] = run_scr[...] + jnp.sum(hit, axis=0, keepdims=True)
    cnt_ref[...] = run_scr[...]


def mix_route(xp, m1p, m2p, xs, m1s, m2s, w_out, g, mod_all, ctx_row, req_tokens, router_w, tm=512):
    tp, ts = xp.shape[0], xs.shape[0]
    t = tp + ts
    pt = tp // tm
    rw = jnp.pad(router_w.astype(F32), ((0, 0), (0, LANES - N_EXPERTS)))
    rw_hi = rw.astype(BF16)
    rw_lo = (rw - rw_hi.astype(F32)).astype(BF16)

    def p_spec(width):
        return pl.BlockSpec((tm, width), lambda i: (jnp.minimum(i, pt - 1), 0))

    def s_spec(width):
        return pl.BlockSpec((tm, width), lambda i: (jnp.maximum(i - pt, 0), 0))

    def mod_spec(chunk):
        return _stream_mod_spec(chunk, pt, req_tokens // tm, ctx_row)

    rw_spec = pl.BlockSpec((D_MODEL, LANES), lambda i: (0, 0))
    return pl.pallas_call(
        functools.partial(_mix_route_kernel, prompt_tiles=pt),
        grid=(t // tm,),
        in_specs=[p_spec(D_MODEL), p_spec(HALF_MIX), p_spec(HALF_MIX),
                  s_spec(D_MODEL), s_spec(HALF_MIX), s_spec(HALF_MIX),
                  pl.BlockSpec((HALF_MIX, D_MODEL), lambda i: (0, 0)),
                  pl.BlockSpec((HALF_MIX, D_MODEL), lambda i: (1, 0)),
                  mod_spec(2), pl.BlockSpec((1, D_MODEL), lambda i: (0, 0)), mod_spec(3), mod_spec(4),
                  rw_spec, rw_spec],
        out_specs=[pl.BlockSpec((tm, D_MODEL), lambda i: (i, 0)),
                   pl.BlockSpec((tm, LANES), lambda i: (i, 0)),
                   pl.BlockSpec((SUBLANES, tm), lambda i: (0, i)),
                   pl.BlockSpec((SUBLANES, LANES), lambda i: (0, 0))],
        out_shape=[jax.ShapeDtypeStruct((t, D_MODEL), F32), jax.ShapeDtypeStruct((t, LANES), F32),
                   jax.ShapeDtypeStruct((SUBLANES, t), F32), jax.ShapeDtypeStruct((SUBLANES, LANES), F32)],
        scratch_shapes=[pltpu.VMEM((tm, tm), BF16), pltpu.VMEM((SUBLANES, LANES), F32)],
        compiler_params=_params("arbitrary"),
        name="mix_route",
    )(xp, m1p, m2p, xs, m1s, m2s, w_out, w_out, mod_all, g.reshape(1, D_MODEL), mod_all, mod_all,
      rw_hi, rw_lo)


def _dispatch_kernel(d0_ref, d1_ref, fill_ref, x_ref, g_ref, sh_ref, sc_ref, xs_hbm, h_scr, zero_scr, sem, zsem):
    i = pl.program_id(0)
    n = pl.num_programs(0)
    tm = x_ref.shape[0]
    slot = i % 2

    def row_copy(r, dst, s):
        return pltpu.make_async_copy(h_scr.at[s, pl.ds(r, 1), :], xs_hbm.at[pl.ds(dst, 1), :], sem.at[s])

    def wait_rows(s):
        for _ in range(2):
            pltpu.make_async_copy(h_scr.at[s], xs_hbm.at[pl.ds(0, tm), :], sem.at[s]).wait()

    @pl.when(i >= 2)
    def _():
        wait_rows(slot)

    h_scr[slot] = _norm_mod(x_ref[...], g_ref[...], sh_ref[0], sc_ref[0])
    base = i * tm

    def body(r8, c):
        rb = pl.multiple_of(r8 * SUBLANES, SUBLANES)
        for k in range(SUBLANES):
            row_copy(rb + k, d0_ref[base + rb + k], slot).start()
            row_copy(rb + k, d1_ref[base + rb + k], slot).start()
        return c

    lax.fori_loop(0, tm // SUBLANES, body, 0)

    @pl.when(i == n - 1)
    def _():
        zero_scr[...] = jnp.zeros_like(zero_scr)

        def zero_row(r):
            return pltpu.make_async_copy(zero_scr.at[pl.ds(0, 1), :], xs_hbm.at[pl.ds(r, 1), :], zsem)

        def zero_block(b):
            start = pl.multiple_of(b * MOE_ZERO_ROWS, MOE_ZERO_ROWS)
            return pltpu.make_async_copy(zero_scr, xs_hbm.at[pl.ds(start, MOE_ZERO_ROWS), :], zsem)

        def start_all(copy):
            def body(r, c):
                copy(r).start()
                return c
            return body

        def wait_all(copy):
            def body(r, c):
                copy(r).wait()
                return c
            return body

        def zero_group(b):
            start = pl.multiple_of(b * SUBLANES, SUBLANES)
            return pltpu.make_async_copy(zero_scr.at[pl.ds(0, SUBLANES), :],
                                         xs_hbm.at[pl.ds(start, SUBLANES), :], zsem)

        for e in range(N_EXPERTS):
            lo = fill_ref[e]
            hi = fill_ref[N_EXPERTS + e]
            lo_group = (lo + SUBLANES - 1) // SUBLANES
            lax.fori_loop(lo, lo_group * SUBLANES, start_all(zero_row), 0)
            lax.fori_loop(lo_group, hi // SUBLANES, start_all(zero_group), 0)
            lax.fori_loop(lo, lo_group * SUBLANES, wait_all(zero_row), 0)
            lax.fori_loop(lo_group, hi // SUBLANES, wait_all(zero_group), 0)
        blocks_per_tile = MOE_ROW_TILE // MOE_ZERO_ROWS
        first = fill_ref[2 * N_EXPERTS] * blocks_per_tile
        last = (xs_hbm.shape[0] // MOE_ROW_TILE) * blocks_per_tile
        lax.fori_loop(first, last, start_all(zero_block), 0)
        lax.fori_loop(first, last, wait_all(zero_block), 0)
        wait_rows(slot)

        @pl.when(n >= 2)
        def _():
            wait_rows(1 - slot)


def moe_dispatch(x, g, mod_all, ctx_row, prompt_tokens, req_tokens, dest0, dest1, fill, n_tiles, tm=512):
    t = x.shape[0]
    pt = prompt_tokens // tm
    grid_spec = pltpu.PrefetchScalarGridSpec(
        num_scalar_prefetch=3,
        grid=(t // tm,),
        in_specs=[pl.BlockSpec((tm, D_MODEL), lambda i, *_: (i, 0)),
                  pl.BlockSpec((1, D_MODEL), lambda i, *_: (0, 0)),
                  _stream_mod_spec(3, pt, req_tokens // tm, ctx_row),
                  _stream_mod_spec(4, pt, req_tokens // tm, ctx_row)],
        out_specs=pl.BlockSpec(memory_space=pl.ANY),
        scratch_shapes=[pltpu.VMEM((2, tm, D_MODEL), F32), pltpu.VMEM((MOE_ZERO_ROWS, D_MODEL), F32),
                        pltpu.SemaphoreType.DMA((2,)), pltpu.SemaphoreType.DMA(())],
    )
    return pl.pallas_call(
        _dispatch_kernel,
        grid_spec=grid_spec,
        out_shape=jax.ShapeDtypeStruct((n_tiles * MOE_ROW_TILE, D_MODEL), F32),
        compiler_params=pltpu.CompilerParams(dimension_semantics=("arbitrary",),
                                             vmem_limit_bytes=VMEM_LIMIT_BYTES,
                                             disable_bounds_checks=True),
        name="moe_dispatch",
    )(dest0, dest1, fill, x, g.reshape(1, D_MODEL), mod_all, mod_all)


def _experts_kernel(te_ref, tv_ref, x_ref, wg_ref, wu_ref, wd_ref, o_ref, h_scr, acc_scr):
    i = pl.program_id(0)
    f = pl.program_id(1)
    last_f = pl.num_programs(1) - 1

    n_valid = tv_ref[i]
    tile_rows = h_scr.shape[0]
    step = tile_rows // MOE_ROW_PARTS

    def swiglu_rows(n_rows):
        rows = slice(0, n_rows)

        @pl.when(f == 0)
        def _():
            h_scr[rows, :] = x_ref[rows, :].astype(BF16)
            acc_scr[rows, :] = jnp.zeros((n_rows, D_MODEL), F32)

        h = h_scr[rows, :]
        a = _dot(h, wg_ref[0].astype(BF16))
        u = _dot(h, wu_ref[0].astype(BF16))
        acc_scr[rows, :] += _dot((_silu(a) * u).astype(BF16), wd_ref[0].astype(BF16))

        @pl.when(f == last_f)
        def _():
            o_ref[rows, :] = acc_scr[rows, :]

    for part in range(1, MOE_ROW_PARTS + 1):
        n_rows = part * step

        @pl.when((n_valid > n_rows - step) & (n_valid <= n_rows))
        def _(n_rows=n_rows):
            swiglu_rows(n_rows)
            if n_rows < tile_rows:
                @pl.when(f == last_f)
                def _():
                    o_ref[n_rows:, :] = jnp.zeros((tile_rows - n_rows, D_MODEL), F32)

    @pl.when((n_valid == 0) & (f == last_f))
    def _():
        o_ref[...] = jnp.zeros_like(o_ref)


def moe_experts(xs_sorted, tile_expert, tile_valid, w_gate, w_up, w_down, tf=512):
    rows = xs_sorted.shape[0]
    fdim = w_gate.shape[2]
    n_f = fdim // tf
    tr = MOE_ROW_TILE

    def f_eff(i, f, tv):
        return jnp.where(tv[i] > 0, f, n_f - 1)

    grid_spec = pltpu.PrefetchScalarGridSpec(
        num_scalar_prefetch=2,
        grid=(rows // tr, n_f),
        in_specs=[pl.BlockSpec((tr, D_MODEL), lambda i, f, te, tv: (i, 0)),
                  pl.BlockSpec((1, D_MODEL, tf), lambda i, f, te, tv: (te[i], 0, f_eff(i, f, tv))),
                  pl.BlockSpec((1, D_MODEL, tf), lambda i, f, te, tv: (te[i], 0, f_eff(i, f, tv))),
                  pl.BlockSpec((1, tf, D_MODEL), lambda i, f, te, tv: (te[i], f_eff(i, f, tv), 0))],
        out_specs=pl.BlockSpec((tr, D_MODEL), lambda i, f, te, tv: (i, 0)),
        scratch_shapes=[pltpu.VMEM((tr, D_MODEL), BF16), pltpu.VMEM((tr, D_MODEL), F32)],
    )
    return pl.pallas_call(
        _experts_kernel,
        grid_spec=grid_spec,
        out_shape=jax.ShapeDtypeStruct((rows, D_MODEL), F32),
        compiler_params=_params("arbitrary", "arbitrary"),
        name="moe_experts",
    )(tile_expert, tile_valid, xs_sorted, w_gate, w_up, w_down)


def _combine_kernel(d0_ref, d1_ref, x_ref, info_ref, gate_ref, fg_ref, ys_hbm, op_ref, os_ref, rbuf, sem,
                    *, prompt_tiles):
    i = pl.program_id(0)
    n = pl.num_programs(0)
    tm = x_ref.shape[0]
    slot = i % 2

    def issue(tile, s, unroll):
        base = tile * tm

        def body(r8, c):
            rb = pl.multiple_of(r8 * SUBLANES, SUBLANES)
            for k in range(SUBLANES):
                pltpu.make_async_copy(ys_hbm.at[pl.ds(d0_ref[base + rb + k], 1), :],
                                      rbuf.at[s, 0, pl.ds(rb + k, 1), :], sem.at[s]).start()
                pltpu.make_async_copy(ys_hbm.at[pl.ds(d1_ref[base + rb + k], 1), :],
                                      rbuf.at[s, 1, pl.ds(rb + k, 1), :], sem.at[s]).start()
            return c

        lax.fori_loop(0, tm // SUBLANES, body, 0, unroll=unroll)

    def wait_rows(s):
        for k in range(2):
            pltpu.make_async_copy(ys_hbm.at[pl.ds(0, tm), :], rbuf.at[s, k], sem.at[s]).wait()

    @pl.when(i == 0)
    def _():
        issue(0, 0, False)

    wait_rows(slot)
    issue(jnp.where(i + 1 < n, i + 1, 0), 1 - slot, True)

    info = info_ref[...]
    moe = info[:, INFO_G0:INFO_G0 + 1] * rbuf[slot, 0] + info[:, INFO_G1:INFO_G1 + 1] * rbuf[slot, 1]
    y = x_ref[...] + gate_ref[0] * moe
    out = y * lax.rsqrt(jnp.mean(y * y, axis=-1, keepdims=True) + EPS) * fg_ref[...]

    @pl.when(i < prompt_tiles)
    def _():
        op_ref[...] = out

    @pl.when(i >= prompt_tiles)
    def _():
        os_ref[...] = out

    @pl.when(i == n - 1)
    def _():
        wait_rows(1 - slot)


def moe_combine(x, info, ys_sorted, dest0, dest1, mod_all, ctx_row, prompt_tokens, req_tokens, final_g, tm=512):
    t = x.shape[0]
    pt = prompt_tokens // tm
    grid_spec = pltpu.PrefetchScalarGridSpec(
        num_scalar_prefetch=2,
        grid=(t // tm,),
        in_specs=[pl.BlockSpec((tm, D_MODEL), lambda i, *_: (i, 0)),
                  pl.BlockSpec((tm, LANES), lambda i, *_: (i, 0)),
                  _stream_mod_spec(5, pt, req_tokens // tm, ctx_row),
                  pl.BlockSpec((1, D_MODEL), lambda i, *_: (0, 0)),
                  pl.BlockSpec(memory_space=pl.ANY)],
        out_specs=[pl.BlockSpec((tm, D_MODEL), lambda i, *_: (jnp.minimum(i, pt - 1), 0)),
                   pl.BlockSpec((tm, D_MODEL), lambda i, *_: (jnp.maximum(i - pt, 0), 0))],
        scratch_shapes=[pltpu.VMEM((2, 2, tm, D_MODEL), F32), pltpu.SemaphoreType.DMA((2,))],
    )
    return pl.pallas_call(
        functools.partial(_combine_kernel, prompt_tiles=pt),
        grid_spec=grid_spec,
        out_shape=[jax.ShapeDtypeStruct((prompt_tokens, D_MODEL), F32),
                   jax.ShapeDtypeStruct((t - prompt_tokens, D_MODEL), F32)],
        compiler_params=pltpu.CompilerParams(dimension_semantics=("arbitrary",),
                                             vmem_limit_bytes=VMEM_LIMIT_BYTES,
                                             disable_bounds_checks=True),
        name="moe_combine",
    )(dest0, dest1, x, info, mod_all, final_g.reshape(1, D_MODEL), ys_sorted)


def moe_layout(counts, fields, n_tiles):
    nt_e = (counts + MOE_ROW_TILE - 1) // MOE_ROW_TILE
    ends = jnp.cumsum(nt_e)
    total = ends[-1]
    offset = (ends - nt_e) * MOE_ROW_TILE
    experts = jnp.arange(N_EXPERTS, dtype=jnp.int32)

    def dest(e_lane, r_lane):
        e = fields[e_lane].astype(jnp.int32)
        off = jnp.sum(jnp.where(e[:, None] == experts[None, :], offset[None, :], 0), axis=1)
        return (off + fields[r_lane].astype(jnp.int32)).astype(jnp.int32)

    fill = jnp.concatenate([offset + counts, ends * MOE_ROW_TILE, total[None]]).astype(jnp.int32)
    ids = jnp.arange(n_tiles, dtype=jnp.int32)
    ids_c = jnp.minimum(ids, total - 1)
    te = jnp.sum((ids_c[:, None] >= ends[None, :]).astype(jnp.int32), axis=1)
    mine = te[:, None] == experts[None, :]
    first_tile = jnp.sum(jnp.where(mine, (ends - nt_e)[None, :], 0), axis=1)
    count = jnp.sum(jnp.where(mine, counts[None, :], 0), axis=1)
    rows_left = jnp.clip(count - (ids - first_tile) * MOE_ROW_TILE, 0, MOE_ROW_TILE)
    tile_rows = jnp.where(ids < total, rows_left, 0)
    return (dest(INFO_E0, INFO_R0), dest(INFO_E1, INFO_R1), fill, te.astype(jnp.int32),
            tile_rows.astype(jnp.int32))


def moe_final(x, info, fields, cnt, g, mod_all, ctx_row, prompt_tokens, req_tokens, final_g, w_gate, w_up,
              w_down):
    t = x.shape[0]
    counts = cnt[0, :N_EXPERTS].astype(jnp.int32)
    n_tiles = (2 * t) // MOE_ROW_TILE + N_EXPERTS
    dest0, dest1, fill, te, tv = moe_layout(counts, fields, n_tiles)
    xs_sorted = moe_dispatch(x, g, mod_all, ctx_row, prompt_tokens, req_tokens, dest0, dest1, fill, n_tiles)
    ys_sorted = moe_experts(xs_sorted, te, tv, w_gate, w_up, w_down)
    return moe_combine(x, info, ys_sorted, dest0, dest1, mod_all, ctx_row, prompt_tokens, req_tokens, final_g)


def kernel(x_prompt, x_sample, state_s5, cache_na_k, cache_na_v, cache_diff_k, cache_diff_v, c, c_ctx, w_mod, b_mod, norm_mix_g, norm_ffn_g, final_norm_g, w_in_e, w_out_e, s5_lam_re, s5_lam_im, s5_log_dt, s5_b_re, s5_b_im, s5_c_re, s5_c_im, s5_d, s5_w_glu, s5_b_glu, na_rpb, ffn_w_gate, ffn_w_up, ffn_w_down, w_in_o, w_out_o, diff_lam_q1, diff_lam_k1, diff_lam_q2, diff_lam_k2, diff_subln_g, conv_w, conv_b, conv_ln_g, conv_ln_b, router_w, moe_w_gate, moe_w_up, moe_w_down):
    bp, lp, d = x_prompt.shape
    bs, ls, _ = x_sample.shape
    tm = 1024
    xp = x_prompt.reshape(bp * lp, d)
    xs = x_sample.reshape(bs * ls, d)
    rows_p = (bp * lp) // tm
    rows_s = ls // tm

    cond8 = jnp.concatenate([c, c_ctx[None, :], jnp.zeros((SUBLANES - bs - 1, d), F32)], axis=0)
    mod = adaln_all(cond8, w_mod, b_mod)
    mod_s = mod[:, 0:bs]
    mod_p = mod[:, bs:bs + 1]

    def tiles(rows, tile):
        return rows * tm // tile

    bmat, cmat, lam8 = s5_params(s5_lam_re[0], s5_lam_im[0], s5_log_dt[0], s5_b_re[0], s5_b_im[0],
                                 s5_c_re[0], s5_c_im[0])
    bias = na_bias_blocks(na_rpb[0])
    n_e = w_in_e.shape[-1]

    tiles_p = tiles(rows_p, 512)
    tiles_s = tiles(rows_s, 512)
    proj_p = in_proj(xp, norm_mix_g[0], mod_p[0], tiles_p, w_in_e[0])
    u_s, qkv_s = in_proj(xs, norm_mix_g[0], mod_s[0], tiles_s, w_in_e[0], n_f32=HALF_MIX)

    y_p, st_p = s5_scan(proj_p.reshape(bp, lp, n_e), bmat, cmat, lam8, None, 1)
    chunks = SUBLANES // bs
    h0 = state_s5[:, 0].reshape(bs, 2, 2, S5_GROUPS * S5_STATE)
    y_s, _ = s5_scan(u_s.reshape(bs * chunks, ls // chunks, HALF_MIX), bmat, cmat, lam8, h0, chunks)
    nao_p, na_k, na_v = na_ctx(proj_p.reshape(bp, lp, n_e))

    def heads_to_lanes(cache):
        return cache.transpose(0, 2, 1, 3).reshape(bs, cache.shape[2], HALF_MIX).astype(BF16)

    nao_s = na_lat(qkv_s.reshape(bs, ls, n_e - HALF_MIX), heads_to_lanes(cache_na_k[:, 0]),
                   heads_to_lanes(cache_na_v[:, 0]), bias)

    glu_w = (s5_d[0], s5_w_glu[0].astype(BF16), s5_b_glu[0])
    ffn_w = (w_out_e[0].astype(BF16), norm_ffn_g[0])
    ffn_w3 = (ffn_w_gate[0].astype(BF16), ffn_w_up[0].astype(BF16), ffn_w_down[0].astype(BF16))
    xp = mix_ffn(xp, y_p.reshape(bp * lp, HALF_MIX), proj_p, *glu_w, nao_p.reshape(bp * lp, HALF_MIX),
                 *ffn_w, mod_p[0], tiles_p, *ffn_w3)
    xs = mix_ffn(xs, y_s.reshape(bs * ls, HALF_MIX), u_s, *glu_w, nao_s.reshape(bs * ls, HALF_MIX),
                 *ffn_w, mod_s[0], tiles_s, *ffn_w3)

    lam_init = 0.8 - 0.6 * math.exp(-0.3 * 1)
    lam = (jnp.exp(jnp.sum(diff_lam_q1[0].astype(F32) * diff_lam_k1[0].astype(F32)))
           - jnp.exp(jnp.sum(diff_lam_q2[0].astype(F32) * diff_lam_k2[0].astype(F32)))
           + lam_init)
    cos, sin = rope_tables(ls)
    n_o = w_in_o.shape[-1]

    proj_p = in_proj(xp, norm_mix_g[1], mod_p[1], tiles_p, w_in_o[0])
    proj_s = in_proj(xs, norm_mix_g[1], mod_s[1], tiles_s, w_in_o[0])

    do_p, diff_k, diff_v = diff_ctx(proj_p.reshape(bp, lp, n_o), lam, diff_subln_g[0], lam_init)
    do_s = diff_lat(proj_s.reshape(bs, ls, n_o), cache_diff_k[:, 0:1], cache_diff_v[:, 0:1], cos, sin,
                    lam, diff_subln_g[0], lam_init)
    co_p = conformer_conv(proj_p.reshape(bp, lp, n_o), conv_w[0], conv_b[0], conv_ln_g[0], conv_ln_b[0], lp)
    co_s = conformer_conv(proj_s.reshape(bs, ls, n_o), conv_w[0], conv_b[0], conv_ln_g[0], conv_ln_b[0], 512)

    mod_all = mod[1]
    x_all, info, fields, cnt = mix_route(
        xp, do_p.reshape(bp * lp, HALF_MIX), co_p.reshape(bp * lp, HALF_MIX),
        xs, do_s.reshape(bs * ls, HALF_MIX), co_s.reshape(bs * ls, HALF_MIX),
        w_out_o[0].astype(BF16), norm_ffn_g[1], mod_all, bs, ls, router_w[0])
    yp, ys = moe_final(x_all, info, fields, cnt, norm_ffn_g[1], mod_all, bs, bp * lp, ls, final_norm_g,
                       moe_w_gate[0], moe_w_up[0], moe_w_down[0])

    new_state = st_p.reshape(bp, 1, 2, 2, S5_GROUPS, S5_STATE)
    return (yp.reshape(bp, lp, d), ys.reshape(bs, ls, d), new_state, na_k, na_v, diff_k, diff_v)
```

```python
import functools
import math

import jax
import jax.numpy as jnp
import numpy as np
from jax import lax
from jax.experimental import pallas as pl
from jax.experimental.pallas import tpu as pltpu

D_MODEL = 1024
DEPTH = 2
GRID_W = 64
HALF_MIX = 512
S5_GROUP_CH = 16
S5_GROUPS = 32
S5_STATE = 64
NA_HEAD_DIM = 64
NA_HEADS = 8
NA_WIN_R = 8
NA_WIN_C = 16
DIFF_D = 64
DIFF_HEAD_DIM = 128
DIFF_HEADS = 4
ROPE_BASE = 10000.0
CONV_WIDTH = 31
N_EXPERTS = 8
EPS = 1e-6

F32 = jnp.float32
BF16 = jnp.bfloat16
NEG_BIG = -1e30

VMEM_LIMIT_BYTES = 56 * 1024 * 1024
LANES = 128
SUBLANES = 8

S5_COL_GROUPS = 8
S5_COL_CH = S5_COL_GROUPS * S5_GROUP_CH
S5_COL_STATE = S5_COL_GROUPS * S5_STATE
S5_N_COL = S5_GROUPS // S5_COL_GROUPS
S5_TIME_BLOCK = 128


def _params(*sem):
    return pltpu.CompilerParams(dimension_semantics=sem, vmem_limit_bytes=VMEM_LIMIT_BYTES)


def _dot(a, b):
    return jnp.dot(a, b, preferred_element_type=F32)


def _dot_nt(a, b):
    return lax.dot_general(a, b, (((1,), (1,)), ((), ())), preferred_element_type=F32)


def _silu(x):
    return x * jax.nn.sigmoid(x)


def _norm_mod(x, g, shift, scale):
    y = x * lax.rsqrt(jnp.mean(x * x, axis=-1, keepdims=True) + EPS) * g
    return y * (1.0 + scale) + shift


def _mod_kernel(cond_ref, w_ref, b_ref, o_ref):
    s = _silu(cond_ref[...])
    o_ref[0, :, 0, :] = jnp.dot(s, w_ref[0], precision=lax.Precision.HIGHEST,
                                preferred_element_type=F32) + b_ref[0]


def adaln_all(cond8, w_mod, b_mod):
    tn = 1536
    n = w_mod.shape[-1]
    return pl.pallas_call(
        _mod_kernel,
        grid=(DEPTH, n // tn),
        in_specs=[pl.BlockSpec((SUBLANES, D_MODEL), lambda l, j: (0, 0)),
                  pl.BlockSpec((1, D_MODEL, tn), lambda l, j: (l, 0, j)),
                  pl.BlockSpec((1, 1, tn), lambda l, j: (l, 0, j))],
        out_specs=pl.BlockSpec((1, SUBLANES, 1, tn), lambda l, j: (l, 0, 0, j)),
        out_shape=jax.ShapeDtypeStruct((DEPTH, SUBLANES, 1, n), F32),
        compiler_params=_params("arbitrary", "arbitrary"),
        name="adaln_mod",
    )(cond8, w_mod, b_mod.reshape(DEPTH, 1, n))


def _mod_spec(chunk, tiles_per_row):
    return pl.BlockSpec((1, 1, D_MODEL), lambda i, *_: (i // tiles_per_row, 0, chunk))


def _in_proj_kernel(x_ref, g_ref, sh_ref, sc_ref, w_ref, *refs, n_f32):
    *outs, wb_scr = refs

    @pl.when(pl.program_id(0) == 0)
    def _():
        wb_scr[...] = w_ref[...].astype(BF16)

    h = _norm_mod(x_ref[...], g_ref[...], sh_ref[0], sc_ref[0]).astype(BF16)
    y = _dot(h, wb_scr[...])
    if len(outs) == 1:
        outs[0][...] = y
    else:
        outs[0][...] = y[:, :n_f32]
        outs[1][...] = y[:, n_f32:].astype(BF16)


def in_proj(x, g, mod, tiles_per_row, w, n_f32=None, tm=512):
    t = x.shape[0]
    n = w.shape[1]
    if n_f32 is None:
        out_specs = pl.BlockSpec((tm, n), lambda i: (i, 0))
        out_shape = jax.ShapeDtypeStruct((t, n), F32)
    else:
        out_specs = [pl.BlockSpec((tm, n_f32), lambda i: (i, 0)), pl.BlockSpec((tm, n - n_f32), lambda i: (i, 0))]
        out_shape = [jax.ShapeDtypeStruct((t, n_f32), F32), jax.ShapeDtypeStruct((t, n - n_f32), BF16)]
    return pl.pallas_call(
        functools.partial(_in_proj_kernel, n_f32=n_f32),
        grid=(t // tm,),
        in_specs=[pl.BlockSpec((tm, D_MODEL), lambda i: (i, 0)),
                  pl.BlockSpec((1, D_MODEL), lambda i: (0, 0)),
                  _mod_spec(0, tiles_per_row),
                  _mod_spec(1, tiles_per_row),
                  pl.BlockSpec((D_MODEL, n), lambda i: (0, 0), pipeline_mode=pl.Buffered(1))],
        out_specs=out_specs,
        out_shape=out_shape,
        scratch_shapes=[pltpu.VMEM((D_MODEL, n), BF16)],
        compiler_params=_params("arbitrary"),
        name="in_proj",
    )(x, g.reshape(1, D_MODEL), mod, mod, w)


def _s5_scan_kernel(*refs, seq, chunks, has_init):
    if has_init:
        u_ref, bm_ref, cm_ref, lam_ref, h0_ref, y_ref, st_ref, bu_scr, hb_scr, ytm_scr = refs
    else:
        u_ref, bm_ref, cm_ref, lam_ref, y_ref, st_ref, bu_scr, hb_scr, ytm_scr = refs
        h0_ref = None
    tb = S5_TIME_BLOCK
    n_tb = seq // tb
    ns = S5_COL_STATE
    row = lax.broadcasted_iota(jnp.int32, (SUBLANES, ns), 0)
    piece = row % chunks

    for d in range(2):
        lam = lam_ref[d, 0]
        lr, li = lam[:, :ns], lam[:, ns:]
        bm = bm_ref[d, 0]
        cm = cm_ref[d, 0]
        blocks = list(range(n_tb)) if d == 0 else list(range(n_tb - 1, -1, -1))

        def project(k, buf):
            ub = u_ref[:, k * tb:(k + 1) * tb, :]
            utm = jnp.swapaxes(ub, 0, 1).reshape(tb * SUBLANES, S5_COL_CH).astype(BF16)
            bu_scr[buf] = _dot(utm, bm).reshape(tb, SUBLANES, 2 * ns)

        def scan_block(h, buf, store):
            def advance(t, hr, hi):
                b = bu_scr[buf, t]
                return lr * hr - li * hi + b[:, :ns], lr * hi + li * hr + b[:, ns:]

            def step(s, carry):
                t_a = (tb - 1 - 2 * s) if d == 1 else 2 * s
                t_b = t_a - 1 if d == 1 else t_a + 1
                ar, ai = advance(t_a, *carry)
                br, bi = advance(t_b, ar, ai)
                if store:
                    first = jnp.concatenate([br, bi] if d == 1 else [ar, ai], axis=1)
                    second = jnp.concatenate([ar, ai] if d == 1 else [br, bi], axis=1)
                    t_lo = t_b if d == 1 else t_a
                    hb_scr[buf, t_lo * SUBLANES:(t_lo + 2) * SUBLANES, :] = (
                        jnp.concatenate([first, second], axis=0).astype(BF16))
                return br, bi

            if not store:
                return lax.fori_loop(0, tb // 2, step, h)
            for s in range(tb // 2):
                h = step(s, h)
            return h

        def run_pass(h, store):
            project(blocks[0], 0)
            for j, k in enumerate(blocks):
                buf = j % 2
                if j + 1 < n_tb:
                    project(blocks[j + 1], 1 - buf)
                h = scan_block(h, buf, store)
                if store:
                    yb = _dot(hb_scr[buf], cm).reshape(tb, SUBLANES, S5_COL_CH)
                    if d == 0:
                        ytm_scr[k * tb:(k + 1) * tb] = yb
                    else:
                        ytm_scr[k * tb:(k + 1) * tb] += yb
            return h

        zero = jnp.zeros((SUBLANES, ns), F32)
        if chunks > 1:
            fr, fi = run_pass((zero, zero), False)
            pr, pi = lr, li
            for _ in range(int(math.log2(seq))):
                pr, pi = pr * pr - pi * pi, 2.0 * pr * pi
            edge = 0 if d == 0 else chunks - 1
            shift = 1 if d == 0 else SUBLANES - 1
            if has_init:
                h0r = h0_ref[:, d, 0, :]
                h0i = h0_ref[:, d, 1, :]
                seq_of_row = row // chunks
                er, ei = zero, zero
                for b in range(SUBLANES // chunks):
                    er = jnp.where(seq_of_row == b, h0r[b:b + 1, :], er)
                    ei = jnp.where(seq_of_row == b, h0i[b:b + 1, :], ei)
            else:
                er, ei = zero, zero
            is_edge = piece == edge
            cr = jnp.where(is_edge, er, zero)
            ci = jnp.where(is_edge, ei, zero)
            for _ in range(chunks - 1):
                tr = fr + pr * cr - pi * ci
                ti = fi + pr * ci + pi * cr
                cr = jnp.where(is_edge, er, pltpu.roll(tr, shift, 0))
                ci = jnp.where(is_edge, ei, pltpu.roll(ti, shift, 0))
            h = (cr, ci)
        else:
            if has_init:
                h = (h0_ref[:, d, 0, :], h0_ref[:, d, 1, :])
            else:
                h = (zero, zero)

        h = run_pass(h, True)
        st_ref[:, d, 0, :] = h[0]
        st_ref[:, d, 1, :] = h[1]

    y_ref[...] = jnp.swapaxes(ytm_scr[...], 0, 1)


def s5_scan(proj3, bmat, cmat, lam8, h0, chunks):
    rows, seq, _ = proj3.shape
    ns = S5_COL_STATE
    has_init = h0 is not None
    in_specs = [pl.BlockSpec((SUBLANES, seq, S5_COL_CH), lambda i, c: (i, 0, c)),
                pl.BlockSpec((2, 1, S5_COL_CH, 2 * ns), lambda i, c: (0, c, 0, 0)),
                pl.BlockSpec((2, 1, 2 * ns, S5_COL_CH), lambda i, c: (0, c, 0, 0)),
                pl.BlockSpec((2, 1, SUBLANES, 2 * ns), lambda i, c: (0, c, 0, 0))]
    args = [proj3, bmat, cmat, lam8]
    if has_init:
        nb = h0.shape[0]
        in_specs.append(pl.BlockSpec((nb, 2, 2, ns), lambda i, c: (0, 0, 0, c)))
        args.append(h0)
    y, st = pl.pallas_call(
        functools.partial(_s5_scan_kernel, seq=seq, chunks=chunks, has_init=has_init),
        grid=(rows // SUBLANES, S5_N_COL),
        in_specs=in_specs,
        out_specs=[pl.BlockSpec((SUBLANES, seq, S5_COL_CH), lambda i, c: (i, 0, c)),
                   pl.BlockSpec((SUBLANES, 2, 2, ns), lambda i, c: (i, 0, 0, c))],
        out_shape=[jax.ShapeDtypeStruct((rows, seq, HALF_MIX), F32),
                   jax.ShapeDtypeStruct((rows, 2, 2, S5_GROUPS * S5_STATE), F32)],
        scratch_shapes=[pltpu.VMEM((2, S5_TIME_BLOCK, SUBLANES, 2 * ns), F32),
                        pltpu.VMEM((2, S5_TIME_BLOCK * SUBLANES, 2 * ns), BF16),
                        pltpu.VMEM((seq, SUBLANES, S5_COL_CH), F32)],
        compiler_params=_params("arbitrary", "arbitrary"),
        name="s5_scan",
    )(*args)
    return y, st


def s5_params(lam_re, lam_im, log_dt, b_re, b_im, c_re, c_im):
    lr = lam_re.astype(F32)
    li = lam_im.astype(F32)
    dt = jnp.exp(log_dt.astype(F32))[..., None]
    mag = jnp.exp(lr * dt)
    bar_re = mag * jnp.cos(li * dt)
    bar_im = mag * jnp.sin(li * dt)
    den = lr * lr + li * li
    q_re = ((bar_re - 1.0) * lr + bar_im * li) / den
    q_im = (bar_im * lr - (bar_re - 1.0) * li) / den
    br = b_re.astype(F32)
    bi = b_im.astype(F32)
    b_bar_re = q_re[..., None] * br - q_im[..., None] * bi
    b_bar_im = q_re[..., None] * bi + q_im[..., None] * br
    eye = jnp.eye(S5_COL_GROUPS, dtype=F32)

    def block_diag_b(m):
        m = m.reshape(2, S5_N_COL, S5_COL_GROUPS, S5_STATE, S5_GROUP_CH)
        bd = jnp.einsum('dngpc,gh->dngchp', m, eye)
        return bd.reshape(2, S5_N_COL, S5_COL_CH, S5_COL_STATE)

    def block_diag_c(m):
        m = m.reshape(2, S5_N_COL, S5_COL_GROUPS, S5_GROUP_CH, S5_STATE)
        bd = jnp.einsum('dngcp,gh->dngphc', m, eye)
        return bd.reshape(2, S5_N_COL, S5_COL_STATE, S5_COL_CH)

    bmat = jnp.concatenate([block_diag_b(b_bar_re), block_diag_b(b_bar_im)], axis=-1).astype(BF16)
    cmat = jnp.concatenate([block_diag_c(c_re.astype(F32)), block_diag_c(-c_im.astype(F32))],
                           axis=-2).astype(BF16)
    lam_cat = jnp.concatenate([bar_re.reshape(2, S5_N_COL, S5_COL_STATE),
                               bar_im.reshape(2, S5_N_COL, S5_COL_STATE)], axis=-1)
    lam8 = jnp.broadcast_to(lam_cat[:, :, None, :], (2, S5_N_COL, SUBLANES, 2 * S5_COL_STATE))
    return bmat, cmat, lam8


def _na_ctx_kernel(q_ref, k_ref, v_ref, o_ref, ko_ref, vo_ref):
    seq = q_ref.shape[1]
    lane = lax.broadcasted_iota(jnp.int32, (seq, LANES), 1)
    low = lane < NA_HEAD_DIM
    outs = []
    for pr in range(NA_HEADS // 2):
        cols = slice(pr * LANES, (pr + 1) * LANES)
        qp = q_ref[0, :, cols] * (NA_HEAD_DIM ** -0.5)
        kp = k_ref[0, :, cols]
        vp = v_ref[0, :, cols]
        for half in range(2):
            sl = slice(half * NA_HEAD_DIM, (half + 1) * NA_HEAD_DIM)
            ko_ref[0, 0, 2 * pr + half] = kp[:, sl]
            vo_ref[0, 0, 2 * pr + half] = vp[:, sl]
        kb = kp.astype(BF16)
        vb = vp.astype(BF16)
        o_pair = None
        for half in range(2):
            qm = jnp.where(low if half == 0 else jnp.logical_not(low), qp, 0.0).astype(BF16)
            s = _dot_nt(qm, kb)
            p = jnp.exp(s - jnp.max(s, axis=-1, keepdims=True))
            o = _dot(p.astype(BF16), vb) * (1.0 / jnp.sum(p, axis=-1, keepdims=True))
            o_pair = o if half == 0 else jnp.where(low, o_pair, o)
        outs.append(o_pair)
    o_ref[0] = jnp.concatenate(outs, axis=-1).astype(o_ref.dtype)


def na_ctx(proj3):
    b, seq, _ = proj3.shape
    cache_shape = jax.ShapeDtypeStruct((b, 1, NA_HEADS, seq, NA_HEAD_DIM), F32)
    cache_spec = pl.BlockSpec((1, 1, NA_HEADS, seq, NA_HEAD_DIM), lambda i: (i, 0, 0, 0, 0))
    return pl.pallas_call(
        _na_ctx_kernel,
        grid=(b,),
        in_specs=[pl.BlockSpec((1, seq, HALF_MIX), lambda i: (i, 0, 1)),
                  pl.BlockSpec((1, seq, HALF_MIX), lambda i: (i, 0, 2)),
                  pl.BlockSpec((1, seq, HALF_MIX), lambda i: (i, 0, 3))],
        out_specs=[pl.BlockSpec((1, seq, HALF_MIX), lambda i: (i, 0, 0)), cache_spec, cache_spec],
        out_shape=[jax.ShapeDtypeStruct((b, seq, HALF_MIX), BF16), cache_shape, cache_shape],
        compiler_params=_params("arbitrary"),
        name="na_ctx",
    )(proj3, proj3, proj3)


NA_Q_ROWS = 4
NA_KEY_ROWS = 12


def na_bias_blocks(rpb):
    qcol = np.arange(GRID_W)
    cc = np.arange(GRID_W)
    cs = np.clip(qcol - NA_WIN_C // 2, 0, GRID_W - NA_WIN_C)
    valid = (cc[None, :] >= cs[:, None]) & (cc[None, :] < cs[:, None] + NA_WIN_C)
    coff = cc[None, :] - qcol[:, None] + (NA_WIN_C - 1)
    n_col = 2 * NA_WIN_C - 1
    sel = ((coff[None] == np.arange(n_col)[:, None, None]) & valid[None]).astype(np.float32)
    sel = sel.reshape(n_col, GRID_W * GRID_W)
    mask = np.where(valid, 0.0, NEG_BIG).astype(np.float32).reshape(1, GRID_W * GRID_W)
    n_row = 2 * NA_WIN_R - 1
    t1 = jnp.dot(rpb.astype(F32).reshape(NA_HEADS * n_row, n_col), jnp.asarray(sel),
                 precision=lax.Precision.HIGHEST) + jnp.asarray(mask)
    t1 = t1.reshape(NA_HEADS, n_row, GRID_W, GRID_W)
    return pl.pallas_call(
        _na_bias_kernel,
        grid=(NA_HEADS,),
        in_specs=[pl.BlockSpec((1, n_row, GRID_W, GRID_W), lambda h: (h, 0, 0, 0))],
        out_specs=pl.BlockSpec((3, 1, NA_Q_ROWS * GRID_W, NA_KEY_ROWS * GRID_W), lambda h: (0, h, 0, 0)),
        out_shape=jax.ShapeDtypeStruct((3, NA_HEADS, NA_Q_ROWS * GRID_W, NA_KEY_ROWS * GRID_W), F32),
        compiler_params=_params("arbitrary"),
        name="na_bias",
    )(t1)


def _na_bias_kernel(t1_ref, o_ref):
    variants = ((lambda ri: 0, NA_WIN_R - 1), (lambda ri: ri, NA_WIN_R // 2 - 1),
                (lambda ri: NA_KEY_ROWS - NA_WIN_R, -1))
    outside = jnp.full((GRID_W, GRID_W), NEG_BIG, F32)
    for v, (lo_of, shift) in enumerate(variants):
        for ri in range(NA_Q_ROWS):
            for wr in range(NA_KEY_ROWS):
                inside = lo_of(ri) <= wr < lo_of(ri) + NA_WIN_R
                o_ref[v, 0, ri * GRID_W:(ri + 1) * GRID_W, wr * GRID_W:(wr + 1) * GRID_W] = (
                    t1_ref[0, wr - ri + shift] if inside else outside)


def _na_lat_kernel(q_ref, k_ref, v_ref, kc_ref, vc_ref, bias_ref, o_ref):
    qb = pl.program_id(1)
    rows = k_ref.shape[1] // GRID_W
    nk = NA_KEY_ROWS * GRID_W
    first_row = jnp.clip(qb * NA_Q_ROWS - NA_WIN_R // 2, 0, rows - NA_KEY_ROWS)
    start = pl.multiple_of(first_row * GRID_W, GRID_W)
    tq = q_ref.shape[1]
    lane = lax.broadcasted_iota(jnp.int32, (tq, LANES), 1)
    low = lane < NA_HEAD_DIM
    outs = []
    for pr in range(NA_HEADS // 2):
        cols = slice(pr * LANES, (pr + 1) * LANES)
        qp = q_ref[0, :, cols].astype(F32) * (NA_HEAD_DIM ** -0.5)
        kw = k_ref[0, pl.ds(start, nk), cols]
        vw = v_ref[0, pl.ds(start, nk), cols]
        kc = kc_ref[0, :, cols]
        vc = vc_ref[0, :, cols]
        o_pair = None
        for half in range(2):
            qm = jnp.where(low if half == 0 else jnp.logical_not(low), qp, 0.0).astype(BF16)
            s_loc = _dot_nt(qm, kw) + bias_ref[0, 2 * pr + half]
            s_ctx = _dot_nt(qm, kc)
            m = jnp.maximum(jnp.max(s_loc, axis=-1, keepdims=True), jnp.max(s_ctx, axis=-1, keepdims=True))
            p_loc = jnp.exp(s_loc - m)
            p_ctx = jnp.exp(s_ctx - m)
            inv = 1.0 / (jnp.sum(p_loc, axis=-1, keepdims=True) + jnp.sum(p_ctx, axis=-1, keepdims=True))
            o = (_dot(p_loc.astype(BF16), vw) + _dot(p_ctx.astype(BF16), vc)) * inv
            o_pair = o if half == 0 else jnp.where(low, o_pair, o)
        outs.append(o_pair)
    o_ref[0] = jnp.concatenate(outs, axis=-1).astype(o_ref.dtype)


def na_lat(qkv3, k_ctx, v_ctx, bias):
    b, seq, _ = qkv3.shape
    tq = NA_Q_ROWS * GRID_W
    n_q = seq // tq
    lc = k_ctx.shape[1]
    ctx_spec = pl.BlockSpec((1, lc, HALF_MIX), lambda i, r: (i, 0, 0))
    return pl.pallas_call(
        _na_lat_kernel,
        grid=(b, n_q),
        in_specs=[pl.BlockSpec((1, tq, HALF_MIX), lambda i, r: (i, r, 0)),
                  pl.BlockSpec((1, seq, HALF_MIX), lambda i, r: (i, 0, 1)),
                  pl.BlockSpec((1, seq, HALF_MIX), lambda i, r: (i, 0, 2)),
                  ctx_spec, ctx_spec,
                  pl.BlockSpec((1, NA_HEADS, tq, NA_KEY_ROWS * GRID_W),
                               lambda i, r: (jnp.where(r == 0, 0, jnp.where(r == n_q - 1, 2, 1)), 0, 0, 0))],
        out_specs=pl.BlockSpec((1, tq, HALF_MIX), lambda i, r: (i, r, 0)),
        out_shape=jax.ShapeDtypeStruct((b, seq, HALF_MIX), BF16),
        compiler_params=_params("arbitrary", "arbitrary"),
        name="na_lat",
    )(qkv3, qkv3, qkv3, k_ctx, v_ctx, bias)


def _softmax_pair_diff(s1, s2, lam):
    p1 = jnp.exp(s1 - jnp.max(s1, axis=-1, keepdims=True))
    p2 = jnp.exp(s2 - jnp.max(s2, axis=-1, keepdims=True))
    inv1 = 1.0 / jnp.sum(p1, axis=-1, keepdims=True)
    inv2 = lam / jnp.sum(p2, axis=-1, keepdims=True)
    return p1 * inv1 - p2 * inv2


def _sub_ln(o, g, lam_init):
    return o * lax.rsqrt(jnp.mean(o * o, axis=-1, keepdims=True) + EPS) * g * (1.0 - lam_init)


def _diff_ctx_kernel(lam_ref, q_ref, k_ref, v_ref, g_ref, o_ref, ko_ref, vo_ref, *, lam_init):
    scale = DIFF_D ** -0.5
    lam = lam_ref[0, 0]
    lane = lax.broadcasted_iota(jnp.int32, (q_ref.shape[1], DIFF_HEAD_DIM), 1)
    first = lane < DIFF_D
    for h in range(DIFF_HEADS):
        sl = slice(h * DIFF_HEAD_DIM, (h + 1) * DIFF_HEAD_DIM)
        qh = q_ref[0, :, sl]
        kh = k_ref[0, :, sl]
        vh = v_ref[0, :, sl]
        ko_ref[0, 0, h] = kh
        vo_ref[0, 0, h] = vh
        kb = kh.astype(BF16)
        s1 = _dot_nt(jnp.where(first, qh, 0.0).astype(BF16), kb) * scale
        s2 = _dot_nt(jnp.where(first, 0.0, qh).astype(BF16), kb) * scale
        a = _softmax_pair_diff(s1, s2, lam)
        o = _dot(a.astype(BF16), vh.astype(BF16))
        o_ref[0, :, sl] = _sub_ln(o, g_ref[...], lam_init).astype(o_ref.dtype)


def diff_ctx(proj3, lam, subln_g, lam_init):
    b, seq, _ = proj3.shape
    cache_shape = jax.ShapeDtypeStruct((b, 1, DIFF_HEADS, seq, DIFF_HEAD_DIM), F32)
    cache_spec = pl.BlockSpec((1, 1, DIFF_HEADS, seq, DIFF_HEAD_DIM), lambda i: (i, 0, 0, 0, 0))
    return pl.pallas_call(
        functools.partial(_diff_ctx_kernel, lam_init=lam_init),
        grid=(b,),
        in_specs=[pl.BlockSpec(memory_space=pltpu.SMEM),
                  pl.BlockSpec((1, seq, HALF_MIX), lambda i: (i, 0, 0)),
                  pl.BlockSpec((1, seq, HALF_MIX), lambda i: (i, 0, 1)),
                  pl.BlockSpec((1, seq, HALF_MIX), lambda i: (i, 0, 2)),
                  pl.BlockSpec((1, DIFF_HEAD_DIM), lambda i: (0, 0))],
        out_specs=[pl.BlockSpec((1, seq, HALF_MIX), lambda i: (i, 0, 0)), cache_spec, cache_spec],
        out_shape=[jax.ShapeDtypeStruct((b, seq, HALF_MIX), BF16), cache_shape, cache_shape],
        compiler_params=_params("arbitrary"),
        name="diff_ctx",
    )(lam.reshape(1, 1), proj3, proj3, proj3, subln_g.reshape(1, DIFF_HEAD_DIM))


def rope_tables(seq):
    t = np.arange(seq)
    row = (t // GRID_W).astype(np.float32)
    col = (t % GRID_W).astype(np.float32)
    n_freq = DIFF_D // 4
    inv = np.float32(ROPE_BASE) ** (-np.arange(n_freq, dtype=np.float32) / np.float32(n_freq))
    ang = np.concatenate([row[:, None] * inv, col[:, None] * inv], axis=-1)
    cos = np.repeat(np.cos(ang), 2, axis=-1)
    sin = np.repeat(np.sin(ang), 2, axis=-1)
    sign = np.where(np.arange(DIFF_D) % 2 == 0, -1.0, 1.0).astype(np.float32)
    sin = sin * sign
    return (jnp.asarray(np.tile(cos, (1, 2)).astype(np.float32)),
            jnp.asarray(np.tile(sin, (1, 2)).astype(np.float32)))


def _rope(x, cos, sin_signed):
    lane = lax.broadcasted_iota(jnp.int32, x.shape, 1)
    nxt = pltpu.roll(x, x.shape[1] - 1, 1)
    prv = pltpu.roll(x, 1, 1)
    partner = jnp.where(lane % 2 == 0, nxt, prv)
    return x * cos + partner * sin_signed


def _diff_lat_kernel(lam_ref, q_ref, k_ref, v_ref, kc_ref, vc_ref, cq_ref, sq_ref, ck_ref, sk_ref,
                     g_ref, o_ref, k_all, v_all, *, lam_init):
    seq = k_ref.shape[1]

    @pl.when(pl.program_id(2) == 0)
    def _():
        k_all[0:seq, :] = _rope(k_ref[0], ck_ref[...], sk_ref[...]).astype(BF16)
        k_all[seq:, :] = kc_ref[0, 0, 0].astype(BF16)
        v_all[0:seq, :] = v_ref[0].astype(BF16)
        v_all[seq:, :] = vc_ref[0, 0, 0].astype(BF16)

    lam = lam_ref[0, 0]
    q = _rope(q_ref[0], cq_ref[...], sq_ref[...]) * (DIFF_D ** -0.5)
    lane = lax.broadcasted_iota(jnp.int32, q.shape, 1)
    first = lane < DIFF_D
    kb = k_all[...]
    s1 = _dot_nt(jnp.where(first, q, 0.0).astype(BF16), kb)
    s2 = _dot_nt(jnp.where(first, 0.0, q).astype(BF16), kb)
    a = _softmax_pair_diff(s1, s2, lam)
    o = _dot(a.astype(BF16), v_all[...])
    o_ref[0] = _sub_ln(o, g_ref[...], lam_init).astype(o_ref.dtype)


def diff_lat(proj3, k_ctx, v_ctx, cos, sin, lam, subln_g, lam_init, tq=256):
    b, seq, _ = proj3.shape
    lc = k_ctx.shape[3]
    hd = DIFF_HEAD_DIM
    ctx_spec = pl.BlockSpec((1, 1, 1, lc, hd), lambda i, h, q: (i, 0, h, 0, 0))
    tq_spec = pl.BlockSpec((tq, hd), lambda i, h, q: (q, 0))
    full_spec = pl.BlockSpec((seq, hd), lambda i, h, q: (0, 0))
    return pl.pallas_call(
        functools.partial(_diff_lat_kernel, lam_init=lam_init),
        grid=(b, DIFF_HEADS, seq // tq),
        in_specs=[pl.BlockSpec(memory_space=pltpu.SMEM),
                  pl.BlockSpec((1, tq, hd), lambda i, h, q: (i, q, h)),
                  pl.BlockSpec((1, seq, hd), lambda i, h, q: (i, 0, DIFF_HEADS + h)),
                  pl.BlockSpec((1, seq, hd), lambda i, h, q: (i, 0, 2 * DIFF_HEADS + h)),
                  ctx_spec, ctx_spec, tq_spec, tq_spec, full_spec, full_spec,
                  pl.BlockSpec((1, hd), lambda i, h, q: (0, 0))],
        out_specs=pl.BlockSpec((1, tq, hd), lambda i, h, q: (i, q, h)),
        out_shape=jax.ShapeDtypeStruct((b, seq, HALF_MIX), BF16),
        scratch_shapes=[pltpu.VMEM((seq + lc, hd), BF16), pltpu.VMEM((seq + lc, hd), BF16)],
        compiler_params=_params("arbitrary", "arbitrary", "arbitrary"),
        name="diff_lat",
    )(lam.reshape(1, 1), proj3, proj3, proj3, k_ctx, v_ctx, cos, sin, cos, sin,
      subln_g.reshape(1, hd))


CONV_PAD = 16
CONV_SUB = 64


def _conv_kernel(a_ref, g_ref, ap_ref, gp_ref, an_ref, gn_ref, w_ref, b_ref, lg_ref, lb_ref, o_ref, xp_scr,
                 xsh_scr):
    t = pl.program_id(1)
    tt = a_ref.shape[1]
    prev = ap_ref[0] * jax.nn.sigmoid(gp_ref[0])
    nxt = an_ref[0] * jax.nn.sigmoid(gn_ref[0])
    xp_scr[0:CONV_PAD, :] = jnp.where(t > 0, prev, 0.0)
    xp_scr[CONV_PAD + tt:, :] = jnp.where(t < pl.num_programs(1) - 1, nxt, 0.0)
    xp_scr[CONV_PAD:CONV_PAD + tt, :] = a_ref[0] * jax.nn.sigmoid(g_ref[0])
    first_tap = CONV_PAD - CONV_WIDTH // 2
    n_rows = xsh_scr.shape[1]
    for b in range(SUBLANES):
        xsh_scr[b] = xp_scr[b:b + n_rows, :]
    for i in range(tt // CONV_SUB):
        s = i * CONV_SUB
        acc = jnp.zeros((CONV_SUB, HALF_MIX), F32)
        for j in range(CONV_WIDTH):
            whole, phase = divmod(first_tap + j, SUBLANES)
            lo = s + whole * SUBLANES
            acc = acc + xsh_scr[phase, lo:lo + CONV_SUB, :] * w_ref[j:j + 1, :]
        y = acc + b_ref[...]
        mu = jnp.mean(y, axis=-1, keepdims=True)
        yc = y - mu
        var = jnp.mean(yc * yc, axis=-1, keepdims=True)
        yn = yc * lax.rsqrt(var + EPS) * lg_ref[...] + lb_ref[...]
        o_ref[0, s:s + CONV_SUB, :] = _silu(yn).astype(o_ref.dtype)


def conformer_conv(proj3, w, b, ln_g, ln_b, tt):
    bsz, seq, _ = proj3.shape
    n_t = seq // tt
    hb = tt // CONV_PAD
    last = seq // CONV_PAD - 1
    vec = pl.BlockSpec((1, HALF_MIX), lambda i, t: (0, 0))

    def main(col):
        return pl.BlockSpec((1, tt, HALF_MIX), lambda i, t: (i, t, col))

    def prev(col):
        return pl.BlockSpec((1, CONV_PAD, HALF_MIX), lambda i, t: (i, jnp.maximum(t * hb - 1, 0), col))

    def nxt(col):
        return pl.BlockSpec((1, CONV_PAD, HALF_MIX), lambda i, t: (i, jnp.minimum((t + 1) * hb, last), col))

    return pl.pallas_call(
        _conv_kernel,
        grid=(bsz, n_t),
        in_specs=[main(3), main(4), prev(3), prev(4), nxt(3), nxt(4),
                  pl.BlockSpec((CONV_WIDTH, HALF_MIX), lambda i, t: (0, 0)),
                  vec, vec, vec],
        out_specs=pl.BlockSpec((1, tt, HALF_MIX), lambda i, t: (i, t, 0)),
        out_shape=jax.ShapeDtypeStruct((bsz, seq, HALF_MIX), BF16),
        scratch_shapes=[pltpu.VMEM((tt + 2 * CONV_PAD, HALF_MIX), F32),
                        pltpu.VMEM((SUBLANES, tt + 2 * CONV_PAD - SUBLANES, HALF_MIX), F32)],
        compiler_params=_params("arbitrary", "arbitrary"),
        name="conformer_conv",
    )(proj3, proj3, proj3, proj3, proj3, proj3, w, b.reshape(1, HALF_MIX), ln_g.reshape(1, HALF_MIX),
      ln_b.reshape(1, HALF_MIX))


FFN_CHUNK = 1024


def _mix_ffn_kernel(x_ref, y_ref, u_ref, d_ref, wglu_ref, bglu_ref, m2_ref, w1_ref, w2_ref, g1_ref,
                    g_ref, sh_ref, sc_ref, g2_ref, wg_ref, wu_ref, wd_ref, o_ref):
    z = jax.nn.gelu(u_ref[...] * d_ref[...] + y_ref[...])
    s5_out = z * jax.nn.sigmoid(_dot(z.astype(BF16), wglu_ref[...]) + bglu_ref[...])
    mix = _dot(s5_out.astype(BF16), w1_ref[...]) + _dot(m2_ref[...], w2_ref[...])
    x1 = x_ref[...] + g1_ref[0] * mix
    h = _norm_mod(x1, g_ref[...], sh_ref[0], sc_ref[0]).astype(BF16)
    fdim = wg_ref.shape[1]
    acc = None
    for lo in range(0, fdim, FFN_CHUNK):
        hi = min(lo + FFN_CHUNK, fdim)
        a = _dot(h, wg_ref[:, lo:hi])
        up = _dot(h, wu_ref[:, lo:hi])
        part = _dot((_silu(a) * up).astype(BF16), wd_ref[lo:hi, :])
        acc = part if acc is None else acc + part
    o_ref[...] = x1 + g2_ref[0] * acc


def mix_ffn(x, y, u, d_skip, w_glu, b_glu, m2, w_out, g, mod, tiles_per_row, w_gate, w_up, w_down, tm=512):
    t = x.shape[0]
    fdim = w_gate.shape[1]
    half = pl.BlockSpec((tm, HALF_MIX), lambda i: (i, 0))
    vec = pl.BlockSpec((1, HALF_MIX), lambda i: (0, 0))

    def resident(shape, index):
        return pl.BlockSpec(shape, index, pipeline_mode=pl.Buffered(1))

    return pl.pallas_call(
        _mix_ffn_kernel,
        grid=(t // tm,),
        in_specs=[pl.BlockSpec((tm, D_MODEL), lambda i: (i, 0)), half, half, vec,
                  resident((HALF_MIX, HALF_MIX), lambda i: (0, 0)), vec, half,
                  resident((HALF_MIX, D_MODEL), lambda i: (0, 0)),
                  resident((HALF_MIX, D_MODEL), lambda i: (1, 0)),
                  _mod_spec(2, tiles_per_row),
                  pl.BlockSpec((1, D_MODEL), lambda i: (0, 0)),
                  _mod_spec(3, tiles_per_row), _mod_spec(4, tiles_per_row), _mod_spec(5, tiles_per_row),
                  resident((D_MODEL, fdim), lambda i: (0, 0)),
                  resident((D_MODEL, fdim), lambda i: (0, 0)),
                  resident((fdim, D_MODEL), lambda i: (0, 0))],
        out_specs=pl.BlockSpec((tm, D_MODEL), lambda i: (i, 0)),
        out_shape=jax.ShapeDtypeStruct((t, D_MODEL), F32),
        compiler_params=_params("arbitrary"),
        name="mix_ffn",
    )(x, y, u, d_skip.reshape(1, HALF_MIX), w_glu, b_glu.reshape(1, HALF_MIX), m2, w_out, w_out, mod,
      g.reshape(1, D_MODEL), mod, mod, mod, w_gate, w_up, w_down)


MOE_ROW_TILE = 1024
MOE_ZERO_ROWS = 256
MOE_ROW_PARTS = 8
INFO_E0, INFO_E1, INFO_G0, INFO_G1, INFO_R0, INFO_R1 = range(6)


def _stream_mod_spec(chunk, prompt_tiles, tiles_per_req, ctx_row):
    def index(i, *_):
        return (jnp.where(i < prompt_tiles, ctx_row, (i - prompt_tiles) // tiles_per_req), 0, chunk)
    return pl.BlockSpec((1, 1, D_MODEL), index)


def _mix_route_kernel(xp_ref, m1p_ref, m2p_ref, xs_ref, m1s_ref, m2s_ref, w1_ref, w2_ref, gate_ref,
                      g_ref, sh_ref, sc_ref, rwh_ref, rwl_ref, o_ref, info_ref, fields_ref, cnt_ref,
                      tri_scr, run_scr, *, prompt_tiles):
    i = pl.program_id(0)
    tm = o_ref.shape[0]
    w1 = w1_ref[...]
    w2 = w2_ref[...]

    @pl.when(i < prompt_tiles)
    def _():
        o_ref[...] = xp_ref[...] + gate_ref[0] * (_dot(m1p_ref[...], w1) + _dot(m2p_ref[...], w2))

    @pl.when(i >= prompt_tiles)
    def _():
        o_ref[...] = xs_ref[...] + gate_ref[0] * (_dot(m1s_ref[...], w1) + _dot(m2s_ref[...], w2))

    @pl.when(i == 0)
    def _():
        r = lax.broadcasted_iota(jnp.int32, (tm, tm), 0)
        c = lax.broadcasted_iota(jnp.int32, (tm, tm), 1)
        tri_scr[...] = jnp.where(c < r, 1.0, 0.0).astype(BF16)
        run_scr[...] = jnp.zeros_like(run_scr)

    h = _norm_mod(o_ref[...], g_ref[...], sh_ref[0], sc_ref[0])
    h_hi = h.astype(BF16)
    h_lo = (h - h_hi.astype(F32)).astype(BF16)
    logits = _dot(h_hi, rwh_ref[...]) + (_dot(h_hi, rwl_ref[...]) + _dot(h_lo, rwh_ref[...]))
    lane = lax.broadcasted_iota(jnp.int32, logits.shape, 1).astype(F32)
    logits = jnp.where(lane < N_EXPERTS, logits, -jnp.inf)
    m1 = jnp.max(logits, axis=-1, keepdims=True)
    i1 = jnp.min(jnp.where(logits == m1, lane, float(LANES)), axis=-1, keepdims=True)
    rest = jnp.where(lane == i1, -jnp.inf, logits)
    m2 = jnp.max(rest, axis=-1, keepdims=True)
    i2 = jnp.min(jnp.where(rest == m2, lane, float(LANES)), axis=-1, keepdims=True)
    e2 = jnp.exp(m2 - m1)
    den = 1.0 + e2
    hit = jnp.where(lane == i1, 1.0, 0.0) + jnp.where(lane == i2, 1.0, 0.0)
    before = _dot(tri_scr[...], hit.astype(BF16)) + run_scr[0:1, :]
    r1 = jnp.sum(jnp.where(lane == i1, before, 0.0), axis=-1, keepdims=True)
    r2 = jnp.sum(jnp.where(lane == i2, before, 0.0), axis=-1, keepdims=True)
    info = jnp.zeros_like(logits)
    for slot, val in ((INFO_E0, i1), (INFO_E1, i2), (INFO_G0, 1.0 / den), (INFO_G1, e2 / den),
                      (INFO_R0, r1), (INFO_R1, r2)):
        info = jnp.where(lane == float(slot), val, info)
    info_ref[...] = info
    fields_ref[...] = jnp.transpose(info)[0:SUBLANES, :]
    run_scr[...] = run_scr[...] + jnp.sum(hit, axis=0, keepdims=True)
    cnt_ref[...] = run_scr[...]


def mix_route(xp, m1p, m2p, xs, m1s, m2s, w_out, g, mod_all, ctx_row, req_tokens, router_w, tm=512):
    tp, ts = xp.shape[0], xs.shape[0]
    t = tp + ts
    pt = tp // tm
    rw = jnp.pad(router_w.astype(F32), ((0, 0), (0, LANES - N_EXPERTS)))
    rw_hi = rw.astype(BF16)
    rw_lo = (rw - rw_hi.astype(F32)).astype(BF16)

    def p_spec(width):
        return pl.BlockSpec((tm, width), lambda i: (jnp.minimum(i, pt - 1), 0))

    def s_spec(width):
        return pl.BlockSpec((tm, width), lambda i: (jnp.maximum(i - pt, 0), 0))

    def mod_spec(chunk):
        return _stream_mod_spec(chunk, pt, req_tokens // tm, ctx_row)

    rw_spec = pl.BlockSpec((D_MODEL, LANES), lambda i: (0, 0))
    return pl.pallas_call(
        functools.partial(_mix_route_kernel, prompt_tiles=pt),
        grid=(t // tm,),
        in_specs=[p_spec(D_MODEL), p_spec(HALF_MIX), p_spec(HALF_MIX),
                  s_spec(D_MODEL), s_spec(HALF_MIX), s_spec(HALF_MIX),
                  pl.BlockSpec((HALF_MIX, D_MODEL), lambda i: (0, 0)),
                  pl.BlockSpec((HALF_MIX, D_MODEL), lambda i: (1, 0)),
                  mod_spec(2), pl.BlockSpec((1, D_MODEL), lambda i: (0, 0)), mod_spec(3), mod_spec(4),
                  rw_spec, rw_spec],
        out_specs=[pl.BlockSpec((tm, D_MODEL), lambda i: (i, 0)),
                   pl.BlockSpec((tm, LANES), lambda i: (i, 0)),
                   pl.BlockSpec((SUBLANES, tm), lambda i: (0, i)),
                   pl.BlockSpec((SUBLANES, LANES), lambda i: (0, 0))],
        out_shape=[jax.ShapeDtypeStruct((t, D_MODEL), F32), jax.ShapeDtypeStruct((t, LANES), F32),
                   jax.ShapeDtypeStruct((SUBLANES, t), F32), jax.ShapeDtypeStruct((SUBLANES, LANES), F32)],
        scratch_shapes=[pltpu.VMEM((tm, tm), BF16), pltpu.VMEM((SUBLANES, LANES), F32)],
        compiler_params=_params("arbitrary"),
        name="mix_route",
    )(xp, m1p, m2p, xs, m1s, m2s, w_out, w_out, mod_all, g.reshape(1, D_MODEL), mod_all, mod_all,
      rw_hi, rw_lo)


def _dispatch_kernel(d0_ref, d1_ref, fill_ref, x_ref, g_ref, sh_ref, sc_ref, xs_hbm, h_scr, zero_scr, sem, zsem):
    i = pl.program_id(0)
    n = pl.num_programs(0)
    tm = x_ref.shape[0]
    slot = i % 2

    def row_copy(r, dst, s):
        return pltpu.make_async_copy(h_scr.at[s, pl.ds(r, 1), :], xs_hbm.at[pl.ds(dst, 1), :], sem.at[s])

    def wait_rows(s):
        for _ in range(2):
            pltpu.make_async_copy(h_scr.at[s], xs_hbm.at[pl.ds(0, tm), :], sem.at[s]).wait()

    @pl.when(i >= 2)
    def _():
        wait_rows(slot)

    h_scr[slot] = _norm_mod(x_ref[...], g_ref[...], sh_ref[0], sc_ref[0])
    base = i * tm

    def body(r8, c):
        rb = pl.multiple_of(r8 * SUBLANES, SUBLANES)
        for k in range(SUBLANES):
            row_copy(rb + k, d0_ref[base + rb + k], slot).start()
            row_copy(rb + k, d1_ref[base + rb + k], slot).start()
        return c

    lax.fori_loop(0, tm // SUBLANES, body, 0)

    @pl.when(i == n - 1)
    def _():
        zero_scr[...] = jnp.zeros_like(zero_scr)

        def zero_row(r):
            return pltpu.make_async_copy(zero_scr.at[pl.ds(0, 1), :], xs_hbm.at[pl.ds(r, 1), :], zsem)

        def zero_block(b):
            start = pl.multiple_of(b * MOE_ZERO_ROWS, MOE_ZERO_ROWS)
            return pltpu.make_async_copy(zero_scr, xs_hbm.at[pl.ds(start, MOE_ZERO_ROWS), :], zsem)

        def start_all(copy):
            def body(r, c):
                copy(r).start()
                return c
            return body

        def wait_all(copy):
            def body(r, c):
                copy(r).wait()
                return c
            return body

        def zero_group(b):
            start = pl.multiple_of(b * SUBLANES, SUBLANES)
            return pltpu.make_async_copy(zero_scr.at[pl.ds(0, SUBLANES), :],
                                         xs_hbm.at[pl.ds(start, SUBLANES), :], zsem)

        for e in range(N_EXPERTS):
            lo = fill_ref[e]
            hi = fill_ref[N_EXPERTS + e]
            lo_group = (lo + SUBLANES - 1) // SUBLANES
            lax.fori_loop(lo, lo_group * SUBLANES, start_all(zero_row), 0)
            lax.fori_loop(lo_group, hi // SUBLANES, start_all(zero_group), 0)
            lax.fori_loop(lo, lo_group * SUBLANES, wait_all(zero_row), 0)
            lax.fori_loop(lo_group, hi // SUBLANES, wait_all(zero_group), 0)
        blocks_per_tile = MOE_ROW_TILE // MOE_ZERO_ROWS
        first = fill_ref[2 * N_EXPERTS] * blocks_per_tile
        last = (xs_hbm.shape[0] // MOE_ROW_TILE) * blocks_per_tile
        lax.fori_loop(first, last, start_all(zero_block), 0)
        lax.fori_loop(first, last, wait_all(zero_block), 0)
        wait_rows(slot)

        @pl.when(n >= 2)
        def _():
            wait_rows(1 - slot)


def moe_dispatch(x, g, mod_all, ctx_row, prompt_tokens, req_tokens, dest0, dest1, fill, n_tiles, tm=512):
    t = x.shape[0]
    pt = prompt_tokens // tm
    grid_spec = pltpu.PrefetchScalarGridSpec(
        num_scalar_prefetch=3,
        grid=(t // tm,),
        in_specs=[pl.BlockSpec((tm, D_MODEL), lambda i, *_: (i, 0)),
                  pl.BlockSpec((1, D_MODEL), lambda i, *_: (0, 0)),
                  _stream_mod_spec(3, pt, req_tokens // tm, ctx_row),
                  _stream_mod_spec(4, pt, req_tokens // tm, ctx_row)],
        out_specs=pl.BlockSpec(memory_space=pl.ANY),
        scratch_shapes=[pltpu.VMEM((2, tm, D_MODEL), F32), pltpu.VMEM((MOE_ZERO_ROWS, D_MODEL), F32),
                        pltpu.SemaphoreType.DMA((2,)), pltpu.SemaphoreType.DMA(())],
    )
    return pl.pallas_call(
        _dispatch_kernel,
        grid_spec=grid_spec,
        out_shape=jax.ShapeDtypeStruct((n_tiles * MOE_ROW_TILE, D_MODEL), F32),
        compiler_params=pltpu.CompilerParams(dimension_semantics=("arbitrary",),
                                             vmem_limit_bytes=VMEM_LIMIT_BYTES,
                                             disable_bounds_checks=True),
        name="moe_dispatch",
    )(dest0, dest1, fill, x, g.reshape(1, D_MODEL), mod_all, mod_all)


def _experts_kernel(te_ref, tv_ref, x_ref, wg_ref, wu_ref, wd_ref, o_ref, h_scr, acc_scr):
    i = pl.program_id(0)
    f = pl.program_id(1)
    last_f = pl.num_programs(1) - 1

    n_valid = tv_ref[i]
    tile_rows = h_scr.shape[0]
    step = tile_rows // MOE_ROW_PARTS

    def swiglu_rows(n_rows):
        rows = slice(0, n_rows)

        @pl.when(f == 0)
        def _():
            h_scr[rows, :] = x_ref[rows, :].astype(BF16)
            acc_scr[rows, :] = jnp.zeros((n_rows, D_MODEL), F32)

        h = h_scr[rows, :]
        a = _dot(h, wg_ref[0].astype(BF16))
        u = _dot(h, wu_ref[0].astype(BF16))
        acc_scr[rows, :] += _dot((_silu(a) * u).astype(BF16), wd_ref[0].astype(BF16))

        @pl.when(f == last_f)
        def _():
            o_ref[rows, :] = acc_scr[rows, :]

    for part in range(1, MOE_ROW_PARTS + 1):
        n_rows = part * step

        @pl.when((n_valid > n_rows - step) & (n_valid <= n_rows))
        def _(n_rows=n_rows):
            swiglu_rows(n_rows)
            if n_rows < tile_rows:
                @pl.when(f == last_f)
                def _():
                    o_ref[n_rows:, :] = jnp.zeros((tile_rows - n_rows, D_MODEL), F32)

    @pl.when((n_valid == 0) & (f == last_f))
    def _():
        o_ref[...] = jnp.zeros_like(o_ref)


def moe_experts(xs_sorted, tile_expert, tile_valid, w_gate, w_up, w_down, tf=512):
    rows = xs_sorted.shape[0]
    fdim = w_gate.shape[2]
    n_f = fdim // tf
    tr = MOE_ROW_TILE

    def f_eff(i, f, tv):
        return jnp.where(tv[i] > 0, f, n_f - 1)

    grid_spec = pltpu.PrefetchScalarGridSpec(
        num_scalar_prefetch=2,
        grid=(rows // tr, n_f),
        in_specs=[pl.BlockSpec((tr, D_MODEL), lambda i, f, te, tv: (i, 0)),
                  pl.BlockSpec((1, D_MODEL, tf), lambda i, f, te, tv: (te[i], 0, f_eff(i, f, tv))),
                  pl.BlockSpec((1, D_MODEL, tf), lambda i, f, te, tv: (te[i], 0, f_eff(i, f, tv))),
                  pl.BlockSpec((1, tf, D_MODEL), lambda i, f, te, tv: (te[i], f_eff(i, f, tv), 0))],
        out_specs=pl.BlockSpec((tr, D_MODEL), lambda i, f, te, tv: (i, 0)),
        scratch_shapes=[pltpu.VMEM((tr, D_MODEL), BF16), pltpu.VMEM((tr, D_MODEL), F32)],
    )
    return pl.pallas_call(
        _experts_kernel,
        grid_spec=grid_spec,
        out_shape=jax.ShapeDtypeStruct((rows, D_MODEL), F32),
        compiler_params=_params("arbitrary", "arbitrary"),
        name="moe_experts",
    )(tile_expert, tile_valid, xs_sorted, w_gate, w_up, w_down)


def _combine_kernel(d0_ref, d1_ref, x_ref, info_ref, gate_ref, fg_ref, ys_hbm, op_ref, os_ref, rbuf, sem,
                    *, prompt_tiles):
    i = pl.program_id(0)
    n = pl.num_programs(0)
    tm = x_ref.shape[0]
    slot = i % 2

    def issue(tile, s, unroll):
        base = tile * tm

        def body(r8, c):
            rb = pl.multiple_of(r8 * SUBLANES, SUBLANES)
            for k in range(SUBLANES):
                pltpu.make_async_copy(ys_hbm.at[pl.ds(d0_ref[base + rb + k], 1), :],
                                      rbuf.at[s, 0, pl.ds(rb + k, 1), :], sem.at[s]).start()
                pltpu.make_async_copy(ys_hbm.at[pl.ds(d1_ref[base + rb + k], 1), :],
                                      rbuf.at[s, 1, pl.ds(rb + k, 1), :], sem.at[s]).start()
            return c

        lax.fori_loop(0, tm // SUBLANES, body, 0, unroll=unroll)

    def wait_rows(s):
        for k in range(2):
            pltpu.make_async_copy(ys_hbm.at[pl.ds(0, tm), :], rbuf.at[s, k], sem.at[s]).wait()

    @pl.when(i == 0)
    def _():
        issue(0, 0, False)

    wait_rows(slot)
    issue(jnp.where(i + 1 < n, i + 1, 0), 1 - slot, True)

    info = info_ref[...]
    moe = info[:, INFO_G0:INFO_G0 + 1] * rbuf[slot, 0] + info[:, INFO_G1:INFO_G1 + 1] * rbuf[slot, 1]
    y = x_ref[...] + gate_ref[0] * moe
    out = y * lax.rsqrt(jnp.mean(y * y, axis=-1, keepdims=True) + EPS) * fg_ref[...]

    @pl.when(i < prompt_tiles)
    def _():
        op_ref[...] = out

    @pl.when(i >= prompt_tiles)
    def _():
        os_ref[...] = out

    @pl.when(i == n - 1)
    def _():
        wait_rows(1 - slot)


def moe_combine(x, info, ys_sorted, dest0, dest1, mod_all, ctx_row, prompt_tokens, req_tokens, final_g, tm=512):
    t = x.shape[0]
    pt = prompt_tokens // tm
    grid_spec = pltpu.PrefetchScalarGridSpec(
        num_scalar_prefetch=2,
        grid=(t // tm,),
        in_specs=[pl.BlockSpec((tm, D_MODEL), lambda i, *_: (i, 0)),
                  pl.BlockSpec((tm, LANES), lambda i, *_: (i, 0)),
                  _stream_mod_spec(5, pt, req_tokens // tm, ctx_row),
                  pl.BlockSpec((1, D_MODEL), lambda i, *_: (0, 0)),
                  pl.BlockSpec(memory_space=pl.ANY)],
        out_specs=[pl.BlockSpec((tm, D_MODEL), lambda i, *_: (jnp.minimum(i, pt - 1), 0)),
                   pl.BlockSpec((tm, D_MODEL), lambda i, *_: (jnp.maximum(i - pt, 0), 0))],
        scratch_shapes=[pltpu.VMEM((2, 2, tm, D_MODEL), F32), pltpu.SemaphoreType.DMA((2,))],
    )
    return pl.pallas_call(
        functools.partial(_combine_kernel, prompt_tiles=pt),
        grid_spec=grid_spec,
        out_shape=[jax.ShapeDtypeStruct((prompt_tokens, D_MODEL), F32),
                   jax.ShapeDtypeStruct((t - prompt_tokens, D_MODEL), F32)],
        compiler_params=pltpu.CompilerParams(dimension_semantics=("arbitrary",),
                                             vmem_limit_bytes=VMEM_LIMIT_BYTES,
                                             disable_bounds_checks=True),
        name="moe_combine",
    )(dest0, dest1, x, info, mod_all, final_g.reshape(1, D_MODEL), ys_sorted)


def moe_layout(counts, fields, n_tiles):
    nt_e = (counts + MOE_ROW_TILE - 1) // MOE_ROW_TILE
    ends = jnp.cumsum(nt_e)
    total = ends[-1]
    offset = (ends - nt_e) * MOE_ROW_TILE
    experts = jnp.arange(N_EXPERTS, dtype=jnp.int32)

    def dest(e_lane, r_lane):
        e = fields[e_lane].astype(jnp.int32)
        off = jnp.sum(jnp.where(e[:, None] == experts[None, :], offset[None, :], 0), axis=1)
        return (off + fields[r_lane].astype(jnp.int32)).astype(jnp.int32)

    fill = jnp.concatenate([offset + counts, ends * MOE_ROW_TILE, total[None]]).astype(jnp.int32)
    ids = jnp.arange(n_tiles, dtype=jnp.int32)
    ids_c = jnp.minimum(ids, total - 1)
    te = jnp.sum((ids_c[:, None] >= ends[None, :]).astype(jnp.int32), axis=1)
    mine = te[:, None] == experts[None, :]
    first_tile = jnp.sum(jnp.where(mine, (ends - nt_e)[None, :], 0), axis=1)
    count = jnp.sum(jnp.where(mine, counts[None, :], 0), axis=1)
    rows_left = jnp.clip(count - (ids - first_tile) * MOE_ROW_TILE, 0, MOE_ROW_TILE)
    tile_rows = jnp.where(ids < total, rows_left, 0)
    return (dest(INFO_E0, INFO_R0), dest(INFO_E1, INFO_R1), fill, te.astype(jnp.int32),
            tile_rows.astype(jnp.int32))


def moe_final(x, info, fields, cnt, g, mod_all, ctx_row, prompt_tokens, req_tokens, final_g, w_gate, w_up,
              w_down):
    t = x.shape[0]
    counts = cnt[0, :N_EXPERTS].astype(jnp.int32)
    n_tiles = (2 * t) // MOE_ROW_TILE + N_EXPERTS
    dest0, dest1, fill, te, tv = moe_layout(counts, fields, n_tiles)
    xs_sorted = moe_dispatch(x, g, mod_all, ctx_row, prompt_tokens, req_tokens, dest0, dest1, fill, n_tiles)
    ys_sorted = moe_experts(xs_sorted, te, tv, w_gate, w_up, w_down)
    return moe_combine(x, info, ys_sorted, dest0, dest1, mod_all, ctx_row, prompt_tokens, req_tokens, final_g)


def kernel(x_prompt, x_sample, state_s5, cache_na_k, cache_na_v, cache_diff_k, cache_diff_v, c, c_ctx, w_mod, b_mod, norm_mix_g, norm_ffn_g, final_norm_g, w_in_e, w_out_e, s5_lam_re, s5_lam_im, s5_log_dt, s5_b_re, s5_b_im, s5_c_re, s5_c_im, s5_d, s5_w_glu, s5_b_glu, na_rpb, ffn_w_gate, ffn_w_up, ffn_w_down, w_in_o, w_out_o, diff_lam_q1, diff_lam_k1, diff_lam_q2, diff_lam_k2, diff_subln_g, conv_w, conv_b, conv_ln_g, conv_ln_b, router_w, moe_w_gate, moe_w_up, moe_w_down):
    bp, lp, d = x_prompt.shape
    bs, ls, _ = x_sample.shape
    tm = 1024
    xp = x_prompt.reshape(bp * lp, d)
    xs = x_sample.reshape(bs * ls, d)
    rows_p = (bp * lp) // tm
    rows_s = ls // tm

    cond8 = jnp.concatenate([c, c_ctx[None, :], jnp.zeros((SUBLANES - bs - 1, d), F32)], axis=0)
    mod = adaln_all(cond8, w_mod, b_mod)
    mod_s = mod[:, 0:bs]
    mod_p = mod[:, bs:bs + 1]

    def tiles(rows, tile):
        return rows * tm // tile

    bmat, cmat, lam8 = s5_params(s5_lam_re[0], s5_lam_im[0], s5_log_dt[0], s5_b_re[0], s5_b_im[0],
                                 s5_c_re[0], s5_c_im[0])
    bias = na_bias_blocks(na_rpb[0])
    n_e = w_in_e.shape[-1]

    tiles_p = tiles(rows_p, 512)
    tiles_s = tiles(rows_s, 512)
    proj_p = in_proj(xp, norm_mix_g[0], mod_p[0], tiles_p, w_in_e[0])
    u_s, qkv_s = in_proj(xs, norm_mix_g[0], mod_s[0], tiles_s, w_in_e[0], n_f32=HALF_MIX)

    y_p, st_p = s5_scan(proj_p.reshape(bp, lp, n_e), bmat, cmat, lam8, None, 1)
    chunks = SUBLANES // bs
    h0 = state_s5[:, 0].reshape(bs, 2, 2, S5_GROUPS * S5_STATE)
    y_s, _ = s5_scan(u_s.reshape(bs * chunks, ls // chunks, HALF_MIX), bmat, cmat, lam8, h0, chunks)
    nao_p, na_k, na_v = na_ctx(proj_p.reshape(bp, lp, n_e))

    def heads_to_lanes(cache):
        return cache.transpose(0, 2, 1, 3).reshape(bs, cache.shape[2], HALF_MIX).astype(BF16)

    nao_s = na_lat(qkv_s.reshape(bs, ls, n_e - HALF_MIX), heads_to_lanes(cache_na_k[:, 0]),
                   heads_to_lanes(cache_na_v[:, 0]), bias)

    glu_w = (s5_d[0], s5_w_glu[0].astype(BF16), s5_b_glu[0])
    ffn_w = (w_out_e[0].astype(BF16), norm_ffn_g[0])
    ffn_w3 = (ffn_w_gate[0].astype(BF16), ffn_w_up[0].astype(BF16), ffn_w_down[0].astype(BF16))
    xp = mix_ffn(xp, y_p.reshape(bp * lp, HALF_MIX), proj_p, *glu_w, nao_p.reshape(bp * lp, HALF_MIX),
                 *ffn_w, mod_p[0], tiles_p, *ffn_w3)
    xs = mix_ffn(xs, y_s.reshape(bs * ls, HALF_MIX), u_s, *glu_w, nao_s.reshape(bs * ls, HALF_MIX),
                 *ffn_w, mod_s[0], tiles_s, *ffn_w3)

    lam_init = 0.8 - 0.6 * math.exp(-0.3 * 1)
    lam = (jnp.exp(jnp.sum(diff_lam_q1[0].astype(F32) * diff_lam_k1[0].astype(F32)))
           - jnp.exp(jnp.sum(diff_lam_q2[0].astype(F32) * diff_lam_k2[0].astype(F32)))
           + lam_init)
    cos, sin = rope_tables(ls)
    n_o = w_in_o.shape[-1]

    proj_p = in_proj(xp, norm_mix_g[1], mod_p[1], tiles_p, w_in_o[0])
    proj_s = in_proj(xs, norm_mix_g[1], mod_s[1], tiles_s, w_in_o[0])

    do_p, diff_k, diff_v = diff_ctx(proj_p.reshape(bp, lp, n_o), lam, diff_subln_g[0], lam_init)
    do_s = diff_lat(proj_s.reshape(bs, ls, n_o), cache_diff_k[:, 0:1], cache_diff_v[:, 0:1], cos, sin,
                    lam, diff_subln_g[0], lam_init)
    co_p = conformer_conv(proj_p.reshape(bp, lp, n_o), conv_w[0], conv_b[0], conv_ln_g[0], conv_ln_b[0], lp)
    co_s = conformer_conv(proj_s.reshape(bs, ls, n_o), conv_w[0], conv_b[0], conv_ln_g[0], conv_ln_b[0], 512)

    mod_all = mod[1]
    x_all, info, fields, cnt = mix_route(
        xp, do_p.reshape(bp * lp, HALF_MIX), co_p.reshape(bp * lp, HALF_MIX),
        xs, do_s.reshape(bs * ls, HALF_MIX), co_s.reshape(bs * ls, HALF_MIX),
        w_out_o[0].astype(BF16), norm_ffn_g[1], mod_all, bs, ls, router_w[0])
    yp, ys = moe_final(x_all, info, fields, cnt, norm_ffn_g[1], mod_all, bs, bp * lp, ls, final_norm_g,
                       moe_w_gate[0], moe_w_up[0], moe_w_down[0])

    new_state = st_p.reshape(bp, 1, 2, 2, S5_GROUPS, S5_STATE)
    return (yp.reshape(bp, lp, d), ys.reshape(bs, ls, d), new_state, na_k, na_v, diff_k, diff_v)
```

```python
import functools
import math

import jax
import jax.numpy as jnp
import numpy as np
from jax import lax
from jax.experimental import pallas as pl
from jax.experimental.pallas import tpu as pltpu

D_MODEL = 1024
DEPTH = 2
GRID_W = 64
HALF_MIX = 512
S5_GROUP_CH = 16
S5_GROUPS = 32
S5_STATE = 64
NA_HEAD_DIM = 64
NA_HEADS = 8
NA_WIN_R = 8
NA_WIN_C = 16
DIFF_D = 64
DIFF_HEAD_DIM = 128
DIFF_HEADS = 4
ROPE_BASE = 10000.0
CONV_WIDTH = 31
N_EXPERTS = 8
EPS = 1e-6

F32 = jnp.float32
BF16 = jnp.bfloat16
NEG_BIG = -1e30

VMEM_LIMIT_BYTES = 56 * 1024 * 1024
LANES = 128
SUBLANES = 8

S5_COL_GROUPS = 8
S5_COL_CH = S5_COL_GROUPS * S5_GROUP_CH
S5_COL_STATE = S5_COL_GROUPS * S5_STATE
S5_N_COL = S5_GROUPS // S5_COL_GROUPS
S5_TIME_BLOCK = 128


def _params(*sem):
    return pltpu.CompilerParams(dimension_semantics=sem, vmem_limit_bytes=VMEM_LIMIT_BYTES)


def _dot(a, b):
    return jnp.dot(a, b, preferred_element_type=F32)


def _dot_nt(a, b):
    return lax.dot_general(a, b, (((1,), (1,)), ((), ())), preferred_element_type=F32)


def _silu(x):
    return x * jax.nn.sigmoid(x)


def _norm_mod(x, g, shift, scale):
    y = x * lax.rsqrt(jnp.mean(x * x, axis=-1, keepdims=True) + EPS) * g
    return y * (1.0 + scale) + shift


def _mod_kernel(cond_ref, w_ref, b_ref, o_ref):
    s = _silu(cond_ref[...])
    o_ref[0, :, 0, :] = jnp.dot(s, w_ref[0], precision=lax.Precision.HIGHEST,
                                preferred_element_type=F32) + b_ref[0]


def adaln_all(cond8, w_mod, b_mod):
    tn = 1536
    n = w_mod.shape[-1]
    return pl.pallas_call(
        _mod_kernel,
        grid=(DEPTH, n // tn),
        in_specs=[pl.BlockSpec((SUBLANES, D_MODEL), lambda l, j: (0, 0)),
                  pl.BlockSpec((1, D_MODEL, tn), lambda l, j: (l, 0, j)),
                  pl.BlockSpec((1, 1, tn), lambda l, j: (l, 0, j))],
        out_specs=pl.BlockSpec((1, SUBLANES, 1, tn), lambda l, j: (l, 0, 0, j)),
        out_shape=jax.ShapeDtypeStruct((DEPTH, SUBLANES, 1, n), F32),
        compiler_params=_params("arbitrary", "arbitrary"),
        name="adaln_mod",
    )(cond8, w_mod, b_mod.reshape(DEPTH, 1, n))


def _mod_spec(chunk, tiles_per_row):
    return pl.BlockSpec((1, 1, D_MODEL), lambda i, *_: (i // tiles_per_row, 0, chunk))


def _in_proj_kernel(x_ref, g_ref, sh_ref, sc_ref, w_ref, *refs, n_f32):
    *outs, wb_scr = refs

    @pl.when(pl.program_id(0) == 0)
    def _():
        wb_scr[...] = w_ref[...].astype(BF16)

    h = _norm_mod(x_ref[...], g_ref[...], sh_ref[0], sc_ref[0]).astype(BF16)
    y = _dot(h, wb_scr[...])
    if len(outs) == 1:
        outs[0][...] = y
    else:
        outs[0][...] = y[:, :n_f32]
        outs[1][...] = y[:, n_f32:].astype(BF16)


def in_proj(x, g, mod, tiles_per_row, w, n_f32=None, tm=512):
    t = x.shape[0]
    n = w.shape[1]
    if n_f32 is None:
        out_specs = pl.BlockSpec((tm, n), lambda i: (i, 0))
        out_shape = jax.ShapeDtypeStruct((t, n), F32)
    else:
        out_specs = [pl.BlockSpec((tm, n_f32), lambda i: (i, 0)), pl.BlockSpec((tm, n - n_f32), lambda i: (i, 0))]
        out_shape = [jax.ShapeDtypeStruct((t, n_f32), F32), jax.ShapeDtypeStruct((t, n - n_f32), BF16)]
    return pl.pallas_call(
        functools.partial(_in_proj_kernel, n_f32=n_f32),
        grid=(t // tm,),
        in_specs=[pl.BlockSpec((tm, D_MODEL), lambda i: (i, 0)),
                  pl.BlockSpec((1, D_MODEL), lambda i: (0, 0)),
                  _mod_spec(0, tiles_per_row),
                  _mod_spec(1, tiles_per_row),
                  pl.BlockSpec((D_MODEL, n), lambda i: (0, 0), pipeline_mode=pl.Buffered(1))],
        out_specs=out_specs,
        out_shape=out_shape,
        scratch_shapes=[pltpu.VMEM((D_MODEL, n), BF16)],
        compiler_params=_params("arbitrary"),
        name="in_proj",
    )(x, g.reshape(1, D_MODEL), mod, mod, w)


def _s5_scan_kernel(*refs, seq, chunks, has_init):
    if has_init:
        u_ref, bm_ref, cm_ref, lam_ref, h0_ref, y_ref, st_ref, bu_scr, hb_scr, ytm_scr = refs
    else:
        u_ref, bm_ref, cm_ref, lam_ref, y_ref, st_ref, bu_scr, hb_scr, ytm_scr = refs
        h0_ref = None
    tb = S5_TIME_BLOCK
    n_tb = seq // tb
    ns = S5_COL_STATE
    row = lax.broadcasted_iota(jnp.int32, (SUBLANES, ns), 0)
    piece = row % chunks

    for d in range(2):
        lam = lam_ref[d, 0]
        lr, li = lam[:, :ns], lam[:, ns:]
        bm = bm_ref[d, 0]
        cm = cm_ref[d, 0]
        blocks = list(range(n_tb)) if d == 0 else list(range(n_tb - 1, -1, -1))

        def project(k, buf):
            ub = u_ref[:, k * tb:(k + 1) * tb, :]
            utm = jnp.swapaxes(ub, 0, 1).reshape(tb * SUBLANES, S5_COL_CH).astype(BF16)
            bu_scr[buf] = _dot(utm, bm).reshape(tb, SUBLANES, 2 * ns)

        def scan_block(h, buf, store):
            def advance(t, hr, hi):
                b = bu_scr[buf, t]
                return lr * hr - li * hi + b[:, :ns], lr * hi + li * hr + b[:, ns:]

            def step(s, carry):
                t_a = (tb - 1 - 2 * s) if d == 1 else 2 * s
                t_b = t_a - 1 if d == 1 else t_a + 1
                ar, ai = advance(t_a, *carry)
                br, bi = advance(t_b, ar, ai)
                if store:
                    first = jnp.concatenate([br, bi] if d == 1 else [ar, ai], axis=1)
                    second = jnp.concatenate([ar, ai] if d == 1 else [br, bi], axis=1)
                    t_lo = t_b if d == 1 else t_a
                    hb_scr[buf, t_lo * SUBLANES:(t_lo + 2) * SUBLANES, :] = (
                        jnp.concatenate([first, second], axis=0).astype(BF16))
                return br, bi

            if not store:
                return lax.fori_loop(0, tb // 2, step, h)
            for s in range(tb // 2):
                h = step(s, h)
            return h

        def run_pass(h, store):
            project(blocks[0], 0)
            for j, k in enumerate(blocks):
                buf = j % 2
                if j + 1 < n_tb:
                    project(blocks[j + 1], 1 - buf)
                h = scan_block(h, buf, store)
                if store:
                    yb = _dot(hb_scr[buf], cm).reshape(tb, SUBLANES, S5_COL_CH)
                    if d == 0:
                        ytm_scr[k * tb:(k + 1) * tb] = yb
                    else:
                        ytm_scr[k * tb:(k + 1) * tb] += yb
            return h

        zero = jnp.zeros((SUBLANES, ns), F32)
        if chunks > 1:
            fr, fi = run_pass((zero, zero), False)
            pr, pi = lr, li
            for _ in range(int(math.log2(seq))):
                pr, pi = pr * pr - pi * pi, 2.0 * pr * pi
            edge = 0 if d == 0 else chunks - 1
            shift = 1 if d == 0 else SUBLANES - 1
            if has_init:
                h0r = h0_ref[:, d, 0, :]
                h0i = h0_ref[:, d, 1, :]
                seq_of_row = row // chunks
                er, ei = zero, zero
                for b in range(SUBLANES // chunks):
                    er = jnp.where(seq_of_row == b, h0r[b:b + 1, :], er)
                    ei = jnp.where(seq_of_row == b, h0i[b:b + 1, :], ei)
            else:
                er, ei = zero, zero
            is_edge = piece == edge
            cr = jnp.where(is_edge, er, zero)
            ci = jnp.where(is_edge, ei, zero)
            for _ in range(chunks - 1):
                tr = fr + pr * cr - pi * ci
                ti = fi + pr * ci + pi * cr
                cr = jnp.where(is_edge, er, pltpu.roll(tr, shift, 0))
                ci = jnp.where(is_edge, ei, pltpu.roll(ti, shift, 0))
            h = (cr, ci)
        else:
            if has_init:
                h = (h0_ref[:, d, 0, :], h0_ref[:, d, 1, :])
            else:
                h = (zero, zero)

        h = run_pass(h, True)
        st_ref[:, d, 0, :] = h[0]
        st_ref[:, d, 1, :] = h[1]

    y_ref[...] = jnp.swapaxes(ytm_scr[...], 0, 1)


def s5_scan(proj3, bmat, cmat, lam8, h0, chunks):
    rows, seq, _ = proj3.shape
    ns = S5_COL_STATE
    has_init = h0 is not None
    in_specs = [pl.BlockSpec((SUBLANES, seq, S5_COL_CH), lambda i, c: (i, 0, c)),
                pl.BlockSpec((2, 1, S5_COL_CH, 2 * ns), lambda i, c: (0, c, 0, 0)),
                pl.BlockSpec((2, 1, 2 * ns, S5_COL_CH), lambda i, c: (0, c, 0, 0)),
                pl.BlockSpec((2, 1, SUBLANES, 2 * ns), lambda i, c: (0, c, 0, 0))]
    args = [proj3, bmat, cmat, lam8]
    if has_init:
        nb = h0.shape[0]
        in_specs.append(pl.BlockSpec((nb, 2, 2, ns), lambda i, c: (0, 0, 0, c)))
        args.append(h0)
    y, st = pl.pallas_call(
        functools.partial(_s5_scan_kernel, seq=seq, chunks=chunks, has_init=has_init),
        grid=(rows // SUBLANES, S5_N_COL),
        in_specs=in_specs,
        out_specs=[pl.BlockSpec((SUBLANES, seq, S5_COL_CH), lambda i, c: (i, 0, c)),
                   pl.BlockSpec((SUBLANES, 2, 2, ns), lambda i, c: (i, 0, 0, c))],
        out_shape=[jax.ShapeDtypeStruct((rows, seq, HALF_MIX), F32),
                   jax.ShapeDtypeStruct((rows, 2, 2, S5_GROUPS * S5_STATE), F32)],
        scratch_shapes=[pltpu.VMEM((2, S5_TIME_BLOCK, SUBLANES, 2 * ns), F32),
                        pltpu.VMEM((2, S5_TIME_BLOCK * SUBLANES, 2 * ns), BF16),
                        pltpu.VMEM((seq, SUBLANES, S5_COL_CH), F32)],
        compiler_params=_params("arbitrary", "arbitrary"),
        name="s5_scan",
    )(*args)
    return y, st


def s5_params(lam_re, lam_im, log_dt, b_re, b_im, c_re, c_im):
    lr = lam_re.astype(F32)
    li = lam_im.astype(F32)
    dt = jnp.exp(log_dt.astype(F32))[..., None]
    mag = jnp.exp(lr * dt)
    bar_re = mag * jnp.cos(li * dt)
    bar_im = mag * jnp.sin(li * dt)
    den = lr * lr + li * li
    q_re = ((bar_re - 1.0) * lr + bar_im * li) / den
    q_im = (bar_im * lr - (bar_re - 1.0) * li) / den
    br = b_re.astype(F32)
    bi = b_im.astype(F32)
    b_bar_re = q_re[..., None] * br - q_im[..., None] * bi
    b_bar_im = q_re[..., None] * bi + q_im[..., None] * br
    eye = jnp.eye(S5_COL_GROUPS, dtype=F32)

    def block_diag_b(m):
        m = m.reshape(2, S5_N_COL, S5_COL_GROUPS, S5_STATE, S5_GROUP_CH)
        bd = jnp.einsum('dngpc,gh->dngchp', m, eye)
        return bd.reshape(2, S5_N_COL, S5_COL_CH, S5_COL_STATE)

    def block_diag_c(m):
        m = m.reshape(2, S5_N_COL, S5_COL_GROUPS, S5_GROUP_CH, S5_STATE)
        bd = jnp.einsum('dngcp,gh->dngphc', m, eye)
        return bd.reshape(2, S5_N_COL, S5_COL_STATE, S5_COL_CH)

    bmat = jnp.concatenate([block_diag_b(b_bar_re), block_diag_b(b_bar_im)], axis=-1).astype(BF16)
    cmat = jnp.concatenate([block_diag_c(c_re.astype(F32)), block_diag_c(-c_im.astype(F32))],
                           axis=-2).astype(BF16)
    lam_cat = jnp.concatenate([bar_re.reshape(2, S5_N_COL, S5_COL_STATE),
                               bar_im.reshape(2, S5_N_COL, S5_COL_STATE)], axis=-1)
    lam8 = jnp.broadcast_to(lam_cat[:, :, None, :], (2, S5_N_COL, SUBLANES, 2 * S5_COL_STATE))
    return bmat, cmat, lam8


def _na_ctx_kernel(q_ref, k_ref, v_ref, o_ref, ko_ref, vo_ref):
    seq = q_ref.shape[1]
    lane = lax.broadcasted_iota(jnp.int32, (seq, LANES), 1)
    low = lane < NA_HEAD_DIM
    outs = []
    for pr in range(NA_HEADS // 2):
        cols = slice(pr * LANES, (pr + 1) * LANES)
        qp = q_ref[0, :, cols] * (NA_HEAD_DIM ** -0.5)
        kp = k_ref[0, :, cols]
        vp = v_ref[0, :, cols]
        for half in range(2):
            sl = slice(half * NA_HEAD_DIM, (half + 1) * NA_HEAD_DIM)
            ko_ref[0, 0, 2 * pr + half] = kp[:, sl]
            vo_ref[0, 0, 2 * pr + half] = vp[:, sl]
        kb = kp.astype(BF16)
        vb = vp.astype(BF16)
        o_pair = None
        for half in range(2):
            qm = jnp.where(low if half == 0 else jnp.logical_not(low), qp, 0.0).astype(BF16)
            s = _dot_nt(qm, kb)
            p = jnp.exp(s - jnp.max(s, axis=-1, keepdims=True))
            o = _dot(p.astype(BF16), vb) * (1.0 / jnp.sum(p, axis=-1, keepdims=True))
            o_pair = o if half == 0 else jnp.where(low, o_pair, o)
        outs.append(o_pair)
    o_ref[0] = jnp.concatenate(outs, axis=-1).astype(o_ref.dtype)


def na_ctx(proj3):
    b, seq, _ = proj3.shape
    cache_shape = jax.ShapeDtypeStruct((b, 1, NA_HEADS, seq, NA_HEAD_DIM), F32)
    cache_spec = pl.BlockSpec((1, 1, NA_HEADS, seq, NA_HEAD_DIM), lambda i: (i, 0, 0, 0, 0))
    return pl.pallas_call(
        _na_ctx_kernel,
        grid=(b,),
        in_specs=[pl.BlockSpec((1, seq, HALF_MIX), lambda i: (i, 0, 1)),
                  pl.BlockSpec((1, seq, HALF_MIX), lambda i: (i, 0, 2)),
                  pl.BlockSpec((1, seq, HALF_MIX), lambda i: (i, 0, 3))],
        out_specs=[pl.BlockSpec((1, seq, HALF_MIX), lambda i: (i, 0, 0)), cache_spec, cache_spec],
        out_shape=[jax.ShapeDtypeStruct((b, seq, HALF_MIX), BF16), cache_shape, cache_shape],
        compiler_params=_params("arbitrary"),
        name="na_ctx",
    )(proj3, proj3, proj3)


NA_Q_ROWS = 4
NA_KEY_ROWS = 12


def na_bias_blocks(rpb):
    qcol = np.arange(GRID_W)
    cc = np.arange(GRID_W)
    cs = np.clip(qcol - NA_WIN_C // 2, 0, GRID_W - NA_WIN_C)
    valid = (cc[None, :] >= cs[:, None]) & (cc[None, :] < cs[:, None] + NA_WIN_C)
    coff = cc[None, :] - qcol[:, None] + (NA_WIN_C - 1)
    n_col = 2 * NA_WIN_C - 1
    sel = ((coff[None] == np.arange(n_col)[:, None, None]) & valid[None]).astype(np.float32)
    sel = sel.reshape(n_col, GRID_W * GRID_W)
    mask = np.where(valid, 0.0, NEG_BIG).astype(np.float32).reshape(1, GRID_W * GRID_W)
    n_row = 2 * NA_WIN_R - 1
    t1 = jnp.dot(rpb.astype(F32).reshape(NA_HEADS * n_row, n_col), jnp.asarray(sel),
                 precision=lax.Precision.HIGHEST) + jnp.asarray(mask)
    t1 = t1.reshape(NA_HEADS, n_row, GRID_W, GRID_W)
    return pl.pallas_call(
        _na_bias_kernel,
        grid=(NA_HEADS,),
        in_specs=[pl.BlockSpec((1, n_row, GRID_W, GRID_W), lambda h: (h, 0, 0, 0))],
        out_specs=pl.BlockSpec((3, 1, NA_Q_ROWS * GRID_W, NA_KEY_ROWS * GRID_W), lambda h: (0, h, 0, 0)),
        out_shape=jax.ShapeDtypeStruct((3, NA_HEADS, NA_Q_ROWS * GRID_W, NA_KEY_ROWS * GRID_W), F32),
        compiler_params=_params("arbitrary"),
        name="na_bias",
    )(t1)


def _na_bias_kernel(t1_ref, o_ref):
    variants = ((lambda ri: 0, NA_WIN_R - 1), (lambda ri: ri, NA_WIN_R // 2 - 1),
                (lambda ri: NA_KEY_ROWS - NA_WIN_R, -1))
    outside = jnp.full((GRID_W, GRID_W), NEG_BIG, F32)
    for v, (lo_of, shift) in enumerate(variants):
        for ri in range(NA_Q_ROWS):
            for wr in range(NA_KEY_ROWS):
                inside = lo_of(ri) <= wr < lo_of(ri) + NA_WIN_R
                o_ref[v, 0, ri * GRID_W:(ri + 1) * GRID_W, wr * GRID_W:(wr + 1) * GRID_W] = (
                    t1_ref[0, wr - ri + shift] if inside else outside)


def _na_lat_kernel(q_ref, k_ref, v_ref, kc_ref, vc_ref, bias_ref, o_ref):
    qb = pl.program_id(1)
    rows = k_ref.shape[1] // GRID_W
    nk = NA_KEY_ROWS * GRID_W
    first_row = jnp.clip(qb * NA_Q_ROWS - NA_WIN_R // 2, 0, rows - NA_KEY_ROWS)
    start = pl.multiple_of(first_row * GRID_W, GRID_W)
    tq = q_ref.shape[1]
    lane = lax.broadcasted_iota(jnp.int32, (tq, LANES), 1)
    low = lane < NA_HEAD_DIM
    outs = []
    for pr in range(NA_HEADS // 2):
        cols = slice(pr * LANES, (pr + 1) * LANES)
        qp = q_ref[0, :, cols].astype(F32) * (NA_HEAD_DIM ** -0.5)
        kw = k_ref[0, pl.ds(start, nk), cols]
        vw = v_ref[0, pl.ds(start, nk), cols]
        kc = kc_ref[0, :, cols]
        vc = vc_ref[0, :, cols]
        o_pair = None
        for half in range(2):
            qm = jnp.where(low if half == 0 else jnp.logical_not(low), qp, 0.0).astype(BF16)
            s_loc = _dot_nt(qm, kw) + bias_ref[0, 2 * pr + half]
            s_ctx = _dot_nt(qm, kc)
            m = jnp.maximum(jnp.max(s_loc, axis=-1, keepdims=True), jnp.max(s_ctx, axis=-1, keepdims=True))
            p_loc = jnp.exp(s_loc - m)
            p_ctx = jnp.exp(s_ctx - m)
            inv = 1.0 / (jnp.sum(p_loc, axis=-1, keepdims=True) + jnp.sum(p_ctx, axis=-1, keepdims=True))
            o = (_dot(p_loc.astype(BF16), vw) + _dot(p_ctx.astype(BF16), vc)) * inv
            o_pair = o if half == 0 else jnp.where(low, o_pair, o)
        outs.append(o_pair)
    o_ref[0] = jnp.concatenate(outs, axis=-1).astype(o_ref.dtype)


def na_lat(qkv3, k_ctx, v_ctx, bias):
    b, seq, _ = qkv3.shape
    tq = NA_Q_ROWS * GRID_W
    n_q = seq // tq
    lc = k_ctx.shape[1]
    ctx_spec = pl.BlockSpec((1, lc, HALF_MIX), lambda i, r: (i, 0, 0))
    return pl.pallas_call(
        _na_lat_kernel,
        grid=(b, n_q),
        in_specs=[pl.BlockSpec((1, tq, HALF_MIX), lambda i, r: (i, r, 0)),
                  pl.BlockSpec((1, seq, HALF_MIX), lambda i, r: (i, 0, 1)),
                  pl.BlockSpec((1, seq, HALF_MIX), lambda i, r: (i, 0, 2)),
                  ctx_spec, ctx_spec,
                  pl.BlockSpec((1, NA_HEADS, tq, NA_KEY_ROWS * GRID_W),
                               lambda i, r: (jnp.where(r == 0, 0, jnp.where(r == n_q - 1, 2, 1)), 0, 0, 0))],
        out_specs=pl.BlockSpec((1, tq, HALF_MIX), lambda i, r: (i, r, 0)),
        out_shape=jax.ShapeDtypeStruct((b, seq, HALF_MIX), BF16),
        compiler_params=_params("arbitrary", "arbitrary"),
        name="na_lat",
    )(qkv3, qkv3, qkv3, k_ctx, v_ctx, bias)


def _softmax_pair_diff(s1, s2, lam):
    p1 = jnp.exp(s1 - jnp.max(s1, axis=-1, keepdims=True))
    p2 = jnp.exp(s2 - jnp.max(s2, axis=-1, keepdims=True))
    inv1 = 1.0 / jnp.sum(p1, axis=-1, keepdims=True)
    inv2 = lam / jnp.sum(p2, axis=-1, keepdims=True)
    return p1 * inv1 - p2 * inv2


def _sub_ln(o, g, lam_init):
    return o * lax.rsqrt(jnp.mean(o * o, axis=-1, keepdims=True) + EPS) * g * (1.0 - lam_init)


def _diff_ctx_kernel(lam_ref, q_ref, k_ref, v_ref, g_ref, o_ref, ko_ref, vo_ref, *, lam_init):
    scale = DIFF_D ** -0.5
    lam = lam_ref[0, 0]
    lane = lax.broadcasted_iota(jnp.int32, (q_ref.shape[1], DIFF_HEAD_DIM), 1)
    first = lane < DIFF_D
    for h in range(DIFF_HEADS):
        sl = slice(h * DIFF_HEAD_DIM, (h + 1) * DIFF_HEAD_DIM)
        qh = q_ref[0, :, sl]
        kh = k_ref[0, :, sl]
        vh = v_ref[0, :, sl]
        ko_ref[0, 0, h] = kh
        vo_ref[0, 0, h] = vh
        kb = kh.astype(BF16)
        s1 = _dot_nt(jnp.where(first, qh, 0.0).astype(BF16), kb) * scale
        s2 = _dot_nt(jnp.where(first, 0.0, qh).astype(BF16), kb) * scale
        a = _softmax_pair_diff(s1, s2, lam)
        o = _dot(a.astype(BF16), vh.astype(BF16))
        o_ref[0, :, sl] = _sub_ln(o, g_ref[...], lam_init).astype(o_ref.dtype)


def diff_ctx(proj3, lam, subln_g, lam_init):
    b, seq, _ = proj3.shape
    cache_shape = jax.ShapeDtypeStruct((b, 1, DIFF_HEADS, seq, DIFF_HEAD_DIM), F32)
    cache_spec = pl.BlockSpec((1, 1, DIFF_HEADS, seq, DIFF_HEAD_DIM), lambda i: (i, 0, 0, 0, 0))
    return pl.pallas_call(
        functools.partial(_diff_ctx_kernel, lam_init=lam_init),
        grid=(b,),
        in_specs=[pl.BlockSpec(memory_space=pltpu.SMEM),
                  pl.BlockSpec((1, seq, HALF_MIX), lambda i: (i, 0, 0)),
                  pl.BlockSpec((1, seq, HALF_MIX), lambda i: (i, 0, 1)),
                  pl.BlockSpec((1, seq, HALF_MIX), lambda i: (i, 0, 2)),
                  pl.BlockSpec((1, DIFF_HEAD_DIM), lambda i: (0, 0))],
        out_specs=[pl.BlockSpec((1, seq, HALF_MIX), lambda i: (i, 0, 0)), cache_spec, cache_spec],
        out_shape=[jax.ShapeDtypeStruct((b, seq, HALF_MIX), BF16), cache_shape, cache_shape],
        compiler_params=_params("arbitrary"),
        name="diff_ctx",
    )(lam.reshape(1, 1), proj3, proj3, proj3, subln_g.reshape(1, DIFF_HEAD_DIM))


def rope_tables(seq):
    t = np.arange(seq)
    row = (t // GRID_W).astype(np.float32)
    col = (t % GRID_W).astype(np.float32)
    n_freq = DIFF_D // 4
    inv = np.float32(ROPE_BASE) ** (-np.arange(n_freq, dtype=np.float32) / np.float32(n_freq))
    ang = np.concatenate([row[:, None] * inv, col[:, None] * inv], axis=-1)
    cos = np.repeat(np.cos(ang), 2, axis=-1)
    sin = np.repeat(np.sin(ang), 2, axis=-1)
    sign = np.where(np.arange(DIFF_D) % 2 == 0, -1.0, 1.0).astype(np.float32)
    sin = sin * sign
    return (jnp.asarray(np.tile(cos, (1, 2)).astype(np.float32)),
            jnp.asarray(np.tile(sin, (1, 2)).astype(np.float32)))


def _rope(x, cos, sin_signed):
    lane = lax.broadcasted_iota(jnp.int32, x.shape, 1)
    nxt = pltpu.roll(x, x.shape[1] - 1, 1)
    prv = pltpu.roll(x, 1, 1)
    partner = jnp.where(lane % 2 == 0, nxt, prv)
    return x * cos + partner * sin_signed


def _diff_lat_kernel(lam_ref, q_ref, k_ref, v_ref, kc_ref, vc_ref, cq_ref, sq_ref, ck_ref, sk_ref,
                     g_ref, o_ref, k_all, v_all, *, lam_init):
    seq = k_ref.shape[1]

    @pl.when(pl.program_id(2) == 0)
    def _():
        k_all[0:seq, :] = _rope(k_ref[0], ck_ref[...], sk_ref[...]).astype(BF16)
        k_all[seq:, :] = kc_ref[0, 0, 0].astype(BF16)
        v_all[0:seq, :] = v_ref[0].astype(BF16)
        v_all[seq:, :] = vc_ref[0, 0, 0].astype(BF16)

    lam = lam_ref[0, 0]
    q = _rope(q_ref[0], cq_ref[...], sq_ref[...]) * (DIFF_D ** -0.5)
    lane = lax.broadcasted_iota(jnp.int32, q.shape, 1)
    first = lane < DIFF_D
    kb = k_all[...]
    s1 = _dot_nt(jnp.where(first, q, 0.0).astype(BF16), kb)
    s2 = _dot_nt(jnp.where(first, 0.0, q).astype(BF16), kb)
    a = _softmax_pair_diff(s1, s2, lam)
    o = _dot(a.astype(BF16), v_all[...])
    o_ref[0] = _sub_ln(o, g_ref[...], lam_init).astype(o_ref.dtype)


def diff_lat(proj3, k_ctx, v_ctx, cos, sin, lam, subln_g, lam_init, tq=256):
    b, seq, _ = proj3.shape
    lc = k_ctx.shape[3]
    hd = DIFF_HEAD_DIM
    ctx_spec = pl.BlockSpec((1, 1, 1, lc, hd), lambda i, h, q: (i, 0, h, 0, 0))
    tq_spec = pl.BlockSpec((tq, hd), lambda i, h, q: (q, 0))
    full_spec = pl.BlockSpec((seq, hd), lambda i, h, q: (0, 0))
    return pl.pallas_call(
        functools.partial(_diff_lat_kernel, lam_init=lam_init),
        grid=(b, DIFF_HEADS, seq // tq),
        in_specs=[pl.BlockSpec(memory_space=pltpu.SMEM),
                  pl.BlockSpec((1, tq, hd), lambda i, h, q: (i, q, h)),
                  pl.BlockSpec((1, seq, hd), lambda i, h, q: (i, 0, DIFF_HEADS + h)),
                  pl.BlockSpec((1, seq, hd), lambda i, h, q: (i, 0, 2 * DIFF_HEADS + h)),
                  ctx_spec, ctx_spec, tq_spec, tq_spec, full_spec, full_spec,
                  pl.BlockSpec((1, hd), lambda i, h, q: (0, 0))],
        out_specs=pl.BlockSpec((1, tq, hd), lambda i, h, q: (i, q, h)),
        out_shape=jax.ShapeDtypeStruct((b, seq, HALF_MIX), BF16),
        scratch_shapes=[pltpu.VMEM((seq + lc, hd), BF16), pltpu.VMEM((seq + lc, hd), BF16)],
        compiler_params=_params("arbitrary", "arbitrary", "arbitrary"),
        name="diff_lat",
    )(lam.reshape(1, 1), proj3, proj3, proj3, k_ctx, v_ctx, cos, sin, cos, sin,
      subln_g.reshape(1, hd))


CONV_PAD = 16
CONV_SUB = 64


def _conv_kernel(a_ref, g_ref, ap_ref, gp_ref, an_ref, gn_ref, w_ref, b_ref, lg_ref, lb_ref, o_ref, xp_scr,
                 xsh_scr):
    t = pl.program_id(1)
    tt = a_ref.shape[1]
    prev = ap_ref[0] * jax.nn.sigmoid(gp_ref[0])
    nxt = an_ref[0] * jax.nn.sigmoid(gn_ref[0])
    xp_scr[0:CONV_PAD, :] = jnp.where(t > 0, prev, 0.0)
    xp_scr[CONV_PAD + tt:, :] = jnp.where(t < pl.num_programs(1) - 1, nxt, 0.0)
    xp_scr[CONV_PAD:CONV_PAD + tt, :] = a_ref[0] * jax.nn.sigmoid(g_ref[0])
    first_tap = CONV_PAD - CONV_WIDTH // 2
    n_rows = xsh_scr.shape[1]
    for b in range(SUBLANES):
        xsh_scr[b] = xp_scr[b:b + n_rows, :]
    for i in range(tt // CONV_SUB):
        s = i * CONV_SUB
        acc = jnp.zeros((CONV_SUB, HALF_MIX), F32)
        for j in range(CONV_WIDTH):
            whole, phase = divmod(first_tap + j, SUBLANES)
            lo = s + whole * SUBLANES
            acc = acc + xsh_scr[phase, lo:lo + CONV_SUB, :] * w_ref[j:j + 1, :]
        y = acc + b_ref[...]
        mu = jnp.mean(y, axis=-1, keepdims=True)
        yc = y - mu
        var = jnp.mean(yc * yc, axis=-1, keepdims=True)
        yn = yc * lax.rsqrt(var + EPS) * lg_ref[...] + lb_ref[...]
        o_ref[0, s:s + CONV_SUB, :] = _silu(yn).astype(o_ref.dtype)


def conformer_conv(proj3, w, b, ln_g, ln_b, tt):
    bsz, seq, _ = proj3.shape
    n_t = seq // tt
    hb = tt // CONV_PAD
    last = seq // CONV_PAD - 1
    vec = pl.BlockSpec((1, HALF_MIX), lambda i, t: (0, 0))

    def main(col):
        return pl.BlockSpec((1, tt, HALF_MIX), lambda i, t: (i, t, col))

    def prev(col):
        return pl.BlockSpec((1, CONV_PAD, HALF_MIX), lambda i, t: (i, jnp.maximum(t * hb - 1, 0), col))

    def nxt(col):
        return pl.BlockSpec((1, CONV_PAD, HALF_MIX), lambda i, t: (i, jnp.minimum((t + 1) * hb, last), col))

    return pl.pallas_call(
        _conv_kernel,
        grid=(bsz, n_t),
        in_specs=[main(3), main(4), prev(3), prev(4), nxt(3), nxt(4),
                  pl.BlockSpec((CONV_WIDTH, HALF_MIX), lambda i, t: (0, 0)),
                  vec, vec, vec],
        out_specs=pl.BlockSpec((1, tt, HALF_MIX), lambda i, t: (i, t, 0)),
        out_shape=jax.ShapeDtypeStruct((bsz, seq, HALF_MIX), BF16),
        scratch_shapes=[pltpu.VMEM((tt + 2 * CONV_PAD, HALF_MIX), F32),
                        pltpu.VMEM((SUBLANES, tt + 2 * CONV_PAD - SUBLANES, HALF_MIX), F32)],
        compiler_params=_params("arbitrary", "arbitrary"),
        name="conformer_conv",
    )(proj3, proj3, proj3, proj3, proj3, proj3, w, b.reshape(1, HALF_MIX), ln_g.reshape(1, HALF_MIX),
      ln_b.reshape(1, HALF_MIX))


FFN_CHUNK = 1024


def _mix_ffn_kernel(x_ref, y_ref, u_ref, d_ref, wglu_ref, bglu_ref, m2_ref, w1_ref, w2_ref, g1_ref,
                    g_ref, sh_ref, sc_ref, g2_ref, wg_ref, wu_ref, wd_ref, o_ref):
    z = jax.nn.gelu(u_ref[...] * d_ref[...] + y_ref[...])
    s5_out = z * jax.nn.sigmoid(_dot(z.astype(BF16), wglu_ref[...]) + bglu_ref[...])
    mix = _dot(s5_out.astype(BF16), w1_ref[...]) + _dot(m2_ref[...], w2_ref[...])
    x1 = x_ref[...] + g1_ref[0] * mix
    h = _norm_mod(x1, g_ref[...], sh_ref[0], sc_ref[0]).astype(BF16)
    fdim = wg_ref.shape[1]
    acc = None
    for lo in range(0, fdim, FFN_CHUNK):
        hi = min(lo + FFN_CHUNK, fdim)
        a = _dot(h, wg_ref[:, lo:hi])
        up = _dot(h, wu_ref[:, lo:hi])
        part = _dot((_silu(a) * up).astype(BF16), wd_ref[lo:hi, :])
        acc = part if acc is None else acc + part
    o_ref[...] = x1 + g2_ref[0] * acc


def mix_ffn(x, y, u, d_skip, w_glu, b_glu, m2, w_out, g, mod, tiles_per_row, w_gate, w_up, w_down, tm=512):
    t = x.shape[0]
    fdim = w_gate.shape[1]
    half = pl.BlockSpec((tm, HALF_MIX), lambda i: (i, 0))
    vec = pl.BlockSpec((1, HALF_MIX), lambda i: (0, 0))

    def resident(shape, index):
        return pl.BlockSpec(shape, index, pipeline_mode=pl.Buffered(1))

    return pl.pallas_call(
        _mix_ffn_kernel,
        grid=(t // tm,),
        in_specs=[pl.BlockSpec((tm, D_MODEL), lambda i: (i, 0)), half, half, vec,
                  resident((HALF_MIX, HALF_MIX), lambda i: (0, 0)), vec, half,
                  resident((HALF_MIX, D_MODEL), lambda i: (0, 0)),
                  resident((HALF_MIX, D_MODEL), lambda i: (1, 0)),
                  _mod_spec(2, tiles_per_row),
                  pl.BlockSpec((1, D_MODEL), lambda i: (0, 0)),
                  _mod_spec(3, tiles_per_row), _mod_spec(4, tiles_per_row), _mod_spec(5, tiles_per_row),
                  resident((D_MODEL, fdim), lambda i: (0, 0)),
                  resident((D_MODEL, fdim), lambda i: (0, 0)),
                  resident((fdim, D_MODEL), lambda i: (0, 0))],
        out_specs=pl.BlockSpec((tm, D_MODEL), lambda i: (i, 0)),
        out_shape=jax.ShapeDtypeStruct((t, D_MODEL), F32),
        compiler_params=_params("arbitrary"),
        name="mix_ffn",
    )(x, y, u, d_skip.reshape(1, HALF_MIX), w_glu, b_glu.reshape(1, HALF_MIX), m2, w_out, w_out, mod,
      g.reshape(1, D_MODEL), mod, mod, mod, w_gate, w_up, w_down)


MOE_ROW_TILE = 1024
MOE_ZERO_ROWS = 256
MOE_DISPATCH_SLOTS = 3
MOE_ROW_PARTS = 8
INFO_E0, INFO_E1, INFO_G0, INFO_G1, INFO_R0, INFO_R1 = range(6)


def _stream_mod_spec(chunk, prompt_tiles, tiles_per_req, ctx_row):
    def index(i, *_):
        return (jnp.where(i < prompt_tiles, ctx_row, (i - prompt_tiles) // tiles_per_req), 0, chunk)
    return pl.BlockSpec((1, 1, D_MODEL), index)


def _mix_route_kernel(xp_ref, m1p_ref, m2p_ref, xs_ref, m1s_ref, m2s_ref, w1_ref, w2_ref, gate_ref,
                      g_ref, sh_ref, sc_ref, rwh_ref, rwl_ref, o_ref, info_ref, fields_ref, cnt_ref,
                      tri_scr, run_scr, *, prompt_tiles):
    i = pl.program_id(0)
    tm = o_ref.shape[0]
    w1 = w1_ref[...]
    w2 = w2_ref[...]

    @pl.when(i < prompt_tiles)
    def _():
        o_ref[...] = xp_ref[...] + gate_ref[0] * (_dot(m1p_ref[...], w1) + _dot(m2p_ref[...], w2))

    @pl.when(i >= prompt_tiles)
    def _():
        o_ref[...] = xs_ref[...] + gate_ref[0] * (_dot(m1s_ref[...], w1) + _dot(m2s_ref[...], w2))

    @pl.when(i == 0)
    def _():
        r = lax.broadcasted_iota(jnp.int32, (tm, tm), 0)
        c = lax.broadcasted_iota(jnp.int32, (tm, tm), 1)
        tri_scr[...] = jnp.where(c < r, 1.0, 0.0).astype(BF16)
        run_scr[...] = jnp.zeros_like(run_scr)

    h = _norm_mod(o_ref[...], g_ref[...], sh_ref[0], sc_ref[0])
    h_hi = h.astype(BF16)
    h_lo = (h - h_hi.astype(F32)).astype(BF16)
    logits = _dot(h_hi, rwh_ref[...]) + (_dot(h_hi, rwl_ref[...]) + _dot(h_lo, rwh_ref[...]))
    lane = lax.broadcasted_iota(jnp.int32, logits.shape, 1).astype(F32)
    logits = jnp.where(lane < N_EXPERTS, logits, -jnp.inf)
    m1 = jnp.max(logits, axis=-1, keepdims=True)
    i1 = jnp.min(jnp.where(logits == m1, lane, float(LANES)), axis=-1, keepdims=True)
    rest = jnp.where(lane == i1, -jnp.inf, logits)
    m2 = jnp.max(rest, axis=-1, keepdims=True)
    i2 = jnp.min(jnp.where(rest == m2, lane, float(LANES)), axis=-1, keepdims=True)
    e2 = jnp.exp(m2 - m1)
    den = 1.0 + e2
    hit = jnp.where(lane == i1, 1.0, 0.0) + jnp.where(lane == i2, 1.0, 0.0)
    before = _dot(tri_scr[...], hit.astype(BF16)) + run_scr[0:1, :]
    r1 = jnp.sum(jnp.where(lane == i1, before, 0.0), axis=-1, keepdims=True)
    r2 = jnp.sum(jnp.where(lane == i2, before, 0.0), axis=-1, keepdims=True)
    info = jnp.zeros_like(logits)
    for slot, val in ((INFO_E0, i1), (INFO_E1, i2), (INFO_G0, 1.0 / den), (INFO_G1, e2 / den),
                      (INFO_R0, r1), (INFO_R1, r2)):
        info = jnp.where(lane == float(slot), val, info)
    info_ref[...] = info
    fields_ref[...] = jnp.transpose(info)[0:SUBLANES, :]
    run_scr[...] = run_scr[...] + jnp.sum(hit, axis=0, keepdims=True)
    cnt_ref[...] = run_scr[...]


def mix_route(xp, m1p, m2p, xs, m1s, m2s, w_out, g, mod_all, ctx_row, req_tokens, router_w, tm=512):
    tp, ts = xp.shape[0], xs.shape[0]
    t = tp + ts
    pt = tp // tm
    rw = jnp.pad(router_w.astype(F32), ((0, 0), (0, LANES - N_EXPERTS)))
    rw_hi = rw.astype(BF16)
    rw_lo = (rw - rw_hi.astype(F32)).astype(BF16)

    def p_spec(width):
        return pl.BlockSpec((tm, width), lambda i: (jnp.minimum(i, pt - 1), 0))

    def s_spec(width):
        return pl.BlockSpec((tm, width), lambda i: (jnp.maximum(i - pt, 0), 0))

    def mod_spec(chunk):
        return _stream_mod_spec(chunk, pt, req_tokens // tm, ctx_row)

    rw_spec = pl.BlockSpec((D_MODEL, LANES), lambda i: (0, 0))
    return pl.pallas_call(
        functools.partial(_mix_route_kernel, prompt_tiles=pt),
        grid=(t // tm,),
        in_specs=[p_spec(D_MODEL), p_spec(HALF_MIX), p_spec(HALF_MIX),
                  s_spec(D_MODEL), s_spec(HALF_MIX), s_spec(HALF_MIX),
                  pl.BlockSpec((HALF_MIX, D_MODEL), lambda i: (0, 0)),
                  pl.BlockSpec((HALF_MIX, D_MODEL), lambda i: (1, 0)),
                  mod_spec(2), pl.BlockSpec((1, D_MODEL), lambda i: (0, 0)), mod_spec(3), mod_spec(4),
                  rw_spec, rw_spec],
        out_specs=[pl.BlockSpec((tm, D_MODEL), lambda i: (i, 0)),
                   pl.BlockSpec((tm, LANES), lambda i: (i, 0)),
                   pl.BlockSpec((SUBLANES, tm), lambda i: (0, i)),
                   pl.BlockSpec((SUBLANES, LANES), lambda i: (0, 0))],
        out_shape=[jax.ShapeDtypeStruct((t, D_MODEL), F32), jax.ShapeDtypeStruct((t, LANES), F32),
                   jax.ShapeDtypeStruct((SUBLANES, t), F32), jax.ShapeDtypeStruct((SUBLANES, LANES), F32)],
        scratch_shapes=[pltpu.VMEM((tm, tm), BF16), pltpu.VMEM((SUBLANES, LANES), F32)],
        compiler_params=_params("arbitrary"),
        name="mix_route",
    )(xp, m1p, m2p, xs, m1s, m2s, w_out, w_out, mod_all, g.reshape(1, D_MODEL), mod_all, mod_all,
      rw_hi, rw_lo)


def _dispatch_kernel(d0_ref, d1_ref, fill_ref, x_ref, g_ref, sh_ref, sc_ref, xs_hbm, h_scr, zero_scr, sem, zsem):
    i = pl.program_id(0)
    n = pl.num_programs(0) - 1
    tm = x_ref.shape[0]
    slot = i % MOE_DISPATCH_SLOTS
    prev = (i + MOE_DISPATCH_SLOTS - 1) % MOE_DISPATCH_SLOTS

    def row_copy(r, dst, s):
        return pltpu.make_async_copy(h_scr.at[s, pl.ds(r, 1), :], xs_hbm.at[pl.ds(dst, 1), :], sem.at[s])

    def wait_rows(s):
        for _ in range(2):
            pltpu.make_async_copy(h_scr.at[s], xs_hbm.at[pl.ds(0, tm), :], sem.at[s]).wait()

    def issue_prev():
        base = (i - 1) * tm
        for r in range(tm):
            row_copy(r, d0_ref[base + r], prev).start()
            row_copy(r, d1_ref[base + r], prev).start()

    def normalise():
        h_scr[slot] = _norm_mod(x_ref[...], g_ref[...], sh_ref[0], sc_ref[0])

    @pl.when(i >= MOE_DISPATCH_SLOTS)
    def _():
        wait_rows(slot)

    @pl.when(i == 0)
    def _():
        normalise()

    @pl.when(i > 0)
    def _():
        issue_prev()
        normalise()

    @pl.when(i == n)
    def _():
        zero_scr[...] = jnp.zeros_like(zero_scr)

        def zero_row(r):
            return pltpu.make_async_copy(zero_scr.at[pl.ds(0, 1), :], xs_hbm.at[pl.ds(r, 1), :], zsem)

        def zero_block(b):
            start = pl.multiple_of(b * MOE_ZERO_ROWS, MOE_ZERO_ROWS)
            return pltpu.make_async_copy(zero_scr, xs_hbm.at[pl.ds(start, MOE_ZERO_ROWS), :], zsem)

        def start_all(copy):
            def body(r, c):
                copy(r).start()
                return c
            return body

        def wait_all(copy):
            def body(r, c):
                copy(r).wait()
                return c
            return body

        def zero_group(b):
            start = pl.multiple_of(b * SUBLANES, SUBLANES)
            return pltpu.make_async_copy(zero_scr.at[pl.ds(0, SUBLANES), :],
                                         xs_hbm.at[pl.ds(start, SUBLANES), :], zsem)

        for e in range(N_EXPERTS):
            lo = fill_ref[e]
            hi = fill_ref[N_EXPERTS + e]
            lo_group = (lo + SUBLANES - 1) // SUBLANES
            lax.fori_loop(lo, lo_group * SUBLANES, start_all(zero_row), 0)
            lax.fori_loop(lo_group, hi // SUBLANES, start_all(zero_group), 0)
            lax.fori_loop(lo, lo_group * SUBLANES, wait_all(zero_row), 0)
            lax.fori_loop(lo_group, hi // SUBLANES, wait_all(zero_group), 0)
        blocks_per_tile = MOE_ROW_TILE // MOE_ZERO_ROWS
        first = fill_ref[2 * N_EXPERTS] * blocks_per_tile
        last = (xs_hbm.shape[0] // MOE_ROW_TILE) * blocks_per_tile
        lax.fori_loop(first, last, start_all(zero_block), 0)
        lax.fori_loop(first, last, wait_all(zero_block), 0)
        wait_rows(prev)
        wait_rows((i + 1) % MOE_DISPATCH_SLOTS)


def moe_dispatch(x, g, mod_all, ctx_row, prompt_tokens, req_tokens, dest0, dest1, fill, n_tiles, tm=512):
    t = x.shape[0]
    pt = prompt_tokens // tm
    last = t // tm - 1

    def mod_spec(chunk):
        inner = _stream_mod_spec(chunk, pt, req_tokens // tm, ctx_row)
        return pl.BlockSpec((1, 1, D_MODEL), lambda i, *refs: inner.index_map(jnp.minimum(i, last), *refs))

    grid_spec = pltpu.PrefetchScalarGridSpec(
        num_scalar_prefetch=3,
        grid=(t // tm + 1,),
        in_specs=[pl.BlockSpec((tm, D_MODEL), lambda i, *_: (jnp.minimum(i, last), 0)),
                  pl.BlockSpec((1, D_MODEL), lambda i, *_: (0, 0)),
                  mod_spec(3), mod_spec(4)],
        out_specs=pl.BlockSpec(memory_space=pl.ANY),
        scratch_shapes=[pltpu.VMEM((MOE_DISPATCH_SLOTS, tm, D_MODEL), F32),
                        pltpu.VMEM((MOE_ZERO_ROWS, D_MODEL), F32),
                        pltpu.SemaphoreType.DMA((MOE_DISPATCH_SLOTS,)), pltpu.SemaphoreType.DMA(())],
    )
    return pl.pallas_call(
        _dispatch_kernel,
        grid_spec=grid_spec,
        out_shape=jax.ShapeDtypeStruct((n_tiles * MOE_ROW_TILE, D_MODEL), F32),
        compiler_params=pltpu.CompilerParams(dimension_semantics=("arbitrary",),
                                             vmem_limit_bytes=VMEM_LIMIT_BYTES,
                                             disable_bounds_checks=True),
        name="moe_dispatch",
    )(dest0, dest1, fill, x, g.reshape(1, D_MODEL), mod_all, mod_all)


def _experts_kernel(te_ref, tv_ref, x_ref, wg_ref, wu_ref, wd_ref, o_ref, h_scr, acc_scr):
    i = pl.program_id(0)
    f = pl.program_id(1)
    last_f = pl.num_programs(1) - 1

    n_valid = tv_ref[i]
    tile_rows = h_scr.shape[0]
    step = tile_rows // MOE_ROW_PARTS

    def swiglu_rows(n_rows):
        rows = slice(0, n_rows)

        @pl.when(f == 0)
        def _():
            h_scr[rows, :] = x_ref[rows, :].astype(BF16)
            acc_scr[rows, :] = jnp.zeros((n_rows, D_MODEL), F32)

        h = h_scr[rows, :]
        a = _dot(h, wg_ref[0].astype(BF16))
        u = _dot(h, wu_ref[0].astype(BF16))
        acc_scr[rows, :] += _dot((_silu(a) * u).astype(BF16), wd_ref[0].astype(BF16))

        @pl.when(f == last_f)
        def _():
            o_ref[rows, :] = acc_scr[rows, :]

    for part in range(1, MOE_ROW_PARTS + 1):
        n_rows = part * step

        @pl.when((n_valid > n_rows - step) & (n_valid <= n_rows))
        def _(n_rows=n_rows):
            swiglu_rows(n_rows)
            if n_rows < tile_rows:
                @pl.when(f == last_f)
                def _():
                    o_ref[n_rows:, :] = jnp.zeros((tile_rows - n_rows, D_MODEL), F32)

    @pl.when((n_valid == 0) & (f == last_f))
    def _():
        o_ref[...] = jnp.zeros_like(o_ref)


def moe_experts(xs_sorted, tile_expert, tile_valid, w_gate, w_up, w_down, tf=512):
    rows = xs_sorted.shape[0]
    fdim = w_gate.shape[2]
    n_f = fdim // tf
    tr = MOE_ROW_TILE

    def f_eff(i, f, tv):
        return jnp.where(tv[i] > 0, f, n_f - 1)

    grid_spec = pltpu.PrefetchScalarGridSpec(
        num_scalar_prefetch=2,
        grid=(rows // tr, n_f),
        in_specs=[pl.BlockSpec((tr, D_MODEL), lambda i, f, te, tv: (i, 0)),
                  pl.BlockSpec((1, D_MODEL, tf), lambda i, f, te, tv: (te[i], 0, f_eff(i, f, tv))),
                  pl.BlockSpec((1, D_MODEL, tf), lambda i, f, te, tv: (te[i], 0, f_eff(i, f, tv))),
                  pl.BlockSpec((1, tf, D_MODEL), lambda i, f, te, tv: (te[i], f_eff(i, f, tv), 0))],
        out_specs=pl.BlockSpec((tr, D_MODEL), lambda i, f, te, tv: (i, 0)),
        scratch_shapes=[pltpu.VMEM((tr, D_MODEL), BF16), pltpu.VMEM((tr, D_MODEL), F32)],
    )
    return pl.pallas_call(
        _experts_kernel,
        grid_spec=grid_spec,
        out_shape=jax.ShapeDtypeStruct((rows, D_MODEL), F32),
        compiler_params=_params("arbitrary", "arbitrary"),
        name="moe_experts",
    )(tile_expert, tile_valid, xs_sorted, w_gate, w_up, w_down)


def _combine_kernel(d0_ref, d1_ref, x_ref, info_ref, gate_ref, fg_ref, ys_hbm, op_ref, os_ref, rbuf, sem,
                    *, prompt_tiles):
    i = pl.program_id(0)
    n = pl.num_programs(0)
    tm = x_ref.shape[0]
    slot = i % 2

    def issue(tile, s, unroll):
        base = tile * tm

        def body(r8, c):
            rb = pl.multiple_of(r8 * SUBLANES, SUBLANES)
            for k in range(SUBLANES):
                pltpu.make_async_copy(ys_hbm.at[pl.ds(d0_ref[base + rb + k], 1), :],
                                      rbuf.at[s, 0, pl.ds(rb + k, 1), :], sem.at[s]).start()
                pltpu.make_async_copy(ys_hbm.at[pl.ds(d1_ref[base + rb + k], 1), :],
                                      rbuf.at[s, 1, pl.ds(rb + k, 1), :], sem.at[s]).start()
            return c

        lax.fori_loop(0, tm // SUBLANES, body, 0, unroll=unroll)

    def wait_rows(s):
        for k in range(2):
            pltpu.make_async_copy(ys_hbm.at[pl.ds(0, tm), :], rbuf.at[s, k], sem.at[s]).wait()

    @pl.when(i == 0)
    def _():
        issue(0, 0, False)

    wait_rows(slot)
    issue(jnp.where(i + 1 < n, i + 1, 0), 1 - slot, True)

    info = info_ref[...]
    moe = info[:, INFO_G0:INFO_G0 + 1] * rbuf[slot, 0] + info[:, INFO_G1:INFO_G1 + 1] * rbuf[slot, 1]
    y = x_ref[...] + gate_ref[0] * moe
    out = y * lax.rsqrt(jnp.mean(y * y, axis=-1, keepdims=True) + EPS) * fg_ref[...]

    @pl.when(i < prompt_tiles)
    def _():
        op_ref[...] = out

    @pl.when(i >= prompt_tiles)
    def _():
        os_ref[...] = out

    @pl.when(i == n - 1)
    def _():
        wait_rows(1 - slot)


def moe_combine(x, info, ys_sorted, dest0, dest1, mod_all, ctx_row, prompt_tokens, req_tokens, final_g, tm=512):
    t = x.shape[0]
    pt = prompt_tokens // tm
    grid_spec = pltpu.PrefetchScalarGridSpec(
        num_scalar_prefetch=2,
        grid=(t // tm,),
        in_specs=[pl.BlockSpec((tm, D_MODEL), lambda i, *_: (i, 0)),
                  pl.BlockSpec((tm, LANES), lambda i, *_: (i, 0)),
                  _stream_mod_spec(5, pt, req_tokens // tm, ctx_row),
                  pl.BlockSpec((1, D_MODEL), lambda i, *_: (0, 0)),
                  pl.BlockSpec(memory_space=pl.ANY)],
        out_specs=[pl.BlockSpec((tm, D_MODEL), lambda i, *_: (jnp.minimum(i, pt - 1), 0)),
                   pl.BlockSpec((tm, D_MODEL), lambda i, *_: (jnp.maximum(i - pt, 0), 0))],
        scratch_shapes=[pltpu.VMEM((2, 2, tm, D_MODEL), F32), pltpu.SemaphoreType.DMA((2,))],
    )
    return pl.pallas_call(
        functools.partial(_combine_kernel, prompt_tiles=pt),
        grid_spec=grid_spec,
        out_shape=[jax.ShapeDtypeStruct((prompt_tokens, D_MODEL), F32),
                   jax.ShapeDtypeStruct((t - prompt_tokens, D_MODEL), F32)],
        compiler_params=pltpu.CompilerParams(dimension_semantics=("arbitrary",),
                                             vmem_limit_bytes=VMEM_LIMIT_BYTES,
                                             disable_bounds_checks=True),
        name="moe_combine",
    )(dest0, dest1, x, info, mod_all, final_g.reshape(1, D_MODEL), ys_sorted)


def moe_layout(counts, fields, n_tiles):
    nt_e = (counts + MOE_ROW_TILE - 1) // MOE_ROW_TILE
    ends = jnp.cumsum(nt_e)
    total = ends[-1]
    offset = (ends - nt_e) * MOE_ROW_TILE
    experts = jnp.arange(N_EXPERTS, dtype=jnp.int32)

    def dest(e_lane, r_lane):
        e = fields[e_lane].astype(jnp.int32)
        off = jnp.sum(jnp.where(e[:, None] == experts[None, :], offset[None, :], 0), axis=1)
        return (off + fields[r_lane].astype(jnp.int32)).astype(jnp.int32)

    fill = jnp.concatenate([offset + counts, ends * MOE_ROW_TILE, total[None]]).astype(jnp.int32)
    ids = jnp.arange(n_tiles, dtype=jnp.int32)
    ids_c = jnp.minimum(ids, total - 1)
    te = jnp.sum((ids_c[:, None] >= ends[None, :]).astype(jnp.int32), axis=1)
    mine = te[:, None] == experts[None, :]
    first_tile = jnp.sum(jnp.where(mine, (ends - nt_e)[None, :], 0), axis=1)
    count = jnp.sum(jnp.where(mine, counts[None, :], 0), axis=1)
    rows_left = jnp.clip(count - (ids - first_tile) * MOE_ROW_TILE, 0, MOE_ROW_TILE)
    tile_rows = jnp.where(ids < total, rows_left, 0)
    return (dest(INFO_E0, INFO_R0), dest(INFO_E1, INFO_R1), fill, te.astype(jnp.int32),
            tile_rows.astype(jnp.int32))


def moe_final(x, info, fields, cnt, g, mod_all, ctx_row, prompt_tokens, req_tokens, final_g, w_gate, w_up,
              w_down):
    t = x.shape[0]
    counts = cnt[0, :N_EXPERTS].astype(jnp.int32)
    n_tiles = (2 * t) // MOE_ROW_TILE + N_EXPERTS
    dest0, dest1, fill, te, tv = moe_layout(counts, fields, n_tiles)
    xs_sorted = moe_dispatch(x, g, mod_all, ctx_row, prompt_tokens, req_tokens, dest0, dest1, fill, n_tiles)
    ys_sorted = moe_experts(xs_sorted, te, tv, w_gate, w_up, w_down)
    return moe_combine(x, info, ys_sorted, dest0, dest1, mod_all, ctx_row, prompt_tokens, req_tokens, final_g)


def kernel(x_prompt, x_sample, state_s5, cache_na_k, cache_na_v, cache_diff_k, cache_diff_v, c, c_ctx, w_mod, b_mod, norm_mix_g, norm_ffn_g, final_norm_g, w_in_e, w_out_e, s5_lam_re, s5_lam_im, s5_log_dt, s5_b_re, s5_b_im, s5_c_re, s5_c_im, s5_d, s5_w_glu, s5_b_glu, na_rpb, ffn_w_gate, ffn_w_up, ffn_w_down, w_in_o, w_out_o, diff_lam_q1, diff_lam_k1, diff_lam_q2, diff_lam_k2, diff_subln_g, conv_w, conv_b, conv_ln_g, conv_ln_b, router_w, moe_w_gate, moe_w_up, moe_w_down):
    bp, lp, d = x_prompt.shape
    bs, ls, _ = x_sample.shape
    tm = 1024
    xp = x_prompt.reshape(bp * lp, d)
    xs = x_sample.reshape(bs * ls, d)
    rows_p = (bp * lp) // tm
    rows_s = ls // tm

    cond8 = jnp.concatenate([c, c_ctx[None, :], jnp.zeros((SUBLANES - bs - 1, d), F32)], axis=0)
    mod = adaln_all(cond8, w_mod, b_mod)
    mod_s = mod[:, 0:bs]
    mod_p = mod[:, bs:bs + 1]

    def tiles(rows, tile):
        return rows * tm // tile

    bmat, cmat, lam8 = s5_params(s5_lam_re[0], s5_lam_im[0], s5_log_dt[0], s5_b_re[0], s5_b_im[0],
                                 s5_c_re[0], s5_c_im[0])
    bias = na_bias_blocks(na_rpb[0])
    n_e = w_in_e.shape[-1]

    tiles_p = tiles(rows_p, 512)
    tiles_s = tiles(rows_s, 512)
    proj_p = in_proj(xp, norm_mix_g[0], mod_p[0], tiles_p, w_in_e[0])
    u_s, qkv_s = in_proj(xs, norm_mix_g[0], mod_s[0], tiles_s, w_in_e[0], n_f32=HALF_MIX)

    y_p, st_p = s5_scan(proj_p.reshape(bp, lp, n_e), bmat, cmat, lam8, None, 1)
    chunks = SUBLANES // bs
    h0 = state_s5[:, 0].reshape(bs, 2, 2, S5_GROUPS * S5_STATE)
    y_s, _ = s5_scan(u_s.reshape(bs * chunks, ls // chunks, HALF_MIX), bmat, cmat, lam8, h0, chunks)
    nao_p, na_k, na_v = na_ctx(proj_p.reshape(bp, lp, n_e))

    def heads_to_lanes(cache):
        return cache.transpose(0, 2, 1, 3).reshape(bs, cache.shape[2], HALF_MIX).astype(BF16)

    nao_s = na_lat(qkv_s.reshape(bs, ls, n_e - HALF_MIX), heads_to_lanes(cache_na_k[:, 0]),
                   heads_to_lanes(cache_na_v[:, 0]), bias)

    glu_w = (s5_d[0], s5_w_glu[0].astype(BF16), s5_b_glu[0])
    ffn_w = (w_out_e[0].astype(BF16), norm_ffn_g[0])
    ffn_w3 = (ffn_w_gate[0].astype(BF16), ffn_w_up[0].astype(BF16), ffn_w_down[0].astype(BF16))
    xp = mix_ffn(xp, y_p.reshape(bp * lp, HALF_MIX), proj_p, *glu_w, nao_p.reshape(bp * lp, HALF_MIX),
                 *ffn_w, mod_p[0], tiles_p, *ffn_w3)
    xs = mix_ffn(xs, y_s.reshape(bs * ls, HALF_MIX), u_s, *glu_w, nao_s.reshape(bs * ls, HALF_MIX),
                 *ffn_w, mod_s[0], tiles_s, *ffn_w3)

    lam_init = 0.8 - 0.6 * math.exp(-0.3 * 1)
    lam = (jnp.exp(jnp.sum(diff_lam_q1[0].astype(F32) * diff_lam_k1[0].astype(F32)))
           - jnp.exp(jnp.sum(diff_lam_q2[0].astype(F32) * diff_lam_k2[0].astype(F32)))
           + lam_init)
    cos, sin = rope_tables(ls)
    n_o = w_in_o.shape[-1]

    proj_p = in_proj(xp, norm_mix_g[1], mod_p[1], tiles_p, w_in_o[0])
    proj_s = in_proj(xs, norm_mix_g[1], mod_s[1], tiles_s, w_in_o[0])

    do_p, diff_k, diff_v = diff_ctx(proj_p.reshape(bp, lp, n_o), lam, diff_subln_g[0], lam_init)
    do_s = diff_lat(proj_s.reshape(bs, ls, n_o), cache_diff_k[:, 0:1], cache_diff_v[:, 0:1], cos, sin,
                    lam, diff_subln_g[0], lam_init)
    co_p = conformer_conv(proj_p.reshape(bp, lp, n_o), conv_w[0], conv_b[0], conv_ln_g[0], conv_ln_b[0], lp)
    co_s = conformer_conv(proj_s.reshape(bs, ls, n_o), conv_w[0], conv_b[0], conv_ln_g[0], conv_ln_b[0], 512)

    mod_all = mod[1]
    x_all, info, fields, cnt = mix_route(
        xp, do_p.reshape(bp * lp, HALF_MIX), co_p.reshape(bp * lp, HALF_MIX),
        xs, do_s.reshape(bs * ls, HALF_MIX), co_s.reshape(bs * ls, HALF_MIX),
        w_out_o[0].astype(BF16), norm_ffn_g[1], mod_all, bs, ls, router_w[0])
    yp, ys = moe_final(x_all, info, fields, cnt, norm_ffn_g[1], mod_all, bs, bp * lp, ls, final_norm_g,
                       moe_w_gate[0], moe_w_up[0], moe_w_down[0])

    new_state = st_p.reshape(bp, 1, 2, 2, S5_GROUPS, S5_STATE)
    return (yp.reshape(bp, lp, d), ys.reshape(bs, ls, d), new_state, na_k, na_v, diff_k, diff_v)
```

```python
import functools
import math

import jax
import jax.numpy as jnp
import numpy as np
from jax import lax
from jax.experimental import pallas as pl
from jax.experimental.pallas import tpu as pltpu

D_MODEL = 1024
DEPTH = 2
GRID_W = 64
HALF_MIX = 512
S5_GROUP_CH = 16
S5_GROUPS = 32
S5_STATE = 64
NA_HEAD_DIM = 64
NA_HEADS = 8
NA_WIN_R = 8
NA_WIN_C = 16
DIFF_D = 64
DIFF_HEAD_DIM = 128
DIFF_HEADS = 4
ROPE_BASE = 10000.0
CONV_WIDTH = 31
N_EXPERTS = 8
EPS = 1e-6

F32 = jnp.float32
BF16 = jnp.bfloat16
NEG_BIG = -1e30

VMEM_LIMIT_BYTES = 56 * 1024 * 1024
LANES = 128
SUBLANES = 8

S5_COL_GROUPS = 8
S5_COL_CH = S5_COL_GROUPS * S5_GROUP_CH
S5_COL_STATE = S5_COL_GROUPS * S5_STATE
S5_N_COL = S5_GROUPS // S5_COL_GROUPS
S5_TIME_BLOCK = 128


def _params(*sem):
    return pltpu.CompilerParams(dimension_semantics=sem, vmem_limit_bytes=VMEM_LIMIT_BYTES)


def _dot(a, b):
    return jnp.dot(a, b, preferred_element_type=F32)


def _dot_nt(a, b):
    return lax.dot_general(a, b, (((1,), (1,)), ((), ())), preferred_element_type=F32)


def _silu(x):
    return x * jax.nn.sigmoid(x)


def _norm_mod(x, g, shift, scale):
    y = x * lax.rsqrt(jnp.mean(x * x, axis=-1, keepdims=True) + EPS) * g
    return y * (1.0 + scale) + shift


def _mod_kernel(cond_ref, w_ref, b_ref, o_ref):
    s = _silu(cond_ref[...])
    o_ref[0, :, 0, :] = jnp.dot(s, w_ref[0], precision=lax.Precision.HIGHEST,
                                preferred_element_type=F32) + b_ref[0]


def adaln_all(cond8, w_mod, b_mod):
    tn = 1536
    n = w_mod.shape[-1]
    return pl.pallas_call(
        _mod_kernel,
        grid=(DEPTH, n // tn),
        in_specs=[pl.BlockSpec((SUBLANES, D_MODEL), lambda l, j: (0, 0)),
                  pl.BlockSpec((1, D_MODEL, tn), lambda l, j: (l, 0, j)),
                  pl.BlockSpec((1, 1, tn), lambda l, j: (l, 0, j))],
        out_specs=pl.BlockSpec((1, SUBLANES, 1, tn), lambda l, j: (l, 0, 0, j)),
        out_shape=jax.ShapeDtypeStruct((DEPTH, SUBLANES, 1, n), F32),
        compiler_params=_params("arbitrary", "arbitrary"),
        name="adaln_mod",
    )(cond8, w_mod, b_mod.reshape(DEPTH, 1, n))


def _mod_spec(chunk, tiles_per_row):
    return pl.BlockSpec((1, 1, D_MODEL), lambda i, *_: (i // tiles_per_row, 0, chunk))


def _in_proj_kernel(x_ref, g_ref, sh_ref, sc_ref, w_ref, *refs, n_f32):
    *outs, wb_scr = refs

    @pl.when(pl.program_id(0) == 0)
    def _():
        wb_scr[...] = w_ref[...].astype(BF16)

    h = _norm_mod(x_ref[...], g_ref[...], sh_ref[0], sc_ref[0]).astype(BF16)
    y = _dot(h, wb_scr[...])
    if len(outs) == 1:
        outs[0][...] = y
    else:
        outs[0][...] = y[:, :n_f32]
        outs[1][...] = y[:, n_f32:].astype(BF16)


def in_proj(x, g, mod, tiles_per_row, w, n_f32=None, tm=512):
    t = x.shape[0]
    n = w.shape[1]
    if n_f32 is None:
        out_specs = pl.BlockSpec((tm, n), lambda i: (i, 0))
        out_shape = jax.ShapeDtypeStruct((t, n), F32)
    else:
        out_specs = [pl.BlockSpec((tm, n_f32), lambda i: (i, 0)), pl.BlockSpec((tm, n - n_f32), lambda i: (i, 0))]
        out_shape = [jax.ShapeDtypeStruct((t, n_f32), F32), jax.ShapeDtypeStruct((t, n - n_f32), BF16)]
    return pl.pallas_call(
        functools.partial(_in_proj_kernel, n_f32=n_f32),
        grid=(t // tm,),
        in_specs=[pl.BlockSpec((tm, D_MODEL), lambda i: (i, 0)),
                  pl.BlockSpec((1, D_MODEL), lambda i: (0, 0)),
                  _mod_spec(0, tiles_per_row),
                  _mod_spec(1, tiles_per_row),
                  pl.BlockSpec((D_MODEL, n), lambda i: (0, 0), pipeline_mode=pl.Buffered(1))],
        out_specs=out_specs,
        out_shape=out_shape,
        scratch_shapes=[pltpu.VMEM((D_MODEL, n), BF16)],
        compiler_params=_params("arbitrary"),
        name="in_proj",
    )(x, g.reshape(1, D_MODEL), mod, mod, w)


def _s5_scan_kernel(*refs, seq, chunks, has_init):
    if has_init:
        u_ref, bm_ref, cm_ref, lam_ref, h0_ref, y_ref, st_ref, bu_scr, hb_scr, ytm_scr = refs
    else:
        u_ref, bm_ref, cm_ref, lam_ref, y_ref, st_ref, bu_scr, hb_scr, ytm_scr = refs
        h0_ref = None
    tb = S5_TIME_BLOCK
    n_tb = seq // tb
    ns = S5_COL_STATE
    row = lax.broadcasted_iota(jnp.int32, (SUBLANES, ns), 0)
    piece = row % chunks

    for d in range(2):
        lam = lam_ref[d, 0]
        lr, li = lam[:, :ns], lam[:, ns:]
        bm = bm_ref[d, 0]
        cm = cm_ref[d, 0]
        blocks = list(range(n_tb)) if d == 0 else list(range(n_tb - 1, -1, -1))

        def project(k, buf):
            ub = u_ref[:, k * tb:(k + 1) * tb, :]
            utm = jnp.swapaxes(ub, 0, 1).reshape(tb * SUBLANES, S5_COL_CH).astype(BF16)
            bu_scr[buf] = _dot(utm, bm).reshape(tb, SUBLANES, 2 * ns)

        def scan_block(h, buf, store):
            def advance(t, hr, hi):
                b = bu_scr[buf, t]
                return lr * hr - li * hi + b[:, :ns], lr * hi + li * hr + b[:, ns:]

            def step(s, carry):
                t_a = (tb - 1 - 2 * s) if d == 1 else 2 * s
                t_b = t_a - 1 if d == 1 else t_a + 1
                ar, ai = advance(t_a, *carry)
                br, bi = advance(t_b, ar, ai)
                if store:
                    first = jnp.concatenate([br, bi] if d == 1 else [ar, ai], axis=1)
                    second = jnp.concatenate([ar, ai] if d == 1 else [br, bi], axis=1)
                    t_lo = t_b if d == 1 else t_a
                    hb_scr[buf, t_lo * SUBLANES:(t_lo + 2) * SUBLANES, :] = (
                        jnp.concatenate([first, second], axis=0).astype(BF16))
                return br, bi

            if not store:
                return lax.fori_loop(0, tb // 2, step, h)
            for s in range(tb // 2):
                h = step(s, h)
            return h

        def run_pass(h, store):
            project(blocks[0], 0)
            for j, k in enumerate(blocks):
                buf = j % 2
                if j + 1 < n_tb:
                    project(blocks[j + 1], 1 - buf)
                h = scan_block(h, buf, store)
                if store:
                    yb = _dot(hb_scr[buf], cm).reshape(tb, SUBLANES, S5_COL_CH)
                    if d == 0:
                        ytm_scr[k * tb:(k + 1) * tb] = yb
                    else:
                        ytm_scr[k * tb:(k + 1) * tb] += yb
            return h

        zero = jnp.zeros((SUBLANES, ns), F32)
        if chunks > 1:
            fr, fi = run_pass((zero, zero), False)
            pr, pi = lr, li
            for _ in range(int(math.log2(seq))):
                pr, pi = pr * pr - pi * pi, 2.0 * pr * pi
            edge = 0 if d == 0 else chunks - 1
            shift = 1 if d == 0 else SUBLANES - 1
            if has_init:
                h0r = h0_ref[:, d, 0, :]
                h0i = h0_ref[:, d, 1, :]
                seq_of_row = row // chunks
                er, ei = zero, zero
                for b in range(SUBLANES // chunks):
                    er = jnp.where(seq_of_row == b, h0r[b:b + 1, :], er)
                    ei = jnp.where(seq_of_row == b, h0i[b:b + 1, :], ei)
            else:
                er, ei = zero, zero
            is_edge = piece == edge
            cr = jnp.where(is_edge, er, zero)
            ci = jnp.where(is_edge, ei, zero)
            for _ in range(chunks - 1):
                tr = fr + pr * cr - pi * ci
                ti = fi + pr * ci + pi * cr
                cr = jnp.where(is_edge, er, pltpu.roll(tr, shift, 0))
                ci = jnp.where(is_edge, ei, pltpu.roll(ti, shift, 0))
            h = (cr, ci)
        else:
            if has_init:
                h = (h0_ref[:, d, 0, :], h0_ref[:, d, 1, :])
            else:
                h = (zero, zero)

        h = run_pass(h, True)
        st_ref[:, d, 0, :] = h[0]
        st_ref[:, d, 1, :] = h[1]

    y_ref[...] = jnp.swapaxes(ytm_scr[...], 0, 1)


def s5_scan(proj3, bmat, cmat, lam8, h0, chunks):
    rows, seq, _ = proj3.shape
    ns = S5_COL_STATE
    has_init = h0 is not None
    in_specs = [pl.BlockSpec((SUBLANES, seq, S5_COL_CH), lambda i, c: (i, 0, c)),
                pl.BlockSpec((2, 1, S5_COL_CH, 2 * ns), lambda i, c: (0, c, 0, 0)),
                pl.BlockSpec((2, 1, 2 * ns, S5_COL_CH), lambda i, c: (0, c, 0, 0)),
                pl.BlockSpec((2, 1, SUBLANES, 2 * ns), lambda i, c: (0, c, 0, 0))]
    args = [proj3, bmat, cmat, lam8]
    if has_init:
        nb = h0.shape[0]
        in_specs.append(pl.BlockSpec((nb, 2, 2, ns), lambda i, c: (0, 0, 0, c)))
        args.append(h0)
    y, st = pl.pallas_call(
        functools.partial(_s5_scan_kernel, seq=seq, chunks=chunks, has_init=has_init),
        grid=(rows // SUBLANES, S5_N_COL),
        in_specs=in_specs,
        out_specs=[pl.BlockSpec((SUBLANES, seq, S5_COL_CH), lambda i, c: (i, 0, c)),
                   pl.BlockSpec((SUBLANES, 2, 2, ns), lambda i, c: (i, 0, 0, c))],
        out_shape=[jax.ShapeDtypeStruct((rows, seq, HALF_MIX), F32),
                   jax.ShapeDtypeStruct((rows, 2, 2, S5_GROUPS * S5_STATE), F32)],
        scratch_shapes=[pltpu.VMEM((2, S5_TIME_BLOCK, SUBLANES, 2 * ns), F32),
                        pltpu.VMEM((2, S5_TIME_BLOCK * SUBLANES, 2 * ns), BF16),
                        pltpu.VMEM((seq, SUBLANES, S5_COL_CH), F32)],
        compiler_params=_params("arbitrary", "arbitrary"),
        name="s5_scan",
    )(*args)
    return y, st


def s5_params(lam_re, lam_im, log_dt, b_re, b_im, c_re, c_im):
    lr = lam_re.astype(F32)
    li = lam_im.astype(F32)
    dt = jnp.exp(log_dt.astype(F32))[..., None]
    mag = jnp.exp(lr * dt)
    bar_re = mag * jnp.cos(li * dt)
    bar_im = mag * jnp.sin(li * dt)
    den = lr * lr + li * li
    q_re = ((bar_re - 1.0) * lr + bar_im * li) / den
    q_im = (bar_im * lr - (bar_re - 1.0) * li) / den
    br = b_re.astype(F32)
    bi = b_im.astype(F32)
    b_bar_re = q_re[..., None] * br - q_im[..., None] * bi
    b_bar_im = q_re[..., None] * bi + q_im[..., None] * br
    eye = jnp.eye(S5_COL_GROUPS, dtype=F32)

    def block_diag_b(m):
        m = m.reshape(2, S5_N_COL, S5_COL_GROUPS, S5_STATE, S5_GROUP_CH)
        bd = jnp.einsum('dngpc,gh->dngchp', m, eye)
        return bd.reshape(2, S5_N_COL, S5_COL_CH, S5_COL_STATE)

    def block_diag_c(m):
        m = m.reshape(2, S5_N_COL, S5_COL_GROUPS, S5_GROUP_CH, S5_STATE)
        bd = jnp.einsum('dngcp,gh->dngphc', m, eye)
        return bd.reshape(2, S5_N_COL, S5_COL_STATE, S5_COL_CH)

    bmat = jnp.concatenate([block_diag_b(b_bar_re), block_diag_b(b_bar_im)], axis=-1).astype(BF16)
    cmat = jnp.concatenate([block_diag_c(c_re.astype(F32)), block_diag_c(-c_im.astype(F32))],
                           axis=-2).astype(BF16)
    lam_cat = jnp.concatenate([bar_re.reshape(2, S5_N_COL, S5_COL_STATE),
                               bar_im.reshape(2, S5_N_COL, S5_COL_STATE)], axis=-1)
    lam8 = jnp.broadcast_to(lam_cat[:, :, None, :], (2, S5_N_COL, SUBLANES, 2 * S5_COL_STATE))
    return bmat, cmat, lam8


def _na_ctx_kernel(q_ref, k_ref, v_ref, o_ref, ko_ref, vo_ref):
    seq = q_ref.shape[1]
    lane = lax.broadcasted_iota(jnp.int32, (seq, LANES), 1)
    low = lane < NA_HEAD_DIM
    outs = []
    for pr in range(NA_HEADS // 2):
        cols = slice(pr * LANES, (pr + 1) * LANES)
        qp = q_ref[0, :, cols] * (NA_HEAD_DIM ** -0.5)
        kp = k_ref[0, :, cols]
        vp = v_ref[0, :, cols]
        for half in range(2):
            sl = slice(half * NA_HEAD_DIM, (half + 1) * NA_HEAD_DIM)
            ko_ref[0, 0, 2 * pr + half] = kp[:, sl]
            vo_ref[0, 0, 2 * pr + half] = vp[:, sl]
        kb = kp.astype(BF16)
        vb = vp.astype(BF16)
        o_pair = None
        for half in range(2):
            qm = jnp.where(low if half == 0 else jnp.logical_not(low), qp, 0.0).astype(BF16)
            s = _dot_nt(qm, kb)
            p = jnp.exp(s - jnp.max(s, axis=-1, keepdims=True))
            o = _dot(p.astype(BF16), vb) * (1.0 / jnp.sum(p, axis=-1, keepdims=True))
            o_pair = o if half == 0 else jnp.where(low, o_pair, o)
        outs.append(o_pair)
    o_ref[0] = jnp.concatenate(outs, axis=-1).astype(o_ref.dtype)


def na_ctx(proj3):
    b, seq, _ = proj3.shape
    cache_shape = jax.ShapeDtypeStruct((b, 1, NA_HEADS, seq, NA_HEAD_DIM), F32)
    cache_spec = pl.BlockSpec((1, 1, NA_HEADS, seq, NA_HEAD_DIM), lambda i: (i, 0, 0, 0, 0))
    return pl.pallas_call(
        _na_ctx_kernel,
        grid=(b,),
        in_specs=[pl.BlockSpec((1, seq, HALF_MIX), lambda i: (i, 0, 1)),
                  pl.BlockSpec((1, seq, HALF_MIX), lambda i: (i, 0, 2)),
                  pl.BlockSpec((1, seq, HALF_MIX), lambda i: (i, 0, 3))],
        out_specs=[pl.BlockSpec((1, seq, HALF_MIX), lambda i: (i, 0, 0)), cache_spec, cache_spec],
        out_shape=[jax.ShapeDtypeStruct((b, seq, HALF_MIX), BF16), cache_shape, cache_shape],
        compiler_params=_params("arbitrary"),
        name="na_ctx",
    )(proj3, proj3, proj3)


NA_Q_ROWS = 4
NA_KEY_ROWS = 12


def na_bias_blocks(rpb):
    qcol = np.arange(GRID_W)
    cc = np.arange(GRID_W)
    cs = np.clip(qcol - NA_WIN_C // 2, 0, GRID_W - NA_WIN_C)
    valid = (cc[None, :] >= cs[:, None]) & (cc[None, :] < cs[:, None] + NA_WIN_C)
    coff = cc[None, :] - qcol[:, None] + (NA_WIN_C - 1)
    n_col = 2 * NA_WIN_C - 1
    sel = ((coff[None] == np.arange(n_col)[:, None, None]) & valid[None]).astype(np.float32)
    sel = sel.reshape(n_col, GRID_W * GRID_W)
    mask = np.where(valid, 0.0, NEG_BIG).astype(np.float32).reshape(1, GRID_W * GRID_W)
    n_row = 2 * NA_WIN_R - 1
    t1 = jnp.dot(rpb.astype(F32).reshape(NA_HEADS * n_row, n_col), jnp.asarray(sel),
                 precision=lax.Precision.HIGHEST) + jnp.asarray(mask)
    t1 = t1.reshape(NA_HEADS, n_row, GRID_W, GRID_W)
    return pl.pallas_call(
        _na_bias_kernel,
        grid=(NA_HEADS,),
        in_specs=[pl.BlockSpec((1, n_row, GRID_W, GRID_W), lambda h: (h, 0, 0, 0))],
        out_specs=pl.BlockSpec((3, 1, NA_Q_ROWS * GRID_W, NA_KEY_ROWS * GRID_W), lambda h: (0, h, 0, 0)),
        out_shape=jax.ShapeDtypeStruct((3, NA_HEADS, NA_Q_ROWS * GRID_W, NA_KEY_ROWS * GRID_W), F32),
        compiler_params=_params("arbitrary"),
        name="na_bias",
    )(t1)


def _na_bias_kernel(t1_ref, o_ref):
    variants = ((lambda ri: 0, NA_WIN_R - 1), (lambda ri: ri, NA_WIN_R // 2 - 1),
                (lambda ri: NA_KEY_ROWS - NA_WIN_R, -1))
    outside = jnp.full((GRID_W, GRID_W), NEG_BIG, F32)
    for v, (lo_of, shift) in enumerate(variants):
        for ri in range(NA_Q_ROWS):
            for wr in range(NA_KEY_ROWS):
                inside = lo_of(ri) <= wr < lo_of(ri) + NA_WIN_R
                o_ref[v, 0, ri * GRID_W:(ri + 1) * GRID_W, wr * GRID_W:(wr + 1) * GRID_W] = (
                    t1_ref[0, wr - ri + shift] if inside else outside)


def _na_lat_kernel(q_ref, k_ref, v_ref, kc_ref, vc_ref, bias_ref, o_ref):
    qb = pl.program_id(1)
    rows = k_ref.shape[1] // GRID_W
    nk = NA_KEY_ROWS * GRID_W
    first_row = jnp.clip(qb * NA_Q_ROWS - NA_WIN_R // 2, 0, rows - NA_KEY_ROWS)
    start = pl.multiple_of(first_row * GRID_W, GRID_W)
    tq = q_ref.shape[1]
    lane = lax.broadcasted_iota(jnp.int32, (tq, LANES), 1)
    low = lane < NA_HEAD_DIM
    outs = []
    for pr in range(NA_HEADS // 2):
        cols = slice(pr * LANES, (pr + 1) * LANES)
        qp = q_ref[0, :, cols].astype(F32) * (NA_HEAD_DIM ** -0.5)
        kw = k_ref[0, pl.ds(start, nk), cols]
        vw = v_ref[0, pl.ds(start, nk), cols]
        kc = kc_ref[0, :, cols]
        vc = vc_ref[0, :, cols]
        o_pair = None
        for half in range(2):
            qm = jnp.where(low if half == 0 else jnp.logical_not(low), qp, 0.0).astype(BF16)
            s_loc = _dot_nt(qm, kw) + bias_ref[0, 2 * pr + half]
            s_ctx = _dot_nt(qm, kc)
            m = jnp.maximum(jnp.max(s_loc, axis=-1, keepdims=True), jnp.max(s_ctx, axis=-1, keepdims=True))
            p_loc = jnp.exp(s_loc - m)
            p_ctx = jnp.exp(s_ctx - m)
            inv = 1.0 / (jnp.sum(p_loc, axis=-1, keepdims=True) + jnp.sum(p_ctx, axis=-1, keepdims=True))
            o = (_dot(p_loc.astype(BF16), vw) + _dot(p_ctx.astype(BF16), vc)) * inv
            o_pair = o if half == 0 else jnp.where(low, o_pair, o)
        outs.append(o_pair)
    o_ref[0] = jnp.concatenate(outs, axis=-1).astype(o_ref.dtype)


def na_lat(qkv3, k_ctx, v_ctx, bias):
    b, seq, _ = qkv3.shape
    tq = NA_Q_ROWS * GRID_W
    n_q = seq // tq
    lc = k_ctx.shape[1]
    ctx_spec = pl.BlockSpec((1, lc, HALF_MIX), lambda i, r: (i, 0, 0))
    return pl.pallas_call(
        _na_lat_kernel,
        grid=(b, n_q),
        in_specs=[pl.BlockSpec((1, tq, HALF_MIX), lambda i, r: (i, r, 0)),
                  pl.BlockSpec((1, seq, HALF_MIX), lambda i, r: (i, 0, 1)),
                  pl.BlockSpec((1, seq, HALF_MIX), lambda i, r: (i, 0, 2)),
                  ctx_spec, ctx_spec,
                  pl.BlockSpec((1, NA_HEADS, tq, NA_KEY_ROWS * GRID_W),
                               lambda i, r: (jnp.where(r == 0, 0, jnp.where(r == n_q - 1, 2, 1)), 0, 0, 0))],
        out_specs=pl.BlockSpec((1, tq, HALF_MIX), lambda i, r: (i, r, 0)),
        out_shape=jax.ShapeDtypeStruct((b, seq, HALF_MIX), BF16),
        compiler_params=_params("arbitrary", "arbitrary"),
        name="na_lat",
    )(qkv3, qkv3, qkv3, k_ctx, v_ctx, bias)


def _softmax_pair_diff(s1, s2, lam):
    p1 = jnp.exp(s1 - jnp.max(s1, axis=-1, keepdims=True))
    p2 = jnp.exp(s2 - jnp.max(s2, axis=-1, keepdims=True))
    inv1 = 1.0 / jnp.sum(p1, axis=-1, keepdims=True)
    inv2 = lam / jnp.sum(p2, axis=-1, keepdims=True)
    return p1 * inv1 - p2 * inv2


def _sub_ln(o, g, lam_init):
    return o * lax.rsqrt(jnp.mean(o * o, axis=-1, keepdims=True) + EPS) * g * (1.0 - lam_init)


def _diff_ctx_kernel(lam_ref, q_ref, k_ref, v_ref, g_ref, o_ref, ko_ref, vo_ref, *, lam_init):
    scale = DIFF_D ** -0.5
    lam = lam_ref[0, 0]
    lane = lax.broadcasted_iota(jnp.int32, (q_ref.shape[1], DIFF_HEAD_DIM), 1)
    first = lane < DIFF_D
    for h in range(DIFF_HEADS):
        sl = slice(h * DIFF_HEAD_DIM, (h + 1) * DIFF_HEAD_DIM)
        qh = q_ref[0, :, sl]
        kh = k_ref[0, :, sl]
        vh = v_ref[0, :, sl]
        ko_ref[0, 0, h] = kh
        vo_ref[0, 0, h] = vh
        kb = kh.astype(BF16)
        s1 = _dot_nt(jnp.where(first, qh, 0.0).astype(BF16), kb) * scale
        s2 = _dot_nt(jnp.where(first, 0.0, qh).astype(BF16), kb) * scale
        a = _softmax_pair_diff(s1, s2, lam)
        o = _dot(a.astype(BF16), vh.astype(BF16))
        o_ref[0, :, sl] = _sub_ln(o, g_ref[...], lam_init).astype(o_ref.dtype)


def diff_ctx(proj3, lam, subln_g, lam_init):
    b, seq, _ = proj3.shape
    cache_shape = jax.ShapeDtypeStruct((b, 1, DIFF_HEADS, seq, DIFF_HEAD_DIM), F32)
    cache_spec = pl.BlockSpec((1, 1, DIFF_HEADS, seq, DIFF_HEAD_DIM), lambda i: (i, 0, 0, 0, 0))
    return pl.pallas_call(
        functools.partial(_diff_ctx_kernel, lam_init=lam_init),
        grid=(b,),
        in_specs=[pl.BlockSpec(memory_space=pltpu.SMEM),
                  pl.BlockSpec((1, seq, HALF_MIX), lambda i: (i, 0, 0)),
                  pl.BlockSpec((1, seq, HALF_MIX), lambda i: (i, 0, 1)),
                  pl.BlockSpec((1, seq, HALF_MIX), lambda i: (i, 0, 2)),
                  pl.BlockSpec((1, DIFF_HEAD_DIM), lambda i: (0, 0))],
        out_specs=[pl.BlockSpec((1, seq, HALF_MIX), lambda i: (i, 0, 0)), cache_spec, cache_spec],
        out_shape=[jax.ShapeDtypeStruct((b, seq, HALF_MIX), BF16), cache_shape, cache_shape],
        compiler_params=_params("arbitrary"),
        name="diff_ctx",
    )(lam.reshape(1, 1), proj3, proj3, proj3, subln_g.reshape(1, DIFF_HEAD_DIM))


def rope_tables(seq):
    t = np.arange(seq)
    row = (t // GRID_W).astype(np.float32)
    col = (t % GRID_W).astype(np.float32)
    n_freq = DIFF_D // 4
    inv = np.float32(ROPE_BASE) ** (-np.arange(n_freq, dtype=np.float32) / np.float32(n_freq))
    ang = np.concatenate([row[:, None] * inv, col[:, None] * inv], axis=-1)
    cos = np.repeat(np.cos(ang), 2, axis=-1)
    sin = np.repeat(np.sin(ang), 2, axis=-1)
    sign = np.where(np.arange(DIFF_D) % 2 == 0, -1.0, 1.0).astype(np.float32)
    sin = sin * sign
    return (jnp.asarray(np.tile(cos, (1, 2)).astype(np.float32)),
            jnp.asarray(np.tile(sin, (1, 2)).astype(np.float32)))


def _rope(x, cos, sin_signed):
    lane = lax.broadcasted_iota(jnp.int32, x.shape, 1)
    nxt = pltpu.roll(x, x.shape[1] - 1, 1)
    prv = pltpu.roll(x, 1, 1)
    partner = jnp.where(lane % 2 == 0, nxt, prv)
    return x * cos + partner * sin_signed


def _diff_lat_kernel(lam_ref, q_ref, k_ref, v_ref, kc_ref, vc_ref, cq_ref, sq_ref, ck_ref, sk_ref,
                     g_ref, o_ref, k_all, v_all, *, lam_init):
    seq = k_ref.shape[1]

    @pl.when(pl.program_id(2) == 0)
    def _():
        k_all[0:seq, :] = _rope(k_ref[0], ck_ref[...], sk_ref[...]).astype(BF16)
        k_all[seq:, :] = kc_ref[0, 0, 0].astype(BF16)
        v_all[0:seq, :] = v_ref[0].astype(BF16)
        v_all[seq:, :] = vc_ref[0, 0, 0].astype(BF16)

    lam = lam_ref[0, 0]
    q = _rope(q_ref[0], cq_ref[...], sq_ref[...]) * (DIFF_D ** -0.5)
    lane = lax.broadcasted_iota(jnp.int32, q.shape, 1)
    first = lane < DIFF_D
    kb = k_all[...]
    s1 = _dot_nt(jnp.where(first, q, 0.0).astype(BF16), kb)
    s2 = _dot_nt(jnp.where(first, 0.0, q).astype(BF16), kb)
    a = _softmax_pair_diff(s1, s2, lam)
    o = _dot(a.astype(BF16), v_all[...])
    o_ref[0] = _sub_ln(o, g_ref[...], lam_init).astype(o_ref.dtype)


def diff_lat(proj3, k_ctx, v_ctx, cos, sin, lam, subln_g, lam_init, tq=256):
    b, seq, _ = proj3.shape
    lc = k_ctx.shape[3]
    hd = DIFF_HEAD_DIM
    ctx_spec = pl.BlockSpec((1, 1, 1, lc, hd), lambda i, h, q: (i, 0, h, 0, 0))
    tq_spec = pl.BlockSpec((tq, hd), lambda i, h, q: (q, 0))
    full_spec = pl.BlockSpec((seq, hd), lambda i, h, q: (0, 0))
    return pl.pallas_call(
        functools.partial(_diff_lat_kernel, lam_init=lam_init),
        grid=(b, DIFF_HEADS, seq // tq),
        in_specs=[pl.BlockSpec(memory_space=pltpu.SMEM),
                  pl.BlockSpec((1, tq, hd), lambda i, h, q: (i, q, h)),
                  pl.BlockSpec((1, seq, hd), lambda i, h, q: (i, 0, DIFF_HEADS + h)),
                  pl.BlockSpec((1, seq, hd), lambda i, h, q: (i, 0, 2 * DIFF_HEADS + h)),
                  ctx_spec, ctx_spec, tq_spec, tq_spec, full_spec, full_spec,
                  pl.BlockSpec((1, hd), lambda i, h, q: (0, 0))],
        out_specs=pl.BlockSpec((1, tq, hd), lambda i, h, q: (i, q, h)),
        out_shape=jax.ShapeDtypeStruct((b, seq, HALF_MIX), BF16),
        scratch_shapes=[pltpu.VMEM((seq + lc, hd), BF16), pltpu.VMEM((seq + lc, hd), BF16)],
        compiler_params=_params("arbitrary", "arbitrary", "arbitrary"),
        name="diff_lat",
    )(lam.reshape(1, 1), proj3, proj3, proj3, k_ctx, v_ctx, cos, sin, cos, sin,
      subln_g.reshape(1, hd))


CONV_PAD = 16
CONV_SUB = 64


def _conv_kernel(a_ref, g_ref, ap_ref, gp_ref, an_ref, gn_ref, w_ref, b_ref, lg_ref, lb_ref, o_ref, xp_scr,
                 xsh_scr):
    t = pl.program_id(1)
    tt = a_ref.shape[1]
    prev = ap_ref[0] * jax.nn.sigmoid(gp_ref[0])
    nxt = an_ref[0] * jax.nn.sigmoid(gn_ref[0])
    xp_scr[0:CONV_PAD, :] = jnp.where(t > 0, prev, 0.0)
    xp_scr[CONV_PAD + tt:, :] = jnp.where(t < pl.num_programs(1) - 1, nxt, 0.0)
    xp_scr[CONV_PAD:CONV_PAD + tt, :] = a_ref[0] * jax.nn.sigmoid(g_ref[0])
    first_tap = CONV_PAD - CONV_WIDTH // 2
    n_rows = xsh_scr.shape[1]
    for b in range(SUBLANES):
        xsh_scr[b] = xp_scr[b:b + n_rows, :]
    for i in range(tt // CONV_SUB):
        s = i * CONV_SUB
        acc = jnp.zeros((CONV_SUB, HALF_MIX), F32)
        for j in range(CONV_WIDTH):
            whole, phase = divmod(first_tap + j, SUBLANES)
            lo = s + whole * SUBLANES
            acc = acc + xsh_scr[phase, lo:lo + CONV_SUB, :] * w_ref[j:j + 1, :]
        y = acc + b_ref[...]
        mu = jnp.mean(y, axis=-1, keepdims=True)
        yc = y - mu
        var = jnp.mean(yc * yc, axis=-1, keepdims=True)
        yn = yc * lax.rsqrt(var + EPS) * lg_ref[...] + lb_ref[...]
        o_ref[0, s:s + CONV_SUB, :] = _silu(yn).astype(o_ref.dtype)


def conformer_conv(proj3, w, b, ln_g, ln_b, tt):
    bsz, seq, _ = proj3.shape
    n_t = seq // tt
    hb = tt // CONV_PAD
    last = seq // CONV_PAD - 1
    vec = pl.BlockSpec((1, HALF_MIX), lambda i, t: (0, 0))

    def main(col):
        return pl.BlockSpec((1, tt, HALF_MIX), lambda i, t: (i, t, col))

    def prev(col):
        return pl.BlockSpec((1, CONV_PAD, HALF_MIX), lambda i, t: (i, jnp.maximum(t * hb - 1, 0), col))

    def nxt(col):
        return pl.BlockSpec((1, CONV_PAD, HALF_MIX), lambda i, t: (i, jnp.minimum((t + 1) * hb, last), col))

    return pl.pallas_call(
        _conv_kernel,
        grid=(bsz, n_t),
        in_specs=[main(3), main(4), prev(3), prev(4), nxt(3), nxt(4),
                  pl.BlockSpec((CONV_WIDTH, HALF_MIX), lambda i, t: (0, 0)),
                  vec, vec, vec],
        out_specs=pl.BlockSpec((1, tt, HALF_MIX), lambda i, t: (i, t, 0)),
        out_shape=jax.ShapeDtypeStruct((bsz, seq, HALF_MIX), BF16),
        scratch_shapes=[pltpu.VMEM((tt + 2 * CONV_PAD, HALF_MIX), F32),
                        pltpu.VMEM((SUBLANES, tt + 2 * CONV_PAD - SUBLANES, HALF_MIX), F32)],
        compiler_params=_params("arbitrary", "arbitrary"),
        name="conformer_conv",
    )(proj3, proj3, proj3, proj3, proj3, proj3, w, b.reshape(1, HALF_MIX), ln_g.reshape(1, HALF_MIX),
      ln_b.reshape(1, HALF_MIX))


FFN_CHUNK = 1024


def _mix_ffn_kernel(x_ref, y_ref, u_ref, d_ref, wglu_ref, bglu_ref, m2_ref, w1_ref, w2_ref, g1_ref,
                    g_ref, sh_ref, sc_ref, g2_ref, wg_ref, wu_ref, wd_ref, o_ref):
    z = jax.nn.gelu(u_ref[...] * d_ref[...] + y_ref[...])
    s5_out = z * jax.nn.sigmoid(_dot(z.astype(BF16), wglu_ref[...]) + bglu_ref[...])
    mix = _dot(s5_out.astype(BF16), w1_ref[...]) + _dot(m2_ref[...], w2_ref[...])
    x1 = x_ref[...] + g1_ref[0] * mix
    h = _norm_mod(x1, g_ref[...], sh_ref[0], sc_ref[0]).astype(BF16)
    fdim = wg_ref.shape[1]
    acc = None
    for lo in range(0, fdim, FFN_CHUNK):
        hi = min(lo + FFN_CHUNK, fdim)
        a = _dot(h, wg_ref[:, lo:hi])
        up = _dot(h, wu_ref[:, lo:hi])
        part = _dot((_silu(a) * up).astype(BF16), wd_ref[lo:hi, :])
        acc = part if acc is None else acc + part
    o_ref[...] = x1 + g2_ref[0] * acc


def mix_ffn(x, y, u, d_skip, w_glu, b_glu, m2, w_out, g, mod, tiles_per_row, w_gate, w_up, w_down, tm=512):
    t = x.shape[0]
    fdim = w_gate.shape[1]
    half = pl.BlockSpec((tm, HALF_MIX), lambda i: (i, 0))
    vec = pl.BlockSpec((1, HALF_MIX), lambda i: (0, 0))

    def resident(shape, index):
        return pl.BlockSpec(shape, index, pipeline_mode=pl.Buffered(1))

    return pl.pallas_call(
        _mix_ffn_kernel,
        grid=(t // tm,),
        in_specs=[pl.BlockSpec((tm, D_MODEL), lambda i: (i, 0)), half, half, vec,
                  resident((HALF_MIX, HALF_MIX), lambda i: (0, 0)), vec, half,
                  resident((HALF_MIX, D_MODEL), lambda i: (0, 0)),
                  resident((HALF_MIX, D_MODEL), lambda i: (1, 0)),
                  _mod_spec(2, tiles_per_row),
                  pl.BlockSpec((1, D_MODEL), lambda i: (0, 0)),
                  _mod_spec(3, tiles_per_row), _mod_spec(4, tiles_per_row), _mod_spec(5, tiles_per_row),
                  resident((D_MODEL, fdim), lambda i: (0, 0)),
                  resident((D_MODEL, fdim), lambda i: (0, 0)),
                  resident((fdim, D_MODEL), lambda i: (0, 0))],
        out_specs=pl.BlockSpec((tm, D_MODEL), lambda i: (i, 0)),
        out_shape=jax.ShapeDtypeStruct((t, D_MODEL), F32),
        compiler_params=_params("arbitrary"),
        name="mix_ffn",
    )(x, y, u, d_skip.reshape(1, HALF_MIX), w_glu, b_glu.reshape(1, HALF_MIX), m2, w_out, w_out, mod,
      g.reshape(1, D_MODEL), mod, mod, mod, w_gate, w_up, w_down)


MOE_ROW_TILE = 1024
MOE_ZERO_ROWS = 256
MOE_DISPATCH_SLOTS = 3
MOE_ROW_PARTS = 8
INFO_E0, INFO_E1, INFO_G0, INFO_G1, INFO_R0, INFO_R1 = range(6)


def _stream_mod_spec(chunk, prompt_tiles, tiles_per_req, ctx_row):
    def index(i, *_):
        return (jnp.where(i < prompt_tiles, ctx_row, (i - prompt_tiles) // tiles_per_req), 0, chunk)
    return pl.BlockSpec((1, 1, D_MODEL), index)


def _mix_route_kernel(xp_ref, m1p_ref, m2p_ref, xs_ref, m1s_ref, m2s_ref, w1_ref, w2_ref, gate_ref,
                      g_ref, sh_ref, sc_ref, rwh_ref, rwl_ref, o_ref, info_ref, fields_ref, cnt_ref,
                      tri_scr, run_scr, *, prompt_tiles):
    i = pl.program_id(0)
    tm = o_ref.shape[0]
    w1 = w1_ref[...]
    w2 = w2_ref[...]

    @pl.when(i < prompt_tiles)
    def _():
        o_ref[...] = xp_ref[...] + gate_ref[0] * (_dot(m1p_ref[...], w1) + _dot(m2p_ref[...], w2))

    @pl.when(i >= prompt_tiles)
    def _():
        o_ref[...] = xs_ref[...] + gate_ref[0] * (_dot(m1s_ref[...], w1) + _dot(m2s_ref[...], w2))

    @pl.when(i == 0)
    def _():
        r = lax.broadcasted_iota(jnp.int32, (tm, tm), 0)
        c = lax.broadcasted_iota(jnp.int32, (tm, tm), 1)
        tri_scr[...] = jnp.where(c < r, 1.0, 0.0).astype(BF16)
        run_scr[...] = jnp.zeros_like(run_scr)

    h = _norm_mod(o_ref[...], g_ref[...], sh_ref[0], sc_ref[0])
    h_hi = h.astype(BF16)
    h_lo = (h - h_hi.astype(F32)).astype(BF16)
    logits = _dot(h_hi, rwh_ref[...]) + (_dot(h_hi, rwl_ref[...]) + _dot(h_lo, rwh_ref[...]))
    lane = lax.broadcasted_iota(jnp.int32, logits.shape, 1).astype(F32)
    logits = jnp.where(lane < N_EXPERTS, logits, -jnp.inf)
    m1 = jnp.max(logits, axis=-1, keepdims=True)
    i1 = jnp.min(jnp.where(logits == m1, lane, float(LANES)), axis=-1, keepdims=True)
    rest = jnp.where(lane == i1, -jnp.inf, logits)
    m2 = jnp.max(rest, axis=-1, keepdims=True)
    i2 = jnp.min(jnp.where(rest == m2, lane, float(LANES)), axis=-1, keepdims=True)
    e2 = jnp.exp(m2 - m1)
    den = 1.0 + e2
    hit = jnp.where(lane == i1, 1.0, 0.0) + jnp.where(lane == i2, 1.0, 0.0)
    before = _dot(tri_scr[...], hit.astype(BF16)) + run_scr[0:1, :]
    r1 = jnp.sum(jnp.where(lane == i1, before, 0.0), axis=-1, keepdims=True)
    r2 = jnp.sum(jnp.where(lane == i2, before, 0.0), axis=-1, keepdims=True)
    info = jnp.zeros_like(logits)
    for slot, val in ((INFO_E0, i1), (INFO_E1, i2), (INFO_G0, 1.0 / den), (INFO_G1, e2 / den),
                      (INFO_R0, r1), (INFO_R1, r2)):
        info = jnp.where(lane == float(slot), val, info)
    info_ref[...] = info
    fields_ref[...] = jnp.transpose(info)[0:SUBLANES, :]
    run_scr[...] = run_scr[...] + jnp.sum(hit, axis=0, keepdims=True)
    cnt_ref[...] = run_scr[...]


def mix_route(xp, m1p, m2p, xs, m1s, m2s, w_out, g, mod_all, ctx_row, req_tokens, router_w, tm=512):
    tp, ts = xp.shape[0], xs.shape[0]
    t = tp + ts
    pt = tp // tm
    rw = jnp.pad(router_w.astype(F32), ((0, 0), (0, LANES - N_EXPERTS)))
    rw_hi = rw.astype(BF16)
    rw_lo = (rw - rw_hi.astype(F32)).astype(BF16)

    def p_spec(width):
        return pl.BlockSpec((tm, width), lambda i: (jnp.minimum(i, pt - 1), 0))

    def s_spec(width):
        return pl.BlockSpec((tm, width), lambda i: (jnp.maximum(i - pt, 0), 0))

    def mod_spec(chunk):
        return _stream_mod_spec(chunk, pt, req_tokens // tm, ctx_row)

    rw_spec = pl.BlockSpec((D_MODEL, LANES), lambda i: (0, 0))
    return pl.pallas_call(
        functools.partial(_mix_route_kernel, prompt_tiles=pt),
        grid=(t // tm,),
        in_specs=[p_spec(D_MODEL), p_spec(HALF_MIX), p_spec(HALF_MIX),
                  s_spec(D_MODEL), s_spec(HALF_MIX), s_spec(HALF_MIX),
                  pl.BlockSpec((HALF_MIX, D_MODEL), lambda i: (0, 0)),
                  pl.BlockSpec((HALF_MIX, D_MODEL), lambda i: (1, 0)),
                  mod_spec(2), pl.BlockSpec((1, D_MODEL), lambda i: (0, 0)), mod_spec(3), mod_spec(4),
                  rw_spec, rw_spec],
        out_specs=[pl.BlockSpec((tm, D_MODEL), lambda i: (i, 0)),
                   pl.BlockSpec((tm, LANES), lambda i: (i, 0)),
                   pl.BlockSpec((SUBLANES, tm), lambda i: (0, i)),
                   pl.BlockSpec((SUBLANES, LANES), lambda i: (0, 0))],
        out_shape=[jax.ShapeDtypeStruct((t, D_MODEL), F32), jax.ShapeDtypeStruct((t, LANES), F32),
                   jax.ShapeDtypeStruct((SUBLANES, t), F32), jax.ShapeDtypeStruct((SUBLANES, LANES), F32)],
        scratch_shapes=[pltpu.VMEM((tm, tm), BF16), pltpu.VMEM((SUBLANES, LANES), F32)],
        compiler_params=_params("arbitrary"),
        name="mix_route",
    )(xp, m1p, m2p, xs, m1s, m2s, w_out, w_out, mod_all, g.reshape(1, D_MODEL), mod_all, mod_all,
      rw_hi, rw_lo)


def _dispatch_kernel(d0_ref, d1_ref, fill_ref, x_ref, g_ref, sh_ref, sc_ref, xs_hbm, h_scr, zero_scr, sem, zsem):
    i = pl.program_id(0)
    n = pl.num_programs(0) - 1
    tm = x_ref.shape[0]
    slot = i % MOE_DISPATCH_SLOTS
    prev = (i + MOE_DISPATCH_SLOTS - 1) % MOE_DISPATCH_SLOTS

    def row_copy(r, dst, s):
        return pltpu.make_async_copy(h_scr.at[s, pl.ds(r, 1), :], xs_hbm.at[pl.ds(dst, 1), :], sem.at[s])

    def wait_rows(s):
        for _ in range(2):
            pltpu.make_async_copy(h_scr.at[s], xs_hbm.at[pl.ds(0, tm), :], sem.at[s]).wait()

    def issue_prev():
        base = (i - 1) * tm
        for r in range(tm):
            row_copy(r, d0_ref[base + r], prev).start()
            row_copy(r, d1_ref[base + r], prev).start()

    def normalise():
        h_scr[slot] = _norm_mod(x_ref[...], g_ref[...], sh_ref[0], sc_ref[0])

    @pl.when(i >= MOE_DISPATCH_SLOTS)
    def _():
        wait_rows(slot)

    @pl.when(i == 0)
    def _():
        normalise()

    @pl.when(i > 0)
    def _():
        issue_prev()
        normalise()

    @pl.when(i == n)
    def _():
        zero_scr[...] = jnp.zeros_like(zero_scr)

        def zero_row(r):
            return pltpu.make_async_copy(zero_scr.at[pl.ds(0, 1), :], xs_hbm.at[pl.ds(r, 1), :], zsem)

        def zero_block(b):
            start = pl.multiple_of(b * MOE_ZERO_ROWS, MOE_ZERO_ROWS)
            return pltpu.make_async_copy(zero_scr, xs_hbm.at[pl.ds(start, MOE_ZERO_ROWS), :], zsem)

        def start_all(copy):
            def body(r, c):
                copy(r).start()
                return c
            return body

        def wait_all(copy):
            def body(r, c):
                copy(r).wait()
                return c
            return body

        def zero_group(b):
            start = pl.multiple_of(b * SUBLANES, SUBLANES)
            return pltpu.make_async_copy(zero_scr.at[pl.ds(0, SUBLANES), :],
                                         xs_hbm.at[pl.ds(start, SUBLANES), :], zsem)

        for e in range(N_EXPERTS):
            lo = fill_ref[e]
            hi = fill_ref[N_EXPERTS + e]
            lo_group = (lo + SUBLANES - 1) // SUBLANES
            lax.fori_loop(lo, lo_group * SUBLANES, start_all(zero_row), 0)
            lax.fori_loop(lo_group, hi // SUBLANES, start_all(zero_group), 0)
            lax.fori_loop(lo, lo_group * SUBLANES, wait_all(zero_row), 0)
            lax.fori_loop(lo_group, hi // SUBLANES, wait_all(zero_group), 0)
        blocks_per_tile = MOE_ROW_TILE // MOE_ZERO_ROWS
        first = fill_ref[2 * N_EXPERTS] * blocks_per_tile
        last = (xs_hbm.shape[0] // MOE_ROW_TILE) * blocks_per_tile
        lax.fori_loop(first, last, start_all(zero_block), 0)
        lax.fori_loop(first, last, wait_all(zero_block), 0)
        wait_rows(prev)
        wait_rows((i + 1) % MOE_DISPATCH_SLOTS)


def moe_dispatch(x, g, mod_all, ctx_row, prompt_tokens, req_tokens, dest0, dest1, fill, n_tiles, tm=512):
    t = x.shape[0]
    pt = prompt_tokens // tm
    last = t // tm - 1

    def mod_spec(chunk):
        inner = _stream_mod_spec(chunk, pt, req_tokens // tm, ctx_row)
        return pl.BlockSpec((1, 1, D_MODEL), lambda i, *refs: inner.index_map(jnp.minimum(i, last), *refs))

    grid_spec = pltpu.PrefetchScalarGridSpec(
        num_scalar_prefetch=3,
        grid=(t // tm + 1,),
        in_specs=[pl.BlockSpec((tm, D_MODEL), lambda i, *_: (jnp.minimum(i, last), 0)),
                  pl.BlockSpec((1, D_MODEL), lambda i, *_: (0, 0)),
                  mod_spec(3), mod_spec(4)],
        out_specs=pl.BlockSpec(memory_space=pl.ANY),
        scratch_shapes=[pltpu.VMEM((MOE_DISPATCH_SLOTS, tm, D_MODEL), F32),
                        pltpu.VMEM((MOE_ZERO_ROWS, D_MODEL), F32),
                        pltpu.SemaphoreType.DMA((MOE_DISPATCH_SLOTS,)), pltpu.SemaphoreType.DMA(())],
    )
    return pl.pallas_call(
        _dispatch_kernel,
        grid_spec=grid_spec,
        out_shape=jax.ShapeDtypeStruct((n_tiles * MOE_ROW_TILE, D_MODEL), F32),
        compiler_params=pltpu.CompilerParams(dimension_semantics=("arbitrary",),
                                             vmem_limit_bytes=VMEM_LIMIT_BYTES,
                                             disable_bounds_checks=True),
        name="moe_dispatch",
    )(dest0, dest1, fill, x, g.reshape(1, D_MODEL), mod_all, mod_all)


def _experts_kernel(te_ref, tv_ref, x_ref, wg_ref, wu_ref, wd_ref, o_ref):
    i = pl.program_id(0)
    f = pl.program_id(1)
    last_f = pl.num_programs(1) - 1

    n_valid = tv_ref[i]
    tile_rows = x_ref.shape[0]
    step = tile_rows // MOE_ROW_PARTS

    def swiglu_rows(n_rows):
        rows = slice(0, n_rows)

        h = x_ref[rows, :].astype(BF16)
        a = _dot(h, wg_ref[0].astype(BF16))
        u = _dot(h, wu_ref[0].astype(BF16))
        part = _dot((_silu(a) * u).astype(BF16), wd_ref[0].astype(BF16))

        @pl.when(f == 0)
        def _():
            o_ref[rows, :] = part

        @pl.when(f > 0)
        def _():
            o_ref[rows, :] += part

    for part in range(1, MOE_ROW_PARTS + 1):
        n_rows = part * step

        @pl.when((n_valid > n_rows - step) & (n_valid <= n_rows))
        def _(n_rows=n_rows):
            swiglu_rows(n_rows)
            if n_rows < tile_rows:
                @pl.when(f == last_f)
                def _():
                    o_ref[n_rows:, :] = jnp.zeros((tile_rows - n_rows, D_MODEL), F32)

    @pl.when((n_valid == 0) & (f == last_f))
    def _():
        o_ref[...] = jnp.zeros_like(o_ref)


def moe_experts(xs_sorted, tile_expert, tile_valid, w_gate, w_up, w_down, tf=512):
    rows = xs_sorted.shape[0]
    fdim = w_gate.shape[2]
    n_f = fdim // tf
    tr = MOE_ROW_TILE

    def f_eff(i, f, tv):
        return jnp.where(tv[i] > 0, f, n_f - 1)

    grid_spec = pltpu.PrefetchScalarGridSpec(
        num_scalar_prefetch=2,
        grid=(rows // tr, n_f),
        in_specs=[pl.BlockSpec((tr, D_MODEL), lambda i, f, te, tv: (i, 0)),
                  pl.BlockSpec((1, D_MODEL, tf), lambda i, f, te, tv: (te[i], 0, f_eff(i, f, tv))),
                  pl.BlockSpec((1, D_MODEL, tf), lambda i, f, te, tv: (te[i], 0, f_eff(i, f, tv))),
                  pl.BlockSpec((1, tf, D_MODEL), lambda i, f, te, tv: (te[i], f_eff(i, f, tv), 0))],
        out_specs=pl.BlockSpec((tr, D_MODEL), lambda i, f, te, tv: (i, 0)),
    )
    return pl.pallas_call(
        _experts_kernel,
        grid_spec=grid_spec,
        out_shape=jax.ShapeDtypeStruct((rows, D_MODEL), F32),
        compiler_params=_params("arbitrary", "arbitrary"),
        name="moe_experts",
    )(tile_expert, tile_valid, xs_sorted, w_gate, w_up, w_down)


def _combine_kernel(d0_ref, d1_ref, x_ref, info_ref, gate_ref, fg_ref, ys_hbm, op_ref, os_ref, rbuf, sem,
                    *, prompt_tiles):
    i = pl.program_id(0)
    n = pl.num_programs(0)
    tm = x_ref.shape[0]
    slot = i % 2

    def issue(tile, s, unroll):
        base = tile * tm

        def body(r8, c):
            rb = pl.multiple_of(r8 * SUBLANES, SUBLANES)
            for k in range(SUBLANES):
                pltpu.make_async_copy(ys_hbm.at[pl.ds(d0_ref[base + rb + k], 1), :],
                                      rbuf.at[s, 0, pl.ds(rb + k, 1), :], sem.at[s]).start()
                pltpu.make_async_copy(ys_hbm.at[pl.ds(d1_ref[base + rb + k], 1), :],
                                      rbuf.at[s, 1, pl.ds(rb + k, 1), :], sem.at[s]).start()
            return c

        lax.fori_loop(0, tm // SUBLANES, body, 0, unroll=unroll)

    def wait_rows(s):
        for k in range(2):
            pltpu.make_async_copy(ys_hbm.at[pl.ds(0, tm), :], rbuf.at[s, k], sem.at[s]).wait()

    @pl.when(i == 0)
    def _():
        issue(0, 0, False)

    wait_rows(slot)
    issue(jnp.where(i + 1 < n, i + 1, 0), 1 - slot, True)

    info = info_ref[...]
    moe = info[:, INFO_G0:INFO_G0 + 1] * rbuf[slot, 0] + info[:, INFO_G1:INFO_G1 + 1] * rbuf[slot, 1]
    y = x_ref[...] + gate_ref[0] * moe
    out = y * lax.rsqrt(jnp.mean(y * y, axis=-1, keepdims=True) + EPS) * fg_ref[...]

    @pl.when(i < prompt_tiles)
    def _():
        op_ref[...] = out

    @pl.when(i >= prompt_tiles)
    def _():
        os_ref[...] = out

    @pl.when(i == n - 1)
    def _():
        wait_rows(1 - slot)


def moe_combine(x, info, ys_sorted, dest0, dest1, mod_all, ctx_row, prompt_tokens, req_tokens, final_g, tm=512):
    t = x.shape[0]
    pt = prompt_tokens // tm
    grid_spec = pltpu.PrefetchScalarGridSpec(
        num_scalar_prefetch=2,
        grid=(t // tm,),
        in_specs=[pl.BlockSpec((tm, D_MODEL), lambda i, *_: (i, 0)),
                  pl.BlockSpec((tm, LANES), lambda i, *_: (i, 0)),
                  _stream_mod_spec(5, pt, req_tokens // tm, ctx_row),
                  pl.BlockSpec((1, D_MODEL), lambda i, *_: (0, 0)),
                  pl.BlockSpec(memory_space=pl.ANY)],
        out_specs=[pl.BlockSpec((tm, D_MODEL), lambda i, *_: (jnp.minimum(i, pt - 1), 0)),
                   pl.BlockSpec((tm, D_MODEL), lambda i, *_: (jnp.maximum(i - pt, 0), 0))],
        scratch_shapes=[pltpu.VMEM((2, 2, tm, D_MODEL), F32), pltpu.SemaphoreType.DMA((2,))],
    )
    return pl.pallas_call(
        functools.partial(_combine_kernel, prompt_tiles=pt),
        grid_spec=grid_spec,
        out_shape=[jax.ShapeDtypeStruct((prompt_tokens, D_MODEL), F32),
                   jax.ShapeDtypeStruct((t - prompt_tokens, D_MODEL), F32)],
        compiler_params=pltpu.CompilerParams(dimension_semantics=("arbitrary",),
                                             vmem_limit_bytes=VMEM_LIMIT_BYTES,
                                             disable_bounds_checks=True),
        name="moe_combine",
    )(dest0, dest1, x, info, mod_all, final_g.reshape(1, D_MODEL), ys_sorted)


def moe_layout(counts, fields, n_tiles):
    nt_e = (counts + MOE_ROW_TILE - 1) // MOE_ROW_TILE
    ends = jnp.cumsum(nt_e)
    total = ends[-1]
    offset = (ends - nt_e) * MOE_ROW_TILE
    experts = jnp.arange(N_EXPERTS, dtype=jnp.int32)

    def dest(e_lane, r_lane):
        e = fields[e_lane].astype(jnp.int32)
        off = jnp.sum(jnp.where(e[:, None] == experts[None, :], offset[None, :], 0), axis=1)
        return (off + fields[r_lane].astype(jnp.int32)).astype(jnp.int32)

    fill = jnp.concatenate([offset + counts, ends * MOE_ROW_TILE, total[None]]).astype(jnp.int32)
    ids = jnp.arange(n_tiles, dtype=jnp.int32)
    ids_c = jnp.minimum(ids, total - 1)
    te = jnp.sum((ids_c[:, None] >= ends[None, :]).astype(jnp.int32), axis=1)
    mine = te[:, None] == experts[None, :]
    first_tile = jnp.sum(jnp.where(mine, (ends - nt_e)[None, :], 0), axis=1)
    count = jnp.sum(jnp.where(mine, counts[None, :], 0), axis=1)
    rows_left = jnp.clip(count - (ids - first_tile) * MOE_ROW_TILE, 0, MOE_ROW_TILE)
    tile_rows = jnp.where(ids < total, rows_left, 0)
    return (dest(INFO_E0, INFO_R0), dest(INFO_E1, INFO_R1), fill, te.astype(jnp.int32),
            tile_rows.astype(jnp.int32))


def moe_final(x, info, fields, cnt, g, mod_all, ctx_row, prompt_tokens, req_tokens, final_g, w_gate, w_up,
              w_down):
    t = x.shape[0]
    counts = cnt[0, :N_EXPERTS].astype(jnp.int32)
    n_tiles = (2 * t) // MOE_ROW_TILE + N_EXPERTS
    dest0, dest1, fill, te, tv = moe_layout(counts, fields, n_tiles)
    xs_sorted = moe_dispatch(x, g, mod_all, ctx_row, prompt_tokens, req_tokens, dest0, dest1, fill, n_tiles)
    ys_sorted = moe_experts(xs_sorted, te, tv, w_gate, w_up, w_down)
    return moe_combine(x, info, ys_sorted, dest0, dest1, mod_all, ctx_row, prompt_tokens, req_tokens, final_g)


def kernel(x_prompt, x_sample, state_s5, cache_na_k, cache_na_v, cache_diff_k, cache_diff_v, c, c_ctx, w_mod, b_mod, norm_mix_g, norm_ffn_g, final_norm_g, w_in_e, w_out_e, s5_lam_re, s5_lam_im, s5_log_dt, s5_b_re, s5_b_im, s5_c_re, s5_c_im, s5_d, s5_w_glu, s5_b_glu, na_rpb, ffn_w_gate, ffn_w_up, ffn_w_down, w_in_o, w_out_o, diff_lam_q1, diff_lam_k1, diff_lam_q2, diff_lam_k2, diff_subln_g, conv_w, conv_b, conv_ln_g, conv_ln_b, router_w, moe_w_gate, moe_w_up, moe_w_down):
    bp, lp, d = x_prompt.shape
    bs, ls, _ = x_sample.shape
    tm = 1024
    xp = x_prompt.reshape(bp * lp, d)
    xs = x_sample.reshape(bs * ls, d)
    rows_p = (bp * lp) // tm
    rows_s = ls // tm

    cond8 = jnp.concatenate([c, c_ctx[None, :], jnp.zeros((SUBLANES - bs - 1, d), F32)], axis=0)
    mod = adaln_all(cond8, w_mod, b_mod)
    mod_s = mod[:, 0:bs]
    mod_p = mod[:, bs:bs + 1]

    def tiles(rows, tile):
        return rows * tm // tile

    bmat, cmat, lam8 = s5_params(s5_lam_re[0], s5_lam_im[0], s5_log_dt[0], s5_b_re[0], s5_b_im[0],
                                 s5_c_re[0], s5_c_im[0])
    bias = na_bias_blocks(na_rpb[0])
    n_e = w_in_e.shape[-1]

    tiles_p = tiles(rows_p, 512)
    tiles_s = tiles(rows_s, 512)
    proj_p = in_proj(xp, norm_mix_g[0], mod_p[0], tiles_p, w_in_e[0])
    u_s, qkv_s = in_proj(xs, norm_mix_g[0], mod_s[0], tiles_s, w_in_e[0], n_f32=HALF_MIX)

    y_p, st_p = s5_scan(proj_p.reshape(bp, lp, n_e), bmat, cmat, lam8, None, 1)
    chunks = SUBLANES // bs
    h0 = state_s5[:, 0].reshape(bs, 2, 2, S5_GROUPS * S5_STATE)
    y_s, _ = s5_scan(u_s.reshape(bs * chunks, ls // chunks, HALF_MIX), bmat, cmat, lam8, h0, chunks)
    nao_p, na_k, na_v = na_ctx(proj_p.reshape(bp, lp, n_e))

    def heads_to_lanes(cache):
        return cache.transpose(0, 2, 1, 3).reshape(bs, cache.shape[2], HALF_MIX).astype(BF16)

    nao_s = na_lat(qkv_s.reshape(bs, ls, n_e - HALF_MIX), heads_to_lanes(cache_na_k[:, 0]),
                   heads_to_lanes(cache_na_v[:, 0]), bias)

    glu_w = (s5_d[0], s5_w_glu[0].astype(BF16), s5_b_glu[0])
    ffn_w = (w_out_e[0].astype(BF16), norm_ffn_g[0])
    ffn_w3 = (ffn_w_gate[0].astype(BF16), ffn_w_up[0].astype(BF16), ffn_w_down[0].astype(BF16))
    xp = mix_ffn(xp, y_p.reshape(bp * lp, HALF_MIX), proj_p, *glu_w, nao_p.reshape(bp * lp, HALF_MIX),
                 *ffn_w, mod_p[0], tiles_p, *ffn_w3)
    xs = mix_ffn(xs, y_s.reshape(bs * ls, HALF_MIX), u_s, *glu_w, nao_s.reshape(bs * ls, HALF_MIX),
                 *ffn_w, mod_s[0], tiles_s, *ffn_w3)

    lam_init = 0.8 - 0.6 * math.exp(-0.3 * 1)
    lam = (jnp.exp(jnp.sum(diff_lam_q1[0].astype(F32) * diff_lam_k1[0].astype(F32)))
           - jnp.exp(jnp.sum(diff_lam_q2[0].astype(F32) * diff_lam_k2[0].astype(F32)))
           + lam_init)
    cos, sin = rope_tables(ls)
    n_o = w_in_o.shape[-1]

    proj_p = in_proj(xp, norm_mix_g[1], mod_p[1], tiles_p, w_in_o[0])
    proj_s = in_proj(xs, norm_mix_g[1], mod_s[1], tiles_s, w_in_o[0])

    do_p, diff_k, diff_v = diff_ctx(proj_p.reshape(bp, lp, n_o), lam, diff_subln_g[0], lam_init)
    do_s = diff_lat(proj_s.reshape(bs, ls, n_o), cache_diff_k[:, 0:1], cache_diff_v[:, 0:1], cos, sin,
                    lam, diff_subln_g[0], lam_init)
    co_p = conformer_conv(proj_p.reshape(bp, lp, n_o), conv_w[0], conv_b[0], conv_ln_g[0], conv_ln_b[0], lp)
    co_s = conformer_conv(proj_s.reshape(bs, ls, n_o), conv_w[0], conv_b[0], conv_ln_g[0], conv_ln_b[0], 512)

    mod_all = mod[1]
    x_all, info, fields, cnt = mix_route(
        xp, do_p.reshape(bp * lp, HALF_MIX), co_p.reshape(bp * lp, HALF_MIX),
        xs, do_s.reshape(bs * ls, HALF_MIX), co_s.reshape(bs * ls, HALF_MIX),
        w_out_o[0].astype(BF16), norm_ffn_g[1], mod_all, bs, ls, router_w[0])
    yp, ys = moe_final(x_all, info, fields, cnt, norm_ffn_g[1], mod_all, bs, bp * lp, ls, final_norm_g,
                       moe_w_gate[0], moe_w_up[0], moe_w_down[0])

    new_state = st_p.reshape(bp, 1, 2, 2, S5_GROUPS, S5_STATE)
    return (yp.reshape(bp, lp, d), ys.reshape(bs, ls, d), new_state, na_k, na_v, diff_k, diff_v)
```

```python
import functools
import math

import jax
import jax.numpy as jnp
import numpy as np
from jax import lax
from jax.experimental import pallas as pl
from jax.experimental.pallas import tpu as pltpu

D_MODEL = 1024
DEPTH = 2
GRID_W = 64
HALF_MIX = 512
S5_GROUP_CH = 16
S5_GROUPS = 32
S5_STATE = 64
NA_HEAD_DIM = 64
NA_HEADS = 8
NA_WIN_R = 8
NA_WIN_C = 16
DIFF_D = 64
DIFF_HEAD_DIM = 128
DIFF_HEADS = 4
ROPE_BASE = 10000.0
CONV_WIDTH = 31
N_EXPERTS = 8
EPS = 1e-6

F32 = jnp.float32
BF16 = jnp.bfloat16
NEG_BIG = -1e30

VMEM_LIMIT_BYTES = 56 * 1024 * 1024
LANES = 128
SUBLANES = 8

S5_COL_GROUPS = 8
S5_COL_CH = S5_COL_GROUPS * S5_GROUP_CH
S5_COL_STATE = S5_COL_GROUPS * S5_STATE
S5_N_COL = S5_GROUPS // S5_COL_GROUPS
S5_TIME_BLOCK = 128


def _params(*sem):
    return pltpu.CompilerParams(dimension_semantics=sem, vmem_limit_bytes=VMEM_LIMIT_BYTES)


def _dot(a, b):
    return jnp.dot(a, b, preferred_element_type=F32)


def _dot_nt(a, b):
    return lax.dot_general(a, b, (((1,), (1,)), ((), ())), preferred_element_type=F32)


def _silu(x):
    return x * jax.nn.sigmoid(x)


def _norm_mod(x, g, shift, scale):
    y = x * lax.rsqrt(jnp.mean(x * x, axis=-1, keepdims=True) + EPS) * g
    return y * (1.0 + scale) + shift


def _mod_kernel(cond_ref, w_ref, b_ref, o_ref):
    s = _silu(cond_ref[...])
    o_ref[0, :, 0, :] = jnp.dot(s, w_ref[0], precision=lax.Precision.HIGHEST,
                                preferred_element_type=F32) + b_ref[0]


def adaln_all(cond8, w_mod, b_mod):
    tn = 1536
    n = w_mod.shape[-1]
    return pl.pallas_call(
        _mod_kernel,
        grid=(DEPTH, n // tn),
        in_specs=[pl.BlockSpec((SUBLANES, D_MODEL), lambda l, j: (0, 0)),
                  pl.BlockSpec((1, D_MODEL, tn), lambda l, j: (l, 0, j)),
                  pl.BlockSpec((1, 1, tn), lambda l, j: (l, 0, j))],
        out_specs=pl.BlockSpec((1, SUBLANES, 1, tn), lambda l, j: (l, 0, 0, j)),
        out_shape=jax.ShapeDtypeStruct((DEPTH, SUBLANES, 1, n), F32),
        compiler_params=_params("arbitrary", "arbitrary"),
        name="adaln_mod",
    )(cond8, w_mod, b_mod.reshape(DEPTH, 1, n))


def _mod_spec(chunk, tiles_per_row):
    return pl.BlockSpec((1, 1, D_MODEL), lambda i, *_: (i // tiles_per_row, 0, chunk))


def _in_proj_kernel(x_ref, g_ref, sh_ref, sc_ref, w_ref, *refs, n_f32):
    *outs, wb_scr = refs

    @pl.when(pl.program_id(0) == 0)
    def _():
        wb_scr[...] = w_ref[...].astype(BF16)

    h = _norm_mod(x_ref[...], g_ref[...], sh_ref[0], sc_ref[0]).astype(BF16)
    y = _dot(h, wb_scr[...])
    if len(outs) == 1:
        outs[0][...] = y
    else:
        outs[0][...] = y[:, :n_f32]
        outs[1][...] = y[:, n_f32:].astype(BF16)


def in_proj(x, g, mod, tiles_per_row, w, n_f32=None, tm=512):
    t = x.shape[0]
    n = w.shape[1]
    if n_f32 is None:
        out_specs = pl.BlockSpec((tm, n), lambda i: (i, 0))
        out_shape = jax.ShapeDtypeStruct((t, n), F32)
    else:
        out_specs = [pl.BlockSpec((tm, n_f32), lambda i: (i, 0)), pl.BlockSpec((tm, n - n_f32), lambda i: (i, 0))]
        out_shape = [jax.ShapeDtypeStruct((t, n_f32), F32), jax.ShapeDtypeStruct((t, n - n_f32), BF16)]
    return pl.pallas_call(
        functools.partial(_in_proj_kernel, n_f32=n_f32),
        grid=(t // tm,),
        in_specs=[pl.BlockSpec((tm, D_MODEL), lambda i: (i, 0)),
                  pl.BlockSpec((1, D_MODEL), lambda i: (0, 0)),
                  _mod_spec(0, tiles_per_row),
                  _mod_spec(1, tiles_per_row),
                  pl.BlockSpec((D_MODEL, n), lambda i: (0, 0), pipeline_mode=pl.Buffered(1))],
        out_specs=out_specs,
        out_shape=out_shape,
        scratch_shapes=[pltpu.VMEM((D_MODEL, n), BF16)],
        compiler_params=_params("arbitrary"),
        name="in_proj",
    )(x, g.reshape(1, D_MODEL), mod, mod, w)


def _s5_scan_kernel(*refs, seq, chunks, has_init):
    if has_init:
        u_ref, bm_ref, cm_ref, lam_ref, h0_ref, y_ref, st_ref, bu_scr, hb_scr, ytm_scr = refs
    else:
        u_ref, bm_ref, cm_ref, lam_ref, y_ref, st_ref, bu_scr, hb_scr, ytm_scr = refs
        h0_ref = None
    tb = S5_TIME_BLOCK
    n_tb = seq // tb
    ns = S5_COL_STATE
    row = lax.broadcasted_iota(jnp.int32, (SUBLANES, ns), 0)
    piece = row % chunks

    for d in range(2):
        lam = lam_ref[d, 0]
        lr, li = lam[:, :ns], lam[:, ns:]
        bm = bm_ref[d, 0]
        cm = cm_ref[d, 0]
        blocks = list(range(n_tb)) if d == 0 else list(range(n_tb - 1, -1, -1))

        def project(k, buf):
            ub = u_ref[:, k * tb:(k + 1) * tb, :]
            utm = jnp.swapaxes(ub, 0, 1).reshape(tb * SUBLANES, S5_COL_CH).astype(BF16)
            bu_scr[buf] = _dot(utm, bm).reshape(tb, SUBLANES, 2 * ns)

        def scan_block(h, buf, store):
            def advance(t, hr, hi):
                b = bu_scr[buf, t]
                return lr * hr - li * hi + b[:, :ns], lr * hi + li * hr + b[:, ns:]

            def step(s, carry):
                t_a = (tb - 1 - 2 * s) if d == 1 else 2 * s
                t_b = t_a - 1 if d == 1 else t_a + 1
                ar, ai = advance(t_a, *carry)
                br, bi = advance(t_b, ar, ai)
                if store:
                    first = jnp.concatenate([br, bi] if d == 1 else [ar, ai], axis=1)
                    second = jnp.concatenate([ar, ai] if d == 1 else [br, bi], axis=1)
                    t_lo = t_b if d == 1 else t_a
                    hb_scr[buf, t_lo * SUBLANES:(t_lo + 2) * SUBLANES, :] = (
                        jnp.concatenate([first, second], axis=0).astype(BF16))
                return br, bi

            if not store:
                return lax.fori_loop(0, tb // 2, step, h)
            for s in range(tb // 2):
                h = step(s, h)
            return h

        def run_pass(h, store):
            project(blocks[0], 0)
            for j, k in enumerate(blocks):
                buf = j % 2
                if j + 1 < n_tb:
                    project(blocks[j + 1], 1 - buf)
                h = scan_block(h, buf, store)
                if store:
                    yb = _dot(hb_scr[buf], cm).reshape(tb, SUBLANES, S5_COL_CH)
                    if d == 0:
                        ytm_scr[k * tb:(k + 1) * tb] = yb
                    else:
                        ytm_scr[k * tb:(k + 1) * tb] += yb
            return h

        zero = jnp.zeros((SUBLANES, ns), F32)
        if chunks > 1:
            fr, fi = run_pass((zero, zero), False)
            pr, pi = lr, li
            for _ in range(int(math.log2(seq))):
                pr, pi = pr * pr - pi * pi, 2.0 * pr * pi
            edge = 0 if d == 0 else chunks - 1
            shift = 1 if d == 0 else SUBLANES - 1
            if has_init:
                h0r = h0_ref[:, d, 0, :]
                h0i = h0_ref[:, d, 1, :]
                seq_of_row = row // chunks
                er, ei = zero, zero
                for b in range(SUBLANES // chunks):
                    er = jnp.where(seq_of_row == b, h0r[b:b + 1, :], er)
                    ei = jnp.where(seq_of_row == b, h0i[b:b + 1, :], ei)
            else:
                er, ei = zero, zero
            is_edge = piece == edge
            cr = jnp.where(is_edge, er, zero)
            ci = jnp.where(is_edge, ei, zero)
            for _ in range(chunks - 1):
                tr = fr + pr * cr - pi * ci
                ti = fi + pr * ci + pi * cr
                cr = jnp.where(is_edge, er, pltpu.roll(tr, shift, 0))
                ci = jnp.where(is_edge, ei, pltpu.roll(ti, shift, 0))
            h = (cr, ci)
        else:
            if has_init:
                h = (h0_ref[:, d, 0, :], h0_ref[:, d, 1, :])
            else:
                h = (zero, zero)

        h = run_pass(h, True)
        st_ref[:, d, 0, :] = h[0]
        st_ref[:, d, 1, :] = h[1]

    y_ref[...] = jnp.swapaxes(ytm_scr[...], 0, 1)


def s5_scan(proj3, bmat, cmat, lam8, h0, chunks):
    rows, seq, _ = proj3.shape
    ns = S5_COL_STATE
    has_init = h0 is not None
    in_specs = [pl.BlockSpec((SUBLANES, seq, S5_COL_CH), lambda i, c: (i, 0, c)),
                pl.BlockSpec((2, 1, S5_COL_CH, 2 * ns), lambda i, c: (0, c, 0, 0)),
                pl.BlockSpec((2, 1, 2 * ns, S5_COL_CH), lambda i, c: (0, c, 0, 0)),
                pl.BlockSpec((2, 1, SUBLANES, 2 * ns), lambda i, c: (0, c, 0, 0))]
    args = [proj3, bmat, cmat, lam8]
    if has_init:
        nb = h0.shape[0]
        in_specs.append(pl.BlockSpec((nb, 2, 2, ns), lambda i, c: (0, 0, 0, c)))
        args.append(h0)
    y, st = pl.pallas_call(
        functools.partial(_s5_scan_kernel, seq=seq, chunks=chunks, has_init=has_init),
        grid=(rows // SUBLANES, S5_N_COL),
        in_specs=in_specs,
        out_specs=[pl.BlockSpec((SUBLANES, seq, S5_COL_CH), lambda i, c: (i, 0, c)),
                   pl.BlockSpec((SUBLANES, 2, 2, ns), lambda i, c: (i, 0, 0, c))],
        out_shape=[jax.ShapeDtypeStruct((rows, seq, HALF_MIX), F32),
                   jax.ShapeDtypeStruct((rows, 2, 2, S5_GROUPS * S5_STATE), F32)],
        scratch_shapes=[pltpu.VMEM((2, S5_TIME_BLOCK, SUBLANES, 2 * ns), F32),
                        pltpu.VMEM((2, S5_TIME_BLOCK * SUBLANES, 2 * ns), BF16),
                        pltpu.VMEM((seq, SUBLANES, S5_COL_CH), F32)],
        compiler_params=_params("arbitrary", "arbitrary"),
        name="s5_scan",
    )(*args)
    return y, st


def s5_params(lam_re, lam_im, log_dt, b_re, b_im, c_re, c_im):
    lr = lam_re.astype(F32)
    li = lam_im.astype(F32)
    dt = jnp.exp(log_dt.astype(F32))[..., None]
    mag = jnp.exp(lr * dt)
    bar_re = mag * jnp.cos(li * dt)
    bar_im = mag * jnp.sin(li * dt)
    den = lr * lr + li * li
    q_re = ((bar_re - 1.0) * lr + bar_im * li) / den
    q_im = (bar_im * lr - (bar_re - 1.0) * li) / den
    br = b_re.astype(F32)
    bi = b_im.astype(F32)
    b_bar_re = q_re[..., None] * br - q_im[..., None] * bi
    b_bar_im = q_re[..., None] * bi + q_im[..., None] * br
    eye = jnp.eye(S5_COL_GROUPS, dtype=F32)

    def block_diag_b(m):
        m = m.reshape(2, S5_N_COL, S5_COL_GROUPS, S5_STATE, S5_GROUP_CH)
        bd = jnp.einsum('dngpc,gh->dngchp', m, eye)
        return bd.reshape(2, S5_N_COL, S5_COL_CH, S5_COL_STATE)

    def block_diag_c(m):
        m = m.reshape(2, S5_N_COL, S5_COL_GROUPS, S5_GROUP_CH, S5_STATE)
        bd = jnp.einsum('dngcp,gh->dngphc', m, eye)
        return bd.reshape(2, S5_N_COL, S5_COL_STATE, S5_COL_CH)

    bmat = jnp.concatenate([block_diag_b(b_bar_re), block_diag_b(b_bar_im)], axis=-1).astype(BF16)
    cmat = jnp.concatenate([block_diag_c(c_re.astype(F32)), block_diag_c(-c_im.astype(F32))],
                           axis=-2).astype(BF16)
    lam_cat = jnp.concatenate([bar_re.reshape(2, S5_N_COL, S5_COL_STATE),
                               bar_im.reshape(2, S5_N_COL, S5_COL_STATE)], axis=-1)
    lam8 = jnp.broadcast_to(lam_cat[:, :, None, :], (2, S5_N_COL, SUBLANES, 2 * S5_COL_STATE))
    return bmat, cmat, lam8


def _na_ctx_kernel(q_ref, k_ref, v_ref, o_ref, ko_ref, vo_ref):
    seq = q_ref.shape[1]
    lane = lax.broadcasted_iota(jnp.int32, (seq, LANES), 1)
    low = lane < NA_HEAD_DIM
    outs = []
    for pr in range(NA_HEADS // 2):
        cols = slice(pr * LANES, (pr + 1) * LANES)
        qp = q_ref[0, :, cols] * (NA_HEAD_DIM ** -0.5)
        kp = k_ref[0, :, cols]
        vp = v_ref[0, :, cols]
        for half in range(2):
            sl = slice(half * NA_HEAD_DIM, (half + 1) * NA_HEAD_DIM)
            ko_ref[0, 0, 2 * pr + half] = kp[:, sl]
            vo_ref[0, 0, 2 * pr + half] = vp[:, sl]
        kb = kp.astype(BF16)
        vb = vp.astype(BF16)
        o_pair = None
        for half in range(2):
            qm = jnp.where(low if half == 0 else jnp.logical_not(low), qp, 0.0).astype(BF16)
            s = _dot_nt(qm, kb)
            p = jnp.exp(s - jnp.max(s, axis=-1, keepdims=True))
            o = _dot(p.astype(BF16), vb) * (1.0 / jnp.sum(p, axis=-1, keepdims=True))
            o_pair = o if half == 0 else jnp.where(low, o_pair, o)
        outs.append(o_pair)
    o_ref[0] = jnp.concatenate(outs, axis=-1).astype(o_ref.dtype)


def na_ctx(proj3):
    b, seq, _ = proj3.shape
    cache_shape = jax.ShapeDtypeStruct((b, 1, NA_HEADS, seq, NA_HEAD_DIM), F32)
    cache_spec = pl.BlockSpec((1, 1, NA_HEADS, seq, NA_HEAD_DIM), lambda i: (i, 0, 0, 0, 0))
    return pl.pallas_call(
        _na_ctx_kernel,
        grid=(b,),
        in_specs=[pl.BlockSpec((1, seq, HALF_MIX), lambda i: (i, 0, 1)),
                  pl.BlockSpec((1, seq, HALF_MIX), lambda i: (i, 0, 2)),
                  pl.BlockSpec((1, seq, HALF_MIX), lambda i: (i, 0, 3))],
        out_specs=[pl.BlockSpec((1, seq, HALF_MIX), lambda i: (i, 0, 0)), cache_spec, cache_spec],
        out_shape=[jax.ShapeDtypeStruct((b, seq, HALF_MIX), BF16), cache_shape, cache_shape],
        compiler_params=_params("arbitrary"),
        name="na_ctx",
    )(proj3, proj3, proj3)


NA_Q_ROWS = 4
NA_KEY_ROWS = 12


def na_bias_blocks(rpb):
    qcol = np.arange(GRID_W)
    cc = np.arange(GRID_W)
    cs = np.clip(qcol - NA_WIN_C // 2, 0, GRID_W - NA_WIN_C)
    valid = (cc[None, :] >= cs[:, None]) & (cc[None, :] < cs[:, None] + NA_WIN_C)
    coff = cc[None, :] - qcol[:, None] + (NA_WIN_C - 1)
    n_col = 2 * NA_WIN_C - 1
    sel = ((coff[None] == np.arange(n_col)[:, None, None]) & valid[None]).astype(np.float32)
    sel = sel.reshape(n_col, GRID_W * GRID_W)
    mask = np.where(valid, 0.0, NEG_BIG).astype(np.float32).reshape(1, GRID_W * GRID_W)
    n_row = 2 * NA_WIN_R - 1
    t1 = jnp.dot(rpb.astype(F32).reshape(NA_HEADS * n_row, n_col), jnp.asarray(sel),
                 precision=lax.Precision.HIGHEST) + jnp.asarray(mask)
    t1 = t1.reshape(NA_HEADS, n_row, GRID_W, GRID_W)
    return pl.pallas_call(
        _na_bias_kernel,
        grid=(NA_HEADS,),
        in_specs=[pl.BlockSpec((1, n_row, GRID_W, GRID_W), lambda h: (h, 0, 0, 0))],
        out_specs=pl.BlockSpec((3, 1, NA_Q_ROWS * GRID_W, NA_KEY_ROWS * GRID_W), lambda h: (0, h, 0, 0)),
        out_shape=jax.ShapeDtypeStruct((3, NA_HEADS, NA_Q_ROWS * GRID_W, NA_KEY_ROWS * GRID_W), F32),
        compiler_params=_params("arbitrary"),
        name="na_bias",
    )(t1)


def _na_bias_kernel(t1_ref, o_ref):
    variants = ((lambda ri: 0, NA_WIN_R - 1), (lambda ri: ri, NA_WIN_R // 2 - 1),
                (lambda ri: NA_KEY_ROWS - NA_WIN_R, -1))
    outside = jnp.full((GRID_W, GRID_W), NEG_BIG, F32)
    for v, (lo_of, shift) in enumerate(variants):
        for ri in range(NA_Q_ROWS):
            for wr in range(NA_KEY_ROWS):
                inside = lo_of(ri) <= wr < lo_of(ri) + NA_WIN_R
                o_ref[v, 0, ri * GRID_W:(ri + 1) * GRID_W, wr * GRID_W:(wr + 1) * GRID_W] = (
                    t1_ref[0, wr - ri + shift] if inside else outside)


def _na_lat_kernel(q_ref, k_ref, v_ref, kc_ref, vc_ref, bias_ref, o_ref):
    qb = pl.program_id(1)
    rows = k_ref.shape[1] // GRID_W
    nk = NA_KEY_ROWS * GRID_W
    first_row = jnp.clip(qb * NA_Q_ROWS - NA_WIN_R // 2, 0, rows - NA_KEY_ROWS)
    start = pl.multiple_of(first_row * GRID_W, GRID_W)
    tq = q_ref.shape[1]
    lane = lax.broadcasted_iota(jnp.int32, (tq, LANES), 1)
    low = lane < NA_HEAD_DIM
    outs = []
    for pr in range(NA_HEADS // 2):
        cols = slice(pr * LANES, (pr + 1) * LANES)
        qp = q_ref[0, :, cols].astype(F32) * (NA_HEAD_DIM ** -0.5)
        kw = k_ref[0, pl.ds(start, nk), cols]
        vw = v_ref[0, pl.ds(start, nk), cols]
        kc = kc_ref[0, :, cols]
        vc = vc_ref[0, :, cols]
        o_pair = None
        for half in range(2):
            qm = jnp.where(low if half == 0 else jnp.logical_not(low), qp, 0.0).astype(BF16)
            s_loc = _dot_nt(qm, kw) + bias_ref[0, 2 * pr + half]
            s_ctx = _dot_nt(qm, kc)
            m = jnp.maximum(jnp.max(s_loc, axis=-1, keepdims=True), jnp.max(s_ctx, axis=-1, keepdims=True))
            p_loc = jnp.exp(s_loc - m)
            p_ctx = jnp.exp(s_ctx - m)
            inv = 1.0 / (jnp.sum(p_loc, axis=-1, keepdims=True) + jnp.sum(p_ctx, axis=-1, keepdims=True))
            o = (_dot(p_loc.astype(BF16), vw) + _dot(p_ctx.astype(BF16), vc)) * inv
            o_pair = o if half == 0 else jnp.where(low, o_pair, o)
        outs.append(o_pair)
    o_ref[0] = jnp.concatenate(outs, axis=-1).astype(o_ref.dtype)


def na_lat(qkv3, k_ctx, v_ctx, bias):
    b, seq, _ = qkv3.shape
    tq = NA_Q_ROWS * GRID_W
    n_q = seq // tq
    lc = k_ctx.shape[1]
    ctx_spec = pl.BlockSpec((1, lc, HALF_MIX), lambda i, r: (i, 0, 0))
    return pl.pallas_call(
        _na_lat_kernel,
        grid=(b, n_q),
        in_specs=[pl.BlockSpec((1, tq, HALF_MIX), lambda i, r: (i, r, 0)),
                  pl.BlockSpec((1, seq, HALF_MIX), lambda i, r: (i, 0, 1)),
                  pl.BlockSpec((1, seq, HALF_MIX), lambda i, r: (i, 0, 2)),
                  ctx_spec, ctx_spec,
                  pl.BlockSpec((1, NA_HEADS, tq, NA_KEY_ROWS * GRID_W),
                               lambda i, r: (jnp.where(r == 0, 0, jnp.where(r == n_q - 1, 2, 1)), 0, 0, 0))],
        out_specs=pl.BlockSpec((1, tq, HALF_MIX), lambda i, r: (i, r, 0)),
        out_shape=jax.ShapeDtypeStruct((b, seq, HALF_MIX), BF16),
        compiler_params=_params("arbitrary", "arbitrary"),
        name="na_lat",
    )(qkv3, qkv3, qkv3, k_ctx, v_ctx, bias)


def _softmax_pair_diff(s1, s2, lam):
    p1 = jnp.exp(s1 - jnp.max(s1, axis=-1, keepdims=True))
    p2 = jnp.exp(s2 - jnp.max(s2, axis=-1, keepdims=True))
    inv1 = 1.0 / jnp.sum(p1, axis=-1, keepdims=True)
    inv2 = lam / jnp.sum(p2, axis=-1, keepdims=True)
    return p1 * inv1 - p2 * inv2


def _sub_ln(o, g, lam_init):
    return o * lax.rsqrt(jnp.mean(o * o, axis=-1, keepdims=True) + EPS) * g * (1.0 - lam_init)


def _diff_ctx_kernel(lam_ref, q_ref, k_ref, v_ref, g_ref, o_ref, ko_ref, vo_ref, *, lam_init):
    scale = DIFF_D ** -0.5
    lam = lam_ref[0, 0]
    lane = lax.broadcasted_iota(jnp.int32, (q_ref.shape[1], DIFF_HEAD_DIM), 1)
    first = lane < DIFF_D
    for h in range(DIFF_HEADS):
        sl = slice(h * DIFF_HEAD_DIM, (h + 1) * DIFF_HEAD_DIM)
        qh = q_ref[0, :, sl]
        kh = k_ref[0, :, sl]
        vh = v_ref[0, :, sl]
        ko_ref[0, 0, h] = kh
        vo_ref[0, 0, h] = vh
        kb = kh.astype(BF16)
        s1 = _dot_nt(jnp.where(first, qh, 0.0).astype(BF16), kb) * scale
        s2 = _dot_nt(jnp.where(first, 0.0, qh).astype(BF16), kb) * scale
        a = _softmax_pair_diff(s1, s2, lam)
        o = _dot(a.astype(BF16), vh.astype(BF16))
        o_ref[0, :, sl] = _sub_ln(o, g_ref[...], lam_init).astype(o_ref.dtype)


def diff_ctx(proj3, lam, subln_g, lam_init):
    b, seq, _ = proj3.shape
    cache_shape = jax.ShapeDtypeStruct((b, 1, DIFF_HEADS, seq, DIFF_HEAD_DIM), F32)
    cache_spec = pl.BlockSpec((1, 1, DIFF_HEADS, seq, DIFF_HEAD_DIM), lambda i: (i, 0, 0, 0, 0))
    return pl.pallas_call(
        functools.partial(_diff_ctx_kernel, lam_init=lam_init),
        grid=(b,),
        in_specs=[pl.BlockSpec(memory_space=pltpu.SMEM),
                  pl.BlockSpec((1, seq, HALF_MIX), lambda i: (i, 0, 0)),
                  pl.BlockSpec((1, seq, HALF_MIX), lambda i: (i, 0, 1)),
                  pl.BlockSpec((1, seq, HALF_MIX), lambda i: (i, 0, 2)),
                  pl.BlockSpec((1, DIFF_HEAD_DIM), lambda i: (0, 0))],
        out_specs=[pl.BlockSpec((1, seq, HALF_MIX), lambda i: (i, 0, 0)), cache_spec, cache_spec],
        out_shape=[jax.ShapeDtypeStruct((b, seq, HALF_MIX), BF16), cache_shape, cache_shape],
        compiler_params=_params("arbitrary"),
        name="diff_ctx",
    )(lam.reshape(1, 1), proj3, proj3, proj3, subln_g.reshape(1, DIFF_HEAD_DIM))


def rope_tables(seq):
    t = np.arange(seq)
    row = (t // GRID_W).astype(np.float32)
    col = (t % GRID_W).astype(np.float32)
    n_freq = DIFF_D // 4
    inv = np.float32(ROPE_BASE) ** (-np.arange(n_freq, dtype=np.float32) / np.float32(n_freq))
    ang = np.concatenate([row[:, None] * inv, col[:, None] * inv], axis=-1)
    cos = np.repeat(np.cos(ang), 2, axis=-1)
    sin = np.repeat(np.sin(ang), 2, axis=-1)
    sign = np.where(np.arange(DIFF_D) % 2 == 0, -1.0, 1.0).astype(np.float32)
    sin = sin * sign
    return (jnp.asarray(np.tile(cos, (1, 2)).astype(np.float32)),
            jnp.asarray(np.tile(sin, (1, 2)).astype(np.float32)))


def _rope(x, cos, sin_signed):
    lane = lax.broadcasted_iota(jnp.int32, x.shape, 1)
    nxt = pltpu.roll(x, x.shape[1] - 1, 1)
    prv = pltpu.roll(x, 1, 1)
    partner = jnp.where(lane % 2 == 0, nxt, prv)
    return x * cos + partner * sin_signed


def _diff_lat_kernel(lam_ref, q_ref, k_ref, v_ref, kc_ref, vc_ref, cq_ref, sq_ref, ck_ref, sk_ref,
                     g_ref, o_ref, k_all, v_all, *, lam_init):
    seq = k_ref.shape[1]

    @pl.when(pl.program_id(2) == 0)
    def _():
        k_all[0:seq, :] = _rope(k_ref[0], ck_ref[...], sk_ref[...]).astype(BF16)
        k_all[seq:, :] = kc_ref[0, 0, 0].astype(BF16)
        v_all[0:seq, :] = v_ref[0].astype(BF16)
        v_all[seq:, :] = vc_ref[0, 0, 0].astype(BF16)

    lam = lam_ref[0, 0]
    q = _rope(q_ref[0], cq_ref[...], sq_ref[...]) * (DIFF_D ** -0.5)
    lane = lax.broadcasted_iota(jnp.int32, q.shape, 1)
    first = lane < DIFF_D
    kb = k_all[...]
    s1 = _dot_nt(jnp.where(first, q, 0.0).astype(BF16), kb)
    s2 = _dot_nt(jnp.where(first, 0.0, q).astype(BF16), kb)
    a = _softmax_pair_diff(s1, s2, lam)
    o = _dot(a.astype(BF16), v_all[...])
    o_ref[0] = _sub_ln(o, g_ref[...], lam_init).astype(o_ref.dtype)


def diff_lat(proj3, k_ctx, v_ctx, cos, sin, lam, subln_g, lam_init, tq=256):
    b, seq, _ = proj3.shape
    lc = k_ctx.shape[3]
    hd = DIFF_HEAD_DIM
    ctx_spec = pl.BlockSpec((1, 1, 1, lc, hd), lambda i, h, q: (i, 0, h, 0, 0))
    tq_spec = pl.BlockSpec((tq, hd), lambda i, h, q: (q, 0))
    full_spec = pl.BlockSpec((seq, hd), lambda i, h, q: (0, 0))
    return pl.pallas_call(
        functools.partial(_diff_lat_kernel, lam_init=lam_init),
        grid=(b, DIFF_HEADS, seq // tq),
        in_specs=[pl.BlockSpec(memory_space=pltpu.SMEM),
                  pl.BlockSpec((1, tq, hd), lambda i, h, q: (i, q, h)),
                  pl.BlockSpec((1, seq, hd), lambda i, h, q: (i, 0, DIFF_HEADS + h)),
                  pl.BlockSpec((1, seq, hd), lambda i, h, q: (i, 0, 2 * DIFF_HEADS + h)),
                  ctx_spec, ctx_spec, tq_spec, tq_spec, full_spec, full_spec,
                  pl.BlockSpec((1, hd), lambda i, h, q: (0, 0))],
        out_specs=pl.BlockSpec((1, tq, hd), lambda i, h, q: (i, q, h)),
        out_shape=jax.ShapeDtypeStruct((b, seq, HALF_MIX), BF16),
        scratch_shapes=[pltpu.VMEM((seq + lc, hd), BF16), pltpu.VMEM((seq + lc, hd), BF16)],
        compiler_params=_params("arbitrary", "arbitrary", "arbitrary"),
        name="diff_lat",
    )(lam.reshape(1, 1), proj3, proj3, proj3, k_ctx, v_ctx, cos, sin, cos, sin,
      subln_g.reshape(1, hd))


CONV_PAD = 16
CONV_SUB = 64


def _conv_kernel(a_ref, g_ref, ap_ref, gp_ref, an_ref, gn_ref, w_ref, b_ref, lg_ref, lb_ref, o_ref, xp_scr,
                 xsh_scr):
    t = pl.program_id(1)
    tt = a_ref.shape[1]
    prev = ap_ref[0] * jax.nn.sigmoid(gp_ref[0])
    nxt = an_ref[0] * jax.nn.sigmoid(gn_ref[0])
    xp_scr[0:CONV_PAD, :] = jnp.where(t > 0, prev, 0.0)
    xp_scr[CONV_PAD + tt:, :] = jnp.where(t < pl.num_programs(1) - 1, nxt, 0.0)
    xp_scr[CONV_PAD:CONV_PAD + tt, :] = a_ref[0] * jax.nn.sigmoid(g_ref[0])
    first_tap = CONV_PAD - CONV_WIDTH // 2
    n_rows = xsh_scr.shape[1]
    for b in range(SUBLANES):
        xsh_scr[b] = xp_scr[b:b + n_rows, :]
    for i in range(tt // CONV_SUB):
        s = i * CONV_SUB
        acc = jnp.zeros((CONV_SUB, HALF_MIX), F32)
        for j in range(CONV_WIDTH):
            whole, phase = divmod(first_tap + j, SUBLANES)
            lo = s + whole * SUBLANES
            acc = acc + xsh_scr[phase, lo:lo + CONV_SUB, :] * w_ref[j:j + 1, :]
        y = acc + b_ref[...]
        mu = jnp.mean(y, axis=-1, keepdims=True)
        yc = y - mu
        var = jnp.mean(yc * yc, axis=-1, keepdims=True)
        yn = yc * lax.rsqrt(var + EPS) * lg_ref[...] + lb_ref[...]
        o_ref[0, s:s + CONV_SUB, :] = _silu(yn).astype(o_ref.dtype)


def conformer_conv(proj3, w, b, ln_g, ln_b, tt):
    bsz, seq, _ = proj3.shape
    n_t = seq // tt
    hb = tt // CONV_PAD
    last = seq // CONV_PAD - 1
    vec = pl.BlockSpec((1, HALF_MIX), lambda i, t: (0, 0))

    def main(col):
        return pl.BlockSpec((1, tt, HALF_MIX), lambda i, t: (i, t, col))

    def prev(col):
        return pl.BlockSpec((1, CONV_PAD, HALF_MIX), lambda i, t: (i, jnp.maximum(t * hb - 1, 0), col))

    def nxt(col):
        return pl.BlockSpec((1, CONV_PAD, HALF_MIX), lambda i, t: (i, jnp.minimum((t + 1) * hb, last), col))

    return pl.pallas_call(
        _conv_kernel,
        grid=(bsz, n_t),
        in_specs=[main(3), main(4), prev(3), prev(4), nxt(3), nxt(4),
                  pl.BlockSpec((CONV_WIDTH, HALF_MIX), lambda i, t: (0, 0)),
                  vec, vec, vec],
        out_specs=pl.BlockSpec((1, tt, HALF_MIX), lambda i, t: (i, t, 0)),
        out_shape=jax.ShapeDtypeStruct((bsz, seq, HALF_MIX), BF16),
        scratch_shapes=[pltpu.VMEM((tt + 2 * CONV_PAD, HALF_MIX), F32),
                        pltpu.VMEM((SUBLANES, tt + 2 * CONV_PAD - SUBLANES, HALF_MIX), F32)],
        compiler_params=_params("arbitrary", "arbitrary"),
        name="conformer_conv",
    )(proj3, proj3, proj3, proj3, proj3, proj3, w, b.reshape(1, HALF_MIX), ln_g.reshape(1, HALF_MIX),
      ln_b.reshape(1, HALF_MIX))


FFN_CHUNK = 1024


def _mix_ffn_kernel(x_ref, y_ref, u_ref, d_ref, wglu_ref, bglu_ref, m2_ref, w1_ref, w2_ref, g1_ref,
                    g_ref, sh_ref, sc_ref, g2_ref, wg_ref, wu_ref, wd_ref, o_ref):
    z = jax.nn.gelu(u_ref[...] * d_ref[...] + y_ref[...])
    s5_out = z * jax.nn.sigmoid(_dot(z.astype(BF16), wglu_ref[...]) + bglu_ref[...])
    mix = _dot(s5_out.astype(BF16), w1_ref[...]) + _dot(m2_ref[...], w2_ref[...])
    x1 = x_ref[...] + g1_ref[0] * mix
    h = _norm_mod(x1, g_ref[...], sh_ref[0], sc_ref[0]).astype(BF16)
    fdim = wg_ref.shape[1]
    acc = None
    for lo in range(0, fdim, FFN_CHUNK):
        hi = min(lo + FFN_CHUNK, fdim)
        a = _dot(h, wg_ref[:, lo:hi])
        up = _dot(h, wu_ref[:, lo:hi])
        part = _dot((_silu(a) * up).astype(BF16), wd_ref[lo:hi, :])
        acc = part if acc is None else acc + part
    o_ref[...] = x1 + g2_ref[0] * acc


def mix_ffn(x, y, u, d_skip, w_glu, b_glu, m2, w_out, g, mod, tiles_per_row, w_gate, w_up, w_down, tm=512):
    t = x.shape[0]
    fdim = w_gate.shape[1]
    half = pl.BlockSpec((tm, HALF_MIX), lambda i: (i, 0))
    vec = pl.BlockSpec((1, HALF_MIX), lambda i: (0, 0))

    def resident(shape, index):
        return pl.BlockSpec(shape, index, pipeline_mode=pl.Buffered(1))

    return pl.pallas_call(
        _mix_ffn_kernel,
        grid=(t // tm,),
        in_specs=[pl.BlockSpec((tm, D_MODEL), lambda i: (i, 0)), half, half, vec,
                  resident((HALF_MIX, HALF_MIX), lambda i: (0, 0)), vec, half,
                  resident((HALF_MIX, D_MODEL), lambda i: (0, 0)),
                  resident((HALF_MIX, D_MODEL), lambda i: (1, 0)),
                  _mod_spec(2, tiles_per_row),
                  pl.BlockSpec((1, D_MODEL), lambda i: (0, 0)),
                  _mod_spec(3, tiles_per_row), _mod_spec(4, tiles_per_row), _mod_spec(5, tiles_per_row),
                  resident((D_MODEL, fdim), lambda i: (0, 0)),
                  resident((D_MODEL, fdim), lambda i: (0, 0)),
                  resident((fdim, D_MODEL), lambda i: (0, 0))],
        out_specs=pl.BlockSpec((tm, D_MODEL), lambda i: (i, 0)),
        out_shape=jax.ShapeDtypeStruct((t, D_MODEL), F32),
        compiler_params=_params("arbitrary"),
        name="mix_ffn",
    )(x, y, u, d_skip.reshape(1, HALF_MIX), w_glu, b_glu.reshape(1, HALF_MIX), m2, w_out, w_out, mod,
      g.reshape(1, D_MODEL), mod, mod, mod, w_gate, w_up, w_down)


MOE_ROW_TILE = 1024
MOE_ZERO_ROWS = 256
MOE_DISPATCH_SLOTS = 3
MOE_ROW_PARTS = 8
INFO_E0, INFO_E1, INFO_G0, INFO_G1, INFO_R0, INFO_R1 = range(6)


def _stream_mod_spec(chunk, prompt_tiles, tiles_per_req, ctx_row):
    def index(i, *_):
        return (jnp.where(i < prompt_tiles, ctx_row, (i - prompt_tiles) // tiles_per_req), 0, chunk)
    return pl.BlockSpec((1, 1, D_MODEL), index)


def _mix_route_kernel(xp_ref, m1p_ref, m2p_ref, xs_ref, m1s_ref, m2s_ref, w1_ref, w2_ref, gate_ref,
                      g_ref, sh_ref, sc_ref, rwh_ref, rwl_ref, o_ref, info_ref, fields_ref, cnt_ref,
                      tri_scr, run_scr, *, prompt_tiles):
    i = pl.program_id(0)
    tm = o_ref.shape[0]
    w1 = w1_ref[...]
    w2 = w2_ref[...]

    @pl.when(i < prompt_tiles)
    def _():
        o_ref[...] = xp_ref[...] + gate_ref[0] * (_dot(m1p_ref[...], w1) + _dot(m2p_ref[...], w2))

    @pl.when(i >= prompt_tiles)
    def _():
        o_ref[...] = xs_ref[...] + gate_ref[0] * (_dot(m1s_ref[...], w1) + _dot(m2s_ref[...], w2))

    @pl.when(i == 0)
    def _():
        r = lax.broadcasted_iota(jnp.int32, (tm, tm), 0)
        c = lax.broadcasted_iota(jnp.int32, (tm, tm), 1)
        tri_scr[...] = jnp.where(c < r, 1.0, 0.0).astype(BF16)
        run_scr[...] = jnp.zeros_like(run_scr)

    h = _norm_mod(o_ref[...], g_ref[...], sh_ref[0], sc_ref[0])
    h_hi = h.astype(BF16)
    h_lo = (h - h_hi.astype(F32)).astype(BF16)
    logits = _dot(h_hi, rwh_ref[...]) + (_dot(h_hi, rwl_ref[...]) + _dot(h_lo, rwh_ref[...]))
    lane = lax.broadcasted_iota(jnp.int32, logits.shape, 1).astype(F32)
    logits = jnp.where(lane < N_EXPERTS, logits, -jnp.inf)
    m1 = jnp.max(logits, axis=-1, keepdims=True)
    i1 = jnp.min(jnp.where(logits == m1, lane, float(LANES)), axis=-1, keepdims=True)
    rest = jnp.where(lane == i1, -jnp.inf, logits)
    m2 = jnp.max(rest, axis=-1, keepdims=True)
    i2 = jnp.min(jnp.where(rest == m2, lane, float(LANES)), axis=-1, keepdims=True)
    e2 = jnp.exp(m2 - m1)
    den = 1.0 + e2
    hit = jnp.where(lane == i1, 1.0, 0.0) + jnp.where(lane == i2, 1.0, 0.0)
    before = _dot(tri_scr[...], hit.astype(BF16)) + run_scr[0:1, :]
    r1 = jnp.sum(jnp.where(lane == i1, before, 0.0), axis=-1, keepdims=True)
    r2 = jnp.sum(jnp.where(lane == i2, before, 0.0), axis=-1, keepdims=True)
    info = jnp.zeros_like(logits)
    for slot, val in ((INFO_E0, i1), (INFO_E1, i2), (INFO_G0, 1.0 / den), (INFO_G1, e2 / den),
                      (INFO_R0, r1), (INFO_R1, r2)):
        info = jnp.where(lane == float(slot), val, info)
    info_ref[...] = info
    fields_ref[...] = jnp.transpose(info)[0:SUBLANES, :]
    run_scr[...] = run_scr[...] + jnp.sum(hit, axis=0, keepdims=True)
    cnt_ref[...] = run_scr[...]


def mix_route(xp, m1p, m2p, xs, m1s, m2s, w_out, g, mod_all, ctx_row, req_tokens, router_w, tm=512):
    tp, ts = xp.shape[0], xs.shape[0]
    t = tp + ts
    pt = tp // tm
    rw = jnp.pad(router_w.astype(F32), ((0, 0), (0, LANES - N_EXPERTS)))
    rw_hi = rw.astype(BF16)
    rw_lo = (rw - rw_hi.astype(F32)).astype(BF16)

    def p_spec(width):
        return pl.BlockSpec((tm, width), lambda i: (jnp.minimum(i, pt - 1), 0))

    def s_spec(width):
        return pl.BlockSpec((tm, width), lambda i: (jnp.maximum(i - pt, 0), 0))

    def mod_spec(chunk):
        return _stream_mod_spec(chunk, pt, req_tokens // tm, ctx_row)

    rw_spec = pl.BlockSpec((D_MODEL, LANES), lambda i: (0, 0))
    return pl.pallas_call(
        functools.partial(_mix_route_kernel, prompt_tiles=pt),
        grid=(t // tm,),
        in_specs=[p_spec(D_MODEL), p_spec(HALF_MIX), p_spec(HALF_MIX),
                  s_spec(D_MODEL), s_spec(HALF_MIX), s_spec(HALF_MIX),
                  pl.BlockSpec((HALF_MIX, D_MODEL), lambda i: (0, 0)),
                  pl.BlockSpec((HALF_MIX, D_MODEL), lambda i: (1, 0)),
                  mod_spec(2), pl.BlockSpec((1, D_MODEL), lambda i: (0, 0)), mod_spec(3), mod_spec(4),
                  rw_spec, rw_spec],
        out_specs=[pl.BlockSpec((tm, D_MODEL), lambda i: (i, 0)),
                   pl.BlockSpec((tm, LANES), lambda i: (i, 0)),
                   pl.BlockSpec((SUBLANES, tm), lambda i: (0, i)),
                   pl.BlockSpec((SUBLANES, LANES), lambda i: (0, 0))],
        out_shape=[jax.ShapeDtypeStruct((t, D_MODEL), F32), jax.ShapeDtypeStruct((t, LANES), F32),
                   jax.ShapeDtypeStruct((SUBLANES, t), F32), jax.ShapeDtypeStruct((SUBLANES, LANES), F32)],
        scratch_shapes=[pltpu.VMEM((tm, tm), BF16), pltpu.VMEM((SUBLANES, LANES), F32)],
        compiler_params=_params("arbitrary"),
        name="mix_route",
    )(xp, m1p, m2p, xs, m1s, m2s, w_out, w_out, mod_all, g.reshape(1, D_MODEL), mod_all, mod_all,
      rw_hi, rw_lo)


def _dispatch_kernel(d0_ref, d1_ref, fill_ref, x_ref, g_ref, sh_ref, sc_ref, xs_hbm, h_scr, zero_scr, sem, zsem):
    i = pl.program_id(0)
    n = pl.num_programs(0) - 1
    tm = x_ref.shape[0]
    slot = i % MOE_DISPATCH_SLOTS
    prev = (i + MOE_DISPATCH_SLOTS - 1) % MOE_DISPATCH_SLOTS

    def row_copy(r, dst, s):
        return pltpu.make_async_copy(h_scr.at[s, pl.ds(r, 1), :], xs_hbm.at[pl.ds(dst, 1), :], sem.at[s])

    def wait_rows(s):
        for _ in range(2):
            pltpu.make_async_copy(h_scr.at[s], xs_hbm.at[pl.ds(0, tm), :], sem.at[s]).wait()

    def issue_prev():
        base = (i - 1) * tm
        for r in range(tm):
            row_copy(r, d0_ref[base + r], prev).start()
            row_copy(r, d1_ref[base + r], prev).start()

    def normalise():
        h_scr[slot] = _norm_mod(x_ref[...], g_ref[...], sh_ref[0], sc_ref[0])

    @pl.when(i >= MOE_DISPATCH_SLOTS)
    def _():
        wait_rows(slot)

    @pl.when(i == 0)
    def _():
        normalise()

    @pl.when(i > 0)
    def _():
        issue_prev()
        normalise()

    @pl.when(i == n)
    def _():
        zero_scr[...] = jnp.zeros_like(zero_scr)

        def zero_row(r):
            return pltpu.make_async_copy(zero_scr.at[pl.ds(0, 1), :], xs_hbm.at[pl.ds(r, 1), :], zsem)

        def zero_block(b):
            start = pl.multiple_of(b * MOE_ZERO_ROWS, MOE_ZERO_ROWS)
            return pltpu.make_async_copy(zero_scr, xs_hbm.at[pl.ds(start, MOE_ZERO_ROWS), :], zsem)

        def start_all(copy):
            def body(r, c):
                copy(r).start()
                return c
            return body

        def wait_all(copy):
            def body(r, c):
                copy(r).wait()
                return c
            return body

        def zero_group(b):
            start = pl.multiple_of(b * SUBLANES, SUBLANES)
            return pltpu.make_async_copy(zero_scr.at[pl.ds(0, SUBLANES), :],
                                         xs_hbm.at[pl.ds(start, SUBLANES), :], zsem)

        for e in range(N_EXPERTS):
            lo = fill_ref[e]
            hi = fill_ref[N_EXPERTS + e]
            lo_group = (lo + SUBLANES - 1) // SUBLANES
            lax.fori_loop(lo, lo_group * SUBLANES, start_all(zero_row), 0)
            lax.fori_loop(lo_group, hi // SUBLANES, start_all(zero_group), 0)
            lax.fori_loop(lo, lo_group * SUBLANES, wait_all(zero_row), 0)
            lax.fori_loop(lo_group, hi // SUBLANES, wait_all(zero_group), 0)
        blocks_per_tile = MOE_ROW_TILE // MOE_ZERO_ROWS
        first = fill_ref[2 * N_EXPERTS] * blocks_per_tile
        last = (xs_hbm.shape[0] // MOE_ROW_TILE) * blocks_per_tile
        lax.fori_loop(first, last, start_all(zero_block), 0)
        lax.fori_loop(first, last, wait_all(zero_block), 0)
        wait_rows(prev)
        wait_rows((i + 1) % MOE_DISPATCH_SLOTS)


def moe_dispatch(x, g, mod_all, ctx_row, prompt_tokens, req_tokens, dest0, dest1, fill, n_tiles, tm=512):
    t = x.shape[0]
    pt = prompt_tokens // tm
    last = t // tm - 1

    def mod_spec(chunk):
        inner = _stream_mod_spec(chunk, pt, req_tokens // tm, ctx_row)
        return pl.BlockSpec((1, 1, D_MODEL), lambda i, *refs: inner.index_map(jnp.minimum(i, last), *refs))

    grid_spec = pltpu.PrefetchScalarGridSpec(
        num_scalar_prefetch=3,
        grid=(t // tm + 1,),
        in_specs=[pl.BlockSpec((tm, D_MODEL), lambda i, *_: (jnp.minimum(i, last), 0)),
                  pl.BlockSpec((1, D_MODEL), lambda i, *_: (0, 0)),
                  mod_spec(3), mod_spec(4)],
        out_specs=pl.BlockSpec(memory_space=pl.ANY),
        scratch_shapes=[pltpu.VMEM((MOE_DISPATCH_SLOTS, tm, D_MODEL), F32),
                        pltpu.VMEM((MOE_ZERO_ROWS, D_MODEL), F32),
                        pltpu.SemaphoreType.DMA((MOE_DISPATCH_SLOTS,)), pltpu.SemaphoreType.DMA(())],
    )
    return pl.pallas_call(
        _dispatch_kernel,
        grid_spec=grid_spec,
        out_shape=jax.ShapeDtypeStruct((n_tiles * MOE_ROW_TILE, D_MODEL), F32),
        compiler_params=pltpu.CompilerParams(dimension_semantics=("arbitrary",),
                                             vmem_limit_bytes=VMEM_LIMIT_BYTES,
                                             disable_bounds_checks=True),
        name="moe_dispatch",
    )(dest0, dest1, fill, x, g.reshape(1, D_MODEL), mod_all, mod_all)


def _experts_kernel(te_ref, tv_ref, x_ref, wg_ref, wu_ref, wd_ref, o_ref, h_scr):
    i = pl.program_id(0)
    f = pl.program_id(1)
    last_f = pl.num_programs(1) - 1

    n_valid = tv_ref[i]
    tile_rows = h_scr.shape[0]
    step = tile_rows // MOE_ROW_PARTS

    def swiglu_rows(n_rows):
        rows = slice(0, n_rows)

        @pl.when(f == 0)
        def _():
            h_scr[rows, :] = x_ref[rows, :].astype(BF16)

        h = h_scr[rows, :]
        a = _dot(h, wg_ref[0].astype(BF16))
        u = _dot(h, wu_ref[0].astype(BF16))
        part = _dot((_silu(a) * u).astype(BF16), wd_ref[0].astype(BF16))

        @pl.when(f == 0)
        def _():
            o_ref[rows, :] = part

        @pl.when(f > 0)
        def _():
            o_ref[rows, :] += part

    for part in range(1, MOE_ROW_PARTS + 1):
        n_rows = part * step

        @pl.when((n_valid > n_rows - step) & (n_valid <= n_rows))
        def _(n_rows=n_rows):
            swiglu_rows(n_rows)
            if n_rows < tile_rows:
                @pl.when(f == last_f)
                def _():
                    o_ref[n_rows:, :] = jnp.zeros((tile_rows - n_rows, D_MODEL), F32)

    @pl.when((n_valid == 0) & (f == last_f))
    def _():
        o_ref[...] = jnp.zeros_like(o_ref)


def moe_experts(xs_sorted, tile_expert, tile_valid, w_gate, w_up, w_down, tf=512):
    rows = xs_sorted.shape[0]
    fdim = w_gate.shape[2]
    n_f = fdim // tf
    tr = MOE_ROW_TILE

    def f_eff(i, f, tv):
        return jnp.where(tv[i] > 0, f, n_f - 1)

    grid_spec = pltpu.PrefetchScalarGridSpec(
        num_scalar_prefetch=2,
        grid=(rows // tr, n_f),
        in_specs=[pl.BlockSpec((tr, D_MODEL), lambda i, f, te, tv: (i, 0)),
                  pl.BlockSpec((1, D_MODEL, tf), lambda i, f, te, tv: (te[i], 0, f_eff(i, f, tv))),
                  pl.BlockSpec((1, D_MODEL, tf), lambda i, f, te, tv: (te[i], 0, f_eff(i, f, tv))),
                  pl.BlockSpec((1, tf, D_MODEL), lambda i, f, te, tv: (te[i], f_eff(i, f, tv), 0))],
        out_specs=pl.BlockSpec((tr, D_MODEL), lambda i, f, te, tv: (i, 0)),
        scratch_shapes=[pltpu.VMEM((tr, D_MODEL), BF16)],
    )
    return pl.pallas_call(
        _experts_kernel,
        grid_spec=grid_spec,
        out_shape=jax.ShapeDtypeStruct((rows, D_MODEL), F32),
        compiler_params=_params("arbitrary", "arbitrary"),
        name="moe_experts",
    )(tile_expert, tile_valid, xs_sorted, w_gate, w_up, w_down)


def _combine_kernel(d0_ref, d1_ref, x_ref, info_ref, gate_ref, fg_ref, ys_hbm, op_ref, os_ref, rbuf, sem,
                    *, prompt_tiles):
    i = pl.program_id(0)
    n = pl.num_programs(0)
    tm = x_ref.shape[0]
    slot = i % 2

    def issue(tile, s, unroll):
        base = tile * tm

        def body(r8, c):
            rb = pl.multiple_of(r8 * SUBLANES, SUBLANES)
            for k in range(SUBLANES):
                pltpu.make_async_copy(ys_hbm.at[pl.ds(d0_ref[base + rb + k], 1), :],
                                      rbuf.at[s, 0, pl.ds(rb + k, 1), :], sem.at[s]).start()
                pltpu.make_async_copy(ys_hbm.at[pl.ds(d1_ref[base + rb + k], 1), :],
                                      rbuf.at[s, 1, pl.ds(rb + k, 1), :], sem.at[s]).start()
            return c

        lax.fori_loop(0, tm // SUBLANES, body, 0, unroll=unroll)

    def wait_rows(s):
        for k in range(2):
            pltpu.make_async_copy(ys_hbm.at[pl.ds(0, tm), :], rbuf.at[s, k], sem.at[s]).wait()

    @pl.when(i == 0)
    def _():
        issue(0, 0, False)

    wait_rows(slot)
    issue(jnp.where(i + 1 < n, i + 1, 0), 1 - slot, True)

    info = info_ref[...]
    moe = info[:, INFO_G0:INFO_G0 + 1] * rbuf[slot, 0] + info[:, INFO_G1:INFO_G1 + 1] * rbuf[slot, 1]
    y = x_ref[...] + gate_ref[0] * moe
    out = y * lax.rsqrt(jnp.mean(y * y, axis=-1, keepdims=True) + EPS) * fg_ref[...]

    @pl.when(i < prompt_tiles)
    def _():
        op_ref[...] = out

    @pl.when(i >= prompt_tiles)
    def _():
        os_ref[...] = out

    @pl.when(i == n - 1)
    def _():
        wait_rows(1 - slot)


def moe_combine(x, info, ys_sorted, dest0, dest1, mod_all, ctx_row, prompt_tokens, req_tokens, final_g, tm=512):
    t = x.shape[0]
    pt = prompt_tokens // tm
    grid_spec = pltpu.PrefetchScalarGridSpec(
        num_scalar_prefetch=2,
        grid=(t // tm,),
        in_specs=[pl.BlockSpec((tm, D_MODEL), lambda i, *_: (i, 0)),
                  pl.BlockSpec((tm, LANES), lambda i, *_: (i, 0)),
                  _stream_mod_spec(5, pt, req_tokens // tm, ctx_row),
                  pl.BlockSpec((1, D_MODEL), lambda i, *_: (0, 0)),
                  pl.BlockSpec(memory_space=pl.ANY)],
        out_specs=[pl.BlockSpec((tm, D_MODEL), lambda i, *_: (jnp.minimum(i, pt - 1), 0)),
                   pl.BlockSpec((tm, D_MODEL), lambda i, *_: (jnp.maximum(i - pt, 0), 0))],
        scratch_shapes=[pltpu.VMEM((2, 2, tm, D_MODEL), F32), pltpu.SemaphoreType.DMA((2,))],
    )
    return pl.pallas_call(
        functools.partial(_combine_kernel, prompt_tiles=pt),
        grid_spec=grid_spec,
        out_shape=[jax.ShapeDtypeStruct((prompt_tokens, D_MODEL), F32),
                   jax.ShapeDtypeStruct((t - prompt_tokens, D_MODEL), F32)],
        compiler_params=pltpu.CompilerParams(dimension_semantics=("arbitrary",),
                                             vmem_limit_bytes=VMEM_LIMIT_BYTES,
                                             disable_bounds_checks=True),
        name="moe_combine",
    )(dest0, dest1, x, info, mod_all, final_g.reshape(1, D_MODEL), ys_sorted)


def moe_layout(counts, fields, n_tiles):
    nt_e = (counts + MOE_ROW_TILE - 1) // MOE_ROW_TILE
    ends = jnp.cumsum(nt_e)
    total = ends[-1]
    offset = (ends - nt_e) * MOE_ROW_TILE
    experts = jnp.arange(N_EXPERTS, dtype=jnp.int32)

    def dest(e_lane, r_lane):
        e = fields[e_lane].astype(jnp.int32)
        off = jnp.sum(jnp.where(e[:, None] == experts[None, :], offset[None, :], 0), axis=1)
        return (off + fields[r_lane].astype(jnp.int32)).astype(jnp.int32)

    fill = jnp.concatenate([offset + counts, ends * MOE_ROW_TILE, total[None]]).astype(jnp.int32)
    ids = jnp.arange(n_tiles, dtype=jnp.int32)
    ids_c = jnp.minimum(ids, total - 1)
    te = jnp.sum((ids_c[:, None] >= ends[None, :]).astype(jnp.int32), axis=1)
    mine = te[:, None] == experts[None, :]
    first_tile = jnp.sum(jnp.where(mine, (ends - nt_e)[None, :], 0), axis=1)
    count = jnp.sum(jnp.where(mine, counts[None, :], 0), axis=1)
    rows_left = jnp.clip(count - (ids - first_tile) * MOE_ROW_TILE, 0, MOE_ROW_TILE)
    tile_rows = jnp.where(ids < total, rows_left, 0)
    return (dest(INFO_E0, INFO_R0), dest(INFO_E1, INFO_R1), fill, te.astype(jnp.int32),
            tile_rows.astype(jnp.int32))


def moe_final(x, info, fields, cnt, g, mod_all, ctx_row, prompt_tokens, req_tokens, final_g, w_gate, w_up,
              w_down):
    t = x.shape[0]
    counts = cnt[0, :N_EXPERTS].astype(jnp.int32)
    n_tiles = (2 * t) // MOE_ROW_TILE + N_EXPERTS
    dest0, dest1, fill, te, tv = moe_layout(counts, fields, n_tiles)
    xs_sorted = moe_dispatch(x, g, mod_all, ctx_row, prompt_tokens, req_tokens, dest0, dest1, fill, n_tiles)
    ys_sorted = moe_experts(xs_sorted, te, tv, w_gate, w_up, w_down)
    return moe_combine(x, info, ys_sorted, dest0, dest1, mod_all, ctx_row, prompt_tokens, req_tokens, final_g)


def kernel(x_prompt, x_sample, state_s5, cache_na_k, cache_na_v, cache_diff_k, cache_diff_v, c, c_ctx, w_mod, b_mod, norm_mix_g, norm_ffn_g, final_norm_g, w_in_e, w_out_e, s5_lam_re, s5_lam_im, s5_log_dt, s5_b_re, s5_b_im, s5_c_re, s5_c_im, s5_d, s5_w_glu, s5_b_glu, na_rpb, ffn_w_gate, ffn_w_up, ffn_w_down, w_in_o, w_out_o, diff_lam_q1, diff_lam_k1, diff_lam_q2, diff_lam_k2, diff_subln_g, conv_w, conv_b, conv_ln_g, conv_ln_b, router_w, moe_w_gate, moe_w_up, moe_w_down):
    bp, lp, d = x_prompt.shape
    bs, ls, _ = x_sample.shape
    tm = 1024
    xp = x_prompt.reshape(bp * lp, d)
    xs = x_sample.reshape(bs * ls, d)
    rows_p = (bp * lp) // tm
    rows_s = ls // tm

    cond8 = jnp.concatenate([c, c_ctx[None, :], jnp.zeros((SUBLANES - bs - 1, d), F32)], axis=0)
    mod = adaln_all(cond8, w_mod, b_mod)
    mod_s = mod[:, 0:bs]
    mod_p = mod[:, bs:bs + 1]

    def tiles(rows, tile):
        return rows * tm // tile

    bmat, cmat, lam8 = s5_params(s5_lam_re[0], s5_lam_im[0], s5_log_dt[0], s5_b_re[0], s5_b_im[0],
                                 s5_c_re[0], s5_c_im[0])
    bias = na_bias_blocks(na_rpb[0])
    n_e = w_in_e.shape[-1]

    tiles_p = tiles(rows_p, 512)
    tiles_s = tiles(rows_s, 512)
    proj_p = in_proj(xp, norm_mix_g[0], mod_p[0], tiles_p, w_in_e[0])
    u_s, qkv_s = in_proj(xs, norm_mix_g[0], mod_s[0], tiles_s, w_in_e[0], n_f32=HALF_MIX)

    y_p, st_p = s5_scan(proj_p.reshape(bp, lp, n_e), bmat, cmat, lam8, None, 1)
    chunks = SUBLANES // bs
    h0 = state_s5[:, 0].reshape(bs, 2, 2, S5_GROUPS * S5_STATE)
    y_s, _ = s5_scan(u_s.reshape(bs * chunks, ls // chunks, HALF_MIX), bmat, cmat, lam8, h0, chunks)
    nao_p, na_k, na_v = na_ctx(proj_p.reshape(bp, lp, n_e))

    def heads_to_lanes(cache):
        return cache.transpose(0, 2, 1, 3).reshape(bs, cache.shape[2], HALF_MIX).astype(BF16)

    nao_s = na_lat(qkv_s.reshape(bs, ls, n_e - HALF_MIX), heads_to_lanes(cache_na_k[:, 0]),
                   heads_to_lanes(cache_na_v[:, 0]), bias)

    glu_w = (s5_d[0], s5_w_glu[0].astype(BF16), s5_b_glu[0])
    ffn_w = (w_out_e[0].astype(BF16), norm_ffn_g[0])
    ffn_w3 = (ffn_w_gate[0].astype(BF16), ffn_w_up[0].astype(BF16), ffn_w_down[0].astype(BF16))
    xp = mix_ffn(xp, y_p.reshape(bp * lp, HALF_MIX), proj_p, *glu_w, nao_p.reshape(bp * lp, HALF_MIX),
                 *ffn_w, mod_p[0], tiles_p, *ffn_w3)
    xs = mix_ffn(xs, y_s.reshape(bs * ls, HALF_MIX), u_s, *glu_w, nao_s.reshape(bs * ls, HALF_MIX),
                 *ffn_w, mod_s[0], tiles_s, *ffn_w3)

    lam_init = 0.8 - 0.6 * math.exp(-0.3 * 1)
    lam = (jnp.exp(jnp.sum(diff_lam_q1[0].astype(F32) * diff_lam_k1[0].astype(F32)))
           - jnp.exp(jnp.sum(diff_lam_q2[0].astype(F32) * diff_lam_k2[0].astype(F32)))
           + lam_init)
    cos, sin = rope_tables(ls)
    n_o = w_in_o.shape[-1]

    proj_p = in_proj(xp, norm_mix_g[1], mod_p[1], tiles_p, w_in_o[0])
    proj_s = in_proj(xs, norm_mix_g[1], mod_s[1], tiles_s, w_in_o[0])

    do_p, diff_k, diff_v = diff_ctx(proj_p.reshape(bp, lp, n_o), lam, diff_subln_g[0], lam_init)
    do_s = diff_lat(proj_s.reshape(bs, ls, n_o), cache_diff_k[:, 0:1], cache_diff_v[:, 0:1], cos, sin,
                    lam, diff_subln_g[0], lam_init)
    co_p = conformer_conv(proj_p.reshape(bp, lp, n_o), conv_w[0], conv_b[0], conv_ln_g[0], conv_ln_b[0], lp)
    co_s = conformer_conv(proj_s.reshape(bs, ls, n_o), conv_w[0], conv_b[0], conv_ln_g[0], conv_ln_b[0], 512)

    mod_all = mod[1]
    x_all, info, fields, cnt = mix_route(
        xp, do_p.reshape(bp * lp, HALF_MIX), co_p.reshape(bp * lp, HALF_MIX),
        xs, do_s.reshape(bs * ls, HALF_MIX), co_s.reshape(bs * ls, HALF_MIX),
        w_out_o[0].astype(BF16), norm_ffn_g[1], mod_all, bs, ls, router_w[0])
    yp, ys = moe_final(x_all, info, fields, cnt, norm_ffn_g[1], mod_all, bs, bp * lp, ls, final_norm_g,
                       moe_w_gate[0], moe_w_up[0], moe_w_down[0])

    new_state = st_p.reshape(bp, 1, 2, 2, S5_GROUPS, S5_STATE)
    return (yp.reshape(bp, lp, d), ys.reshape(bs, ls, d), new_state, na_k, na_v, diff_k, diff_v)
```
